```python
import math
import jax
import jax.numpy as jnp
from jax import lax
import numpy as np

D_MODEL = 1024
BATCH = 2
SEQ = 8192
DEPTH = 1
DEC_BATCH = 32
DEC_SEQ = 8
PAST_LEN = 8192
PAGE_SIZE = 128

HEAD_DIM = 64
N_Q_HEADS = 8
N_KV_HEADS = 2
GQA_GROUP = N_Q_HEADS // N_KV_HEADS
ATTN_WIDTH = N_Q_HEADS * HEAD_DIM
KV_WIDTH = N_KV_HEADS * HEAD_DIM
ROPE_DIM = HEAD_DIM // 4
ROPE_THETA = 500000.0
CMP_LEN = 32
CMP_STRIDE = 16
CMP_HIDDEN = 4 * HEAD_DIM
SEL_BLOCK = 64
N_SEL = 16
N_LOCAL = 2
WINDOW = 512
Q_BLOCK = 128

SSM_HEADS = 8
SSM_HEAD_DIM = 64
SSM_WIDTH = SSM_HEADS * SSM_HEAD_DIM
SSM_GROUPS = 2
SSM_STATE = 128
CONV_WIDTH = 4
CONV_DIM = SSM_WIDTH + 2 * SSM_GROUPS * SSM_STATE
SSD_CHUNK = 128
MIX_WIDTH = ATTN_WIDTH + SSM_WIDTH

N_EXPERTS = 64
N_EXPERT_GROUPS = 8
TOPK_GROUPS = 4
TOP_K = 8
D_EXPERT = 256
D_SHARED = 256
ROUTED_SCALE = 2.5
MOE_BLOCK = 128

IN_SIZES = (ATTN_WIDTH, 6 * KV_WIDTH, 3 * N_Q_HEADS, SSM_WIDTH, CONV_DIM, SSM_HEADS)
N_IN = sum(IN_SIZES)
EPS = 1e-6
NEG = -1e30
BIG = 1e6

kernel_name = 'hymba_nsa_ssd_moe_adaln_step'


def rms_norm(x, w):
    xf = x.astype(jnp.float32)
    y = xf * lax.rsqrt(jnp.mean(xf * xf, axis=-1, keepdims=True) + EPS)
    return (y * w.astype(jnp.float32)).astype(x.dtype)


def rope(x, pos):
    half = ROPE_DIM // 2
    inv_freq = ROPE_THETA ** (-jnp.arange(half, dtype=jnp.float32) / half)
    ang = pos.astype(jnp.float32)[:, None] * inv_freq[None, :]
    cos = jnp.cos(ang)[:, None, :]
    sin = jnp.sin(ang)[:, None, :]
    xf = x.astype(jnp.float32)
    x1, x2, rest = xf[..., :half], xf[..., half:ROPE_DIM], xf[..., ROPE_DIM:]
    return jnp.concatenate([x1 * cos - x2 * sin, x2 * cos + x1 * sin, rest], axis=-1).astype(x.dtype)


def adaln(c, w_ada, b_ada):
    m = jax.nn.silu(c) @ w_ada + b_ada
    return jnp.split(m, 6, axis=-1)


def modulate(x, w_norm, shift, scale):
    return rms_norm(x, w_norm) * (1.0 + scale[:, None, :]) + shift[:, None, :]


def project(h, w_in, q_norm_w, k_norm_w, pos):
    b, t, _ = h.shape
    splits = [int(v) for v in np.cumsum(IN_SIZES)[:-1]]
    q, kv, g, z, xbc, dt = jnp.split(h @ w_in, splits, axis=-1)
    q = rope(rms_norm(q.reshape(b, t, N_Q_HEADS, HEAD_DIM), q_norm_w), pos)
    kv = kv.reshape(b, t, 3, 2, N_KV_HEADS, HEAD_DIM)
    k = rms_norm(kv[:, :, :, 0], k_norm_w[:, None, :])
    k = rope(k.reshape(b, t, 3 * N_KV_HEADS, HEAD_DIM), pos).reshape(b, t, 3, N_KV_HEADS, HEAD_DIM)
    kv = jnp.stack([k, kv[:, :, :, 1]], axis=3)
    kv_rows = kv[:, :, :2].reshape(b, t, 4, N_KV_HEADS, HEAD_DIM)
    win_rows = kv[:, :, 2]
    gates = jax.nn.sigmoid(g.astype(jnp.float32)).reshape(b, t, N_Q_HEADS, 3)
    return q, gates, kv_rows, win_rows, z, xbc, dt


def compress(k, pe, w1, w2):
    b, s, kvh, dh = k.shape
    nb = s // CMP_STRIDE
    r = CMP_LEN // CMP_STRIDE
    nc = nb - r + 1
    kb = k.reshape(b, nb, CMP_STRIDE, kvh, dh)
    blocks = jnp.concatenate([kb[:, i:i + nc] for i in range(r)], axis=2)
    blocks = blocks + pe[:, None, :].astype(k.dtype)
    blocks = jnp.moveaxis(blocks, 3, 2).reshape(b, nc, kvh, CMP_LEN * dh)
    return jax.nn.silu(blocks @ w1) @ w2


def cmp_sel_cover(nc, nsb):
    i = np.arange(nc)[:, None]
    j = np.arange(nsb)[None, :]
    start = i * CMP_STRIDE
    m = (start < (j + 1) * SEL_BLOCK) & (start + CMP_LEN > j * SEL_BLOCK)
    return jnp.asarray(m.astype(np.float32))


def nsa_context(kv_full, cmp_pe, cmp_w1, cmp_w2):
    b, s = kv_full.shape[:2]
    nsb = max(-(-s // SEL_BLOCK), N_SEL)
    s_pad = nsb * SEL_BLOCK
    kvp = jnp.pad(kv_full, ((0, 0), (0, s_pad - s), (0, 0), (0, 0), (0, 0)))
    kc = compress(kvp[:, :, 0], cmp_pe[0], cmp_w1[0], cmp_w2[0])
    vc = compress(kvp[:, :, 1], cmp_pe[1], cmp_w1[1], cmp_w2[1])
    nc = kc.shape[1]
    cend = jnp.arange(nc) * CMP_STRIDE + CMP_LEN - 1
    cover = cmp_sel_cover(nc, nsb)
    sel = kvp[:, :, 2:4].reshape(b, nsb, SEL_BLOCK, 2, N_KV_HEADS, HEAD_DIM)
    sel = jnp.transpose(sel, (3, 0, 4, 1, 2, 5))
    return kc, vc, cend, cover, sel[0], sel[1]


def masked_softmax(s, mask):
    return jax.nn.softmax(jnp.where(mask, s.astype(jnp.float32), NEG), axis=-1)


def nsa_attend(q, gates, qpos, ctx, kw, vw, wpos):
    kc, vc, cend, cover, ksb, vsb = ctx
    b, tq = q.shape[:2]
    scale = HEAD_DIM ** -0.5
    qg = q.reshape(b, tq, N_KV_HEADS, GQA_GROUP, HEAD_DIM)
    valid_c = cend[None, :] <= qpos[:, None]
    s = jnp.einsum('btkgd,bckd->bkgtc', qg, kc) * scale
    p_cmp = jnp.where(valid_c, masked_softmax(s, valid_c), 0.0)
    o_cmp = jnp.einsum('bkgtc,bckd->btkgd', p_cmp.astype(vc.dtype), vc)
    imp = jnp.einsum('bkgtc,cs->bkts', p_cmp, cover)
    blk = jnp.arange(cover.shape[1])[None, :]
    cur = (qpos // SEL_BLOCK)[:, None]
    forced = (blk == 0) | ((blk <= cur) & (blk > cur - N_LOCAL))
    imp = jnp.where(forced, BIG, jnp.where(blk > cur, -BIG, imp))
    _, idx = lax.top_k(imp, N_SEL)
    gather = jax.vmap(jax.vmap(lambda blocks, ids: blocks[ids]))
    n_sel_keys = N_SEL * SEL_BLOCK
    ks = gather(ksb, idx).reshape(b, N_KV_HEADS, tq, n_sel_keys, HEAD_DIM)
    vs = gather(vsb, idx).reshape(b, N_KV_HEADS, tq, n_sel_keys, HEAD_DIM)
    kpos = (idx[..., None] * SEL_BLOCK + jnp.arange(SEL_BLOCK)).reshape(b, N_KV_HEADS, tq, n_sel_keys)
    valid_s = (kpos <= qpos[None, None, :, None])[:, :, None]
    s = jnp.einsum('btkgd,bktsd->bkgts', qg, ks) * scale
    p = masked_softmax(s, valid_s)
    o_sel = jnp.einsum('bkgts,bktsd->btkgd', p.astype(vs.dtype), vs)
    valid_w = (wpos[None, :] <= qpos[:, None]) & (wpos[None, :] > qpos[:, None] - WINDOW) & (wpos[None, :] >= 0)
    s = jnp.einsum('btkgd,bskd->bkgts', qg, kw) * scale
    p = masked_softmax(s, valid_w)
    o_win = jnp.einsum('bkgts,bskd->btkgd', p.astype(vw.dtype), vw)
    g = gates.reshape(b, tq, N_KV_HEADS, GQA_GROUP, 3).astype(q.dtype)
    o = g[..., 0:1] * o_cmp + g[..., 1:2] * o_sel + g[..., 2:3] * o_win
    return o.reshape(b, tq, ATTN_WIDTH)


def causal_conv(xbc, buf, w, bias):
    xp = jnp.concatenate([buf, xbc], axis=1)
    y = lax.conv_general_dilated(xp, w[:, None, :], window_strides=(1,), padding='VALID',
                                 dimension_numbers=('NWC', 'WIO', 'NWC'), feature_group_count=CONV_DIM)
    return jax.nn.silu(y + bias), xp[:, -(CONV_WIDTH - 1):]


def ssd_scan(x, dt, a_log, bm, cm, h0):
    bsz, t, nh, hp = x.shape
    L = SSD_CHUNK if t % SSD_CHUNK == 0 else t
    nc = t // L
    rep = nh // SSM_GROUPS
    xc = x.astype(jnp.float32).reshape(bsz, nc, L, nh, hp)
    dtc = dt.astype(jnp.float32).reshape(bsz, nc, L, nh)
    bc = jnp.repeat(bm.astype(jnp.float32), rep, axis=2).reshape(bsz, nc, L, nh, SSM_STATE)
    cc = jnp.repeat(cm.astype(jnp.float32), rep, axis=2).reshape(bsz, nc, L, nh, SSM_STATE)
    acum = jnp.cumsum(dtc * (-jnp.exp(a_log.astype(jnp.float32))), axis=2)
    seg = acum[:, :, :, None, :] - acum[:, :, None, :, :]
    causal = jnp.tril(jnp.ones((L, L), bool))[None, None, :, :, None]
    decay = jnp.exp(jnp.where(causal, seg, -jnp.inf))
    scores = jnp.einsum('bclhn,bcshn->bclsh', cc, bc) * decay
    y = jnp.einsum('bclsh,bcsh,bcshp->bclhp', scores, dtc, xc)
    w_end = jnp.exp(acum[:, :, -1:, :] - acum) * dtc
    states = jnp.einsum('bclhn,bclh,bclhp->bchpn', bc, w_end, xc)
    chunk_decay = jnp.exp(acum[:, :, -1, :])

    def step(h, inp):
        st, d = inp
        return h * d[:, :, None, None] + st, h

    h_final, h_start = lax.scan(step, h0.astype(jnp.float32),
                                (jnp.moveaxis(states, 1, 0), jnp.moveaxis(chunk_decay, 1, 0)))
    h_start = jnp.moveaxis(h_start, 0, 1)
    y = y + jnp.einsum('bclhn,bchpn,bclh->bclhp', cc, h_start, jnp.exp(acum))
    return y.reshape(bsz, t, nh, hp), h_final


def ssm_mixer(z, xbc, dt_raw, conv_buf, h0, conv_w, conv_b, dt_bias, a_log, d_skip, norm_w):
    b, t, _ = xbc.shape
    xc, new_buf = causal_conv(xbc, conv_buf, conv_w, conv_b)
    xs, bm, cm = jnp.split(xc, [SSM_WIDTH, SSM_WIDTH + SSM_GROUPS * SSM_STATE], axis=-1)
    xs = xs.reshape(b, t, SSM_HEADS, SSM_HEAD_DIM)
    bm = bm.reshape(b, t, SSM_GROUPS, SSM_STATE)
    cm = cm.reshape(b, t, SSM_GROUPS, SSM_STATE)
    dt = jax.nn.softplus(dt_raw.astype(jnp.float32) + dt_bias.astype(jnp.float32))
    y, h_new = ssd_scan(xs, dt, a_log, bm, cm, h0)
    y = y + d_skip.astype(jnp.float32)[:, None] * xs.astype(jnp.float32)
    y = y.reshape(b, t, SSM_WIDTH) * jax.nn.silu(z.astype(jnp.float32))
    return rms_norm(y, norm_w).astype(z.dtype), new_buf, h_new


def merge_heads(o_attn, y_ssm, attn_norm_w, w_out):
    return jnp.concatenate([rms_norm(o_attn, attn_norm_w), y_ssm], axis=-1) @ w_out


def swiglu(x, w_gu, w_down):
    g, u = jnp.split(x @ w_gu, 2, axis=-1)
    return (jax.nn.silu(g) * u) @ w_down


def route(h, w_router, e_bias):
    scores = jax.nn.sigmoid((h @ w_router).astype(jnp.float32))
    biased = scores + e_bias.astype(jnp.float32)
    per_group = N_EXPERTS // N_EXPERT_GROUPS
    grp_score = lax.top_k(biased.reshape(-1, N_EXPERT_GROUPS, per_group), 2)[0].sum(-1)
    _, gidx = lax.top_k(grp_score, TOPK_GROUPS)
    keep = jnp.any(gidx[:, :, None] == jnp.arange(N_EXPERT_GROUPS)[None, None, :], axis=1)
    keep = jnp.repeat(keep, per_group, axis=-1)
    _, idx = lax.top_k(jnp.where(keep, biased, NEG), TOP_K)
    w = jnp.take_along_axis(scores, idx, axis=-1)
    return idx, w / jnp.sum(w, axis=-1, keepdims=True) * ROUTED_SCALE


def moe_dispatch(h, idx, wts, w_gu, w_down):
    n, d = h.shape
    n_slots = n * TOP_K
    e_flat = idx.reshape(-1)
    tok = jnp.repeat(jnp.arange(n, dtype=jnp.int32), TOP_K)
    order = jnp.argsort(e_flat)
    e_sorted = e_flat[order]
    counts = jnp.bincount(e_flat, length=N_EXPERTS)
    padded = (counts + MOE_BLOCK - 1) // MOE_BLOCK * MOE_BLOCK
    start = jnp.cumsum(counts) - counts
    pad_end = jnp.cumsum(padded)
    dest = pad_end[e_sorted] - padded[e_sorted] + jnp.arange(n_slots) - start[e_sorted]
    n_blocks = -(-n_slots // MOE_BLOCK) + N_EXPERTS
    cap = n_blocks * MOE_BLOCK
    buf_tok = jnp.full((cap,), n, jnp.int32).at[dest].set(tok[order])
    buf_w = jnp.zeros((cap,), h.dtype).at[dest].set(wts.reshape(-1)[order].astype(h.dtype))
    blk_expert = jnp.minimum(jnp.searchsorted(pad_end, jnp.arange(n_blocks) * MOE_BLOCK, side='right'),
                             N_EXPERTS - 1)
    h_pad = jnp.concatenate([h, jnp.zeros((1, d), h.dtype)], axis=0)

    def expert_block(args):
        e, toks = args
        return swiglu(h_pad[toks], w_gu[e], w_down[e])

    out = lax.map(expert_block, (blk_expert, buf_tok.reshape(n_blocks, MOE_BLOCK)))
    out = out.reshape(cap, d) * buf_w[:, None]
    return jax.ops.segment_sum(out, buf_tok, num_segments=n + 1)[:n]


def channel_mixer(h, w_router, e_bias, w_exp_gu, w_exp_down, w_sh_gu, w_sh_down):
    b, t, d = h.shape
    hf = h.reshape(b * t, d)
    idx, wts = route(hf, w_router, e_bias)
    y = moe_dispatch(hf, idx, wts, w_exp_gu, w_exp_down) + swiglu(hf, w_sh_gu, w_sh_down)
    return y.reshape(b, t, d)


def setup_inputs(seed: int = 0) -> dict:
    key = jax.random.key(seed)
    keys = iter(jax.random.split(key, 48))

    def nrm(shape, scale):
        return jax.random.normal(next(keys), shape, jnp.float32) * scale

    n_pages = PAST_LEN // PAGE_SIZE
    n_used = DEC_BATCH * n_pages
    n_pool = n_used + n_used // 4
    page_table = jax.random.permutation(next(keys), n_pool)[:n_used].reshape(DEC_BATCH, n_pages).astype(jnp.int32)
    dt0 = jnp.exp(jax.random.uniform(next(keys), (DEPTH, SSM_HEADS), jnp.float32, math.log(1e-3), math.log(1e-1)))
    a_log = jnp.log(jax.random.uniform(next(keys), (DEPTH, SSM_HEADS), jnp.float32, 1.0, 16.0))
    return {
        'x_prompt': nrm((BATCH, SEQ, D_MODEL), 1.0),
        'x_sample': nrm((DEC_BATCH, DEC_SEQ, D_MODEL), 1.0),
        'cache_kv': nrm((DEPTH, n_pool, PAGE_SIZE, 4, N_KV_HEADS, HEAD_DIM), 1.0),
        'cache_win': nrm((DEPTH, DEC_BATCH, WINDOW, 2, N_KV_HEADS, HEAD_DIM), 1.0),
        'state_ssm': nrm((DEPTH, DEC_BATCH, SSM_HEADS, SSM_HEAD_DIM, SSM_STATE), 0.1),
        'state_conv': nrm((DEPTH, DEC_BATCH, CONV_WIDTH - 1, CONV_DIM), 1.0),
        'page_table': page_table,
        'c_prompt': nrm((BATCH, D_MODEL), 1.0),
        'c_sample': nrm((DEC_BATCH, D_MODEL), 1.0),
        'w_ada': nrm((DEPTH, D_MODEL, 6 * D_MODEL), 0.5 * D_MODEL ** -0.5),
        'b_ada': nrm((DEPTH, 6 * D_MODEL), 0.02),
        'norm1_w': 1.0 + nrm((DEPTH, D_MODEL), 0.02),
        'norm2_w': 1.0 + nrm((DEPTH, D_MODEL), 0.02),
        'w_in': nrm((DEPTH, D_MODEL, N_IN), D_MODEL ** -0.5),
        'q_norm_w': 1.0 + nrm((DEPTH, HEAD_DIM), 0.02),
        'k_norm_w': 1.0 + nrm((DEPTH, 3, HEAD_DIM), 0.02),
        'cmp_pe': nrm((DEPTH, 2, CMP_LEN, HEAD_DIM), 0.1),
        'cmp_w1': nrm((DEPTH, 2, CMP_LEN * HEAD_DIM, CMP_HIDDEN), (CMP_LEN * HEAD_DIM) ** -0.5),
        'cmp_w2': nrm((DEPTH, 2, CMP_HIDDEN, HEAD_DIM), CMP_HIDDEN ** -0.5),
        'attn_out_norm_w': 1.0 + nrm((DEPTH, ATTN_WIDTH), 0.02),
        'conv_w': nrm((DEPTH, CONV_WIDTH, CONV_DIM), CONV_WIDTH ** -0.5),
        'conv_b': nrm((DEPTH, CONV_DIM), 0.02),
        'dt_bias': dt0 + jnp.log(-jnp.expm1(-dt0)),
        'a_log': a_log,
        'd_skip': 1.0 + nrm((DEPTH, SSM_HEADS), 0.1),
        'ssm_norm_w': 1.0 + nrm((DEPTH, SSM_WIDTH), 0.02),
        'w_out': nrm((DEPTH, MIX_WIDTH, D_MODEL), MIX_WIDTH ** -0.5),
        'w_router': nrm((DEPTH, D_MODEL, N_EXPERTS), D_MODEL ** -0.5),
        'e_bias': nrm((DEPTH, N_EXPERTS), 0.01),
        'w_exp_gu': nrm((DEPTH, N_EXPERTS, D_MODEL, 2 * D_EXPERT), D_MODEL ** -0.5),
        'w_exp_down': nrm((DEPTH, N_EXPERTS, D_EXPERT, D_MODEL), D_EXPERT ** -0.5),
        'w_sh_gu': nrm((DEPTH, D_MODEL, 2 * D_SHARED), D_MODEL ** -0.5),
        'w_sh_down': nrm((DEPTH, D_SHARED, D_MODEL), D_SHARED ** -0.5),
    }


def reference(x_prompt, x_sample, cache_kv, cache_win, state_ssm, state_conv, page_table,
              c_prompt, c_sample, w_ada, b_ada, norm1_w, norm2_w, w_in, q_norm_w, k_norm_w,
              cmp_pe, cmp_w1, cmp_w2, attn_out_norm_w, conv_w, conv_b, dt_bias, a_log, d_skip,
              ssm_norm_w, w_out, w_router, e_bias, w_exp_gu, w_exp_down, w_sh_gu, w_sh_down):
    xp = x_prompt
    xq = x_sample
    bp, tp, _ = xp.shape
    bq, tq, _ = xq.shape
    past_len = page_table.shape[1] * PAGE_SIZE
    pos_p = jnp.arange(tp)
    pos_q = past_len + jnp.arange(tq)
    kv_p, win_p, ssm_p, conv_p = [], [], [], []
    kv_q, win_q, ssm_q, conv_q = [], [], [], []
    for l in range(DEPTH):
        sh1p, sc1p, g1p, sh2p, sc2p, g2p = adaln(c_prompt, w_ada[l], b_ada[l])
        sh1q, sc1q, g1q, sh2q, sc2q, g2q = adaln(c_sample, w_ada[l], b_ada[l])

        h = modulate(xp, norm1_w[l], sh1p, sc1p)
        q, gates, kv_rows, win_rows, z, xbc, dt = project(h, w_in[l], q_norm_w[l], k_norm_w[l], pos_p)
        ctx = nsa_context(kv_rows, cmp_pe[l], cmp_w1[l], cmp_w2[l])
        win_all = jnp.concatenate([jnp.zeros((bp, WINDOW) + win_rows.shape[2:], win_rows.dtype), win_rows], axis=1)

        def attend_block(ci):
            t0 = ci * Q_BLOCK
            qc = lax.dynamic_slice_in_dim(q, t0, Q_BLOCK, axis=1)
            gc = lax.dynamic_slice_in_dim(gates, t0, Q_BLOCK, axis=1)
            wc = lax.dynamic_slice_in_dim(win_all, t0, WINDOW + Q_BLOCK, axis=1)
            qpos = t0 + jnp.arange(Q_BLOCK)
            wpos = t0 - WINDOW + jnp.arange(WINDOW + Q_BLOCK)
            return nsa_attend(qc, gc, qpos, ctx, wc[:, :, 0], wc[:, :, 1], wpos)

        o = lax.map(attend_block, jnp.arange(tp // Q_BLOCK))
        o = jnp.moveaxis(o, 0, 1).reshape(bp, tp, ATTN_WIDTH)
        y_ssm, conv_new, h_new = ssm_mixer(
            z, xbc, dt, jnp.zeros((bp, CONV_WIDTH - 1, CONV_DIM), xbc.dtype),
            jnp.zeros((bp, SSM_HEADS, SSM_HEAD_DIM, SSM_STATE), jnp.float32),
            conv_w[l], conv_b[l], dt_bias[l], a_log[l], d_skip[l], ssm_norm_w[l])
        xp = xp + g1p[:, None, :] * merge_heads(o, y_ssm, attn_out_norm_w[l], w_out[l])
        h = modulate(xp, norm2_w[l], sh2p, sc2p)
        xp = xp + g2p[:, None, :] * channel_mixer(h, w_router[l], e_bias[l], w_exp_gu[l], w_exp_down[l],
                                                  w_sh_gu[l], w_sh_down[l])
        kv_p.append(kv_rows)
        win_p.append(win_all[:, -WINDOW:])
        ssm_p.append(h_new.astype(xp.dtype))
        conv_p.append(conv_new)

        h = modulate(xq, norm1_w[l], sh1q, sc1q)
        q, gates, kv_rows, win_rows, z, xbc, dt = project(h, w_in[l], q_norm_w[l], k_norm_w[l], pos_q)
        past = cache_kv[l][page_table].reshape((bq, past_len) + kv_rows.shape[2:])
        ctx = nsa_context(jnp.concatenate([past, kv_rows.astype(past.dtype)], axis=1), cmp_pe[l], cmp_w1[l], cmp_w2[l])
        win_all = jnp.concatenate([cache_win[l], win_rows.astype(cache_win.dtype)], axis=1)
        wpos = past_len - WINDOW + jnp.arange(WINDOW + tq)
        o = nsa_attend(q, gates, pos_q, ctx, win_all[:, :, 0], win_all[:, :, 1], wpos)
        y_ssm, conv_new, h_new = ssm_mixer(z, xbc, dt, state_conv[l].astype(xbc.dtype), state_ssm[l],
                                           conv_w[l], conv_b[l], dt_bias[l], a_log[l], d_skip[l], ssm_norm_w[l])
        xq = xq + g1q[:, None, :] * merge_heads(o, y_ssm, attn_out_norm_w[l], w_out[l])
        h = modulate(xq, norm2_w[l], sh2q, sc2q)
        xq = xq + g2q[:, None, :] * channel_mixer(h, w_router[l], e_bias[l], w_exp_gu[l], w_exp_down[l],
                                                  w_sh_gu[l], w_sh_down[l])
        kv_q.append(kv_rows)
        win_q.append(win_all[:, -WINDOW:])
        ssm_q.append(h_new.astype(xq.dtype))
        conv_q.append(conv_new)

    kv_prompt = jnp.stack(kv_p)
    win_prompt = jnp.stack(win_p)
    ssm_prompt = jnp.stack(ssm_p)
    conv_prompt = jnp.stack(conv_p)
    kv_sample = jnp.stack(kv_q)
    win_sample = jnp.stack(win_q)
    ssm_sample = jnp.stack(ssm_q)
    conv_sample = jnp.stack(conv_q)
    return (xp, xq, kv_prompt, win_prompt, ssm_prompt, conv_prompt, kv_sample, win_sample, ssm_sample, conv_sample)
```

```python
import functools
import math

import jax
import jax.numpy as jnp
import numpy as np
from jax import lax
from jax.experimental import pallas as pl
from jax.experimental.pallas import tpu as pltpu

D_MODEL = 1024
PAGE_SIZE = 128
HEAD_DIM = 64
N_Q_HEADS = 8
N_KV_HEADS = 2
GQA_GROUP = N_Q_HEADS // N_KV_HEADS
ATTN_WIDTH = N_Q_HEADS * HEAD_DIM
KV_WIDTH = N_KV_HEADS * HEAD_DIM
ROPE_DIM = HEAD_DIM // 4
ROPE_THETA = 500000.0
CMP_LEN = 32
CMP_STRIDE = 16
CMP_HIDDEN = 4 * HEAD_DIM
SEL_BLOCK = 64
N_SEL = 16
N_LOCAL = 2
WINDOW = 512
SSM_HEADS = 8
SSM_HEAD_DIM = 64
SSM_WIDTH = SSM_HEADS * SSM_HEAD_DIM
SSM_GROUPS = 2
SSM_STATE = 128
CONV_WIDTH = 4
CONV_DIM = SSM_WIDTH + 2 * SSM_GROUPS * SSM_STATE
SSD_CHUNK = 128
MIX_WIDTH = ATTN_WIDTH + SSM_WIDTH
N_EXPERTS = 64
N_EXPERT_GROUPS = 8
TOPK_GROUPS = 4
TOP_K = 8
D_EXPERT = 256
D_SHARED = 256
ROUTED_SCALE = 2.5
IN_SIZES = (ATTN_WIDTH, 6 * KV_WIDTH, 3 * N_Q_HEADS, SSM_WIDTH, CONV_DIM, SSM_HEADS)
N_IN = sum(IN_SIZES)
EPS = 1e-6
NEG = -1e30
BIG = 1e6

LANES = 128
VMEM_LIMIT = 56 * 1024 * 1024

BF16 = jnp.bfloat16
F32 = jnp.float32


def _cparams(sem):
    return pltpu.CompilerParams(dimension_semantics=sem, vmem_limit_bytes=VMEM_LIMIT)


def _silu(x):
    return x * jax.nn.sigmoid(x)


def _dot(a, b):
    return jnp.dot(a, b, preferred_element_type=F32)


def _dot_nt(a, b):
    return lax.dot_general(a, b, (((1,), (1,)), ((), ())), preferred_element_type=F32)


def _mod_spec(mod, col, tm, tiles_per_b, row0):
    if mod.ndim == 3:
        return pl.BlockSpec((1, 1, D_MODEL), lambda i, *_: (row0 + i // tiles_per_b, 0, col))
    return pl.BlockSpec((tm, D_MODEL), lambda i, *_: (i, col))


def _mod(ref):
    return ref[0] if len(ref.shape) == 3 else ref[...]


def _adaln_kernel(c_ref, w_ref, b_ref, o_ref):
    c = c_ref[...]
    a = _silu(c).astype(BF16)
    o_ref[...] = _dot(a, w_ref[...].astype(BF16)) + b_ref[...]


def adaln_all(c_all, w_ada, b_ada):
    rows = c_all.shape[0]
    n = w_ada.shape[1]
    tn = 1024
    return pl.pallas_call(
        _adaln_kernel,
        grid=(n // tn,),
        in_specs=[
            pl.BlockSpec((rows, D_MODEL), lambda j: (0, 0)),
            pl.BlockSpec((D_MODEL, tn), lambda j: (0, j)),
            pl.BlockSpec((1, tn), lambda j: (0, j)),
        ],
        out_specs=pl.BlockSpec((rows, tn), lambda j: (0, j)),
        out_shape=jax.ShapeDtypeStruct((rows, n), F32),
        compiler_params=_cparams(("arbitrary",)),
        name="adaln",
    )(c_all, w_ada, b_ada.reshape(1, n))


_C_Q = 0
_C_KV = _C_Q + ATTN_WIDTH
_C_Z = _C_KV + 6 * KV_WIDTH
_C_XBC = _C_Z + SSM_WIDTH
_C_MISC = _C_XBC + CONV_DIM
N_IN_PAD = _C_MISC + LANES
N_GATES = 3 * N_Q_HEADS


def _prep_w_in(w_in):
    s = np.cumsum((0,) + IN_SIZES)
    q, kv, g, z, xbc, dt = (w_in[:, int(s[i]):int(s[i + 1])] for i in range(6))
    pad = jnp.zeros((w_in.shape[0], LANES - N_GATES - SSM_HEADS), w_in.dtype)
    return jnp.concatenate([q, kv, z, xbc, dt, g, pad], axis=1).astype(BF16)


def _group_mean_matrix(width):
    i = np.arange(width)
    m = (i[:, None] // HEAD_DIM == i[None, :] // HEAD_DIM).astype(np.float32) / HEAD_DIM
    return jnp.asarray(m, BF16)


def _rope_tables(pos):
    half = ROPE_DIM // 2
    inv_freq = ROPE_THETA ** (-jnp.arange(half, dtype=F32) / half)
    ang = pos.astype(F32)[:, None] * inv_freq[None, :]
    cos, sin = jnp.cos(ang), jnp.sin(ang)
    t = pos.shape[0]
    one = jnp.ones((t, HEAD_DIM - ROPE_DIM), F32)
    zero = jnp.zeros((t, HEAD_DIM - ROPE_DIM), F32)
    zh = jnp.zeros((t, half), F32)
    c = jnp.concatenate([cos, cos, one], axis=1)
    s_up = jnp.concatenate([-sin, zh, zero], axis=1)
    s_dn = jnp.concatenate([zh, sin, zero], axis=1)
    rep = LANES // HEAD_DIM
    return jnp.tile(c, (1, rep)), jnp.tile(s_up, (1, rep)), jnp.tile(s_dn, (1, rep))


def _rope(x, c, s_up, s_dn):
    w = x.shape[1]
    half = ROPE_DIM // 2
    rep = w // LANES
    ct = jnp.concatenate([c] * rep, axis=1) if rep > 1 else c
    su = jnp.concatenate([s_up] * rep, axis=1) if rep > 1 else s_up
    sd = jnp.concatenate([s_dn] * rep, axis=1) if rep > 1 else s_dn
    up = pltpu.roll(x, w - half, axis=1)
    dn = pltpu.roll(x, half, axis=1)
    return x * ct + up * su + dn * sd


def _inproj_kernel(x_ref, shift_ref, scale_ref, nw_ref, w_ref, qw_ref, kw_ref, gq_ref, gk_ref,
                   c_ref, su_ref, sd_ref,
                   qp_ref, kv_ref, kvb_ref, win_ref, winb_ref, z_ref, xbc_ref, misc_ref):
    x = x_ref[...]
    ms = jnp.mean(x * x, axis=-1, keepdims=True)
    h = x * lax.rsqrt(ms + EPS) * nw_ref[...]
    h = h * (1.0 + _mod(scale_ref)) + _mod(shift_ref)
    hb = h.astype(BF16)
    c, su, sd = c_ref[...], su_ref[...], sd_ref[...]

    q = _dot(hb, w_ref[:, _C_Q:_C_Q + ATTN_WIDTH])
    qms = _dot((q * q).astype(BF16), gq_ref[...])
    q = q * lax.rsqrt(qms + EPS) * qw_ref[...]
    q = _rope(q, c, su, sd) * (HEAD_DIM ** -0.5)
    lane = lax.broadcasted_iota(jnp.int32, q.shape, 1) % LANES
    lo = lane < HEAD_DIM
    q_up = pltpu.roll(q, ATTN_WIDTH - HEAD_DIM, axis=1)
    q_dn = pltpu.roll(q, HEAD_DIM, axis=1)
    zero = jnp.zeros_like(q)
    nat_lo = jnp.where(lo, q, zero)
    nat_hi = jnp.where(lo, zero, q)
    up_lo = jnp.where(lo, q_up, zero)
    dn_hi = jnp.where(lo, zero, q_dn)
    blocks = []
    for hd in range(N_Q_HEADS):
        pair = hd // 2
        sl = slice(pair * LANES, (pair + 1) * LANES)
        if hd < GQA_GROUP:
            blocks.append((nat_lo if hd % 2 == 0 else up_lo)[:, sl])
        else:
            blocks.append((dn_hi if hd % 2 == 0 else nat_hi)[:, sl])
    qp_ref[...] = jnp.concatenate(blocks, axis=1).astype(BF16)

    kv = _dot(hb, w_ref[:, _C_KV:_C_KV + 6 * KV_WIDTH])
    outs = []
    for br in range(3):
        k = kv[:, br * 2 * KV_WIDTH:br * 2 * KV_WIDTH + KV_WIDTH]
        v = kv[:, br * 2 * KV_WIDTH + KV_WIDTH:(br + 1) * 2 * KV_WIDTH]
        kms = _dot((k * k).astype(BF16), gk_ref[...])
        k = k * lax.rsqrt(kms + EPS) * kw_ref[:, br * KV_WIDTH:(br + 1) * KV_WIDTH]
        k = _rope(k, c, su, sd)
        outs += [k, v]
    kvrows = jnp.concatenate(outs[:4], axis=1)
    winrows = jnp.concatenate(outs[4:], axis=1)
    kv_ref[...] = kvrows
    kvb_ref[...] = kvrows.astype(BF16)
    win_ref[...] = winrows
    winb_ref[...] = winrows.astype(BF16)

    z_ref[...] = _dot(hb, w_ref[:, _C_Z:_C_Z + SSM_WIDTH])
    xbc_ref[...] = _dot(hb, w_ref[:, _C_XBC:_C_XBC + CONV_DIM])
    misc_ref[...] = _dot(hb, w_ref[:, _C_MISC:_C_MISC + LANES])


def inproj(x, mod3, mod_row0, norm_w, wp, q_norm_w, k_norm_w, pos, tm):
    b, t, d = x.shape
    n = b * t
    tiles_per_b = t // tm
    xf = x.reshape(n, d)
    c, su, sd = _rope_tables(pos)
    qw = jnp.tile(q_norm_w, N_Q_HEADS).reshape(1, ATTN_WIDTH)
    kw = jnp.concatenate([jnp.tile(k_norm_w[i], N_KV_HEADS) for i in range(3)]).reshape(1, 3 * KV_WIDTH)
    gq = _group_mean_matrix(ATTN_WIDTH)
    gk = _group_mean_matrix(KV_WIDTH)

    def mod_spec(col):
        return _mod_spec(mod3, col, tm, tiles_per_b, mod_row0)

    def tok(wd):
        return pl.BlockSpec((tm, wd), lambda i: (i, 0))

    def full(a):
        return pl.BlockSpec(a.shape, lambda i: (0,) * a.ndim)

    rope_spec = pl.BlockSpec((tm, LANES), lambda i: (i % tiles_per_b, 0))
    out_shape = (
        jax.ShapeDtypeStruct((n, N_Q_HEADS * LANES), BF16),
        jax.ShapeDtypeStruct((n, 4 * KV_WIDTH), F32),
        jax.ShapeDtypeStruct((n, 4 * KV_WIDTH), BF16),
        jax.ShapeDtypeStruct((n, 2 * KV_WIDTH), F32),
        jax.ShapeDtypeStruct((n, 2 * KV_WIDTH), BF16),
        jax.ShapeDtypeStruct((n, SSM_WIDTH), F32),
        jax.ShapeDtypeStruct((n, CONV_DIM), F32),
        jax.ShapeDtypeStruct((n, LANES), F32),
    )
    return pl.pallas_call(
        _inproj_kernel,
        grid=(n // tm,),
        in_specs=[tok(d), mod_spec(0), mod_spec(1), full(norm_w), full(wp), full(qw), full(kw), full(gq), full(gk),
                  rope_spec, rope_spec, rope_spec],
        out_specs=tuple(tok(s.shape[1]) for s in out_shape),
        out_shape=out_shape,
        compiler_params=_cparams(("arbitrary",)),
        name="inproj",
    )(xf, mod3, mod3, norm_w, wp, qw, kw, gq, gk, c, su, sd)


def _prep_compress(cmp_pe, cmp_w1, cmp_w2):
    half = CMP_LEN // 2
    eye = jnp.eye(N_KV_HEADS, dtype=F32)
    w1 = cmp_w1.reshape(2, CMP_LEN, HEAD_DIM, CMP_HIDDEN)
    w1s = []
    for part in (w1[:, :half], w1[:, half:]):
        w1s.append(jnp.einsum("pjdo,hg->pjhdgo", part, eye).reshape(2, half * KV_WIDTH, N_KV_HEADS * CMP_HIDDEN))
    w1p = jnp.concatenate(w1s, axis=2).astype(BF16)
    pe = cmp_pe.reshape(2, 2, half, 1, HEAD_DIM)
    pep = jnp.broadcast_to(pe, (2, 2, half, N_KV_HEADS, HEAD_DIM)).reshape(2, 2, half * KV_WIDTH)
    w2p = jnp.einsum("poe,hg->phoge", cmp_w2, eye).reshape(2, N_KV_HEADS * CMP_HIDDEN, KV_WIDTH).astype(BF16)
    return w1p, pep, w2p


def _compress_kernel(x_ref, w1_ref, pe_ref, w2_ref, o_ref, *, row_w):
    part = pl.program_id(1)
    nb = x_ref.shape[1]
    half = CMP_LEN // 2
    hid = N_KV_HEADS * CMP_HIDDEN
    cols = []
    for j in range(half):
        a = x_ref[0, :, j * row_w:j * row_w + KV_WIDTH]
        b = x_ref[0, :, j * row_w + KV_WIDTH:j * row_w + 2 * KV_WIDTH]
        cols.append(jnp.where(part == 0, a, b))
    x = jnp.concatenate(cols, axis=1).astype(F32)
    pe = pe_ref[0]
    u = _dot((x + pe[0:1]).astype(BF16), w1_ref[0, :, :hid])
    v = _dot((x + pe[1:2]).astype(BF16), w1_ref[0, :, hid:])
    h1 = u + pltpu.roll(v, nb - 1, axis=0)
    out = _dot(_silu(h1).astype(BF16), w2_ref[0])
    row = lax.broadcasted_iota(jnp.int32, out.shape, 0)
    o_ref[0, 0] = jnp.where(row < nb - 1, out, 0.0).astype(o_ref.dtype)


def compress(kvb, w1p, pep, w2p):
    b, s, row_w = kvb.shape
    nb = s // CMP_STRIDE
    x = kvb.reshape(b, nb, CMP_STRIDE * row_w)
    return pl.pallas_call(
        functools.partial(_compress_kernel, row_w=row_w),
        grid=(b, 2),
        in_specs=[
            pl.BlockSpec((1, nb, CMP_STRIDE * row_w), lambda i, p: (i, 0, 0)),
            pl.BlockSpec((1,) + w1p.shape[1:], lambda i, p: (p, 0, 0)),
            pl.BlockSpec((1,) + pep.shape[1:], lambda i, p: (p, 0, 0)),
            pl.BlockSpec((1,) + w2p.shape[1:], lambda i, p: (p, 0, 0)),
        ],
        out_specs=pl.BlockSpec((1, 1, nb, KV_WIDTH), lambda i, p: (i, p, 0, 0)),
        out_shape=jax.ShapeDtypeStruct((b, 2, nb, KV_WIDTH), BF16),
        compiler_params=_cparams(("arbitrary", "arbitrary")),
        name="compress",
    )(x, w1p, pep, w2p)


N_SEL_LANES = LANES


def _cover_matrix(nb):
    c = np.arange(nb)[:, None]
    j = np.arange(N_SEL_LANES)[None, :]
    start = c * CMP_STRIDE
    m = (start < (j + 1) * SEL_BLOCK) & (start + CMP_LEN > j * SEL_BLOCK)
    return jnp.asarray(m.astype(np.float32), BF16)


def _place_heads(res, kv):
    lane = lax.broadcasted_iota(jnp.int32, res[0].shape, 1)
    lo = lane < HEAD_DIM
    blocks = []
    for pair in range(GQA_GROUP // 2):
        a, b = res[2 * pair], res[2 * pair + 1]
        if kv == 0:
            blocks.append(jnp.where(lo, a, pltpu.roll(b, HEAD_DIM, axis=1)))
        else:
            blocks.append(jnp.where(lo, pltpu.roll(a, HEAD_DIM, axis=1), b))
    return jnp.concatenate(blocks, axis=1)


def _cmp_select_kernel(q_ref, kc_ref, vc_ref, cover_ref, o_ref, m_ref, *, q_off, n_pick):
    tq = q_ref.shape[1]
    nb = kc_ref.shape[2]
    t0 = q_off + pl.program_id(1) * tq
    kc = kc_ref[0, 0]
    vc = vc_ref[0, 0]
    qpos = t0 + lax.broadcasted_iota(jnp.int32, (tq, nb), 0)
    cend = lax.broadcasted_iota(jnp.int32, (tq, nb), 1) * CMP_STRIDE + (CMP_LEN - 1)
    valid = cend <= qpos
    blk = lax.broadcasted_iota(jnp.int32, (tq, N_SEL_LANES), 1)
    cur = (t0 + lax.broadcasted_iota(jnp.int32, (tq, N_SEL_LANES), 0)) // SEL_BLOCK
    forced = (blk == 0) | ((blk <= cur) & (blk > cur - N_LOCAL))
    o_groups = []
    for kv in range(N_KV_HEADS):
        res = []
        psum = jnp.zeros((tq, nb), F32)
        for g in range(GQA_GROUP):
            hd = kv * GQA_GROUP + g
            s = _dot_nt(q_ref[0, :, hd * LANES:(hd + 1) * LANES], kc)
            s = jnp.where(valid, s, NEG)
            e = jnp.exp(s - jnp.max(s, axis=1, keepdims=True))
            p = e / jnp.sum(e, axis=1, keepdims=True)
            p = jnp.where(valid, p, 0.0)
            psum = psum + p
            res.append(_dot(p.astype(BF16), vc))
        o_groups.append(_place_heads(res, kv))
        hi = psum.astype(BF16)
        lo = (psum - hi.astype(F32)).astype(BF16)
        imp = _dot(hi, cover_ref[...]) + _dot(lo, cover_ref[...])
        x = jnp.where(forced, BIG, jnp.where(blk > cur, -BIG, imp))
        sel = jnp.zeros(x.shape, jnp.bool_)
        for _ in range(n_pick):
            mx = jnp.max(x, axis=1, keepdims=True)
            idx = jnp.min(jnp.where(x == mx, blk, N_SEL_LANES), axis=1, keepdims=True)
            hit = blk == idx
            sel = sel | hit
            x = jnp.where(hit, -jnp.inf, x)
        m_ref[0, kv] = jnp.where(sel, 0.0, NEG).astype(m_ref.dtype)
    o_ref[0] = jnp.concatenate(o_groups, axis=1)


def cmp_select(qp, kcv, q_off, n_pick, tq):
    b, t, _ = qp.shape
    nb = kcv.shape[2]
    cover = _cover_matrix(nb)
    return pl.pallas_call(
        functools.partial(_cmp_select_kernel, q_off=q_off, n_pick=n_pick),
        grid=(b, t // tq),
        in_specs=[
            pl.BlockSpec((1, tq, N_Q_HEADS * LANES), lambda i, j: (i, j, 0)),
            pl.BlockSpec((1, 1, nb, KV_WIDTH), lambda i, j: (i, 0, 0, 0)),
            pl.BlockSpec((1, 1, nb, KV_WIDTH), lambda i, j: (i, 1, 0, 0)),
            pl.BlockSpec((nb, N_SEL_LANES), lambda i, j: (0, 0)),
        ],
        out_specs=(
            pl.BlockSpec((1, tq, ATTN_WIDTH), lambda i, j: (i, j, 0)),
            pl.BlockSpec((1, N_KV_HEADS, tq, N_SEL_LANES), lambda i, j: (i, 0, j, 0)),
        ),
        out_shape=(
            jax.ShapeDtypeStruct((b, t, ATTN_WIDTH), F32),
            jax.ShapeDtypeStruct((b, N_KV_HEADS, t, N_SEL_LANES), BF16),
        ),
        compiler_params=_cparams(("arbitrary", "arbitrary")),
        name="cmp_select",
    )(qp, kcv, kcv, cover)


SEL_CHUNK = 256
WIN_CHUNK = 128


def _block_onehot(s):
    key = np.arange(s)[:, None]
    j = np.arange(N_SEL_LANES)[None, :]
    return jnp.asarray((key // SEL_BLOCK == j).astype(np.float32), BF16)


def _gate_expand():
    m = np.zeros((3, LANES, ATTN_WIDTH), np.float32)
    for br in range(3):
        for hd in range(N_Q_HEADS):
            m[br, SSM_HEADS + 3 * hd + br, hd * HEAD_DIM:(hd + 1) * HEAD_DIM] = 1.0
    return jnp.asarray(m, BF16)


def _flash_update(s, v, m_ref, l_ref, acc_ref):
    m_old = m_ref[...]
    m_new = jnp.maximum(m_old, jnp.max(s, axis=1, keepdims=True))
    alpha = jnp.exp(m_old - m_new)
    p = jnp.exp(s - m_new)
    l_ref[...] = alpha * l_ref[...] + jnp.sum(p, axis=1, keepdims=True)
    acc_ref[...] = alpha * acc_ref[...] + _dot(p.astype(BF16), v)
    m_ref[...] = m_new


def _sel_win_kernel(q_ref, mneg_ref, ksel_ref, vsel_ref, et_ref, kwin_ref, vwin_ref, ocmp_ref, misc_ref, eg_ref,
                    o_ref, lhs_ref, m_ref, l_ref, acc_ref, *, q_off, win_pos0):
    tq = q_ref.shape[1]
    rows = GQA_GROUP * tq
    t0 = q_off + pl.program_id(1) * tq
    n_sel = lax.shift_right_logical(t0 + tq - 1, int(math.log2(SEL_CHUNK))) + 1
    w_lo = jnp.maximum(t0 - (WINDOW - 1) - win_pos0, 0) // WIN_CHUNK
    w_hi = (t0 + tq - 1 - win_pos0) // WIN_CHUNK + 1
    qrow = lax.broadcasted_iota(jnp.int32, (rows, 1), 0) % tq + t0

    def init():
        m_ref[...] = jnp.full(m_ref.shape, NEG, F32)
        l_ref[...] = jnp.zeros(l_ref.shape, F32)
        acc_ref[...] = jnp.zeros(acc_ref.shape, F32)

    def finish(kv):
        out = acc_ref[...] / l_ref[...]
        return _place_heads([out[g * tq:(g + 1) * tq] for g in range(GQA_GROUP)], kv)

    o_sel, o_win = [], []
    for kv in range(N_KV_HEADS):
        for g in range(GQA_GROUP):
            hd = kv * GQA_GROUP + g
            lhs_ref[g * tq:(g + 1) * tq, :LANES] = q_ref[0, :, hd * LANES:(hd + 1) * LANES]
            lhs_ref[g * tq:(g + 1) * tq, LANES:] = mneg_ref[0, kv]

        init()

        def sel_step(c, carry):
            r0 = pl.multiple_of(c * SEL_CHUNK, SEL_CHUNK)
            rhs = jnp.concatenate([ksel_ref[0, pl.ds(r0, SEL_CHUNK), :], et_ref[pl.ds(r0, SEL_CHUNK), :]], axis=1)
            s = _dot_nt(lhs_ref[...], rhs)
            kpos = r0 + lax.broadcasted_iota(jnp.int32, (1, SEL_CHUNK), 1)
            s = jnp.where(kpos <= qrow, s, NEG)
            _flash_update(s, vsel_ref[0, pl.ds(r0, SEL_CHUNK), :], m_ref, l_ref, acc_ref)
            return carry

        lax.fori_loop(0, n_sel, sel_step, 0)
        o_sel.append(finish(kv))

        init()

        def win_step(c, carry):
            r0 = pl.multiple_of(c * WIN_CHUNK, WIN_CHUNK)
            s = _dot_nt(lhs_ref[:, :LANES], kwin_ref[0, pl.ds(r0, WIN_CHUNK), :])
            wpos = win_pos0 + r0 + lax.broadcasted_iota(jnp.int32, (1, WIN_CHUNK), 1)
            s = jnp.where((wpos <= qrow) & (wpos > qrow - WINDOW), s, NEG)
            _flash_update(s, vwin_ref[0, pl.ds(r0, WIN_CHUNK), :], m_ref, l_ref, acc_ref)
            return carry

        lax.fori_loop(w_lo, w_hi, win_step, 0)
        o_win.append(finish(kv))

    gates = jax.nn.sigmoid(misc_ref[0])
    ghi = gates.astype(BF16)
    glo = (gates - ghi.astype(F32)).astype(BF16)
    branches = (ocmp_ref[0], jnp.concatenate(o_sel, axis=1), jnp.concatenate(o_win, axis=1))
    out = jnp.zeros(branches[0].shape, F32)
    for br in range(3):
        out = out + (_dot(ghi, eg_ref[br]) + _dot(glo, eg_ref[br])) * branches[br]
    o_ref[0] = out


def sel_win_attention(qp, mneg, kvb, winb, o_cmp, misc, q_off, win_pos0, tq):
    b, t, _ = qp.shape
    s = kvb.shape[1]
    sw = winb.shape[1]
    et = _block_onehot(s)
    eg = _gate_expand()
    rows = GQA_GROUP * tq
    return pl.pallas_call(
        functools.partial(_sel_win_kernel, q_off=q_off, win_pos0=win_pos0),
        grid=(b, t // tq),
        in_specs=[
            pl.BlockSpec((1, tq, N_Q_HEADS * LANES), lambda i, j: (i, j, 0)),
            pl.BlockSpec((1, N_KV_HEADS, tq, N_SEL_LANES), lambda i, j: (i, 0, j, 0)),
            pl.BlockSpec((1, s, KV_WIDTH), lambda i, j: (i, 0, 2)),
            pl.BlockSpec((1, s, KV_WIDTH), lambda i, j: (i, 0, 3)),
            pl.BlockSpec((s, N_SEL_LANES), lambda i, j: (0, 0)),
            pl.BlockSpec((1, sw, KV_WIDTH), lambda i, j: (i, 0, 0)),
            pl.BlockSpec((1, sw, KV_WIDTH), lambda i, j: (i, 0, 1)),
            pl.BlockSpec((1, tq, ATTN_WIDTH), lambda i, j: (i, j, 0)),
            pl.BlockSpec((1, tq, LANES), lambda i, j: (i, j, 0)),
            pl.BlockSpec((3, LANES, ATTN_WIDTH), lambda i, j: (0, 0, 0)),
        ],
        out_specs=pl.BlockSpec((1, tq, ATTN_WIDTH), lambda i, j: (i, j, 0)),
        out_shape=jax.ShapeDtypeStruct((b, t, ATTN_WIDTH), F32),
        scratch_shapes=[
            pltpu.VMEM((rows, 2 * LANES), BF16),
            pltpu.VMEM((rows, 1), F32),
            pltpu.VMEM((rows, 1), F32),
            pltpu.VMEM((rows, LANES), F32),
        ],
        compiler_params=_cparams(("arbitrary", "arbitrary")),
        name="sel_win_attention",
    )(qp, mneg, kvb, kvb, et, winb, winb, o_cmp, misc, eg)


CONV_PAD = 8
HEAD_PAIRS = SSM_HEADS // 2


def _split3(x):
    a = x.astype(BF16)
    r = x - a.astype(F32)
    b = r.astype(BF16)
    c = (r - b.astype(F32)).astype(BF16)
    return a, b, c


def _ssd_kernel(xbc_ref, z_ref, misc_ref, conv0_ref, h0_ref, cw_ref, cb_ref, dtb_ref, a_ref, dsk_ref, nw_ref,
                y_ref, hout_ref, cout_ref, xp_ref, h_ref, ms_ref, *, t_valid):
    ch = pl.program_id(1)
    L = SSD_CHUNK
    keep = CONV_WIDTH - 1

    @pl.when(ch == 0)
    def _():
        xp_ref[...] = jnp.zeros(xp_ref.shape, F32)
        xp_ref[CONV_PAD - keep:CONV_PAD, :] = conv0_ref[0]
        h_ref[...] = h0_ref[0]

    xp_ref[CONV_PAD:CONV_PAD + t_valid, :] = xbc_ref[0]
    conv = cb_ref[...]
    for j in range(CONV_WIDTH):
        conv = conv + cw_ref[j:j + 1, :] * xp_ref[CONV_PAD - keep + j:CONV_PAD - keep + j + L, :]
    last = xp_ref[CONV_PAD + t_valid - keep:CONV_PAD + t_valid, :]
    cout_ref[0] = last
    xp_ref[CONV_PAD - keep:CONV_PAD, :] = last
    xc = _silu(conv)

    row = lax.broadcasted_iota(jnp.int32, (L, LANES), 0)
    lane = lax.broadcasted_iota(jnp.int32, (L, LANES), 1)
    if t_valid == L:
        raw = misc_ref[0]
    else:
        ms_ref[...] = jnp.zeros(ms_ref.shape, F32)
        ms_ref[0:t_valid, :] = misc_ref[0]
        raw = ms_ref[...]
    v = raw + dtb_ref[...]
    dt = jnp.maximum(v, 0.0) + jnp.log(1.0 + jnp.exp(-jnp.abs(v)))
    dt = jnp.where((lane < SSM_HEADS) & (row < t_valid), dt, 0.0)
    da = dt * a_ref[...]
    tri = (lax.broadcasted_iota(jnp.int32, (L, L), 1) <= lax.broadcasted_iota(jnp.int32, (L, L), 0))
    trib = tri.astype(BF16)
    acum = sum(_dot(trib, part) for part in _split3(da))
    acum_t = jnp.transpose(acum)
    dt_t = jnp.transpose(dt)
    e_acum = jnp.exp(acum)
    e_last = jnp.exp(acum[L - 1:L, :])
    w_end = jnp.exp(acum[L - 1:L, :] - acum) * dt
    lo = lane < SSM_HEAD_DIM

    ys = []
    for pair in range(HEAD_PAIRS):
        grp = (2 * pair) // (SSM_HEADS // SSM_GROUPS)
        bg = xc[:, SSM_WIDTH + grp * SSM_STATE:SSM_WIDTH + (grp + 1) * SSM_STATE].astype(BF16)
        cg = xc[:, SSM_WIDTH + (SSM_GROUPS + grp) * SSM_STATE:SSM_WIDTH + (SSM_GROUPS + grp + 1) * SSM_STATE].astype(BF16)
        g = _dot_nt(cg, bg)
        xpair = xc[:, pair * LANES:(pair + 1) * LANES]
        y = jnp.zeros((L, LANES), F32)
        for sub in range(2):
            hd = 2 * pair + sub
            seg = acum[:, hd:hd + 1] - acum_t[hd:hd + 1, :]
            m = g * jnp.exp(jnp.where(tri, seg, NEG)) * dt_t[hd:hd + 1, :]
            xm = jnp.where(lo if sub == 0 else ~lo, xpair, 0.0)
            y = y + _dot(m.astype(BF16), xm.astype(BF16))
        col = lambda a: jnp.where(lo, a[:, 2 * pair:2 * pair + 1], a[:, 2 * pair + 1:2 * pair + 2])
        hp = h_ref[pair]
        y = y + _dot_nt(cg, hp.astype(BF16)) * col(e_acum)
        y = y + col(dsk_ref[...]) * xpair
        xw = (xpair * col(w_end)).astype(BF16)
        st = lax.dot_general(xw, bg, (((0,), (0,)), ((), ())), preferred_element_type=F32)
        prow = lax.broadcasted_iota(jnp.int32, (LANES, LANES), 0) < SSM_HEAD_DIM
        dec = jnp.where(prow, e_last[:, 2 * pair:2 * pair + 1], e_last[:, 2 * pair + 1:2 * pair + 2])
        h_ref[pair] = hp * dec + st
        ys.append(y)
    y = jnp.concatenate(ys, axis=1)
    if t_valid != L:
        y = y[:t_valid]
    y = y * _silu(z_ref[0])
    y = y * lax.rsqrt(jnp.mean(y * y, axis=-1, keepdims=True) + EPS) * nw_ref[...]
    y_ref[0] = y

    @pl.when(ch == pl.num_programs(1) - 1)
    def _():
        hout_ref[0] = h_ref[...]


def ssd(xbc, z, misc, conv0, h0, conv_w, conv_b, dt_bias, a_log, d_skip, norm_w):
    b, t, _ = xbc.shape
    L = SSD_CHUNK
    t_valid = L if t % L == 0 else t
    assert t_valid == L or t < L
    n_ch = max(t // L, 1)
    keep = CONV_WIDTH - 1
    pad8 = lambda v: jnp.pad(v.astype(F32), (0, LANES - SSM_HEADS)).reshape(1, LANES)
    dtb = pad8(dt_bias)
    a = pad8(-jnp.exp(a_log.astype(F32)))
    dsk = pad8(d_skip)
    h0p = h0.reshape(b, HEAD_PAIRS, 2 * SSM_HEAD_DIM, SSM_STATE)
    full = lambda arr: pl.BlockSpec(arr.shape, lambda i, c: (0,) * arr.ndim)
    tok = lambda wd: pl.BlockSpec((1, t_valid, wd), lambda i, c: (i, c, 0))
    y, hout, cout = pl.pallas_call(
        functools.partial(_ssd_kernel, t_valid=t_valid),
        grid=(b, n_ch),
        in_specs=[
            tok(CONV_DIM), tok(SSM_WIDTH), tok(LANES),
            pl.BlockSpec((1, keep, CONV_DIM), lambda i, c: (i, 0, 0)),
            pl.BlockSpec((1, HEAD_PAIRS, 2 * SSM_HEAD_DIM, SSM_STATE), lambda i, c: (i, 0, 0, 0)),
            full(conv_w), pl.BlockSpec((1, CONV_DIM), lambda i, c: (0, 0)),
            full(dtb), full(a), full(dsk), pl.BlockSpec((1, SSM_WIDTH), lambda i, c: (0, 0)),
        ],
        out_specs=(
            tok(SSM_WIDTH),
            pl.BlockSpec((1, HEAD_PAIRS, 2 * SSM_HEAD_DIM, SSM_STATE), lambda i, c: (i, 0, 0, 0)),
            pl.BlockSpec((1, keep, CONV_DIM), lambda i, c: (i, 0, 0)),
        ),
        out_shape=(
            jax.ShapeDtypeStruct((b, t, SSM_WIDTH), F32),
            jax.ShapeDtypeStruct((b, HEAD_PAIRS, 2 * SSM_HEAD_DIM, SSM_STATE), F32),
            jax.ShapeDtypeStruct((b, keep, CONV_DIM), F32),
        ),
        scratch_shapes=[
            pltpu.VMEM((CONV_PAD + L, CONV_DIM), F32),
            pltpu.VMEM((HEAD_PAIRS, 2 * SSM_HEAD_DIM, SSM_STATE), F32),
            pltpu.VMEM((L, LANES), F32),
        ],
        compiler_params=_cparams(("arbitrary", "arbitrary")),
        name="ssd",
    )(xbc, z, misc, conv0, h0p, conv_w, conv_b.reshape(1, CONV_DIM), dtb, a, dsk, norm_w.reshape(1, SSM_WIDTH))
    return y, hout.reshape(b, SSM_HEADS, SSM_HEAD_DIM, SSM_STATE), cout


def _split2(x):
    hi = x.astype(BF16)
    return hi, (x - hi.astype(F32)).astype(BF16)


def _merge_kernel(oa_ref, ys_ref, x_ref, g1_ref, sh2_ref, sc2_ref, anw_ref, wo_ref, n2w_ref, wrh_ref, wrl_ref,
                  x1_ref, h2_ref, lg_ref):
    oa = oa_ref[...]
    a = oa * lax.rsqrt(jnp.mean(oa * oa, axis=-1, keepdims=True) + EPS) * anw_ref[...]
    cat = jnp.concatenate([a.astype(BF16), ys_ref[...].astype(BF16)], axis=1)
    x1 = x_ref[...] + _mod(g1_ref) * _dot(cat, wo_ref[...])
    x1_ref[...] = x1
    h2 = x1 * lax.rsqrt(jnp.mean(x1 * x1, axis=-1, keepdims=True) + EPS) * n2w_ref[...]
    h2 = h2 * (1.0 + _mod(sc2_ref)) + _mod(sh2_ref)
    h2_ref[...] = h2.astype(BF16)
    hh, hl = _split2(h2)
    lg_ref[...] = _dot_nt(wrh_ref[...], hh) + _dot_nt(wrh_ref[...], hl) + _dot_nt(wrl_ref[...], hh)


def merge(o_attn, y_ssm, x, mod3, mod_row0, attn_norm_w, wo, norm2_w, w_router, tm):
    b, t, d = x.shape
    n = b * t
    tiles_per_b = t // tm
    wrt = jnp.transpose(w_router)
    wrh, wrl = _split2(wrt)

    def mod_spec(col):
        return _mod_spec(mod3, col, tm, tiles_per_b, mod_row0)

    tok = lambda wd: pl.BlockSpec((tm, wd), lambda i: (i, 0))
    full = lambda a: pl.BlockSpec(a.shape, lambda i: (0,) * a.ndim)
    return pl.pallas_call(
        _merge_kernel,
        grid=(n // tm,),
        in_specs=[tok(ATTN_WIDTH), tok(SSM_WIDTH), tok(d), mod_spec(2), mod_spec(3), mod_spec(4),
                  full(attn_norm_w), full(wo), full(norm2_w), full(wrh), full(wrl)],
        out_specs=(tok(d), tok(d), pl.BlockSpec((N_EXPERTS, tm), lambda i: (0, i))),
        out_shape=(jax.ShapeDtypeStruct((n, d), F32), jax.ShapeDtypeStruct((n, d), BF16),
                   jax.ShapeDtypeStruct((N_EXPERTS, n), F32)),
        compiler_params=_cparams(("arbitrary",)),
        name="merge",
    )(o_attn.reshape(n, ATTN_WIDTH), y_ssm.reshape(n, SSM_WIDTH), x.reshape(n, d), mod3, mod3, mod3,
      attn_norm_w, wo, norm2_w, wrh, wrl)


EXPERTS_PER_GROUP = N_EXPERTS // N_EXPERT_GROUPS


def _first_max(x, ids, axes, n_ids):
    mx = jnp.max(x, axis=axes, keepdims=True)
    return ids == jnp.min(jnp.where(x == mx, ids, n_ids), axis=axes, keepdims=True), mx


def _route_kernel(lg_ref, eb_ref, tri_ref, w_ref, pos_ref, cnt_ref):
    lg = lg_ref[...]
    tn = lg.shape[2]
    scores = jax.nn.sigmoid(lg)
    biased = scores + eb_ref[...]
    sub = lax.broadcasted_iota(jnp.int32, lg.shape, 1)
    grp = lax.broadcasted_iota(jnp.int32, (N_EXPERT_GROUPS, 1, tn), 0)
    eid = lax.broadcasted_iota(jnp.int32, lg.shape, 0) * EXPERTS_PER_GROUP + sub
    hit, m1 = _first_max(biased, sub, 1, EXPERTS_PER_GROUP)
    m2 = jnp.max(jnp.where(hit, -jnp.inf, biased), axis=1, keepdims=True)
    gs = m1 + m2
    keep = jnp.zeros(gs.shape, jnp.bool_)
    for _ in range(TOPK_GROUPS):
        hit, _m = _first_max(gs, grp, 0, N_EXPERT_GROUPS)
        keep = keep | hit
        gs = jnp.where(hit, -jnp.inf, gs)
    x = jnp.where(keep, biased, NEG)
    sel = jnp.zeros(lg.shape, jnp.bool_)
    for _ in range(TOP_K):
        hit, _m = _first_max(x, eid, (0, 1), N_EXPERTS)
        sel = sel | hit
        x = jnp.where(hit, -jnp.inf, x)
    w = jnp.where(sel, scores, 0.0)
    w = w / jnp.sum(w, axis=(0, 1), keepdims=True) * ROUTED_SCALE
    w_ref[...] = w
    selb = sel.astype(BF16).reshape(N_EXPERTS, tn)
    pos = _dot(selb, tri_ref[...])
    pos_ref[...] = jnp.where(sel, pos.reshape(lg.shape), -1.0)
    cnt = jnp.sum(sel.astype(F32), axis=2, keepdims=True)
    cnt_ref[0] = jnp.broadcast_to(cnt, cnt_ref.shape[1:]).astype(jnp.int32)


def route(logits_t, e_bias, tn):
    n = logits_t.shape[1]
    lg3 = logits_t.reshape(N_EXPERT_GROUPS, EXPERTS_PER_GROUP, n)
    eb = e_bias.astype(F32).reshape(N_EXPERT_GROUPS, EXPERTS_PER_GROUP, 1)
    tri = jnp.asarray(np.triu(np.ones((tn, tn), np.float32), 1), BF16)
    blk = pl.BlockSpec((N_EXPERT_GROUPS, EXPERTS_PER_GROUP, tn), lambda i: (0, 0, i))
    w, pos, cnt = pl.pallas_call(
        _route_kernel,
        grid=(n // tn,),
        in_specs=[blk, pl.BlockSpec(eb.shape, lambda i: (0, 0, 0)), pl.BlockSpec((tn, tn), lambda i: (0, 0))],
        out_specs=(blk, blk, pl.BlockSpec((1, N_EXPERT_GROUPS, EXPERTS_PER_GROUP, LANES), lambda i: (i, 0, 0, 0))),
        out_shape=(jax.ShapeDtypeStruct(lg3.shape, F32), jax.ShapeDtypeStruct(lg3.shape, F32),
                   jax.ShapeDtypeStruct((n // tn, N_EXPERT_GROUPS, EXPERTS_PER_GROUP, LANES), jnp.int32)),
        compiler_params=_cparams(("arbitrary",)),
        name="route",
    )(lg3, eb, tri)
    return w.reshape(N_EXPERTS, n), pos.reshape(N_EXPERTS, n), cnt[..., 0].reshape(n // tn, N_EXPERTS)


MOE_ROWS = 128


def _swiglu(xb, wgu, wd, width):
    gu = _dot(xb, wgu)
    act = _silu(gu[:, :width]) * gu[:, width:]
    return _dot(act.astype(BF16), wd)


def _moe_kernel(cnt_ref, h2_ref, w_ref, pos_ref, x1_ref, g2_ref, wgu_ref, wd_ref, sgu_ref, sd_ref, o_ref, acc_ref):
    i = pl.program_id(0)
    e = pl.program_id(1)
    tm = h2_ref.shape[0]

    @pl.when(e == 0)
    def _():
        acc_ref[...] = _swiglu(h2_ref[...], sgu_ref[...], sd_ref[...], D_SHARED)

    cnt = cnt_ref[i * N_EXPERTS + e]
    pos = pos_ref[pl.ds(e, 1), :]
    wrow = w_ref[pl.ds(e, 1), :]
    slot = lax.broadcasted_iota(jnp.int32, (MOE_ROWS, tm), 0).astype(F32)

    def step(j, carry):
        hit = pos == slot + (j * MOE_ROWS).astype(F32)
        g = hit.astype(BF16)
        xg = _dot(g, h2_ref[...]).astype(BF16)
        out = _swiglu(xg, wgu_ref[0], wd_ref[0], D_EXPERT)
        out = out * jnp.sum(jnp.where(hit, wrow, 0.0), axis=1, keepdims=True)
        oh, ol = _split2(out)
        dn = (((0,), (0,)), ((), ()))
        acc_ref[...] += (lax.dot_general(g, oh, dn, preferred_element_type=F32)
                         + lax.dot_general(g, ol, dn, preferred_element_type=F32))
        return carry

    lax.fori_loop(0, (cnt + MOE_ROWS - 1) // MOE_ROWS, step, 0)

    @pl.when(e == N_EXPERTS - 1)
    def _():
        o_ref[...] = x1_ref[...] + _mod(g2_ref) * acc_ref[...]


def moe(h2, w_t, pos_t, counts, x1, mod3, mod_row0, t_per_b, wgu, wd, sgu, sd, tm):
    n, d = h2.shape
    tiles_per_b = t_per_b // tm
    grid_spec = pltpu.PrefetchScalarGridSpec(
        num_scalar_prefetch=1,
        grid=(n // tm, N_EXPERTS),
        in_specs=[
            pl.BlockSpec((tm, d), lambda i, e, c: (i, 0)),
            pl.BlockSpec((N_EXPERTS, tm), lambda i, e, c: (0, i)),
            pl.BlockSpec((N_EXPERTS, tm), lambda i, e, c: (0, i)),
            pl.BlockSpec((tm, d), lambda i, e, c: (i, 0)),
            _mod_spec(mod3, 5, tm, tiles_per_b, mod_row0),
            pl.BlockSpec((1, d, 2 * D_EXPERT), lambda i, e, c: (e, 0, 0)),
            pl.BlockSpec((1, D_EXPERT, d), lambda i, e, c: (e, 0, 0)),
            pl.BlockSpec(sgu.shape, lambda i, e, c: (0, 0)),
            pl.BlockSpec(sd.shape, lambda i, e, c: (0, 0)),
        ],
        out_specs=pl.BlockSpec((tm, d), lambda i, e, c: (i, 0)),
        scratch_shapes=[pltpu.VMEM((tm, d), F32)],
    )
    return pl.pallas_call(
        _moe_kernel,
        grid_spec=grid_spec,
        out_shape=jax.ShapeDtypeStruct((n, d), F32),
        compiler_params=_cparams(("arbitrary", "arbitrary")),
        name="moe",
    )(counts.reshape(-1), h2, w_t, pos_t, x1, mod3, wgu, wd, sgu, sd)


GATHER_PAGES = 8


def _gather_kernel(pt_ref, *refs):
    pages, new_ref, o_ref = refs[:GATHER_PAGES], refs[GATHER_PAGES], refs[GATHER_PAGES + 1]
    step = pl.program_id(1)
    last = pl.num_programs(1) - 1

    @pl.when(step < last)
    def _():
        for k in range(GATHER_PAGES):
            o_ref[0, k * PAGE_SIZE:(k + 1) * PAGE_SIZE, :] = pages[k][0].astype(o_ref.dtype)

    @pl.when(step == last)
    def _():
        new = new_ref[0]
        pad = jnp.zeros((o_ref.shape[1] - new.shape[0], new.shape[1]), F32)
        o_ref[0] = jnp.concatenate([new, pad], axis=0).astype(o_ref.dtype)


def gather_pages(cache, page_table, new_rows):
    b, n_pages = page_table.shape
    width = cache.shape[2]
    steps = n_pages // GATHER_PAGES
    rows = GATHER_PAGES * PAGE_SIZE

    def page_spec(k):
        def idx(i, s, pt):
            p = jnp.minimum(s, steps - 1) * GATHER_PAGES + k
            return (pt[i * n_pages + p], 0, 0)
        return pl.BlockSpec((1, PAGE_SIZE, width), idx)

    grid_spec = pltpu.PrefetchScalarGridSpec(
        num_scalar_prefetch=1,
        grid=(b, steps + 1),
        in_specs=[page_spec(k) for k in range(GATHER_PAGES)]
        + [pl.BlockSpec((1,) + new_rows.shape[1:], lambda i, s, pt: (i, 0, 0))],
        out_specs=pl.BlockSpec((1, rows, width), lambda i, s, pt: (i, s, 0)),
    )
    return pl.pallas_call(
        _gather_kernel,
        grid_spec=grid_spec,
        out_shape=jax.ShapeDtypeStruct((b, (steps + 1) * rows, width), BF16),
        compiler_params=_cparams(("arbitrary", "arbitrary")),
        name="gather_pages",
    )(page_table.reshape(-1), *([cache] * GATHER_PAGES), new_rows)


def _attention(qp, kvb, winb, misc, cmp_w, q_off, win_pos0, tq):
    t = qp.shape[1]
    cur_lo, cur_hi = q_off // SEL_BLOCK, (q_off + t - 1) // SEL_BLOCK
    assert cur_hi < N_SEL_LANES or (cur_lo == cur_hi == N_SEL_LANES), (q_off, t)
    n_pick = N_SEL - (1 if cur_hi >= N_SEL_LANES else 0)
    kcv = compress(kvb, *cmp_w)
    o_cmp, mneg = cmp_select(qp, kcv, q_off, n_pick, tq)
    return sel_win_attention(qp, mneg, kvb, winb, o_cmp, misc, q_off, win_pos0, tq)


def kernel(x_prompt, x_sample, cache_kv, cache_win, state_ssm, state_conv, page_table, c_prompt, c_sample, w_ada, b_ada, norm1_w, norm2_w, w_in, q_norm_w, k_norm_w, cmp_pe, cmp_w1, cmp_w2, attn_out_norm_w, conv_w, conv_b, dt_bias, a_log, d_skip, ssm_norm_w, w_out, w_router, e_bias, w_exp_gu, w_exp_down, w_sh_gu, w_sh_down):
    xp, xq = x_prompt, x_sample
    bp, tp, d = xp.shape
    bq, tq, _ = xq.shape
    depth = w_ada.shape[0]
    past_len = page_table.shape[1] * PAGE_SIZE
    nq = bq * tq
    assert tp % 512 == 0 and tp >= WINDOW and nq % 8 == 0 and tq <= 16
    tq_pad = 16
    pos_p = jnp.arange(tp, dtype=jnp.int32)
    pos_q = jnp.tile(past_len + jnp.arange(tq, dtype=jnp.int32), bq)
    c_all = jnp.concatenate([c_prompt, c_sample], axis=0)
    c_all = jnp.pad(c_all, ((0, -c_all.shape[0] % 8), (0, 0)))
    outs = [[] for _ in range(8)]
    for l in range(depth):
        mod = adaln_all(c_all, w_ada[l], b_ada[l])
        mod_p = mod.reshape(mod.shape[0], 1, 6 * d)
        mod_q = jnp.repeat(mod[bp:bp + bq], tq, axis=0)
        wp = _prep_w_in(w_in[l])
        cmp_w = _prep_compress(cmp_pe[l], cmp_w1[l], cmp_w2[l])
        wo = w_out[l].astype(BF16)
        wgu, wd = w_exp_gu[l].astype(BF16), w_exp_down[l].astype(BF16)
        sgu, sd = w_sh_gu[l].astype(BF16), w_sh_down[l].astype(BF16)
        ssm_w = (conv_w[l], conv_b[l], dt_bias[l], a_log[l], d_skip[l], ssm_norm_w[l])
        n1w, n2w, anw = norm1_w[l:l + 1], norm2_w[l:l + 1], attn_out_norm_w[l:l + 1]

        qp, kv, kvb, win, winb, z, xbc, misc = inproj(xp, mod_p, 0, n1w, wp, q_norm_w[l], k_norm_w[l], pos_p, 512)
        r3 = lambda a: a.reshape(bp, tp, a.shape[-1])
        o_attn = _attention(r3(qp), r3(kvb), r3(winb), r3(misc), cmp_w, 0, 0, 128)
        y_ssm, h_new, conv_new = ssd(r3(xbc), r3(z), r3(misc), jnp.zeros((bp, CONV_WIDTH - 1, CONV_DIM), F32),
                                     jnp.zeros((bp, SSM_HEADS, SSM_HEAD_DIM, SSM_STATE), F32), *ssm_w)
        x1, h2, lg = merge(o_attn, y_ssm, xp, mod_p, 0, anw, wo, n2w, w_router[l], 512)
        w_t, pos_t, cnt = route(lg, e_bias[l], 512)
        xp = moe(h2, w_t, pos_t, cnt, x1, mod_p, 0, tp, wgu, wd, sgu, sd, 512).reshape(bp, tp, d)
        outs[0].append(kv.reshape(bp, tp, 4, N_KV_HEADS, HEAD_DIM))
        outs[1].append(win.reshape(bp, tp, 2, N_KV_HEADS, HEAD_DIM)[:, tp - WINDOW:])
        outs[2].append(h_new)
        outs[3].append(conv_new)

        xq1 = xq.reshape(1, nq, d)
        qp, kv, kvb, win, winb, z, xbc, misc = inproj(xq1, mod_q, 0, n1w, wp, q_norm_w[l], k_norm_w[l], pos_q, nq)
        rq = lambda a: a.reshape(bq, tq, a.shape[-1])
        padq = lambda a: jnp.pad(rq(a), ((0, 0), (0, tq_pad - tq), (0, 0)))
        past = gather_pages(cache_kv[l].reshape(cache_kv.shape[1], PAGE_SIZE, 4 * KV_WIDTH), page_table, rq(kv))
        win_all = jnp.concatenate([cache_win[l].reshape(bq, WINDOW, 2 * KV_WIDTH).astype(BF16), rq(winb),
                                   jnp.zeros((bq, WIN_CHUNK - tq, 2 * KV_WIDTH), BF16)], axis=1)
        o_attn = _attention(padq(qp), past, win_all, padq(misc), cmp_w, past_len, past_len - WINDOW, tq_pad)[:, :tq]
        y_ssm, h_new, conv_new = ssd(rq(xbc), rq(z), rq(misc), state_conv[l], state_ssm[l], *ssm_w)
        x1, h2, lg = merge(o_attn.reshape(1, nq, ATTN_WIDTH), y_ssm.reshape(1, nq, SSM_WIDTH), xq1, mod_q, 0,
                           anw, wo, n2w, w_router[l], nq)
        w_t, pos_t, cnt = route(lg, e_bias[l], nq)
        xq = moe(h2, w_t, pos_t, cnt, x1, mod_q, 0, nq, wgu, wd, sgu, sd, nq).reshape(bq, tq, d)
        win_rows = win.reshape(bq, tq, 2, N_KV_HEADS, HEAD_DIM)
        outs[4].append(kv.reshape(bq, tq, 4, N_KV_HEADS, HEAD_DIM))
        outs[5].append(jnp.concatenate([cache_win[l], win_rows.astype(cache_win.dtype)], axis=1)[:, tq:])
        outs[6].append(h_new)
        outs[7].append(conv_new)
    return (xp, xq) + tuple(jnp.stack(o) for o in outs)
```

```python
import functools
import math

import jax
import jax.numpy as jnp
import numpy as np
from jax import lax
from jax.experimental import pallas as pl
from jax.experimental.pallas import tpu as pltpu

D_MODEL = 1024
PAGE_SIZE = 128
HEAD_DIM = 64
N_Q_HEADS = 8
N_KV_HEADS = 2
GQA_GROUP = N_Q_HEADS // N_KV_HEADS
ATTN_WIDTH = N_Q_HEADS * HEAD_DIM
KV_WIDTH = N_KV_HEADS * HEAD_DIM
ROPE_DIM = HEAD_DIM // 4
ROPE_THETA = 500000.0
CMP_LEN = 32
CMP_STRIDE = 16
CMP_HIDDEN = 4 * HEAD_DIM
SEL_BLOCK = 64
N_SEL = 16
N_LOCAL = 2
WINDOW = 512
SSM_HEADS = 8
SSM_HEAD_DIM = 64
SSM_WIDTH = SSM_HEADS * SSM_HEAD_DIM
SSM_GROUPS = 2
SSM_STATE = 128
CONV_WIDTH = 4
CONV_DIM = SSM_WIDTH + 2 * SSM_GROUPS * SSM_STATE
SSD_CHUNK = 128
MIX_WIDTH = ATTN_WIDTH + SSM_WIDTH
N_EXPERTS = 64
N_EXPERT_GROUPS = 8
TOPK_GROUPS = 4
TOP_K = 8
D_EXPERT = 256
D_SHARED = 256
ROUTED_SCALE = 2.5
IN_SIZES = (ATTN_WIDTH, 6 * KV_WIDTH, 3 * N_Q_HEADS, SSM_WIDTH, CONV_DIM, SSM_HEADS)
N_IN = sum(IN_SIZES)
EPS = 1e-6
NEG = -1e30
BIG = 1e6

LANES = 128
VMEM_LIMIT = 56 * 1024 * 1024

BF16 = jnp.bfloat16
F32 = jnp.float32
LOG2E = math.log2(math.e)


def _cparams(sem):
    return pltpu.CompilerParams(dimension_semantics=sem, vmem_limit_bytes=VMEM_LIMIT)


def _silu(x):
    return x * jax.nn.sigmoid(x)


def _dot(a, b):
    return jnp.dot(a, b, preferred_element_type=F32)


def _dot_nt(a, b):
    return lax.dot_general(a, b, (((1,), (1,)), ((), ())), preferred_element_type=F32)


def _mod_spec(mod, col, tm, tiles_per_b, row0):
    if mod.ndim == 3:
        return pl.BlockSpec((1, 1, D_MODEL), lambda i, *_: (row0 + i // tiles_per_b, 0, col))
    return pl.BlockSpec((tm, D_MODEL), lambda i, *_: (i, col))


def _mod(ref):
    return ref[0] if len(ref.shape) == 3 else ref[...]


def _adaln_kernel(c_ref, w_ref, b_ref, o_ref):
    c = c_ref[...]
    a = _silu(c).astype(BF16)
    o_ref[...] = _dot(a, w_ref[...].astype(BF16)) + b_ref[...]


def adaln_all(c_all, w_ada, b_ada):
    rows = c_all.shape[0]
    n = w_ada.shape[1]
    tn = 1024
    return pl.pallas_call(
        _adaln_kernel,
        grid=(n // tn,),
        in_specs=[
            pl.BlockSpec((rows, D_MODEL), lambda j: (0, 0)),
            pl.BlockSpec((D_MODEL, tn), lambda j: (0, j)),
            pl.BlockSpec((1, tn), lambda j: (0, j)),
        ],
        out_specs=pl.BlockSpec((rows, tn), lambda j: (0, j)),
        out_shape=jax.ShapeDtypeStruct((rows, n), F32),
        compiler_params=_cparams(("arbitrary",)),
        name="adaln",
    )(c_all, w_ada, b_ada.reshape(1, n))


_C_Q = 0
_C_KV = _C_Q + ATTN_WIDTH
_C_Z = _C_KV + 6 * KV_WIDTH
_C_XBC = _C_Z + SSM_WIDTH
_C_MISC = _C_XBC + CONV_DIM
N_IN_PAD = _C_MISC + LANES
N_GATES = 3 * N_Q_HEADS


def _prep_w_in(w_in):
    s = np.cumsum((0,) + IN_SIZES)
    q, kv, g, z, xbc, dt = (w_in[:, int(s[i]):int(s[i + 1])] for i in range(6))
    pad = jnp.zeros((w_in.shape[0], LANES - N_GATES - SSM_HEADS), w_in.dtype)
    return jnp.concatenate([q, kv, z, xbc, dt, g, pad], axis=1).astype(BF16)


def _group_mean_matrix(width):
    i = np.arange(width)
    m = (i[:, None] // HEAD_DIM == i[None, :] // HEAD_DIM).astype(np.float32) / HEAD_DIM
    return jnp.asarray(m, BF16)


def _rope_tables(pos):
    half = ROPE_DIM // 2
    inv_freq = ROPE_THETA ** (-jnp.arange(half, dtype=F32) / half)
    ang = pos.astype(F32)[:, None] * inv_freq[None, :]
    cos, sin = jnp.cos(ang), jnp.sin(ang)
    t = pos.shape[0]
    one = jnp.ones((t, HEAD_DIM - ROPE_DIM), F32)
    zero = jnp.zeros((t, HEAD_DIM - ROPE_DIM), F32)
    zh = jnp.zeros((t, half), F32)
    c = jnp.concatenate([cos, cos, one], axis=1)
    s_up = jnp.concatenate([-sin, zh, zero], axis=1)
    s_dn = jnp.concatenate([zh, sin, zero], axis=1)
    rep = LANES // HEAD_DIM
    return jnp.tile(c, (1, rep)), jnp.tile(s_up, (1, rep)), jnp.tile(s_dn, (1, rep))


def _rope(x, c, s_up, s_dn):
    w = x.shape[1]
    half = ROPE_DIM // 2
    rep = w // LANES
    ct = jnp.concatenate([c] * rep, axis=1) if rep > 1 else c
    su = jnp.concatenate([s_up] * rep, axis=1) if rep > 1 else s_up
    sd = jnp.concatenate([s_dn] * rep, axis=1) if rep > 1 else s_dn
    up = pltpu.roll(x, w - half, axis=1)
    dn = pltpu.roll(x, half, axis=1)
    return x * ct + up * su + dn * sd


def _stride_block_store(stage_ref, cmpx_ref, n_rows):
    for j in range(CMP_STRIDE):
        for s in range(2):
            rows_j = stage_ref[s, pl.ds(j, n_rows // CMP_STRIDE, stride=CMP_STRIDE), :]
            c0 = (2 * j + s) * KV_WIDTH
            cmpx_ref[0, :, c0:c0 + KV_WIDTH] = rows_j.astype(BF16)


def _inproj_kernel(x_ref, shift_ref, scale_ref, nw_ref, w_ref, qw_ref, kw_ref, gq_ref, gk_ref,
                   c_ref, su_ref, sd_ref,
                   qp_ref, kvb_ref, win_ref, winb_ref, z_ref, xbc_ref, misc_ref, *rest, seq_layout):
    x = x_ref[...]
    ms = jnp.mean(x * x, axis=-1, keepdims=True)
    h = x * lax.rsqrt(ms + EPS) * nw_ref[...]
    h = h * (1.0 + _mod(scale_ref)) + _mod(shift_ref)
    hb = h.astype(BF16)
    c, su, sd = c_ref[...], su_ref[...], sd_ref[...]

    q = _dot(hb, w_ref[:, _C_Q:_C_Q + ATTN_WIDTH])
    qms = _dot((q * q).astype(BF16), gq_ref[...])
    q = q * lax.rsqrt(qms + EPS) * qw_ref[...]
    q = _rope(q, c, su, sd) * (HEAD_DIM ** -0.5 * LOG2E)
    lane = lax.broadcasted_iota(jnp.int32, q.shape, 1) % LANES
    lo = lane < HEAD_DIM
    q_up = pltpu.roll(q, ATTN_WIDTH - HEAD_DIM, axis=1)
    q_dn = pltpu.roll(q, HEAD_DIM, axis=1)
    zero = jnp.zeros_like(q)
    nat_lo = jnp.where(lo, q, zero)
    nat_hi = jnp.where(lo, zero, q)
    up_lo = jnp.where(lo, q_up, zero)
    dn_hi = jnp.where(lo, zero, q_dn)
    blocks = []
    for hd in range(N_Q_HEADS):
        pair = hd // 2
        sl = slice(pair * LANES, (pair + 1) * LANES)
        if hd < GQA_GROUP:
            blocks.append((nat_lo if hd % 2 == 0 else up_lo)[:, sl])
        else:
            blocks.append((dn_hi if hd % 2 == 0 else nat_hi)[:, sl])
    qp_ref[...] = jnp.concatenate(blocks, axis=1).astype(BF16)

    kv = _dot(hb, w_ref[:, _C_KV:_C_KV + 6 * KV_WIDTH])
    outs = []
    for br in range(3):
        k = kv[:, br * 2 * KV_WIDTH:br * 2 * KV_WIDTH + KV_WIDTH]
        v = kv[:, br * 2 * KV_WIDTH + KV_WIDTH:(br + 1) * 2 * KV_WIDTH]
        kms = _dot((k * k).astype(BF16), gk_ref[...])
        k = k * lax.rsqrt(kms + EPS) * kw_ref[:, br * KV_WIDTH:(br + 1) * KV_WIDTH]
        k = _rope(k, c, su, sd)
        outs += [k, v]
    kvrows = jnp.concatenate(outs[:4], axis=1)
    winrows = jnp.concatenate(outs[4:], axis=1)
    kvb_ref[...] = kvrows.astype(BF16)
    win_ref[...] = winrows
    winb_ref[...] = winrows.astype(BF16)
    if seq_layout:
        kvt_ref, cmpx_ref, stage_ref = rest
        tm = kvrows.shape[0]
        for r in range(4):
            kvt_ref[0, r] = jnp.transpose(kvrows[:, r * KV_WIDTH:(r + 1) * KV_WIDTH])
        for s in range(2):
            stage_ref[s] = kvrows[:, s * KV_WIDTH:(s + 1) * KV_WIDTH]
        _stride_block_store(stage_ref, cmpx_ref, tm)
    else:
        rest[0][...] = kvrows

    z_ref[...] = _dot(hb, w_ref[:, _C_Z:_C_Z + SSM_WIDTH])
    xbc_ref[...] = _dot(hb, w_ref[:, _C_XBC:_C_XBC + CONV_DIM])
    misc_ref[...] = _dot(hb, w_ref[:, _C_MISC:_C_MISC + LANES])


def inproj(x, mod3, mod_row0, norm_w, wp, q_norm_w, k_norm_w, pos, tm, seq_layout):
    b, t, d = x.shape
    n = b * t
    tiles_per_b = t // tm
    xf = x.reshape(n, d)
    c, su, sd = _rope_tables(pos)
    qw = jnp.tile(q_norm_w, N_Q_HEADS).reshape(1, ATTN_WIDTH)
    kw = jnp.concatenate([jnp.tile(k_norm_w[i], N_KV_HEADS) for i in range(3)]).reshape(1, 3 * KV_WIDTH)
    gq = _group_mean_matrix(ATTN_WIDTH)
    gk = _group_mean_matrix(KV_WIDTH)

    def mod_spec(col):
        return _mod_spec(mod3, col, tm, tiles_per_b, mod_row0)

    def tok(wd):
        return pl.BlockSpec((tm, wd), lambda i: (i, 0))

    def full(a):
        return pl.BlockSpec(a.shape, lambda i: (0,) * a.ndim)

    rope_spec = pl.BlockSpec((tm, LANES), lambda i: (i % tiles_per_b, 0))
    out_shape = [
        jax.ShapeDtypeStruct((n, N_Q_HEADS * LANES), BF16),
        jax.ShapeDtypeStruct((n, 4 * KV_WIDTH), BF16),
        jax.ShapeDtypeStruct((n, 2 * KV_WIDTH), F32),
        jax.ShapeDtypeStruct((n, 2 * KV_WIDTH), BF16),
        jax.ShapeDtypeStruct((n, SSM_WIDTH), F32),
        jax.ShapeDtypeStruct((n, CONV_DIM), F32),
        jax.ShapeDtypeStruct((n, LANES), F32),
    ]
    out_specs = [tok(s.shape[1]) for s in out_shape]
    scratch = []
    if seq_layout:
        out_shape += [jax.ShapeDtypeStruct((b, 4, KV_WIDTH, t), F32),
                      jax.ShapeDtypeStruct((b, t // CMP_STRIDE, CMP_STRIDE * 2 * KV_WIDTH), BF16)]
        out_specs += [pl.BlockSpec((1, 4, KV_WIDTH, tm), lambda i: (i // tiles_per_b, 0, 0, i % tiles_per_b)),
                      pl.BlockSpec((1, tm // CMP_STRIDE, CMP_STRIDE * 2 * KV_WIDTH),
                                   lambda i: (i // tiles_per_b, i % tiles_per_b, 0))]
        scratch = [pltpu.VMEM((2, tm, KV_WIDTH), F32)]
    else:
        out_shape += [jax.ShapeDtypeStruct((n, 4 * KV_WIDTH), F32)]
        out_specs += [tok(4 * KV_WIDTH)]
    return pl.pallas_call(
        functools.partial(_inproj_kernel, seq_layout=seq_layout),
        grid=(n // tm,),
        in_specs=[tok(d), mod_spec(0), mod_spec(1), full(norm_w), full(wp), full(qw), full(kw), full(gq), full(gk),
                  rope_spec, rope_spec, rope_spec],
        out_specs=tuple(out_specs),
        out_shape=tuple(out_shape),
        scratch_shapes=scratch,
        compiler_params=_cparams(("arbitrary",)),
        name="inproj",
    )(xf, mod3, mod3, norm_w, wp, qw, kw, gq, gk, c, su, sd)


def _prep_compress(cmp_pe, cmp_w1, cmp_w2):
    half = CMP_LEN // 2
    eye = jnp.eye(N_KV_HEADS, dtype=F32)
    w1 = cmp_w1.reshape(2, CMP_LEN, HEAD_DIM, CMP_HIDDEN)
    w1s = []
    for part in (w1[:, :half], w1[:, half:]):
        w1s.append(jnp.einsum("pjdo,hg->pjhdgo", part, eye).reshape(2, half * KV_WIDTH, N_KV_HEADS * CMP_HIDDEN))
    w1p = jnp.concatenate(w1s, axis=2).astype(BF16)
    pe = cmp_pe.reshape(2, 2, half, 1, HEAD_DIM)
    pep = jnp.broadcast_to(pe, (2, 2, half, N_KV_HEADS, HEAD_DIM)).reshape(2, 2, half * KV_WIDTH)
    w2p = jnp.einsum("poe,hg->phoge", cmp_w2, eye).reshape(2, N_KV_HEADS * CMP_HIDDEN, KV_WIDTH).astype(BF16)
    return w1p, pep, w2p


def _compress_kernel(x_ref, w1_ref, pe_ref, w2_ref, o_ref, *, row_w):
    part = pl.program_id(1)
    nb = x_ref.shape[1]
    half = CMP_LEN // 2
    hid = N_KV_HEADS * CMP_HIDDEN
    cols = []
    for j in range(half):
        a = x_ref[0, :, j * row_w:j * row_w + KV_WIDTH]
        b = x_ref[0, :, j * row_w + KV_WIDTH:j * row_w + 2 * KV_WIDTH]
        cols.append(jnp.where(part == 0, a, b))
    x = jnp.concatenate(cols, axis=1).astype(F32)
    pe = pe_ref[0]
    u = _dot((x + pe[0:1]).astype(BF16), w1_ref[0, :, :hid])
    v = _dot((x + pe[1:2]).astype(BF16), w1_ref[0, :, hid:])
    h1 = u + pltpu.roll(v, nb - 1, axis=0)
    out = _dot(_silu(h1).astype(BF16), w2_ref[0])
    row = lax.broadcasted_iota(jnp.int32, out.shape, 0)
    o_ref[0, 0] = jnp.where(row < nb - 1, out, 0.0).astype(o_ref.dtype)


def compress(x, w1p, pep, w2p):
    b, nb, width = x.shape
    row_w = width // CMP_STRIDE
    return pl.pallas_call(
        functools.partial(_compress_kernel, row_w=row_w),
        grid=(b, 2),
        in_specs=[
            pl.BlockSpec((1, nb, CMP_STRIDE * row_w), lambda i, p: (i, 0, 0)),
            pl.BlockSpec((1,) + w1p.shape[1:], lambda i, p: (p, 0, 0)),
            pl.BlockSpec((1,) + pep.shape[1:], lambda i, p: (p, 0, 0)),
            pl.BlockSpec((1,) + w2p.shape[1:], lambda i, p: (p, 0, 0)),
        ],
        out_specs=pl.BlockSpec((1, 1, nb, KV_WIDTH), lambda i, p: (i, p, 0, 0)),
        out_shape=jax.ShapeDtypeStruct((b, 2, nb, KV_WIDTH), BF16),
        compiler_params=_cparams(("arbitrary", "arbitrary")),
        name="compress",
    )(x, w1p, pep, w2p)


N_SEL_LANES = LANES


def _cover_matrix(nb):
    c = np.arange(nb)[:, None]
    j = np.arange(N_SEL_LANES)[None, :]
    start = c * CMP_STRIDE
    m = (start < (j + 1) * SEL_BLOCK) & (start + CMP_LEN > j * SEL_BLOCK)
    return jnp.asarray(m.astype(np.float32), BF16)


def _place_heads(res, kv):
    lane = lax.broadcasted_iota(jnp.int32, res[0].shape, 1)
    lo = lane < HEAD_DIM
    blocks = []
    for pair in range(GQA_GROUP // 2):
        a, b = res[2 * pair], res[2 * pair + 1]
        if kv == 0:
            blocks.append(jnp.where(lo, a, pltpu.roll(b, HEAD_DIM, axis=1)))
        else:
            blocks.append(jnp.where(lo, pltpu.roll(a, HEAD_DIM, axis=1), b))
    return jnp.concatenate(blocks, axis=1)


def _group_rows(q_ref, kv):
    heads = range(kv * GQA_GROUP, (kv + 1) * GQA_GROUP)
    return jnp.concatenate([q_ref[0, :, hd * LANES:(hd + 1) * LANES] for hd in heads], axis=0)


def _heads_from_transposed(out_t, tq, kv):
    out = jnp.transpose(out_t)
    return _place_heads([out[g * tq:(g + 1) * tq] for g in range(GQA_GROUP)], kv)


def _cmp_select_kernel(q_ref, kc_ref, vc_ref, covt_ref, o_ref, m_ref, *, q_off, n_pick):
    tq = q_ref.shape[1]
    rows = GQA_GROUP * tq
    nb = kc_ref.shape[2]
    wl = max(tq, LANES)
    assert tq % LANES == 0 or rows == LANES
    t0 = q_off + pl.program_id(1) * tq
    kc = kc_ref[0, 0]
    vc = vc_ref[0, 0]
    qpos = t0 + lax.broadcasted_iota(jnp.int32, (nb, rows), 1) % tq
    cend = lax.broadcasted_iota(jnp.int32, (nb, rows), 0) * CMP_STRIDE + (CMP_LEN - 1)
    valid = cend <= qpos
    blk = lax.broadcasted_iota(jnp.int32, (N_SEL_LANES, wl), 0)
    cur = (t0 + lax.broadcasted_iota(jnp.int32, (N_SEL_LANES, wl), 1) % tq) // SEL_BLOCK
    forced = (blk == 0) | ((blk <= cur) & (blk > cur - N_LOCAL))
    o_groups = []
    for kv in range(N_KV_HEADS):
        s = _dot_nt(kc, _group_rows(q_ref, kv))
        s = jnp.where(valid, s, NEG)
        e = jnp.exp2(s - jnp.max(s, axis=0, keepdims=True))
        p = e / jnp.sum(e, axis=0, keepdims=True)
        p = jnp.where(valid, p, 0.0)
        o_t = lax.dot_general(vc, p.astype(BF16), (((0,), (0,)), ((), ())), preferred_element_type=F32)
        o_groups.append(_heads_from_transposed(o_t, tq, kv))
        if tq % LANES == 0:
            psum = sum(p[:, g * tq:(g + 1) * tq] for g in range(GQA_GROUP))
        else:
            psum = p + sum(pltpu.roll(p, g * tq, axis=1) for g in range(1, GQA_GROUP))
        hi, lo = _split2(psum)
        imp = _dot(covt_ref[...], hi) + _dot(covt_ref[...], lo)
        x = jnp.where(forced, BIG, jnp.where(blk > cur, -BIG, imp))
        sel = jnp.zeros(x.shape, jnp.bool_)
        for _ in range(n_pick):
            mx = jnp.max(x, axis=0, keepdims=True)
            idx = jnp.min(jnp.where(x == mx, blk, N_SEL_LANES), axis=0, keepdims=True)
            hit = blk == idx
            sel = sel | hit
            x = jnp.where(hit, -jnp.inf, x)
        mneg = jnp.transpose(jnp.where(sel, 0.0, NEG))
        m_ref[0, kv] = mneg[:tq].astype(m_ref.dtype)
    o_ref[0] = jnp.concatenate(o_groups, axis=1)


def cmp_select(qp, kcv, q_off, n_pick, tq):
    b, t, _ = qp.shape
    nb = kcv.shape[2]
    cover = jnp.transpose(_cover_matrix(nb))
    return pl.pallas_call(
        functools.partial(_cmp_select_kernel, q_off=q_off, n_pick=n_pick),
        grid=(b, t // tq),
        in_specs=[
            pl.BlockSpec((1, tq, N_Q_HEADS * LANES), lambda i, j: (i, j, 0)),
            pl.BlockSpec((1, 1, nb, KV_WIDTH), lambda i, j: (i, 0, 0, 0)),
            pl.BlockSpec((1, 1, nb, KV_WIDTH), lambda i, j: (i, 1, 0, 0)),
            pl.BlockSpec((N_SEL_LANES, nb), lambda i, j: (0, 0)),
        ],
        out_specs=(
            pl.BlockSpec((1, tq, ATTN_WIDTH), lambda i, j: (i, j, 0)),
            pl.BlockSpec((1, N_KV_HEADS, tq, N_SEL_LANES), lambda i, j: (i, 0, j, 0)),
        ),
        out_shape=(
            jax.ShapeDtypeStruct((b, t, ATTN_WIDTH), F32),
            jax.ShapeDtypeStruct((b, N_KV_HEADS, t, N_SEL_LANES), BF16),
        ),
        compiler_params=_cparams(("arbitrary", "arbitrary")),
        name="cmp_select",
    )(qp, kcv, kcv, cover)


SEL_CHUNK = 512
WIN_CHUNK = 256


def _block_onehot(s):
    key = np.arange(s)[:, None]
    j = np.arange(N_SEL_LANES)[None, :]
    return jnp.asarray((key // SEL_BLOCK == j).astype(np.float32), BF16)


def _gate_expand():
    m = np.zeros((3, LANES, ATTN_WIDTH), np.float32)
    for br in range(3):
        for hd in range(N_Q_HEADS):
            m[br, SSM_HEADS + 3 * hd + br, hd * HEAD_DIM:(hd + 1) * HEAD_DIM] = 1.0
    return jnp.asarray(m, BF16)


def _flash_update(ss, v, m_ref, l_ref, acc_ref):
    stage = []
    for k, s in enumerate(ss):
        m_old = m_ref[k]
        m_new = jnp.maximum(m_old, jnp.max(s, axis=0, keepdims=True))
        alpha = jnp.exp2(m_old - m_new)
        p = jnp.exp2(s - m_new)
        l_ref[k] = alpha * l_ref[k] + jnp.sum(p, axis=0, keepdims=True)
        m_ref[k] = m_new
        stage.append((alpha, p.astype(BF16)))
    for k, (alpha, p) in enumerate(stage):
        pv = lax.dot_general(v, p, (((0,), (0,)), ((), ())), preferred_element_type=F32)
        acc_ref[k] = alpha * acc_ref[k] + pv


def _sel_win_kernel(q_ref, mneg_ref, ksel_ref, vsel_ref, et_ref, kwin_ref, vwin_ref, ocmp_ref, misc_ref, eg_ref,
                    o_ref, lhs_ref, m_ref, l_ref, acc_ref, *, q_off, win_pos0):
    tq = q_ref.shape[1]
    rows = GQA_GROUP * tq
    t0 = q_off + pl.program_id(1) * tq
    n_sel = lax.shift_right_logical(t0 + tq - 1, int(math.log2(SEL_CHUNK))) + 1
    w_lo = jnp.maximum(t0 - (WINDOW - 1) - win_pos0, 0) // WIN_CHUNK
    w_hi = (t0 + tq - 1 - win_pos0) // WIN_CHUNK + 1

    def qrow(n_keys):
        return lax.broadcasted_iota(jnp.int32, (n_keys, rows), 1) % tq + t0

    def init():
        m_ref[...] = jnp.full(m_ref.shape, NEG, F32)
        l_ref[...] = jnp.zeros(l_ref.shape, F32)
        acc_ref[...] = jnp.zeros(acc_ref.shape, F32)

    def finish():
        return jnp.concatenate([_heads_from_transposed(acc_ref[kv] / l_ref[kv], tq, kv)
                                for kv in range(N_KV_HEADS)], axis=1)

    for kv in range(N_KV_HEADS):
        for g in range(GQA_GROUP):
            hd = kv * GQA_GROUP + g
            lhs_ref[kv, g * tq:(g + 1) * tq, :LANES] = q_ref[0, :, hd * LANES:(hd + 1) * LANES]
            lhs_ref[kv, g * tq:(g + 1) * tq, LANES:] = mneg_ref[0, kv]

    init()

    def sel_step(c, carry, causal):
        r0 = pl.multiple_of(c * SEL_CHUNK, SEL_CHUNK)
        rhs = jnp.concatenate([ksel_ref[0, pl.ds(r0, SEL_CHUNK), :], et_ref[pl.ds(r0, SEL_CHUNK), :]], axis=1)
        v = vsel_ref[0, pl.ds(r0, SEL_CHUNK), :]
        if causal:
            ok = r0 + lax.broadcasted_iota(jnp.int32, (SEL_CHUNK, rows), 0) <= qrow(SEL_CHUNK)
        ss = [_dot_nt(rhs, lhs_ref[kv]) for kv in range(N_KV_HEADS)]
        if causal:
            ss = [jnp.where(ok, s, NEG) for s in ss]
        _flash_update(ss, v, m_ref, l_ref, acc_ref)
        return carry

    n_full = lax.shift_right_logical(t0 + 1, int(math.log2(SEL_CHUNK)))
    lax.fori_loop(0, n_full, functools.partial(sel_step, causal=False), 0)
    lax.fori_loop(n_full, n_sel, functools.partial(sel_step, causal=True), 0)
    o_sel = finish()

    init()

    def win_step(c, carry):
        r0 = pl.multiple_of(c * WIN_CHUNK, WIN_CHUNK)
        k = kwin_ref[0, pl.ds(r0, WIN_CHUNK), :]
        v = vwin_ref[0, pl.ds(r0, WIN_CHUNK), :]
        wpos = win_pos0 + r0 + lax.broadcasted_iota(jnp.int32, (WIN_CHUNK, rows), 0)
        qr = qrow(WIN_CHUNK)
        ok = (wpos <= qr) & (wpos > qr - WINDOW)
        ss = [jnp.where(ok, _dot_nt(k, lhs_ref[kv, :, :LANES]), NEG) for kv in range(N_KV_HEADS)]
        _flash_update(ss, v, m_ref, l_ref, acc_ref)
        return carry

    lax.fori_loop(w_lo, w_hi, win_step, 0)
    o_win = finish()

    gates = jax.nn.sigmoid(misc_ref[0])
    ghi = gates.astype(BF16)
    glo = (gates - ghi.astype(F32)).astype(BF16)
    branches = (ocmp_ref[0], o_sel, o_win)
    out = jnp.zeros(branches[0].shape, F32)
    for br in range(3):
        out = out + (_dot(ghi, eg_ref[br]) + _dot(glo, eg_ref[br])) * branches[br]
    o_ref[0] = out


def sel_win_attention(qp, mneg, kvb, sel_col, winb, o_cmp, misc, q_off, win_pos0, tq):
    b, t, _ = qp.shape
    s = kvb.shape[1]
    sw = winb.shape[1]
    et = _block_onehot(s)
    eg = _gate_expand()
    rows = GQA_GROUP * tq
    return pl.pallas_call(
        functools.partial(_sel_win_kernel, q_off=q_off, win_pos0=win_pos0),
        grid=(b, t // tq),
        in_specs=[
            pl.BlockSpec((1, tq, N_Q_HEADS * LANES), lambda i, j: (i, j, 0)),
            pl.BlockSpec((1, N_KV_HEADS, tq, N_SEL_LANES), lambda i, j: (i, 0, j, 0)),
            pl.BlockSpec((1, s, KV_WIDTH), lambda i, j: (i, 0, sel_col)),
            pl.BlockSpec((1, s, KV_WIDTH), lambda i, j: (i, 0, sel_col + 1)),
            pl.BlockSpec((s, N_SEL_LANES), lambda i, j: (0, 0)),
            pl.BlockSpec((1, sw, KV_WIDTH), lambda i, j: (i, 0, 0)),
            pl.BlockSpec((1, sw, KV_WIDTH), lambda i, j: (i, 0, 1)),
            pl.BlockSpec((1, tq, ATTN_WIDTH), lambda i, j: (i, j, 0)),
            pl.BlockSpec((1, tq, LANES), lambda i, j: (i, j, 0)),
            pl.BlockSpec((3, LANES, ATTN_WIDTH), lambda i, j: (0, 0, 0)),
        ],
        out_specs=pl.BlockSpec((1, tq, ATTN_WIDTH), lambda i, j: (i, j, 0)),
        out_shape=jax.ShapeDtypeStruct((b, t, ATTN_WIDTH), F32),
        scratch_shapes=[
            pltpu.VMEM((N_KV_HEADS, rows, 2 * LANES), BF16),
            pltpu.VMEM((N_KV_HEADS, 1, rows), F32),
            pltpu.VMEM((N_KV_HEADS, 1, rows), F32),
            pltpu.VMEM((N_KV_HEADS, LANES, rows), F32),
        ],
        compiler_params=_cparams(("arbitrary", "arbitrary")),
        name="sel_win_attention",
    )(qp, mneg, kvb, kvb, et, winb, winb, o_cmp, misc, eg)


CONV_PAD = 8
HEAD_PAIRS = SSM_HEADS // 2


def _split3(x):
    a = x.astype(BF16)
    r = x - a.astype(F32)
    b = r.astype(BF16)
    c = (r - b.astype(F32)).astype(BF16)
    return a, b, c


def _ssd_kernel(xbc_ref, z_ref, misc_ref, conv0_ref, h0_ref, cw_ref, cb_ref, dtb_ref, a_ref, dsk_ref, nw_ref,
                y_ref, hout_ref, cout_ref, xp_ref, h_ref, ms_ref, *, t_valid):
    ch = pl.program_id(1)
    L = SSD_CHUNK
    keep = CONV_WIDTH - 1

    @pl.when(ch == 0)
    def _():
        xp_ref[...] = jnp.zeros(xp_ref.shape, F32)
        xp_ref[CONV_PAD - keep:CONV_PAD, :] = conv0_ref[0]
        h_ref[...] = h0_ref[0]

    xp_ref[CONV_PAD:CONV_PAD + t_valid, :] = xbc_ref[0]
    conv = cb_ref[...]
    for j in range(CONV_WIDTH):
        conv = conv + cw_ref[j:j + 1, :] * xp_ref[CONV_PAD - keep + j:CONV_PAD - keep + j + L, :]
    last = xp_ref[CONV_PAD + t_valid - keep:CONV_PAD + t_valid, :]
    cout_ref[0] = last
    xp_ref[CONV_PAD - keep:CONV_PAD, :] = last
    xc = _silu(conv)

    row = lax.broadcasted_iota(jnp.int32, (L, LANES), 0)
    lane = lax.broadcasted_iota(jnp.int32, (L, LANES), 1)
    if t_valid == L:
        raw = misc_ref[0]
    else:
        ms_ref[...] = jnp.zeros(ms_ref.shape, F32)
        ms_ref[0:t_valid, :] = misc_ref[0]
        raw = ms_ref[...]
    v = raw + dtb_ref[...]
    dt = jnp.maximum(v, 0.0) + jnp.log(1.0 + jnp.exp(-jnp.abs(v)))
    dt = jnp.where((lane < SSM_HEADS) & (row < t_valid), dt, 0.0)
    da = dt * a_ref[...]
    tri = (lax.broadcasted_iota(jnp.int32, (L, L), 1) <= lax.broadcasted_iota(jnp.int32, (L, L), 0))
    trib = tri.astype(BF16)
    acum = sum(_dot(trib, part) for part in _split3(da))
    acum_t = jnp.transpose(acum)
    dt_t = jnp.transpose(dt)
    e_acum = jnp.exp(acum)
    e_last = jnp.exp(acum[L - 1:L, :])
    w_end = jnp.exp(acum[L - 1:L, :] - acum) * dt
    lo = lane < SSM_HEAD_DIM

    ys = []
    for pair in range(HEAD_PAIRS):
        grp = (2 * pair) // (SSM_HEADS // SSM_GROUPS)
        bg = xc[:, SSM_WIDTH + grp * SSM_STATE:SSM_WIDTH + (grp + 1) * SSM_STATE].astype(BF16)
        cg = xc[:, SSM_WIDTH + (SSM_GROUPS + grp) * SSM_STATE:SSM_WIDTH + (SSM_GROUPS + grp + 1) * SSM_STATE].astype(BF16)
        g = _dot_nt(cg, bg)
        xpair = xc[:, pair * LANES:(pair + 1) * LANES]
        y = jnp.zeros((L, LANES), F32)
        for sub in range(2):
            hd = 2 * pair + sub
            seg = acum[:, hd:hd + 1] - acum_t[hd:hd + 1, :]
            m = g * jnp.exp(jnp.where(tri, seg, NEG)) * dt_t[hd:hd + 1, :]
            xm = jnp.where(lo if sub == 0 else ~lo, xpair, 0.0)
            y = y + _dot(m.astype(BF16), xm.astype(BF16))
        col = lambda a: jnp.where(lo, a[:, 2 * pair:2 * pair + 1], a[:, 2 * pair + 1:2 * pair + 2])
        hp = h_ref[pair]
        y = y + _dot_nt(cg, hp.astype(BF16)) * col(e_acum)
        y = y + col(dsk_ref[...]) * xpair
        xw = (xpair * col(w_end)).astype(BF16)
        st = lax.dot_general(xw, bg, (((0,), (0,)), ((), ())), preferred_element_type=F32)
        prow = lax.broadcasted_iota(jnp.int32, (LANES, LANES), 0) < SSM_HEAD_DIM
        dec = jnp.where(prow, e_last[:, 2 * pair:2 * pair + 1], e_last[:, 2 * pair + 1:2 * pair + 2])
        h_ref[pair] = hp * dec + st
        ys.append(y)
    y = jnp.concatenate(ys, axis=1)
    if t_valid != L:
        y = y[:t_valid]
    y = y * _silu(z_ref[0])
    y = y * lax.rsqrt(jnp.mean(y * y, axis=-1, keepdims=True) + EPS) * nw_ref[...]
    y_ref[0] = y

    @pl.when(ch == pl.num_programs(1) - 1)
    def _():
        hout_ref[0] = h_ref[...]


def ssd(xbc, z, misc, conv0, h0, conv_w, conv_b, dt_bias, a_log, d_skip, norm_w):
    b, t, _ = xbc.shape
    L = SSD_CHUNK
    t_valid = L if t % L == 0 else t
    assert t_valid == L or t < L
    n_ch = max(t // L, 1)
    keep = CONV_WIDTH - 1
    pad8 = lambda v: jnp.pad(v.astype(F32), (0, LANES - SSM_HEADS)).reshape(1, LANES)
    dtb = pad8(dt_bias)
    a = pad8(-jnp.exp(a_log.astype(F32)))
    dsk = pad8(d_skip)
    h0p = h0.reshape(b, HEAD_PAIRS, 2 * SSM_HEAD_DIM, SSM_STATE)
    full = lambda arr: pl.BlockSpec(arr.shape, lambda i, c: (0,) * arr.ndim)
    tok = lambda wd: pl.BlockSpec((1, t_valid, wd), lambda i, c: (i, c, 0))
    y, hout, cout = pl.pallas_call(
        functools.partial(_ssd_kernel, t_valid=t_valid),
        grid=(b, n_ch),
        in_specs=[
            tok(CONV_DIM), tok(SSM_WIDTH), tok(LANES),
            pl.BlockSpec((1, keep, CONV_DIM), lambda i, c: (i, 0, 0)),
            pl.BlockSpec((1, HEAD_PAIRS, 2 * SSM_HEAD_DIM, SSM_STATE), lambda i, c: (i, 0, 0, 0)),
            full(conv_w), pl.BlockSpec((1, CONV_DIM), lambda i, c: (0, 0)),
            full(dtb), full(a), full(dsk), pl.BlockSpec((1, SSM_WIDTH), lambda i, c: (0, 0)),
        ],
        out_specs=(
            tok(SSM_WIDTH),
            pl.BlockSpec((1, HEAD_PAIRS, 2 * SSM_HEAD_DIM, SSM_STATE), lambda i, c: (i, 0, 0, 0)),
            pl.BlockSpec((1, keep, CONV_DIM), lambda i, c: (i, 0, 0)),
        ),
        out_shape=(
            jax.ShapeDtypeStruct((b, t, SSM_WIDTH), F32),
            jax.ShapeDtypeStruct((b, HEAD_PAIRS, 2 * SSM_HEAD_DIM, SSM_STATE), F32),
            jax.ShapeDtypeStruct((b, keep, CONV_DIM), F32),
        ),
        scratch_shapes=[
            pltpu.VMEM((CONV_PAD + L, CONV_DIM), F32),
            pltpu.VMEM((HEAD_PAIRS, 2 * SSM_HEAD_DIM, SSM_STATE), F32),
            pltpu.VMEM((L, LANES), F32),
        ],
        compiler_params=_cparams(("arbitrary", "arbitrary")),
        name="ssd",
    )(xbc, z, misc, conv0, h0p, conv_w, conv_b.reshape(1, CONV_DIM), dtb, a, dsk, norm_w.reshape(1, SSM_WIDTH))
    return y, hout.reshape(b, SSM_HEADS, SSM_HEAD_DIM, SSM_STATE), cout


def _split2(x):
    hi = x.astype(BF16)
    return hi, (x - hi.astype(F32)).astype(BF16)


def _merge_kernel(oa_ref, ys_ref, x_ref, g1_ref, sh2_ref, sc2_ref, anw_ref, wo_ref, n2w_ref, wrh_ref, wrl_ref,
                  x1_ref, h2_ref, lg_ref):
    oa = oa_ref[...]
    a = oa * lax.rsqrt(jnp.mean(oa * oa, axis=-1, keepdims=True) + EPS) * anw_ref[...]
    cat = jnp.concatenate([a.astype(BF16), ys_ref[...].astype(BF16)], axis=1)
    x1 = x_ref[...] + _mod(g1_ref) * _dot(cat, wo_ref[...])
    x1_ref[...] = x1
    h2 = x1 * lax.rsqrt(jnp.mean(x1 * x1, axis=-1, keepdims=True) + EPS) * n2w_ref[...]
    h2 = h2 * (1.0 + _mod(sc2_ref)) + _mod(sh2_ref)
    h2_ref[...] = h2.astype(BF16)
    hh, hl = _split2(h2)
    lg_ref[...] = _dot_nt(wrh_ref[...], hh) + _dot_nt(wrh_ref[...], hl) + _dot_nt(wrl_ref[...], hh)


def merge(o_attn, y_ssm, x, mod3, mod_row0, attn_norm_w, wo, norm2_w, w_router, tm):
    b, t, d = x.shape
    n = b * t
    tiles_per_b = t // tm
    wrt = jnp.transpose(w_router)
    wrh, wrl = _split2(wrt)

    def mod_spec(col):
        return _mod_spec(mod3, col, tm, tiles_per_b, mod_row0)

    tok = lambda wd: pl.BlockSpec((tm, wd), lambda i: (i, 0))
    full = lambda a: pl.BlockSpec(a.shape, lambda i: (0,) * a.ndim)
    return pl.pallas_call(
        _merge_kernel,
        grid=(n // tm,),
        in_specs=[tok(ATTN_WIDTH), tok(SSM_WIDTH), tok(d), mod_spec(2), mod_spec(3), mod_spec(4),
                  full(attn_norm_w), full(wo), full(norm2_w), full(wrh), full(wrl)],
        out_specs=(tok(d), tok(d), pl.BlockSpec((N_EXPERTS, tm), lambda i: (0, i))),
        out_shape=(jax.ShapeDtypeStruct((n, d), F32), jax.ShapeDtypeStruct((n, d), BF16),
                   jax.ShapeDtypeStruct((N_EXPERTS, n), F32)),
        compiler_params=_cparams(("arbitrary",)),
        name="merge",
    )(o_attn.reshape(n, ATTN_WIDTH), y_ssm.reshape(n, SSM_WIDTH), x.reshape(n, d), mod3, mod3, mod3,
      attn_norm_w, wo, norm2_w, wrh, wrl)


EXPERTS_PER_GROUP = N_EXPERTS // N_EXPERT_GROUPS


def _first_max(x, ids, axes, n_ids):
    mx = jnp.max(x, axis=axes, keepdims=True)
    return ids == jnp.min(jnp.where(x == mx, ids, n_ids), axis=axes, keepdims=True), mx


def _route_kernel(lg_ref, eb_ref, tri_ref, w_ref, pos_ref, cnt_ref):
    lg = lg_ref[...]
    tn = lg.shape[2]
    scores = jax.nn.sigmoid(lg)
    biased = scores + eb_ref[...]
    sub = lax.broadcasted_iota(jnp.int32, lg.shape, 1)
    grp = lax.broadcasted_iota(jnp.int32, (N_EXPERT_GROUPS, 1, tn), 0)
    eid = lax.broadcasted_iota(jnp.int32, lg.shape, 0) * EXPERTS_PER_GROUP + sub
    hit, m1 = _first_max(biased, sub, 1, EXPERTS_PER_GROUP)
    m2 = jnp.max(jnp.where(hit, -jnp.inf, biased), axis=1, keepdims=True)
    gs = m1 + m2
    keep = jnp.zeros(gs.shape, jnp.bool_)
    for _ in range(TOPK_GROUPS):
        hit, _m = _first_max(gs, grp, 0, N_EXPERT_GROUPS)
        keep = keep | hit
        gs = jnp.where(hit, -jnp.inf, gs)
    x = jnp.where(keep, biased, NEG)
    sel = jnp.zeros(lg.shape, jnp.bool_)
    for _ in range(TOP_K):
        hit, _m = _first_max(x, eid, (0, 1), N_EXPERTS)
        sel = sel | hit
        x = jnp.where(hit, -jnp.inf, x)
    w = jnp.where(sel, scores, 0.0)
    w = w / jnp.sum(w, axis=(0, 1), keepdims=True) * ROUTED_SCALE
    w_ref[...] = w
    selb = sel.astype(BF16).reshape(N_EXPERTS, tn)
    pos = _dot(selb, tri_ref[...])
    pos_ref[...] = jnp.where(sel, pos.reshape(lg.shape), -1.0)
    cnt = jnp.sum(sel.astype(F32), axis=2, keepdims=True)
    cnt_ref[0] = jnp.broadcast_to(cnt, cnt_ref.shape[1:]).astype(jnp.int32)


def route(logits_t, e_bias, tn):
    n = logits_t.shape[1]
    lg3 = logits_t.reshape(N_EXPERT_GROUPS, EXPERTS_PER_GROUP, n)
    eb = e_bias.astype(F32).reshape(N_EXPERT_GROUPS, EXPERTS_PER_GROUP, 1)
    tri = jnp.asarray(np.triu(np.ones((tn, tn), np.float32), 1), BF16)
    blk = pl.BlockSpec((N_EXPERT_GROUPS, EXPERTS_PER_GROUP, tn), lambda i: (0, 0, i))
    w, pos, cnt = pl.pallas_call(
        _route_kernel,
        grid=(n // tn,),
        in_specs=[blk, pl.BlockSpec(eb.shape, lambda i: (0, 0, 0)), pl.BlockSpec((tn, tn), lambda i: (0, 0))],
        out_specs=(blk, blk, pl.BlockSpec((1, N_EXPERT_GROUPS, EXPERTS_PER_GROUP, LANES), lambda i: (i, 0, 0, 0))),
        out_shape=(jax.ShapeDtypeStruct(lg3.shape, F32), jax.ShapeDtypeStruct(lg3.shape, F32),
                   jax.ShapeDtypeStruct((n // tn, N_EXPERT_GROUPS, EXPERTS_PER_GROUP, LANES), jnp.int32)),
        compiler_params=_cparams(("arbitrary",)),
        name="route",
    )(lg3, eb, tri)
    return w.reshape(N_EXPERTS, n), pos.reshape(N_EXPERTS, n), cnt[..., 0].reshape(n // tn, N_EXPERTS)


MOE_ROWS = 128


def _swiglu(xb, wgu, wd, width):
    gu = _dot(xb, wgu)
    act = _silu(gu[:, :width]) * gu[:, width:]
    return _dot(act.astype(BF16), wd)


MOE_ALIGN = 16


def _moe_slots(tm):
    worst = TOP_K * tm + N_EXPERTS * (MOE_ALIGN - 1) + MOE_ROWS
    return -(-worst // LANES) * LANES


def _moe_kernel(cnt_ref, start_ref, h2_ref, w_ref, pos_ref, x1_ref, g2_ref, wgu_ref, wd_ref, sgu_ref, sd_ref,
                o_ref, g_buf, o_buf):
    i = pl.program_id(0)
    e = pl.program_id(1)
    tm = h2_ref.shape[0]

    @pl.when(e == 0)
    def _():
        g_buf[...] = jnp.zeros(g_buf.shape, BF16)
        o_buf[...] = jnp.zeros(o_buf.shape, BF16)

    cnt = cnt_ref[i * N_EXPERTS + e]
    start = start_ref[i * N_EXPERTS + e]
    pos = pos_ref[pl.ds(e, 1), :]
    wrow = w_ref[pl.ds(e, 1), :]
    slot = lax.broadcasted_iota(jnp.int32, (MOE_ROWS, tm), 0).astype(F32)

    def step(j, carry):
        r0 = pl.multiple_of(start + j * MOE_ROWS, MOE_ALIGN)
        hit = pos == slot + (j * MOE_ROWS).astype(F32)
        g = hit.astype(BF16)
        xg = _dot(g, h2_ref[...]).astype(BF16)
        out = _swiglu(xg, wgu_ref[0], wd_ref[0], D_EXPERT)
        out = out * jnp.sum(jnp.where(hit, wrow, 0.0), axis=1, keepdims=True)
        g_buf[pl.ds(r0, MOE_ROWS), :] = g
        o_buf[pl.ds(r0, MOE_ROWS), :] = out.astype(BF16)
        return carry

    lax.fori_loop(0, (cnt + MOE_ROWS - 1) // MOE_ROWS, step, 0)

    @pl.when(e == N_EXPERTS - 1)
    def _():
        y = lax.dot_general(g_buf[...], o_buf[...], (((0,), (0,)), ((), ())), preferred_element_type=F32)
        y = y + _swiglu(h2_ref[...], sgu_ref[...], sd_ref[...], D_SHARED)
        o_ref[...] = x1_ref[...] + _mod(g2_ref) * y


def moe(h2, w_t, pos_t, counts, x1, mod3, mod_row0, t_per_b, wgu, wd, sgu, sd, tm):
    n, d = h2.shape
    tiles_per_b = t_per_b // tm
    slots = _moe_slots(tm)
    padded = (counts + MOE_ALIGN - 1) // MOE_ALIGN * MOE_ALIGN
    starts = jnp.cumsum(padded, axis=1) - padded
    grid_spec = pltpu.PrefetchScalarGridSpec(
        num_scalar_prefetch=2,
        grid=(n // tm, N_EXPERTS),
        in_specs=[
            pl.BlockSpec((tm, d), lambda i, e, *_: (i, 0)),
            pl.BlockSpec((N_EXPERTS, tm), lambda i, e, *_: (0, i)),
            pl.BlockSpec((N_EXPERTS, tm), lambda i, e, *_: (0, i)),
            pl.BlockSpec((tm, d), lambda i, e, *_: (i, 0)),
            _mod_spec(mod3, 5, tm, tiles_per_b, mod_row0),
            pl.BlockSpec((1, d, 2 * D_EXPERT), lambda i, e, *_: (e, 0, 0)),
            pl.BlockSpec((1, D_EXPERT, d), lambda i, e, *_: (e, 0, 0)),
            pl.BlockSpec(sgu.shape, lambda i, e, *_: (0, 0)),
            pl.BlockSpec(sd.shape, lambda i, e, *_: (0, 0)),
        ],
        out_specs=pl.BlockSpec((tm, d), lambda i, e, *_: (i, 0)),
        scratch_shapes=[pltpu.VMEM((slots, tm), BF16), pltpu.VMEM((slots, d), BF16)],
    )
    return pl.pallas_call(
        _moe_kernel,
        grid_spec=grid_spec,
        out_shape=jax.ShapeDtypeStruct((n, d), F32),
        compiler_params=_cparams(("arbitrary", "arbitrary")),
        name="moe",
    )(counts.reshape(-1), starts.reshape(-1).astype(jnp.int32), h2, w_t, pos_t, x1, mod3, wgu, wd, sgu, sd)


GATHER_PAGES = 8


def _gather_kernel(pt_ref, *refs):
    pages, new_ref = refs[:GATHER_PAGES], refs[GATHER_PAGES]
    rows_ref, cmpx_ref, stage_ref = refs[GATHER_PAGES + 1:]
    step = pl.program_id(1)
    last = pl.num_programs(1) - 1
    n_rows = GATHER_PAGES * PAGE_SIZE

    @pl.when(step < last)
    def _():
        for k in range(GATHER_PAGES):
            sl = slice(k * PAGE_SIZE, (k + 1) * PAGE_SIZE)
            for r in range(4):
                tile = jnp.transpose(pages[k][0, r])
                if r < 2:
                    stage_ref[r, sl, :] = tile
                else:
                    rows_ref[0, sl, (r - 2) * KV_WIDTH:(r - 1) * KV_WIDTH] = tile.astype(BF16)

    @pl.when(step == last)
    def _():
        new = new_ref[0]
        tn = new.shape[0]
        stage_ref[...] = jnp.zeros(stage_ref.shape, F32)
        for s in range(2):
            stage_ref[s, 0:tn, :] = new[:, s * KV_WIDTH:(s + 1) * KV_WIDTH]
        pad = jnp.zeros((n_rows - tn, 2 * KV_WIDTH), F32)
        rows_ref[0] = jnp.concatenate([new[:, 2 * KV_WIDTH:], pad], axis=0).astype(BF16)

    _stride_block_store(stage_ref, cmpx_ref, n_rows)


def gather_pages(cache_t, page_table, new_rows):
    b, n_pages = page_table.shape
    steps = n_pages // GATHER_PAGES
    rows = GATHER_PAGES * PAGE_SIZE
    s_out = (steps + 1) * rows

    def page_spec(k):
        def idx(i, s, pt):
            p = jnp.minimum(s, steps - 1) * GATHER_PAGES + k
            return (pt[i * n_pages + p], 0, 0, 0)
        return pl.BlockSpec((1, 4, KV_WIDTH, PAGE_SIZE), idx)

    grid_spec = pltpu.PrefetchScalarGridSpec(
        num_scalar_prefetch=1,
        grid=(b, steps + 1),
        in_specs=[page_spec(k) for k in range(GATHER_PAGES)]
        + [pl.BlockSpec((1,) + new_rows.shape[1:], lambda i, s, pt: (i, 0, 0))],
        out_specs=(
            pl.BlockSpec((1, rows, 2 * KV_WIDTH), lambda i, s, pt: (i, s, 0)),
            pl.BlockSpec((1, rows // CMP_STRIDE, CMP_STRIDE * 2 * KV_WIDTH), lambda i, s, pt: (i, s, 0)),
        ),
        scratch_shapes=[pltpu.VMEM((2, rows, KV_WIDTH), F32)],
    )
    return pl.pallas_call(
        _gather_kernel,
        grid_spec=grid_spec,
        out_shape=(jax.ShapeDtypeStruct((b, s_out, 2 * KV_WIDTH), BF16),
                   jax.ShapeDtypeStruct((b, s_out // CMP_STRIDE, CMP_STRIDE * 2 * KV_WIDTH), BF16)),
        compiler_params=_cparams(("arbitrary", "arbitrary")),
        name="gather_pages",
    )(page_table.reshape(-1), *([cache_t] * GATHER_PAGES), new_rows)


def _attention(qp, cmpx, kvb, sel_col, winb, misc, cmp_w, q_off, win_pos0, tq):
    t = qp.shape[1]
    cur_lo, cur_hi = q_off // SEL_BLOCK, (q_off + t - 1) // SEL_BLOCK
    assert cur_hi < N_SEL_LANES or (cur_lo == cur_hi == N_SEL_LANES), (q_off, t)
    n_pick = N_SEL - (1 if cur_hi >= N_SEL_LANES else 0)
    kcv = compress(cmpx, *cmp_w)
    o_cmp, mneg = cmp_select(qp, kcv, q_off, n_pick, tq)
    return sel_win_attention(qp, mneg, kvb, sel_col, winb, o_cmp, misc, q_off, win_pos0, tq)


def kernel(x_prompt, x_sample, cache_kv, cache_win, state_ssm, state_conv, page_table, c_prompt, c_sample, w_ada, b_ada, norm1_w, norm2_w, w_in, q_norm_w, k_norm_w, cmp_pe, cmp_w1, cmp_w2, attn_out_norm_w, conv_w, conv_b, dt_bias, a_log, d_skip, ssm_norm_w, w_out, w_router, e_bias, w_exp_gu, w_exp_down, w_sh_gu, w_sh_down):
    xp, xq = x_prompt, x_sample
    bp, tp, d = xp.shape
    bq, tq, _ = xq.shape
    depth = w_ada.shape[0]
    past_len = page_table.shape[1] * PAGE_SIZE
    nq = bq * tq
    tq_pad = LANES // GQA_GROUP
    assert tp % 512 == 0 and tp >= WINDOW and nq % 8 == 0 and tq <= tq_pad
    pos_p = jnp.arange(tp, dtype=jnp.int32)
    pos_q = jnp.tile(past_len + jnp.arange(tq, dtype=jnp.int32), bq)
    c_all = jnp.concatenate([c_prompt, c_sample], axis=0)
    c_all = jnp.pad(c_all, ((0, -c_all.shape[0] % 8), (0, 0)))
    outs = [[] for _ in range(8)]
    for l in range(depth):
        mod = adaln_all(c_all, w_ada[l], b_ada[l])
        mod_p = mod.reshape(mod.shape[0], 1, 6 * d)
        mod_q = jnp.repeat(mod[bp:bp + bq], tq, axis=0)
        wp = _prep_w_in(w_in[l])
        cmp_w = _prep_compress(cmp_pe[l], cmp_w1[l], cmp_w2[l])
        wo = w_out[l].astype(BF16)
        wgu, wd = w_exp_gu[l].astype(BF16), w_exp_down[l].astype(BF16)
        sgu, sd = w_sh_gu[l].astype(BF16), w_sh_down[l].astype(BF16)
        ssm_w = (conv_w[l], conv_b[l], dt_bias[l], a_log[l], d_skip[l], ssm_norm_w[l])
        n1w, n2w, anw = norm1_w[l:l + 1], norm2_w[l:l + 1], attn_out_norm_w[l:l + 1]

        qp, kvb, win, winb, z, xbc, misc, kvt, cmpx = inproj(xp, mod_p, 0, n1w, wp, q_norm_w[l], k_norm_w[l], pos_p,
                                                            512, True)
        r3 = lambda a: a.reshape(bp, tp, a.shape[-1])
        o_attn = _attention(r3(qp), cmpx, r3(kvb), 2, r3(winb), r3(misc), cmp_w, 0, 0, 128)
        y_ssm, h_new, conv_new = ssd(r3(xbc), r3(z), r3(misc), jnp.zeros((bp, CONV_WIDTH - 1, CONV_DIM), F32),
                                     jnp.zeros((bp, SSM_HEADS, SSM_HEAD_DIM, SSM_STATE), F32), *ssm_w)
        x1, h2, lg = merge(o_attn, y_ssm, xp, mod_p, 0, anw, wo, n2w, w_router[l], 512)
        w_t, pos_t, cnt = route(lg, e_bias[l], 512)
        xp = moe(h2, w_t, pos_t, cnt, x1, mod_p, 0, tp, wgu, wd, sgu, sd, 512).reshape(bp, tp, d)
        outs[0].append(jnp.transpose(kvt.reshape(bp, 4, N_KV_HEADS, HEAD_DIM, tp), (0, 4, 1, 2, 3)))
        outs[1].append(win.reshape(bp, tp, 2, N_KV_HEADS, HEAD_DIM)[:, tp - WINDOW:])
        outs[2].append(h_new)
        outs[3].append(conv_new)

        xq1 = xq.reshape(1, nq, d)
        qp, kvb, win, winb, z, xbc, misc, kv = inproj(xq1, mod_q, 0, n1w, wp, q_norm_w[l], k_norm_w[l], pos_q, nq,
                                                      False)
        rq = lambda a: a.reshape(bq, tq, a.shape[-1])
        padq = lambda a: jnp.pad(rq(a), ((0, 0), (0, tq_pad - tq), (0, 0)))
        cache_t = jnp.transpose(cache_kv[l], (0, 2, 3, 4, 1)).reshape(cache_kv.shape[1], 4, KV_WIDTH, PAGE_SIZE)
        past, cmpx = gather_pages(cache_t, page_table, rq(kv))
        win_all = jnp.concatenate([cache_win[l].reshape(bq, WINDOW, 2 * KV_WIDTH).astype(BF16), rq(winb),
                                   jnp.zeros((bq, -(WINDOW + tq_pad) % WIN_CHUNK + tq_pad - tq, 2 * KV_WIDTH), BF16)],
                                  axis=1)
        o_attn = _attention(padq(qp), cmpx, past, 0, win_all, padq(misc), cmp_w, past_len, past_len - WINDOW,
                            tq_pad)[:, :tq]
        y_ssm, h_new, conv_new = ssd(rq(xbc), rq(z), rq(misc), state_conv[l], state_ssm[l], *ssm_w)
        x1, h2, lg = merge(o_attn.reshape(1, nq, ATTN_WIDTH), y_ssm.reshape(1, nq, SSM_WIDTH), xq1, mod_q, 0,
                           anw, wo, n2w, w_router[l], nq)
        w_t, pos_t, cnt = route(lg, e_bias[l], nq)
        xq = moe(h2, w_t, pos_t, cnt, x1, mod_q, 0, nq, wgu, wd, sgu, sd, nq).reshape(bq, tq, d)
        win_rows = win.reshape(bq, tq, 2, N_KV_HEADS, HEAD_DIM)
        outs[4].append(kv.reshape(bq, tq, 4, N_KV_HEADS, HEAD_DIM))
        outs[5].append(jnp.concatenate([cache_win[l], win_rows.astype(cache_win.dtype)], axis=1)[:, tq:])
        outs[6].append(h_new)
        outs[7].append(conv_new)
    return (xp, xq) + tuple(jnp.stack(o) for o in outs)
```

```python
import functools
import math

import jax
import jax.numpy as jnp
import numpy as np
from jax import lax
from jax.experimental import pallas as pl
from jax.experimental.pallas import tpu as pltpu

D_MODEL = 1024
PAGE_SIZE = 128
HEAD_DIM = 64
N_Q_HEADS = 8
N_KV_HEADS = 2
GQA_GROUP = N_Q_HEADS // N_KV_HEADS
ATTN_WIDTH = N_Q_HEADS * HEAD_DIM
KV_WIDTH = N_KV_HEADS * HEAD_DIM
ROPE_DIM = HEAD_DIM // 4
ROPE_THETA = 500000.0
CMP_LEN = 32
CMP_STRIDE = 16
CMP_HIDDEN = 4 * HEAD_DIM
SEL_BLOCK = 64
N_SEL = 16
N_LOCAL = 2
WINDOW = 512
SSM_HEADS = 8
SSM_HEAD_DIM = 64
SSM_WIDTH = SSM_HEADS * SSM_HEAD_DIM
SSM_GROUPS = 2
SSM_STATE = 128
CONV_WIDTH = 4
CONV_DIM = SSM_WIDTH + 2 * SSM_GROUPS * SSM_STATE
SSD_CHUNK = 128
MIX_WIDTH = ATTN_WIDTH + SSM_WIDTH
N_EXPERTS = 64
N_EXPERT_GROUPS = 8
TOPK_GROUPS = 4
TOP_K = 8
D_EXPERT = 256
D_SHARED = 256
ROUTED_SCALE = 2.5
IN_SIZES = (ATTN_WIDTH, 6 * KV_WIDTH, 3 * N_Q_HEADS, SSM_WIDTH, CONV_DIM, SSM_HEADS)
N_IN = sum(IN_SIZES)
EPS = 1e-6
NEG = -1e30
BIG = 1e6

LANES = 128
VMEM_LIMIT = 56 * 1024 * 1024

BF16 = jnp.bfloat16
F32 = jnp.float32
LOG2E = math.log2(math.e)


def _cparams(sem, flags=None):
    return pltpu.CompilerParams(dimension_semantics=sem, vmem_limit_bytes=VMEM_LIMIT, flags=flags)


def _silu(x):
    return x * jax.nn.sigmoid(x)


def _dot(a, b):
    return jnp.dot(a, b, preferred_element_type=F32)


def _dot_nt(a, b):
    return lax.dot_general(a, b, (((1,), (1,)), ((), ())), preferred_element_type=F32)


def _mod_spec(mod, col, tm, tiles_per_b, row0):
    if mod.ndim == 3:
        return pl.BlockSpec((1, 1, D_MODEL), lambda i, *_: (row0 + i // tiles_per_b, 0, col))
    return pl.BlockSpec((tm, D_MODEL), lambda i, *_: (i, col))


def _mod(ref):
    return ref[0] if len(ref.shape) == 3 else ref[...]


def _adaln_kernel(c_ref, w_ref, b_ref, o_ref):
    c = c_ref[...]
    a = _silu(c).astype(BF16)
    o_ref[...] = _dot(a, w_ref[...].astype(BF16)) + b_ref[...]


def adaln_all(c_all, w_ada, b_ada):
    rows = c_all.shape[0]
    n = w_ada.shape[1]
    tn = 1024
    return pl.pallas_call(
        _adaln_kernel,
        grid=(n // tn,),
        in_specs=[
            pl.BlockSpec((rows, D_MODEL), lambda j: (0, 0)),
            pl.BlockSpec((D_MODEL, tn), lambda j: (0, j)),
            pl.BlockSpec((1, tn), lambda j: (0, j)),
        ],
        out_specs=pl.BlockSpec((rows, tn), lambda j: (0, j)),
        out_shape=jax.ShapeDtypeStruct((rows, n), F32),
        compiler_params=_cparams(("arbitrary",)),
        name="adaln",
    )(c_all, w_ada, b_ada.reshape(1, n))


_C_Q = 0
_C_KV = _C_Q + ATTN_WIDTH
_C_Z = _C_KV + 6 * KV_WIDTH
_C_XBC = _C_Z + SSM_WIDTH
_C_MISC = _C_XBC + CONV_DIM
N_IN_PAD = _C_MISC + LANES
N_GATES = 3 * N_Q_HEADS


def _prep_w_in(w_in):
    s = np.cumsum((0,) + IN_SIZES)
    q, kv, g, z, xbc, dt = (w_in[:, int(s[i]):int(s[i + 1])] for i in range(6))
    pad = jnp.zeros((w_in.shape[0], LANES - N_GATES - SSM_HEADS), w_in.dtype)
    return jnp.concatenate([q, kv, z, xbc, dt, g, pad], axis=1).astype(BF16)


def _group_mean_matrix(width):
    i = np.arange(width)
    m = (i[:, None] // HEAD_DIM == i[None, :] // HEAD_DIM).astype(np.float32) / HEAD_DIM
    return jnp.asarray(m, BF16)


def _rope_tables(pos):
    half = ROPE_DIM // 2
    inv_freq = ROPE_THETA ** (-jnp.arange(half, dtype=F32) / half)
    ang = pos.astype(F32)[:, None] * inv_freq[None, :]
    cos, sin = jnp.cos(ang), jnp.sin(ang)
    t = pos.shape[0]
    one = jnp.ones((t, HEAD_DIM - ROPE_DIM), F32)
    zero = jnp.zeros((t, HEAD_DIM - ROPE_DIM), F32)
    zh = jnp.zeros((t, half), F32)
    c = jnp.concatenate([cos, cos, one], axis=1)
    s_up = jnp.concatenate([-sin, zh, zero], axis=1)
    s_dn = jnp.concatenate([zh, sin, zero], axis=1)
    rep = LANES // HEAD_DIM
    return jnp.tile(c, (1, rep)), jnp.tile(s_up, (1, rep)), jnp.tile(s_dn, (1, rep))


def _rope(x, c, s_up, s_dn):
    w = x.shape[1]
    half = ROPE_DIM // 2
    rep = w // LANES
    ct = jnp.concatenate([c] * rep, axis=1) if rep > 1 else c
    su = jnp.concatenate([s_up] * rep, axis=1) if rep > 1 else s_up
    sd = jnp.concatenate([s_dn] * rep, axis=1) if rep > 1 else s_dn
    up = pltpu.roll(x, w - half, axis=1)
    dn = pltpu.roll(x, half, axis=1)
    return x * ct + up * su + dn * sd


def _stride_block_store(stage_ref, cmpx_ref, n_rows):
    for j in range(CMP_STRIDE):
        for s in range(2):
            rows_j = stage_ref[s, pl.ds(j, n_rows // CMP_STRIDE, stride=CMP_STRIDE), :]
            c0 = (2 * j + s) * KV_WIDTH
            cmpx_ref[0, :, c0:c0 + KV_WIDTH] = rows_j.astype(BF16)


def _inproj_kernel(x_ref, shift_ref, scale_ref, nw_ref, w_ref, qw_ref, kw_ref, gq_ref, gk_ref,
                   c_ref, su_ref, sd_ref,
                   qp_ref, kvb_ref, win_ref, winb_ref, z_ref, xbc_ref, misc_ref, *rest, seq_layout):
    x = x_ref[...]
    ms = jnp.mean(x * x, axis=-1, keepdims=True)
    h = x * lax.rsqrt(ms + EPS) * nw_ref[...]
    h = h * (1.0 + _mod(scale_ref)) + _mod(shift_ref)
    hb = h.astype(BF16)
    c, su, sd = c_ref[...], su_ref[...], sd_ref[...]

    q = _dot(hb, w_ref[:, _C_Q:_C_Q + ATTN_WIDTH])
    qms = _dot((q * q).astype(BF16), gq_ref[...])
    q = q * lax.rsqrt(qms + EPS) * qw_ref[...]
    q = _rope(q, c, su, sd) * (HEAD_DIM ** -0.5 * LOG2E)
    lane = lax.broadcasted_iota(jnp.int32, q.shape, 1) % LANES
    lo = lane < HEAD_DIM
    q_up = pltpu.roll(q, ATTN_WIDTH - HEAD_DIM, axis=1)
    q_dn = pltpu.roll(q, HEAD_DIM, axis=1)
    zero = jnp.zeros_like(q)
    nat_lo = jnp.where(lo, q, zero)
    nat_hi = jnp.where(lo, zero, q)
    up_lo = jnp.where(lo, q_up, zero)
    dn_hi = jnp.where(lo, zero, q_dn)
    blocks = []
    for hd in range(N_Q_HEADS):
        pair = hd // 2
        sl = slice(pair * LANES, (pair + 1) * LANES)
        if hd < GQA_GROUP:
            blocks.append((nat_lo if hd % 2 == 0 else up_lo)[:, sl])
        else:
            blocks.append((dn_hi if hd % 2 == 0 else nat_hi)[:, sl])
    qp_ref[...] = jnp.concatenate(blocks, axis=1).astype(BF16)

    kv = _dot(hb, w_ref[:, _C_KV:_C_KV + 6 * KV_WIDTH])
    outs = []
    for br in range(3):
        k = kv[:, br * 2 * KV_WIDTH:br * 2 * KV_WIDTH + KV_WIDTH]
        v = kv[:, br * 2 * KV_WIDTH + KV_WIDTH:(br + 1) * 2 * KV_WIDTH]
        kms = _dot((k * k).astype(BF16), gk_ref[...])
        k = k * lax.rsqrt(kms + EPS) * kw_ref[:, br * KV_WIDTH:(br + 1) * KV_WIDTH]
        k = _rope(k, c, su, sd)
        outs += [k, v]
    kvrows = jnp.concatenate(outs[:4], axis=1)
    winrows = jnp.concatenate(outs[4:], axis=1)
    kvb_ref[...] = kvrows.astype(BF16)
    win_ref[...] = winrows
    winb_ref[...] = winrows.astype(BF16)
    if seq_layout:
        kvt_ref, cmpx_ref, stage_ref = rest
        tm = kvrows.shape[0]
        for r in range(4):
            kvt_ref[0, r] = jnp.transpose(kvrows[:, r * KV_WIDTH:(r + 1) * KV_WIDTH])
        for s in range(2):
            stage_ref[s] = kvrows[:, s * KV_WIDTH:(s + 1) * KV_WIDTH]
        _stride_block_store(stage_ref, cmpx_ref, tm)
    else:
        rest[0][...] = kvrows

    z_ref[...] = _dot(hb, w_ref[:, _C_Z:_C_Z + SSM_WIDTH])
    xbc_ref[...] = _dot(hb, w_ref[:, _C_XBC:_C_XBC + CONV_DIM])
    misc_ref[...] = _dot(hb, w_ref[:, _C_MISC:_C_MISC + LANES])


def inproj(x, mod3, mod_row0, norm_w, wp, q_norm_w, k_norm_w, pos, tm, seq_layout):
    b, t, d = x.shape
    n = b * t
    tiles_per_b = t // tm
    xf = x.reshape(n, d)
    c, su, sd = _rope_tables(pos)
    qw = jnp.tile(q_norm_w, N_Q_HEADS).reshape(1, ATTN_WIDTH)
    kw = jnp.concatenate([jnp.tile(k_norm_w[i], N_KV_HEADS) for i in range(3)]).reshape(1, 3 * KV_WIDTH)
    gq = _group_mean_matrix(ATTN_WIDTH)
    gk = _group_mean_matrix(KV_WIDTH)

    def mod_spec(col):
        return _mod_spec(mod3, col, tm, tiles_per_b, mod_row0)

    def tok(wd):
        return pl.BlockSpec((tm, wd), lambda i: (i, 0))

    def full(a):
        return pl.BlockSpec(a.shape, lambda i: (0,) * a.ndim)

    rope_spec = pl.BlockSpec((tm, LANES), lambda i: (i % tiles_per_b, 0))
    out_shape = [
        jax.ShapeDtypeStruct((n, N_Q_HEADS * LANES), BF16),
        jax.ShapeDtypeStruct((n, 4 * KV_WIDTH), BF16),
        jax.ShapeDtypeStruct((n, 2 * KV_WIDTH), F32),
        jax.ShapeDtypeStruct((n, 2 * KV_WIDTH), BF16),
        jax.ShapeDtypeStruct((n, SSM_WIDTH), F32),
        jax.ShapeDtypeStruct((n, CONV_DIM), F32),
        jax.ShapeDtypeStruct((n, LANES), F32),
    ]
    out_specs = [tok(s.shape[1]) for s in out_shape]
    scratch = []
    if seq_layout:
        out_shape += [jax.ShapeDtypeStruct((b, 4, KV_WIDTH, t), F32),
                      jax.ShapeDtypeStruct((b, t // CMP_STRIDE, CMP_STRIDE * 2 * KV_WIDTH), BF16)]
        out_specs += [pl.BlockSpec((1, 4, KV_WIDTH, tm), lambda i: (i // tiles_per_b, 0, 0, i % tiles_per_b)),
                      pl.BlockSpec((1, tm // CMP_STRIDE, CMP_STRIDE * 2 * KV_WIDTH),
                                   lambda i: (i // tiles_per_b, i % tiles_per_b, 0))]
        scratch = [pltpu.VMEM((2, tm, KV_WIDTH), F32)]
    else:
        out_shape += [jax.ShapeDtypeStruct((n, 4 * KV_WIDTH), F32)]
        out_specs += [tok(4 * KV_WIDTH)]
    return pl.pallas_call(
        functools.partial(_inproj_kernel, seq_layout=seq_layout),
        grid=(n // tm,),
        in_specs=[tok(d), mod_spec(0), mod_spec(1), full(norm_w), full(wp), full(qw), full(kw), full(gq), full(gk),
                  rope_spec, rope_spec, rope_spec],
        out_specs=tuple(out_specs),
        out_shape=tuple(out_shape),
        scratch_shapes=scratch,
        compiler_params=_cparams(("arbitrary",)),
        name="inproj",
    )(xf, mod3, mod3, norm_w, wp, qw, kw, gq, gk, c, su, sd)


def _prep_compress(cmp_pe, cmp_w1, cmp_w2):
    half = CMP_LEN // 2
    eye = jnp.eye(N_KV_HEADS, dtype=F32)
    w1 = cmp_w1.reshape(2, CMP_LEN, HEAD_DIM, CMP_HIDDEN)
    w1s = []
    for part in (w1[:, :half], w1[:, half:]):
        w1s.append(jnp.einsum("pjdo,hg->pjhdgo", part, eye).reshape(2, half * KV_WIDTH, N_KV_HEADS * CMP_HIDDEN))
    w1p = jnp.concatenate(w1s, axis=2).astype(BF16)
    pe = cmp_pe.reshape(2, 2, half, 1, HEAD_DIM)
    pep = jnp.broadcast_to(pe, (2, 2, half, N_KV_HEADS, HEAD_DIM)).reshape(2, 2, half * KV_WIDTH)
    w2p = jnp.einsum("poe,hg->phoge", cmp_w2, eye).reshape(2, N_KV_HEADS * CMP_HIDDEN, KV_WIDTH).astype(BF16)
    return w1p, pep, w2p


def _compress_kernel(x_ref, w1_ref, pe_ref, w2_ref, o_ref, *, row_w):
    part = pl.program_id(1)
    nb = x_ref.shape[1]
    half = CMP_LEN // 2
    hid = N_KV_HEADS * CMP_HIDDEN
    cols = []
    for j in range(half):
        a = x_ref[0, :, j * row_w:j * row_w + KV_WIDTH]
        b = x_ref[0, :, j * row_w + KV_WIDTH:j * row_w + 2 * KV_WIDTH]
        cols.append(jnp.where(part == 0, a, b))
    x = jnp.concatenate(cols, axis=1).astype(F32)
    pe = pe_ref[0]
    u = _dot((x + pe[0:1]).astype(BF16), w1_ref[0, :, :hid])
    v = _dot((x + pe[1:2]).astype(BF16), w1_ref[0, :, hid:])
    h1 = u + pltpu.roll(v, nb - 1, axis=0)
    out = _dot(_silu(h1).astype(BF16), w2_ref[0])
    row = lax.broadcasted_iota(jnp.int32, out.shape, 0)
    o_ref[0, 0] = jnp.where(row < nb - 1, out, 0.0).astype(o_ref.dtype)


def compress(x, w1p, pep, w2p):
    b, nb, width = x.shape
    row_w = width // CMP_STRIDE
    return pl.pallas_call(
        functools.partial(_compress_kernel, row_w=row_w),
        grid=(b, 2),
        in_specs=[
            pl.BlockSpec((1, nb, CMP_STRIDE * row_w), lambda i, p: (i, 0, 0)),
            pl.BlockSpec((1,) + w1p.shape[1:], lambda i, p: (p, 0, 0)),
            pl.BlockSpec((1,) + pep.shape[1:], lambda i, p: (p, 0, 0)),
            pl.BlockSpec((1,) + w2p.shape[1:], lambda i, p: (p, 0, 0)),
        ],
        out_specs=pl.BlockSpec((1, 1, nb, KV_WIDTH), lambda i, p: (i, p, 0, 0)),
        out_shape=jax.ShapeDtypeStruct((b, 2, nb, KV_WIDTH), BF16),
        compiler_params=_cparams(("arbitrary", "arbitrary")),
        name="compress",
    )(x, w1p, pep, w2p)


N_SEL_LANES = LANES


def _cover_matrix(nb):
    c = np.arange(nb)[:, None]
    j = np.arange(N_SEL_LANES)[None, :]
    start = c * CMP_STRIDE
    m = (start < (j + 1) * SEL_BLOCK) & (start + CMP_LEN > j * SEL_BLOCK)
    return jnp.asarray(m.astype(np.float32), BF16)


def _place_heads(res, kv):
    lane = lax.broadcasted_iota(jnp.int32, res[0].shape, 1)
    lo = lane < HEAD_DIM
    blocks = []
    for pair in range(GQA_GROUP // 2):
        a, b = res[2 * pair], res[2 * pair + 1]
        if kv == 0:
            blocks.append(jnp.where(lo, a, pltpu.roll(b, HEAD_DIM, axis=1)))
        else:
            blocks.append(jnp.where(lo, pltpu.roll(a, HEAD_DIM, axis=1), b))
    return jnp.concatenate(blocks, axis=1)


def _group_rows(q_ref, kv):
    heads = range(kv * GQA_GROUP, (kv + 1) * GQA_GROUP)
    return jnp.concatenate([q_ref[0, :, hd * LANES:(hd + 1) * LANES] for hd in heads], axis=0)


def _heads_from_transposed(out_t, tq, kv):
    out = jnp.transpose(out_t)
    return _place_heads([out[g * tq:(g + 1) * tq] for g in range(GQA_GROUP)], kv)


def _cmp_select_kernel(q_ref, kc_ref, vc_ref, covt_ref, o_ref, m_ref, *, q_off, n_pick):
    tq = q_ref.shape[1]
    rows = GQA_GROUP * tq
    nb = kc_ref.shape[2]
    wl = max(tq, LANES)
    assert tq % LANES == 0 or rows == LANES
    t0 = q_off + pl.program_id(1) * tq
    kc = kc_ref[0, 0]
    vc = vc_ref[0, 0]
    qpos = t0 + lax.broadcasted_iota(jnp.int32, (nb, rows), 1) % tq
    cend = lax.broadcasted_iota(jnp.int32, (nb, rows), 0) * CMP_STRIDE + (CMP_LEN - 1)
    valid = cend <= qpos
    blk = lax.broadcasted_iota(jnp.int32, (N_SEL_LANES, wl), 0)
    cur = (t0 + lax.broadcasted_iota(jnp.int32, (N_SEL_LANES, wl), 1) % tq) // SEL_BLOCK
    forced = (blk == 0) | ((blk <= cur) & (blk > cur - N_LOCAL))
    o_groups = []
    for kv in range(N_KV_HEADS):
        s = _dot_nt(kc, _group_rows(q_ref, kv))
        s = jnp.where(valid, s, NEG)
        e = jnp.exp2(s - jnp.max(s, axis=0, keepdims=True))
        p = e / jnp.sum(e, axis=0, keepdims=True)
        p = jnp.where(valid, p, 0.0)
        o_t = lax.dot_general(vc, p.astype(BF16), (((0,), (0,)), ((), ())), preferred_element_type=F32)
        o_groups.append(_heads_from_transposed(o_t, tq, kv))
        if tq % LANES == 0:
            psum = sum(p[:, g * tq:(g + 1) * tq] for g in range(GQA_GROUP))
        else:
            psum = p + sum(pltpu.roll(p, g * tq, axis=1) for g in range(1, GQA_GROUP))
        hi, lo = _split2(psum)
        imp = _dot(covt_ref[...], hi) + _dot(covt_ref[...], lo)
        x = jnp.where(forced, BIG, jnp.where(blk > cur, -BIG, imp))
        sel = jnp.zeros(x.shape, jnp.bool_)
        for _ in range(n_pick):
            mx = jnp.max(x, axis=0, keepdims=True)
            idx = jnp.min(jnp.where(x == mx, blk, N_SEL_LANES), axis=0, keepdims=True)
            hit = blk == idx
            sel = sel | hit
            x = jnp.where(hit, -jnp.inf, x)
        mneg = jnp.transpose(jnp.where(sel, 0.0, NEG))
        m_ref[0, kv] = mneg[:tq].astype(m_ref.dtype)
    o_ref[0] = jnp.concatenate(o_groups, axis=1)


def cmp_select(qp, kcv, q_off, n_pick, tq):
    b, t, _ = qp.shape
    nb = kcv.shape[2]
    cover = jnp.transpose(_cover_matrix(nb))
    return pl.pallas_call(
        functools.partial(_cmp_select_kernel, q_off=q_off, n_pick=n_pick),
        grid=(b, t // tq),
        in_specs=[
            pl.BlockSpec((1, tq, N_Q_HEADS * LANES), lambda i, j: (i, j, 0)),
            pl.BlockSpec((1, 1, nb, KV_WIDTH), lambda i, j: (i, 0, 0, 0)),
            pl.BlockSpec((1, 1, nb, KV_WIDTH), lambda i, j: (i, 1, 0, 0)),
            pl.BlockSpec((N_SEL_LANES, nb), lambda i, j: (0, 0)),
        ],
        out_specs=(
            pl.BlockSpec((1, tq, ATTN_WIDTH), lambda i, j: (i, j, 0)),
            pl.BlockSpec((1, N_KV_HEADS, tq, N_SEL_LANES), lambda i, j: (i, 0, j, 0)),
        ),
        out_shape=(
            jax.ShapeDtypeStruct((b, t, ATTN_WIDTH), F32),
            jax.ShapeDtypeStruct((b, N_KV_HEADS, t, N_SEL_LANES), BF16),
        ),
        compiler_params=_cparams(("arbitrary", "arbitrary")),
        name="cmp_select",
    )(qp, kcv, kcv, cover)


SEL_TILE_ELEMS = 512 * 512
WIN_CHUNK = 256


def _block_onehot(s):
    key = np.arange(s)[:, None]
    j = np.arange(N_SEL_LANES)[None, :]
    return jnp.asarray((key // SEL_BLOCK == j).astype(np.float32), BF16)


def _gate_expand():
    m = np.zeros((3, LANES, ATTN_WIDTH), np.float32)
    for br in range(3):
        for hd in range(N_Q_HEADS):
            m[br, SSM_HEADS + 3 * hd + br, hd * HEAD_DIM:(hd + 1) * HEAD_DIM] = 1.0
    return jnp.asarray(m, BF16)


def _flash_update(ss, v, m_ref, acc_ref):
    lane = lax.broadcasted_iota(jnp.int32, v.shape, 1)
    one = jnp.ones(v.shape, v.dtype)
    stage = []
    for k, s in enumerate(ss):
        m_old = m_ref[k]
        m_new = jnp.maximum(m_old, jnp.max(s, axis=0, keepdims=True))
        alpha = jnp.exp2(m_old - m_new)
        p = jnp.exp2(s - m_new)
        m_ref[k] = m_new
        stage.append((alpha, p.astype(BF16)))
    for k, (alpha, p) in enumerate(stage):
        vk = jnp.where((lane < HEAD_DIM) == (k == 0), v, one)
        pv = lax.dot_general(vk, p, (((0,), (0,)), ((), ())), preferred_element_type=F32)
        acc_ref[k] = alpha * acc_ref[k] + pv


def _sel_chunk(rows, n_keys):
    chunk = SEL_TILE_ELEMS // rows
    while n_keys % chunk:
        chunk //= 2
    return chunk


def _sel_win_kernel(q_ref, mneg_ref, ksel_ref, vsel_ref, et_ref, kwin_ref, vwin_ref, ocmp_ref, misc_ref, eg_ref,
                    o_ref, lhs_ref, m_ref, acc_ref, *, q_off, win_pos0):
    tq = q_ref.shape[1]
    rows = GQA_GROUP * tq
    SEL_CHUNK = _sel_chunk(rows, ksel_ref.shape[1])
    t0 = q_off + pl.program_id(1) * tq
    n_sel = lax.shift_right_logical(t0 + tq - 1, int(math.log2(SEL_CHUNK))) + 1
    w_lo = jnp.maximum(t0 - (WINDOW - 1) - win_pos0, 0) // WIN_CHUNK
    w_hi = (t0 + tq - 1 - win_pos0) // WIN_CHUNK + 1

    def qrow(n_keys):
        return lax.broadcasted_iota(jnp.int32, (n_keys, rows), 1) % tq + t0

    def init():
        m_ref[...] = jnp.full(m_ref.shape, NEG, F32)
        acc_ref[...] = jnp.zeros(acc_ref.shape, F32)

    def finish():
        outs = []
        for kv in range(N_KV_HEADS):
            acc = acc_ref[kv]
            denom_row = HEAD_DIM * (1 - kv)
            outs.append(_heads_from_transposed(acc / acc[denom_row:denom_row + 1, :], tq, kv))
        return jnp.concatenate(outs, axis=1)

    for kv in range(N_KV_HEADS):
        for g in range(GQA_GROUP):
            hd = kv * GQA_GROUP + g
            lhs_ref[kv, g * tq:(g + 1) * tq, :LANES] = q_ref[0, :, hd * LANES:(hd + 1) * LANES]
            lhs_ref[kv, g * tq:(g + 1) * tq, LANES:] = mneg_ref[0, kv]

    init()

    def sel_step(c, carry, causal):
        r0 = pl.multiple_of(c * SEL_CHUNK, SEL_CHUNK)
        rhs = jnp.concatenate([ksel_ref[0, pl.ds(r0, SEL_CHUNK), :], et_ref[pl.ds(r0, SEL_CHUNK), :]], axis=1)
        v = vsel_ref[0, pl.ds(r0, SEL_CHUNK), :]
        if causal:
            ok = r0 + lax.broadcasted_iota(jnp.int32, (SEL_CHUNK, rows), 0) <= qrow(SEL_CHUNK)
        ss = [_dot_nt(rhs, lhs_ref[kv]) for kv in range(N_KV_HEADS)]
        if causal:
            ss = [jnp.where(ok, s, NEG) for s in ss]
        _flash_update(ss, v, m_ref, acc_ref)
        return carry

    n_full = lax.shift_right_logical(t0 + 1, int(math.log2(SEL_CHUNK)))
    lax.fori_loop(0, n_full, functools.partial(sel_step, causal=False), 0)
    lax.fori_loop(n_full, n_sel, functools.partial(sel_step, causal=True), 0)
    o_sel = finish()

    init()

    def win_step(c, carry):
        r0 = pl.multiple_of(c * WIN_CHUNK, WIN_CHUNK)
        k = kwin_ref[0, pl.ds(r0, WIN_CHUNK), :]
        v = vwin_ref[0, pl.ds(r0, WIN_CHUNK), :]
        wpos = win_pos0 + r0 + lax.broadcasted_iota(jnp.int32, (WIN_CHUNK, rows), 0)
        qr = qrow(WIN_CHUNK)
        ok = (wpos <= qr) & (wpos > qr - WINDOW)
        ss = [jnp.where(ok, _dot_nt(k, lhs_ref[kv, :, :LANES]), NEG) for kv in range(N_KV_HEADS)]
        _flash_update(ss, v, m_ref, acc_ref)
        return carry

    lax.fori_loop(w_lo, w_hi, win_step, 0)
    o_win = finish()

    gates = jax.nn.sigmoid(misc_ref[0])
    ghi = gates.astype(BF16)
    glo = (gates - ghi.astype(F32)).astype(BF16)
    branches = (ocmp_ref[0], o_sel, o_win)
    out = jnp.zeros(branches[0].shape, F32)
    for br in range(3):
        out = out + (_dot(ghi, eg_ref[br]) + _dot(glo, eg_ref[br])) * branches[br]
    o_ref[0] = out


def sel_win_attention(qp, mneg, kvb, sel_col, winb, o_cmp, misc, q_off, win_pos0, tq):
    b, t, _ = qp.shape
    s = kvb.shape[1]
    sw = winb.shape[1]
    et = _block_onehot(s)
    eg = _gate_expand()
    rows = GQA_GROUP * tq
    assert q_off + t <= s and q_off + t - win_pos0 <= sw and sw % WIN_CHUNK == 0
    return pl.pallas_call(
        functools.partial(_sel_win_kernel, q_off=q_off, win_pos0=win_pos0),
        grid=(b, t // tq),
        in_specs=[
            pl.BlockSpec((1, tq, N_Q_HEADS * LANES), lambda i, j: (i, j, 0)),
            pl.BlockSpec((1, N_KV_HEADS, tq, N_SEL_LANES), lambda i, j: (i, 0, j, 0)),
            pl.BlockSpec((1, s, KV_WIDTH), lambda i, j: (i, 0, sel_col)),
            pl.BlockSpec((1, s, KV_WIDTH), lambda i, j: (i, 0, sel_col + 1)),
            pl.BlockSpec((s, N_SEL_LANES), lambda i, j: (0, 0)),
            pl.BlockSpec((1, sw, KV_WIDTH), lambda i, j: (i, 0, 0)),
            pl.BlockSpec((1, sw, KV_WIDTH), lambda i, j: (i, 0, 1)),
            pl.BlockSpec((1, tq, ATTN_WIDTH), lambda i, j: (i, j, 0)),
            pl.BlockSpec((1, tq, LANES), lambda i, j: (i, j, 0)),
            pl.BlockSpec((3, LANES, ATTN_WIDTH), lambda i, j: (0, 0, 0)),
        ],
        out_specs=pl.BlockSpec((1, tq, ATTN_WIDTH), lambda i, j: (i, j, 0)),
        out_shape=jax.ShapeDtypeStruct((b, t, ATTN_WIDTH), F32),
        scratch_shapes=[
            pltpu.VMEM((N_KV_HEADS, rows, 2 * LANES), BF16),
            pltpu.VMEM((N_KV_HEADS, 1, rows), F32),
            pltpu.VMEM((N_KV_HEADS, LANES, rows), F32),
        ],
        compiler_params=_cparams(("arbitrary", "arbitrary")),
        name="sel_win_attention",
    )(qp, mneg, kvb, kvb, et, winb, winb, o_cmp, misc, eg)


CONV_PAD = 8
HEAD_PAIRS = SSM_HEADS // 2


def _split3(x):
    a = x.astype(BF16)
    r = x - a.astype(F32)
    b = r.astype(BF16)
    c = (r - b.astype(F32)).astype(BF16)
    return a, b, c


def _ssd_kernel(xbc_ref, z_ref, misc_ref, conv0_ref, h0_ref, cw_ref, cb_ref, dtb_ref, a_ref, dsk_ref, nw_ref,
                y_ref, hout_ref, cout_ref, xp_ref, h_ref, ms_ref, *, t_valid):
    ch = pl.program_id(1)
    L = SSD_CHUNK
    keep = CONV_WIDTH - 1

    @pl.when(ch == 0)
    def _():
        xp_ref[...] = jnp.zeros(xp_ref.shape, F32)
        xp_ref[CONV_PAD - keep:CONV_PAD, :] = conv0_ref[0]
        h_ref[...] = h0_ref[0]

    xp_ref[CONV_PAD:CONV_PAD + t_valid, :] = xbc_ref[0]
    conv = cb_ref[...]
    for j in range(CONV_WIDTH):
        conv = conv + cw_ref[j:j + 1, :] * xp_ref[CONV_PAD - keep + j:CONV_PAD - keep + j + L, :]
    last = xp_ref[CONV_PAD + t_valid - keep:CONV_PAD + t_valid, :]
    cout_ref[0] = last
    xp_ref[CONV_PAD - keep:CONV_PAD, :] = last
    xc = _silu(conv)

    row = lax.broadcasted_iota(jnp.int32, (L, LANES), 0)
    lane = lax.broadcasted_iota(jnp.int32, (L, LANES), 1)
    if t_valid == L:
        raw = misc_ref[0]
    else:
        ms_ref[...] = jnp.zeros(ms_ref.shape, F32)
        ms_ref[0:t_valid, :] = misc_ref[0]
        raw = ms_ref[...]
    v = raw + dtb_ref[...]
    dt = jnp.maximum(v, 0.0) + jnp.log(1.0 + jnp.exp(-jnp.abs(v)))
    dt = jnp.where((lane < SSM_HEADS) & (row < t_valid), dt, 0.0)
    da = dt * a_ref[...]
    tri = (lax.broadcasted_iota(jnp.int32, (L, L), 1) <= lax.broadcasted_iota(jnp.int32, (L, L), 0))
    trib = tri.astype(BF16)
    acum = sum(_dot(trib, part) for part in _split3(da))
    acum_t = jnp.transpose(acum)
    dt_t = jnp.transpose(dt)
    e_acum = jnp.exp(acum)
    e_last = jnp.exp(acum[L - 1:L, :])
    w_end = jnp.exp(acum[L - 1:L, :] - acum) * dt
    lo = lane < SSM_HEAD_DIM

    ys = []
    for pair in range(HEAD_PAIRS):
        grp = (2 * pair) // (SSM_HEADS // SSM_GROUPS)
        bg = xc[:, SSM_WIDTH + grp * SSM_STATE:SSM_WIDTH + (grp + 1) * SSM_STATE].astype(BF16)
        cg = xc[:, SSM_WIDTH + (SSM_GROUPS + grp) * SSM_STATE:SSM_WIDTH + (SSM_GROUPS + grp + 1) * SSM_STATE].astype(BF16)
        g = _dot_nt(cg, bg)
        xpair = xc[:, pair * LANES:(pair + 1) * LANES]
        y = jnp.zeros((L, LANES), F32)
        for sub in range(2):
            hd = 2 * pair + sub
            seg = acum[:, hd:hd + 1] - acum_t[hd:hd + 1, :]
            m = g * jnp.exp(jnp.where(tri, seg, NEG)) * dt_t[hd:hd + 1, :]
            xm = jnp.where(lo if sub == 0 else ~lo, xpair, 0.0)
            y = y + _dot(m.astype(BF16), xm.astype(BF16))
        col = lambda a: jnp.where(lo, a[:, 2 * pair:2 * pair + 1], a[:, 2 * pair + 1:2 * pair + 2])
        hp = h_ref[pair]
        y = y + _dot_nt(cg, hp.astype(BF16)) * col(e_acum)
        y = y + col(dsk_ref[...]) * xpair
        xw = (xpair * col(w_end)).astype(BF16)
        st = lax.dot_general(xw, bg, (((0,), (0,)), ((), ())), preferred_element_type=F32)
        prow = lax.broadcasted_iota(jnp.int32, (LANES, LANES), 0) < SSM_HEAD_DIM
        dec = jnp.where(prow, e_last[:, 2 * pair:2 * pair + 1], e_last[:, 2 * pair + 1:2 * pair + 2])
        h_ref[pair] = hp * dec + st
        ys.append(y)
    y = jnp.concatenate(ys, axis=1)
    if t_valid != L:
        y = y[:t_valid]
    y = y * _silu(z_ref[0])
    y = y * lax.rsqrt(jnp.mean(y * y, axis=-1, keepdims=True) + EPS) * nw_ref[...]
    y_ref[0] = y

    @pl.when(ch == pl.num_programs(1) - 1)
    def _():
        hout_ref[0] = h_ref[...]


def ssd(xbc, z, misc, conv0, h0, conv_w, conv_b, dt_bias, a_log, d_skip, norm_w):
    b, t, _ = xbc.shape
    L = SSD_CHUNK
    t_valid = L if t % L == 0 else t
    assert t_valid == L or t < L
    n_ch = max(t // L, 1)
    keep = CONV_WIDTH - 1
    pad8 = lambda v: jnp.pad(v.astype(F32), (0, LANES - SSM_HEADS)).reshape(1, LANES)
    dtb = pad8(dt_bias)
    a = pad8(-jnp.exp(a_log.astype(F32)))
    dsk = pad8(d_skip)
    h0p = h0.reshape(b, HEAD_PAIRS, 2 * SSM_HEAD_DIM, SSM_STATE)
    full = lambda arr: pl.BlockSpec(arr.shape, lambda i, c: (0,) * arr.ndim)
    tok = lambda wd: pl.BlockSpec((1, t_valid, wd), lambda i, c: (i, c, 0))
    y, hout, cout = pl.pallas_call(
        functools.partial(_ssd_kernel, t_valid=t_valid),
        grid=(b, n_ch),
        in_specs=[
            tok(CONV_DIM), tok(SSM_WIDTH), tok(LANES),
            pl.BlockSpec((1, keep, CONV_DIM), lambda i, c: (i, 0, 0)),
            pl.BlockSpec((1, HEAD_PAIRS, 2 * SSM_HEAD_DIM, SSM_STATE), lambda i, c: (i, 0, 0, 0)),
            full(conv_w), pl.BlockSpec((1, CONV_DIM), lambda i, c: (0, 0)),
            full(dtb), full(a), full(dsk), pl.BlockSpec((1, SSM_WIDTH), lambda i, c: (0, 0)),
        ],
        out_specs=(
            tok(SSM_WIDTH),
            pl.BlockSpec((1, HEAD_PAIRS, 2 * SSM_HEAD_DIM, SSM_STATE), lambda i, c: (i, 0, 0, 0)),
            pl.BlockSpec((1, keep, CONV_DIM), lambda i, c: (i, 0, 0)),
        ),
        out_shape=(
            jax.ShapeDtypeStruct((b, t, SSM_WIDTH), F32),
            jax.ShapeDtypeStruct((b, HEAD_PAIRS, 2 * SSM_HEAD_DIM, SSM_STATE), F32),
            jax.ShapeDtypeStruct((b, keep, CONV_DIM), F32),
        ),
        scratch_shapes=[
            pltpu.VMEM((CONV_PAD + L, CONV_DIM), F32),
            pltpu.VMEM((HEAD_PAIRS, 2 * SSM_HEAD_DIM, SSM_STATE), F32),
            pltpu.VMEM((L, LANES), F32),
        ],
        compiler_params=_cparams(("arbitrary", "arbitrary")),
        name="ssd",
    )(xbc, z, misc, conv0, h0p, conv_w, conv_b.reshape(1, CONV_DIM), dtb, a, dsk, norm_w.reshape(1, SSM_WIDTH))
    return y, hout.reshape(b, SSM_HEADS, SSM_HEAD_DIM, SSM_STATE), cout


def _split2(x):
    hi = x.astype(BF16)
    return hi, (x - hi.astype(F32)).astype(BF16)


def _merge_kernel(oa_ref, ys_ref, x_ref, g1_ref, sh2_ref, sc2_ref, anw_ref, wo_ref, n2w_ref, wrh_ref, wrl_ref,
                  x1_ref, h2_ref, lg_ref):
    oa = oa_ref[...]
    a = oa * lax.rsqrt(jnp.mean(oa * oa, axis=-1, keepdims=True) + EPS) * anw_ref[...]
    cat = jnp.concatenate([a.astype(BF16), ys_ref[...].astype(BF16)], axis=1)
    x1 = x_ref[...] + _mod(g1_ref) * _dot(cat, wo_ref[...])
    x1_ref[...] = x1
    h2 = x1 * lax.rsqrt(jnp.mean(x1 * x1, axis=-1, keepdims=True) + EPS) * n2w_ref[...]
    h2 = h2 * (1.0 + _mod(sc2_ref)) + _mod(sh2_ref)
    h2_ref[...] = h2.astype(BF16)
    hh, hl = _split2(h2)
    lg_ref[...] = _dot_nt(wrh_ref[...], hh) + _dot_nt(wrh_ref[...], hl) + _dot_nt(wrl_ref[...], hh)


def merge(o_attn, y_ssm, x, mod3, mod_row0, attn_norm_w, wo, norm2_w, w_router, tm):
    b, t, d = x.shape
    n = b * t
    tiles_per_b = t // tm
    wrt = jnp.transpose(w_router)
    wrh, wrl = _split2(wrt)

    def mod_spec(col):
        return _mod_spec(mod3, col, tm, tiles_per_b, mod_row0)

    tok = lambda wd: pl.BlockSpec((tm, wd), lambda i: (i, 0))
    full = lambda a: pl.BlockSpec(a.shape, lambda i: (0,) * a.ndim)
    return pl.pallas_call(
        _merge_kernel,
        grid=(n // tm,),
        in_specs=[tok(ATTN_WIDTH), tok(SSM_WIDTH), tok(d), mod_spec(2), mod_spec(3), mod_spec(4),
                  full(attn_norm_w), full(wo), full(norm2_w), full(wrh), full(wrl)],
        out_specs=(tok(d), tok(d), pl.BlockSpec((N_EXPERTS, tm), lambda i: (0, i))),
        out_shape=(jax.ShapeDtypeStruct((n, d), F32), jax.ShapeDtypeStruct((n, d), BF16),
                   jax.ShapeDtypeStruct((N_EXPERTS, n), F32)),
        compiler_params=_cparams(("arbitrary",)),
        name="merge",
    )(o_attn.reshape(n, ATTN_WIDTH), y_ssm.reshape(n, SSM_WIDTH), x.reshape(n, d), mod3, mod3, mod3,
      attn_norm_w, wo, norm2_w, wrh, wrl)


EXPERTS_PER_GROUP = N_EXPERTS // N_EXPERT_GROUPS


def _first_max(x, ids, axes, n_ids):
    mx = jnp.max(x, axis=axes, keepdims=True)
    return ids == jnp.min(jnp.where(x == mx, ids, n_ids), axis=axes, keepdims=True), mx


def _route_kernel(lg_ref, eb_ref, tri_ref, w_ref, pos_ref, cnt_ref):
    lg = lg_ref[...]
    tn = lg.shape[2]
    scores = jax.nn.sigmoid(lg)
    biased = scores + eb_ref[...]
    sub = lax.broadcasted_iota(jnp.int32, lg.shape, 1)
    grp = lax.broadcasted_iota(jnp.int32, (N_EXPERT_GROUPS, 1, tn), 0)
    eid = lax.broadcasted_iota(jnp.int32, lg.shape, 0) * EXPERTS_PER_GROUP + sub
    hit, m1 = _first_max(biased, sub, 1, EXPERTS_PER_GROUP)
    m2 = jnp.max(jnp.where(hit, -jnp.inf, biased), axis=1, keepdims=True)
    gs = m1 + m2
    keep = jnp.zeros(gs.shape, jnp.bool_)
    for _ in range(TOPK_GROUPS):
        hit, _m = _first_max(gs, grp, 0, N_EXPERT_GROUPS)
        keep = keep | hit
        gs = jnp.where(hit, -jnp.inf, gs)
    x = jnp.where(keep, biased, NEG)
    sel = jnp.zeros(lg.shape, jnp.bool_)
    for _ in range(TOP_K):
        hit, _m = _first_max(x, eid, (0, 1), N_EXPERTS)
        sel = sel | hit
        x = jnp.where(hit, -jnp.inf, x)
    w = jnp.where(sel, scores, 0.0)
    w = w / jnp.sum(w, axis=(0, 1), keepdims=True) * ROUTED_SCALE
    w_ref[...] = w
    selb = sel.astype(BF16).reshape(N_EXPERTS, tn)
    pos = _dot(selb, tri_ref[...])
    pos_ref[...] = jnp.where(sel, pos.reshape(lg.shape), -1.0)
    cnt = jnp.sum(sel.astype(F32), axis=2, keepdims=True)
    cnt_ref[0] = jnp.broadcast_to(cnt, cnt_ref.shape[1:]).astype(jnp.int32)


def route(logits_t, e_bias, tn):
    n = logits_t.shape[1]
    lg3 = logits_t.reshape(N_EXPERT_GROUPS, EXPERTS_PER_GROUP, n)
    eb = e_bias.astype(F32).reshape(N_EXPERT_GROUPS, EXPERTS_PER_GROUP, 1)
    tri = jnp.asarray(np.triu(np.ones((tn, tn), np.float32), 1), BF16)
    blk = pl.BlockSpec((N_EXPERT_GROUPS, EXPERTS_PER_GROUP, tn), lambda i: (0, 0, i))
    w, pos, cnt = pl.pallas_call(
        _route_kernel,
        grid=(n // tn,),
        in_specs=[blk, pl.BlockSpec(eb.shape, lambda i: (0, 0, 0)), pl.BlockSpec((tn, tn), lambda i: (0, 0))],
        out_specs=(blk, blk, pl.BlockSpec((1, N_EXPERT_GROUPS, EXPERTS_PER_GROUP, LANES), lambda i: (i, 0, 0, 0))),
        out_shape=(jax.ShapeDtypeStruct(lg3.shape, F32), jax.ShapeDtypeStruct(lg3.shape, F32),
                   jax.ShapeDtypeStruct((n // tn, N_EXPERT_GROUPS, EXPERTS_PER_GROUP, LANES), jnp.int32)),
        compiler_params=_cparams(("arbitrary",)),
        name="route",
    )(lg3, eb, tri)
    return w.reshape(N_EXPERTS, n), pos.reshape(N_EXPERTS, n), cnt[..., 0].reshape(n // tn, N_EXPERTS)


MOE_ROWS = 128


def _swiglu(xb, wgu, wd, width):
    gu = _dot(xb, wgu)
    act = _silu(gu[:, :width]) * gu[:, width:]
    return _dot(act.astype(BF16), wd)


MOE_EXPERTS_PER_STEP = 4


def _moe_kernel(cnt_ref, h2_ref, w_ref, pos_ref, x1_ref, g2_ref, wgu_ref, wd_ref, sgu_ref, sd_ref,
                o_ref, acc_ref, g_buf, o_buf):
    i = pl.program_id(0)
    es = pl.program_id(1)
    tm = h2_ref.shape[0]
    tn = (((0,), (0,)), ((), ()))

    @pl.when(es == 0)
    def _():
        acc_ref[...] = _swiglu(h2_ref[...], sgu_ref[...], sd_ref[...], D_SHARED)

    slot = lax.broadcasted_iota(jnp.int32, (MOE_ROWS, tm), 0).astype(F32)
    for q in range(MOE_EXPERTS_PER_STEP):
        e = es * MOE_EXPERTS_PER_STEP + q
        cnt = cnt_ref[i * N_EXPERTS + e]
        pos = pos_ref[pl.ds(e, 1), :]
        wrow = w_ref[pl.ds(e, 1), :]

        def window(first_slot, q=q, pos=pos, wrow=wrow):
            hit = pos == slot + first_slot
            g = hit.astype(BF16)
            xg = _dot(g, h2_ref[...]).astype(BF16)
            out = _swiglu(xg, wgu_ref[q], wd_ref[q], D_EXPERT)
            out = out * jnp.sum(jnp.where(hit, wrow, 0.0), axis=1, keepdims=True)
            return g, out.astype(BF16)

        g, out = window(0.0)
        g_buf[q * MOE_ROWS:(q + 1) * MOE_ROWS, :] = g
        o_buf[q * MOE_ROWS:(q + 1) * MOE_ROWS, :] = out

        def extra(j, carry, window=window):
            g, out = window((j * MOE_ROWS).astype(F32))
            acc_ref[...] += lax.dot_general(g, out, tn, preferred_element_type=F32)
            return carry

        lax.fori_loop(1, (cnt + MOE_ROWS - 1) // MOE_ROWS, extra, 0)

    acc_ref[...] += lax.dot_general(g_buf[...], o_buf[...], tn, preferred_element_type=F32)

    @pl.when(es == pl.num_programs(1) - 1)
    def _():
        o_ref[...] = x1_ref[...] + _mod(g2_ref) * acc_ref[...]


def moe(h2, w_t, pos_t, counts, x1, mod3, mod_row0, t_per_b, wgu, wd, sgu, sd, tm):
    n, d = h2.shape
    tiles_per_b = t_per_b // tm
    eps = MOE_EXPERTS_PER_STEP
    grid_spec = pltpu.PrefetchScalarGridSpec(
        num_scalar_prefetch=1,
        grid=(n // tm, N_EXPERTS // eps),
        in_specs=[
            pl.BlockSpec((tm, d), lambda i, e, *_: (i, 0)),
            pl.BlockSpec((N_EXPERTS, tm), lambda i, e, *_: (0, i)),
            pl.BlockSpec((N_EXPERTS, tm), lambda i, e, *_: (0, i)),
            pl.BlockSpec((tm, d), lambda i, e, *_: (i, 0)),
            _mod_spec(mod3, 5, tm, tiles_per_b, mod_row0),
            pl.BlockSpec((eps, d, 2 * D_EXPERT), lambda i, e, *_: (e, 0, 0)),
            pl.BlockSpec((eps, D_EXPERT, d), lambda i, e, *_: (e, 0, 0)),
            pl.BlockSpec(sgu.shape, lambda i, e, *_: (0, 0)),
            pl.BlockSpec(sd.shape, lambda i, e, *_: (0, 0)),
        ],
        out_specs=pl.BlockSpec((tm, d), lambda i, e, *_: (i, 0)),
        scratch_shapes=[pltpu.VMEM((tm, d), F32), pltpu.VMEM((eps * MOE_ROWS, tm), BF16),
                        pltpu.VMEM((eps * MOE_ROWS, d), BF16)],
    )
    return pl.pallas_call(
        _moe_kernel,
        grid_spec=grid_spec,
        out_shape=jax.ShapeDtypeStruct((n, d), F32),
        compiler_params=_cparams(("arbitrary", "arbitrary")),
        name="moe",
    )(counts.reshape(-1), h2, w_t, pos_t, x1, mod3, wgu, wd, sgu, sd)


GATHER_PAGES = 8


def _gather_kernel(pt_ref, *refs):
    pages, new_ref = refs[:GATHER_PAGES], refs[GATHER_PAGES]
    rows_ref, cmpx_ref, stage_ref = refs[GATHER_PAGES + 1:]
    step = pl.program_id(1)
    last = pl.num_programs(1) - 1
    n_rows = GATHER_PAGES * PAGE_SIZE

    @pl.when(step < last)
    def _():
        for k in range(GATHER_PAGES):
            sl = slice(k * PAGE_SIZE, (k + 1) * PAGE_SIZE)
            for r in range(4):
                tile = jnp.transpose(pages[k][0, r])
                if r < 2:
                    stage_ref[r, sl, :] = tile
                else:
                    rows_ref[0, sl, (r - 2) * KV_WIDTH:(r - 1) * KV_WIDTH] = tile.astype(BF16)

    @pl.when(step == last)
    def _():
        new = new_ref[0]
        tn = new.shape[0]
        stage_ref[...] = jnp.zeros(stage_ref.shape, F32)
        for s in range(2):
            stage_ref[s, 0:tn, :] = new[:, s * KV_WIDTH:(s + 1) * KV_WIDTH]
        pad = jnp.zeros((n_rows - tn, 2 * KV_WIDTH), F32)
        rows_ref[0] = jnp.concatenate([new[:, 2 * KV_WIDTH:], pad], axis=0).astype(BF16)

    _stride_block_store(stage_ref, cmpx_ref, n_rows)


def gather_pages(cache_t, page_table, new_rows):
    b, n_pages = page_table.shape
    steps = n_pages // GATHER_PAGES
    rows = GATHER_PAGES * PAGE_SIZE
    s_out = (steps + 1) * rows

    def page_spec(k):
        def idx(i, s, pt):
            p = jnp.minimum(s, steps - 1) * GATHER_PAGES + k
            return (pt[i * n_pages + p], 0, 0, 0)
        return pl.BlockSpec((1, 4, KV_WIDTH, PAGE_SIZE), idx)

    grid_spec = pltpu.PrefetchScalarGridSpec(
        num_scalar_prefetch=1,
        grid=(b, steps + 1),
        in_specs=[page_spec(k) for k in range(GATHER_PAGES)]
        + [pl.BlockSpec((1,) + new_rows.shape[1:], lambda i, s, pt: (i, 0, 0))],
        out_specs=(
            pl.BlockSpec((1, rows, 2 * KV_WIDTH), lambda i, s, pt: (i, s, 0)),
            pl.BlockSpec((1, rows // CMP_STRIDE, CMP_STRIDE * 2 * KV_WIDTH), lambda i, s, pt: (i, s, 0)),
        ),
        scratch_shapes=[pltpu.VMEM((2, rows, KV_WIDTH), F32)],
    )
    return pl.pallas_call(
        _gather_kernel,
        grid_spec=grid_spec,
        out_shape=(jax.ShapeDtypeStruct((b, s_out, 2 * KV_WIDTH), BF16),
                   jax.ShapeDtypeStruct((b, s_out // CMP_STRIDE, CMP_STRIDE * 2 * KV_WIDTH), BF16)),
        compiler_params=_cparams(("arbitrary", "arbitrary")),
        name="gather_pages",
    )(page_table.reshape(-1), *([cache_t] * GATHER_PAGES), new_rows)


def _attention(qp, cmpx, kvb, sel_col, winb, misc, cmp_w, q_off, win_pos0, tq):
    t = qp.shape[1]
    cur_lo, cur_hi = q_off // SEL_BLOCK, (q_off + t - 1) // SEL_BLOCK
    assert cur_hi < N_SEL_LANES or (cur_lo == cur_hi == N_SEL_LANES), (q_off, t)
    n_pick = N_SEL - (1 if cur_hi >= N_SEL_LANES else 0)
    kcv = compress(cmpx, *cmp_w)
    o_cmp, mneg = cmp_select(qp, kcv, q_off, n_pick, tq)
    return sel_win_attention(qp, mneg, kvb, sel_col, winb, o_cmp, misc, q_off, win_pos0, tq)


def kernel(x_prompt, x_sample, cache_kv, cache_win, state_ssm, state_conv, page_table, c_prompt, c_sample, w_ada, b_ada, norm1_w, norm2_w, w_in, q_norm_w, k_norm_w, cmp_pe, cmp_w1, cmp_w2, attn_out_norm_w, conv_w, conv_b, dt_bias, a_log, d_skip, ssm_norm_w, w_out, w_router, e_bias, w_exp_gu, w_exp_down, w_sh_gu, w_sh_down):
    xp, xq = x_prompt, x_sample
    bp, tp, d = xp.shape
    bq, tq, _ = xq.shape
    depth = w_ada.shape[0]
    past_len = page_table.shape[1] * PAGE_SIZE
    nq = bq * tq
    tq_pad = LANES // GQA_GROUP
    assert tp % 512 == 0 and tp >= WINDOW and nq % 8 == 0 and tq <= tq_pad
    pos_p = jnp.arange(tp, dtype=jnp.int32)
    pos_q = jnp.tile(past_len + jnp.arange(tq, dtype=jnp.int32), bq)
    c_all = jnp.concatenate([c_prompt, c_sample], axis=0)
    c_all = jnp.pad(c_all, ((0, -c_all.shape[0] % 8), (0, 0)))
    outs = [[] for _ in range(8)]
    for l in range(depth):
        mod = adaln_all(c_all, w_ada[l], b_ada[l])
        mod_p = mod.reshape(mod.shape[0], 1, 6 * d)
        mod_q = jnp.repeat(mod[bp:bp + bq], tq, axis=0)
        wp = _prep_w_in(w_in[l])
        cmp_w = _prep_compress(cmp_pe[l], cmp_w1[l], cmp_w2[l])
        wo = w_out[l].astype(BF16)
        wgu, wd = w_exp_gu[l].astype(BF16), w_exp_down[l].astype(BF16)
        sgu, sd = w_sh_gu[l].astype(BF16), w_sh_down[l].astype(BF16)
        ssm_w = (conv_w[l], conv_b[l], dt_bias[l], a_log[l], d_skip[l], ssm_norm_w[l])
        n1w, n2w, anw = norm1_w[l:l + 1], norm2_w[l:l + 1], attn_out_norm_w[l:l + 1]

        qp, kvb, win, winb, z, xbc, misc, kvt, cmpx = inproj(xp, mod_p, 0, n1w, wp, q_norm_w[l], k_norm_w[l], pos_p,
                                                            512, True)
        r3 = lambda a: a.reshape(bp, tp, a.shape[-1])
        o_attn = _attention(r3(qp), cmpx, r3(kvb), 2, r3(winb), r3(misc), cmp_w, 0, 0, 128)
        y_ssm, h_new, conv_new = ssd(r3(xbc), r3(z), r3(misc), jnp.zeros((bp, CONV_WIDTH - 1, CONV_DIM), F32),
                                     jnp.zeros((bp, SSM_HEADS, SSM_HEAD_DIM, SSM_STATE), F32), *ssm_w)
        x1, h2, lg = merge(o_attn, y_ssm, xp, mod_p, 0, anw, wo, n2w, w_router[l], 512)
        w_t, pos_t, cnt = route(lg, e_bias[l], 512)
        xp = moe(h2, w_t, pos_t, cnt, x1, mod_p, 0, tp, wgu, wd, sgu, sd, 512).reshape(bp, tp, d)
        outs[0].append(jnp.transpose(kvt.reshape(bp, 4, N_KV_HEADS, HEAD_DIM, tp), (0, 4, 1, 2, 3)))
        outs[1].append(win.reshape(bp, tp, 2, N_KV_HEADS, HEAD_DIM)[:, tp - WINDOW:])
        outs[2].append(h_new)
        outs[3].append(conv_new)

        xq1 = xq.reshape(1, nq, d)
        qp, kvb, win, winb, z, xbc, misc, kv = inproj(xq1, mod_q, 0, n1w, wp, q_norm_w[l], k_norm_w[l], pos_q, nq,
                                                      False)
        rq = lambda a: a.reshape(bq, tq, a.shape[-1])
        padq = lambda a: jnp.pad(rq(a), ((0, 0), (0, tq_pad - tq), (0, 0)))
        cache_t = jnp.transpose(cache_kv[l], (0, 2, 3, 4, 1)).reshape(cache_kv.shape[1], 4, KV_WIDTH, PAGE_SIZE)
        past, cmpx = gather_pages(cache_t, page_table, rq(kv))
        win_all = jnp.concatenate([cache_win[l].reshape(bq, WINDOW, 2 * KV_WIDTH).astype(BF16), rq(winb),
                                   jnp.zeros((bq, -(WINDOW + tq_pad) % WIN_CHUNK + tq_pad - tq, 2 * KV_WIDTH), BF16)],
                                  axis=1)
        o_attn = _attention(padq(qp), cmpx, past, 0, win_all, padq(misc), cmp_w, past_len, past_len - WINDOW,
                            tq_pad)[:, :tq]
        y_ssm, h_new, conv_new = ssd(rq(xbc), rq(z), rq(misc), state_conv[l], state_ssm[l], *ssm_w)
        x1, h2, lg = merge(o_attn.reshape(1, nq, ATTN_WIDTH), y_ssm.reshape(1, nq, SSM_WIDTH), xq1, mod_q, 0,
                           anw, wo, n2w, w_router[l], nq)
        w_t, pos_t, cnt = route(lg, e_bias[l], nq)
        xq = moe(h2, w_t, pos_t, cnt, x1, mod_q, 0, nq, wgu, wd, sgu, sd, nq).reshape(bq, tq, d)
        win_rows = win.reshape(bq, tq, 2, N_KV_HEADS, HEAD_DIM)
        outs[4].append(kv.reshape(bq, tq, 4, N_KV_HEADS, HEAD_DIM))
        outs[5].append(jnp.concatenate([cache_win[l], win_rows.astype(cache_win.dtype)], axis=1)[:, tq:])
        outs[6].append(h_new)
        outs[7].append(conv_new)
    return (xp, xq) + tuple(jnp.stack(o) for o in outs)
```

```python
import functools
import math

import jax
import jax.numpy as jnp
import numpy as np
from jax import lax
from jax.experimental import pallas as pl
from jax.experimental.pallas import tpu as pltpu

D_MODEL = 1024
PAGE_SIZE = 128
HEAD_DIM = 64
N_Q_HEADS = 8
N_KV_HEADS = 2
GQA_GROUP = N_Q_HEADS // N_KV_HEADS
ATTN_WIDTH = N_Q_HEADS * HEAD_DIM
KV_WIDTH = N_KV_HEADS * HEAD_DIM
ROPE_DIM = HEAD_DIM // 4
ROPE_THETA = 500000.0
CMP_LEN = 32
CMP_STRIDE = 16
CMP_HIDDEN = 4 * HEAD_DIM
SEL_BLOCK = 64
N_SEL = 16
N_LOCAL = 2
WINDOW = 512
SSM_HEADS = 8
SSM_HEAD_DIM = 64
SSM_WIDTH = SSM_HEADS * SSM_HEAD_DIM
SSM_GROUPS = 2
SSM_STATE = 128
CONV_WIDTH = 4
CONV_DIM = SSM_WIDTH + 2 * SSM_GROUPS * SSM_STATE
SSD_CHUNK = 128
MIX_WIDTH = ATTN_WIDTH + SSM_WIDTH
N_EXPERTS = 64
N_EXPERT_GROUPS = 8
TOPK_GROUPS = 4
TOP_K = 8
D_EXPERT = 256
D_SHARED = 256
ROUTED_SCALE = 2.5
IN_SIZES = (ATTN_WIDTH, 6 * KV_WIDTH, 3 * N_Q_HEADS, SSM_WIDTH, CONV_DIM, SSM_HEADS)
N_IN = sum(IN_SIZES)
EPS = 1e-6
NEG = -1e30
BIG = 1e6

LANES = 128
VMEM_LIMIT = 56 * 1024 * 1024

BF16 = jnp.bfloat16
F32 = jnp.float32
LOG2E = math.log2(math.e)


def _cparams(sem, flags=None):
    return pltpu.CompilerParams(dimension_semantics=sem, vmem_limit_bytes=VMEM_LIMIT, flags=flags)


def _silu(x):
    return x * jax.nn.sigmoid(x)


def _dot(a, b):
    return jnp.dot(a, b, preferred_element_type=F32)


def _dot_nt(a, b):
    return lax.dot_general(a, b, (((1,), (1,)), ((), ())), preferred_element_type=F32)


def _mod_spec(mod, col, tm, tiles_per_b, row0):
    if mod.ndim == 3:
        return pl.BlockSpec((1, 1, D_MODEL), lambda i, *_: (row0 + i // tiles_per_b, 0, col))
    return pl.BlockSpec((tm, D_MODEL), lambda i, *_: (i, col))


def _mod(ref):
    return ref[0] if len(ref.shape) == 3 else ref[...]


def _adaln_kernel(c_ref, w_ref, b_ref, o_ref):
    c = c_ref[...]
    a = _silu(c).astype(BF16)
    o_ref[...] = _dot(a, w_ref[...].astype(BF16)) + b_ref[...]


def adaln_all(c_all, w_ada, b_ada):
    rows = c_all.shape[0]
    n = w_ada.shape[1]
    tn = 1024
    return pl.pallas_call(
        _adaln_kernel,
        grid=(n // tn,),
        in_specs=[
            pl.BlockSpec((rows, D_MODEL), lambda j: (0, 0)),
            pl.BlockSpec((D_MODEL, tn), lambda j: (0, j)),
            pl.BlockSpec((1, tn), lambda j: (0, j)),
        ],
        out_specs=pl.BlockSpec((rows, tn), lambda j: (0, j)),
        out_shape=jax.ShapeDtypeStruct((rows, n), F32),
        compiler_params=_cparams(("arbitrary",)),
        name="adaln",
    )(c_all, w_ada, b_ada.reshape(1, n))


_C_Q = 0
_C_KV = _C_Q + ATTN_WIDTH
_C_Z = _C_KV + 6 * KV_WIDTH
_C_XBC = _C_Z + SSM_WIDTH
_C_MISC = _C_XBC + CONV_DIM
N_IN_PAD = _C_MISC + LANES
N_GATES = 3 * N_Q_HEADS


def _prep_w_in(w_in):
    s = np.cumsum((0,) + IN_SIZES)
    q, kv, g, z, xbc, dt = (w_in[:, int(s[i]):int(s[i + 1])] for i in range(6))
    pad = jnp.zeros((w_in.shape[0], LANES - N_GATES - SSM_HEADS), w_in.dtype)
    return jnp.concatenate([q, kv, z, xbc, dt, g, pad], axis=1).astype(BF16)


def _group_mean_matrix(width):
    i = np.arange(width)
    m = (i[:, None] // HEAD_DIM == i[None, :] // HEAD_DIM).astype(np.float32) / HEAD_DIM
    return jnp.asarray(m, BF16)


def _rope_tables(pos):
    half = ROPE_DIM // 2
    inv_freq = ROPE_THETA ** (-jnp.arange(half, dtype=F32) / half)
    ang = pos.astype(F32)[:, None] * inv_freq[None, :]
    cos, sin = jnp.cos(ang), jnp.sin(ang)
    t = pos.shape[0]
    one = jnp.ones((t, HEAD_DIM - ROPE_DIM), F32)
    zero = jnp.zeros((t, HEAD_DIM - ROPE_DIM), F32)
    zh = jnp.zeros((t, half), F32)
    c = jnp.concatenate([cos, cos, one], axis=1)
    s_up = jnp.concatenate([-sin, zh, zero], axis=1)
    s_dn = jnp.concatenate([zh, sin, zero], axis=1)
    rep = LANES // HEAD_DIM
    return jnp.tile(c, (1, rep)), jnp.tile(s_up, (1, rep)), jnp.tile(s_dn, (1, rep))


def _rope(x, c, s_up, s_dn):
    w = x.shape[1]
    half = ROPE_DIM // 2
    rep = w // LANES
    ct = jnp.concatenate([c] * rep, axis=1) if rep > 1 else c
    su = jnp.concatenate([s_up] * rep, axis=1) if rep > 1 else s_up
    sd = jnp.concatenate([s_dn] * rep, axis=1) if rep > 1 else s_dn
    up = pltpu.roll(x, w - half, axis=1)
    dn = pltpu.roll(x, half, axis=1)
    return x * ct + up * su + dn * sd


def _stride_block_store(stage_ref, cmpx_ref, n_rows):
    for j in range(CMP_STRIDE):
        for s in range(2):
            rows_j = stage_ref[s, pl.ds(j, n_rows // CMP_STRIDE, stride=CMP_STRIDE), :]
            c0 = (2 * j + s) * KV_WIDTH
            cmpx_ref[0, :, c0:c0 + KV_WIDTH] = rows_j.astype(BF16)


def _inproj_kernel(x_ref, shift_ref, scale_ref, nw_ref, w_ref, qw_ref, kw_ref, gq_ref, gk_ref,
                   c_ref, su_ref, sd_ref,
                   qp_ref, kvb_ref, win_ref, winb_ref, z_ref, xbc_ref, misc_ref, *rest, seq_layout):
    x = x_ref[...]
    ms = jnp.mean(x * x, axis=-1, keepdims=True)
    h = x * lax.rsqrt(ms + EPS) * nw_ref[...]
    h = h * (1.0 + _mod(scale_ref)) + _mod(shift_ref)
    hb = h.astype(BF16)
    c, su, sd = c_ref[...], su_ref[...], sd_ref[...]

    q = _dot(hb, w_ref[:, _C_Q:_C_Q + ATTN_WIDTH])
    qms = _dot((q * q).astype(BF16), gq_ref[...])
    q = q * lax.rsqrt(qms + EPS) * qw_ref[...]
    q = _rope(q, c, su, sd) * (HEAD_DIM ** -0.5 * LOG2E)
    lane = lax.broadcasted_iota(jnp.int32, q.shape, 1) % LANES
    lo = lane < HEAD_DIM
    q_up = pltpu.roll(q, ATTN_WIDTH - HEAD_DIM, axis=1)
    q_dn = pltpu.roll(q, HEAD_DIM, axis=1)
    zero = jnp.zeros_like(q)
    nat_lo = jnp.where(lo, q, zero)
    nat_hi = jnp.where(lo, zero, q)
    up_lo = jnp.where(lo, q_up, zero)
    dn_hi = jnp.where(lo, zero, q_dn)
    blocks = []
    for hd in range(N_Q_HEADS):
        pair = hd // 2
        sl = slice(pair * LANES, (pair + 1) * LANES)
        if hd < GQA_GROUP:
            blocks.append((nat_lo if hd % 2 == 0 else up_lo)[:, sl])
        else:
            blocks.append((dn_hi if hd % 2 == 0 else nat_hi)[:, sl])
    qp_ref[...] = jnp.concatenate(blocks, axis=1).astype(BF16)

    kv = _dot(hb, w_ref[:, _C_KV:_C_KV + 6 * KV_WIDTH])
    outs = []
    for br in range(3):
        k = kv[:, br * 2 * KV_WIDTH:br * 2 * KV_WIDTH + KV_WIDTH]
        v = kv[:, br * 2 * KV_WIDTH + KV_WIDTH:(br + 1) * 2 * KV_WIDTH]
        kms = _dot((k * k).astype(BF16), gk_ref[...])
        k = k * lax.rsqrt(kms + EPS) * kw_ref[:, br * KV_WIDTH:(br + 1) * KV_WIDTH]
        k = _rope(k, c, su, sd)
        outs += [k, v]
    kvrows = jnp.concatenate(outs[:4], axis=1)
    winrows = jnp.concatenate(outs[4:], axis=1)
    kvb_ref[...] = kvrows.astype(BF16)
    win_ref[...] = winrows
    winb_ref[...] = winrows.astype(BF16)
    if seq_layout:
        kvt_ref, cmpx_ref, stage_ref = rest
        tm = kvrows.shape[0]
        for r in range(4):
            kvt_ref[0, r] = jnp.transpose(kvrows[:, r * KV_WIDTH:(r + 1) * KV_WIDTH])
        for s in range(2):
            stage_ref[s] = kvrows[:, s * KV_WIDTH:(s + 1) * KV_WIDTH]
        _stride_block_store(stage_ref, cmpx_ref, tm)
    else:
        rest[0][...] = kvrows

    z_ref[...] = _dot(hb, w_ref[:, _C_Z:_C_Z + SSM_WIDTH])
    xbc_ref[...] = _dot(hb, w_ref[:, _C_XBC:_C_XBC + CONV_DIM])
    misc_ref[...] = _dot(hb, w_ref[:, _C_MISC:_C_MISC + LANES])


def inproj(x, mod3, mod_row0, norm_w, wp, q_norm_w, k_norm_w, pos, tm, seq_layout):
    b, t, d = x.shape
    n = b * t
    tiles_per_b = t // tm
    xf = x.reshape(n, d)
    c, su, sd = _rope_tables(pos)
    qw = jnp.tile(q_norm_w, N_Q_HEADS).reshape(1, ATTN_WIDTH)
    kw = jnp.concatenate([jnp.tile(k_norm_w[i], N_KV_HEADS) for i in range(3)]).reshape(1, 3 * KV_WIDTH)
    gq = _group_mean_matrix(ATTN_WIDTH)
    gk = _group_mean_matrix(KV_WIDTH)

    def mod_spec(col):
        return _mod_spec(mod3, col, tm, tiles_per_b, mod_row0)

    def tok(wd):
        return pl.BlockSpec((tm, wd), lambda i: (i, 0))

    def full(a):
        return pl.BlockSpec(a.shape, lambda i: (0,) * a.ndim)

    rope_spec = pl.BlockSpec((tm, LANES), lambda i: (i % tiles_per_b, 0))
    out_shape = [
        jax.ShapeDtypeStruct((n, N_Q_HEADS * LANES), BF16),
        jax.ShapeDtypeStruct((n, 4 * KV_WIDTH), BF16),
        jax.ShapeDtypeStruct((n, 2 * KV_WIDTH), F32),
        jax.ShapeDtypeStruct((n, 2 * KV_WIDTH), BF16),
        jax.ShapeDtypeStruct((n, SSM_WIDTH), F32),
        jax.ShapeDtypeStruct((n, CONV_DIM), F32),
        jax.ShapeDtypeStruct((n, LANES), F32),
    ]
    out_specs = [tok(s.shape[1]) for s in out_shape]
    scratch = []
    if seq_layout:
        out_shape += [jax.ShapeDtypeStruct((b, 4, KV_WIDTH, t), F32),
                      jax.ShapeDtypeStruct((b, t // CMP_STRIDE, CMP_STRIDE * 2 * KV_WIDTH), BF16)]
        out_specs += [pl.BlockSpec((1, 4, KV_WIDTH, tm), lambda i: (i // tiles_per_b, 0, 0, i % tiles_per_b)),
                      pl.BlockSpec((1, tm // CMP_STRIDE, CMP_STRIDE * 2 * KV_WIDTH),
                                   lambda i: (i // tiles_per_b, i % tiles_per_b, 0))]
        scratch = [pltpu.VMEM((2, tm, KV_WIDTH), F32)]
    else:
        out_shape += [jax.ShapeDtypeStruct((n, 4 * KV_WIDTH), F32)]
        out_specs += [tok(4 * KV_WIDTH)]
    return pl.pallas_call(
        functools.partial(_inproj_kernel, seq_layout=seq_layout),
        grid=(n // tm,),
        in_specs=[tok(d), mod_spec(0), mod_spec(1), full(norm_w), full(wp), full(qw), full(kw), full(gq), full(gk),
                  rope_spec, rope_spec, rope_spec],
        out_specs=tuple(out_specs),
        out_shape=tuple(out_shape),
        scratch_shapes=scratch,
        compiler_params=_cparams(("arbitrary",)),
        name="inproj",
    )(xf, mod3, mod3, norm_w, wp, qw, kw, gq, gk, c, su, sd)


def _prep_compress(cmp_pe, cmp_w1, cmp_w2):
    half = CMP_LEN // 2
    eye = jnp.eye(N_KV_HEADS, dtype=F32)
    w1 = cmp_w1.reshape(2, CMP_LEN, HEAD_DIM, CMP_HIDDEN)
    w1s = []
    for part in (w1[:, :half], w1[:, half:]):
        w1s.append(jnp.einsum("pjdo,hg->pjhdgo", part, eye).reshape(2, half * KV_WIDTH, N_KV_HEADS * CMP_HIDDEN))
    w1p = jnp.concatenate(w1s, axis=2).astype(BF16)
    pe = cmp_pe.reshape(2, 2, half, 1, HEAD_DIM)
    pep = jnp.broadcast_to(pe, (2, 2, half, N_KV_HEADS, HEAD_DIM)).reshape(2, 2, half * KV_WIDTH)
    w2p = jnp.einsum("poe,hg->phoge", cmp_w2, eye).reshape(2, N_KV_HEADS * CMP_HIDDEN, KV_WIDTH).astype(BF16)
    return w1p, pep, w2p


def _compress_kernel(x_ref, w1_ref, pe_ref, w2_ref, o_ref, *, row_w):
    part = pl.program_id(1)
    nb = x_ref.shape[1]
    half = CMP_LEN // 2
    hid = N_KV_HEADS * CMP_HIDDEN
    cols = []
    for j in range(half):
        a = x_ref[0, :, j * row_w:j * row_w + KV_WIDTH]
        b = x_ref[0, :, j * row_w + KV_WIDTH:j * row_w + 2 * KV_WIDTH]
        cols.append(jnp.where(part == 0, a, b))
    x = jnp.concatenate(cols, axis=1).astype(F32)
    pe = pe_ref[0]
    u = _dot((x + pe[0:1]).astype(BF16), w1_ref[0, :, :hid])
    v = _dot((x + pe[1:2]).astype(BF16), w1_ref[0, :, hid:])
    h1 = u + pltpu.roll(v, nb - 1, axis=0)
    out = _dot(_silu(h1).astype(BF16), w2_ref[0])
    row = lax.broadcasted_iota(jnp.int32, out.shape, 0)
    o_ref[0, 0] = jnp.where(row < nb - 1, out, 0.0).astype(o_ref.dtype)


def compress(x, w1p, pep, w2p):
    b, nb, width = x.shape
    row_w = width // CMP_STRIDE
    return pl.pallas_call(
        functools.partial(_compress_kernel, row_w=row_w),
        grid=(b, 2),
        in_specs=[
            pl.BlockSpec((1, nb, CMP_STRIDE * row_w), lambda i, p: (i, 0, 0)),
            pl.BlockSpec((1,) + w1p.shape[1:], lambda i, p: (p, 0, 0)),
            pl.BlockSpec((1,) + pep.shape[1:], lambda i, p: (p, 0, 0)),
            pl.BlockSpec((1,) + w2p.shape[1:], lambda i, p: (p, 0, 0)),
        ],
        out_specs=pl.BlockSpec((1, 1, nb, KV_WIDTH), lambda i, p: (i, p, 0, 0)),
        out_shape=jax.ShapeDtypeStruct((b, 2, nb, KV_WIDTH), BF16),
        compiler_params=_cparams(("arbitrary", "arbitrary")),
        name="compress",
    )(x, w1p, pep, w2p)


N_SEL_LANES = LANES


def _cover_matrix(nb):
    c = np.arange(nb)[:, None]
    j = np.arange(N_SEL_LANES)[None, :]
    start = c * CMP_STRIDE
    m = (start < (j + 1) * SEL_BLOCK) & (start + CMP_LEN > j * SEL_BLOCK)
    return jnp.asarray(m.astype(np.float32), BF16)


def _place_heads(res, kv):
    lane = lax.broadcasted_iota(jnp.int32, res[0].shape, 1)
    lo = lane < HEAD_DIM
    blocks = []
    for pair in range(GQA_GROUP // 2):
        a, b = res[2 * pair], res[2 * pair + 1]
        if kv == 0:
            blocks.append(jnp.where(lo, a, pltpu.roll(b, HEAD_DIM, axis=1)))
        else:
            blocks.append(jnp.where(lo, pltpu.roll(a, HEAD_DIM, axis=1), b))
    return jnp.concatenate(blocks, axis=1)


def _group_rows(q_ref, kv):
    heads = range(kv * GQA_GROUP, (kv + 1) * GQA_GROUP)
    return jnp.concatenate([q_ref[0, :, hd * LANES:(hd + 1) * LANES] for hd in heads], axis=0)


def _heads_from_transposed(out_t, tq, kv):
    out = jnp.transpose(out_t)
    return _place_heads([out[g * tq:(g + 1) * tq] for g in range(GQA_GROUP)], kv)


def _cmp_select_kernel(q_ref, kc_ref, vc_ref, covt_ref, o_ref, m_ref, *, q_off, n_pick):
    tq = q_ref.shape[1]
    rows = GQA_GROUP * tq
    nb = kc_ref.shape[2]
    wl = max(tq, LANES)
    assert tq % LANES == 0 or rows == LANES
    t0 = q_off + pl.program_id(1) * tq
    kc = kc_ref[0, 0]
    vc = vc_ref[0, 0]
    qpos = t0 + lax.broadcasted_iota(jnp.int32, (nb, rows), 1) % tq
    cend = lax.broadcasted_iota(jnp.int32, (nb, rows), 0) * CMP_STRIDE + (CMP_LEN - 1)
    valid = cend <= qpos
    blk = lax.broadcasted_iota(jnp.int32, (N_SEL_LANES, wl), 0)
    cur = (t0 + lax.broadcasted_iota(jnp.int32, (N_SEL_LANES, wl), 1) % tq) // SEL_BLOCK
    forced = (blk == 0) | ((blk <= cur) & (blk > cur - N_LOCAL))
    o_groups = []
    for kv in range(N_KV_HEADS):
        s = _dot_nt(kc, _group_rows(q_ref, kv))
        s = jnp.where(valid, s, NEG)
        e = jnp.exp2(s - jnp.max(s, axis=0, keepdims=True))
        p = e / jnp.sum(e, axis=0, keepdims=True)
        p = jnp.where(valid, p, 0.0)
        o_t = lax.dot_general(vc, p.astype(BF16), (((0,), (0,)), ((), ())), preferred_element_type=F32)
        o_groups.append(_heads_from_transposed(o_t, tq, kv))
        if tq % LANES == 0:
            psum = sum(p[:, g * tq:(g + 1) * tq] for g in range(GQA_GROUP))
        else:
            psum = p + sum(pltpu.roll(p, g * tq, axis=1) for g in range(1, GQA_GROUP))
        hi, lo = _split2(psum)
        imp = _dot(covt_ref[...], hi) + _dot(covt_ref[...], lo)
        x = jnp.where(forced, BIG, jnp.where(blk > cur, -BIG, imp))
        sel = jnp.zeros(x.shape, jnp.bool_)
        for _ in range(n_pick):
            mx = jnp.max(x, axis=0, keepdims=True)
            idx = jnp.min(jnp.where(x == mx, blk, N_SEL_LANES), axis=0, keepdims=True)
            hit = blk == idx
            sel = sel | hit
            x = jnp.where(hit, -jnp.inf, x)
        mneg = jnp.transpose(jnp.where(sel, 0.0, NEG))
        m_ref[0, kv] = mneg[:tq].astype(m_ref.dtype)
    o_ref[0] = jnp.concatenate(o_groups, axis=1)


def cmp_select(qp, kcv, q_off, n_pick, tq):
    b, t, _ = qp.shape
    nb = kcv.shape[2]
    cover = jnp.transpose(_cover_matrix(nb))
    return pl.pallas_call(
        functools.partial(_cmp_select_kernel, q_off=q_off, n_pick=n_pick),
        grid=(b, t // tq),
        in_specs=[
            pl.BlockSpec((1, tq, N_Q_HEADS * LANES), lambda i, j: (i, j, 0)),
            pl.BlockSpec((1, 1, nb, KV_WIDTH), lambda i, j: (i, 0, 0, 0)),
            pl.BlockSpec((1, 1, nb, KV_WIDTH), lambda i, j: (i, 1, 0, 0)),
            pl.BlockSpec((N_SEL_LANES, nb), lambda i, j: (0, 0)),
        ],
        out_specs=(
            pl.BlockSpec((1, tq, ATTN_WIDTH), lambda i, j: (i, j, 0)),
            pl.BlockSpec((1, N_KV_HEADS, tq, N_SEL_LANES), lambda i, j: (i, 0, j, 0)),
        ),
        out_shape=(
            jax.ShapeDtypeStruct((b, t, ATTN_WIDTH), F32),
            jax.ShapeDtypeStruct((b, N_KV_HEADS, t, N_SEL_LANES), BF16),
        ),
        compiler_params=_cparams(("arbitrary", "arbitrary")),
        name="cmp_select",
    )(qp, kcv, kcv, cover)


SEL_TILE_ELEMS = 512 * 512
WIN_CHUNK = 256


def _block_onehot(s):
    key = np.arange(s)[:, None]
    j = np.arange(N_SEL_LANES)[None, :]
    return jnp.asarray((key // SEL_BLOCK == j).astype(np.float32), BF16)


def _gate_expand():
    m = np.zeros((3, LANES, ATTN_WIDTH), np.float32)
    for br in range(3):
        for hd in range(N_Q_HEADS):
            m[br, SSM_HEADS + 3 * hd + br, hd * HEAD_DIM:(hd + 1) * HEAD_DIM] = 1.0
    return jnp.asarray(m, BF16)


def _flash_update(ss, v, m_ref, acc_ref):
    lane = lax.broadcasted_iota(jnp.int32, v.shape, 1)
    one = jnp.ones(v.shape, v.dtype)
    stage = []
    for k, s in enumerate(ss):
        m_old = m_ref[k]
        m_new = jnp.maximum(m_old, jnp.max(s, axis=0, keepdims=True))
        alpha = jnp.exp2(m_old - m_new)
        p = jnp.exp2(s - m_new)
        m_ref[k] = m_new
        stage.append((alpha, p.astype(BF16)))
    for k, (alpha, p) in enumerate(stage):
        vk = jnp.where((lane < HEAD_DIM) == (k == 0), v, one)
        pv = lax.dot_general(vk, p, (((0,), (0,)), ((), ())), preferred_element_type=F32)
        acc_ref[k] = alpha * acc_ref[k] + pv


def _sel_chunk(rows, n_keys):
    chunk = SEL_TILE_ELEMS // rows
    while n_keys % chunk:
        chunk //= 2
    return chunk


def _sel_win_kernel(q_ref, mneg_ref, ksel_ref, vsel_ref, et_ref, kwin_ref, vwin_ref, ocmp_ref, misc_ref, eg_ref,
                    o_ref, lhs_ref, m_ref, acc_ref, *, q_off, win_pos0):
    tq = q_ref.shape[1]
    rows = GQA_GROUP * tq
    SEL_CHUNK = _sel_chunk(rows, ksel_ref.shape[1])
    t0 = q_off + pl.program_id(1) * tq
    n_sel = lax.shift_right_logical(t0 + tq - 1, int(math.log2(SEL_CHUNK))) + 1
    w_lo = jnp.maximum(t0 - (WINDOW - 1) - win_pos0, 0) // WIN_CHUNK
    w_hi = (t0 + tq - 1 - win_pos0) // WIN_CHUNK + 1

    def qrow(n_keys):
        return lax.broadcasted_iota(jnp.int32, (n_keys, rows), 1) % tq + t0

    def init():
        m_ref[...] = jnp.full(m_ref.shape, NEG, F32)
        acc_ref[...] = jnp.zeros(acc_ref.shape, F32)

    def finish():
        outs = []
        for kv in range(N_KV_HEADS):
            acc = acc_ref[kv]
            denom_row = HEAD_DIM * (1 - kv)
            outs.append(_heads_from_transposed(acc / acc[denom_row:denom_row + 1, :], tq, kv))
        return jnp.concatenate(outs, axis=1)

    for kv in range(N_KV_HEADS):
        for g in range(GQA_GROUP):
            hd = kv * GQA_GROUP + g
            lhs_ref[kv, g * tq:(g + 1) * tq, :LANES] = q_ref[0, :, hd * LANES:(hd + 1) * LANES]
            lhs_ref[kv, g * tq:(g + 1) * tq, LANES:] = mneg_ref[0, kv]

    init()

    def sel_step(c, carry, causal):
        r0 = pl.multiple_of(c * SEL_CHUNK, SEL_CHUNK)
        rhs = jnp.concatenate([ksel_ref[0, pl.ds(r0, SEL_CHUNK), :], et_ref[pl.ds(r0, SEL_CHUNK), :]], axis=1)
        v = vsel_ref[0, pl.ds(r0, SEL_CHUNK), :]
        if causal:
            ok = r0 + lax.broadcasted_iota(jnp.int32, (SEL_CHUNK, rows), 0) <= qrow(SEL_CHUNK)
        ss = [_dot_nt(rhs, lhs_ref[kv]) for kv in range(N_KV_HEADS)]
        if causal:
            ss = [jnp.where(ok, s, NEG) for s in ss]
        _flash_update(ss, v, m_ref, acc_ref)
        return carry

    n_full = lax.shift_right_logical(t0 + 1, int(math.log2(SEL_CHUNK)))
    lax.fori_loop(0, n_full, functools.partial(sel_step, causal=False), 0)
    lax.fori_loop(n_full, n_sel, functools.partial(sel_step, causal=True), 0)
    o_sel = finish()

    init()

    def win_step(c, carry):
        r0 = pl.multiple_of(c * WIN_CHUNK, WIN_CHUNK)
        k = kwin_ref[0, pl.ds(r0, WIN_CHUNK), :]
        v = vwin_ref[0, pl.ds(r0, WIN_CHUNK), :]
        wpos = win_pos0 + r0 + lax.broadcasted_iota(jnp.int32, (WIN_CHUNK, rows), 0)
        qr = qrow(WIN_CHUNK)
        ok = (wpos <= qr) & (wpos > qr - WINDOW)
        ss = [jnp.where(ok, _dot_nt(k, lhs_ref[kv, :, :LANES]), NEG) for kv in range(N_KV_HEADS)]
        _flash_update(ss, v, m_ref, acc_ref)
        return carry

    lax.fori_loop(w_lo, w_hi, win_step, 0)
    o_win = finish()

    gates = jax.nn.sigmoid(misc_ref[0])
    ghi = gates.astype(BF16)
    glo = (gates - ghi.astype(F32)).astype(BF16)
    branches = (ocmp_ref[0], o_sel, o_win)
    out = jnp.zeros(branches[0].shape, F32)
    for br in range(3):
        out = out + (_dot(ghi, eg_ref[br]) + _dot(glo, eg_ref[br])) * branches[br]
    o_ref[0] = out


def sel_win_attention(qp, mneg, kvb, sel_col, winb, o_cmp, misc, q_off, win_pos0, tq):
    b, t, _ = qp.shape
    s = kvb.shape[1]
    sw = winb.shape[1]
    et = _block_onehot(s)
    eg = _gate_expand()
    rows = GQA_GROUP * tq
    assert q_off + t <= s and q_off + t - win_pos0 <= sw and sw % WIN_CHUNK == 0
    return pl.pallas_call(
        functools.partial(_sel_win_kernel, q_off=q_off, win_pos0=win_pos0),
        grid=(b, t // tq),
        in_specs=[
            pl.BlockSpec((1, tq, N_Q_HEADS * LANES), lambda i, j: (i, j, 0)),
            pl.BlockSpec((1, N_KV_HEADS, tq, N_SEL_LANES), lambda i, j: (i, 0, j, 0)),
            pl.BlockSpec((1, s, KV_WIDTH), lambda i, j: (i, 0, sel_col)),
            pl.BlockSpec((1, s, KV_WIDTH), lambda i, j: (i, 0, sel_col + 1)),
            pl.BlockSpec((s, N_SEL_LANES), lambda i, j: (0, 0)),
            pl.BlockSpec((1, sw, KV_WIDTH), lambda i, j: (i, 0, 0)),
            pl.BlockSpec((1, sw, KV_WIDTH), lambda i, j: (i, 0, 1)),
            pl.BlockSpec((1, tq, ATTN_WIDTH), lambda i, j: (i, j, 0)),
            pl.BlockSpec((1, tq, LANES), lambda i, j: (i, j, 0)),
            pl.BlockSpec((3, LANES, ATTN_WIDTH), lambda i, j: (0, 0, 0)),
        ],
        out_specs=pl.BlockSpec((1, tq, ATTN_WIDTH), lambda i, j: (i, j, 0)),
        out_shape=jax.ShapeDtypeStruct((b, t, ATTN_WIDTH), F32),
        scratch_shapes=[
            pltpu.VMEM((N_KV_HEADS, rows, 2 * LANES), BF16),
            pltpu.VMEM((N_KV_HEADS, 1, rows), F32),
            pltpu.VMEM((N_KV_HEADS, LANES, rows), F32),
        ],
        compiler_params=_cparams(("arbitrary", "arbitrary")),
        name="sel_win_attention",
    )(qp, mneg, kvb, kvb, et, winb, winb, o_cmp, misc, eg)


CONV_PAD = 8
HEAD_PAIRS = SSM_HEADS // 2


def _split3(x):
    a = x.astype(BF16)
    r = x - a.astype(F32)
    b = r.astype(BF16)
    c = (r - b.astype(F32)).astype(BF16)
    return a, b, c


def _ssd_kernel(xbc_ref, z_ref, misc_ref, conv0_ref, h0_ref, cw_ref, cb_ref, dtb_ref, a_ref, dsk_ref, nw_ref,
                y_ref, hout_ref, cout_ref, xp_ref, h_ref, ms_ref, *, t_valid):
    ch = pl.program_id(1)
    L = SSD_CHUNK
    keep = CONV_WIDTH - 1

    @pl.when(ch == 0)
    def _():
        xp_ref[...] = jnp.zeros(xp_ref.shape, F32)
        xp_ref[CONV_PAD - keep:CONV_PAD, :] = conv0_ref[0]
        h_ref[...] = h0_ref[0]

    xp_ref[CONV_PAD:CONV_PAD + t_valid, :] = xbc_ref[0]
    conv = cb_ref[...]
    for j in range(CONV_WIDTH):
        conv = conv + cw_ref[j:j + 1, :] * xp_ref[CONV_PAD - keep + j:CONV_PAD - keep + j + L, :]
    last = xp_ref[CONV_PAD + t_valid - keep:CONV_PAD + t_valid, :]
    cout_ref[0] = last
    xp_ref[CONV_PAD - keep:CONV_PAD, :] = last
    xc = _silu(conv)

    row = lax.broadcasted_iota(jnp.int32, (L, LANES), 0)
    lane = lax.broadcasted_iota(jnp.int32, (L, LANES), 1)
    if t_valid == L:
        raw = misc_ref[0]
    else:
        ms_ref[...] = jnp.zeros(ms_ref.shape, F32)
        ms_ref[0:t_valid, :] = misc_ref[0]
        raw = ms_ref[...]
    v = raw + dtb_ref[...]
    dt = jnp.maximum(v, 0.0) + jnp.log(1.0 + jnp.exp(-jnp.abs(v)))
    dt = jnp.where((lane < SSM_HEADS) & (row < t_valid), dt, 0.0)
    da = dt * a_ref[...]
    tri = (lax.broadcasted_iota(jnp.int32, (L, L), 1) <= lax.broadcasted_iota(jnp.int32, (L, L), 0))
    trib = tri.astype(BF16)
    acum = sum(_dot(trib, part) for part in _split3(da))
    acum_t = jnp.transpose(acum)
    dt_t = jnp.transpose(dt)
    e_acum = jnp.exp(acum)
    e_last = jnp.exp(acum[L - 1:L, :])
    w_end = jnp.exp(acum[L - 1:L, :] - acum) * dt
    lo = lane < SSM_HEAD_DIM

    ys = []
    for pair in range(HEAD_PAIRS):
        grp = (2 * pair) // (SSM_HEADS // SSM_GROUPS)
        bg = xc[:, SSM_WIDTH + grp * SSM_STATE:SSM_WIDTH + (grp + 1) * SSM_STATE].astype(BF16)
        cg = xc[:, SSM_WIDTH + (SSM_GROUPS + grp) * SSM_STATE:SSM_WIDTH + (SSM_GROUPS + grp + 1) * SSM_STATE].astype(BF16)
        g = _dot_nt(cg, bg)
        xpair = xc[:, pair * LANES:(pair + 1) * LANES]
        y = jnp.zeros((L, LANES), F32)
        for sub in range(2):
            hd = 2 * pair + sub
            seg = acum[:, hd:hd + 1] - acum_t[hd:hd + 1, :]
            m = g * jnp.exp(jnp.where(tri, seg, NEG)) * dt_t[hd:hd + 1, :]
            xm = jnp.where(lo if sub == 0 else ~lo, xpair, 0.0)
            y = y + _dot(m.astype(BF16), xm.astype(BF16))
        col = lambda a: jnp.where(lo, a[:, 2 * pair:2 * pair + 1], a[:, 2 * pair + 1:2 * pair + 2])
        hp = h_ref[pair]
        y = y + _dot_nt(cg, hp.astype(BF16)) * col(e_acum)
        y = y + col(dsk_ref[...]) * xpair
        xw = (xpair * col(w_end)).astype(BF16)
        st = lax.dot_general(xw, bg, (((0,), (0,)), ((), ())), preferred_element_type=F32)
        prow = lax.broadcasted_iota(jnp.int32, (LANES, LANES), 0) < SSM_HEAD_DIM
        dec = jnp.where(prow, e_last[:, 2 * pair:2 * pair + 1], e_last[:, 2 * pair + 1:2 * pair + 2])
        h_ref[pair] = hp * dec + st
        ys.append(y)
    y = jnp.concatenate(ys, axis=1)
    if t_valid != L:
        y = y[:t_valid]
    y = y * _silu(z_ref[0])
    y = y * lax.rsqrt(jnp.mean(y * y, axis=-1, keepdims=True) + EPS) * nw_ref[...]
    y_ref[0] = y

    @pl.when(ch == pl.num_programs(1) - 1)
    def _():
        hout_ref[0] = h_ref[...]


def ssd(xbc, z, misc, conv0, h0, conv_w, conv_b, dt_bias, a_log, d_skip, norm_w):
    b, t, _ = xbc.shape
    L = SSD_CHUNK
    t_valid = L if t % L == 0 else t
    assert t_valid == L or t < L
    n_ch = max(t // L, 1)
    keep = CONV_WIDTH - 1
    pad8 = lambda v: jnp.pad(v.astype(F32), (0, LANES - SSM_HEADS)).reshape(1, LANES)
    dtb = pad8(dt_bias)
    a = pad8(-jnp.exp(a_log.astype(F32)))
    dsk = pad8(d_skip)
    h0p = h0.reshape(b, HEAD_PAIRS, 2 * SSM_HEAD_DIM, SSM_STATE)
    full = lambda arr: pl.BlockSpec(arr.shape, lambda i, c: (0,) * arr.ndim)
    tok = lambda wd: pl.BlockSpec((1, t_valid, wd), lambda i, c: (i, c, 0))
    y, hout, cout = pl.pallas_call(
        functools.partial(_ssd_kernel, t_valid=t_valid),
        grid=(b, n_ch),
        in_specs=[
            tok(CONV_DIM), tok(SSM_WIDTH), tok(LANES),
            pl.BlockSpec((1, keep, CONV_DIM), lambda i, c: (i, 0, 0)),
            pl.BlockSpec((1, HEAD_PAIRS, 2 * SSM_HEAD_DIM, SSM_STATE), lambda i, c: (i, 0, 0, 0)),
            full(conv_w), pl.BlockSpec((1, CONV_DIM), lambda i, c: (0, 0)),
            full(dtb), full(a), full(dsk), pl.BlockSpec((1, SSM_WIDTH), lambda i, c: (0, 0)),
        ],
        out_specs=(
            tok(SSM_WIDTH),
            pl.BlockSpec((1, HEAD_PAIRS, 2 * SSM_HEAD_DIM, SSM_STATE), lambda i, c: (i, 0, 0, 0)),
            pl.BlockSpec((1, keep, CONV_DIM), lambda i, c: (i, 0, 0)),
        ),
        out_shape=(
            jax.ShapeDtypeStruct((b, t, SSM_WIDTH), F32),
            jax.ShapeDtypeStruct((b, HEAD_PAIRS, 2 * SSM_HEAD_DIM, SSM_STATE), F32),
            jax.ShapeDtypeStruct((b, keep, CONV_DIM), F32),
        ),
        scratch_shapes=[
            pltpu.VMEM((CONV_PAD + L, CONV_DIM), F32),
            pltpu.VMEM((HEAD_PAIRS, 2 * SSM_HEAD_DIM, SSM_STATE), F32),
            pltpu.VMEM((L, LANES), F32),
        ],
        compiler_params=_cparams(("arbitrary", "arbitrary")),
        name="ssd",
    )(xbc, z, misc, conv0, h0p, conv_w, conv_b.reshape(1, CONV_DIM), dtb, a, dsk, norm_w.reshape(1, SSM_WIDTH))
    return y, hout.reshape(b, SSM_HEADS, SSM_HEAD_DIM, SSM_STATE), cout


def _split2(x):
    hi = x.astype(BF16)
    return hi, (x - hi.astype(F32)).astype(BF16)


def _merge_kernel(oa_ref, ys_ref, x_ref, g1_ref, sh2_ref, sc2_ref, anw_ref, wo_ref, n2w_ref, wrh_ref, wrl_ref,
                  x1_ref, h2_ref, lg_ref):
    oa = oa_ref[...]
    a = oa * lax.rsqrt(jnp.mean(oa * oa, axis=-1, keepdims=True) + EPS) * anw_ref[...]
    cat = jnp.concatenate([a.astype(BF16), ys_ref[...].astype(BF16)], axis=1)
    x1 = x_ref[...] + _mod(g1_ref) * _dot(cat, wo_ref[...])
    x1_ref[...] = x1
    h2 = x1 * lax.rsqrt(jnp.mean(x1 * x1, axis=-1, keepdims=True) + EPS) * n2w_ref[...]
    h2 = h2 * (1.0 + _mod(sc2_ref)) + _mod(sh2_ref)
    h2_ref[...] = h2.astype(BF16)
    hh, hl = _split2(h2)
    lg_ref[...] = _dot_nt(wrh_ref[...], hh) + _dot_nt(wrh_ref[...], hl) + _dot_nt(wrl_ref[...], hh)


def merge(o_attn, y_ssm, x, mod3, mod_row0, attn_norm_w, wo, norm2_w, w_router, tm):
    b, t, d = x.shape
    n = b * t
    tiles_per_b = t // tm
    wrt = jnp.transpose(w_router)
    wrh, wrl = _split2(wrt)

    def mod_spec(col):
        return _mod_spec(mod3, col, tm, tiles_per_b, mod_row0)

    tok = lambda wd: pl.BlockSpec((tm, wd), lambda i: (i, 0))
    full = lambda a: pl.BlockSpec(a.shape, lambda i: (0,) * a.ndim)
    return pl.pallas_call(
        _merge_kernel,
        grid=(n // tm,),
        in_specs=[tok(ATTN_WIDTH), tok(SSM_WIDTH), tok(d), mod_spec(2), mod_spec(3), mod_spec(4),
                  full(attn_norm_w), full(wo), full(norm2_w), full(wrh), full(wrl)],
        out_specs=(tok(d), tok(d), pl.BlockSpec((N_EXPERTS, tm), lambda i: (0, i))),
        out_shape=(jax.ShapeDtypeStruct((n, d), F32), jax.ShapeDtypeStruct((n, d), BF16),
                   jax.ShapeDtypeStruct((N_EXPERTS, n), F32)),
        compiler_params=_cparams(("arbitrary",)),
        name="merge",
    )(o_attn.reshape(n, ATTN_WIDTH), y_ssm.reshape(n, SSM_WIDTH), x.reshape(n, d), mod3, mod3, mod3,
      attn_norm_w, wo, norm2_w, wrh, wrl)


EXPERTS_PER_GROUP = N_EXPERTS // N_EXPERT_GROUPS


def _first_max(x, ids, axes, n_ids):
    mx = jnp.max(x, axis=axes, keepdims=True)
    return ids == jnp.min(jnp.where(x == mx, ids, n_ids), axis=axes, keepdims=True), mx


def _route_kernel(lg_ref, eb_ref, tri_ref, w_ref, pos_ref, cnt_ref):
    lg = lg_ref[...]
    tn = lg.shape[2]
    scores = jax.nn.sigmoid(lg)
    biased = scores + eb_ref[...]
    sub = lax.broadcasted_iota(jnp.int32, lg.shape, 1)
    grp = lax.broadcasted_iota(jnp.int32, (N_EXPERT_GROUPS, 1, tn), 0)
    eid = lax.broadcasted_iota(jnp.int32, lg.shape, 0) * EXPERTS_PER_GROUP + sub
    hit, m1 = _first_max(biased, sub, 1, EXPERTS_PER_GROUP)
    m2 = jnp.max(jnp.where(hit, -jnp.inf, biased), axis=1, keepdims=True)
    gs = m1 + m2
    keep = jnp.zeros(gs.shape, jnp.bool_)
    for _ in range(TOPK_GROUPS):
        hit, _m = _first_max(gs, grp, 0, N_EXPERT_GROUPS)
        keep = keep | hit
        gs = jnp.where(hit, -jnp.inf, gs)
    x = jnp.where(keep, biased, NEG)
    sel = jnp.zeros(lg.shape, jnp.bool_)
    for _ in range(TOP_K):
        hit, _m = _first_max(x, eid, (0, 1), N_EXPERTS)
        sel = sel | hit
        x = jnp.where(hit, -jnp.inf, x)
    w = jnp.where(sel, scores, 0.0)
    w = w / jnp.sum(w, axis=(0, 1), keepdims=True) * ROUTED_SCALE
    w_ref[...] = w
    selb = sel.astype(BF16).reshape(N_EXPERTS, tn)
    pos = _dot(selb, tri_ref[...])
    pos_ref[...] = jnp.where(sel, pos.reshape(lg.shape), -1.0)
    cnt = jnp.sum(sel.astype(F32), axis=2, keepdims=True)
    cnt_ref[0] = jnp.broadcast_to(cnt, cnt_ref.shape[1:]).astype(jnp.int32)


def route(logits_t, e_bias, tn):
    n = logits_t.shape[1]
    lg3 = logits_t.reshape(N_EXPERT_GROUPS, EXPERTS_PER_GROUP, n)
    eb = e_bias.astype(F32).reshape(N_EXPERT_GROUPS, EXPERTS_PER_GROUP, 1)
    tri = jnp.asarray(np.triu(np.ones((tn, tn), np.float32), 1), BF16)
    blk = pl.BlockSpec((N_EXPERT_GROUPS, EXPERTS_PER_GROUP, tn), lambda i: (0, 0, i))
    w, pos, cnt = pl.pallas_call(
        _route_kernel,
        grid=(n // tn,),
        in_specs=[blk, pl.BlockSpec(eb.shape, lambda i: (0, 0, 0)), pl.BlockSpec((tn, tn), lambda i: (0, 0))],
        out_specs=(blk, blk, pl.BlockSpec((1, N_EXPERT_GROUPS, EXPERTS_PER_GROUP, LANES), lambda i: (i, 0, 0, 0))),
        out_shape=(jax.ShapeDtypeStruct(lg3.shape, F32), jax.ShapeDtypeStruct(lg3.shape, F32),
                   jax.ShapeDtypeStruct((n // tn, N_EXPERT_GROUPS, EXPERTS_PER_GROUP, LANES), jnp.int32)),
        compiler_params=_cparams(("arbitrary",)),
        name="route",
    )(lg3, eb, tri)
    return w.reshape(N_EXPERTS, n), pos.reshape(N_EXPERTS, n), cnt[..., 0].reshape(n // tn, N_EXPERTS)


MOE_ROWS = 128


def _swiglu(xb, wgu, wd, width):
    gu = _dot(xb, wgu)
    act = _silu(gu[:, :width]) * gu[:, width:]
    return _dot(act.astype(BF16), wd)


MOE_EXPERTS_PER_STEP = 4


MOE_ALIGN = 16
MOE_GATHER_ROWS = 896


def _moe_slots(tm):
    worst = TOP_K * tm + N_EXPERTS * (MOE_ALIGN - 1) + MOE_ROWS
    return -(-worst // MOE_GATHER_ROWS) * MOE_GATHER_ROWS


def _moe_kernel(cnt_ref, start_ref, h2_ref, w_ref, pos_ref, x1_ref, g2_ref, wgu_ref, wd_ref, sgu_ref, sd_ref,
                o_ref, g_all, xs):
    i = pl.program_id(0)
    es = pl.program_id(1)
    tm = h2_ref.shape[0]
    slots = g_all.shape[0]
    slot = lax.broadcasted_iota(jnp.int32, (MOE_ROWS, tm), 0).astype(F32)
    row = lax.broadcasted_iota(jnp.int32, (MOE_ROWS, 1), 0)

    def n_windows(cnt):
        return (cnt + MOE_ROWS - 1) // MOE_ROWS

    def window_start(e, j):
        return pl.multiple_of(start_ref[i * N_EXPERTS + e] + j * MOE_ROWS, MOE_ALIGN)

    @pl.when(es == 0)
    def _():
        g_all[...] = jnp.zeros(g_all.shape, BF16)

        def mark(e, carry):
            pos = pos_ref[pl.ds(e, 1), :]

            def mark_window(j, carry):
                hit = pos == slot + (j * MOE_ROWS).astype(F32)
                g_all[pl.ds(window_start(e, j), MOE_ROWS), :] = hit.astype(BF16)
                return carry

            return lax.fori_loop(0, n_windows(cnt_ref[i * N_EXPERTS + e]), mark_window, carry)

        lax.fori_loop(0, N_EXPERTS, mark, 0)

        def gather(c, carry):
            r0 = pl.multiple_of(c * MOE_GATHER_ROWS, MOE_GATHER_ROWS)
            rows = _dot(g_all[pl.ds(r0, MOE_GATHER_ROWS), :], h2_ref[...])
            xs[pl.ds(r0, MOE_GATHER_ROWS), :] = rows.astype(BF16)
            return carry

        lax.fori_loop(0, slots // MOE_GATHER_ROWS, gather, 0)

    for q in range(MOE_EXPERTS_PER_STEP):
        e = es * MOE_EXPERTS_PER_STEP + q
        cnt = cnt_ref[i * N_EXPERTS + e]
        wrow = w_ref[pl.ds(e, 1), :]

        def window(j, carry, q=q, e=e, cnt=cnt, wrow=wrow):
            r0 = window_start(e, j)
            xg = xs[pl.ds(r0, MOE_ROWS), :]
            out = _swiglu(xg, wgu_ref[q], wd_ref[q], D_EXPERT)
            g = g_all[pl.ds(r0, MOE_ROWS), :].astype(F32)
            out = out * jnp.sum(g * wrow, axis=1, keepdims=True)
            mine = row < cnt - j * MOE_ROWS
            xs[pl.ds(r0, MOE_ROWS), :] = jnp.where(mine, out.astype(BF16), xg)
            return carry

        lax.fori_loop(0, n_windows(cnt), window, 0)

    @pl.when(es == pl.num_programs(1) - 1)
    def _():
        y = lax.dot_general(g_all[...], xs[...], (((0,), (0,)), ((), ())), preferred_element_type=F32)
        y = y + _swiglu(h2_ref[...], sgu_ref[...], sd_ref[...], D_SHARED)
        o_ref[...] = x1_ref[...] + _mod(g2_ref) * y


def moe(h2, w_t, pos_t, counts, x1, mod3, mod_row0, t_per_b, wgu, wd, sgu, sd, tm):
    n, d = h2.shape
    tiles_per_b = t_per_b // tm
    eps = MOE_EXPERTS_PER_STEP
    slots = _moe_slots(tm)
    padded = (counts + MOE_ALIGN - 1) // MOE_ALIGN * MOE_ALIGN
    starts = jnp.cumsum(padded, axis=1) - padded
    grid_spec = pltpu.PrefetchScalarGridSpec(
        num_scalar_prefetch=2,
        grid=(n // tm, N_EXPERTS // eps),
        in_specs=[
            pl.BlockSpec((tm, d), lambda i, e, *_: (i, 0)),
            pl.BlockSpec((N_EXPERTS, tm), lambda i, e, *_: (0, i)),
            pl.BlockSpec((N_EXPERTS, tm), lambda i, e, *_: (0, i)),
            pl.BlockSpec((tm, d), lambda i, e, *_: (i, 0)),
            _mod_spec(mod3, 5, tm, tiles_per_b, mod_row0),
            pl.BlockSpec((eps, d, 2 * D_EXPERT), lambda i, e, *_: (e, 0, 0)),
            pl.BlockSpec((eps, D_EXPERT, d), lambda i, e, *_: (e, 0, 0)),
            pl.BlockSpec(sgu.shape, lambda i, e, *_: (0, 0)),
            pl.BlockSpec(sd.shape, lambda i, e, *_: (0, 0)),
        ],
        out_specs=pl.BlockSpec((tm, d), lambda i, e, *_: (i, 0)),
        scratch_shapes=[pltpu.VMEM((slots, tm), BF16), pltpu.VMEM((slots, d), BF16)],
    )
    return pl.pallas_call(
        _moe_kernel,
        grid_spec=grid_spec,
        out_shape=jax.ShapeDtypeStruct((n, d), F32),
        compiler_params=_cparams(("arbitrary", "arbitrary")),
        name="moe",
    )(counts.reshape(-1), starts.reshape(-1).astype(jnp.int32), h2, w_t, pos_t, x1, mod3, wgu, wd, sgu, sd)


GATHER_PAGES = 8


def _gather_kernel(pt_ref, *refs):
    pages, new_ref = refs[:GATHER_PAGES], refs[GATHER_PAGES]
    rows_ref, cmpx_ref, stage_ref = refs[GATHER_PAGES + 1:]
    step = pl.program_id(1)
    last = pl.num_programs(1) - 1
    n_rows = GATHER_PAGES * PAGE_SIZE

    @pl.when(step < last)
    def _():
        for k in range(GATHER_PAGES):
            sl = slice(k * PAGE_SIZE, (k + 1) * PAGE_SIZE)
            for r in range(4):
                tile = jnp.transpose(pages[k][0, r])
                if r < 2:
                    stage_ref[r, sl, :] = tile
                else:
                    rows_ref[0, sl, (r - 2) * KV_WIDTH:(r - 1) * KV_WIDTH] = tile.astype(BF16)

    @pl.when(step == last)
    def _():
        new = new_ref[0]
        tn = new.shape[0]
        stage_ref[...] = jnp.zeros(stage_ref.shape, F32)
        for s in range(2):
            stage_ref[s, 0:tn, :] = new[:, s * KV_WIDTH:(s + 1) * KV_WIDTH]
        pad = jnp.zeros((n_rows - tn, 2 * KV_WIDTH), F32)
        rows_ref[0] = jnp.concatenate([new[:, 2 * KV_WIDTH:], pad], axis=0).astype(BF16)

    _stride_block_store(stage_ref, cmpx_ref, n_rows)


def gather_pages(cache_t, page_table, new_rows):
    b, n_pages = page_table.shape
    steps = n_pages // GATHER_PAGES
    rows = GATHER_PAGES * PAGE_SIZE
    s_out = (steps + 1) * rows

    def page_spec(k):
        def idx(i, s, pt):
            p = jnp.minimum(s, steps - 1) * GATHER_PAGES + k
            return (pt[i * n_pages + p], 0, 0, 0)
        return pl.BlockSpec((1, 4, KV_WIDTH, PAGE_SIZE), idx)

    grid_spec = pltpu.PrefetchScalarGridSpec(
        num_scalar_prefetch=1,
        grid=(b, steps + 1),
        in_specs=[page_spec(k) for k in range(GATHER_PAGES)]
        + [pl.BlockSpec((1,) + new_rows.shape[1:], lambda i, s, pt: (i, 0, 0))],
        out_specs=(
            pl.BlockSpec((1, rows, 2 * KV_WIDTH), lambda i, s, pt: (i, s, 0)),
            pl.BlockSpec((1, rows // CMP_STRIDE, CMP_STRIDE * 2 * KV_WIDTH), lambda i, s, pt: (i, s, 0)),
        ),
        scratch_shapes=[pltpu.VMEM((2, rows, KV_WIDTH), F32)],
    )
    return pl.pallas_call(
        _gather_kernel,
        grid_spec=grid_spec,
        out_shape=(jax.ShapeDtypeStruct((b, s_out, 2 * KV_WIDTH), BF16),
                   jax.ShapeDtypeStruct((b, s_out // CMP_STRIDE, CMP_STRIDE * 2 * KV_WIDTH), BF16)),
        compiler_params=_cparams(("arbitrary", "arbitrary")),
        name="gather_pages",
    )(page_table.reshape(-1), *([cache_t] * GATHER_PAGES), new_rows)


def _attention(qp, cmpx, kvb, sel_col, winb, misc, cmp_w, q_off, win_pos0, tq):
    t = qp.shape[1]
    cur_lo, cur_hi = q_off // SEL_BLOCK, (q_off + t - 1) // SEL_BLOCK
    assert cur_hi < N_SEL_LANES or (cur_lo == cur_hi == N_SEL_LANES), (q_off, t)
    n_pick = N_SEL - (1 if cur_hi >= N_SEL_LANES else 0)
    kcv = compress(cmpx, *cmp_w)
    o_cmp, mneg = cmp_select(qp, kcv, q_off, n_pick, tq)
    return sel_win_attention(qp, mneg, kvb, sel_col, winb, o_cmp, misc, q_off, win_pos0, tq)


def kernel(x_prompt, x_sample, cache_kv, cache_win, state_ssm, state_conv, page_table, c_prompt, c_sample, w_ada, b_ada, norm1_w, norm2_w, w_in, q_norm_w, k_norm_w, cmp_pe, cmp_w1, cmp_w2, attn_out_norm_w, conv_w, conv_b, dt_bias, a_log, d_skip, ssm_norm_w, w_out, w_router, e_bias, w_exp_gu, w_exp_down, w_sh_gu, w_sh_down):
    xp, xq = x_prompt, x_sample
    bp, tp, d = xp.shape
    bq, tq, _ = xq.shape
    depth = w_ada.shape[0]
    past_len = page_table.shape[1] * PAGE_SIZE
    nq = bq * tq
    tq_pad = LANES // GQA_GROUP
    assert tp % 512 == 0 and tp >= WINDOW and nq % 8 == 0 and tq <= tq_pad
    pos_p = jnp.arange(tp, dtype=jnp.int32)
    pos_q = jnp.tile(past_len + jnp.arange(tq, dtype=jnp.int32), bq)
    c_all = jnp.concatenate([c_prompt, c_sample], axis=0)
    c_all = jnp.pad(c_all, ((0, -c_all.shape[0] % 8), (0, 0)))
    outs = [[] for _ in range(8)]
    for l in range(depth):
        mod = adaln_all(c_all, w_ada[l], b_ada[l])
        mod_p = mod.reshape(mod.shape[0], 1, 6 * d)
        mod_q = jnp.repeat(mod[bp:bp + bq], tq, axis=0)
        wp = _prep_w_in(w_in[l])
        cmp_w = _prep_compress(cmp_pe[l], cmp_w1[l], cmp_w2[l])
        wo = w_out[l].astype(BF16)
        wgu, wd = w_exp_gu[l].astype(BF16), w_exp_down[l].astype(BF16)
        sgu, sd = w_sh_gu[l].astype(BF16), w_sh_down[l].astype(BF16)
        ssm_w = (conv_w[l], conv_b[l], dt_bias[l], a_log[l], d_skip[l], ssm_norm_w[l])
        n1w, n2w, anw = norm1_w[l:l + 1], norm2_w[l:l + 1], attn_out_norm_w[l:l + 1]

        qp, kvb, win, winb, z, xbc, misc, kvt, cmpx = inproj(xp, mod_p, 0, n1w, wp, q_norm_w[l], k_norm_w[l], pos_p,
                                                            512, True)
        r3 = lambda a: a.reshape(bp, tp, a.shape[-1])
        o_attn = _attention(r3(qp), cmpx, r3(kvb), 2, r3(winb), r3(misc), cmp_w, 0, 0, 128)
        y_ssm, h_new, conv_new = ssd(r3(xbc), r3(z), r3(misc), jnp.zeros((bp, CONV_WIDTH - 1, CONV_DIM), F32),
                                     jnp.zeros((bp, SSM_HEADS, SSM_HEAD_DIM, SSM_STATE), F32), *ssm_w)
        x1, h2, lg = merge(o_attn, y_ssm, xp, mod_p, 0, anw, wo, n2w, w_router[l], 512)
        w_t, pos_t, cnt = route(lg, e_bias[l], 512)
        xp = moe(h2, w_t, pos_t, cnt, x1, mod_p, 0, tp, wgu, wd, sgu, sd, 512).reshape(bp, tp, d)
        outs[0].append(jnp.transpose(kvt.reshape(bp, 4, N_KV_HEADS, HEAD_DIM, tp), (0, 4, 1, 2, 3)))
        outs[1].append(win.reshape(bp, tp, 2, N_KV_HEADS, HEAD_DIM)[:, tp - WINDOW:])
        outs[2].append(h_new)
        outs[3].append(conv_new)

        xq1 = xq.reshape(1, nq, d)
        qp, kvb, win, winb, z, xbc, misc, kv = inproj(xq1, mod_q, 0, n1w, wp, q_norm_w[l], k_norm_w[l], pos_q, nq,
                                                      False)
        rq = lambda a: a.reshape(bq, tq, a.shape[-1])
        padq = lambda a: jnp.pad(rq(a), ((0, 0), (0, tq_pad - tq), (0, 0)))
        cache_t = jnp.transpose(cache_kv[l], (0, 2, 3, 4, 1)).reshape(cache_kv.shape[1], 4, KV_WIDTH, PAGE_SIZE)
        past, cmpx = gather_pages(cache_t, page_table, rq(kv))
        win_all = jnp.concatenate([cache_win[l].reshape(bq, WINDOW, 2 * KV_WIDTH).astype(BF16), rq(winb),
                                   jnp.zeros((bq, -(WINDOW + tq_pad) % WIN_CHUNK + tq_pad - tq, 2 * KV_WIDTH), BF16)],
                                  axis=1)
        o_attn = _attention(padq(qp), cmpx, past, 0, win_all, padq(misc), cmp_w, past_len, past_len - WINDOW,
                            tq_pad)[:, :tq]
        y_ssm, h_new, conv_new = ssd(rq(xbc), rq(z), rq(misc), state_conv[l], state_ssm[l], *ssm_w)
        x1, h2, lg = merge(o_attn.reshape(1, nq, ATTN_WIDTH), y_ssm.reshape(1, nq, SSM_WIDTH), xq1, mod_q, 0,
                           anw, wo, n2w, w_router[l], nq)
        w_t, pos_t, cnt = route(lg, e_bias[l], nq)
        xq = moe(h2, w_t, pos_t, cnt, x1, mod_q, 0, nq, wgu, wd, sgu, sd, nq).reshape(bq, tq, d)
        win_rows = win.reshape(bq, tq, 2, N_KV_HEADS, HEAD_DIM)
        outs[4].append(kv.reshape(bq, tq, 4, N_KV_HEADS, HEAD_DIM))
        outs[5].append(jnp.concatenate([cache_win[l], win_rows.astype(cache_win.dtype)], axis=1)[:, tq:])
        outs[6].append(h_new)
        outs[7].append(conv_new)
    return (xp, xq) + tuple(jnp.stack(o) for o in outs)
```

```python
import functools
import math

import jax
import jax.numpy as jnp
import numpy as np
from jax import lax
from jax.experimental import pallas as pl
from jax.experimental.pallas import tpu as pltpu
from jax.experimental.pallas import tpu_sc as plsc

D_MODEL = 1024
PAGE_SIZE = 128
HEAD_DIM = 64
N_Q_HEADS = 8
N_KV_HEADS = 2
GQA_GROUP = N_Q_HEADS // N_KV_HEADS
ATTN_WIDTH = N_Q_HEADS * HEAD_DIM
KV_WIDTH = N_KV_HEADS * HEAD_DIM
ROPE_DIM = HEAD_DIM // 4
ROPE_THETA = 500000.0
CMP_LEN = 32
CMP_STRIDE = 16
CMP_HIDDEN = 4 * HEAD_DIM
SEL_BLOCK = 64
N_SEL = 16
N_LOCAL = 2
WINDOW = 512
SSM_HEADS = 8
SSM_HEAD_DIM = 64
SSM_WIDTH = SSM_HEADS * SSM_HEAD_DIM
SSM_GROUPS = 2
SSM_STATE = 128
CONV_WIDTH = 4
CONV_DIM = SSM_WIDTH + 2 * SSM_GROUPS * SSM_STATE
SSD_CHUNK = 128
MIX_WIDTH = ATTN_WIDTH + SSM_WIDTH
N_EXPERTS = 64
N_EXPERT_GROUPS = 8
TOPK_GROUPS = 4
TOP_K = 8
D_EXPERT = 256
D_SHARED = 256
ROUTED_SCALE = 2.5
IN_SIZES = (ATTN_WIDTH, 6 * KV_WIDTH, 3 * N_Q_HEADS, SSM_WIDTH, CONV_DIM, SSM_HEADS)
N_IN = sum(IN_SIZES)
EPS = 1e-6
NEG = -1e30
BIG = 1e6

LANES = 128
VMEM_LIMIT = 56 * 1024 * 1024

BF16 = jnp.bfloat16
F32 = jnp.float32
LOG2E = math.log2(math.e)


def _cparams(sem, flags=None):
    return pltpu.CompilerParams(dimension_semantics=sem, vmem_limit_bytes=VMEM_LIMIT, flags=flags)


def _silu(x):
    return x * jax.nn.sigmoid(x)


def _dot(a, b):
    return jnp.dot(a, b, preferred_element_type=F32)


def _dot_nt(a, b):
    return lax.dot_general(a, b, (((1,), (1,)), ((), ())), preferred_element_type=F32)


def _mod_spec(mod, col, tm, tiles_per_b, row0):
    if mod.ndim == 3:
        return pl.BlockSpec((1, 1, D_MODEL), lambda i, *_: (row0 + i // tiles_per_b, 0, col))
    return pl.BlockSpec((tm, D_MODEL), lambda i, *_: (i, col))


def _mod(ref):
    return ref[0] if len(ref.shape) == 3 else ref[...]


def _adaln_kernel(c_ref, w_ref, b_ref, o_ref):
    c = c_ref[...]
    a = _silu(c).astype(BF16)
    o_ref[...] = _dot(a, w_ref[...].astype(BF16)) + b_ref[...]


def adaln_all(c_all, w_ada, b_ada):
    rows = c_all.shape[0]
    n = w_ada.shape[1]
    tn = 1024
    return pl.pallas_call(
        _adaln_kernel,
        grid=(n // tn,),
        in_specs=[
            pl.BlockSpec((rows, D_MODEL), lambda j: (0, 0)),
            pl.BlockSpec((D_MODEL, tn), lambda j: (0, j)),
            pl.BlockSpec((1, tn), lambda j: (0, j)),
        ],
        out_specs=pl.BlockSpec((rows, tn), lambda j: (0, j)),
        out_shape=jax.ShapeDtypeStruct((rows, n), F32),
        compiler_params=_cparams(("arbitrary",)),
        name="adaln",
    )(c_all, w_ada, b_ada.reshape(1, n))


_C_Q = 0
_C_KV = _C_Q + ATTN_WIDTH
_C_Z = _C_KV + 6 * KV_WIDTH
_C_XBC = _C_Z + SSM_WIDTH
_C_MISC = _C_XBC + CONV_DIM
N_IN_PAD = _C_MISC + LANES
N_GATES = 3 * N_Q_HEADS


def _prep_w_in(w_in):
    s = np.cumsum((0,) + IN_SIZES)
    q, kv, g, z, xbc, dt = (w_in[:, int(s[i]):int(s[i + 1])] for i in range(6))
    pad = jnp.zeros((w_in.shape[0], LANES - N_GATES - SSM_HEADS), w_in.dtype)
    return jnp.concatenate([q, kv, z, xbc, dt, g, pad], axis=1).astype(BF16)


def _group_mean_matrix(width):
    i = np.arange(width)
    m = (i[:, None] // HEAD_DIM == i[None, :] // HEAD_DIM).astype(np.float32) / HEAD_DIM
    return jnp.asarray(m, BF16)


def _rope_tables(pos):
    half = ROPE_DIM // 2
    inv_freq = ROPE_THETA ** (-jnp.arange(half, dtype=F32) / half)
    ang = pos.astype(F32)[:, None] * inv_freq[None, :]
    cos, sin = jnp.cos(ang), jnp.sin(ang)
    t = pos.shape[0]
    one = jnp.ones((t, HEAD_DIM - ROPE_DIM), F32)
    zero = jnp.zeros((t, HEAD_DIM - ROPE_DIM), F32)
    zh = jnp.zeros((t, half), F32)
    c = jnp.concatenate([cos, cos, one], axis=1)
    s_up = jnp.concatenate([-sin, zh, zero], axis=1)
    s_dn = jnp.concatenate([zh, sin, zero], axis=1)
    rep = LANES // HEAD_DIM
    return jnp.tile(c, (1, rep)), jnp.tile(s_up, (1, rep)), jnp.tile(s_dn, (1, rep))


def _rope(x, c, s_up, s_dn):
    w = x.shape[1]
    half = ROPE_DIM // 2
    rep = w // LANES
    ct = jnp.concatenate([c] * rep, axis=1) if rep > 1 else c
    su = jnp.concatenate([s_up] * rep, axis=1) if rep > 1 else s_up
    sd = jnp.concatenate([s_dn] * rep, axis=1) if rep > 1 else s_dn
    up = pltpu.roll(x, w - half, axis=1)
    dn = pltpu.roll(x, half, axis=1)
    return x * ct + up * su + dn * sd


def _stride_block_store(stage_ref, cmpx_ref, n_rows):
    for j in range(CMP_STRIDE):
        for s in range(2):
            rows_j = stage_ref[s, pl.ds(j, n_rows // CMP_STRIDE, stride=CMP_STRIDE), :]
            c0 = (2 * j + s) * KV_WIDTH
            cmpx_ref[0, :, c0:c0 + KV_WIDTH] = rows_j.astype(BF16)


def _inproj_kernel(x_ref, shift_ref, scale_ref, nw_ref, w_ref, qw_ref, kw_ref, gq_ref, gk_ref,
                   c_ref, su_ref, sd_ref,
                   qp_ref, kvb_ref, win_ref, winb_ref, z_ref, xbc_ref, misc_ref, *rest, seq_layout):
    x = x_ref[...]
    ms = jnp.mean(x * x, axis=-1, keepdims=True)
    h = x * lax.rsqrt(ms + EPS) * nw_ref[...]
    h = h * (1.0 + _mod(scale_ref)) + _mod(shift_ref)
    hb = h.astype(BF16)
    c, su, sd = c_ref[...], su_ref[...], sd_ref[...]

    q = _dot(hb, w_ref[:, _C_Q:_C_Q + ATTN_WIDTH])
    qms = _dot((q * q).astype(BF16), gq_ref[...])
    q = q * lax.rsqrt(qms + EPS) * qw_ref[...]
    q = _rope(q, c, su, sd) * (HEAD_DIM ** -0.5 * LOG2E)
    lane = lax.broadcasted_iota(jnp.int32, q.shape, 1) % LANES
    lo = lane < HEAD_DIM
    q_up = pltpu.roll(q, ATTN_WIDTH - HEAD_DIM, axis=1)
    q_dn = pltpu.roll(q, HEAD_DIM, axis=1)
    zero = jnp.zeros_like(q)
    nat_lo = jnp.where(lo, q, zero)
    nat_hi = jnp.where(lo, zero, q)
    up_lo = jnp.where(lo, q_up, zero)
    dn_hi = jnp.where(lo, zero, q_dn)
    blocks = []
    for hd in range(N_Q_HEADS):
        pair = hd // 2
        sl = slice(pair * LANES, (pair + 1) * LANES)
        if hd < GQA_GROUP:
            blocks.append((nat_lo if hd % 2 == 0 else up_lo)[:, sl])
        else:
            blocks.append((dn_hi if hd % 2 == 0 else nat_hi)[:, sl])
    qp_ref[...] = jnp.concatenate(blocks, axis=1).astype(BF16)

    kv = _dot(hb, w_ref[:, _C_KV:_C_KV + 6 * KV_WIDTH])
    outs = []
    for br in range(3):
        k = kv[:, br * 2 * KV_WIDTH:br * 2 * KV_WIDTH + KV_WIDTH]
        v = kv[:, br * 2 * KV_WIDTH + KV_WIDTH:(br + 1) * 2 * KV_WIDTH]
        kms = _dot((k * k).astype(BF16), gk_ref[...])
        k = k * lax.rsqrt(kms + EPS) * kw_ref[:, br * KV_WIDTH:(br + 1) * KV_WIDTH]
        k = _rope(k, c, su, sd)
        outs += [k, v]
    kvrows = jnp.concatenate(outs[:4], axis=1)
    winrows = jnp.concatenate(outs[4:], axis=1)
    kvb_ref[...] = kvrows.astype(BF16)
    win_ref[...] = winrows
    winb_ref[...] = winrows.astype(BF16)
    if seq_layout:
        kvt_ref, cmpx_ref, stage_ref = rest
        tm = kvrows.shape[0]
        for r in range(4):
            kvt_ref[0, r] = jnp.transpose(kvrows[:, r * KV_WIDTH:(r + 1) * KV_WIDTH])
        for s in range(2):
            stage_ref[s] = kvrows[:, s * KV_WIDTH:(s + 1) * KV_WIDTH]
        _stride_block_store(stage_ref, cmpx_ref, tm)
    else:
        rest[0][...] = kvrows

    z_ref[...] = _dot(hb, w_ref[:, _C_Z:_C_Z + SSM_WIDTH])
    xbc_ref[...] = _dot(hb, w_ref[:, _C_XBC:_C_XBC + CONV_DIM])
    misc_ref[...] = _dot(hb, w_ref[:, _C_MISC:_C_MISC + LANES])


def inproj(x, mod3, mod_row0, norm_w, wp, q_norm_w, k_norm_w, pos, tm, seq_layout):
    b, t, d = x.shape
    n = b * t
    tiles_per_b = t // tm
    xf = x.reshape(n, d)
    c, su, sd = _rope_tables(pos)
    qw = jnp.tile(q_norm_w, N_Q_HEADS).reshape(1, ATTN_WIDTH)
    kw = jnp.concatenate([jnp.tile(k_norm_w[i], N_KV_HEADS) for i in range(3)]).reshape(1, 3 * KV_WIDTH)
    gq = _group_mean_matrix(ATTN_WIDTH)
    gk = _group_mean_matrix(KV_WIDTH)

    def mod_spec(col):
        return _mod_spec(mod3, col, tm, tiles_per_b, mod_row0)

    def tok(wd):
        return pl.BlockSpec((tm, wd), lambda i: (i, 0))

    def full(a):
        return pl.BlockSpec(a.shape, lambda i: (0,) * a.ndim)

    rope_spec = pl.BlockSpec((tm, LANES), lambda i: (i % tiles_per_b, 0))
    out_shape = [
        jax.ShapeDtypeStruct((n, N_Q_HEADS * LANES), BF16),
        jax.ShapeDtypeStruct((n, 4 * KV_WIDTH), BF16),
        jax.ShapeDtypeStruct((n, 2 * KV_WIDTH), F32),
        jax.ShapeDtypeStruct((n, 2 * KV_WIDTH), BF16),
        jax.ShapeDtypeStruct((n, SSM_WIDTH), F32),
        jax.ShapeDtypeStruct((n, CONV_DIM), F32),
        jax.ShapeDtypeStruct((n, LANES), F32),
    ]
    out_specs = [tok(s.shape[1]) for s in out_shape]
    scratch = []
    if seq_layout:
        out_shape += [jax.ShapeDtypeStruct((b, 4, KV_WIDTH, t), F32),
                      jax.ShapeDtypeStruct((b, t // CMP_STRIDE, CMP_STRIDE * 2 * KV_WIDTH), BF16)]
        out_specs += [pl.BlockSpec((1, 4, KV_WIDTH, tm), lambda i: (i // tiles_per_b, 0, 0, i % tiles_per_b)),
                      pl.BlockSpec((1, tm // CMP_STRIDE, CMP_STRIDE * 2 * KV_WIDTH),
                                   lambda i: (i // tiles_per_b, i % tiles_per_b, 0))]
        scratch = [pltpu.VMEM((2, tm, KV_WIDTH), F32)]
    else:
        out_shape += [jax.ShapeDtypeStruct((n, 4 * KV_WIDTH), F32)]
        out_specs += [tok(4 * KV_WIDTH)]
    return pl.pallas_call(
        functools.partial(_inproj_kernel, seq_layout=seq_layout),
        grid=(n // tm,),
        in_specs=[tok(d), mod_spec(0), mod_spec(1), full(norm_w), full(wp), full(qw), full(kw), full(gq), full(gk),
                  rope_spec, rope_spec, rope_spec],
        out_specs=tuple(out_specs),
        out_shape=tuple(out_shape),
        scratch_shapes=scratch,
        compiler_params=_cparams(("arbitrary",)),
        name="inproj",
    )(xf, mod3, mod3, norm_w, wp, qw, kw, gq, gk, c, su, sd)


def _prep_compress(cmp_pe, cmp_w1, cmp_w2):
    half = CMP_LEN // 2
    eye = jnp.eye(N_KV_HEADS, dtype=F32)
    w1 = cmp_w1.reshape(2, CMP_LEN, HEAD_DIM, CMP_HIDDEN)
    w1s = []
    for part in (w1[:, :half], w1[:, half:]):
        w1s.append(jnp.einsum("pjdo,hg->pjhdgo", part, eye).reshape(2, half * KV_WIDTH, N_KV_HEADS * CMP_HIDDEN))
    w1p = jnp.concatenate(w1s, axis=2).astype(BF16)
    pe = cmp_pe.reshape(2, 2, half, 1, HEAD_DIM)
    pep = jnp.broadcast_to(pe, (2, 2, half, N_KV_HEADS, HEAD_DIM)).reshape(2, 2, half * KV_WIDTH)
    w2p = jnp.einsum("poe,hg->phoge", cmp_w2, eye).reshape(2, N_KV_HEADS * CMP_HIDDEN, KV_WIDTH).astype(BF16)
    return w1p, pep, w2p


def _compress_kernel(x_ref, w1_ref, pe_ref, w2_ref, o_ref, *, row_w):
    part = pl.program_id(1)
    nb = x_ref.shape[1]
    half = CMP_LEN // 2
    hid = N_KV_HEADS * CMP_HIDDEN
    cols = []
    for j in range(half):
        a = x_ref[0, :, j * row_w:j * row_w + KV_WIDTH]
        b = x_ref[0, :, j * row_w + KV_WIDTH:j * row_w + 2 * KV_WIDTH]
        cols.append(jnp.where(part == 0, a, b))
    x = jnp.concatenate(cols, axis=1).astype(F32)
    pe = pe_ref[0]
    u = _dot((x + pe[0:1]).astype(BF16), w1_ref[0, :, :hid])
    v = _dot((x + pe[1:2]).astype(BF16), w1_ref[0, :, hid:])
    h1 = u + pltpu.roll(v, nb - 1, axis=0)
    out = _dot(_silu(h1).astype(BF16), w2_ref[0])
    row = lax.broadcasted_iota(jnp.int32, out.shape, 0)
    o_ref[0, 0] = jnp.where(row < nb - 1, out, 0.0).astype(o_ref.dtype)


def compress(x, w1p, pep, w2p):
    b, nb, width = x.shape
    row_w = width // CMP_STRIDE
    return pl.pallas_call(
        functools.partial(_compress_kernel, row_w=row_w),
        grid=(b, 2),
        in_specs=[
            pl.BlockSpec((1, nb, CMP_STRIDE * row_w), lambda i, p: (i, 0, 0)),
            pl.BlockSpec((1,) + w1p.shape[1:], lambda i, p: (p, 0, 0)),
            pl.BlockSpec((1,) + pep.shape[1:], lambda i, p: (p, 0, 0)),
            pl.BlockSpec((1,) + w2p.shape[1:], lambda i, p: (p, 0, 0)),
        ],
        out_specs=pl.BlockSpec((1, 1, nb, KV_WIDTH), lambda i, p: (i, p, 0, 0)),
        out_shape=jax.ShapeDtypeStruct((b, 2, nb, KV_WIDTH), BF16),
        compiler_params=_cparams(("arbitrary", "arbitrary")),
        name="compress",
    )(x, w1p, pep, w2p)


N_SEL_LANES = LANES


def _cover_matrix(nb):
    c = np.arange(nb)[:, None]
    j = np.arange(N_SEL_LANES)[None, :]
    start = c * CMP_STRIDE
    m = (start < (j + 1) * SEL_BLOCK) & (start + CMP_LEN > j * SEL_BLOCK)
    return jnp.asarray(m.astype(np.float32), BF16)


def _place_heads(res, kv):
    lane = lax.broadcasted_iota(jnp.int32, res[0].shape, 1)
    lo = lane < HEAD_DIM
    blocks = []
    for pair in range(GQA_GROUP // 2):
        a, b = res[2 * pair], res[2 * pair + 1]
        if kv == 0:
            blocks.append(jnp.where(lo, a, pltpu.roll(b, HEAD_DIM, axis=1)))
        else:
            blocks.append(jnp.where(lo, pltpu.roll(a, HEAD_DIM, axis=1), b))
    return jnp.concatenate(blocks, axis=1)


def _group_rows(q_ref, kv):
    heads = range(kv * GQA_GROUP, (kv + 1) * GQA_GROUP)
    return jnp.concatenate([q_ref[0, :, hd * LANES:(hd + 1) * LANES] for hd in heads], axis=0)


def _heads_from_transposed(out_t, tq, kv):
    out = jnp.transpose(out_t)
    return _place_heads([out[g * tq:(g + 1) * tq] for g in range(GQA_GROUP)], kv)


def _cmp_select_kernel(q_ref, kc_ref, vc_ref, covt_ref, o_ref, m_ref, *, q_off, n_pick):
    tq = q_ref.shape[1]
    rows = GQA_GROUP * tq
    nb = kc_ref.shape[2]
    wl = max(tq, LANES)
    assert tq % LANES == 0 or rows == LANES
    t0 = q_off + pl.program_id(1) * tq
    kc = kc_ref[0, 0]
    vc = vc_ref[0, 0]
    qpos = t0 + lax.broadcasted_iota(jnp.int32, (nb, rows), 1) % tq
    cend = lax.broadcasted_iota(jnp.int32, (nb, rows), 0) * CMP_STRIDE + (CMP_LEN - 1)
    valid = cend <= qpos
    blk = lax.broadcasted_iota(jnp.int32, (N_SEL_LANES, wl), 0)
    cur = (t0 + lax.broadcasted_iota(jnp.int32, (N_SEL_LANES, wl), 1) % tq) // SEL_BLOCK
    forced = (blk == 0) | ((blk <= cur) & (blk > cur - N_LOCAL))
    o_groups = []
    for kv in range(N_KV_HEADS):
        s = _dot_nt(kc, _group_rows(q_ref, kv))
        s = jnp.where(valid, s, NEG)
        e = jnp.exp2(s - jnp.max(s, axis=0, keepdims=True))
        p = e / jnp.sum(e, axis=0, keepdims=True)
        p = jnp.where(valid, p, 0.0)
        o_t = lax.dot_general(vc, p.astype(BF16), (((0,), (0,)), ((), ())), preferred_element_type=F32)
        o_groups.append(_heads_from_transposed(o_t, tq, kv))
        if tq % LANES == 0:
            psum = sum(p[:, g * tq:(g + 1) * tq] for g in range(GQA_GROUP))
        else:
            psum = p + sum(pltpu.roll(p, g * tq, axis=1) for g in range(1, GQA_GROUP))
        hi, lo = _split2(psum)
        imp = _dot(covt_ref[...], hi) + _dot(covt_ref[...], lo)
        x = jnp.where(forced, BIG, jnp.where(blk > cur, -BIG, imp))
        sel = jnp.zeros(x.shape, jnp.bool_)
        for _ in range(n_pick):
            mx = jnp.max(x, axis=0, keepdims=True)
            idx = jnp.min(jnp.where(x == mx, blk, N_SEL_LANES), axis=0, keepdims=True)
            hit = blk == idx
            sel = sel | hit
            x = jnp.where(hit, -jnp.inf, x)
        mneg = jnp.transpose(jnp.where(sel, 0.0, NEG))
        m_ref[0, kv] = mneg[:tq].astype(m_ref.dtype)
    o_ref[0] = jnp.concatenate(o_groups, axis=1)


def cmp_select(qp, kcv, q_off, n_pick, tq):
    b, t, _ = qp.shape
    nb = kcv.shape[2]
    cover = jnp.transpose(_cover_matrix(nb))
    return pl.pallas_call(
        functools.partial(_cmp_select_kernel, q_off=q_off, n_pick=n_pick),
        grid=(b, t // tq),
        in_specs=[
            pl.BlockSpec((1, tq, N_Q_HEADS * LANES), lambda i, j: (i, j, 0)),
            pl.BlockSpec((1, 1, nb, KV_WIDTH), lambda i, j: (i, 0, 0, 0)),
            pl.BlockSpec((1, 1, nb, KV_WIDTH), lambda i, j: (i, 1, 0, 0)),
            pl.BlockSpec((N_SEL_LANES, nb), lambda i, j: (0, 0)),
        ],
        out_specs=(
            pl.BlockSpec((1, tq, ATTN_WIDTH), lambda i, j: (i, j, 0)),
            pl.BlockSpec((1, N_KV_HEADS, tq, N_SEL_LANES), lambda i, j: (i, 0, j, 0)),
        ),
        out_shape=(
            jax.ShapeDtypeStruct((b, t, ATTN_WIDTH), F32),
            jax.ShapeDtypeStruct((b, N_KV_HEADS, t, N_SEL_LANES), BF16),
        ),
        compiler_params=_cparams(("arbitrary", "arbitrary")),
        name="cmp_select",
    )(qp, kcv, kcv, cover)


SEL_TILE_ELEMS = 512 * 512
WIN_CHUNK = 256


def _block_onehot(s):
    key = np.arange(s)[:, None]
    j = np.arange(N_SEL_LANES)[None, :]
    return jnp.asarray((key // SEL_BLOCK == j).astype(np.float32), BF16)


def _gate_expand():
    m = np.zeros((3, LANES, ATTN_WIDTH), np.float32)
    for br in range(3):
        for hd in range(N_Q_HEADS):
            m[br, SSM_HEADS + 3 * hd + br, hd * HEAD_DIM:(hd + 1) * HEAD_DIM] = 1.0
    return jnp.asarray(m, BF16)


def _flash_update(ss, v, m_ref, acc_ref):
    lane = lax.broadcasted_iota(jnp.int32, v.shape, 1)
    one = jnp.ones(v.shape, v.dtype)
    stage = []
    for k, s in enumerate(ss):
        m_old = m_ref[k]
        m_new = jnp.maximum(m_old, jnp.max(s, axis=0, keepdims=True))
        alpha = jnp.exp2(m_old - m_new)
        p = jnp.exp2(s - m_new)
        m_ref[k] = m_new
        stage.append((alpha, p.astype(BF16)))
    for k, (alpha, p) in enumerate(stage):
        vk = jnp.where((lane < HEAD_DIM) == (k == 0), v, one)
        pv = lax.dot_general(vk, p, (((0,), (0,)), ((), ())), preferred_element_type=F32)
        acc_ref[k] = alpha * acc_ref[k] + pv


def _sel_chunk(rows, n_keys):
    chunk = SEL_TILE_ELEMS // rows
    while n_keys % chunk:
        chunk //= 2
    return chunk


def _sel_win_kernel(q_ref, mneg_ref, ksel_ref, vsel_ref, et_ref, kwin_ref, vwin_ref, ocmp_ref, misc_ref, eg_ref,
                    o_ref, lhs_ref, m_ref, acc_ref, *, q_off, win_pos0):
    tq = q_ref.shape[1]
    rows = GQA_GROUP * tq
    SEL_CHUNK = _sel_chunk(rows, ksel_ref.shape[1])
    t0 = q_off + pl.program_id(1) * tq
    n_sel = lax.shift_right_logical(t0 + tq - 1, int(math.log2(SEL_CHUNK))) + 1
    w_lo = jnp.maximum(t0 - (WINDOW - 1) - win_pos0, 0) // WIN_CHUNK
    w_hi = (t0 + tq - 1 - win_pos0) // WIN_CHUNK + 1

    def qrow(n_keys):
        return lax.broadcasted_iota(jnp.int32, (n_keys, rows), 1) % tq + t0

    def init():
        m_ref[...] = jnp.full(m_ref.shape, NEG, F32)
        acc_ref[...] = jnp.zeros(acc_ref.shape, F32)

    def finish():
        outs = []
        for kv in range(N_KV_HEADS):
            acc = acc_ref[kv]
            denom_row = HEAD_DIM * (1 - kv)
            outs.append(_heads_from_transposed(acc / acc[denom_row:denom_row + 1, :], tq, kv))
        return jnp.concatenate(outs, axis=1)

    for kv in range(N_KV_HEADS):
        for g in range(GQA_GROUP):
            hd = kv * GQA_GROUP + g
            lhs_ref[kv, g * tq:(g + 1) * tq, :LANES] = q_ref[0, :, hd * LANES:(hd + 1) * LANES]
            lhs_ref[kv, g * tq:(g + 1) * tq, LANES:] = mneg_ref[0, kv]

    init()

    def sel_step(c, carry, causal):
        r0 = pl.multiple_of(c * SEL_CHUNK, SEL_CHUNK)
        rhs = jnp.concatenate([ksel_ref[0, pl.ds(r0, SEL_CHUNK), :], et_ref[pl.ds(r0, SEL_CHUNK), :]], axis=1)
        v = vsel_ref[0, pl.ds(r0, SEL_CHUNK), :]
        if causal:
            ok = r0 + lax.broadcasted_iota(jnp.int32, (SEL_CHUNK, rows), 0) <= qrow(SEL_CHUNK)
        ss = [_dot_nt(rhs, lhs_ref[kv]) for kv in range(N_KV_HEADS)]
        if causal:
            ss = [jnp.where(ok, s, NEG) for s in ss]
        _flash_update(ss, v, m_ref, acc_ref)
        return carry

    n_full = lax.shift_right_logical(t0 + 1, int(math.log2(SEL_CHUNK)))
    lax.fori_loop(0, n_full, functools.partial(sel_step, causal=False), 0)
    lax.fori_loop(n_full, n_sel, functools.partial(sel_step, causal=True), 0)
    o_sel = finish()

    init()

    def win_step(c, carry):
        r0 = pl.multiple_of(c * WIN_CHUNK, WIN_CHUNK)
        k = kwin_ref[0, pl.ds(r0, WIN_CHUNK), :]
        v = vwin_ref[0, pl.ds(r0, WIN_CHUNK), :]
        wpos = win_pos0 + r0 + lax.broadcasted_iota(jnp.int32, (WIN_CHUNK, rows), 0)
        qr = qrow(WIN_CHUNK)
        ok = (wpos <= qr) & (wpos > qr - WINDOW)
        ss = [jnp.where(ok, _dot_nt(k, lhs_ref[kv, :, :LANES]), NEG) for kv in range(N_KV_HEADS)]
        _flash_update(ss, v, m_ref, acc_ref)
        return carry

    lax.fori_loop(w_lo, w_hi, win_step, 0)
    o_win = finish()

    gates = jax.nn.sigmoid(misc_ref[0])
    ghi = gates.astype(BF16)
    glo = (gates - ghi.astype(F32)).astype(BF16)
    branches = (ocmp_ref[0], o_sel, o_win)
    out = jnp.zeros(branches[0].shape, F32)
    for br in range(3):
        out = out + (_dot(ghi, eg_ref[br]) + _dot(glo, eg_ref[br])) * branches[br]
    o_ref[0] = out


def sel_win_attention(qp, mneg, kvb, sel_col, winb, o_cmp, misc, q_off, win_pos0, tq):
    b, t, _ = qp.shape
    s = kvb.shape[1]
    sw = winb.shape[1]
    et = _block_onehot(s)
    eg = _gate_expand()
    rows = GQA_GROUP * tq
    assert q_off + t <= s and q_off + t - win_pos0 <= sw and sw % WIN_CHUNK == 0
    return pl.pallas_call(
        functools.partial(_sel_win_kernel, q_off=q_off, win_pos0=win_pos0),
        grid=(b, t // tq),
        in_specs=[
            pl.BlockSpec((1, tq, N_Q_HEADS * LANES), lambda i, j: (i, j, 0)),
            pl.BlockSpec((1, N_KV_HEADS, tq, N_SEL_LANES), lambda i, j: (i, 0, j, 0)),
            pl.BlockSpec((1, s, KV_WIDTH), lambda i, j: (i, 0, sel_col)),
            pl.BlockSpec((1, s, KV_WIDTH), lambda i, j: (i, 0, sel_col + 1)),
            pl.BlockSpec((s, N_SEL_LANES), lambda i, j: (0, 0)),
            pl.BlockSpec((1, sw, KV_WIDTH), lambda i, j: (i, 0, 0)),
            pl.BlockSpec((1, sw, KV_WIDTH), lambda i, j: (i, 0, 1)),
            pl.BlockSpec((1, tq, ATTN_WIDTH), lambda i, j: (i, j, 0)),
            pl.BlockSpec((1, tq, LANES), lambda i, j: (i, j, 0)),
            pl.BlockSpec((3, LANES, ATTN_WIDTH), lambda i, j: (0, 0, 0)),
        ],
        out_specs=pl.BlockSpec((1, tq, ATTN_WIDTH), lambda i, j: (i, j, 0)),
        out_shape=jax.ShapeDtypeStruct((b, t, ATTN_WIDTH), F32),
        scratch_shapes=[
            pltpu.VMEM((N_KV_HEADS, rows, 2 * LANES), BF16),
            pltpu.VMEM((N_KV_HEADS, 1, rows), F32),
            pltpu.VMEM((N_KV_HEADS, LANES, rows), F32),
        ],
        compiler_params=_cparams(("arbitrary", "arbitrary")),
        name="sel_win_attention",
    )(qp, mneg, kvb, kvb, et, winb, winb, o_cmp, misc, eg)


CONV_PAD = 8
HEAD_PAIRS = SSM_HEADS // 2


def _split3(x):
    a = x.astype(BF16)
    r = x - a.astype(F32)
    b = r.astype(BF16)
    c = (r - b.astype(F32)).astype(BF16)
    return a, b, c


def _ssd_kernel(xbc_ref, z_ref, misc_ref, conv0_ref, h0_ref, cw_ref, cb_ref, dtb_ref, a_ref, dsk_ref, nw_ref,
                y_ref, hout_ref, cout_ref, xp_ref, h_ref, ms_ref, *, t_valid):
    ch = pl.program_id(1)
    L = SSD_CHUNK
    keep = CONV_WIDTH - 1

    @pl.when(ch == 0)
    def _():
        xp_ref[...] = jnp.zeros(xp_ref.shape, F32)
        xp_ref[CONV_PAD - keep:CONV_PAD, :] = conv0_ref[0]
        h_ref[...] = h0_ref[0]

    xp_ref[CONV_PAD:CONV_PAD + t_valid, :] = xbc_ref[0]
    conv = cb_ref[...]
    for j in range(CONV_WIDTH):
        conv = conv + cw_ref[j:j + 1, :] * xp_ref[CONV_PAD - keep + j:CONV_PAD - keep + j + L, :]
    last = xp_ref[CONV_PAD + t_valid - keep:CONV_PAD + t_valid, :]
    cout_ref[0] = last
    xp_ref[CONV_PAD - keep:CONV_PAD, :] = last
    xc = _silu(conv)

    row = lax.broadcasted_iota(jnp.int32, (L, LANES), 0)
    lane = lax.broadcasted_iota(jnp.int32, (L, LANES), 1)
    if t_valid == L:
        raw = misc_ref[0]
    else:
        ms_ref[...] = jnp.zeros(ms_ref.shape, F32)
        ms_ref[0:t_valid, :] = misc_ref[0]
        raw = ms_ref[...]
    v = raw + dtb_ref[...]
    dt = jnp.maximum(v, 0.0) + jnp.log(1.0 + jnp.exp(-jnp.abs(v)))
    dt = jnp.where((lane < SSM_HEADS) & (row < t_valid), dt, 0.0)
    da = dt * a_ref[...]
    tri = (lax.broadcasted_iota(jnp.int32, (L, L), 1) <= lax.broadcasted_iota(jnp.int32, (L, L), 0))
    trib = tri.astype(BF16)
    acum = sum(_dot(trib, part) for part in _split3(da))
    acum_t = jnp.transpose(acum)
    dt_t = jnp.transpose(dt)
    e_acum = jnp.exp(acum)
    e_last = jnp.exp(acum[L - 1:L, :])
    w_end = jnp.exp(acum[L - 1:L, :] - acum) * dt
    lo = lane < SSM_HEAD_DIM

    ys = []
    for pair in range(HEAD_PAIRS):
        grp = (2 * pair) // (SSM_HEADS // SSM_GROUPS)
        bg = xc[:, SSM_WIDTH + grp * SSM_STATE:SSM_WIDTH + (grp + 1) * SSM_STATE].astype(BF16)
        cg = xc[:, SSM_WIDTH + (SSM_GROUPS + grp) * SSM_STATE:SSM_WIDTH + (SSM_GROUPS + grp + 1) * SSM_STATE].astype(BF16)
        g = _dot_nt(cg, bg)
        xpair = xc[:, pair * LANES:(pair + 1) * LANES]
        y = jnp.zeros((L, LANES), F32)
        for sub in range(2):
            hd = 2 * pair + sub
            seg = acum[:, hd:hd + 1] - acum_t[hd:hd + 1, :]
            m = g * jnp.exp(jnp.where(tri, seg, NEG)) * dt_t[hd:hd + 1, :]
            xm = jnp.where(lo if sub == 0 else ~lo, xpair, 0.0)
            y = y + _dot(m.astype(BF16), xm.astype(BF16))
        col = lambda a: jnp.where(lo, a[:, 2 * pair:2 * pair + 1], a[:, 2 * pair + 1:2 * pair + 2])
        hp = h_ref[pair]
        y = y + _dot_nt(cg, hp.astype(BF16)) * col(e_acum)
        y = y + col(dsk_ref[...]) * xpair
        xw = (xpair * col(w_end)).astype(BF16)
        st = lax.dot_general(xw, bg, (((0,), (0,)), ((), ())), preferred_element_type=F32)
        prow = lax.broadcasted_iota(jnp.int32, (LANES, LANES), 0) < SSM_HEAD_DIM
        dec = jnp.where(prow, e_last[:, 2 * pair:2 * pair + 1], e_last[:, 2 * pair + 1:2 * pair + 2])
        h_ref[pair] = hp * dec + st
        ys.append(y)
    y = jnp.concatenate(ys, axis=1)
    if t_valid != L:
        y = y[:t_valid]
    y = y * _silu(z_ref[0])
    y = y * lax.rsqrt(jnp.mean(y * y, axis=-1, keepdims=True) + EPS) * nw_ref[...]
    y_ref[0] = y

    @pl.when(ch == pl.num_programs(1) - 1)
    def _():
        hout_ref[0] = h_ref[...]


def ssd(xbc, z, misc, conv0, h0, conv_w, conv_b, dt_bias, a_log, d_skip, norm_w):
    b, t, _ = xbc.shape
    L = SSD_CHUNK
    t_valid = L if t % L == 0 else t
    assert t_valid == L or t < L
    n_ch = max(t // L, 1)
    keep = CONV_WIDTH - 1
    pad8 = lambda v: jnp.pad(v.astype(F32), (0, LANES - SSM_HEADS)).reshape(1, LANES)
    dtb = pad8(dt_bias)
    a = pad8(-jnp.exp(a_log.astype(F32)))
    dsk = pad8(d_skip)
    h0p = h0.reshape(b, HEAD_PAIRS, 2 * SSM_HEAD_DIM, SSM_STATE)
    full = lambda arr: pl.BlockSpec(arr.shape, lambda i, c: (0,) * arr.ndim)
    tok = lambda wd: pl.BlockSpec((1, t_valid, wd), lambda i, c: (i, c, 0))
    y, hout, cout = pl.pallas_call(
        functools.partial(_ssd_kernel, t_valid=t_valid),
        grid=(b, n_ch),
        in_specs=[
            tok(CONV_DIM), tok(SSM_WIDTH), tok(LANES),
            pl.BlockSpec((1, keep, CONV_DIM), lambda i, c: (i, 0, 0)),
            pl.BlockSpec((1, HEAD_PAIRS, 2 * SSM_HEAD_DIM, SSM_STATE), lambda i, c: (i, 0, 0, 0)),
            full(conv_w), pl.BlockSpec((1, CONV_DIM), lambda i, c: (0, 0)),
            full(dtb), full(a), full(dsk), pl.BlockSpec((1, SSM_WIDTH), lambda i, c: (0, 0)),
        ],
        out_specs=(
            tok(SSM_WIDTH),
            pl.BlockSpec((1, HEAD_PAIRS, 2 * SSM_HEAD_DIM, SSM_STATE), lambda i, c: (i, 0, 0, 0)),
            pl.BlockSpec((1, keep, CONV_DIM), lambda i, c: (i, 0, 0)),
        ),
        out_shape=(
            jax.ShapeDtypeStruct((b, t, SSM_WIDTH), F32),
            jax.ShapeDtypeStruct((b, HEAD_PAIRS, 2 * SSM_HEAD_DIM, SSM_STATE), F32),
            jax.ShapeDtypeStruct((b, keep, CONV_DIM), F32),
        ),
        scratch_shapes=[
            pltpu.VMEM((CONV_PAD + L, CONV_DIM), F32),
            pltpu.VMEM((HEAD_PAIRS, 2 * SSM_HEAD_DIM, SSM_STATE), F32),
            pltpu.VMEM((L, LANES), F32),
        ],
        compiler_params=_cparams(("arbitrary", "arbitrary")),
        name="ssd",
    )(xbc, z, misc, conv0, h0p, conv_w, conv_b.reshape(1, CONV_DIM), dtb, a, dsk, norm_w.reshape(1, SSM_WIDTH))
    return y, hout.reshape(b, SSM_HEADS, SSM_HEAD_DIM, SSM_STATE), cout


def _split2(x):
    hi = x.astype(BF16)
    return hi, (x - hi.astype(F32)).astype(BF16)


def _merge_kernel(oa_ref, ys_ref, x_ref, g1_ref, sh2_ref, sc2_ref, anw_ref, wo_ref, n2w_ref, wrh_ref, wrl_ref,
                  x1_ref, h2_ref, lg_ref):
    oa = oa_ref[...]
    a = oa * lax.rsqrt(jnp.mean(oa * oa, axis=-1, keepdims=True) + EPS) * anw_ref[...]
    cat = jnp.concatenate([a.astype(BF16), ys_ref[...].astype(BF16)], axis=1)
    x1 = x_ref[...] + _mod(g1_ref) * _dot(cat, wo_ref[...])
    x1_ref[...] = x1
    h2 = x1 * lax.rsqrt(jnp.mean(x1 * x1, axis=-1, keepdims=True) + EPS) * n2w_ref[...]
    h2 = h2 * (1.0 + _mod(sc2_ref)) + _mod(sh2_ref)
    h2_ref[...] = h2.astype(BF16)
    hh, hl = _split2(h2)
    lg_ref[...] = _dot_nt(wrh_ref[...], hh) + _dot_nt(wrh_ref[...], hl) + _dot_nt(wrl_ref[...], hh)


def merge(o_attn, y_ssm, x, mod3, mod_row0, attn_norm_w, wo, norm2_w, w_router, tm):
    b, t, d = x.shape
    n = b * t
    tiles_per_b = t // tm
    wrt = jnp.transpose(w_router)
    wrh, wrl = _split2(wrt)

    def mod_spec(col):
        return _mod_spec(mod3, col, tm, tiles_per_b, mod_row0)

    tok = lambda wd: pl.BlockSpec((tm, wd), lambda i: (i, 0))
    full = lambda a: pl.BlockSpec(a.shape, lambda i: (0,) * a.ndim)
    return pl.pallas_call(
        _merge_kernel,
        grid=(n // tm,),
        in_specs=[tok(ATTN_WIDTH), tok(SSM_WIDTH), tok(d), mod_spec(2), mod_spec(3), mod_spec(4),
                  full(attn_norm_w), full(wo), full(norm2_w), full(wrh), full(wrl)],
        out_specs=(tok(d), tok(d), pl.BlockSpec((N_EXPERTS, tm), lambda i: (0, i))),
        out_shape=(jax.ShapeDtypeStruct((n, d), F32), jax.ShapeDtypeStruct((n, d), BF16),
                   jax.ShapeDtypeStruct((N_EXPERTS, n), F32)),
        compiler_params=_cparams(("arbitrary",)),
        name="merge",
    )(o_attn.reshape(n, ATTN_WIDTH), y_ssm.reshape(n, SSM_WIDTH), x.reshape(n, d), mod3, mod3, mod3,
      attn_norm_w, wo, norm2_w, wrh, wrl)


EXPERTS_PER_GROUP = N_EXPERTS // N_EXPERT_GROUPS


def _first_max(x, ids, axes, n_ids):
    mx = jnp.max(x, axis=axes, keepdims=True)
    return ids == jnp.min(jnp.where(x == mx, ids, n_ids), axis=axes, keepdims=True), mx


def _route_kernel(lg_ref, eb_ref, tri_ref, w_ref, pos_ref, cnt_ref):
    lg = lg_ref[...]
    tn = lg.shape[2]
    scores = jax.nn.sigmoid(lg)
    biased = scores + eb_ref[...]
    sub = lax.broadcasted_iota(jnp.int32, lg.shape, 1)
    grp = lax.broadcasted_iota(jnp.int32, (N_EXPERT_GROUPS, 1, tn), 0)
    eid = lax.broadcasted_iota(jnp.int32, lg.shape, 0) * EXPERTS_PER_GROUP + sub
    hit, m1 = _first_max(biased, sub, 1, EXPERTS_PER_GROUP)
    m2 = jnp.max(jnp.where(hit, -jnp.inf, biased), axis=1, keepdims=True)
    gs = m1 + m2
    keep = jnp.zeros(gs.shape, jnp.bool_)
    for _ in range(TOPK_GROUPS):
        hit, _m = _first_max(gs, grp, 0, N_EXPERT_GROUPS)
        keep = keep | hit
        gs = jnp.where(hit, -jnp.inf, gs)
    x = jnp.where(keep, biased, NEG)
    sel = jnp.zeros(lg.shape, jnp.bool_)
    for _ in range(TOP_K):
        hit, _m = _first_max(x, eid, (0, 1), N_EXPERTS)
        sel = sel | hit
        x = jnp.where(hit, -jnp.inf, x)
    w = jnp.where(sel, scores, 0.0)
    w = w / jnp.sum(w, axis=(0, 1), keepdims=True) * ROUTED_SCALE
    w_ref[...] = w
    selb = sel.astype(BF16).reshape(N_EXPERTS, tn)
    pos = _dot(selb, tri_ref[...])
    pos_ref[...] = jnp.where(sel, pos.reshape(lg.shape), -1.0)
    cnt = jnp.sum(sel.astype(F32), axis=2, keepdims=True)
    cnt_ref[0] = jnp.broadcast_to(cnt, cnt_ref.shape[1:]).astype(jnp.int32)


def route(logits_t, e_bias, tn):
    n = logits_t.shape[1]
    lg3 = logits_t.reshape(N_EXPERT_GROUPS, EXPERTS_PER_GROUP, n)
    eb = e_bias.astype(F32).reshape(N_EXPERT_GROUPS, EXPERTS_PER_GROUP, 1)
    tri = jnp.asarray(np.triu(np.ones((tn, tn), np.float32), 1), BF16)
    blk = pl.BlockSpec((N_EXPERT_GROUPS, EXPERTS_PER_GROUP, tn), lambda i: (0, 0, i))
    w, pos, cnt = pl.pallas_call(
        _route_kernel,
        grid=(n // tn,),
        in_specs=[blk, pl.BlockSpec(eb.shape, lambda i: (0, 0, 0)), pl.BlockSpec((tn, tn), lambda i: (0, 0))],
        out_specs=(blk, blk, pl.BlockSpec((1, N_EXPERT_GROUPS, EXPERTS_PER_GROUP, LANES), lambda i: (i, 0, 0, 0))),
        out_shape=(jax.ShapeDtypeStruct(lg3.shape, F32), jax.ShapeDtypeStruct(lg3.shape, F32),
                   jax.ShapeDtypeStruct((n // tn, N_EXPERT_GROUPS, EXPERTS_PER_GROUP, LANES), jnp.int32)),
        compiler_params=_cparams(("arbitrary",)),
        name="route",
    )(lg3, eb, tri)
    return w.reshape(N_EXPERTS, n), pos.reshape(N_EXPERTS, n), cnt[..., 0].reshape(n // tn, N_EXPERTS)


MOE_ROWS = 128


def _swiglu(xb, wgu, wd, width):
    gu = _dot(xb, wgu)
    act = _silu(gu[:, :width]) * gu[:, width:]
    return _dot(act.astype(BF16), wd)


MOE_EXPERTS_PER_STEP = 4


MOE_ALIGN = 16
MOE_GATHER_ROWS = 896


def _moe_slots(tm):
    worst = TOP_K * tm + N_EXPERTS * (MOE_ALIGN - 1) + MOE_ROWS
    return -(-worst // MOE_GATHER_ROWS) * MOE_GATHER_ROWS


def _moe_kernel(cnt_ref, start_ref, h2_ref, w_ref, pos_ref, x1_ref, g2_ref, wgu_ref, wd_ref, sgu_ref, sd_ref,
                o_ref, g_all, xs):
    i = pl.program_id(0)
    es = pl.program_id(1)
    tm = h2_ref.shape[0]
    slots = g_all.shape[0]
    slot = lax.broadcasted_iota(jnp.int32, (MOE_ROWS, tm), 0).astype(F32)
    row = lax.broadcasted_iota(jnp.int32, (MOE_ROWS, 1), 0)

    def n_windows(cnt):
        return (cnt + MOE_ROWS - 1) // MOE_ROWS

    def window_start(e, j):
        return pl.multiple_of(start_ref[i * N_EXPERTS + e] + j * MOE_ROWS, MOE_ALIGN)

    @pl.when(es == 0)
    def _():
        g_all[...] = jnp.zeros(g_all.shape, BF16)

        def mark(e, carry):
            pos = pos_ref[pl.ds(e, 1), :]

            def mark_window(j, carry):
                hit = pos == slot + (j * MOE_ROWS).astype(F32)
                g_all[pl.ds(window_start(e, j), MOE_ROWS), :] = hit.astype(BF16)
                return carry

            return lax.fori_loop(0, n_windows(cnt_ref[i * N_EXPERTS + e]), mark_window, carry)

        lax.fori_loop(0, N_EXPERTS, mark, 0)

        def gather(c, carry):
            r0 = pl.multiple_of(c * MOE_GATHER_ROWS, MOE_GATHER_ROWS)
            rows = _dot(g_all[pl.ds(r0, MOE_GATHER_ROWS), :], h2_ref[...])
            xs[pl.ds(r0, MOE_GATHER_ROWS), :] = rows.astype(BF16)
            return carry

        lax.fori_loop(0, slots // MOE_GATHER_ROWS, gather, 0)

    for q in range(MOE_EXPERTS_PER_STEP):
        e = es * MOE_EXPERTS_PER_STEP + q
        cnt = cnt_ref[i * N_EXPERTS + e]
        wrow = w_ref[pl.ds(e, 1), :]

        def window(j, carry, q=q, e=e, cnt=cnt, wrow=wrow):
            r0 = window_start(e, j)
            xg = xs[pl.ds(r0, MOE_ROWS), :]
            out = _swiglu(xg, wgu_ref[q], wd_ref[q], D_EXPERT)
            g = g_all[pl.ds(r0, MOE_ROWS), :].astype(F32)
            out = out * jnp.sum(g * wrow, axis=1, keepdims=True)
            mine = row < cnt - j * MOE_ROWS
            xs[pl.ds(r0, MOE_ROWS), :] = jnp.where(mine, out.astype(BF16), xg)
            return carry

        lax.fori_loop(0, n_windows(cnt), window, 0)

    @pl.when(es == pl.num_programs(1) - 1)
    def _():
        y = lax.dot_general(g_all[...], xs[...], (((0,), (0,)), ((), ())), preferred_element_type=F32)
        y = y + _swiglu(h2_ref[...], sgu_ref[...], sd_ref[...], D_SHARED)
        o_ref[...] = x1_ref[...] + _mod(g2_ref) * y


def moe(h2, w_t, pos_t, counts, x1, mod3, mod_row0, t_per_b, wgu, wd, sgu, sd, tm):
    n, d = h2.shape
    tiles_per_b = t_per_b // tm
    eps = MOE_EXPERTS_PER_STEP
    slots = _moe_slots(tm)
    padded = (counts + MOE_ALIGN - 1) // MOE_ALIGN * MOE_ALIGN
    starts = jnp.cumsum(padded, axis=1) - padded
    grid_spec = pltpu.PrefetchScalarGridSpec(
        num_scalar_prefetch=2,
        grid=(n // tm, N_EXPERTS // eps),
        in_specs=[
            pl.BlockSpec((tm, d), lambda i, e, *_: (i, 0)),
            pl.BlockSpec((N_EXPERTS, tm), lambda i, e, *_: (0, i)),
            pl.BlockSpec((N_EXPERTS, tm), lambda i, e, *_: (0, i)),
            pl.BlockSpec((tm, d), lambda i, e, *_: (i, 0)),
            _mod_spec(mod3, 5, tm, tiles_per_b, mod_row0),
            pl.BlockSpec((eps, d, 2 * D_EXPERT), lambda i, e, *_: (e, 0, 0)),
            pl.BlockSpec((eps, D_EXPERT, d), lambda i, e, *_: (e, 0, 0)),
            pl.BlockSpec(sgu.shape, lambda i, e, *_: (0, 0)),
            pl.BlockSpec(sd.shape, lambda i, e, *_: (0, 0)),
        ],
        out_specs=pl.BlockSpec((tm, d), lambda i, e, *_: (i, 0)),
        scratch_shapes=[pltpu.VMEM((slots, tm), BF16), pltpu.VMEM((slots, d), BF16)],
    )
    return pl.pallas_call(
        _moe_kernel,
        grid_spec=grid_spec,
        out_shape=jax.ShapeDtypeStruct((n, d), F32),
        compiler_params=_cparams(("arbitrary", "arbitrary")),
        name="moe",
    )(counts.reshape(-1), starts.reshape(-1).astype(jnp.int32), h2, w_t, pos_t, x1, mod3, wgu, wd, sgu, sd)


SC_WINDOW = 128
PACK_W = 256
MOE_BLOCK_ROWS = 256
HI_MASK = -65536


def _pack_pair(x):
    bits = pltpu.bitcast(x.astype(BF16).astype(F32), jnp.int32)
    return lax.shift_right_logical(bits[:, :PACK_W], 16) | (bits[:, PACK_W:] & HI_MASK)


def _unpack_pair(word):
    lo = pltpu.bitcast(lax.shift_left(word, 16), F32)
    hi = pltpu.bitcast(word & HI_MASK, F32)
    return jnp.concatenate([lo, hi], axis=1)


def _pack_kernel(x_ref, a_ref, b_ref):
    x = x_ref[...]
    a_ref[...] = _pack_pair(x[:, :2 * PACK_W])
    b_ref[...] = _pack_pair(x[:, 2 * PACK_W:])


def pack_rows(x, tm):
    n, d = x.shape
    tok = lambda wd: pl.BlockSpec((tm, wd), lambda i: (i, 0))
    return pl.pallas_call(
        _pack_kernel, grid=(n // tm,), in_specs=[tok(d)], out_specs=(tok(PACK_W), tok(PACK_W)),
        out_shape=(jax.ShapeDtypeStruct((n, PACK_W), jnp.int32),) * 2,
        compiler_params=_cparams(("arbitrary",)), name="pack_rows",
    )(x)


def _slots_kernel(w_ref, pos_ref, base_ref, tri_ref, slot_ref, wt_ref):
    w = w_ref[...]
    pos = pos_ref[...]
    sel = pos >= 0.0
    rank = _dot(tri_ref[...], sel.astype(BF16))
    dest = base_ref[0] + pos
    slots, wts = [], []
    for j in range(TOP_K):
        mine = sel & (rank == float(j))
        slots.append(jnp.sum(jnp.where(mine, dest, 0.0), axis=0, keepdims=True))
        wts.append(jnp.sum(jnp.where(mine, w, 0.0), axis=0, keepdims=True))
    slot_ref[...] = jnp.concatenate(slots, axis=0).astype(jnp.int32)
    wpad = jnp.concatenate(wts + [jnp.zeros((LANES - TOP_K, w.shape[1]), F32)], axis=0)
    wt_ref[...] = jnp.transpose(wpad)


def slots_of(w_t, pos_t, base, tn):
    n = w_t.shape[1]
    tri = jnp.asarray(np.tril(np.ones((N_EXPERTS, N_EXPERTS), np.float32), -1), BF16)
    blk = pl.BlockSpec((N_EXPERTS, tn), lambda i: (0, i))
    return pl.pallas_call(
        _slots_kernel, grid=(n // tn,),
        in_specs=[blk, blk, pl.BlockSpec((1, N_EXPERTS, 1), lambda i: (i, 0, 0)),
                  pl.BlockSpec((N_EXPERTS, N_EXPERTS), lambda i: (0, 0))],
        out_specs=(pl.BlockSpec((TOP_K, tn), lambda i: (0, i)), pl.BlockSpec((tn, LANES), lambda i: (i, 0))),
        out_shape=(jax.ShapeDtypeStruct((TOP_K, n), jnp.int32), jax.ShapeDtypeStruct((n, LANES), F32)),
        compiler_params=_cparams(("arbitrary",)), name="moe_slots",
    )(w_t, pos_t, base, tri)


def sc_scatter_rows(rows, idx, n_out):
    n, d = rows.shape
    m = idx.shape[0]
    nb = n // SC_WINDOW
    mesh = plsc.VectorSubcoreMesh(core_axis_name="core", subcore_axis_name="subcore")

    @functools.partial(pl.kernel, out_type=jax.ShapeDtypeStruct((n_out, d), rows.dtype), mesh=mesh)
    def scatter(x_hbm, i_hbm, o_hbm):
        def body(x_vmem, i_vmem):
            pltpu.sync_copy(x_vmem, o_hbm.at[i_vmem.at[0]])

        pltpu.emit_pipeline(
            body, grid=(m // SC_WINDOW,),
            in_specs=[pl.BlockSpec((SC_WINDOW, d), index_map=lambda i: (i % nb, 0)),
                      pl.BlockSpec((1, SC_WINDOW), index_map=lambda i: (0, i))],
            out_specs=[], core_axis_name=("core", "subcore"), dimension_semantics=(pltpu.PARALLEL,),
        )(x_hbm, i_hbm)

    return scatter(rows, idx.reshape(1, m))


def sc_gather_rows(table, idx):
    d = table.shape[1]
    m = idx.shape[0]
    mesh = plsc.VectorSubcoreMesh(core_axis_name="core", subcore_axis_name="subcore")

    @functools.partial(pl.kernel, out_type=jax.ShapeDtypeStruct((m, d), table.dtype), mesh=mesh)
    def gather(x_hbm, i_hbm, o_hbm):
        def body(i_vmem, o_vmem):
            pltpu.sync_copy(x_hbm.at[i_vmem.at[0]], o_vmem)

        pltpu.emit_pipeline(
            body, grid=(m // SC_WINDOW,),
            in_specs=[pl.BlockSpec((1, SC_WINDOW), index_map=lambda i: (0, i))],
            out_specs=[pl.BlockSpec((SC_WINDOW, d), index_map=lambda i: (i, 0))],
            core_axis_name=("core", "subcore"), dimension_semantics=(pltpu.PARALLEL,),
        )(i_hbm, o_hbm)

    return gather(table, idx.reshape(1, m))


def _experts_kernel(be_ref, nu_ref, xa_ref, xb_ref, wgu_ref, wd_ref, oa_ref, ob_ref):
    @pl.when(pl.program_id(0) < nu_ref[0])
    def _():
        x = jnp.concatenate([_unpack_pair(xa_ref[...]), _unpack_pair(xb_ref[...])], axis=1).astype(BF16)
        out = _swiglu(x, wgu_ref[0], wd_ref[0], D_EXPERT)
        oa_ref[...] = _pack_pair(out[:, :2 * PACK_W])
        ob_ref[...] = _pack_pair(out[:, 2 * PACK_W:])


def experts_sorted(xa, xb, block_expert, n_used, wgu, wd):
    r = xa.shape[0]
    d = wd.shape[2]
    row = lambda b, be, nu: (jnp.minimum(b, nu[0] - 1), 0)
    blk = pl.BlockSpec((MOE_BLOCK_ROWS, PACK_W), row)
    grid_spec = pltpu.PrefetchScalarGridSpec(
        num_scalar_prefetch=2, grid=(r // MOE_BLOCK_ROWS,),
        in_specs=[blk, blk,
                  pl.BlockSpec((1, d, 2 * D_EXPERT), lambda b, be, nu: (be[b], 0, 0)),
                  pl.BlockSpec((1, D_EXPERT, d), lambda b, be, nu: (be[b], 0, 0))],
        out_specs=(blk, blk),
    )
    return pl.pallas_call(
        _experts_kernel, grid_spec=grid_spec,
        out_shape=(jax.ShapeDtypeStruct((r, PACK_W), jnp.int32),) * 2,
        compiler_params=_cparams(("arbitrary",)), name="moe_experts",
    )(block_expert, n_used, xa, xb, wgu, wd)


def _combine_kernel(ya_ref, yb_ref, wt_ref, h2_ref, x1_ref, g2_ref, sgu_ref, sd_ref, o_ref):
    wt = wt_ref[...]
    acc = _swiglu(h2_ref[...], sgu_ref[...], sd_ref[...], D_SHARED)
    for j in range(TOP_K):
        y = jnp.concatenate([_unpack_pair(ya_ref[j]), _unpack_pair(yb_ref[j])], axis=1)
        acc = acc + wt[:, j:j + 1] * y
    o_ref[...] = x1_ref[...] + _mod(g2_ref) * acc


def combine_sorted(ya, yb, wt, h2, x1, mod3, mod_row0, t_per_b, sgu, sd, tm):
    n, d = h2.shape
    tiles_per_b = t_per_b // tm
    tok = lambda wd: pl.BlockSpec((tm, wd), lambda i: (i, 0))
    yblk = pl.BlockSpec((TOP_K, tm, PACK_W), lambda i: (0, i, 0))
    full = lambda a: pl.BlockSpec(a.shape, lambda i: (0,) * a.ndim)
    return pl.pallas_call(
        _combine_kernel, grid=(n // tm,),
        in_specs=[yblk, yblk, tok(LANES), tok(d), tok(d), _mod_spec(mod3, 5, tm, tiles_per_b, mod_row0),
                  full(sgu), full(sd)],
        out_specs=tok(d), out_shape=jax.ShapeDtypeStruct((n, d), F32),
        compiler_params=_cparams(("arbitrary",)), name="moe_combine",
    )(ya, yb, wt, h2, x1, mod3, sgu, sd)


def moe_sorted(h2, w_t, pos_t, counts, x1, mod3, mod_row0, t_per_b, wgu, wd, sgu, sd, tm):
    n, d = h2.shape
    assert d == 4 * PACK_W and n % SC_WINDOW == 0
    n_blocks = (TOP_K * n + N_EXPERTS * (MOE_BLOCK_ROWS - 1)) // MOE_BLOCK_ROWS
    total = jnp.sum(counts, axis=0)
    region = (total + MOE_BLOCK_ROWS - 1) // MOE_BLOCK_ROWS * MOE_BLOCK_ROWS
    region_end = jnp.cumsum(region)
    base = (region_end - region)[None, :] + jnp.cumsum(counts, axis=0) - counts
    block_expert = jnp.searchsorted(region_end, jnp.arange(n_blocks) * MOE_BLOCK_ROWS, side="right")
    block_expert = jnp.minimum(block_expert, N_EXPERTS - 1).astype(jnp.int32)
    n_used = (region_end[-1:] // MOE_BLOCK_ROWS).astype(jnp.int32)
    slot, wt = slots_of(w_t, pos_t, base.astype(F32).reshape(-1, N_EXPERTS, 1), tm)
    dest = slot.reshape(-1)
    ha, hb = pack_rows(h2, tm)
    rows = n_blocks * MOE_BLOCK_ROWS
    xa, xb = sc_scatter_rows(ha, dest, rows), sc_scatter_rows(hb, dest, rows)
    oa, ob = experts_sorted(xa, xb, block_expert, n_used, wgu, wd)
    ya = sc_gather_rows(oa, dest).reshape(TOP_K, n, PACK_W)
    yb = sc_gather_rows(ob, dest).reshape(TOP_K, n, PACK_W)
    return combine_sorted(ya, yb, wt, h2, x1, mod3, mod_row0, t_per_b, sgu, sd, tm)


GATHER_PAGES = 8


def _gather_kernel(pt_ref, *refs):
    pages, new_ref = refs[:GATHER_PAGES], refs[GATHER_PAGES]
    rows_ref, cmpx_ref, stage_ref = refs[GATHER_PAGES + 1:]
    step = pl.program_id(1)
    last = pl.num_programs(1) - 1
    n_rows = GATHER_PAGES * PAGE_SIZE

    @pl.when(step < last)
    def _():
        for k in range(GATHER_PAGES):
            sl = slice(k * PAGE_SIZE, (k + 1) * PAGE_SIZE)
            for r in range(4):
                tile = jnp.transpose(pages[k][0, r])
                if r < 2:
                    stage_ref[r, sl, :] = tile
                else:
                    rows_ref[0, sl, (r - 2) * KV_WIDTH:(r - 1) * KV_WIDTH] = tile.astype(BF16)

    @pl.when(step == last)
    def _():
        new = new_ref[0]
        tn = new.shape[0]
        stage_ref[...] = jnp.zeros(stage_ref.shape, F32)
        for s in range(2):
            stage_ref[s, 0:tn, :] = new[:, s * KV_WIDTH:(s + 1) * KV_WIDTH]
        pad = jnp.zeros((n_rows - tn, 2 * KV_WIDTH), F32)
        rows_ref[0] = jnp.concatenate([new[:, 2 * KV_WIDTH:], pad], axis=0).astype(BF16)

    _stride_block_store(stage_ref, cmpx_ref, n_rows)


def gather_pages(cache_t, page_table, new_rows):
    b, n_pages = page_table.shape
    steps = n_pages // GATHER_PAGES
    rows = GATHER_PAGES * PAGE_SIZE
    s_out = (steps + 1) * rows

    def page_spec(k):
        def idx(i, s, pt):
            p = jnp.minimum(s, steps - 1) * GATHER_PAGES + k
            return (pt[i * n_pages + p], 0, 0, 0)
        return pl.BlockSpec((1, 4, KV_WIDTH, PAGE_SIZE), idx)

    grid_spec = pltpu.PrefetchScalarGridSpec(
        num_scalar_prefetch=1,
        grid=(b, steps + 1),
        in_specs=[page_spec(k) for k in range(GATHER_PAGES)]
        + [pl.BlockSpec((1,) + new_rows.shape[1:], lambda i, s, pt: (i, 0, 0))],
        out_specs=(
            pl.BlockSpec((1, rows, 2 * KV_WIDTH), lambda i, s, pt: (i, s, 0)),
            pl.BlockSpec((1, rows // CMP_STRIDE, CMP_STRIDE * 2 * KV_WIDTH), lambda i, s, pt: (i, s, 0)),
        ),
        scratch_shapes=[pltpu.VMEM((2, rows, KV_WIDTH), F32)],
    )
    return pl.pallas_call(
        _gather_kernel,
        grid_spec=grid_spec,
        out_shape=(jax.ShapeDtypeStruct((b, s_out, 2 * KV_WIDTH), BF16),
                   jax.ShapeDtypeStruct((b, s_out // CMP_STRIDE, CMP_STRIDE * 2 * KV_WIDTH), BF16)),
        compiler_params=_cparams(("arbitrary", "arbitrary")),
        name="gather_pages",
    )(page_table.reshape(-1), *([cache_t] * GATHER_PAGES), new_rows)


def _attention(qp, cmpx, kvb, sel_col, winb, misc, cmp_w, q_off, win_pos0, tq):
    t = qp.shape[1]
    cur_lo, cur_hi = q_off // SEL_BLOCK, (q_off + t - 1) // SEL_BLOCK
    assert cur_hi < N_SEL_LANES or (cur_lo == cur_hi == N_SEL_LANES), (q_off, t)
    n_pick = N_SEL - (1 if cur_hi >= N_SEL_LANES else 0)
    kcv = compress(cmpx, *cmp_w)
    o_cmp, mneg = cmp_select(qp, kcv, q_off, n_pick, tq)
    return sel_win_attention(qp, mneg, kvb, sel_col, winb, o_cmp, misc, q_off, win_pos0, tq)


def kernel(x_prompt, x_sample, cache_kv, cache_win, state_ssm, state_conv, page_table, c_prompt, c_sample, w_ada, b_ada, norm1_w, norm2_w, w_in, q_norm_w, k_norm_w, cmp_pe, cmp_w1, cmp_w2, attn_out_norm_w, conv_w, conv_b, dt_bias, a_log, d_skip, ssm_norm_w, w_out, w_router, e_bias, w_exp_gu, w_exp_down, w_sh_gu, w_sh_down):
    xp, xq = x_prompt, x_sample
    bp, tp, d = xp.shape
    bq, tq, _ = xq.shape
    depth = w_ada.shape[0]
    past_len = page_table.shape[1] * PAGE_SIZE
    nq = bq * tq
    tq_pad = LANES // GQA_GROUP
    assert tp % 512 == 0 and tp >= WINDOW and nq % 8 == 0 and tq <= tq_pad
    pos_p = jnp.arange(tp, dtype=jnp.int32)
    pos_q = jnp.tile(past_len + jnp.arange(tq, dtype=jnp.int32), bq)
    c_all = jnp.concatenate([c_prompt, c_sample], axis=0)
    c_all = jnp.pad(c_all, ((0, -c_all.shape[0] % 8), (0, 0)))
    outs = [[] for _ in range(8)]
    for l in range(depth):
        mod = adaln_all(c_all, w_ada[l], b_ada[l])
        mod_p = mod.reshape(mod.shape[0], 1, 6 * d)
        mod_q = jnp.repeat(mod[bp:bp + bq], tq, axis=0)
        wp = _prep_w_in(w_in[l])
        cmp_w = _prep_compress(cmp_pe[l], cmp_w1[l], cmp_w2[l])
        wo = w_out[l].astype(BF16)
        wgu, wd = w_exp_gu[l].astype(BF16), w_exp_down[l].astype(BF16)
        sgu, sd = w_sh_gu[l].astype(BF16), w_sh_down[l].astype(BF16)
        ssm_w = (conv_w[l], conv_b[l], dt_bias[l], a_log[l], d_skip[l], ssm_norm_w[l])
        n1w, n2w, anw = norm1_w[l:l + 1], norm2_w[l:l + 1], attn_out_norm_w[l:l + 1]

        qp, kvb, win, winb, z, xbc, misc, kvt, cmpx = inproj(xp, mod_p, 0, n1w, wp, q_norm_w[l], k_norm_w[l], pos_p,
                                                            512, True)
        r3 = lambda a: a.reshape(bp, tp, a.shape[-1])
        o_attn = _attention(r3(qp), cmpx, r3(kvb), 2, r3(winb), r3(misc), cmp_w, 0, 0, 128)
        y_ssm, h_new, conv_new = ssd(r3(xbc), r3(z), r3(misc), jnp.zeros((bp, CONV_WIDTH - 1, CONV_DIM), F32),
                                     jnp.zeros((bp, SSM_HEADS, SSM_HEAD_DIM, SSM_STATE), F32), *ssm_w)
        x1, h2, lg = merge(o_attn, y_ssm, xp, mod_p, 0, anw, wo, n2w, w_router[l], 512)
        w_t, pos_t, cnt = route(lg, e_bias[l], 512)
        xp = moe_sorted(h2, w_t, pos_t, cnt, x1, mod_p, 0, tp, wgu, wd, sgu, sd, 512).reshape(bp, tp, d)
        outs[0].append(jnp.transpose(kvt.reshape(bp, 4, N_KV_HEADS, HEAD_DIM, tp), (0, 4, 1, 2, 3)))
        outs[1].append(win.reshape(bp, tp, 2, N_KV_HEADS, HEAD_DIM)[:, tp - WINDOW:])
        outs[2].append(h_new)
        outs[3].append(conv_new)

        xq1 = xq.reshape(1, nq, d)
        qp, kvb, win, winb, z, xbc, misc, kv = inproj(xq1, mod_q, 0, n1w, wp, q_norm_w[l], k_norm_w[l], pos_q, nq,
                                                      False)
        rq = lambda a: a.reshape(bq, tq, a.shape[-1])
        padq = lambda a: jnp.pad(rq(a), ((0, 0), (0, tq_pad - tq), (0, 0)))
        cache_t = jnp.transpose(cache_kv[l], (0, 2, 3, 4, 1)).reshape(cache_kv.shape[1], 4, KV_WIDTH, PAGE_SIZE)
        past, cmpx = gather_pages(cache_t, page_table, rq(kv))
        win_all = jnp.concatenate([cache_win[l].reshape(bq, WINDOW, 2 * KV_WIDTH).astype(BF16), rq(winb),
                                   jnp.zeros((bq, -(WINDOW + tq_pad) % WIN_CHUNK + tq_pad - tq, 2 * KV_WIDTH), BF16)],
                                  axis=1)
        o_attn = _attention(padq(qp), cmpx, past, 0, win_all, padq(misc), cmp_w, past_len, past_len - WINDOW,
                            tq_pad)[:, :tq]
        y_ssm, h_new, conv_new = ssd(rq(xbc), rq(z), rq(misc), state_conv[l], state_ssm[l], *ssm_w)
        x1, h2, lg = merge(o_attn.reshape(1, nq, ATTN_WIDTH), y_ssm.reshape(1, nq, SSM_WIDTH), xq1, mod_q, 0,
                           anw, wo, n2w, w_router[l], nq)
        w_t, pos_t, cnt = route(lg, e_bias[l], nq)
        xq = moe(h2, w_t, pos_t, cnt, x1, mod_q, 0, nq, wgu, wd, sgu, sd, nq).reshape(bq, tq, d)
        win_rows = win.reshape(bq, tq, 2, N_KV_HEADS, HEAD_DIM)
        outs[4].append(kv.reshape(bq, tq, 4, N_KV_HEADS, HEAD_DIM))
        outs[5].append(jnp.concatenate([cache_win[l], win_rows.astype(cache_win.dtype)], axis=1)[:, tq:])
        outs[6].append(h_new)
        outs[7].append(conv_new)
    return (xp, xq) + tuple(jnp.stack(o) for o in outs)
```

```python
import functools
import math

import jax
import jax.numpy as jnp
import numpy as np
from jax import lax
from jax.experimental import pallas as pl
from jax.experimental.pallas import tpu as pltpu
from jax.experimental.pallas import tpu_sc as plsc

D_MODEL = 1024
PAGE_SIZE = 128
HEAD_DIM = 64
N_Q_HEADS = 8
N_KV_HEADS = 2
GQA_GROUP = N_Q_HEADS // N_KV_HEADS
ATTN_WIDTH = N_Q_HEADS * HEAD_DIM
KV_WIDTH = N_KV_HEADS * HEAD_DIM
ROPE_DIM = HEAD_DIM // 4
ROPE_THETA = 500000.0
CMP_LEN = 32
CMP_STRIDE = 16
CMP_HIDDEN = 4 * HEAD_DIM
SEL_BLOCK = 64
N_SEL = 16
N_LOCAL = 2
WINDOW = 512
SSM_HEADS = 8
SSM_HEAD_DIM = 64
SSM_WIDTH = SSM_HEADS * SSM_HEAD_DIM
SSM_GROUPS = 2
SSM_STATE = 128
CONV_WIDTH = 4
CONV_DIM = SSM_WIDTH + 2 * SSM_GROUPS * SSM_STATE
SSD_CHUNK = 128
MIX_WIDTH = ATTN_WIDTH + SSM_WIDTH
N_EXPERTS = 64
N_EXPERT_GROUPS = 8
TOPK_GROUPS = 4
TOP_K = 8
D_EXPERT = 256
D_SHARED = 256
ROUTED_SCALE = 2.5
IN_SIZES = (ATTN_WIDTH, 6 * KV_WIDTH, 3 * N_Q_HEADS, SSM_WIDTH, CONV_DIM, SSM_HEADS)
N_IN = sum(IN_SIZES)
EPS = 1e-6
NEG = -1e30
BIG = 1e6

LANES = 128
VMEM_LIMIT = 56 * 1024 * 1024

BF16 = jnp.bfloat16
F32 = jnp.float32
LOG2E = math.log2(math.e)


def _cparams(sem, flags=None):
    return pltpu.CompilerParams(dimension_semantics=sem, vmem_limit_bytes=VMEM_LIMIT, flags=flags)


def _silu(x):
    return x * jax.nn.sigmoid(x)


def _dot(a, b):
    return jnp.dot(a, b, preferred_element_type=F32)


def _dot_nt(a, b):
    return lax.dot_general(a, b, (((1,), (1,)), ((), ())), preferred_element_type=F32)


def _mod_spec(mod, col, tm, tiles_per_b, row0):
    if mod.ndim == 3:
        return pl.BlockSpec((1, 1, D_MODEL), lambda i, *_: (row0 + i // tiles_per_b, 0, col))
    return pl.BlockSpec((tm, D_MODEL), lambda i, *_: (i, col))


def _mod(ref):
    return ref[0] if len(ref.shape) == 3 else ref[...]


def _adaln_kernel(c_ref, w_ref, b_ref, o_ref):
    c = c_ref[...]
    a = _silu(c).astype(BF16)
    o_ref[...] = _dot(a, w_ref[...].astype(BF16)) + b_ref[...]


def adaln_all(c_all, w_ada, b_ada):
    rows = c_all.shape[0]
    n = w_ada.shape[1]
    tn = 1024
    return pl.pallas_call(
        _adaln_kernel,
        grid=(n // tn,),
        in_specs=[
            pl.BlockSpec((rows, D_MODEL), lambda j: (0, 0)),
            pl.BlockSpec((D_MODEL, tn), lambda j: (0, j)),
            pl.BlockSpec((1, tn), lambda j: (0, j)),
        ],
        out_specs=pl.BlockSpec((rows, tn), lambda j: (0, j)),
        out_shape=jax.ShapeDtypeStruct((rows, n), F32),
        compiler_params=_cparams(("arbitrary",)),
        name="adaln",
    )(c_all, w_ada, b_ada.reshape(1, n))


_C_Q = 0
_C_KV = _C_Q + ATTN_WIDTH
_C_Z = _C_KV + 6 * KV_WIDTH
_C_XBC = _C_Z + SSM_WIDTH
_C_MISC = _C_XBC + CONV_DIM
N_IN_PAD = _C_MISC + LANES
N_GATES = 3 * N_Q_HEADS


def _prep_w_in(w_in):
    s = np.cumsum((0,) + IN_SIZES)
    q, kv, g, z, xbc, dt = (w_in[:, int(s[i]):int(s[i + 1])] for i in range(6))
    pad = jnp.zeros((w_in.shape[0], LANES - N_GATES - SSM_HEADS), w_in.dtype)
    return jnp.concatenate([q, kv, z, xbc, dt, g, pad], axis=1).astype(BF16)


def _group_mean_matrix(width):
    i = np.arange(width)
    m = (i[:, None] // HEAD_DIM == i[None, :] // HEAD_DIM).astype(np.float32) / HEAD_DIM
    return jnp.asarray(m, BF16)


def _rope_tables(pos):
    half = ROPE_DIM // 2
    inv_freq = ROPE_THETA ** (-jnp.arange(half, dtype=F32) / half)
    ang = pos.astype(F32)[:, None] * inv_freq[None, :]
    cos, sin = jnp.cos(ang), jnp.sin(ang)
    t = pos.shape[0]
    one = jnp.ones((t, HEAD_DIM - ROPE_DIM), F32)
    zero = jnp.zeros((t, HEAD_DIM - ROPE_DIM), F32)
    zh = jnp.zeros((t, half), F32)
    c = jnp.concatenate([cos, cos, one], axis=1)
    s_up = jnp.concatenate([-sin, zh, zero], axis=1)
    s_dn = jnp.concatenate([zh, sin, zero], axis=1)
    rep = LANES // HEAD_DIM
    return jnp.tile(c, (1, rep)), jnp.tile(s_up, (1, rep)), jnp.tile(s_dn, (1, rep))


def _rope(x, c, s_up, s_dn):
    w = x.shape[1]
    half = ROPE_DIM // 2
    rep = w // LANES
    ct = jnp.concatenate([c] * rep, axis=1) if rep > 1 else c
    su = jnp.concatenate([s_up] * rep, axis=1) if rep > 1 else s_up
    sd = jnp.concatenate([s_dn] * rep, axis=1) if rep > 1 else s_dn
    up = pltpu.roll(x, w - half, axis=1)
    dn = pltpu.roll(x, half, axis=1)
    return x * ct + up * su + dn * sd


def _stride_block_store(stage_ref, cmpx_ref, n_rows):
    for j in range(CMP_STRIDE):
        for s in range(2):
            rows_j = stage_ref[s, pl.ds(j, n_rows // CMP_STRIDE, stride=CMP_STRIDE), :]
            c0 = (2 * j + s) * KV_WIDTH
            cmpx_ref[0, :, c0:c0 + KV_WIDTH] = rows_j.astype(BF16)


def _inproj_kernel(x_ref, shift_ref, scale_ref, nw_ref, w_ref, qw_ref, kw_ref, gq_ref, gk_ref,
                   c_ref, su_ref, sd_ref,
                   qp_ref, kvb_ref, win_ref, winb_ref, z_ref, xbc_ref, misc_ref, *rest, seq_layout):
    x = x_ref[...]
    ms = jnp.mean(x * x, axis=-1, keepdims=True)
    h = x * lax.rsqrt(ms + EPS) * nw_ref[...]
    h = h * (1.0 + _mod(scale_ref)) + _mod(shift_ref)
    hb = h.astype(BF16)
    c, su, sd = c_ref[...], su_ref[...], sd_ref[...]

    q = _dot(hb, w_ref[:, _C_Q:_C_Q + ATTN_WIDTH])
    qms = _dot((q * q).astype(BF16), gq_ref[...])
    q = q * lax.rsqrt(qms + EPS) * qw_ref[...]
    q = _rope(q, c, su, sd) * (HEAD_DIM ** -0.5 * LOG2E)
    lane = lax.broadcasted_iota(jnp.int32, q.shape, 1) % LANES
    lo = lane < HEAD_DIM
    q_up = pltpu.roll(q, ATTN_WIDTH - HEAD_DIM, axis=1)
    q_dn = pltpu.roll(q, HEAD_DIM, axis=1)
    zero = jnp.zeros_like(q)
    nat_lo = jnp.where(lo, q, zero)
    nat_hi = jnp.where(lo, zero, q)
    up_lo = jnp.where(lo, q_up, zero)
    dn_hi = jnp.where(lo, zero, q_dn)
    blocks = []
    for hd in range(N_Q_HEADS):
        pair = hd // 2
        sl = slice(pair * LANES, (pair + 1) * LANES)
        if hd < GQA_GROUP:
            blocks.append((nat_lo if hd % 2 == 0 else up_lo)[:, sl])
        else:
            blocks.append((dn_hi if hd % 2 == 0 else nat_hi)[:, sl])
    qp_ref[...] = jnp.concatenate(blocks, axis=1).astype(BF16)

    kv = _dot(hb, w_ref[:, _C_KV:_C_KV + 6 * KV_WIDTH])
    outs = []
    for br in range(3):
        k = kv[:, br * 2 * KV_WIDTH:br * 2 * KV_WIDTH + KV_WIDTH]
        v = kv[:, br * 2 * KV_WIDTH + KV_WIDTH:(br + 1) * 2 * KV_WIDTH]
        kms = _dot((k * k).astype(BF16), gk_ref[...])
        k = k * lax.rsqrt(kms + EPS) * kw_ref[:, br * KV_WIDTH:(br + 1) * KV_WIDTH]
        k = _rope(k, c, su, sd)
        outs += [k, v]
    kvrows = jnp.concatenate(outs[:4], axis=1)
    winrows = jnp.concatenate(outs[4:], axis=1)
    kvb_ref[...] = kvrows.astype(BF16)
    win_ref[...] = winrows
    winb_ref[...] = winrows.astype(BF16)
    if seq_layout:
        kvt_ref, cmpx_ref, stage_ref = rest
        tm = kvrows.shape[0]
        for r in range(4):
            kvt_ref[0, r] = jnp.transpose(kvrows[:, r * KV_WIDTH:(r + 1) * KV_WIDTH])
        for s in range(2):
            stage_ref[s] = kvrows[:, s * KV_WIDTH:(s + 1) * KV_WIDTH]
        _stride_block_store(stage_ref, cmpx_ref, tm)
    else:
        rest[0][...] = kvrows

    z_ref[...] = _dot(hb, w_ref[:, _C_Z:_C_Z + SSM_WIDTH])
    xbc_ref[...] = _dot(hb, w_ref[:, _C_XBC:_C_XBC + CONV_DIM])
    misc_ref[...] = _dot(hb, w_ref[:, _C_MISC:_C_MISC + LANES])


def inproj(x, mod3, mod_row0, norm_w, wp, q_norm_w, k_norm_w, pos, tm, seq_layout):
    b, t, d = x.shape
    n = b * t
    tiles_per_b = t // tm
    xf = x.reshape(n, d)
    c, su, sd = _rope_tables(pos)
    qw = jnp.tile(q_norm_w, N_Q_HEADS).reshape(1, ATTN_WIDTH)
    kw = jnp.concatenate([jnp.tile(k_norm_w[i], N_KV_HEADS) for i in range(3)]).reshape(1, 3 * KV_WIDTH)
    gq = _group_mean_matrix(ATTN_WIDTH)
    gk = _group_mean_matrix(KV_WIDTH)

    def mod_spec(col):
        return _mod_spec(mod3, col, tm, tiles_per_b, mod_row0)

    def tok(wd):
        return pl.BlockSpec((tm, wd), lambda i: (i, 0))

    def full(a):
        return pl.BlockSpec(a.shape, lambda i: (0,) * a.ndim)

    rope_spec = pl.BlockSpec((tm, LANES), lambda i: (i % tiles_per_b, 0))
    out_shape = [
        jax.ShapeDtypeStruct((n, N_Q_HEADS * LANES), BF16),
        jax.ShapeDtypeStruct((n, 4 * KV_WIDTH), BF16),
        jax.ShapeDtypeStruct((n, 2 * KV_WIDTH), F32),
        jax.ShapeDtypeStruct((n, 2 * KV_WIDTH), BF16),
        jax.ShapeDtypeStruct((n, SSM_WIDTH), F32),
        jax.ShapeDtypeStruct((n, CONV_DIM), F32),
        jax.ShapeDtypeStruct((n, LANES), F32),
    ]
    out_specs = [tok(s.shape[1]) for s in out_shape]
    scratch = []
    if seq_layout:
        out_shape += [jax.ShapeDtypeStruct((b, 4, KV_WIDTH, t), F32),
                      jax.ShapeDtypeStruct((b, t // CMP_STRIDE, CMP_STRIDE * 2 * KV_WIDTH), BF16)]
        out_specs += [pl.BlockSpec((1, 4, KV_WIDTH, tm), lambda i: (i // tiles_per_b, 0, 0, i % tiles_per_b)),
                      pl.BlockSpec((1, tm // CMP_STRIDE, CMP_STRIDE * 2 * KV_WIDTH),
                                   lambda i: (i // tiles_per_b, i % tiles_per_b, 0))]
        scratch = [pltpu.VMEM((2, tm, KV_WIDTH), F32)]
    else:
        out_shape += [jax.ShapeDtypeStruct((n, 4 * KV_WIDTH), F32)]
        out_specs += [tok(4 * KV_WIDTH)]
    return pl.pallas_call(
        functools.partial(_inproj_kernel, seq_layout=seq_layout),
        grid=(n // tm,),
        in_specs=[tok(d), mod_spec(0), mod_spec(1), full(norm_w), full(wp), full(qw), full(kw), full(gq), full(gk),
                  rope_spec, rope_spec, rope_spec],
        out_specs=tuple(out_specs),
        out_shape=tuple(out_shape),
        scratch_shapes=scratch,
        compiler_params=_cparams(("arbitrary",)),
        name="inproj",
    )(xf, mod3, mod3, norm_w, wp, qw, kw, gq, gk, c, su, sd)


def _prep_compress(cmp_pe, cmp_w1, cmp_w2):
    half = CMP_LEN // 2
    eye = jnp.eye(N_KV_HEADS, dtype=F32)
    w1 = cmp_w1.reshape(2, CMP_LEN, HEAD_DIM, CMP_HIDDEN)
    w1s = []
    for part in (w1[:, :half], w1[:, half:]):
        w1s.append(jnp.einsum("pjdo,hg->pjhdgo", part, eye).reshape(2, half * KV_WIDTH, N_KV_HEADS * CMP_HIDDEN))
    w1p = jnp.concatenate(w1s, axis=2).astype(BF16)
    pe = cmp_pe.reshape(2, 2, half, 1, HEAD_DIM)
    pep = jnp.broadcast_to(pe, (2, 2, half, N_KV_HEADS, HEAD_DIM)).reshape(2, 2, half * KV_WIDTH)
    w2p = jnp.einsum("poe,hg->phoge", cmp_w2, eye).reshape(2, N_KV_HEADS * CMP_HIDDEN, KV_WIDTH).astype(BF16)
    return w1p, pep, w2p


def _compress_kernel(x_ref, w1_ref, pe_ref, w2_ref, o_ref, *, row_w):
    part = pl.program_id(1)
    nb = x_ref.shape[1]
    half = CMP_LEN // 2
    hid = N_KV_HEADS * CMP_HIDDEN
    cols = []
    for j in range(half):
        a = x_ref[0, :, j * row_w:j * row_w + KV_WIDTH]
        b = x_ref[0, :, j * row_w + KV_WIDTH:j * row_w + 2 * KV_WIDTH]
        cols.append(jnp.where(part == 0, a, b))
    x = jnp.concatenate(cols, axis=1).astype(F32)
    pe = pe_ref[0]
    u = _dot((x + pe[0:1]).astype(BF16), w1_ref[0, :, :hid])
    v = _dot((x + pe[1:2]).astype(BF16), w1_ref[0, :, hid:])
    h1 = u + pltpu.roll(v, nb - 1, axis=0)
    out = _dot(_silu(h1).astype(BF16), w2_ref[0])
    row = lax.broadcasted_iota(jnp.int32, out.shape, 0)
    o_ref[0, 0] = jnp.where(row < nb - 1, out, 0.0).astype(o_ref.dtype)


def compress(x, w1p, pep, w2p):
    b, nb, width = x.shape
    row_w = width // CMP_STRIDE
    return pl.pallas_call(
        functools.partial(_compress_kernel, row_w=row_w),
        grid=(b, 2),
        in_specs=[
            pl.BlockSpec((1, nb, CMP_STRIDE * row_w), lambda i, p: (i, 0, 0)),
            pl.BlockSpec((1,) + w1p.shape[1:], lambda i, p: (p, 0, 0)),
            pl.BlockSpec((1,) + pep.shape[1:], lambda i, p: (p, 0, 0)),
            pl.BlockSpec((1,) + w2p.shape[1:], lambda i, p: (p, 0, 0)),
        ],
        out_specs=pl.BlockSpec((1, 1, nb, KV_WIDTH), lambda i, p: (i, p, 0, 0)),
        out_shape=jax.ShapeDtypeStruct((b, 2, nb, KV_WIDTH), BF16),
        compiler_params=_cparams(("arbitrary", "arbitrary")),
        name="compress",
    )(x, w1p, pep, w2p)


N_SEL_LANES = LANES


def _cover_matrix(nb):
    c = np.arange(nb)[:, None]
    j = np.arange(N_SEL_LANES)[None, :]
    start = c * CMP_STRIDE
    m = (start < (j + 1) * SEL_BLOCK) & (start + CMP_LEN > j * SEL_BLOCK)
    return jnp.asarray(m.astype(np.float32), BF16)


def _place_heads(res, kv):
    lane = lax.broadcasted_iota(jnp.int32, res[0].shape, 1)
    lo = lane < HEAD_DIM
    blocks = []
    for pair in range(GQA_GROUP // 2):
        a, b = res[2 * pair], res[2 * pair + 1]
        if kv == 0:
            blocks.append(jnp.where(lo, a, pltpu.roll(b, HEAD_DIM, axis=1)))
        else:
            blocks.append(jnp.where(lo, pltpu.roll(a, HEAD_DIM, axis=1), b))
    return jnp.concatenate(blocks, axis=1)


def _group_rows(q_ref, kv):
    heads = range(kv * GQA_GROUP, (kv + 1) * GQA_GROUP)
    return jnp.concatenate([q_ref[0, :, hd * LANES:(hd + 1) * LANES] for hd in heads], axis=0)


def _heads_from_transposed(out_t, tq, kv):
    out = jnp.transpose(out_t)
    return _place_heads([out[g * tq:(g + 1) * tq] for g in range(GQA_GROUP)], kv)


def _cmp_select_kernel(q_ref, kc_ref, vc_ref, covt_ref, o_ref, m_ref, *, q_off, n_pick):
    tq = q_ref.shape[1]
    rows = GQA_GROUP * tq
    nb = kc_ref.shape[2]
    wl = max(tq, LANES)
    assert tq % LANES == 0 or rows == LANES
    t0 = q_off + pl.program_id(1) * tq
    kc = kc_ref[0, 0]
    vc = vc_ref[0, 0]
    qpos = t0 + lax.broadcasted_iota(jnp.int32, (nb, rows), 1) % tq
    cend = lax.broadcasted_iota(jnp.int32, (nb, rows), 0) * CMP_STRIDE + (CMP_LEN - 1)
    valid = cend <= qpos
    blk = lax.broadcasted_iota(jnp.int32, (N_SEL_LANES, wl), 0)
    cur = (t0 + lax.broadcasted_iota(jnp.int32, (N_SEL_LANES, wl), 1) % tq) // SEL_BLOCK
    forced = (blk == 0) | ((blk <= cur) & (blk > cur - N_LOCAL))
    o_groups = []
    for kv in range(N_KV_HEADS):
        s = _dot_nt(kc, _group_rows(q_ref, kv))
        s = jnp.where(valid, s, NEG)
        e = jnp.exp2(s - jnp.max(s, axis=0, keepdims=True))
        p = e / jnp.sum(e, axis=0, keepdims=True)
        p = jnp.where(valid, p, 0.0)
        o_t = lax.dot_general(vc, p.astype(BF16), (((0,), (0,)), ((), ())), preferred_element_type=F32)
        o_groups.append(_heads_from_transposed(o_t, tq, kv))
        if tq % LANES == 0:
            psum = sum(p[:, g * tq:(g + 1) * tq] for g in range(GQA_GROUP))
        else:
            psum = p + sum(pltpu.roll(p, g * tq, axis=1) for g in range(1, GQA_GROUP))
        hi, lo = _split2(psum)
        imp = _dot(covt_ref[...], hi) + _dot(covt_ref[...], lo)
        x = jnp.where(forced, BIG, jnp.where(blk > cur, -BIG, imp))
        sel = jnp.zeros(x.shape, jnp.bool_)
        for _ in range(n_pick):
            mx = jnp.max(x, axis=0, keepdims=True)
            idx = jnp.min(jnp.where(x == mx, blk, N_SEL_LANES), axis=0, keepdims=True)
            hit = blk == idx
            sel = sel | hit
            x = jnp.where(hit, -jnp.inf, x)
        mneg = jnp.transpose(jnp.where(sel, 0.0, NEG))
        m_ref[0, kv] = mneg[:tq].astype(m_ref.dtype)
    o_ref[0] = jnp.concatenate(o_groups, axis=1)


def cmp_select(qp, kcv, q_off, n_pick, tq):
    b, t, _ = qp.shape
    nb = kcv.shape[2]
    cover = jnp.transpose(_cover_matrix(nb))
    return pl.pallas_call(
        functools.partial(_cmp_select_kernel, q_off=q_off, n_pick=n_pick),
        grid=(b, t // tq),
        in_specs=[
            pl.BlockSpec((1, tq, N_Q_HEADS * LANES), lambda i, j: (i, j, 0)),
            pl.BlockSpec((1, 1, nb, KV_WIDTH), lambda i, j: (i, 0, 0, 0)),
            pl.BlockSpec((1, 1, nb, KV_WIDTH), lambda i, j: (i, 1, 0, 0)),
            pl.BlockSpec((N_SEL_LANES, nb), lambda i, j: (0, 0)),
        ],
        out_specs=(
            pl.BlockSpec((1, tq, ATTN_WIDTH), lambda i, j: (i, j, 0)),
            pl.BlockSpec((1, N_KV_HEADS, tq, N_SEL_LANES), lambda i, j: (i, 0, j, 0)),
        ),
        out_shape=(
            jax.ShapeDtypeStruct((b, t, ATTN_WIDTH), F32),
            jax.ShapeDtypeStruct((b, N_KV_HEADS, t, N_SEL_LANES), BF16),
        ),
        compiler_params=_cparams(("arbitrary", "arbitrary")),
        name="cmp_select",
    )(qp, kcv, kcv, cover)


SEL_TILE_ELEMS = 512 * 512
WIN_CHUNK = 256


def _block_onehot(s):
    key = np.arange(s)[:, None]
    j = np.arange(N_SEL_LANES)[None, :]
    return jnp.asarray((key // SEL_BLOCK == j).astype(np.float32), BF16)


def _gate_expand():
    m = np.zeros((3, LANES, ATTN_WIDTH), np.float32)
    for br in range(3):
        for hd in range(N_Q_HEADS):
            m[br, SSM_HEADS + 3 * hd + br, hd * HEAD_DIM:(hd + 1) * HEAD_DIM] = 1.0
    return jnp.asarray(m, BF16)


def _flash_update(ss, v, m_ref, acc_ref):
    lane = lax.broadcasted_iota(jnp.int32, v.shape, 1)
    one = jnp.ones(v.shape, v.dtype)
    stage = []
    for k, s in enumerate(ss):
        m_old = m_ref[k]
        m_new = jnp.maximum(m_old, jnp.max(s, axis=0, keepdims=True))
        alpha = jnp.exp2(m_old - m_new)
        p = jnp.exp2(s - m_new)
        m_ref[k] = m_new
        stage.append((alpha, p.astype(BF16)))
    for k, (alpha, p) in enumerate(stage):
        vk = jnp.where((lane < HEAD_DIM) == (k == 0), v, one)
        pv = lax.dot_general(vk, p, (((0,), (0,)), ((), ())), preferred_element_type=F32)
        acc_ref[k] = alpha * acc_ref[k] + pv


def _sel_chunk(rows, n_keys):
    chunk = SEL_TILE_ELEMS // rows
    while n_keys % chunk:
        chunk //= 2
    return chunk


def _sel_win_kernel(q_ref, mneg_ref, ksel_ref, vsel_ref, et_ref, kwin_ref, vwin_ref, ocmp_ref, misc_ref, eg_ref,
                    o_ref, lhs_ref, m_ref, acc_ref, *, q_off, win_pos0):
    tq = q_ref.shape[1]
    rows = GQA_GROUP * tq
    SEL_CHUNK = _sel_chunk(rows, ksel_ref.shape[1])
    t0 = q_off + pl.program_id(1) * tq
    n_sel = lax.shift_right_logical(t0 + tq - 1, int(math.log2(SEL_CHUNK))) + 1
    w_lo = jnp.maximum(t0 - (WINDOW - 1) - win_pos0, 0) // WIN_CHUNK
    w_hi = (t0 + tq - 1 - win_pos0) // WIN_CHUNK + 1

    def qrow(n_keys):
        return lax.broadcasted_iota(jnp.int32, (n_keys, rows), 1) % tq + t0

    def init():
        m_ref[...] = jnp.full(m_ref.shape, NEG, F32)
        acc_ref[...] = jnp.zeros(acc_ref.shape, F32)

    def finish():
        outs = []
        for kv in range(N_KV_HEADS):
            acc = acc_ref[kv]
            denom_row = HEAD_DIM * (1 - kv)
            outs.append(_heads_from_transposed(acc / acc[denom_row:denom_row + 1, :], tq, kv))
        return jnp.concatenate(outs, axis=1)

    for kv in range(N_KV_HEADS):
        for g in range(GQA_GROUP):
            hd = kv * GQA_GROUP + g
            lhs_ref[kv, g * tq:(g + 1) * tq, :LANES] = q_ref[0, :, hd * LANES:(hd + 1) * LANES]
            lhs_ref[kv, g * tq:(g + 1) * tq, LANES:] = mneg_ref[0, kv]

    init()

    def sel_step(c, carry, causal):
        r0 = pl.multiple_of(c * SEL_CHUNK, SEL_CHUNK)
        rhs = jnp.concatenate([ksel_ref[0, pl.ds(r0, SEL_CHUNK), :], et_ref[pl.ds(r0, SEL_CHUNK), :]], axis=1)
        v = vsel_ref[0, pl.ds(r0, SEL_CHUNK), :]
        if causal:
            ok = r0 + lax.broadcasted_iota(jnp.int32, (SEL_CHUNK, rows), 0) <= qrow(SEL_CHUNK)
        ss = [_dot_nt(rhs, lhs_ref[kv]) for kv in range(N_KV_HEADS)]
        if causal:
            ss = [jnp.where(ok, s, NEG) for s in ss]
        _flash_update(ss, v, m_ref, acc_ref)
        return carry

    n_full = lax.shift_right_logical(t0 + 1, int(math.log2(SEL_CHUNK)))
    lax.fori_loop(0, n_full, functools.partial(sel_step, causal=False), 0)
    lax.fori_loop(n_full, n_sel, functools.partial(sel_step, causal=True), 0)
    o_sel = finish()

    init()

    def win_step(c, carry):
        r0 = pl.multiple_of(c * WIN_CHUNK, WIN_CHUNK)
        k = kwin_ref[0, pl.ds(r0, WIN_CHUNK), :]
        v = vwin_ref[0, pl.ds(r0, WIN_CHUNK), :]
        wpos = win_pos0 + r0 + lax.broadcasted_iota(jnp.int32, (WIN_CHUNK, rows), 0)
        qr = qrow(WIN_CHUNK)
        ok = (wpos <= qr) & (wpos > qr - WINDOW)
        ss = [jnp.where(ok, _dot_nt(k, lhs_ref[kv, :, :LANES]), NEG) for kv in range(N_KV_HEADS)]
        _flash_update(ss, v, m_ref, acc_ref)
        return carry

    lax.fori_loop(w_lo, w_hi, win_step, 0)
    o_win = finish()

    gates = jax.nn.sigmoid(misc_ref[0])
    ghi = gates.astype(BF16)
    glo = (gates - ghi.astype(F32)).astype(BF16)
    branches = (ocmp_ref[0], o_sel, o_win)
    out = jnp.zeros(branches[0].shape, F32)
    for br in range(3):
        out = out + (_dot(ghi, eg_ref[br]) + _dot(glo, eg_ref[br])) * branches[br]
    o_ref[0] = out


def sel_win_attention(qp, mneg, kvb, sel_col, winb, o_cmp, misc, q_off, win_pos0, tq):
    b, t, _ = qp.shape
    s = kvb.shape[1]
    sw = winb.shape[1]
    et = _block_onehot(s)
    eg = _gate_expand()
    rows = GQA_GROUP * tq
    assert q_off + t <= s and q_off + t - win_pos0 <= sw and sw % WIN_CHUNK == 0
    return pl.pallas_call(
        functools.partial(_sel_win_kernel, q_off=q_off, win_pos0=win_pos0),
        grid=(b, t // tq),
        in_specs=[
            pl.BlockSpec((1, tq, N_Q_HEADS * LANES), lambda i, j: (i, j, 0)),
            pl.BlockSpec((1, N_KV_HEADS, tq, N_SEL_LANES), lambda i, j: (i, 0, j, 0)),
            pl.BlockSpec((1, s, KV_WIDTH), lambda i, j: (i, 0, sel_col)),
            pl.BlockSpec((1, s, KV_WIDTH), lambda i, j: (i, 0, sel_col + 1)),
            pl.BlockSpec((s, N_SEL_LANES), lambda i, j: (0, 0)),
            pl.BlockSpec((1, sw, KV_WIDTH), lambda i, j: (i, 0, 0)),
            pl.BlockSpec((1, sw, KV_WIDTH), lambda i, j: (i, 0, 1)),
            pl.BlockSpec((1, tq, ATTN_WIDTH), lambda i, j: (i, j, 0)),
            pl.BlockSpec((1, tq, LANES), lambda i, j: (i, j, 0)),
            pl.BlockSpec((3, LANES, ATTN_WIDTH), lambda i, j: (0, 0, 0)),
        ],
        out_specs=pl.BlockSpec((1, tq, ATTN_WIDTH), lambda i, j: (i, j, 0)),
        out_shape=jax.ShapeDtypeStruct((b, t, ATTN_WIDTH), F32),
        scratch_shapes=[
            pltpu.VMEM((N_KV_HEADS, rows, 2 * LANES), BF16),
            pltpu.VMEM((N_KV_HEADS, 1, rows), F32),
            pltpu.VMEM((N_KV_HEADS, LANES, rows), F32),
        ],
        compiler_params=_cparams(("arbitrary", "arbitrary")),
        name="sel_win_attention",
    )(qp, mneg, kvb, kvb, et, winb, winb, o_cmp, misc, eg)


CONV_PAD = 8
HEAD_PAIRS = SSM_HEADS // 2


def _split3(x):
    a = x.astype(BF16)
    r = x - a.astype(F32)
    b = r.astype(BF16)
    c = (r - b.astype(F32)).astype(BF16)
    return a, b, c


def _ssd_kernel(xbc_ref, z_ref, misc_ref, conv0_ref, h0_ref, cw_ref, cb_ref, dtb_ref, a_ref, dsk_ref, nw_ref,
                y_ref, hout_ref, cout_ref, xp_ref, h_ref, ms_ref, *, t_valid):
    ch = pl.program_id(1)
    L = SSD_CHUNK
    keep = CONV_WIDTH - 1

    @pl.when(ch == 0)
    def _():
        xp_ref[...] = jnp.zeros(xp_ref.shape, F32)
        xp_ref[CONV_PAD - keep:CONV_PAD, :] = conv0_ref[0]
        h_ref[...] = h0_ref[0]

    xp_ref[CONV_PAD:CONV_PAD + t_valid, :] = xbc_ref[0]
    conv = cb_ref[...]
    for j in range(CONV_WIDTH):
        conv = conv + cw_ref[j:j + 1, :] * xp_ref[CONV_PAD - keep + j:CONV_PAD - keep + j + L, :]
    last = xp_ref[CONV_PAD + t_valid - keep:CONV_PAD + t_valid, :]
    cout_ref[0] = last
    xp_ref[CONV_PAD - keep:CONV_PAD, :] = last
    xc = _silu(conv)

    row = lax.broadcasted_iota(jnp.int32, (L, LANES), 0)
    lane = lax.broadcasted_iota(jnp.int32, (L, LANES), 1)
    if t_valid == L:
        raw = misc_ref[0]
    else:
        ms_ref[...] = jnp.zeros(ms_ref.shape, F32)
        ms_ref[0:t_valid, :] = misc_ref[0]
        raw = ms_ref[...]
    v = raw + dtb_ref[...]
    dt = jnp.maximum(v, 0.0) + jnp.log(1.0 + jnp.exp(-jnp.abs(v)))
    dt = jnp.where((lane < SSM_HEADS) & (row < t_valid), dt, 0.0)
    da = dt * a_ref[...]
    tri = (lax.broadcasted_iota(jnp.int32, (L, L), 1) <= lax.broadcasted_iota(jnp.int32, (L, L), 0))
    trib = tri.astype(BF16)
    acum = sum(_dot(trib, part) for part in _split3(da))
    acum_t = jnp.transpose(acum)
    dt_t = jnp.transpose(dt)
    e_acum = jnp.exp(acum)
    e_last = jnp.exp(acum[L - 1:L, :])
    w_end = jnp.exp(acum[L - 1:L, :] - acum) * dt
    lo = lane < SSM_HEAD_DIM

    ys = []
    for pair in range(HEAD_PAIRS):
        grp = (2 * pair) // (SSM_HEADS // SSM_GROUPS)
        bg = xc[:, SSM_WIDTH + grp * SSM_STATE:SSM_WIDTH + (grp + 1) * SSM_STATE].astype(BF16)
        cg = xc[:, SSM_WIDTH + (SSM_GROUPS + grp) * SSM_STATE:SSM_WIDTH + (SSM_GROUPS + grp + 1) * SSM_STATE].astype(BF16)
        g = _dot_nt(cg, bg)
        xpair = xc[:, pair * LANES:(pair + 1) * LANES]
        y = jnp.zeros((L, LANES), F32)
        for sub in range(2):
            hd = 2 * pair + sub
            seg = acum[:, hd:hd + 1] - acum_t[hd:hd + 1, :]
            m = g * jnp.exp(jnp.where(tri, seg, NEG)) * dt_t[hd:hd + 1, :]
            xm = jnp.where(lo if sub == 0 else ~lo, xpair, 0.0)
            y = y + _dot(m.astype(BF16), xm.astype(BF16))
        col = lambda a: jnp.where(lo, a[:, 2 * pair:2 * pair + 1], a[:, 2 * pair + 1:2 * pair + 2])
        hp = h_ref[pair]
        y = y + _dot_nt(cg, hp.astype(BF16)) * col(e_acum)
        y = y + col(dsk_ref[...]) * xpair
        xw = (xpair * col(w_end)).astype(BF16)
        st = lax.dot_general(xw, bg, (((0,), (0,)), ((), ())), preferred_element_type=F32)
        prow = lax.broadcasted_iota(jnp.int32, (LANES, LANES), 0) < SSM_HEAD_DIM
        dec = jnp.where(prow, e_last[:, 2 * pair:2 * pair + 1], e_last[:, 2 * pair + 1:2 * pair + 2])
        h_ref[pair] = hp * dec + st
        ys.append(y)
    y = jnp.concatenate(ys, axis=1)
    if t_valid != L:
        y = y[:t_valid]
    y = y * _silu(z_ref[0])
    y = y * lax.rsqrt(jnp.mean(y * y, axis=-1, keepdims=True) + EPS) * nw_ref[...]
    y_ref[0] = y

    @pl.when(ch == pl.num_programs(1) - 1)
    def _():
        hout_ref[0] = h_ref[...]


def ssd(xbc, z, misc, conv0, h0, conv_w, conv_b, dt_bias, a_log, d_skip, norm_w):
    b, t, _ = xbc.shape
    L = SSD_CHUNK
    t_valid = L if t % L == 0 else t
    assert t_valid == L or t < L
    n_ch = max(t // L, 1)
    keep = CONV_WIDTH - 1
    pad8 = lambda v: jnp.pad(v.astype(F32), (0, LANES - SSM_HEADS)).reshape(1, LANES)
    dtb = pad8(dt_bias)
    a = pad8(-jnp.exp(a_log.astype(F32)))
    dsk = pad8(d_skip)
    h0p = h0.reshape(b, HEAD_PAIRS, 2 * SSM_HEAD_DIM, SSM_STATE)
    full = lambda arr: pl.BlockSpec(arr.shape, lambda i, c: (0,) * arr.ndim)
    tok = lambda wd: pl.BlockSpec((1, t_valid, wd), lambda i, c: (i, c, 0))
    y, hout, cout = pl.pallas_call(
        functools.partial(_ssd_kernel, t_valid=t_valid),
        grid=(b, n_ch),
        in_specs=[
            tok(CONV_DIM), tok(SSM_WIDTH), tok(LANES),
            pl.BlockSpec((1, keep, CONV_DIM), lambda i, c: (i, 0, 0)),
            pl.BlockSpec((1, HEAD_PAIRS, 2 * SSM_HEAD_DIM, SSM_STATE), lambda i, c: (i, 0, 0, 0)),
            full(conv_w), pl.BlockSpec((1, CONV_DIM), lambda i, c: (0, 0)),
            full(dtb), full(a), full(dsk), pl.BlockSpec((1, SSM_WIDTH), lambda i, c: (0, 0)),
        ],
        out_specs=(
            tok(SSM_WIDTH),
            pl.BlockSpec((1, HEAD_PAIRS, 2 * SSM_HEAD_DIM, SSM_STATE), lambda i, c: (i, 0, 0, 0)),
            pl.BlockSpec((1, keep, CONV_DIM), lambda i, c: (i, 0, 0)),
        ),
        out_shape=(
            jax.ShapeDtypeStruct((b, t, SSM_WIDTH), F32),
            jax.ShapeDtypeStruct((b, HEAD_PAIRS, 2 * SSM_HEAD_DIM, SSM_STATE), F32),
            jax.ShapeDtypeStruct((b, keep, CONV_DIM), F32),
        ),
        scratch_shapes=[
            pltpu.VMEM((CONV_PAD + L, CONV_DIM), F32),
            pltpu.VMEM((HEAD_PAIRS, 2 * SSM_HEAD_DIM, SSM_STATE), F32),
            pltpu.VMEM((L, LANES), F32),
        ],
        compiler_params=_cparams(("arbitrary", "arbitrary")),
        name="ssd",
    )(xbc, z, misc, conv0, h0p, conv_w, conv_b.reshape(1, CONV_DIM), dtb, a, dsk, norm_w.reshape(1, SSM_WIDTH))
    return y, hout.reshape(b, SSM_HEADS, SSM_HEAD_DIM, SSM_STATE), cout


def _split2(x):
    hi = x.astype(BF16)
    return hi, (x - hi.astype(F32)).astype(BF16)


def _merge_kernel(oa_ref, ys_ref, x_ref, g1_ref, sh2_ref, sc2_ref, anw_ref, wo_ref, n2w_ref, wrh_ref, wrl_ref,
                  x1_ref, h2_ref, lg_ref):
    oa = oa_ref[...]
    a = oa * lax.rsqrt(jnp.mean(oa * oa, axis=-1, keepdims=True) + EPS) * anw_ref[...]
    cat = jnp.concatenate([a.astype(BF16), ys_ref[...].astype(BF16)], axis=1)
    x1 = x_ref[...] + _mod(g1_ref) * _dot(cat, wo_ref[...])
    x1_ref[...] = x1
    h2 = x1 * lax.rsqrt(jnp.mean(x1 * x1, axis=-1, keepdims=True) + EPS) * n2w_ref[...]
    h2 = h2 * (1.0 + _mod(sc2_ref)) + _mod(sh2_ref)
    h2_ref[...] = h2.astype(BF16)
    hh, hl = _split2(h2)
    lg_ref[...] = _dot_nt(wrh_ref[...], hh) + _dot_nt(wrh_ref[...], hl) + _dot_nt(wrl_ref[...], hh)


def merge(o_attn, y_ssm, x, mod3, mod_row0, attn_norm_w, wo, norm2_w, w_router, tm):
    b, t, d = x.shape
    n = b * t
    tiles_per_b = t // tm
    wrt = jnp.transpose(w_router)
    wrh, wrl = _split2(wrt)

    def mod_spec(col):
        return _mod_spec(mod3, col, tm, tiles_per_b, mod_row0)

    tok = lambda wd: pl.BlockSpec((tm, wd), lambda i: (i, 0))
    full = lambda a: pl.BlockSpec(a.shape, lambda i: (0,) * a.ndim)
    return pl.pallas_call(
        _merge_kernel,
        grid=(n // tm,),
        in_specs=[tok(ATTN_WIDTH), tok(SSM_WIDTH), tok(d), mod_spec(2), mod_spec(3), mod_spec(4),
                  full(attn_norm_w), full(wo), full(norm2_w), full(wrh), full(wrl)],
        out_specs=(tok(d), tok(d), pl.BlockSpec((N_EXPERTS, tm), lambda i: (0, i))),
        out_shape=(jax.ShapeDtypeStruct((n, d), F32), jax.ShapeDtypeStruct((n, d), BF16),
                   jax.ShapeDtypeStruct((N_EXPERTS, n), F32)),
        compiler_params=_cparams(("arbitrary",)),
        name="merge",
    )(o_attn.reshape(n, ATTN_WIDTH), y_ssm.reshape(n, SSM_WIDTH), x.reshape(n, d), mod3, mod3, mod3,
      attn_norm_w, wo, norm2_w, wrh, wrl)


EXPERTS_PER_GROUP = N_EXPERTS // N_EXPERT_GROUPS


def _first_max(x, ids, axes, n_ids):
    mx = jnp.max(x, axis=axes, keepdims=True)
    return ids == jnp.min(jnp.where(x == mx, ids, n_ids), axis=axes, keepdims=True), mx


def _route_kernel(lg_ref, eb_ref, tri_ref, w_ref, pos_ref, cnt_ref):
    lg = lg_ref[...]
    tn = lg.shape[2]
    scores = jax.nn.sigmoid(lg)
    biased = scores + eb_ref[...]
    sub = lax.broadcasted_iota(jnp.int32, lg.shape, 1)
    grp = lax.broadcasted_iota(jnp.int32, (N_EXPERT_GROUPS, 1, tn), 0)
    eid = lax.broadcasted_iota(jnp.int32, lg.shape, 0) * EXPERTS_PER_GROUP + sub
    hit, m1 = _first_max(biased, sub, 1, EXPERTS_PER_GROUP)
    m2 = jnp.max(jnp.where(hit, -jnp.inf, biased), axis=1, keepdims=True)
    gs = m1 + m2
    keep = jnp.zeros(gs.shape, jnp.bool_)
    for _ in range(TOPK_GROUPS):
        hit, _m = _first_max(gs, grp, 0, N_EXPERT_GROUPS)
        keep = keep | hit
        gs = jnp.where(hit, -jnp.inf, gs)
    x = jnp.where(keep, biased, NEG)
    sel = jnp.zeros(lg.shape, jnp.bool_)
    for _ in range(TOP_K):
        hit, _m = _first_max(x, eid, (0, 1), N_EXPERTS)
        sel = sel | hit
        x = jnp.where(hit, -jnp.inf, x)
    w = jnp.where(sel, scores, 0.0)
    w = w / jnp.sum(w, axis=(0, 1), keepdims=True) * ROUTED_SCALE
    w_ref[...] = w
    selb = sel.astype(BF16).reshape(N_EXPERTS, tn)
    pos = _dot(selb, tri_ref[...])
    pos_ref[...] = jnp.where(sel, pos.reshape(lg.shape), -1.0)
    cnt = jnp.sum(sel.astype(F32), axis=2, keepdims=True)
    cnt_ref[0] = jnp.broadcast_to(cnt, cnt_ref.shape[1:]).astype(jnp.int32)


def route(logits_t, e_bias, tn):
    n = logits_t.shape[1]
    lg3 = logits_t.reshape(N_EXPERT_GROUPS, EXPERTS_PER_GROUP, n)
    eb = e_bias.astype(F32).reshape(N_EXPERT_GROUPS, EXPERTS_PER_GROUP, 1)
    tri = jnp.asarray(np.triu(np.ones((tn, tn), np.float32), 1), BF16)
    blk = pl.BlockSpec((N_EXPERT_GROUPS, EXPERTS_PER_GROUP, tn), lambda i: (0, 0, i))
    w, pos, cnt = pl.pallas_call(
        _route_kernel,
        grid=(n // tn,),
        in_specs=[blk, pl.BlockSpec(eb.shape, lambda i: (0, 0, 0)), pl.BlockSpec((tn, tn), lambda i: (0, 0))],
        out_specs=(blk, blk, pl.BlockSpec((1, N_EXPERT_GROUPS, EXPERTS_PER_GROUP, LANES), lambda i: (i, 0, 0, 0))),
        out_shape=(jax.ShapeDtypeStruct(lg3.shape, F32), jax.ShapeDtypeStruct(lg3.shape, F32),
                   jax.ShapeDtypeStruct((n // tn, N_EXPERT_GROUPS, EXPERTS_PER_GROUP, LANES), jnp.int32)),
        compiler_params=_cparams(("arbitrary",)),
        name="route",
    )(lg3, eb, tri)
    return w.reshape(N_EXPERTS, n), pos.reshape(N_EXPERTS, n), cnt[..., 0].reshape(n // tn, N_EXPERTS)


MOE_ROWS = 128


def _swiglu(xb, wgu, wd, width):
    gu = _dot(xb, wgu)
    act = _silu(gu[:, :width]) * gu[:, width:]
    return _dot(act.astype(BF16), wd)


MOE_EXPERTS_PER_STEP = 4


MOE_ALIGN = 16
MOE_GATHER_ROWS = 896


def _moe_slots(tm):
    worst = TOP_K * tm + N_EXPERTS * (MOE_ALIGN - 1) + MOE_ROWS
    return -(-worst // MOE_GATHER_ROWS) * MOE_GATHER_ROWS


def _moe_kernel(cnt_ref, start_ref, h2_ref, w_ref, pos_ref, x1_ref, g2_ref, wgu_ref, wd_ref, sgu_ref, sd_ref,
                o_ref, g_all, xs):
    i = pl.program_id(0)
    es = pl.program_id(1)
    tm = h2_ref.shape[0]
    slots = g_all.shape[0]
    slot = lax.broadcasted_iota(jnp.int32, (MOE_ROWS, tm), 0).astype(F32)
    row = lax.broadcasted_iota(jnp.int32, (MOE_ROWS, 1), 0)

    def n_windows(cnt):
        return (cnt + MOE_ROWS - 1) // MOE_ROWS

    def window_start(e, j):
        return pl.multiple_of(start_ref[i * N_EXPERTS + e] + j * MOE_ROWS, MOE_ALIGN)

    @pl.when(es == 0)
    def _():
        g_all[...] = jnp.zeros(g_all.shape, BF16)

        def mark(e, carry):
            pos = pos_ref[pl.ds(e, 1), :]

            def mark_window(j, carry):
                hit = pos == slot + (j * MOE_ROWS).astype(F32)
                g_all[pl.ds(window_start(e, j), MOE_ROWS), :] = hit.astype(BF16)
                return carry

            return lax.fori_loop(0, n_windows(cnt_ref[i * N_EXPERTS + e]), mark_window, carry)

        lax.fori_loop(0, N_EXPERTS, mark, 0)

        def gather(c, carry):
            r0 = pl.multiple_of(c * MOE_GATHER_ROWS, MOE_GATHER_ROWS)
            rows = _dot(g_all[pl.ds(r0, MOE_GATHER_ROWS), :], h2_ref[...])
            xs[pl.ds(r0, MOE_GATHER_ROWS), :] = rows.astype(BF16)
            return carry

        lax.fori_loop(0, slots // MOE_GATHER_ROWS, gather, 0)

    for q in range(MOE_EXPERTS_PER_STEP):
        e = es * MOE_EXPERTS_PER_STEP + q
        cnt = cnt_ref[i * N_EXPERTS + e]
        wrow = w_ref[pl.ds(e, 1), :]

        def window(j, carry, q=q, e=e, cnt=cnt, wrow=wrow):
            r0 = window_start(e, j)
            xg = xs[pl.ds(r0, MOE_ROWS), :]
            out = _swiglu(xg, wgu_ref[q], wd_ref[q], D_EXPERT)
            g = g_all[pl.ds(r0, MOE_ROWS), :].astype(F32)
            out = out * jnp.sum(g * wrow, axis=1, keepdims=True)
            mine = row < cnt - j * MOE_ROWS
            xs[pl.ds(r0, MOE_ROWS), :] = jnp.where(mine, out.astype(BF16), xg)
            return carry

        lax.fori_loop(0, n_windows(cnt), window, 0)

    @pl.when(es == pl.num_programs(1) - 1)
    def _():
        y = lax.dot_general(g_all[...], xs[...], (((0,), (0,)), ((), ())), preferred_element_type=F32)
        y = y + _swiglu(h2_ref[...], sgu_ref[...], sd_ref[...], D_SHARED)
        o_ref[...] = x1_ref[...] + _mod(g2_ref) * y


def moe(h2, w_t, pos_t, counts, x1, mod3, mod_row0, t_per_b, wgu, wd, sgu, sd, tm):
    n, d = h2.shape
    tiles_per_b = t_per_b // tm
    eps = MOE_EXPERTS_PER_STEP
    slots = _moe_slots(tm)
    padded = (counts + MOE_ALIGN - 1) // MOE_ALIGN * MOE_ALIGN
    starts = jnp.cumsum(padded, axis=1) - padded
    grid_spec = pltpu.PrefetchScalarGridSpec(
        num_scalar_prefetch=2,
        grid=(n // tm, N_EXPERTS // eps),
        in_specs=[
            pl.BlockSpec((tm, d), lambda i, e, *_: (i, 0)),
            pl.BlockSpec((N_EXPERTS, tm), lambda i, e, *_: (0, i)),
            pl.BlockSpec((N_EXPERTS, tm), lambda i, e, *_: (0, i)),
            pl.BlockSpec((tm, d), lambda i, e, *_: (i, 0)),
            _mod_spec(mod3, 5, tm, tiles_per_b, mod_row0),
            pl.BlockSpec((eps, d, 2 * D_EXPERT), lambda i, e, *_: (e, 0, 0)),
            pl.BlockSpec((eps, D_EXPERT, d), lambda i, e, *_: (e, 0, 0)),
            pl.BlockSpec(sgu.shape, lambda i, e, *_: (0, 0)),
            pl.BlockSpec(sd.shape, lambda i, e, *_: (0, 0)),
        ],
        out_specs=pl.BlockSpec((tm, d), lambda i, e, *_: (i, 0)),
        scratch_shapes=[pltpu.VMEM((slots, tm), BF16), pltpu.VMEM((slots, d), BF16)],
    )
    return pl.pallas_call(
        _moe_kernel,
        grid_spec=grid_spec,
        out_shape=jax.ShapeDtypeStruct((n, d), F32),
        compiler_params=_cparams(("arbitrary", "arbitrary")),
        name="moe",
    )(counts.reshape(-1), starts.reshape(-1).astype(jnp.int32), h2, w_t, pos_t, x1, mod3, wgu, wd, sgu, sd)


SC_WINDOW = 128
PACK_W = 256
MOE_BLOCK_ROWS = 512
HI_MASK = -65536


def _pack_pair(x):
    bits = pltpu.bitcast(x.astype(BF16).astype(F32), jnp.int32)
    return lax.shift_right_logical(bits[:, :PACK_W], 16) | (bits[:, PACK_W:] & HI_MASK)


def _unpack_pair(word):
    lo = pltpu.bitcast(lax.shift_left(word, 16), F32)
    hi = pltpu.bitcast(word & HI_MASK, F32)
    return jnp.concatenate([lo, hi], axis=1)


def _pack_kernel(x_ref, a_ref, b_ref):
    x = x_ref[...]
    a_ref[...] = _pack_pair(x[:, :2 * PACK_W])
    b_ref[...] = _pack_pair(x[:, 2 * PACK_W:])


def pack_rows(x, tm):
    n, d = x.shape
    tok = lambda wd: pl.BlockSpec((tm, wd), lambda i: (i, 0))
    return pl.pallas_call(
        _pack_kernel, grid=(n // tm,), in_specs=[tok(d)], out_specs=(tok(PACK_W), tok(PACK_W)),
        out_shape=(jax.ShapeDtypeStruct((n, PACK_W), jnp.int32),) * 2,
        compiler_params=_cparams(("arbitrary",)), name="pack_rows",
    )(x)


def _slots_kernel(w_ref, pos_ref, base_ref, tri_ref, slot_ref, wt_ref):
    w = w_ref[...]
    pos = pos_ref[...]
    sel = pos >= 0.0
    rank = _dot(tri_ref[...], sel.astype(BF16))
    dest = base_ref[0] + pos
    slots, wts = [], []
    for j in range(TOP_K):
        mine = sel & (rank == float(j))
        slots.append(jnp.sum(jnp.where(mine, dest, 0.0), axis=0, keepdims=True))
        wts.append(jnp.sum(jnp.where(mine, w, 0.0), axis=0, keepdims=True))
    slot_ref[...] = jnp.concatenate(slots, axis=0).astype(jnp.int32)
    wpad = jnp.concatenate(wts + [jnp.zeros((LANES - TOP_K, w.shape[1]), F32)], axis=0)
    wt_ref[...] = jnp.transpose(wpad)


def slots_of(w_t, pos_t, base, tn):
    n = w_t.shape[1]
    tri = jnp.asarray(np.tril(np.ones((N_EXPERTS, N_EXPERTS), np.float32), -1), BF16)
    blk = pl.BlockSpec((N_EXPERTS, tn), lambda i: (0, i))
    return pl.pallas_call(
        _slots_kernel, grid=(n // tn,),
        in_specs=[blk, blk, pl.BlockSpec((1, N_EXPERTS, 1), lambda i: (i, 0, 0)),
                  pl.BlockSpec((N_EXPERTS, N_EXPERTS), lambda i: (0, 0))],
        out_specs=(pl.BlockSpec((TOP_K, tn), lambda i: (0, i)), pl.BlockSpec((tn, LANES), lambda i: (i, 0))),
        out_shape=(jax.ShapeDtypeStruct((TOP_K, n), jnp.int32), jax.ShapeDtypeStruct((n, LANES), F32)),
        compiler_params=_cparams(("arbitrary",)), name="moe_slots",
    )(w_t, pos_t, base, tri)


def sc_scatter_rows(rows, idx, n_out):
    n, d = rows.shape
    m = idx.shape[0]
    nb = n // SC_WINDOW
    mesh = plsc.VectorSubcoreMesh(core_axis_name="core", subcore_axis_name="subcore")

    @functools.partial(pl.kernel, out_type=jax.ShapeDtypeStruct((n_out, d), rows.dtype), mesh=mesh)
    def scatter(x_hbm, i_hbm, o_hbm):
        def body(x_vmem, i_vmem):
            pltpu.sync_copy(x_vmem, o_hbm.at[i_vmem.at[0]])

        pltpu.emit_pipeline(
            body, grid=(m // SC_WINDOW,),
            in_specs=[pl.BlockSpec((SC_WINDOW, d), index_map=lambda i: (i % nb, 0)),
                      pl.BlockSpec((1, SC_WINDOW), index_map=lambda i: (0, i))],
            out_specs=[], core_axis_name=("core", "subcore"), dimension_semantics=(pltpu.PARALLEL,),
        )(x_hbm, i_hbm)

    return scatter(rows, idx.reshape(1, m))


def sc_gather_rows(table, idx):
    d = table.shape[1]
    m = idx.shape[0]
    mesh = plsc.VectorSubcoreMesh(core_axis_name="core", subcore_axis_name="subcore")

    @functools.partial(pl.kernel, out_type=jax.ShapeDtypeStruct((m, d), table.dtype), mesh=mesh)
    def gather(x_hbm, i_hbm, o_hbm):
        def body(i_vmem, o_vmem):
            pltpu.sync_copy(x_hbm.at[i_vmem.at[0]], o_vmem)

        pltpu.emit_pipeline(
            body, grid=(m // SC_WINDOW,),
            in_specs=[pl.BlockSpec((1, SC_WINDOW), index_map=lambda i: (0, i))],
            out_specs=[pl.BlockSpec((SC_WINDOW, d), index_map=lambda i: (i, 0))],
            core_axis_name=("core", "subcore"), dimension_semantics=(pltpu.PARALLEL,),
        )(i_hbm, o_hbm)

    return gather(table, idx.reshape(1, m))


def _experts_kernel(be_ref, nu_ref, xa_ref, xb_ref, wgu_ref, wd_ref, oa_ref, ob_ref):
    @pl.when(pl.program_id(0) < nu_ref[0])
    def _():
        x = jnp.concatenate([_unpack_pair(xa_ref[...]), _unpack_pair(xb_ref[...])], axis=1).astype(BF16)
        out = _swiglu(x, wgu_ref[0], wd_ref[0], D_EXPERT)
        oa_ref[...] = _pack_pair(out[:, :2 * PACK_W])
        ob_ref[...] = _pack_pair(out[:, 2 * PACK_W:])


def experts_sorted(xa, xb, block_expert, n_used, wgu, wd):
    r = xa.shape[0]
    d = wd.shape[2]
    row = lambda b, be, nu: (jnp.minimum(b, nu[0] - 1), 0)
    blk = pl.BlockSpec((MOE_BLOCK_ROWS, PACK_W), row)
    grid_spec = pltpu.PrefetchScalarGridSpec(
        num_scalar_prefetch=2, grid=(r // MOE_BLOCK_ROWS,),
        in_specs=[blk, blk,
                  pl.BlockSpec((1, d, 2 * D_EXPERT), lambda b, be, nu: (be[b], 0, 0)),
                  pl.BlockSpec((1, D_EXPERT, d), lambda b, be, nu: (be[b], 0, 0))],
        out_specs=(blk, blk),
    )
    return pl.pallas_call(
        _experts_kernel, grid_spec=grid_spec,
        out_shape=(jax.ShapeDtypeStruct((r, PACK_W), jnp.int32),) * 2,
        compiler_params=_cparams(("arbitrary",)), name="moe_experts",
    )(block_expert, n_used, xa, xb, wgu, wd)


def _combine_kernel(ya_ref, yb_ref, wt_ref, h2_ref, x1_ref, g2_ref, sgu_ref, sd_ref, o_ref):
    wt = wt_ref[...]
    acc = _swiglu(h2_ref[...], sgu_ref[...], sd_ref[...], D_SHARED)
    for j in range(TOP_K):
        y = jnp.concatenate([_unpack_pair(ya_ref[j]), _unpack_pair(yb_ref[j])], axis=1)
        acc = acc + wt[:, j:j + 1] * y
    o_ref[...] = x1_ref[...] + _mod(g2_ref) * acc


def combine_sorted(ya, yb, wt, h2, x1, mod3, mod_row0, t_per_b, sgu, sd, tm):
    n, d = h2.shape
    tiles_per_b = t_per_b // tm
    tok = lambda wd: pl.BlockSpec((tm, wd), lambda i: (i, 0))
    yblk = pl.BlockSpec((TOP_K, tm, PACK_W), lambda i: (0, i, 0))
    full = lambda a: pl.BlockSpec(a.shape, lambda i: (0,) * a.ndim)
    return pl.pallas_call(
        _combine_kernel, grid=(n // tm,),
        in_specs=[yblk, yblk, tok(LANES), tok(d), tok(d), _mod_spec(mod3, 5, tm, tiles_per_b, mod_row0),
                  full(sgu), full(sd)],
        out_specs=tok(d), out_shape=jax.ShapeDtypeStruct((n, d), F32),
        compiler_params=_cparams(("arbitrary",)), name="moe_combine",
    )(ya, yb, wt, h2, x1, mod3, sgu, sd)


def moe_sorted(h2, w_t, pos_t, counts, x1, mod3, mod_row0, t_per_b, wgu, wd, sgu, sd, tm):
    n, d = h2.shape
    assert d == 4 * PACK_W and n % SC_WINDOW == 0
    n_blocks = (TOP_K * n + N_EXPERTS * (MOE_BLOCK_ROWS - 1)) // MOE_BLOCK_ROWS
    total = jnp.sum(counts, axis=0)
    region = (total + MOE_BLOCK_ROWS - 1) // MOE_BLOCK_ROWS * MOE_BLOCK_ROWS
    region_end = jnp.cumsum(region)
    base = (region_end - region)[None, :] + jnp.cumsum(counts, axis=0) - counts
    block_row0 = jnp.arange(n_blocks, dtype=region_end.dtype) * MOE_BLOCK_ROWS
    block_expert = jnp.sum(region_end[None, :] <= block_row0[:, None], axis=1)
    block_expert = jnp.minimum(block_expert, N_EXPERTS - 1).astype(jnp.int32)
    n_used = (region_end[-1:] // MOE_BLOCK_ROWS).astype(jnp.int32)
    slot, wt = slots_of(w_t, pos_t, base.astype(F32).reshape(-1, N_EXPERTS, 1), tm)
    dest = slot.reshape(-1)
    ha, hb = pack_rows(h2, tm)
    rows = n_blocks * MOE_BLOCK_ROWS
    xa, xb = sc_scatter_rows(ha, dest, rows), sc_scatter_rows(hb, dest, rows)
    oa, ob = experts_sorted(xa, xb, block_expert, n_used, wgu, wd)
    ya = sc_gather_rows(oa, dest).reshape(TOP_K, n, PACK_W)
    yb = sc_gather_rows(ob, dest).reshape(TOP_K, n, PACK_W)
    return combine_sorted(ya, yb, wt, h2, x1, mod3, mod_row0, t_per_b, sgu, sd, tm)


GATHER_PAGES = 8


def _gather_kernel(pt_ref, *refs):
    pages, new_ref = refs[:GATHER_PAGES], refs[GATHER_PAGES]
    rows_ref, cmpx_ref, stage_ref = refs[GATHER_PAGES + 1:]
    step = pl.program_id(1)
    last = pl.num_programs(1) - 1
    n_rows = GATHER_PAGES * PAGE_SIZE

    @pl.when(step < last)
    def _():
        for k in range(GATHER_PAGES):
            sl = slice(k * PAGE_SIZE, (k + 1) * PAGE_SIZE)
            for r in range(4):
                tile = jnp.transpose(pages[k][0, r])
                if r < 2:
                    stage_ref[r, sl, :] = tile
                else:
                    rows_ref[0, sl, (r - 2) * KV_WIDTH:(r - 1) * KV_WIDTH] = tile.astype(BF16)

    @pl.when(step == last)
    def _():
        new = new_ref[0]
        tn = new.shape[0]
        stage_ref[...] = jnp.zeros(stage_ref.shape, F32)
        for s in range(2):
            stage_ref[s, 0:tn, :] = new[:, s * KV_WIDTH:(s + 1) * KV_WIDTH]
        pad = jnp.zeros((n_rows - tn, 2 * KV_WIDTH), F32)
        rows_ref[0] = jnp.concatenate([new[:, 2 * KV_WIDTH:], pad], axis=0).astype(BF16)

    _stride_block_store(stage_ref, cmpx_ref, n_rows)


def gather_pages(cache_t, page_table, new_rows):
    b, n_pages = page_table.shape
    steps = n_pages // GATHER_PAGES
    rows = GATHER_PAGES * PAGE_SIZE
    s_out = (steps + 1) * rows

    def page_spec(k):
        def idx(i, s, pt):
            p = jnp.minimum(s, steps - 1) * GATHER_PAGES + k
            return (pt[i * n_pages + p], 0, 0, 0)
        return pl.BlockSpec((1, 4, KV_WIDTH, PAGE_SIZE), idx)

    grid_spec = pltpu.PrefetchScalarGridSpec(
        num_scalar_prefetch=1,
        grid=(b, steps + 1),
        in_specs=[page_spec(k) for k in range(GATHER_PAGES)]
        + [pl.BlockSpec((1,) + new_rows.shape[1:], lambda i, s, pt: (i, 0, 0))],
        out_specs=(
            pl.BlockSpec((1, rows, 2 * KV_WIDTH), lambda i, s, pt: (i, s, 0)),
            pl.BlockSpec((1, rows // CMP_STRIDE, CMP_STRIDE * 2 * KV_WIDTH), lambda i, s, pt: (i, s, 0)),
        ),
        scratch_shapes=[pltpu.VMEM((2, rows, KV_WIDTH), F32)],
    )
    return pl.pallas_call(
        _gather_kernel,
        grid_spec=grid_spec,
        out_shape=(jax.ShapeDtypeStruct((b, s_out, 2 * KV_WIDTH), BF16),
                   jax.ShapeDtypeStruct((b, s_out // CMP_STRIDE, CMP_STRIDE * 2 * KV_WIDTH), BF16)),
        compiler_params=_cparams(("arbitrary", "arbitrary")),
        name="gather_pages",
    )(page_table.reshape(-1), *([cache_t] * GATHER_PAGES), new_rows)


def _attention(qp, cmpx, kvb, sel_col, winb, misc, cmp_w, q_off, win_pos0, tq):
    t = qp.shape[1]
    cur_lo, cur_hi = q_off // SEL_BLOCK, (q_off + t - 1) // SEL_BLOCK
    assert cur_hi < N_SEL_LANES or (cur_lo == cur_hi == N_SEL_LANES), (q_off, t)
    n_pick = N_SEL - (1 if cur_hi >= N_SEL_LANES else 0)
    kcv = compress(cmpx, *cmp_w)
    o_cmp, mneg = cmp_select(qp, kcv, q_off, n_pick, tq)
    return sel_win_attention(qp, mneg, kvb, sel_col, winb, o_cmp, misc, q_off, win_pos0, tq)


def kernel(x_prompt, x_sample, cache_kv, cache_win, state_ssm, state_conv, page_table, c_prompt, c_sample, w_ada, b_ada, norm1_w, norm2_w, w_in, q_norm_w, k_norm_w, cmp_pe, cmp_w1, cmp_w2, attn_out_norm_w, conv_w, conv_b, dt_bias, a_log, d_skip, ssm_norm_w, w_out, w_router, e_bias, w_exp_gu, w_exp_down, w_sh_gu, w_sh_down):
    xp, xq = x_prompt, x_sample
    bp, tp, d = xp.shape
    bq, tq, _ = xq.shape
    depth = w_ada.shape[0]
    past_len = page_table.shape[1] * PAGE_SIZE
    nq = bq * tq
    tq_pad = LANES // GQA_GROUP
    assert tp % 512 == 0 and tp >= WINDOW and nq % 8 == 0 and tq <= tq_pad
    pos_p = jnp.arange(tp, dtype=jnp.int32)
    pos_q = jnp.tile(past_len + jnp.arange(tq, dtype=jnp.int32), bq)
    c_all = jnp.concatenate([c_prompt, c_sample], axis=0)
    c_all = jnp.pad(c_all, ((0, -c_all.shape[0] % 8), (0, 0)))
    outs = [[] for _ in range(8)]
    for l in range(depth):
        mod = adaln_all(c_all, w_ada[l], b_ada[l])
        mod_p = mod.reshape(mod.shape[0], 1, 6 * d)
        mod_q = jnp.repeat(mod[bp:bp + bq], tq, axis=0)
        wp = _prep_w_in(w_in[l])
        cmp_w = _prep_compress(cmp_pe[l], cmp_w1[l], cmp_w2[l])
        wo = w_out[l].astype(BF16)
        wgu, wd = w_exp_gu[l].astype(BF16), w_exp_down[l].astype(BF16)
        sgu, sd = w_sh_gu[l].astype(BF16), w_sh_down[l].astype(BF16)
        ssm_w = (conv_w[l], conv_b[l], dt_bias[l], a_log[l], d_skip[l], ssm_norm_w[l])
        n1w, n2w, anw = norm1_w[l:l + 1], norm2_w[l:l + 1], attn_out_norm_w[l:l + 1]

        qp, kvb, win, winb, z, xbc, misc, kvt, cmpx = inproj(xp, mod_p, 0, n1w, wp, q_norm_w[l], k_norm_w[l], pos_p,
                                                            512, True)
        r3 = lambda a: a.reshape(bp, tp, a.shape[-1])
        o_attn = _attention(r3(qp), cmpx, r3(kvb), 2, r3(winb), r3(misc), cmp_w, 0, 0, 128)
        y_ssm, h_new, conv_new = ssd(r3(xbc), r3(z), r3(misc), jnp.zeros((bp, CONV_WIDTH - 1, CONV_DIM), F32),
                                     jnp.zeros((bp, SSM_HEADS, SSM_HEAD_DIM, SSM_STATE), F32), *ssm_w)
        x1, h2, lg = merge(o_attn, y_ssm, xp, mod_p, 0, anw, wo, n2w, w_router[l], 512)
        w_t, pos_t, cnt = route(lg, e_bias[l], 512)
        xp = moe_sorted(h2, w_t, pos_t, cnt, x1, mod_p, 0, tp, wgu, wd, sgu, sd, 512).reshape(bp, tp, d)
        outs[0].append(jnp.transpose(kvt.reshape(bp, 4, N_KV_HEADS, HEAD_DIM, tp), (0, 4, 1, 2, 3)))
        outs[1].append(win.reshape(bp, tp, 2, N_KV_HEADS, HEAD_DIM)[:, tp - WINDOW:])
        outs[2].append(h_new)
        outs[3].append(conv_new)

        xq1 = xq.reshape(1, nq, d)
        qp, kvb, win, winb, z, xbc, misc, kv = inproj(xq1, mod_q, 0, n1w, wp, q_norm_w[l], k_norm_w[l], pos_q, nq,
                                                      False)
        rq = lambda a: a.reshape(bq, tq, a.shape[-1])
        padq = lambda a: jnp.pad(rq(a), ((0, 0), (0, tq_pad - tq), (0, 0)))
        cache_t = jnp.transpose(cache_kv[l], (0, 2, 3, 4, 1)).reshape(cache_kv.shape[1], 4, KV_WIDTH, PAGE_SIZE)
        past, cmpx = gather_pages(cache_t, page_table, rq(kv))
        win_all = jnp.concatenate([cache_win[l].reshape(bq, WINDOW, 2 * KV_WIDTH).astype(BF16), rq(winb),
                                   jnp.zeros((bq, -(WINDOW + tq_pad) % WIN_CHUNK + tq_pad - tq, 2 * KV_WIDTH), BF16)],
                                  axis=1)
        o_attn = _attention(padq(qp), cmpx, past, 0, win_all, padq(misc), cmp_w, past_len, past_len - WINDOW,
                            tq_pad)[:, :tq]
        y_ssm, h_new, conv_new = ssd(rq(xbc), rq(z), rq(misc), state_conv[l], state_ssm[l], *ssm_w)
        x1, h2, lg = merge(o_attn.reshape(1, nq, ATTN_WIDTH), y_ssm.reshape(1, nq, SSM_WIDTH), xq1, mod_q, 0,
                           anw, wo, n2w, w_router[l], nq)
        w_t, pos_t, cnt = route(lg, e_bias[l], nq)
        xq = moe(h2, w_t, pos_t, cnt, x1, mod_q, 0, nq, wgu, wd, sgu, sd, nq).reshape(bq, tq, d)
        win_rows = win.reshape(bq, tq, 2, N_KV_HEADS, HEAD_DIM)
        outs[4].append(kv.reshape(bq, tq, 4, N_KV_HEADS, HEAD_DIM))
        outs[5].append(jnp.concatenate([cache_win[l], win_rows.astype(cache_win.dtype)], axis=1)[:, tq:])
        outs[6].append(h_new)
        outs[7].append(conv_new)
    return (xp, xq) + tuple(jnp.stack(o) for o in outs)
```

```python
import functools
import math

import jax
import jax.numpy as jnp
import numpy as np
from jax import lax
from jax.experimental import pallas as pl
from jax.experimental.pallas import tpu as pltpu
from jax.experimental.pallas import tpu_sc as plsc

D_MODEL = 1024
PAGE_SIZE = 128
HEAD_DIM = 64
N_Q_HEADS = 8
N_KV_HEADS = 2
GQA_GROUP = N_Q_HEADS // N_KV_HEADS
ATTN_WIDTH = N_Q_HEADS * HEAD_DIM
KV_WIDTH = N_KV_HEADS * HEAD_DIM
ROPE_DIM = HEAD_DIM // 4
ROPE_THETA = 500000.0
CMP_LEN = 32
CMP_STRIDE = 16
CMP_HIDDEN = 4 * HEAD_DIM
SEL_BLOCK = 64
N_SEL = 16
N_LOCAL = 2
WINDOW = 512
SSM_HEADS = 8
SSM_HEAD_DIM = 64
SSM_WIDTH = SSM_HEADS * SSM_HEAD_DIM
SSM_GROUPS = 2
SSM_STATE = 128
CONV_WIDTH = 4
CONV_DIM = SSM_WIDTH + 2 * SSM_GROUPS * SSM_STATE
SSD_CHUNK = 128
MIX_WIDTH = ATTN_WIDTH + SSM_WIDTH
N_EXPERTS = 64
N_EXPERT_GROUPS = 8
TOPK_GROUPS = 4
TOP_K = 8
D_EXPERT = 256
D_SHARED = 256
ROUTED_SCALE = 2.5
IN_SIZES = (ATTN_WIDTH, 6 * KV_WIDTH, 3 * N_Q_HEADS, SSM_WIDTH, CONV_DIM, SSM_HEADS)
N_IN = sum(IN_SIZES)
EPS = 1e-6
NEG = -1e30
BIG = 1e6

LANES = 128
VMEM_LIMIT = 56 * 1024 * 1024

BF16 = jnp.bfloat16
F32 = jnp.float32
LOG2E = math.log2(math.e)


def _cparams(sem, flags=None):
    return pltpu.CompilerParams(dimension_semantics=sem, vmem_limit_bytes=VMEM_LIMIT, flags=flags)


def _silu(x):
    return x * jax.nn.sigmoid(x)


def _dot(a, b):
    return jnp.dot(a, b, preferred_element_type=F32)


def _dot_nt(a, b):
    return lax.dot_general(a, b, (((1,), (1,)), ((), ())), preferred_element_type=F32)


def _mod_spec(mod, col, tm, tiles_per_b, row0):
    if mod.ndim == 3:
        return pl.BlockSpec((1, 1, D_MODEL), lambda i, *_: (row0 + i // tiles_per_b, 0, col))
    return pl.BlockSpec((tm, D_MODEL), lambda i, *_: (i, col))


def _mod(ref):
    return ref[0] if len(ref.shape) == 3 else ref[...]


def _adaln_kernel(c_ref, w_ref, b_ref, o_ref):
    c = c_ref[...]
    a = _silu(c).astype(BF16)
    o_ref[...] = _dot(a, w_ref[...].astype(BF16)) + b_ref[...]


def adaln_all(c_all, w_ada, b_ada):
    rows = c_all.shape[0]
    n = w_ada.shape[1]
    tn = 1024
    return pl.pallas_call(
        _adaln_kernel,
        grid=(n // tn,),
        in_specs=[
            pl.BlockSpec((rows, D_MODEL), lambda j: (0, 0)),
            pl.BlockSpec((D_MODEL, tn), lambda j: (0, j)),
            pl.BlockSpec((1, tn), lambda j: (0, j)),
        ],
        out_specs=pl.BlockSpec((rows, tn), lambda j: (0, j)),
        out_shape=jax.ShapeDtypeStruct((rows, n), F32),
        compiler_params=_cparams(("arbitrary",)),
        name="adaln",
    )(c_all, w_ada, b_ada.reshape(1, n))


_C_Q = 0
_C_KV = _C_Q + ATTN_WIDTH
_C_Z = _C_KV + 6 * KV_WIDTH
_C_XBC = _C_Z + SSM_WIDTH
_C_MISC = _C_XBC + CONV_DIM
N_IN_PAD = _C_MISC + LANES
N_GATES = 3 * N_Q_HEADS


def _prep_w_in(w_in):
    s = np.cumsum((0,) + IN_SIZES)
    q, kv, g, z, xbc, dt = (w_in[:, int(s[i]):int(s[i + 1])] for i in range(6))
    pad = jnp.zeros((w_in.shape[0], LANES - N_GATES - SSM_HEADS), w_in.dtype)
    return jnp.concatenate([q, kv, z, xbc, dt, g, pad], axis=1).astype(BF16)


def _group_mean_matrix(width):
    i = np.arange(width)
    m = (i[:, None] // HEAD_DIM == i[None, :] // HEAD_DIM).astype(np.float32) / HEAD_DIM
    return jnp.asarray(m, BF16)


def _rope_tables(pos):
    half = ROPE_DIM // 2
    inv_freq = ROPE_THETA ** (-jnp.arange(half, dtype=F32) / half)
    ang = pos.astype(F32)[:, None] * inv_freq[None, :]
    cos, sin = jnp.cos(ang), jnp.sin(ang)
    t = pos.shape[0]
    one = jnp.ones((t, HEAD_DIM - ROPE_DIM), F32)
    zero = jnp.zeros((t, HEAD_DIM - ROPE_DIM), F32)
    zh = jnp.zeros((t, half), F32)
    c = jnp.concatenate([cos, cos, one], axis=1)
    s_up = jnp.concatenate([-sin, zh, zero], axis=1)
    s_dn = jnp.concatenate([zh, sin, zero], axis=1)
    rep = LANES // HEAD_DIM
    return jnp.tile(c, (1, rep)), jnp.tile(s_up, (1, rep)), jnp.tile(s_dn, (1, rep))


def _rope(x, c, s_up, s_dn):
    w = x.shape[1]
    half = ROPE_DIM // 2
    rep = w // LANES
    ct = jnp.concatenate([c] * rep, axis=1) if rep > 1 else c
    su = jnp.concatenate([s_up] * rep, axis=1) if rep > 1 else s_up
    sd = jnp.concatenate([s_dn] * rep, axis=1) if rep > 1 else s_dn
    up = pltpu.roll(x, w - half, axis=1)
    dn = pltpu.roll(x, half, axis=1)
    return x * ct + up * su + dn * sd


def _stride_block_store(stage_ref, cmpx_ref, n_rows):
    for j in range(CMP_STRIDE):
        for s in range(2):
            rows_j = stage_ref[s, pl.ds(j, n_rows // CMP_STRIDE, stride=CMP_STRIDE), :]
            c0 = (2 * j + s) * KV_WIDTH
            cmpx_ref[0, :, c0:c0 + KV_WIDTH] = rows_j.astype(BF16)


def _inproj_kernel(x_ref, shift_ref, scale_ref, nw_ref, w_ref, qw_ref, kw_ref, gq_ref, gk_ref,
                   c_ref, su_ref, sd_ref,
                   qp_ref, kvb_ref, win_ref, winb_ref, z_ref, xbc_ref, misc_ref, *rest, seq_layout):
    x = x_ref[...]
    ms = jnp.mean(x * x, axis=-1, keepdims=True)
    h = x * lax.rsqrt(ms + EPS) * nw_ref[...]
    h = h * (1.0 + _mod(scale_ref)) + _mod(shift_ref)
    hb = h.astype(BF16)
    c, su, sd = c_ref[...], su_ref[...], sd_ref[...]

    q = _dot(hb, w_ref[:, _C_Q:_C_Q + ATTN_WIDTH])
    qms = _dot((q * q).astype(BF16), gq_ref[...])
    q = q * lax.rsqrt(qms + EPS) * qw_ref[...]
    q = _rope(q, c, su, sd) * (HEAD_DIM ** -0.5 * LOG2E)
    lane = lax.broadcasted_iota(jnp.int32, q.shape, 1) % LANES
    lo = lane < HEAD_DIM
    q_up = pltpu.roll(q, ATTN_WIDTH - HEAD_DIM, axis=1)
    q_dn = pltpu.roll(q, HEAD_DIM, axis=1)
    zero = jnp.zeros_like(q)
    nat_lo = jnp.where(lo, q, zero)
    nat_hi = jnp.where(lo, zero, q)
    up_lo = jnp.where(lo, q_up, zero)
    dn_hi = jnp.where(lo, zero, q_dn)
    blocks = []
    for hd in range(N_Q_HEADS):
        pair = hd // 2
        sl = slice(pair * LANES, (pair + 1) * LANES)
        if hd < GQA_GROUP:
            blocks.append((nat_lo if hd % 2 == 0 else up_lo)[:, sl])
        else:
            blocks.append((dn_hi if hd % 2 == 0 else nat_hi)[:, sl])
    qp_ref[...] = jnp.concatenate(blocks, axis=1).astype(BF16)

    kv = _dot(hb, w_ref[:, _C_KV:_C_KV + 6 * KV_WIDTH])
    outs = []
    for br in range(3):
        k = kv[:, br * 2 * KV_WIDTH:br * 2 * KV_WIDTH + KV_WIDTH]
        v = kv[:, br * 2 * KV_WIDTH + KV_WIDTH:(br + 1) * 2 * KV_WIDTH]
        kms = _dot((k * k).astype(BF16), gk_ref[...])
        k = k * lax.rsqrt(kms + EPS) * kw_ref[:, br * KV_WIDTH:(br + 1) * KV_WIDTH]
        k = _rope(k, c, su, sd)
        outs += [k, v]
    kvrows = jnp.concatenate(outs[:4], axis=1)
    winrows = jnp.concatenate(outs[4:], axis=1)
    kvb_ref[...] = kvrows.astype(BF16)
    win_ref[...] = winrows
    winb_ref[...] = winrows.astype(BF16)
    if seq_layout:
        kvt_ref, cmpx_ref, stage_ref = rest
        tm = kvrows.shape[0]
        for r in range(4):
            kvt_ref[0, r] = jnp.transpose(kvrows[:, r * KV_WIDTH:(r + 1) * KV_WIDTH])
        for s in range(2):
            stage_ref[s] = kvrows[:, s * KV_WIDTH:(s + 1) * KV_WIDTH]
        _stride_block_store(stage_ref, cmpx_ref, tm)
    else:
        rest[0][...] = kvrows

    z_ref[...] = _dot(hb, w_ref[:, _C_Z:_C_Z + SSM_WIDTH])
    xbc_ref[...] = _dot(hb, w_ref[:, _C_XBC:_C_XBC + CONV_DIM])
    misc_ref[...] = _dot(hb, w_ref[:, _C_MISC:_C_MISC + LANES])


def inproj(x, mod3, mod_row0, norm_w, wp, q_norm_w, k_norm_w, pos, tm, seq_layout):
    b, t, d = x.shape
    n = b * t
    tiles_per_b = t // tm
    xf = x.reshape(n, d)
    c, su, sd = _rope_tables(pos)
    qw = jnp.tile(q_norm_w, N_Q_HEADS).reshape(1, ATTN_WIDTH)
    kw = jnp.concatenate([jnp.tile(k_norm_w[i], N_KV_HEADS) for i in range(3)]).reshape(1, 3 * KV_WIDTH)
    gq = _group_mean_matrix(ATTN_WIDTH)
    gk = _group_mean_matrix(KV_WIDTH)

    def mod_spec(col):
        return _mod_spec(mod3, col, tm, tiles_per_b, mod_row0)

    def tok(wd):
        return pl.BlockSpec((tm, wd), lambda i: (i, 0))

    def full(a):
        return pl.BlockSpec(a.shape, lambda i: (0,) * a.ndim)

    rope_spec = pl.BlockSpec((tm, LANES), lambda i: (i % tiles_per_b, 0))
    out_shape = [
        jax.ShapeDtypeStruct((n, N_Q_HEADS * LANES), BF16),
        jax.ShapeDtypeStruct((n, 4 * KV_WIDTH), BF16),
        jax.ShapeDtypeStruct((n, 2 * KV_WIDTH), F32),
        jax.ShapeDtypeStruct((n, 2 * KV_WIDTH), BF16),
        jax.ShapeDtypeStruct((n, SSM_WIDTH), F32),
        jax.ShapeDtypeStruct((n, CONV_DIM), F32),
        jax.ShapeDtypeStruct((n, LANES), F32),
    ]
    out_specs = [tok(s.shape[1]) for s in out_shape]
    scratch = []
    if seq_layout:
        out_shape += [jax.ShapeDtypeStruct((b, 4, KV_WIDTH, t), F32),
                      jax.ShapeDtypeStruct((b, t // CMP_STRIDE, CMP_STRIDE * 2 * KV_WIDTH), BF16)]
        out_specs += [pl.BlockSpec((1, 4, KV_WIDTH, tm), lambda i: (i // tiles_per_b, 0, 0, i % tiles_per_b)),
                      pl.BlockSpec((1, tm // CMP_STRIDE, CMP_STRIDE * 2 * KV_WIDTH),
                                   lambda i: (i // tiles_per_b, i % tiles_per_b, 0))]
        scratch = [pltpu.VMEM((2, tm, KV_WIDTH), F32)]
    else:
        out_shape += [jax.ShapeDtypeStruct((n, 4 * KV_WIDTH), F32)]
        out_specs += [tok(4 * KV_WIDTH)]
    return pl.pallas_call(
        functools.partial(_inproj_kernel, seq_layout=seq_layout),
        grid=(n // tm,),
        in_specs=[tok(d), mod_spec(0), mod_spec(1), full(norm_w), full(wp), full(qw), full(kw), full(gq), full(gk),
                  rope_spec, rope_spec, rope_spec],
        out_specs=tuple(out_specs),
        out_shape=tuple(out_shape),
        scratch_shapes=scratch,
        compiler_params=_cparams(("arbitrary",)),
        name="inproj",
    )(xf, mod3, mod3, norm_w, wp, qw, kw, gq, gk, c, su, sd)


def _prep_compress(cmp_pe, cmp_w1, cmp_w2):
    half = CMP_LEN // 2
    eye = jnp.eye(N_KV_HEADS, dtype=F32)
    w1 = cmp_w1.reshape(2, CMP_LEN, HEAD_DIM, CMP_HIDDEN)
    w1s = []
    for part in (w1[:, :half], w1[:, half:]):
        w1s.append(jnp.einsum("pjdo,hg->pjhdgo", part, eye).reshape(2, half * KV_WIDTH, N_KV_HEADS * CMP_HIDDEN))
    w1p = jnp.concatenate(w1s, axis=2).astype(BF16)
    pe = cmp_pe.reshape(2, 2, half, 1, HEAD_DIM)
    pep = jnp.broadcast_to(pe, (2, 2, half, N_KV_HEADS, HEAD_DIM)).reshape(2, 2, half * KV_WIDTH)
    w2p = jnp.einsum("poe,hg->phoge", cmp_w2, eye).reshape(2, N_KV_HEADS * CMP_HIDDEN, KV_WIDTH).astype(BF16)
    return w1p, pep, w2p


def _compress_kernel(x_ref, w1_ref, pe_ref, w2_ref, o_ref, *, row_w):
    part = pl.program_id(1)
    nb = x_ref.shape[1]
    half = CMP_LEN // 2
    hid = N_KV_HEADS * CMP_HIDDEN
    cols = []
    for j in range(half):
        a = x_ref[0, :, j * row_w:j * row_w + KV_WIDTH]
        b = x_ref[0, :, j * row_w + KV_WIDTH:j * row_w + 2 * KV_WIDTH]
        cols.append(jnp.where(part == 0, a, b))
    x = jnp.concatenate(cols, axis=1).astype(F32)
    pe = pe_ref[0]
    u = _dot((x + pe[0:1]).astype(BF16), w1_ref[0, :, :hid])
    v = _dot((x + pe[1:2]).astype(BF16), w1_ref[0, :, hid:])
    h1 = u + pltpu.roll(v, nb - 1, axis=0)
    out = _dot(_silu(h1).astype(BF16), w2_ref[0])
    row = lax.broadcasted_iota(jnp.int32, out.shape, 0)
    o_ref[0, 0] = jnp.where(row < nb - 1, out, 0.0).astype(o_ref.dtype)


def compress(x, w1p, pep, w2p):
    b, nb, width = x.shape
    row_w = width // CMP_STRIDE
    return pl.pallas_call(
        functools.partial(_compress_kernel, row_w=row_w),
        grid=(b, 2),
        in_specs=[
            pl.BlockSpec((1, nb, CMP_STRIDE * row_w), lambda i, p: (i, 0, 0)),
            pl.BlockSpec((1,) + w1p.shape[1:], lambda i, p: (p, 0, 0)),
            pl.BlockSpec((1,) + pep.shape[1:], lambda i, p: (p, 0, 0)),
            pl.BlockSpec((1,) + w2p.shape[1:], lambda i, p: (p, 0, 0)),
        ],
        out_specs=pl.BlockSpec((1, 1, nb, KV_WIDTH), lambda i, p: (i, p, 0, 0)),
        out_shape=jax.ShapeDtypeStruct((b, 2, nb, KV_WIDTH), BF16),
        compiler_params=_cparams(("arbitrary", "arbitrary")),
        name="compress",
    )(x, w1p, pep, w2p)


N_SEL_LANES = LANES


def _cover_matrix(nb):
    c = np.arange(nb)[:, None]
    j = np.arange(N_SEL_LANES)[None, :]
    start = c * CMP_STRIDE
    m = (start < (j + 1) * SEL_BLOCK) & (start + CMP_LEN > j * SEL_BLOCK)
    return jnp.asarray(m.astype(np.float32), BF16)


def _place_heads(res, kv):
    lane = lax.broadcasted_iota(jnp.int32, res[0].shape, 1)
    lo = lane < HEAD_DIM
    blocks = []
    for pair in range(GQA_GROUP // 2):
        a, b = res[2 * pair], res[2 * pair + 1]
        if kv == 0:
            blocks.append(jnp.where(lo, a, pltpu.roll(b, HEAD_DIM, axis=1)))
        else:
            blocks.append(jnp.where(lo, pltpu.roll(a, HEAD_DIM, axis=1), b))
    return jnp.concatenate(blocks, axis=1)


def _group_rows(q_ref, kv):
    heads = range(kv * GQA_GROUP, (kv + 1) * GQA_GROUP)
    return jnp.concatenate([q_ref[0, :, hd * LANES:(hd + 1) * LANES] for hd in heads], axis=0)


def _heads_from_transposed(out_t, tq, kv):
    out = jnp.transpose(out_t)
    return _place_heads([out[g * tq:(g + 1) * tq] for g in range(GQA_GROUP)], kv)


def _cmp_select_kernel(q_ref, kc_ref, vc_ref, covt_ref, o_ref, m_ref, *, q_off, n_pick):
    tq = q_ref.shape[1]
    rows = GQA_GROUP * tq
    nb = kc_ref.shape[2]
    wl = max(tq, LANES)
    assert tq % LANES == 0 or rows == LANES
    t0 = q_off + pl.program_id(1) * tq
    kc = kc_ref[0, 0]
    vc = vc_ref[0, 0]
    qpos = t0 + lax.broadcasted_iota(jnp.int32, (nb, rows), 1) % tq
    cend = lax.broadcasted_iota(jnp.int32, (nb, rows), 0) * CMP_STRIDE + (CMP_LEN - 1)
    valid = cend <= qpos
    blk = lax.broadcasted_iota(jnp.int32, (N_SEL_LANES, wl), 0)
    cur = (t0 + lax.broadcasted_iota(jnp.int32, (N_SEL_LANES, wl), 1) % tq) // SEL_BLOCK
    forced = (blk == 0) | ((blk <= cur) & (blk > cur - N_LOCAL))
    o_groups = []
    for kv in range(N_KV_HEADS):
        s = _dot_nt(kc, _group_rows(q_ref, kv))
        s = jnp.where(valid, s, NEG)
        e = jnp.exp2(s - jnp.max(s, axis=0, keepdims=True))
        p = e / jnp.sum(e, axis=0, keepdims=True)
        p = jnp.where(valid, p, 0.0)
        o_t = lax.dot_general(vc, p.astype(BF16), (((0,), (0,)), ((), ())), preferred_element_type=F32)
        o_groups.append(_heads_from_transposed(o_t, tq, kv))
        if tq % LANES == 0:
            psum = sum(p[:, g * tq:(g + 1) * tq] for g in range(GQA_GROUP))
        else:
            psum = p + sum(pltpu.roll(p, g * tq, axis=1) for g in range(1, GQA_GROUP))
        hi, lo = _split2(psum)
        imp = _dot(covt_ref[...], hi) + _dot(covt_ref[...], lo)
        x = jnp.where(forced, BIG, jnp.where(blk > cur, -BIG, imp))
        sel = jnp.zeros(x.shape, jnp.bool_)
        for _ in range(n_pick):
            mx = jnp.max(x, axis=0, keepdims=True)
            idx = jnp.min(jnp.where(x == mx, blk, N_SEL_LANES), axis=0, keepdims=True)
            hit = blk == idx
            sel = sel | hit
            x = jnp.where(hit, -jnp.inf, x)
        mneg = jnp.transpose(jnp.where(sel, 0.0, NEG))
        m_ref[0, kv] = mneg[:tq].astype(m_ref.dtype)
    o_ref[0] = jnp.concatenate(o_groups, axis=1)


def cmp_select(qp, kcv, q_off, n_pick, tq):
    b, t, _ = qp.shape
    nb = kcv.shape[2]
    cover = jnp.transpose(_cover_matrix(nb))
    return pl.pallas_call(
        functools.partial(_cmp_select_kernel, q_off=q_off, n_pick=n_pick),
        grid=(b, t // tq),
        in_specs=[
            pl.BlockSpec((1, tq, N_Q_HEADS * LANES), lambda i, j: (i, j, 0)),
            pl.BlockSpec((1, 1, nb, KV_WIDTH), lambda i, j: (i, 0, 0, 0)),
            pl.BlockSpec((1, 1, nb, KV_WIDTH), lambda i, j: (i, 1, 0, 0)),
            pl.BlockSpec((N_SEL_LANES, nb), lambda i, j: (0, 0)),
        ],
        out_specs=(
            pl.BlockSpec((1, tq, ATTN_WIDTH), lambda i, j: (i, j, 0)),
            pl.BlockSpec((1, N_KV_HEADS, tq, N_SEL_LANES), lambda i, j: (i, 0, j, 0)),
        ),
        out_shape=(
            jax.ShapeDtypeStruct((b, t, ATTN_WIDTH), F32),
            jax.ShapeDtypeStruct((b, N_KV_HEADS, t, N_SEL_LANES), BF16),
        ),
        compiler_params=_cparams(("arbitrary", "arbitrary")),
        name="cmp_select",
    )(qp, kcv, kcv, cover)


SEL_TILE_ELEMS = 512 * 512
WIN_CHUNK = 256


def _block_onehot(s):
    key = np.arange(s)[:, None]
    j = np.arange(N_SEL_LANES)[None, :]
    return jnp.asarray((key // SEL_BLOCK == j).astype(np.float32), BF16)


def _gate_expand():
    m = np.zeros((3, LANES, ATTN_WIDTH), np.float32)
    for br in range(3):
        for hd in range(N_Q_HEADS):
            m[br, SSM_HEADS + 3 * hd + br, hd * HEAD_DIM:(hd + 1) * HEAD_DIM] = 1.0
    return jnp.asarray(m, BF16)


def _flash_update(ss, v, m_ref, acc_ref):
    lane = lax.broadcasted_iota(jnp.int32, v.shape, 1)
    one = jnp.ones(v.shape, v.dtype)
    stage = []
    for k, s in enumerate(ss):
        m_old = m_ref[k]
        m_new = jnp.maximum(m_old, jnp.max(s, axis=0, keepdims=True))
        alpha = jnp.exp2(m_old - m_new)
        p = jnp.exp2(s - m_new)
        m_ref[k] = m_new
        stage.append((alpha, p.astype(BF16)))
    for k, (alpha, p) in enumerate(stage):
        vk = jnp.where((lane < HEAD_DIM) == (k == 0), v, one)
        pv = lax.dot_general(vk, p, (((0,), (0,)), ((), ())), preferred_element_type=F32)
        acc_ref[k] = alpha * acc_ref[k] + pv


def _sel_chunk(rows, n_keys):
    chunk = SEL_TILE_ELEMS // rows
    while n_keys % chunk:
        chunk //= 2
    return chunk


def _sel_win_kernel(q_ref, mneg_ref, ksel_ref, vsel_ref, et_ref, kwin_ref, vwin_ref, ocmp_ref, misc_ref, eg_ref,
                    o_ref, lhs_ref, m_ref, acc_ref, *, q_off, win_pos0):
    tq = q_ref.shape[1]
    rows = GQA_GROUP * tq
    SEL_CHUNK = _sel_chunk(rows, ksel_ref.shape[1])
    t0 = q_off + pl.program_id(1) * tq
    n_sel = lax.shift_right_logical(t0 + tq - 1, int(math.log2(SEL_CHUNK))) + 1
    w_lo = jnp.maximum(t0 - (WINDOW - 1) - win_pos0, 0) // WIN_CHUNK
    w_hi = (t0 + tq - 1 - win_pos0) // WIN_CHUNK + 1

    def qrow(n_keys):
        return lax.broadcasted_iota(jnp.int32, (n_keys, rows), 1) % tq + t0

    def init():
        m_ref[...] = jnp.full(m_ref.shape, NEG, F32)
        acc_ref[...] = jnp.zeros(acc_ref.shape, F32)

    def finish():
        outs = []
        for kv in range(N_KV_HEADS):
            acc = acc_ref[kv]
            denom_row = HEAD_DIM * (1 - kv)
            outs.append(_heads_from_transposed(acc / acc[denom_row:denom_row + 1, :], tq, kv))
        return jnp.concatenate(outs, axis=1)

    for kv in range(N_KV_HEADS):
        for g in range(GQA_GROUP):
            hd = kv * GQA_GROUP + g
            lhs_ref[kv, g * tq:(g + 1) * tq, :LANES] = q_ref[0, :, hd * LANES:(hd + 1) * LANES]
            lhs_ref[kv, g * tq:(g + 1) * tq, LANES:] = mneg_ref[0, kv]

    init()

    def sel_step(c, carry, causal):
        r0 = pl.multiple_of(c * SEL_CHUNK, SEL_CHUNK)
        rhs = jnp.concatenate([ksel_ref[0, pl.ds(r0, SEL_CHUNK), :], et_ref[pl.ds(r0, SEL_CHUNK), :]], axis=1)
        v = vsel_ref[0, pl.ds(r0, SEL_CHUNK), :]
        if causal:
            ok = r0 + lax.broadcasted_iota(jnp.int32, (SEL_CHUNK, rows), 0) <= qrow(SEL_CHUNK)
        ss = [_dot_nt(rhs, lhs_ref[kv]) for kv in range(N_KV_HEADS)]
        if causal:
            ss = [jnp.where(ok, s, NEG) for s in ss]
        _flash_update(ss, v, m_ref, acc_ref)
        return carry

    n_full = lax.shift_right_logical(t0 + 1, int(math.log2(SEL_CHUNK)))
    lax.fori_loop(0, n_full, functools.partial(sel_step, causal=False), 0)
    lax.fori_loop(n_full, n_sel, functools.partial(sel_step, causal=True), 0)
    o_sel = finish()

    init()

    def win_step(c, carry):
        r0 = pl.multiple_of(c * WIN_CHUNK, WIN_CHUNK)
        k = kwin_ref[0, pl.ds(r0, WIN_CHUNK), :]
        v = vwin_ref[0, pl.ds(r0, WIN_CHUNK), :]
        wpos = win_pos0 + r0 + lax.broadcasted_iota(jnp.int32, (WIN_CHUNK, rows), 0)
        qr = qrow(WIN_CHUNK)
        ok = (wpos <= qr) & (wpos > qr - WINDOW)
        ss = [jnp.where(ok, _dot_nt(k, lhs_ref[kv, :, :LANES]), NEG) for kv in range(N_KV_HEADS)]
        _flash_update(ss, v, m_ref, acc_ref)
        return carry

    lax.fori_loop(w_lo, w_hi, win_step, 0)
    o_win = finish()

    gates = jax.nn.sigmoid(misc_ref[0])
    ghi = gates.astype(BF16)
    glo = (gates - ghi.astype(F32)).astype(BF16)
    branches = (ocmp_ref[0], o_sel, o_win)
    out = jnp.zeros(branches[0].shape, F32)
    for br in range(3):
        out = out + (_dot(ghi, eg_ref[br]) + _dot(glo, eg_ref[br])) * branches[br]
    o_ref[0] = out


def sel_win_attention(qp, mneg, kvb, sel_col, winb, o_cmp, misc, q_off, win_pos0, tq):
    b, t, _ = qp.shape
    s = kvb.shape[1]
    sw = winb.shape[1]
    et = _block_onehot(s)
    eg = _gate_expand()
    rows = GQA_GROUP * tq
    assert q_off + t <= s and q_off + t - win_pos0 <= sw and sw % WIN_CHUNK == 0
    return pl.pallas_call(
        functools.partial(_sel_win_kernel, q_off=q_off, win_pos0=win_pos0),
        grid=(b, t // tq),
        in_specs=[
            pl.BlockSpec((1, tq, N_Q_HEADS * LANES), lambda i, j: (i, j, 0)),
            pl.BlockSpec((1, N_KV_HEADS, tq, N_SEL_LANES), lambda i, j: (i, 0, j, 0)),
            pl.BlockSpec((1, s, KV_WIDTH), lambda i, j: (i, 0, sel_col)),
            pl.BlockSpec((1, s, KV_WIDTH), lambda i, j: (i, 0, sel_col + 1)),
            pl.BlockSpec((s, N_SEL_LANES), lambda i, j: (0, 0)),
            pl.BlockSpec((1, sw, KV_WIDTH), lambda i, j: (i, 0, 0)),
            pl.BlockSpec((1, sw, KV_WIDTH), lambda i, j: (i, 0, 1)),
            pl.BlockSpec((1, tq, ATTN_WIDTH), lambda i, j: (i, j, 0)),
            pl.BlockSpec((1, tq, LANES), lambda i, j: (i, j, 0)),
            pl.BlockSpec((3, LANES, ATTN_WIDTH), lambda i, j: (0, 0, 0)),
        ],
        out_specs=pl.BlockSpec((1, tq, ATTN_WIDTH), lambda i, j: (i, j, 0)),
        out_shape=jax.ShapeDtypeStruct((b, t, ATTN_WIDTH), F32),
        scratch_shapes=[
            pltpu.VMEM((N_KV_HEADS, rows, 2 * LANES), BF16),
            pltpu.VMEM((N_KV_HEADS, 1, rows), F32),
            pltpu.VMEM((N_KV_HEADS, LANES, rows), F32),
        ],
        compiler_params=_cparams(("arbitrary", "arbitrary")),
        name="sel_win_attention",
    )(qp, mneg, kvb, kvb, et, winb, winb, o_cmp, misc, eg)


CONV_PAD = 8
HEAD_PAIRS = SSM_HEADS // 2


def _split3(x):
    a = x.astype(BF16)
    r = x - a.astype(F32)
    b = r.astype(BF16)
    c = (r - b.astype(F32)).astype(BF16)
    return a, b, c


def _ssd_kernel(xbc_ref, z_ref, misc_ref, conv0_ref, h0_ref, cw_ref, cb_ref, dtb_ref, a_ref, dsk_ref, nw_ref,
                y_ref, hout_ref, cout_ref, xp_ref, h_ref, ms_ref, *, t_valid):
    ch = pl.program_id(1)
    L = SSD_CHUNK
    keep = CONV_WIDTH - 1

    @pl.when(ch == 0)
    def _():
        xp_ref[...] = jnp.zeros(xp_ref.shape, F32)
        xp_ref[CONV_PAD - keep:CONV_PAD, :] = conv0_ref[0]
        h_ref[...] = h0_ref[0]

    xp_ref[CONV_PAD:CONV_PAD + t_valid, :] = xbc_ref[0]
    conv = cb_ref[...]
    for j in range(CONV_WIDTH):
        conv = conv + cw_ref[j:j + 1, :] * xp_ref[CONV_PAD - keep + j:CONV_PAD - keep + j + L, :]
    last = xp_ref[CONV_PAD + t_valid - keep:CONV_PAD + t_valid, :]
    cout_ref[0] = last
    xp_ref[CONV_PAD - keep:CONV_PAD, :] = last
    xc = _silu(conv)

    row = lax.broadcasted_iota(jnp.int32, (L, LANES), 0)
    lane = lax.broadcasted_iota(jnp.int32, (L, LANES), 1)
    if t_valid == L:
        raw = misc_ref[0]
    else:
        ms_ref[...] = jnp.zeros(ms_ref.shape, F32)
        ms_ref[0:t_valid, :] = misc_ref[0]
        raw = ms_ref[...]
    v = raw + dtb_ref[...]
    dt = jnp.maximum(v, 0.0) + jnp.log(1.0 + jnp.exp(-jnp.abs(v)))
    dt = jnp.where((lane < SSM_HEADS) & (row < t_valid), dt, 0.0)
    da = dt * a_ref[...]
    tri = (lax.broadcasted_iota(jnp.int32, (L, L), 1) <= lax.broadcasted_iota(jnp.int32, (L, L), 0))
    trib = tri.astype(BF16)
    acum = sum(_dot(trib, part) for part in _split3(da))
    acum_t = jnp.transpose(acum)
    dt_t = jnp.transpose(dt)
    e_acum = jnp.exp(acum)
    e_last = jnp.exp(acum[L - 1:L, :])
    w_end = jnp.exp(acum[L - 1:L, :] - acum) * dt
    lo = lane < SSM_HEAD_DIM

    ys = []
    for pair in range(HEAD_PAIRS):
        grp = (2 * pair) // (SSM_HEADS // SSM_GROUPS)
        bg = xc[:, SSM_WIDTH + grp * SSM_STATE:SSM_WIDTH + (grp + 1) * SSM_STATE].astype(BF16)
        cg = xc[:, SSM_WIDTH + (SSM_GROUPS + grp) * SSM_STATE:SSM_WIDTH + (SSM_GROUPS + grp + 1) * SSM_STATE].astype(BF16)
        g = _dot_nt(cg, bg)
        xpair = xc[:, pair * LANES:(pair + 1) * LANES]
        y = jnp.zeros((L, LANES), F32)
        for sub in range(2):
            hd = 2 * pair + sub
            seg = acum[:, hd:hd + 1] - acum_t[hd:hd + 1, :]
            m = g * jnp.exp(jnp.where(tri, seg, NEG)) * dt_t[hd:hd + 1, :]
            xm = jnp.where(lo if sub == 0 else ~lo, xpair, 0.0)
            y = y + _dot(m.astype(BF16), xm.astype(BF16))
        col = lambda a: jnp.where(lo, a[:, 2 * pair:2 * pair + 1], a[:, 2 * pair + 1:2 * pair + 2])
        hp = h_ref[pair]
        y = y + _dot_nt(cg, hp.astype(BF16)) * col(e_acum)
        y = y + col(dsk_ref[...]) * xpair
        xw = (xpair * col(w_end)).astype(BF16)
        st = lax.dot_general(xw, bg, (((0,), (0,)), ((), ())), preferred_element_type=F32)
        prow = lax.broadcasted_iota(jnp.int32, (LANES, LANES), 0) < SSM_HEAD_DIM
        dec = jnp.where(prow, e_last[:, 2 * pair:2 * pair + 1], e_last[:, 2 * pair + 1:2 * pair + 2])
        h_ref[pair] = hp * dec + st
        ys.append(y)
    y = jnp.concatenate(ys, axis=1)
    if t_valid != L:
        y = y[:t_valid]
    y = y * _silu(z_ref[0])
    y = y * lax.rsqrt(jnp.mean(y * y, axis=-1, keepdims=True) + EPS) * nw_ref[...]
    y_ref[0] = y

    @pl.when(ch == pl.num_programs(1) - 1)
    def _():
        hout_ref[0] = h_ref[...]


def ssd(xbc, z, misc, conv0, h0, conv_w, conv_b, dt_bias, a_log, d_skip, norm_w):
    b, t, _ = xbc.shape
    L = SSD_CHUNK
    t_valid = L if t % L == 0 else t
    assert t_valid == L or t < L
    n_ch = max(t // L, 1)
    keep = CONV_WIDTH - 1
    pad8 = lambda v: jnp.pad(v.astype(F32), (0, LANES - SSM_HEADS)).reshape(1, LANES)
    dtb = pad8(dt_bias)
    a = pad8(-jnp.exp(a_log.astype(F32)))
    dsk = pad8(d_skip)
    h0p = h0.reshape(b, HEAD_PAIRS, 2 * SSM_HEAD_DIM, SSM_STATE)
    full = lambda arr: pl.BlockSpec(arr.shape, lambda i, c: (0,) * arr.ndim)
    tok = lambda wd: pl.BlockSpec((1, t_valid, wd), lambda i, c: (i, c, 0))
    y, hout, cout = pl.pallas_call(
        functools.partial(_ssd_kernel, t_valid=t_valid),
        grid=(b, n_ch),
        in_specs=[
            tok(CONV_DIM), tok(SSM_WIDTH), tok(LANES),
            pl.BlockSpec((1, keep, CONV_DIM), lambda i, c: (i, 0, 0)),
            pl.BlockSpec((1, HEAD_PAIRS, 2 * SSM_HEAD_DIM, SSM_STATE), lambda i, c: (i, 0, 0, 0)),
            full(conv_w), pl.BlockSpec((1, CONV_DIM), lambda i, c: (0, 0)),
            full(dtb), full(a), full(dsk), pl.BlockSpec((1, SSM_WIDTH), lambda i, c: (0, 0)),
        ],
        out_specs=(
            tok(SSM_WIDTH),
            pl.BlockSpec((1, HEAD_PAIRS, 2 * SSM_HEAD_DIM, SSM_STATE), lambda i, c: (i, 0, 0, 0)),
            pl.BlockSpec((1, keep, CONV_DIM), lambda i, c: (i, 0, 0)),
        ),
        out_shape=(
            jax.ShapeDtypeStruct((b, t, SSM_WIDTH), F32),
            jax.ShapeDtypeStruct((b, HEAD_PAIRS, 2 * SSM_HEAD_DIM, SSM_STATE), F32),
            jax.ShapeDtypeStruct((b, keep, CONV_DIM), F32),
        ),
        scratch_shapes=[
            pltpu.VMEM((CONV_PAD + L, CONV_DIM), F32),
            pltpu.VMEM((HEAD_PAIRS, 2 * SSM_HEAD_DIM, SSM_STATE), F32),
            pltpu.VMEM((L, LANES), F32),
        ],
        compiler_params=_cparams(("arbitrary", "arbitrary")),
        name="ssd",
    )(xbc, z, misc, conv0, h0p, conv_w, conv_b.reshape(1, CONV_DIM), dtb, a, dsk, norm_w.reshape(1, SSM_WIDTH))
    return y, hout.reshape(b, SSM_HEADS, SSM_HEAD_DIM, SSM_STATE), cout


def _split2(x):
    hi = x.astype(BF16)
    return hi, (x - hi.astype(F32)).astype(BF16)


def _merge_kernel(oa_ref, ys_ref, x_ref, g1_ref, sh2_ref, sc2_ref, anw_ref, wo_ref, n2w_ref, wrh_ref, wrl_ref,
                  x1_ref, h2_ref, lg_ref):
    oa = oa_ref[...]
    a = oa * lax.rsqrt(jnp.mean(oa * oa, axis=-1, keepdims=True) + EPS) * anw_ref[...]
    cat = jnp.concatenate([a.astype(BF16), ys_ref[...].astype(BF16)], axis=1)
    x1 = x_ref[...] + _mod(g1_ref) * _dot(cat, wo_ref[...])
    x1_ref[...] = x1
    h2 = x1 * lax.rsqrt(jnp.mean(x1 * x1, axis=-1, keepdims=True) + EPS) * n2w_ref[...]
    h2 = h2 * (1.0 + _mod(sc2_ref)) + _mod(sh2_ref)
    h2_ref[...] = h2.astype(BF16)
    hh, hl = _split2(h2)
    lg_ref[...] = _dot_nt(wrh_ref[...], hh) + _dot_nt(wrh_ref[...], hl) + _dot_nt(wrl_ref[...], hh)


def merge(o_attn, y_ssm, x, mod3, mod_row0, attn_norm_w, wo, norm2_w, w_router, tm):
    b, t, d = x.shape
    n = b * t
    tiles_per_b = t // tm
    wrt = jnp.transpose(w_router)
    wrh, wrl = _split2(wrt)

    def mod_spec(col):
        return _mod_spec(mod3, col, tm, tiles_per_b, mod_row0)

    tok = lambda wd: pl.BlockSpec((tm, wd), lambda i: (i, 0))
    full = lambda a: pl.BlockSpec(a.shape, lambda i: (0,) * a.ndim)
    return pl.pallas_call(
        _merge_kernel,
        grid=(n // tm,),
        in_specs=[tok(ATTN_WIDTH), tok(SSM_WIDTH), tok(d), mod_spec(2), mod_spec(3), mod_spec(4),
                  full(attn_norm_w), full(wo), full(norm2_w), full(wrh), full(wrl)],
        out_specs=(tok(d), tok(d), pl.BlockSpec((N_EXPERTS, tm), lambda i: (0, i))),
        out_shape=(jax.ShapeDtypeStruct((n, d), F32), jax.ShapeDtypeStruct((n, d), BF16),
                   jax.ShapeDtypeStruct((N_EXPERTS, n), F32)),
        compiler_params=_cparams(("arbitrary",)),
        name="merge",
    )(o_attn.reshape(n, ATTN_WIDTH), y_ssm.reshape(n, SSM_WIDTH), x.reshape(n, d), mod3, mod3, mod3,
      attn_norm_w, wo, norm2_w, wrh, wrl)


EXPERTS_PER_GROUP = N_EXPERTS // N_EXPERT_GROUPS


def _first_max(x, ids, axes, n_ids):
    mx = jnp.max(x, axis=axes, keepdims=True)
    return ids == jnp.min(jnp.where(x == mx, ids, n_ids), axis=axes, keepdims=True), mx


def _route_kernel(lg_ref, eb_ref, tri_ref, w_ref, pos_ref, cnt_ref):
    lg = lg_ref[...]
    tn = lg.shape[2]
    scores = jax.nn.sigmoid(lg)
    biased = scores + eb_ref[...]
    sub = lax.broadcasted_iota(jnp.int32, lg.shape, 1)
    grp = lax.broadcasted_iota(jnp.int32, (N_EXPERT_GROUPS, 1, tn), 0)
    eid = lax.broadcasted_iota(jnp.int32, lg.shape, 0) * EXPERTS_PER_GROUP + sub
    hit, m1 = _first_max(biased, sub, 1, EXPERTS_PER_GROUP)
    m2 = jnp.max(jnp.where(hit, -jnp.inf, biased), axis=1, keepdims=True)
    gs = m1 + m2
    keep = jnp.zeros(gs.shape, jnp.bool_)
    for _ in range(TOPK_GROUPS):
        hit, _m = _first_max(gs, grp, 0, N_EXPERT_GROUPS)
        keep = keep | hit
        gs = jnp.where(hit, -jnp.inf, gs)
    x = jnp.where(keep, biased, NEG)
    sel = jnp.zeros(lg.shape, jnp.bool_)
    for _ in range(TOP_K):
        hit, _m = _first_max(x, eid, (0, 1), N_EXPERTS)
        sel = sel | hit
        x = jnp.where(hit, -jnp.inf, x)
    w = jnp.where(sel, scores, 0.0)
    w = w / jnp.sum(w, axis=(0, 1), keepdims=True) * ROUTED_SCALE
    w_ref[...] = w
    selb = sel.astype(BF16).reshape(N_EXPERTS, tn)
    pos = _dot(selb, tri_ref[...])
    pos_ref[...] = jnp.where(sel, pos.reshape(lg.shape), -1.0)
    cnt = jnp.sum(sel.astype(F32), axis=2, keepdims=True)
    cnt_ref[0] = jnp.broadcast_to(cnt, cnt_ref.shape[1:]).astype(jnp.int32)


def route(logits_t, e_bias, tn):
    n = logits_t.shape[1]
    lg3 = logits_t.reshape(N_EXPERT_GROUPS, EXPERTS_PER_GROUP, n)
    eb = e_bias.astype(F32).reshape(N_EXPERT_GROUPS, EXPERTS_PER_GROUP, 1)
    tri = jnp.asarray(np.triu(np.ones((tn, tn), np.float32), 1), BF16)
    blk = pl.BlockSpec((N_EXPERT_GROUPS, EXPERTS_PER_GROUP, tn), lambda i: (0, 0, i))
    w, pos, cnt = pl.pallas_call(
        _route_kernel,
        grid=(n // tn,),
        in_specs=[blk, pl.BlockSpec(eb.shape, lambda i: (0, 0, 0)), pl.BlockSpec((tn, tn), lambda i: (0, 0))],
        out_specs=(blk, blk, pl.BlockSpec((1, N_EXPERT_GROUPS, EXPERTS_PER_GROUP, LANES), lambda i: (i, 0, 0, 0))),
        out_shape=(jax.ShapeDtypeStruct(lg3.shape, F32), jax.ShapeDtypeStruct(lg3.shape, F32),
                   jax.ShapeDtypeStruct((n // tn, N_EXPERT_GROUPS, EXPERTS_PER_GROUP, LANES), jnp.int32)),
        compiler_params=_cparams(("arbitrary",)),
        name="route",
    )(lg3, eb, tri)
    return w.reshape(N_EXPERTS, n), pos.reshape(N_EXPERTS, n), cnt[..., 0].reshape(n // tn, N_EXPERTS)


MOE_ROWS = 128


def _swiglu(xb, wgu, wd, width):
    gu = _dot(xb, wgu)
    act = _silu(gu[:, :width]) * gu[:, width:]
    return _dot(act.astype(BF16), wd)


MOE_EXPERTS_PER_STEP = 4


MOE_ALIGN = 16
MOE_GATHER_ROWS = 896


def _moe_slots(tm):
    worst = TOP_K * tm + N_EXPERTS * (MOE_ALIGN - 1) + MOE_ROWS
    return -(-worst // MOE_GATHER_ROWS) * MOE_GATHER_ROWS


def _moe_kernel(cnt_ref, start_ref, h2_ref, w_ref, pos_ref, x1_ref, g2_ref, wgu_ref, wd_ref, sgu_ref, sd_ref,
                o_ref, g_all, xs):
    i = pl.program_id(0)
    es = pl.program_id(1)
    tm = h2_ref.shape[0]
    slots = g_all.shape[0]
    slot = lax.broadcasted_iota(jnp.int32, (MOE_ROWS, tm), 0).astype(F32)
    row = lax.broadcasted_iota(jnp.int32, (MOE_ROWS, 1), 0)

    def n_windows(cnt):
        return (cnt + MOE_ROWS - 1) // MOE_ROWS

    def window_start(e, j):
        return pl.multiple_of(start_ref[i * N_EXPERTS + e] + j * MOE_ROWS, MOE_ALIGN)

    @pl.when(es == 0)
    def _():
        g_all[...] = jnp.zeros(g_all.shape, BF16)

        def mark(e, carry):
            pos = pos_ref[pl.ds(e, 1), :]

            def mark_window(j, carry):
                hit = pos == slot + (j * MOE_ROWS).astype(F32)
                g_all[pl.ds(window_start(e, j), MOE_ROWS), :] = hit.astype(BF16)
                return carry

            return lax.fori_loop(0, n_windows(cnt_ref[i * N_EXPERTS + e]), mark_window, carry)

        lax.fori_loop(0, N_EXPERTS, mark, 0)

        def gather(c, carry):
            r0 = pl.multiple_of(c * MOE_GATHER_ROWS, MOE_GATHER_ROWS)
            rows = _dot(g_all[pl.ds(r0, MOE_GATHER_ROWS), :], h2_ref[...])
            xs[pl.ds(r0, MOE_GATHER_ROWS), :] = rows.astype(BF16)
            return carry

        lax.fori_loop(0, slots // MOE_GATHER_ROWS, gather, 0)

    for q in range(MOE_EXPERTS_PER_STEP):
        e = es * MOE_EXPERTS_PER_STEP + q
        cnt = cnt_ref[i * N_EXPERTS + e]
        wrow = w_ref[pl.ds(e, 1), :]

        def window(j, carry, q=q, e=e, cnt=cnt, wrow=wrow):
            r0 = window_start(e, j)
            xg = xs[pl.ds(r0, MOE_ROWS), :]
            out = _swiglu(xg, wgu_ref[q].astype(BF16), wd_ref[q].astype(BF16), D_EXPERT)
            g = g_all[pl.ds(r0, MOE_ROWS), :].astype(F32)
            out = out * jnp.sum(g * wrow, axis=1, keepdims=True)
            mine = row < cnt - j * MOE_ROWS
            xs[pl.ds(r0, MOE_ROWS), :] = jnp.where(mine, out.astype(BF16), xg)
            return carry

        lax.fori_loop(0, n_windows(cnt), window, 0)

    @pl.when(es == pl.num_programs(1) - 1)
    def _():
        y = lax.dot_general(g_all[...], xs[...], (((0,), (0,)), ((), ())), preferred_element_type=F32)
        y = y + _swiglu(h2_ref[...], sgu_ref[...], sd_ref[...], D_SHARED)
        o_ref[...] = x1_ref[...] + _mod(g2_ref) * y


def moe(h2, w_t, pos_t, counts, x1, mod3, mod_row0, t_per_b, wgu, wd, sgu, sd, tm):
    n, d = h2.shape
    tiles_per_b = t_per_b // tm
    eps = MOE_EXPERTS_PER_STEP
    slots = _moe_slots(tm)
    padded = (counts + MOE_ALIGN - 1) // MOE_ALIGN * MOE_ALIGN
    starts = jnp.cumsum(padded, axis=1) - padded
    grid_spec = pltpu.PrefetchScalarGridSpec(
        num_scalar_prefetch=2,
        grid=(n // tm, N_EXPERTS // eps),
        in_specs=[
            pl.BlockSpec((tm, d), lambda i, e, *_: (i, 0)),
            pl.BlockSpec((N_EXPERTS, tm), lambda i, e, *_: (0, i)),
            pl.BlockSpec((N_EXPERTS, tm), lambda i, e, *_: (0, i)),
            pl.BlockSpec((tm, d), lambda i, e, *_: (i, 0)),
            _mod_spec(mod3, 5, tm, tiles_per_b, mod_row0),
            pl.BlockSpec((eps, d, 2 * D_EXPERT), lambda i, e, *_: (e, 0, 0)),
            pl.BlockSpec((eps, D_EXPERT, d), lambda i, e, *_: (e, 0, 0)),
            pl.BlockSpec(sgu.shape, lambda i, e, *_: (0, 0)),
            pl.BlockSpec(sd.shape, lambda i, e, *_: (0, 0)),
        ],
        out_specs=pl.BlockSpec((tm, d), lambda i, e, *_: (i, 0)),
        scratch_shapes=[pltpu.VMEM((slots, tm), BF16), pltpu.VMEM((slots, d), BF16)],
    )
    return pl.pallas_call(
        _moe_kernel,
        grid_spec=grid_spec,
        out_shape=jax.ShapeDtypeStruct((n, d), F32),
        compiler_params=_cparams(("arbitrary", "arbitrary")),
        name="moe",
    )(counts.reshape(-1), starts.reshape(-1).astype(jnp.int32), h2, w_t, pos_t, x1, mod3, wgu, wd, sgu, sd)


SC_WINDOW = 128
PACK_W = 256
MOE_BLOCK_ROWS = 512
HI_MASK = -65536


def _pack_pair(x):
    bits = pltpu.bitcast(x.astype(BF16).astype(F32), jnp.int32)
    return lax.shift_right_logical(bits[:, :PACK_W], 16) | (bits[:, PACK_W:] & HI_MASK)


def _unpack_pair(word):
    lo = pltpu.bitcast(lax.shift_left(word, 16), F32)
    hi = pltpu.bitcast(word & HI_MASK, F32)
    return jnp.concatenate([lo, hi], axis=1)


def _pack_kernel(x_ref, a_ref, b_ref):
    x = x_ref[...]
    a_ref[...] = _pack_pair(x[:, :2 * PACK_W])
    b_ref[...] = _pack_pair(x[:, 2 * PACK_W:])


def pack_rows(x, tm):
    n, d = x.shape
    tok = lambda wd: pl.BlockSpec((tm, wd), lambda i: (i, 0))
    return pl.pallas_call(
        _pack_kernel, grid=(n // tm,), in_specs=[tok(d)], out_specs=(tok(PACK_W), tok(PACK_W)),
        out_shape=(jax.ShapeDtypeStruct((n, PACK_W), jnp.int32),) * 2,
        compiler_params=_cparams(("arbitrary",)), name="pack_rows",
    )(x)


def _slots_kernel(w_ref, pos_ref, base_ref, tri_ref, slot_ref, wt_ref):
    w = w_ref[...]
    pos = pos_ref[...]
    sel = pos >= 0.0
    rank = _dot(tri_ref[...], sel.astype(BF16))
    dest = base_ref[0] + pos
    slots, wts = [], []
    for j in range(TOP_K):
        mine = sel & (rank == float(j))
        slots.append(jnp.sum(jnp.where(mine, dest, 0.0), axis=0, keepdims=True))
        wts.append(jnp.sum(jnp.where(mine, w, 0.0), axis=0, keepdims=True))
    slot_ref[...] = jnp.concatenate(slots, axis=0).astype(jnp.int32)
    wpad = jnp.concatenate(wts + [jnp.zeros((LANES - TOP_K, w.shape[1]), F32)], axis=0)
    wt_ref[...] = jnp.transpose(wpad)


def slots_of(w_t, pos_t, base, tn):
    n = w_t.shape[1]
    tri = jnp.asarray(np.tril(np.ones((N_EXPERTS, N_EXPERTS), np.float32), -1), BF16)
    blk = pl.BlockSpec((N_EXPERTS, tn), lambda i: (0, i))
    return pl.pallas_call(
        _slots_kernel, grid=(n // tn,),
        in_specs=[blk, blk, pl.BlockSpec((1, N_EXPERTS, 1), lambda i: (i, 0, 0)),
                  pl.BlockSpec((N_EXPERTS, N_EXPERTS), lambda i: (0, 0))],
        out_specs=(pl.BlockSpec((TOP_K, tn), lambda i: (0, i)), pl.BlockSpec((tn, LANES), lambda i: (i, 0))),
        out_shape=(jax.ShapeDtypeStruct((TOP_K, n), jnp.int32), jax.ShapeDtypeStruct((n, LANES), F32)),
        compiler_params=_cparams(("arbitrary",)), name="moe_slots",
    )(w_t, pos_t, base, tri)


def sc_scatter_rows(rows, idx, n_out):
    n, d = rows.shape
    m = idx.shape[0]
    nb = n // SC_WINDOW
    mesh = plsc.VectorSubcoreMesh(core_axis_name="core", subcore_axis_name="subcore")

    @functools.partial(pl.kernel, out_type=jax.ShapeDtypeStruct((n_out, d), rows.dtype), mesh=mesh)
    def scatter(x_hbm, i_hbm, o_hbm):
        def body(x_vmem, i_vmem):
            pltpu.sync_copy(x_vmem, o_hbm.at[i_vmem.at[0]])

        pltpu.emit_pipeline(
            body, grid=(m // SC_WINDOW,),
            in_specs=[pl.BlockSpec((SC_WINDOW, d), index_map=lambda i: (i % nb, 0)),
                      pl.BlockSpec((1, SC_WINDOW), index_map=lambda i: (0, i))],
            out_specs=[], core_axis_name=("core", "subcore"), dimension_semantics=(pltpu.PARALLEL,),
        )(x_hbm, i_hbm)

    return scatter(rows, idx.reshape(1, m))


def sc_gather_rows(table, idx):
    d = table.shape[1]
    m = idx.shape[0]
    mesh = plsc.VectorSubcoreMesh(core_axis_name="core", subcore_axis_name="subcore")

    @functools.partial(pl.kernel, out_type=jax.ShapeDtypeStruct((m, d), table.dtype), mesh=mesh)
    def gather(x_hbm, i_hbm, o_hbm):
        def body(i_vmem, o_vmem):
            pltpu.sync_copy(x_hbm.at[i_vmem.at[0]], o_vmem)

        pltpu.emit_pipeline(
            body, grid=(m // SC_WINDOW,),
            in_specs=[pl.BlockSpec((1, SC_WINDOW), index_map=lambda i: (0, i))],
            out_specs=[pl.BlockSpec((SC_WINDOW, d), index_map=lambda i: (i, 0))],
            core_axis_name=("core", "subcore"), dimension_semantics=(pltpu.PARALLEL,),
        )(i_hbm, o_hbm)

    return gather(table, idx.reshape(1, m))


def _experts_kernel(be_ref, nu_ref, xa_ref, xb_ref, wgu_ref, wd_ref, oa_ref, ob_ref, wgu_bf, wd_bf):
    b = pl.program_id(0)

    @pl.when(b < nu_ref[0])
    def _():
        @pl.when((b == 0) | (be_ref[b] != be_ref[jnp.maximum(b - 1, 0)]))
        def _():
            wgu_bf[...] = wgu_ref[0].astype(BF16)
            wd_bf[...] = wd_ref[0].astype(BF16)

        x = jnp.concatenate([_unpack_pair(xa_ref[...]), _unpack_pair(xb_ref[...])], axis=1).astype(BF16)
        out = _swiglu(x, wgu_bf[...], wd_bf[...], D_EXPERT)
        oa_ref[...] = _pack_pair(out[:, :2 * PACK_W])
        ob_ref[...] = _pack_pair(out[:, 2 * PACK_W:])


def experts_sorted(xa, xb, block_expert, n_used, wgu, wd):
    r = xa.shape[0]
    d = wd.shape[2]
    row = lambda b, be, nu: (jnp.minimum(b, nu[0] - 1), 0)
    blk = pl.BlockSpec((MOE_BLOCK_ROWS, PACK_W), row)
    grid_spec = pltpu.PrefetchScalarGridSpec(
        num_scalar_prefetch=2, grid=(r // MOE_BLOCK_ROWS,),
        in_specs=[blk, blk,
                  pl.BlockSpec((1, d, 2 * D_EXPERT), lambda b, be, nu: (be[b], 0, 0)),
                  pl.BlockSpec((1, D_EXPERT, d), lambda b, be, nu: (be[b], 0, 0))],
        out_specs=(blk, blk),
        scratch_shapes=[pltpu.VMEM((d, 2 * D_EXPERT), BF16), pltpu.VMEM((D_EXPERT, d), BF16)],
    )
    return pl.pallas_call(
        _experts_kernel, grid_spec=grid_spec,
        out_shape=(jax.ShapeDtypeStruct((r, PACK_W), jnp.int32),) * 2,
        compiler_params=_cparams(("arbitrary",)), name="moe_experts",
    )(block_expert, n_used, xa, xb, wgu, wd)


def _combine_kernel(ya_ref, yb_ref, wt_ref, h2_ref, x1_ref, g2_ref, sgu_ref, sd_ref, o_ref):
    wt = wt_ref[...]
    acc = _swiglu(h2_ref[...], sgu_ref[...], sd_ref[...], D_SHARED)
    for j in range(TOP_K):
        y = jnp.concatenate([_unpack_pair(ya_ref[j]), _unpack_pair(yb_ref[j])], axis=1)
        acc = acc + wt[:, j:j + 1] * y
    o_ref[...] = x1_ref[...] + _mod(g2_ref) * acc


def combine_sorted(ya, yb, wt, h2, x1, mod3, mod_row0, t_per_b, sgu, sd, tm):
    n, d = h2.shape
    tiles_per_b = t_per_b // tm
    tok = lambda wd: pl.BlockSpec((tm, wd), lambda i: (i, 0))
    yblk = pl.BlockSpec((TOP_K, tm, PACK_W), lambda i: (0, i, 0))
    full = lambda a: pl.BlockSpec(a.shape, lambda i: (0,) * a.ndim)
    return pl.pallas_call(
        _combine_kernel, grid=(n // tm,),
        in_specs=[yblk, yblk, tok(LANES), tok(d), tok(d), _mod_spec(mod3, 5, tm, tiles_per_b, mod_row0),
                  full(sgu), full(sd)],
        out_specs=tok(d), out_shape=jax.ShapeDtypeStruct((n, d), F32),
        compiler_params=_cparams(("arbitrary",)), name="moe_combine",
    )(ya, yb, wt, h2, x1, mod3, sgu, sd)


def moe_sorted(h2, w_t, pos_t, counts, x1, mod3, mod_row0, t_per_b, wgu, wd, sgu, sd, tm):
    n, d = h2.shape
    assert d == 4 * PACK_W and n % SC_WINDOW == 0
    n_blocks = (TOP_K * n + N_EXPERTS * (MOE_BLOCK_ROWS - 1)) // MOE_BLOCK_ROWS
    total = jnp.sum(counts, axis=0)
    region = (total + MOE_BLOCK_ROWS - 1) // MOE_BLOCK_ROWS * MOE_BLOCK_ROWS
    region_end = jnp.cumsum(region)
    base = (region_end - region)[None, :] + jnp.cumsum(counts, axis=0) - counts
    block_row0 = jnp.arange(n_blocks, dtype=region_end.dtype) * MOE_BLOCK_ROWS
    block_expert = jnp.sum(region_end[None, :] <= block_row0[:, None], axis=1)
    block_expert = jnp.minimum(block_expert, N_EXPERTS - 1).astype(jnp.int32)
    n_used = (region_end[-1:] // MOE_BLOCK_ROWS).astype(jnp.int32)
    slot, wt = slots_of(w_t, pos_t, base.astype(F32).reshape(-1, N_EXPERTS, 1), tm)
    dest = slot.reshape(-1)
    ha, hb = pack_rows(h2, tm)
    rows = n_blocks * MOE_BLOCK_ROWS
    xa, xb = sc_scatter_rows(ha, dest, rows), sc_scatter_rows(hb, dest, rows)
    oa, ob = experts_sorted(xa, xb, block_expert, n_used, wgu, wd)
    ya = sc_gather_rows(oa, dest).reshape(TOP_K, n, PACK_W)
    yb = sc_gather_rows(ob, dest).reshape(TOP_K, n, PACK_W)
    return combine_sorted(ya, yb, wt, h2, x1, mod3, mod_row0, t_per_b, sgu, sd, tm)


GATHER_PAGES = 8


def _gather_kernel(pt_ref, *refs):
    pages, new_ref = refs[:GATHER_PAGES], refs[GATHER_PAGES]
    rows_ref, cmpx_ref, stage_ref = refs[GATHER_PAGES + 1:]
    step = pl.program_id(1)
    last = pl.num_programs(1) - 1
    n_rows = GATHER_PAGES * PAGE_SIZE

    @pl.when(step < last)
    def _():
        for k in range(GATHER_PAGES):
            sl = slice(k * PAGE_SIZE, (k + 1) * PAGE_SIZE)
            for r in range(4):
                tile = jnp.transpose(pages[k][0, r])
                if r < 2:
                    stage_ref[r, sl, :] = tile
                else:
                    rows_ref[0, sl, (r - 2) * KV_WIDTH:(r - 1) * KV_WIDTH] = tile.astype(BF16)

    @pl.when(step == last)
    def _():
        new = new_ref[0]
        tn = new.shape[0]
        stage_ref[...] = jnp.zeros(stage_ref.shape, F32)
        for s in range(2):
            stage_ref[s, 0:tn, :] = new[:, s * KV_WIDTH:(s + 1) * KV_WIDTH]
        pad = jnp.zeros((n_rows - tn, 2 * KV_WIDTH), F32)
        rows_ref[0] = jnp.concatenate([new[:, 2 * KV_WIDTH:], pad], axis=0).astype(BF16)

    _stride_block_store(stage_ref, cmpx_ref, n_rows)


def gather_pages(cache_t, page_table, new_rows):
    b, n_pages = page_table.shape
    steps = n_pages // GATHER_PAGES
    rows = GATHER_PAGES * PAGE_SIZE
    s_out = (steps + 1) * rows

    def page_spec(k):
        def idx(i, s, pt):
            p = jnp.minimum(s, steps - 1) * GATHER_PAGES + k
            return (pt[i * n_pages + p], 0, 0, 0)
        return pl.BlockSpec((1, 4, KV_WIDTH, PAGE_SIZE), idx)

    grid_spec = pltpu.PrefetchScalarGridSpec(
        num_scalar_prefetch=1,
        grid=(b, steps + 1),
        in_specs=[page_spec(k) for k in range(GATHER_PAGES)]
        + [pl.BlockSpec((1,) + new_rows.shape[1:], lambda i, s, pt: (i, 0, 0))],
        out_specs=(
            pl.BlockSpec((1, rows, 2 * KV_WIDTH), lambda i, s, pt: (i, s, 0)),
            pl.BlockSpec((1, rows // CMP_STRIDE, CMP_STRIDE * 2 * KV_WIDTH), lambda i, s, pt: (i, s, 0)),
        ),
        scratch_shapes=[pltpu.VMEM((2, rows, KV_WIDTH), F32)],
    )
    return pl.pallas_call(
        _gather_kernel,
        grid_spec=grid_spec,
        out_shape=(jax.ShapeDtypeStruct((b, s_out, 2 * KV_WIDTH), BF16),
                   jax.ShapeDtypeStruct((b, s_out // CMP_STRIDE, CMP_STRIDE * 2 * KV_WIDTH), BF16)),
        compiler_params=_cparams(("arbitrary", "arbitrary")),
        name="gather_pages",
    )(page_table.reshape(-1), *([cache_t] * GATHER_PAGES), new_rows)


def _attention(qp, cmpx, kvb, sel_col, winb, misc, cmp_w, q_off, win_pos0, tq):
    t = qp.shape[1]
    cur_lo, cur_hi = q_off // SEL_BLOCK, (q_off + t - 1) // SEL_BLOCK
    assert cur_hi < N_SEL_LANES or (cur_lo == cur_hi == N_SEL_LANES), (q_off, t)
    n_pick = N_SEL - (1 if cur_hi >= N_SEL_LANES else 0)
    kcv = compress(cmpx, *cmp_w)
    o_cmp, mneg = cmp_select(qp, kcv, q_off, n_pick, tq)
    return sel_win_attention(qp, mneg, kvb, sel_col, winb, o_cmp, misc, q_off, win_pos0, tq)


def kernel(x_prompt, x_sample, cache_kv, cache_win, state_ssm, state_conv, page_table, c_prompt, c_sample, w_ada, b_ada, norm1_w, norm2_w, w_in, q_norm_w, k_norm_w, cmp_pe, cmp_w1, cmp_w2, attn_out_norm_w, conv_w, conv_b, dt_bias, a_log, d_skip, ssm_norm_w, w_out, w_router, e_bias, w_exp_gu, w_exp_down, w_sh_gu, w_sh_down):
    xp, xq = x_prompt, x_sample
    bp, tp, d = xp.shape
    bq, tq, _ = xq.shape
    depth = w_ada.shape[0]
    past_len = page_table.shape[1] * PAGE_SIZE
    nq = bq * tq
    tq_pad = LANES // GQA_GROUP
    assert tp % 512 == 0 and tp >= WINDOW and nq % 8 == 0 and tq <= tq_pad
    pos_p = jnp.arange(tp, dtype=jnp.int32)
    pos_q = jnp.tile(past_len + jnp.arange(tq, dtype=jnp.int32), bq)
    c_all = jnp.concatenate([c_prompt, c_sample], axis=0)
    c_all = jnp.pad(c_all, ((0, -c_all.shape[0] % 8), (0, 0)))
    outs = [[] for _ in range(8)]
    for l in range(depth):
        mod = adaln_all(c_all, w_ada[l], b_ada[l])
        mod_p = mod.reshape(mod.shape[0], 1, 6 * d)
        mod_q = jnp.repeat(mod[bp:bp + bq], tq, axis=0)
        wp = _prep_w_in(w_in[l])
        cmp_w = _prep_compress(cmp_pe[l], cmp_w1[l], cmp_w2[l])
        wo = w_out[l].astype(BF16)
        wgu, wd = w_exp_gu[l], w_exp_down[l]
        sgu, sd = w_sh_gu[l].astype(BF16), w_sh_down[l].astype(BF16)
        ssm_w = (conv_w[l], conv_b[l], dt_bias[l], a_log[l], d_skip[l], ssm_norm_w[l])
        n1w, n2w, anw = norm1_w[l:l + 1], norm2_w[l:l + 1], attn_out_norm_w[l:l + 1]

        qp, kvb, win, winb, z, xbc, misc, kvt, cmpx = inproj(xp, mod_p, 0, n1w, wp, q_norm_w[l], k_norm_w[l], pos_p,
                                                            512, True)
        r3 = lambda a: a.reshape(bp, tp, a.shape[-1])
        o_attn = _attention(r3(qp), cmpx, r3(kvb), 2, r3(winb), r3(misc), cmp_w, 0, 0, 128)
        y_ssm, h_new, conv_new = ssd(r3(xbc), r3(z), r3(misc), jnp.zeros((bp, CONV_WIDTH - 1, CONV_DIM), F32),
                                     jnp.zeros((bp, SSM_HEADS, SSM_HEAD_DIM, SSM_STATE), F32), *ssm_w)
        x1, h2, lg = merge(o_attn, y_ssm, xp, mod_p, 0, anw, wo, n2w, w_router[l], 512)
        w_t, pos_t, cnt = route(lg, e_bias[l], 512)
        xp = moe_sorted(h2, w_t, pos_t, cnt, x1, mod_p, 0, tp, wgu, wd, sgu, sd, 512).reshape(bp, tp, d)
        outs[0].append(jnp.transpose(kvt.reshape(bp, 4, N_KV_HEADS, HEAD_DIM, tp), (0, 4, 1, 2, 3)))
        outs[1].append(win.reshape(bp, tp, 2, N_KV_HEADS, HEAD_DIM)[:, tp - WINDOW:])
        outs[2].append(h_new)
        outs[3].append(conv_new)

        xq1 = xq.reshape(1, nq, d)
        qp, kvb, win, winb, z, xbc, misc, kv = inproj(xq1, mod_q, 0, n1w, wp, q_norm_w[l], k_norm_w[l], pos_q, nq,
                                                      False)
        rq = lambda a: a.reshape(bq, tq, a.shape[-1])
        padq = lambda a: jnp.pad(rq(a), ((0, 0), (0, tq_pad - tq), (0, 0)))
        cache_t = jnp.transpose(cache_kv[l], (0, 2, 3, 4, 1)).reshape(cache_kv.shape[1], 4, KV_WIDTH, PAGE_SIZE)
        past, cmpx = gather_pages(cache_t, page_table, rq(kv))
        win_all = jnp.concatenate([cache_win[l].reshape(bq, WINDOW, 2 * KV_WIDTH).astype(BF16), rq(winb),
                                   jnp.zeros((bq, -(WINDOW + tq_pad) % WIN_CHUNK + tq_pad - tq, 2 * KV_WIDTH), BF16)],
                                  axis=1)
        o_attn = _attention(padq(qp), cmpx, past, 0, win_all, padq(misc), cmp_w, past_len, past_len - WINDOW,
                            tq_pad)[:, :tq]
        y_ssm, h_new, conv_new = ssd(rq(xbc), rq(z), rq(misc), state_conv[l], state_ssm[l], *ssm_w)
        x1, h2, lg = merge(o_attn.reshape(1, nq, ATTN_WIDTH), y_ssm.reshape(1, nq, SSM_WIDTH), xq1, mod_q, 0,
                           anw, wo, n2w, w_router[l], nq)
        w_t, pos_t, cnt = route(lg, e_bias[l], nq)
        xq = moe(h2, w_t, pos_t, cnt, x1, mod_q, 0, nq, wgu, wd, sgu, sd, nq).reshape(bq, tq, d)
        win_rows = win.reshape(bq, tq, 2, N_KV_HEADS, HEAD_DIM)
        outs[4].append(kv.reshape(bq, tq, 4, N_KV_HEADS, HEAD_DIM))
        outs[5].append(jnp.concatenate([cache_win[l], win_rows.astype(cache_win.dtype)], axis=1)[:, tq:])
        outs[6].append(h_new)
        outs[7].append(conv_new)
    return (xp, xq) + tuple(jnp.stack(o) for o in outs)
```

```python
import functools
import math

import jax
import jax.numpy as jnp
import numpy as np
from jax import lax
from jax.experimental import pallas as pl
from jax.experimental.pallas import tpu as pltpu
from jax.experimental.pallas import tpu_sc as plsc

D_MODEL = 1024
PAGE_SIZE = 128
HEAD_DIM = 64
N_Q_HEADS = 8
N_KV_HEADS = 2
GQA_GROUP = N_Q_HEADS // N_KV_HEADS
ATTN_WIDTH = N_Q_HEADS * HEAD_DIM
KV_WIDTH = N_KV_HEADS * HEAD_DIM
ROPE_DIM = HEAD_DIM // 4
ROPE_THETA = 500000.0
CMP_LEN = 32
CMP_STRIDE = 16
CMP_HIDDEN = 4 * HEAD_DIM
SEL_BLOCK = 64
N_SEL = 16
N_LOCAL = 2
WINDOW = 512
SSM_HEADS = 8
SSM_HEAD_DIM = 64
SSM_WIDTH = SSM_HEADS * SSM_HEAD_DIM
SSM_GROUPS = 2
SSM_STATE = 128
CONV_WIDTH = 4
CONV_DIM = SSM_WIDTH + 2 * SSM_GROUPS * SSM_STATE
SSD_CHUNK = 128
MIX_WIDTH = ATTN_WIDTH + SSM_WIDTH
N_EXPERTS = 64
N_EXPERT_GROUPS = 8
TOPK_GROUPS = 4
TOP_K = 8
D_EXPERT = 256
D_SHARED = 256
ROUTED_SCALE = 2.5
IN_SIZES = (ATTN_WIDTH, 6 * KV_WIDTH, 3 * N_Q_HEADS, SSM_WIDTH, CONV_DIM, SSM_HEADS)
N_IN = sum(IN_SIZES)
EPS = 1e-6
NEG = -1e30
BIG = 1e6

LANES = 128
VMEM_LIMIT = 56 * 1024 * 1024

BF16 = jnp.bfloat16
F32 = jnp.float32
LOG2E = math.log2(math.e)


def _cparams(sem, flags=None):
    return pltpu.CompilerParams(dimension_semantics=sem, vmem_limit_bytes=VMEM_LIMIT, flags=flags)


def _silu(x):
    return x * jax.nn.sigmoid(x)


def _dot(a, b):
    return jnp.dot(a, b, preferred_element_type=F32)


def _dot_nt(a, b):
    return lax.dot_general(a, b, (((1,), (1,)), ((), ())), preferred_element_type=F32)


def _mod_spec(mod, col, tm, tiles_per_b, row0):
    if mod.ndim == 3:
        return pl.BlockSpec((1, 1, D_MODEL), lambda i, *_: (row0 + i // tiles_per_b, 0, col))
    return pl.BlockSpec((tm, D_MODEL), lambda i, *_: (i, col))


def _mod(ref):
    return ref[0] if len(ref.shape) == 3 else ref[...]


def _adaln_kernel(c_ref, w_ref, b_ref, o_ref):
    c = c_ref[...]
    a = _silu(c).astype(BF16)
    o_ref[...] = _dot(a, w_ref[...].astype(BF16)) + b_ref[...]


def adaln_all(c_all, w_ada, b_ada):
    rows = c_all.shape[0]
    n = w_ada.shape[1]
    tn = 1024
    return pl.pallas_call(
        _adaln_kernel,
        grid=(n // tn,),
        in_specs=[
            pl.BlockSpec((rows, D_MODEL), lambda j: (0, 0)),
            pl.BlockSpec((D_MODEL, tn), lambda j: (0, j)),
            pl.BlockSpec((1, tn), lambda j: (0, j)),
        ],
        out_specs=pl.BlockSpec((rows, tn), lambda j: (0, j)),
        out_shape=jax.ShapeDtypeStruct((rows, n), F32),
        compiler_params=_cparams(("arbitrary",)),
        name="adaln",
    )(c_all, w_ada, b_ada.reshape(1, n))


_C_Q = 0
_C_KV = _C_Q + ATTN_WIDTH
_C_Z = _C_KV + 6 * KV_WIDTH
_C_XBC = _C_Z + SSM_WIDTH
_C_MISC = _C_XBC + CONV_DIM
N_IN_PAD = _C_MISC + LANES
N_GATES = 3 * N_Q_HEADS


def _prep_w_in(w_in):
    s = np.cumsum((0,) + IN_SIZES)
    q, kv, g, z, xbc, dt = (w_in[:, int(s[i]):int(s[i + 1])] for i in range(6))
    pad = jnp.zeros((w_in.shape[0], LANES - N_GATES - SSM_HEADS), w_in.dtype)
    return jnp.concatenate([q, kv, z, xbc, dt, g, pad], axis=1).astype(BF16)


def _group_mean_matrix(width):
    i = np.arange(width)
    m = (i[:, None] // HEAD_DIM == i[None, :] // HEAD_DIM).astype(np.float32) / HEAD_DIM
    return jnp.asarray(m, BF16)


def _rope_tables(pos):
    half = ROPE_DIM // 2
    inv_freq = ROPE_THETA ** (-jnp.arange(half, dtype=F32) / half)
    ang = pos.astype(F32)[:, None] * inv_freq[None, :]
    cos, sin = jnp.cos(ang), jnp.sin(ang)
    t = pos.shape[0]
    one = jnp.ones((t, HEAD_DIM - ROPE_DIM), F32)
    zero = jnp.zeros((t, HEAD_DIM - ROPE_DIM), F32)
    zh = jnp.zeros((t, half), F32)
    c = jnp.concatenate([cos, cos, one], axis=1)
    s_up = jnp.concatenate([-sin, zh, zero], axis=1)
    s_dn = jnp.concatenate([zh, sin, zero], axis=1)
    rep = LANES // HEAD_DIM
    return jnp.tile(c, (1, rep)), jnp.tile(s_up, (1, rep)), jnp.tile(s_dn, (1, rep))


def _rope(x, c, s_up, s_dn):
    w = x.shape[1]
    half = ROPE_DIM // 2
    rep = w // LANES
    ct = jnp.concatenate([c] * rep, axis=1) if rep > 1 else c
    su = jnp.concatenate([s_up] * rep, axis=1) if rep > 1 else s_up
    sd = jnp.concatenate([s_dn] * rep, axis=1) if rep > 1 else s_dn
    up = pltpu.roll(x, w - half, axis=1)
    dn = pltpu.roll(x, half, axis=1)
    return x * ct + up * su + dn * sd


def _stride_block_store(stage_ref, cmpx_ref, n_rows):
    for j in range(CMP_STRIDE):
        for s in range(2):
            rows_j = stage_ref[s, pl.ds(j, n_rows // CMP_STRIDE, stride=CMP_STRIDE), :]
            c0 = (2 * j + s) * KV_WIDTH
            cmpx_ref[0, :, c0:c0 + KV_WIDTH] = rows_j.astype(BF16)


def _inproj_kernel(x_ref, shift_ref, scale_ref, nw_ref, w_ref, qw_ref, kw_ref, gq_ref, gk_ref,
                   c_ref, su_ref, sd_ref,
                   qp_ref, kvb_ref, win_ref, winb_ref, z_ref, xbc_ref, misc_ref, *rest, seq_layout):
    x = x_ref[...]
    ms = jnp.mean(x * x, axis=-1, keepdims=True)
    h = x * lax.rsqrt(ms + EPS) * nw_ref[...]
    h = h * (1.0 + _mod(scale_ref)) + _mod(shift_ref)
    hb = h.astype(BF16)
    c, su, sd = c_ref[...], su_ref[...], sd_ref[...]

    q = _dot(hb, w_ref[:, _C_Q:_C_Q + ATTN_WIDTH])
    qms = _dot((q * q).astype(BF16), gq_ref[...])
    q = q * lax.rsqrt(qms + EPS) * qw_ref[...]
    q = _rope(q, c, su, sd) * (HEAD_DIM ** -0.5 * LOG2E)
    lane = lax.broadcasted_iota(jnp.int32, q.shape, 1) % LANES
    lo = lane < HEAD_DIM
    q_up = pltpu.roll(q, ATTN_WIDTH - HEAD_DIM, axis=1)
    q_dn = pltpu.roll(q, HEAD_DIM, axis=1)
    zero = jnp.zeros_like(q)
    nat_lo = jnp.where(lo, q, zero)
    nat_hi = jnp.where(lo, zero, q)
    up_lo = jnp.where(lo, q_up, zero)
    dn_hi = jnp.where(lo, zero, q_dn)
    blocks = []
    for hd in range(N_Q_HEADS):
        pair = hd // 2
        sl = slice(pair * LANES, (pair + 1) * LANES)
        if hd < GQA_GROUP:
            blocks.append((nat_lo if hd % 2 == 0 else up_lo)[:, sl])
        else:
            blocks.append((dn_hi if hd % 2 == 0 else nat_hi)[:, sl])
    qp_ref[...] = jnp.concatenate(blocks, axis=1).astype(BF16)

    kv = _dot(hb, w_ref[:, _C_KV:_C_KV + 6 * KV_WIDTH])
    outs = []
    for br in range(3):
        k = kv[:, br * 2 * KV_WIDTH:br * 2 * KV_WIDTH + KV_WIDTH]
        v = kv[:, br * 2 * KV_WIDTH + KV_WIDTH:(br + 1) * 2 * KV_WIDTH]
        kms = _dot((k * k).astype(BF16), gk_ref[...])
        k = k * lax.rsqrt(kms + EPS) * kw_ref[:, br * KV_WIDTH:(br + 1) * KV_WIDTH]
        k = _rope(k, c, su, sd)
        outs += [k, v]
    kvrows = jnp.concatenate(outs[:4], axis=1)
    winrows = jnp.concatenate(outs[4:], axis=1)
    kvb_ref[...] = kvrows.astype(BF16)
    win_ref[...] = winrows
    winb_ref[...] = winrows.astype(BF16)
    if seq_layout:
        kvt_ref, cmpx_ref, stage_ref = rest
        tm = kvrows.shape[0]
        for r in range(4):
            kvt_ref[0, r] = jnp.transpose(kvrows[:, r * KV_WIDTH:(r + 1) * KV_WIDTH])
        for s in range(2):
            stage_ref[s] = kvrows[:, s * KV_WIDTH:(s + 1) * KV_WIDTH]
        _stride_block_store(stage_ref, cmpx_ref, tm)
    else:
        rest[0][...] = kvrows

    z_ref[...] = _dot(hb, w_ref[:, _C_Z:_C_Z + SSM_WIDTH])
    xbc_ref[...] = _dot(hb, w_ref[:, _C_XBC:_C_XBC + CONV_DIM])
    misc_ref[...] = _dot(hb, w_ref[:, _C_MISC:_C_MISC + LANES])


def inproj(x, mod3, mod_row0, norm_w, wp, q_norm_w, k_norm_w, pos, tm, seq_layout):
    b, t, d = x.shape
    n = b * t
    tiles_per_b = t // tm
    xf = x.reshape(n, d)
    c, su, sd = _rope_tables(pos)
    qw = jnp.tile(q_norm_w, N_Q_HEADS).reshape(1, ATTN_WIDTH)
    kw = jnp.concatenate([jnp.tile(k_norm_w[i], N_KV_HEADS) for i in range(3)]).reshape(1, 3 * KV_WIDTH)
    gq = _group_mean_matrix(ATTN_WIDTH)
    gk = _group_mean_matrix(KV_WIDTH)

    def mod_spec(col):
        return _mod_spec(mod3, col, tm, tiles_per_b, mod_row0)

    def tok(wd):
        return pl.BlockSpec((tm, wd), lambda i: (i, 0))

    def full(a):
        return pl.BlockSpec(a.shape, lambda i: (0,) * a.ndim)

    rope_spec = pl.BlockSpec((tm, LANES), lambda i: (i % tiles_per_b, 0))
    out_shape = [
        jax.ShapeDtypeStruct((n, N_Q_HEADS * LANES), BF16),
        jax.ShapeDtypeStruct((n, 4 * KV_WIDTH), BF16),
        jax.ShapeDtypeStruct((n, 2 * KV_WIDTH), F32),
        jax.ShapeDtypeStruct((n, 2 * KV_WIDTH), BF16),
        jax.ShapeDtypeStruct((n, SSM_WIDTH), F32),
        jax.ShapeDtypeStruct((n, CONV_DIM), F32),
        jax.ShapeDtypeStruct((n, LANES), F32),
    ]
    out_specs = [tok(s.shape[1]) for s in out_shape]
    scratch = []
    if seq_layout:
        out_shape += [jax.ShapeDtypeStruct((b, 4, KV_WIDTH, t), F32),
                      jax.ShapeDtypeStruct((b, t // CMP_STRIDE, CMP_STRIDE * 2 * KV_WIDTH), BF16)]
        out_specs += [pl.BlockSpec((1, 4, KV_WIDTH, tm), lambda i: (i // tiles_per_b, 0, 0, i % tiles_per_b)),
                      pl.BlockSpec((1, tm // CMP_STRIDE, CMP_STRIDE * 2 * KV_WIDTH),
                                   lambda i: (i // tiles_per_b, i % tiles_per_b, 0))]
        scratch = [pltpu.VMEM((2, tm, KV_WIDTH), F32)]
    else:
        out_shape += [jax.ShapeDtypeStruct((n, 4 * KV_WIDTH), F32)]
        out_specs += [tok(4 * KV_WIDTH)]
    return pl.pallas_call(
        functools.partial(_inproj_kernel, seq_layout=seq_layout),
        grid=(n // tm,),
        in_specs=[tok(d), mod_spec(0), mod_spec(1), full(norm_w), full(wp), full(qw), full(kw), full(gq), full(gk),
                  rope_spec, rope_spec, rope_spec],
        out_specs=tuple(out_specs),
        out_shape=tuple(out_shape),
        scratch_shapes=scratch,
        compiler_params=_cparams(("arbitrary",)),
        name="inproj",
    )(xf, mod3, mod3, norm_w, wp, qw, kw, gq, gk, c, su, sd)


def _prep_compress(cmp_pe, cmp_w1, cmp_w2):
    half = CMP_LEN // 2
    eye = jnp.eye(N_KV_HEADS, dtype=F32)
    w1 = cmp_w1.reshape(2, CMP_LEN, HEAD_DIM, CMP_HIDDEN)
    w1s = []
    for part in (w1[:, :half], w1[:, half:]):
        w1s.append(jnp.einsum("pjdo,hg->pjhdgo", part, eye).reshape(2, half * KV_WIDTH, N_KV_HEADS * CMP_HIDDEN))
    w1p = jnp.concatenate(w1s, axis=2).astype(BF16)
    pe = cmp_pe.reshape(2, 2, half, 1, HEAD_DIM)
    pep = jnp.broadcast_to(pe, (2, 2, half, N_KV_HEADS, HEAD_DIM)).reshape(2, 2, half * KV_WIDTH)
    w2p = jnp.einsum("poe,hg->phoge", cmp_w2, eye).reshape(2, N_KV_HEADS * CMP_HIDDEN, KV_WIDTH).astype(BF16)
    return w1p, pep, w2p


def _compress_kernel(x_ref, w1_ref, pe_ref, w2_ref, o_ref, *, row_w):
    part = pl.program_id(1)
    nb = x_ref.shape[1]
    half = CMP_LEN // 2
    hid = N_KV_HEADS * CMP_HIDDEN
    cols = []
    for j in range(half):
        a = x_ref[0, :, j * row_w:j * row_w + KV_WIDTH]
        b = x_ref[0, :, j * row_w + KV_WIDTH:j * row_w + 2 * KV_WIDTH]
        cols.append(jnp.where(part == 0, a, b))
    x = jnp.concatenate(cols, axis=1).astype(F32)
    pe = pe_ref[0]
    u = _dot((x + pe[0:1]).astype(BF16), w1_ref[0, :, :hid])
    v = _dot((x + pe[1:2]).astype(BF16), w1_ref[0, :, hid:])
    h1 = u + pltpu.roll(v, nb - 1, axis=0)
    out = _dot(_silu(h1).astype(BF16), w2_ref[0])
    row = lax.broadcasted_iota(jnp.int32, out.shape, 0)
    o_ref[0, 0] = jnp.where(row < nb - 1, out, 0.0).astype(o_ref.dtype)


def compress(x, w1p, pep, w2p):
    b, nb, width = x.shape
    row_w = width // CMP_STRIDE
    return pl.pallas_call(
        functools.partial(_compress_kernel, row_w=row_w),
        grid=(b, 2),
        in_specs=[
            pl.BlockSpec((1, nb, CMP_STRIDE * row_w), lambda i, p: (i, 0, 0)),
            pl.BlockSpec((1,) + w1p.shape[1:], lambda i, p: (p, 0, 0)),
            pl.BlockSpec((1,) + pep.shape[1:], lambda i, p: (p, 0, 0)),
            pl.BlockSpec((1,) + w2p.shape[1:], lambda i, p: (p, 0, 0)),
        ],
        out_specs=pl.BlockSpec((1, 1, nb, KV_WIDTH), lambda i, p: (i, p, 0, 0)),
        out_shape=jax.ShapeDtypeStruct((b, 2, nb, KV_WIDTH), BF16),
        compiler_params=_cparams(("arbitrary", "arbitrary")),
        name="compress",
    )(x, w1p, pep, w2p)


N_SEL_LANES = LANES


def _cover_matrix(nb):
    c = np.arange(nb)[:, None]
    j = np.arange(N_SEL_LANES)[None, :]
    start = c * CMP_STRIDE
    m = (start < (j + 1) * SEL_BLOCK) & (start + CMP_LEN > j * SEL_BLOCK)
    return jnp.asarray(m.astype(np.float32), BF16)


def _place_heads(res, kv):
    lane = lax.broadcasted_iota(jnp.int32, res[0].shape, 1)
    lo = lane < HEAD_DIM
    blocks = []
    for pair in range(GQA_GROUP // 2):
        a, b = res[2 * pair], res[2 * pair + 1]
        if kv == 0:
            blocks.append(jnp.where(lo, a, pltpu.roll(b, HEAD_DIM, axis=1)))
        else:
            blocks.append(jnp.where(lo, pltpu.roll(a, HEAD_DIM, axis=1), b))
    return jnp.concatenate(blocks, axis=1)


def _group_rows(q_ref, kv):
    heads = range(kv * GQA_GROUP, (kv + 1) * GQA_GROUP)
    return jnp.concatenate([q_ref[0, :, hd * LANES:(hd + 1) * LANES] for hd in heads], axis=0)


def _heads_from_transposed(out_t, tq, kv):
    out = jnp.transpose(out_t)
    return _place_heads([out[g * tq:(g + 1) * tq] for g in range(GQA_GROUP)], kv)


def _cmp_select_kernel(q_ref, kc_ref, vc_ref, covt_ref, o_ref, m_ref, *, q_off, n_pick):
    tq = q_ref.shape[1]
    rows = GQA_GROUP * tq
    nb = kc_ref.shape[2]
    wl = max(tq, LANES)
    assert tq % LANES == 0 or rows == LANES
    t0 = q_off + pl.program_id(1) * tq
    kc = kc_ref[0, 0]
    vc = vc_ref[0, 0]
    qpos = t0 + lax.broadcasted_iota(jnp.int32, (nb, rows), 1) % tq
    cend = lax.broadcasted_iota(jnp.int32, (nb, rows), 0) * CMP_STRIDE + (CMP_LEN - 1)
    valid = cend <= qpos
    blk = lax.broadcasted_iota(jnp.int32, (N_SEL_LANES, wl), 0)
    cur = (t0 + lax.broadcasted_iota(jnp.int32, (N_SEL_LANES, wl), 1) % tq) // SEL_BLOCK
    forced = (blk == 0) | ((blk <= cur) & (blk > cur - N_LOCAL))
    o_groups = []
    for kv in range(N_KV_HEADS):
        s = _dot_nt(kc, _group_rows(q_ref, kv))
        s = jnp.where(valid, s, NEG)
        e = jnp.exp2(s - jnp.max(s, axis=0, keepdims=True))
        p = e / jnp.sum(e, axis=0, keepdims=True)
        p = jnp.where(valid, p, 0.0)
        o_t = lax.dot_general(vc, p.astype(BF16), (((0,), (0,)), ((), ())), preferred_element_type=F32)
        o_groups.append(_heads_from_transposed(o_t, tq, kv))
        if tq % LANES == 0:
            psum = sum(p[:, g * tq:(g + 1) * tq] for g in range(GQA_GROUP))
        else:
            psum = p + sum(pltpu.roll(p, g * tq, axis=1) for g in range(1, GQA_GROUP))
        hi, lo = _split2(psum)
        imp = _dot(covt_ref[...], hi) + _dot(covt_ref[...], lo)
        x = jnp.where(forced, BIG, jnp.where(blk > cur, -BIG, imp))
        sel = jnp.zeros(x.shape, jnp.bool_)
        for _ in range(n_pick):
            mx = jnp.max(x, axis=0, keepdims=True)
            idx = jnp.min(jnp.where(x == mx, blk, N_SEL_LANES), axis=0, keepdims=True)
            hit = blk == idx
            sel = sel | hit
            x = jnp.where(hit, -jnp.inf, x)
        mneg = jnp.transpose(jnp.where(sel, 0.0, NEG))
        m_ref[0, kv] = mneg[:tq].astype(m_ref.dtype)
    o_ref[0] = jnp.concatenate(o_groups, axis=1)


def cmp_select(qp, kcv, q_off, n_pick, tq):
    b, t, _ = qp.shape
    nb = kcv.shape[2]
    cover = jnp.transpose(_cover_matrix(nb))
    return pl.pallas_call(
        functools.partial(_cmp_select_kernel, q_off=q_off, n_pick=n_pick),
        grid=(b, t // tq),
        in_specs=[
            pl.BlockSpec((1, tq, N_Q_HEADS * LANES), lambda i, j: (i, j, 0)),
            pl.BlockSpec((1, 1, nb, KV_WIDTH), lambda i, j: (i, 0, 0, 0)),
            pl.BlockSpec((1, 1, nb, KV_WIDTH), lambda i, j: (i, 1, 0, 0)),
            pl.BlockSpec((N_SEL_LANES, nb), lambda i, j: (0, 0)),
        ],
        out_specs=(
            pl.BlockSpec((1, tq, ATTN_WIDTH), lambda i, j: (i, j, 0)),
            pl.BlockSpec((1, N_KV_HEADS, tq, N_SEL_LANES), lambda i, j: (i, 0, j, 0)),
        ),
        out_shape=(
            jax.ShapeDtypeStruct((b, t, ATTN_WIDTH), F32),
            jax.ShapeDtypeStruct((b, N_KV_HEADS, t, N_SEL_LANES), BF16),
        ),
        compiler_params=_cparams(("arbitrary", "arbitrary")),
        name="cmp_select",
    )(qp, kcv, kcv, cover)


SEL_TILE_ELEMS = 512 * 512
WIN_CHUNK = 256


def _block_onehot(s):
    key = np.arange(s)[:, None]
    j = np.arange(N_SEL_LANES)[None, :]
    return jnp.asarray((key // SEL_BLOCK == j).astype(np.float32), BF16)


def _gate_expand():
    m = np.zeros((3, LANES, ATTN_WIDTH), np.float32)
    for br in range(3):
        for hd in range(N_Q_HEADS):
            m[br, SSM_HEADS + 3 * hd + br, hd * HEAD_DIM:(hd + 1) * HEAD_DIM] = 1.0
    return jnp.asarray(m, BF16)


def _flash_update(ss, v, m_ref, acc_ref):
    lane = lax.broadcasted_iota(jnp.int32, v.shape, 1)
    one = jnp.ones(v.shape, v.dtype)
    stage = []
    for k, s in enumerate(ss):
        m_old = m_ref[k]
        m_new = jnp.maximum(m_old, jnp.max(s, axis=0, keepdims=True))
        alpha = jnp.exp2(m_old - m_new)
        p = jnp.exp2(s - m_new)
        m_ref[k] = m_new
        stage.append((alpha, p.astype(BF16)))
    for k, (alpha, p) in enumerate(stage):
        vk = jnp.where((lane < HEAD_DIM) == (k == 0), v, one)
        pv = lax.dot_general(vk, p, (((0,), (0,)), ((), ())), preferred_element_type=F32)
        acc_ref[k] = alpha * acc_ref[k] + pv


def _sel_chunk(rows, n_keys):
    chunk = SEL_TILE_ELEMS // rows
    while n_keys % chunk:
        chunk //= 2
    return chunk


def _sel_win_kernel(q_ref, mneg_ref, ksel_ref, vsel_ref, et_ref, kwin_ref, vwin_ref, ocmp_ref, misc_ref, eg_ref,
                    o_ref, lhs_ref, m_ref, acc_ref, *, q_off, win_pos0):
    tq = q_ref.shape[1]
    rows = GQA_GROUP * tq
    SEL_CHUNK = _sel_chunk(rows, ksel_ref.shape[1])
    t0 = q_off + pl.program_id(1) * tq
    n_sel = lax.shift_right_logical(t0 + tq - 1, int(math.log2(SEL_CHUNK))) + 1
    w_lo = jnp.maximum(t0 - (WINDOW - 1) - win_pos0, 0) // WIN_CHUNK
    w_hi = (t0 + tq - 1 - win_pos0) // WIN_CHUNK + 1

    def qrow(n_keys):
        return lax.broadcasted_iota(jnp.int32, (n_keys, rows), 1) % tq + t0

    def init():
        m_ref[...] = jnp.full(m_ref.shape, NEG, F32)
        acc_ref[...] = jnp.zeros(acc_ref.shape, F32)

    def finish():
        outs = []
        for kv in range(N_KV_HEADS):
            acc = acc_ref[kv]
            denom_row = HEAD_DIM * (1 - kv)
            outs.append(_heads_from_transposed(acc / acc[denom_row:denom_row + 1, :], tq, kv))
        return jnp.concatenate(outs, axis=1)

    for kv in range(N_KV_HEADS):
        for g in range(GQA_GROUP):
            hd = kv * GQA_GROUP + g
            lhs_ref[kv, g * tq:(g + 1) * tq, :LANES] = q_ref[0, :, hd * LANES:(hd + 1) * LANES]
            lhs_ref[kv, g * tq:(g + 1) * tq, LANES:] = mneg_ref[0, kv]

    init()

    def sel_step(c, carry, causal):
        r0 = pl.multiple_of(c * SEL_CHUNK, SEL_CHUNK)
        rhs = jnp.concatenate([ksel_ref[0, pl.ds(r0, SEL_CHUNK), :], et_ref[pl.ds(r0, SEL_CHUNK), :]], axis=1)
        v = vsel_ref[0, pl.ds(r0, SEL_CHUNK), :]
        if causal:
            ok = r0 + lax.broadcasted_iota(jnp.int32, (SEL_CHUNK, rows), 0) <= qrow(SEL_CHUNK)
        ss = [_dot_nt(rhs, lhs_ref[kv]) for kv in range(N_KV_HEADS)]
        if causal:
            ss = [jnp.where(ok, s, NEG) for s in ss]
        _flash_update(ss, v, m_ref, acc_ref)
        return carry

    n_full = lax.shift_right_logical(t0 + 1, int(math.log2(SEL_CHUNK)))
    lax.fori_loop(0, n_full, functools.partial(sel_step, causal=False), 0)
    lax.fori_loop(n_full, n_sel, functools.partial(sel_step, causal=True), 0)
    o_sel = finish()

    init()

    def win_step(c, carry):
        r0 = pl.multiple_of(c * WIN_CHUNK, WIN_CHUNK)
        k = kwin_ref[0, pl.ds(r0, WIN_CHUNK), :]
        v = vwin_ref[0, pl.ds(r0, WIN_CHUNK), :]
        wpos = win_pos0 + r0 + lax.broadcasted_iota(jnp.int32, (WIN_CHUNK, rows), 0)
        qr = qrow(WIN_CHUNK)
        ok = (wpos <= qr) & (wpos > qr - WINDOW)
        ss = [jnp.where(ok, _dot_nt(k, lhs_ref[kv, :, :LANES]), NEG) for kv in range(N_KV_HEADS)]
        _flash_update(ss, v, m_ref, acc_ref)
        return carry

    lax.fori_loop(w_lo, w_hi, win_step, 0)
    o_win = finish()

    gates = jax.nn.sigmoid(misc_ref[0])
    ghi = gates.astype(BF16)
    glo = (gates - ghi.astype(F32)).astype(BF16)
    branches = (ocmp_ref[0], o_sel, o_win)
    out = jnp.zeros(branches[0].shape, F32)
    for br in range(3):
        out = out + (_dot(ghi, eg_ref[br]) + _dot(glo, eg_ref[br])) * branches[br]
    o_ref[0] = out


def sel_win_attention(qp, mneg, kvb, sel_col, winb, o_cmp, misc, q_off, win_pos0, tq):
    b, t, _ = qp.shape
    s = kvb.shape[1]
    sw = winb.shape[1]
    et = _block_onehot(s)
    eg = _gate_expand()
    rows = GQA_GROUP * tq
    assert q_off + t <= s and q_off + t - win_pos0 <= sw and sw % WIN_CHUNK == 0
    return pl.pallas_call(
        functools.partial(_sel_win_kernel, q_off=q_off, win_pos0=win_pos0),
        grid=(b, t // tq),
        in_specs=[
            pl.BlockSpec((1, tq, N_Q_HEADS * LANES), lambda i, j: (i, j, 0)),
            pl.BlockSpec((1, N_KV_HEADS, tq, N_SEL_LANES), lambda i, j: (i, 0, j, 0)),
            pl.BlockSpec((1, s, KV_WIDTH), lambda i, j: (i, 0, sel_col)),
            pl.BlockSpec((1, s, KV_WIDTH), lambda i, j: (i, 0, sel_col + 1)),
            pl.BlockSpec((s, N_SEL_LANES), lambda i, j: (0, 0)),
            pl.BlockSpec((1, sw, KV_WIDTH), lambda i, j: (i, 0, 0)),
            pl.BlockSpec((1, sw, KV_WIDTH), lambda i, j: (i, 0, 1)),
            pl.BlockSpec((1, tq, ATTN_WIDTH), lambda i, j: (i, j, 0)),
            pl.BlockSpec((1, tq, LANES), lambda i, j: (i, j, 0)),
            pl.BlockSpec((3, LANES, ATTN_WIDTH), lambda i, j: (0, 0, 0)),
        ],
        out_specs=pl.BlockSpec((1, tq, ATTN_WIDTH), lambda i, j: (i, j, 0)),
        out_shape=jax.ShapeDtypeStruct((b, t, ATTN_WIDTH), F32),
        scratch_shapes=[
            pltpu.VMEM((N_KV_HEADS, rows, 2 * LANES), BF16),
            pltpu.VMEM((N_KV_HEADS, 1, rows), F32),
            pltpu.VMEM((N_KV_HEADS, LANES, rows), F32),
        ],
        compiler_params=_cparams(("arbitrary", "arbitrary")),
        name="sel_win_attention",
    )(qp, mneg, kvb, kvb, et, winb, winb, o_cmp, misc, eg)


CONV_PAD = 8
HEAD_PAIRS = SSM_HEADS // 2


def _split3(x):
    a = x.astype(BF16)
    r = x - a.astype(F32)
    b = r.astype(BF16)
    c = (r - b.astype(F32)).astype(BF16)
    return a, b, c


def _ssd_kernel(xbc_ref, z_ref, misc_ref, conv0_ref, h0_ref, cw_ref, cb_ref, dtb_ref, a_ref, dsk_ref, nw_ref,
                y_ref, hout_ref, cout_ref, xp_ref, h_ref, ms_ref, *, t_valid):
    ch = pl.program_id(1)
    L = SSD_CHUNK
    keep = CONV_WIDTH - 1

    @pl.when(ch == 0)
    def _():
        xp_ref[...] = jnp.zeros(xp_ref.shape, F32)
        xp_ref[CONV_PAD - keep:CONV_PAD, :] = conv0_ref[0]
        h_ref[...] = h0_ref[0]

    xp_ref[CONV_PAD:CONV_PAD + t_valid, :] = xbc_ref[0]
    conv = cb_ref[...]
    for j in range(CONV_WIDTH):
        conv = conv + cw_ref[j:j + 1, :] * xp_ref[CONV_PAD - keep + j:CONV_PAD - keep + j + L, :]
    last = xp_ref[CONV_PAD + t_valid - keep:CONV_PAD + t_valid, :]
    cout_ref[0] = last
    xp_ref[CONV_PAD - keep:CONV_PAD, :] = last
    xc = _silu(conv)

    row = lax.broadcasted_iota(jnp.int32, (L, LANES), 0)
    lane = lax.broadcasted_iota(jnp.int32, (L, LANES), 1)
    if t_valid == L:
        raw = misc_ref[0]
    else:
        ms_ref[...] = jnp.zeros(ms_ref.shape, F32)
        ms_ref[0:t_valid, :] = misc_ref[0]
        raw = ms_ref[...]
    v = raw + dtb_ref[...]
    dt = jnp.maximum(v, 0.0) + jnp.log(1.0 + jnp.exp(-jnp.abs(v)))
    dt = jnp.where((lane < SSM_HEADS) & (row < t_valid), dt, 0.0)
    da = dt * a_ref[...]
    tri = (lax.broadcasted_iota(jnp.int32, (L, L), 1) <= lax.broadcasted_iota(jnp.int32, (L, L), 0))
    trib = tri.astype(BF16)
    acum = sum(_dot(trib, part) for part in _split3(da))
    acum_t = jnp.transpose(acum)
    dt_t = jnp.transpose(dt)
    e_acum = jnp.exp(acum)
    e_last = jnp.exp(acum[L - 1:L, :])
    w_end = jnp.exp(acum[L - 1:L, :] - acum) * dt
    lo = lane < SSM_HEAD_DIM

    ys = []
    for pair in range(HEAD_PAIRS):
        grp = (2 * pair) // (SSM_HEADS // SSM_GROUPS)
        bg = xc[:, SSM_WIDTH + grp * SSM_STATE:SSM_WIDTH + (grp + 1) * SSM_STATE].astype(BF16)
        cg = xc[:, SSM_WIDTH + (SSM_GROUPS + grp) * SSM_STATE:SSM_WIDTH + (SSM_GROUPS + grp + 1) * SSM_STATE].astype(BF16)
        g = _dot_nt(cg, bg)
        xpair = xc[:, pair * LANES:(pair + 1) * LANES]
        y = jnp.zeros((L, LANES), F32)
        for sub in range(2):
            hd = 2 * pair + sub
            seg = acum[:, hd:hd + 1] - acum_t[hd:hd + 1, :]
            m = g * jnp.exp(jnp.where(tri, seg, NEG)) * dt_t[hd:hd + 1, :]
            xm = jnp.where(lo if sub == 0 else ~lo, xpair, 0.0)
            y = y + _dot(m.astype(BF16), xm.astype(BF16))
        col = lambda a: jnp.where(lo, a[:, 2 * pair:2 * pair + 1], a[:, 2 * pair + 1:2 * pair + 2])
        hp = h_ref[pair]
        y = y + _dot_nt(cg, hp.astype(BF16)) * col(e_acum)
        y = y + col(dsk_ref[...]) * xpair
        xw = (xpair * col(w_end)).astype(BF16)
        st = lax.dot_general(xw, bg, (((0,), (0,)), ((), ())), preferred_element_type=F32)
        prow = lax.broadcasted_iota(jnp.int32, (LANES, LANES), 0) < SSM_HEAD_DIM
        dec = jnp.where(prow, e_last[:, 2 * pair:2 * pair + 1], e_last[:, 2 * pair + 1:2 * pair + 2])
        h_ref[pair] = hp * dec + st
        ys.append(y)
    y = jnp.concatenate(ys, axis=1)
    if t_valid != L:
        y = y[:t_valid]
    y = y * _silu(z_ref[0])
    y = y * lax.rsqrt(jnp.mean(y * y, axis=-1, keepdims=True) + EPS) * nw_ref[...]
    y_ref[0] = y

    @pl.when(ch == pl.num_programs(1) - 1)
    def _():
        hout_ref[0] = h_ref[...]


def ssd(xbc, z, misc, conv0, h0, conv_w, conv_b, dt_bias, a_log, d_skip, norm_w):
    b, t, _ = xbc.shape
    L = SSD_CHUNK
    t_valid = L if t % L == 0 else t
    assert t_valid == L or t < L
    n_ch = max(t // L, 1)
    keep = CONV_WIDTH - 1
    pad8 = lambda v: jnp.pad(v.astype(F32), (0, LANES - SSM_HEADS)).reshape(1, LANES)
    dtb = pad8(dt_bias)
    a = pad8(-jnp.exp(a_log.astype(F32)))
    dsk = pad8(d_skip)
    h0p = h0.reshape(b, HEAD_PAIRS, 2 * SSM_HEAD_DIM, SSM_STATE)
    full = lambda arr: pl.BlockSpec(arr.shape, lambda i, c: (0,) * arr.ndim)
    tok = lambda wd: pl.BlockSpec((1, t_valid, wd), lambda i, c: (i, c, 0))
    y, hout, cout = pl.pallas_call(
        functools.partial(_ssd_kernel, t_valid=t_valid),
        grid=(b, n_ch),
        in_specs=[
            tok(CONV_DIM), tok(SSM_WIDTH), tok(LANES),
            pl.BlockSpec((1, keep, CONV_DIM), lambda i, c: (i, 0, 0)),
            pl.BlockSpec((1, HEAD_PAIRS, 2 * SSM_HEAD_DIM, SSM_STATE), lambda i, c: (i, 0, 0, 0)),
            full(conv_w), pl.BlockSpec((1, CONV_DIM), lambda i, c: (0, 0)),
            full(dtb), full(a), full(dsk), pl.BlockSpec((1, SSM_WIDTH), lambda i, c: (0, 0)),
        ],
        out_specs=(
            tok(SSM_WIDTH),
            pl.BlockSpec((1, HEAD_PAIRS, 2 * SSM_HEAD_DIM, SSM_STATE), lambda i, c: (i, 0, 0, 0)),
            pl.BlockSpec((1, keep, CONV_DIM), lambda i, c: (i, 0, 0)),
        ),
        out_shape=(
            jax.ShapeDtypeStruct((b, t, SSM_WIDTH), F32),
            jax.ShapeDtypeStruct((b, HEAD_PAIRS, 2 * SSM_HEAD_DIM, SSM_STATE), F32),
            jax.ShapeDtypeStruct((b, keep, CONV_DIM), F32),
        ),
        scratch_shapes=[
            pltpu.VMEM((CONV_PAD + L, CONV_DIM), F32),
            pltpu.VMEM((HEAD_PAIRS, 2 * SSM_HEAD_DIM, SSM_STATE), F32),
            pltpu.VMEM((L, LANES), F32),
        ],
        compiler_params=_cparams(("arbitrary", "arbitrary")),
        name="ssd",
    )(xbc, z, misc, conv0, h0p, conv_w, conv_b.reshape(1, CONV_DIM), dtb, a, dsk, norm_w.reshape(1, SSM_WIDTH))
    return y, hout.reshape(b, SSM_HEADS, SSM_HEAD_DIM, SSM_STATE), cout


def _split2(x):
    hi = x.astype(BF16)
    return hi, (x - hi.astype(F32)).astype(BF16)


def _merge_kernel(oa_ref, ys_ref, x_ref, g1_ref, sh2_ref, sc2_ref, anw_ref, wo_ref, n2w_ref, wrh_ref, wrl_ref,
                  x1_ref, h2_ref, lg_ref):
    oa = oa_ref[...]
    a = oa * lax.rsqrt(jnp.mean(oa * oa, axis=-1, keepdims=True) + EPS) * anw_ref[...]
    cat = jnp.concatenate([a.astype(BF16), ys_ref[...].astype(BF16)], axis=1)
    x1 = x_ref[...] + _mod(g1_ref) * _dot(cat, wo_ref[...])
    x1_ref[...] = x1
    h2 = x1 * lax.rsqrt(jnp.mean(x1 * x1, axis=-1, keepdims=True) + EPS) * n2w_ref[...]
    h2 = h2 * (1.0 + _mod(sc2_ref)) + _mod(sh2_ref)
    h2_ref[...] = h2.astype(BF16)
    hh, hl = _split2(h2)
    lg_ref[...] = _dot_nt(wrh_ref[...], hh) + _dot_nt(wrh_ref[...], hl) + _dot_nt(wrl_ref[...], hh)


def merge(o_attn, y_ssm, x, mod3, mod_row0, attn_norm_w, wo, norm2_w, w_router, tm):
    b, t, d = x.shape
    n = b * t
    tiles_per_b = t // tm
    wrt = jnp.transpose(w_router)
    wrh, wrl = _split2(wrt)

    def mod_spec(col):
        return _mod_spec(mod3, col, tm, tiles_per_b, mod_row0)

    tok = lambda wd: pl.BlockSpec((tm, wd), lambda i: (i, 0))
    full = lambda a: pl.BlockSpec(a.shape, lambda i: (0,) * a.ndim)
    return pl.pallas_call(
        _merge_kernel,
        grid=(n // tm,),
        in_specs=[tok(ATTN_WIDTH), tok(SSM_WIDTH), tok(d), mod_spec(2), mod_spec(3), mod_spec(4),
                  full(attn_norm_w), full(wo), full(norm2_w), full(wrh), full(wrl)],
        out_specs=(tok(d), tok(d), pl.BlockSpec((N_EXPERTS, tm), lambda i: (0, i))),
        out_shape=(jax.ShapeDtypeStruct((n, d), F32), jax.ShapeDtypeStruct((n, d), BF16),
                   jax.ShapeDtypeStruct((N_EXPERTS, n), F32)),
        compiler_params=_cparams(("arbitrary",)),
        name="merge",
    )(o_attn.reshape(n, ATTN_WIDTH), y_ssm.reshape(n, SSM_WIDTH), x.reshape(n, d), mod3, mod3, mod3,
      attn_norm_w, wo, norm2_w, wrh, wrl)


EXPERTS_PER_GROUP = N_EXPERTS // N_EXPERT_GROUPS


def _first_max(x, ids, axes, n_ids):
    mx = jnp.max(x, axis=axes, keepdims=True)
    return ids == jnp.min(jnp.where(x == mx, ids, n_ids), axis=axes, keepdims=True), mx


def _route_kernel(lg_ref, eb_ref, tri_ref, w_ref, pos_ref, cnt_ref):
    lg = lg_ref[...]
    tn = lg.shape[2]
    scores = jax.nn.sigmoid(lg)
    biased = scores + eb_ref[...]
    sub = lax.broadcasted_iota(jnp.int32, lg.shape, 1)
    grp = lax.broadcasted_iota(jnp.int32, (N_EXPERT_GROUPS, 1, tn), 0)
    eid = lax.broadcasted_iota(jnp.int32, lg.shape, 0) * EXPERTS_PER_GROUP + sub
    hit, m1 = _first_max(biased, sub, 1, EXPERTS_PER_GROUP)
    m2 = jnp.max(jnp.where(hit, -jnp.inf, biased), axis=1, keepdims=True)
    gs = m1 + m2
    keep = jnp.zeros(gs.shape, jnp.bool_)
    for _ in range(TOPK_GROUPS):
        hit, _m = _first_max(gs, grp, 0, N_EXPERT_GROUPS)
        keep = keep | hit
        gs = jnp.where(hit, -jnp.inf, gs)
    x = jnp.where(keep, biased, NEG)
    sel = jnp.zeros(lg.shape, jnp.bool_)
    for _ in range(TOP_K):
        hit, _m = _first_max(x, eid, (0, 1), N_EXPERTS)
        sel = sel | hit
        x = jnp.where(hit, -jnp.inf, x)
    w = jnp.where(sel, scores, 0.0)
    w = w / jnp.sum(w, axis=(0, 1), keepdims=True) * ROUTED_SCALE
    w_ref[...] = w
    selb = sel.astype(BF16).reshape(N_EXPERTS, tn)
    pos = _dot(selb, tri_ref[...])
    pos_ref[...] = jnp.where(sel, pos.reshape(lg.shape), -1.0)
    cnt = jnp.sum(sel.astype(F32), axis=2, keepdims=True)
    cnt_ref[0] = jnp.broadcast_to(cnt, cnt_ref.shape[1:]).astype(jnp.int32)


def route(logits_t, e_bias, tn):
    n = logits_t.shape[1]
    lg3 = logits_t.reshape(N_EXPERT_GROUPS, EXPERTS_PER_GROUP, n)
    eb = e_bias.astype(F32).reshape(N_EXPERT_GROUPS, EXPERTS_PER_GROUP, 1)
    tri = jnp.asarray(np.triu(np.ones((tn, tn), np.float32), 1), BF16)
    blk = pl.BlockSpec((N_EXPERT_GROUPS, EXPERTS_PER_GROUP, tn), lambda i: (0, 0, i))
    w, pos, cnt = pl.pallas_call(
        _route_kernel,
        grid=(n // tn,),
        in_specs=[blk, pl.BlockSpec(eb.shape, lambda i: (0, 0, 0)), pl.BlockSpec((tn, tn), lambda i: (0, 0))],
        out_specs=(blk, blk, pl.BlockSpec((1, N_EXPERT_GROUPS, EXPERTS_PER_GROUP, LANES), lambda i: (i, 0, 0, 0))),
        out_shape=(jax.ShapeDtypeStruct(lg3.shape, F32), jax.ShapeDtypeStruct(lg3.shape, F32),
                   jax.ShapeDtypeStruct((n // tn, N_EXPERT_GROUPS, EXPERTS_PER_GROUP, LANES), jnp.int32)),
        compiler_params=_cparams(("arbitrary",)),
        name="route",
    )(lg3, eb, tri)
    return w.reshape(N_EXPERTS, n), pos.reshape(N_EXPERTS, n), cnt[..., 0].reshape(n // tn, N_EXPERTS)


MOE_ROWS = 128


def _swiglu(xb, wgu, wd, width):
    gu = _dot(xb, wgu)
    act = _silu(gu[:, :width]) * gu[:, width:]
    return _dot(act.astype(BF16), wd)


MOE_EXPERTS_PER_STEP = 4


MOE_ALIGN = 16
MOE_GATHER_ROWS = 896


def _moe_slots(tm):
    worst = TOP_K * tm + N_EXPERTS * (MOE_ALIGN - 1) + MOE_ROWS
    return -(-worst // MOE_GATHER_ROWS) * MOE_GATHER_ROWS


def _moe_kernel(cnt_ref, start_ref, h2_ref, w_ref, pos_ref, x1_ref, g2_ref, wgu_ref, wd_ref, sgu_ref, sd_ref,
                o_ref, g_all, xs):
    i = pl.program_id(0)
    es = pl.program_id(1)
    tm = h2_ref.shape[0]
    slots = g_all.shape[0]
    slot = lax.broadcasted_iota(jnp.int32, (MOE_ROWS, tm), 0).astype(F32)
    row = lax.broadcasted_iota(jnp.int32, (MOE_ROWS, 1), 0)

    def n_windows(cnt):
        return (cnt + MOE_ROWS - 1) // MOE_ROWS

    def window_start(e, j):
        return pl.multiple_of(start_ref[i * N_EXPERTS + e] + j * MOE_ROWS, MOE_ALIGN)

    @pl.when(es == 0)
    def _():
        g_all[...] = jnp.zeros(g_all.shape, BF16)

        def mark(e, carry):
            pos = pos_ref[pl.ds(e, 1), :]

            def mark_window(j, carry):
                hit = pos == slot + (j * MOE_ROWS).astype(F32)
                g_all[pl.ds(window_start(e, j), MOE_ROWS), :] = hit.astype(BF16)
                return carry

            return lax.fori_loop(0, n_windows(cnt_ref[i * N_EXPERTS + e]), mark_window, carry)

        lax.fori_loop(0, N_EXPERTS, mark, 0)

        def gather(c, carry):
            r0 = pl.multiple_of(c * MOE_GATHER_ROWS, MOE_GATHER_ROWS)
            rows = _dot(g_all[pl.ds(r0, MOE_GATHER_ROWS), :], h2_ref[...])
            xs[pl.ds(r0, MOE_GATHER_ROWS), :] = rows.astype(BF16)
            return carry

        lax.fori_loop(0, slots // MOE_GATHER_ROWS, gather, 0)

    for q in range(MOE_EXPERTS_PER_STEP):
        e = es * MOE_EXPERTS_PER_STEP + q
        cnt = cnt_ref[i * N_EXPERTS + e]
        wrow = w_ref[pl.ds(e, 1), :]

        def window(j, carry, q=q, e=e, cnt=cnt, wrow=wrow):
            r0 = window_start(e, j)
            xg = xs[pl.ds(r0, MOE_ROWS), :]
            out = _swiglu(xg, wgu_ref[q].astype(BF16), wd_ref[q].astype(BF16), D_EXPERT)
            g = g_all[pl.ds(r0, MOE_ROWS), :].astype(F32)
            out = out * jnp.sum(g * wrow, axis=1, keepdims=True)
            mine = row < cnt - j * MOE_ROWS
            xs[pl.ds(r0, MOE_ROWS), :] = jnp.where(mine, out.astype(BF16), xg)
            return carry

        lax.fori_loop(0, n_windows(cnt), window, 0)

    @pl.when(es == pl.num_programs(1) - 1)
    def _():
        y = lax.dot_general(g_all[...], xs[...], (((0,), (0,)), ((), ())), preferred_element_type=F32)
        y = y + _swiglu(h2_ref[...], sgu_ref[...], sd_ref[...], D_SHARED)
        o_ref[...] = x1_ref[...] + _mod(g2_ref) * y


def moe(h2, w_t, pos_t, counts, x1, mod3, mod_row0, t_per_b, wgu, wd, sgu, sd, tm):
    n, d = h2.shape
    tiles_per_b = t_per_b // tm
    eps = MOE_EXPERTS_PER_STEP
    slots = _moe_slots(tm)
    padded = (counts + MOE_ALIGN - 1) // MOE_ALIGN * MOE_ALIGN
    starts = jnp.cumsum(padded, axis=1) - padded
    grid_spec = pltpu.PrefetchScalarGridSpec(
        num_scalar_prefetch=2,
        grid=(n // tm, N_EXPERTS // eps),
        in_specs=[
            pl.BlockSpec((tm, d), lambda i, e, *_: (i, 0)),
            pl.BlockSpec((N_EXPERTS, tm), lambda i, e, *_: (0, i)),
            pl.BlockSpec((N_EXPERTS, tm), lambda i, e, *_: (0, i)),
            pl.BlockSpec((tm, d), lambda i, e, *_: (i, 0)),
            _mod_spec(mod3, 5, tm, tiles_per_b, mod_row0),
            pl.BlockSpec((eps, d, 2 * D_EXPERT), lambda i, e, *_: (e, 0, 0)),
            pl.BlockSpec((eps, D_EXPERT, d), lambda i, e, *_: (e, 0, 0)),
            pl.BlockSpec(sgu.shape, lambda i, e, *_: (0, 0)),
            pl.BlockSpec(sd.shape, lambda i, e, *_: (0, 0)),
        ],
        out_specs=pl.BlockSpec((tm, d), lambda i, e, *_: (i, 0)),
        scratch_shapes=[pltpu.VMEM((slots, tm), BF16), pltpu.VMEM((slots, d), BF16)],
    )
    return pl.pallas_call(
        _moe_kernel,
        grid_spec=grid_spec,
        out_shape=jax.ShapeDtypeStruct((n, d), F32),
        compiler_params=_cparams(("arbitrary", "arbitrary")),
        name="moe",
    )(counts.reshape(-1), starts.reshape(-1).astype(jnp.int32), h2, w_t, pos_t, x1, mod3, wgu, wd, sgu, sd)


SC_WINDOW = 128
PACK_W = 256
MOE_BLOCK_ROWS = 512
HI_MASK = -65536


def _pack_pair(x):
    bits = pltpu.bitcast(x.astype(BF16).astype(F32), jnp.int32)
    return lax.shift_right_logical(bits[:, :PACK_W], 16) | (bits[:, PACK_W:] & HI_MASK)


def _unpack_pair(word):
    lo = pltpu.bitcast(lax.shift_left(word, 16), F32)
    hi = pltpu.bitcast(word & HI_MASK, F32)
    return jnp.concatenate([lo, hi], axis=1)


def _pack_kernel(x_ref, a_ref, b_ref):
    x = x_ref[...]
    a_ref[...] = _pack_pair(x[:, :2 * PACK_W])
    b_ref[...] = _pack_pair(x[:, 2 * PACK_W:])


def pack_rows(x, tm):
    n, d = x.shape
    tok = lambda wd: pl.BlockSpec((tm, wd), lambda i: (i, 0))
    return pl.pallas_call(
        _pack_kernel, grid=(n // tm,), in_specs=[tok(d)], out_specs=(tok(PACK_W), tok(PACK_W)),
        out_shape=(jax.ShapeDtypeStruct((n, PACK_W), jnp.int32),) * 2,
        compiler_params=_cparams(("arbitrary",)), name="pack_rows",
    )(x)


def _slots_kernel(w_ref, pos_ref, base_ref, tri_ref, slot_ref, wt_ref):
    w = w_ref[...]
    pos = pos_ref[...]
    sel = pos >= 0.0
    rank = _dot(tri_ref[...], sel.astype(BF16))
    dest = base_ref[0] + pos
    slots, wts = [], []
    for j in range(TOP_K):
        mine = sel & (rank == float(j))
        slots.append(jnp.sum(jnp.where(mine, dest, 0.0), axis=0, keepdims=True))
        wts.append(jnp.sum(jnp.where(mine, w, 0.0), axis=0, keepdims=True))
    slot_ref[...] = jnp.concatenate(slots, axis=0).astype(jnp.int32)
    wpad = jnp.concatenate(wts + [jnp.zeros((LANES - TOP_K, w.shape[1]), F32)], axis=0)
    wt_ref[...] = jnp.transpose(wpad)


def slots_of(w_t, pos_t, base, tn):
    n = w_t.shape[1]
    tri = jnp.asarray(np.tril(np.ones((N_EXPERTS, N_EXPERTS), np.float32), -1), BF16)
    blk = pl.BlockSpec((N_EXPERTS, tn), lambda i: (0, i))
    return pl.pallas_call(
        _slots_kernel, grid=(n // tn,),
        in_specs=[blk, blk, pl.BlockSpec((1, N_EXPERTS, 1), lambda i: (i, 0, 0)),
                  pl.BlockSpec((N_EXPERTS, N_EXPERTS), lambda i: (0, 0))],
        out_specs=(pl.BlockSpec((TOP_K, tn), lambda i: (0, i)), pl.BlockSpec((tn, LANES), lambda i: (i, 0))),
        out_shape=(jax.ShapeDtypeStruct((TOP_K, n), jnp.int32), jax.ShapeDtypeStruct((n, LANES), F32)),
        compiler_params=_cparams(("arbitrary",)), name="moe_slots",
    )(w_t, pos_t, base, tri)


def sc_scatter_rows(rows, idx, n_out):
    n, d = rows.shape
    m = idx.shape[0]
    nb = n // SC_WINDOW
    mesh = plsc.VectorSubcoreMesh(core_axis_name="core", subcore_axis_name="subcore")

    @functools.partial(pl.kernel, out_type=jax.ShapeDtypeStruct((n_out, d), rows.dtype), mesh=mesh)
    def scatter(x_hbm, i_hbm, o_hbm):
        def body(x_vmem, i_vmem):
            pltpu.sync_copy(x_vmem, o_hbm.at[i_vmem.at[0]])

        pltpu.emit_pipeline(
            body, grid=(m // SC_WINDOW,),
            in_specs=[pl.BlockSpec((SC_WINDOW, d), index_map=lambda i: (i % nb, 0)),
                      pl.BlockSpec((1, SC_WINDOW), index_map=lambda i: (0, i))],
            out_specs=[], core_axis_name=("core", "subcore"), dimension_semantics=(pltpu.PARALLEL,),
        )(x_hbm, i_hbm)

    return scatter(rows, idx.reshape(1, m))


def sc_gather_rows(table, idx):
    d = table.shape[1]
    m = idx.shape[0]
    mesh = plsc.VectorSubcoreMesh(core_axis_name="core", subcore_axis_name="subcore")

    @functools.partial(pl.kernel, out_type=jax.ShapeDtypeStruct((m, d), table.dtype), mesh=mesh)
    def gather(x_hbm, i_hbm, o_hbm):
        def body(i_vmem, o_vmem):
            pltpu.sync_copy(x_hbm.at[i_vmem.at[0]], o_vmem)

        pltpu.emit_pipeline(
            body, grid=(m // SC_WINDOW,),
            in_specs=[pl.BlockSpec((1, SC_WINDOW), index_map=lambda i: (0, i))],
            out_specs=[pl.BlockSpec((SC_WINDOW, d), index_map=lambda i: (i, 0))],
            core_axis_name=("core", "subcore"), dimension_semantics=(pltpu.PARALLEL,),
        )(i_hbm, o_hbm)

    return gather(table, idx.reshape(1, m))


def _experts_kernel(be_ref, nu_ref, xa_ref, xb_ref, wgu_ref, wd_ref, oa_ref, ob_ref, wgu_bf, wd_bf):
    b = pl.program_id(0)

    @pl.when(b < nu_ref[0])
    def _():
        @pl.when((b == 0) | (be_ref[b] != be_ref[jnp.maximum(b - 1, 0)]))
        def _():
            wgu_bf[...] = wgu_ref[0].astype(BF16)
            wd_bf[...] = wd_ref[0].astype(BF16)

        x = jnp.concatenate([_unpack_pair(xa_ref[...]), _unpack_pair(xb_ref[...])], axis=1).astype(BF16)
        out = _swiglu(x, wgu_bf[...], wd_bf[...], D_EXPERT)
        oa_ref[...] = _pack_pair(out[:, :2 * PACK_W])
        ob_ref[...] = _pack_pair(out[:, 2 * PACK_W:])


def experts_sorted(xa, xb, block_expert, n_used, wgu, wd):
    r = xa.shape[0]
    d = wd.shape[2]
    row = lambda b, be, nu: (jnp.minimum(b, nu[0] - 1), 0)
    blk = pl.BlockSpec((MOE_BLOCK_ROWS, PACK_W), row)
    grid_spec = pltpu.PrefetchScalarGridSpec(
        num_scalar_prefetch=2, grid=(r // MOE_BLOCK_ROWS,),
        in_specs=[blk, blk,
                  pl.BlockSpec((1, d, 2 * D_EXPERT), lambda b, be, nu: (be[b], 0, 0)),
                  pl.BlockSpec((1, D_EXPERT, d), lambda b, be, nu: (be[b], 0, 0))],
        out_specs=(blk, blk),
        scratch_shapes=[pltpu.VMEM((d, 2 * D_EXPERT), BF16), pltpu.VMEM((D_EXPERT, d), BF16)],
    )
    return pl.pallas_call(
        _experts_kernel, grid_spec=grid_spec,
        out_shape=(jax.ShapeDtypeStruct((r, PACK_W), jnp.int32),) * 2,
        compiler_params=_cparams(("arbitrary",)), name="moe_experts",
    )(block_expert, n_used, xa, xb, wgu, wd)


def _combine_kernel(ya_ref, yb_ref, wt_ref, h2_ref, x1_ref, g2_ref, sgu_ref, sd_ref, o_ref):
    wt = wt_ref[...]
    acc = _swiglu(h2_ref[...], sgu_ref[...], sd_ref[...], D_SHARED)
    for j in range(TOP_K):
        y = jnp.concatenate([_unpack_pair(ya_ref[j]), _unpack_pair(yb_ref[j])], axis=1)
        acc = acc + wt[:, j:j + 1] * y
    o_ref[...] = x1_ref[...] + _mod(g2_ref) * acc


def combine_sorted(ya, yb, wt, h2, x1, mod3, mod_row0, t_per_b, sgu, sd, tm):
    n, d = h2.shape
    tiles_per_b = t_per_b // tm
    tok = lambda wd: pl.BlockSpec((tm, wd), lambda i: (i, 0))
    yblk = pl.BlockSpec((TOP_K, tm, PACK_W), lambda i: (0, i, 0))
    full = lambda a: pl.BlockSpec(a.shape, lambda i: (0,) * a.ndim)
    return pl.pallas_call(
        _combine_kernel, grid=(n // tm,),
        in_specs=[yblk, yblk, tok(LANES), tok(d), tok(d), _mod_spec(mod3, 5, tm, tiles_per_b, mod_row0),
                  full(sgu), full(sd)],
        out_specs=tok(d), out_shape=jax.ShapeDtypeStruct((n, d), F32),
        compiler_params=_cparams(("arbitrary",)), name="moe_combine",
    )(ya, yb, wt, h2, x1, mod3, sgu, sd)


def moe_sorted(h2, w_t, pos_t, counts, x1, mod3, mod_row0, t_per_b, wgu, wd, sgu, sd, tm, overlap):
    n, d = h2.shape
    assert d == 4 * PACK_W and n % SC_WINDOW == 0
    n_blocks = (TOP_K * n + N_EXPERTS * (MOE_BLOCK_ROWS - 1)) // MOE_BLOCK_ROWS
    total = jnp.sum(counts, axis=0)
    region = (total + MOE_BLOCK_ROWS - 1) // MOE_BLOCK_ROWS * MOE_BLOCK_ROWS
    region_end = jnp.cumsum(region)
    base = (region_end - region)[None, :] + jnp.cumsum(counts, axis=0) - counts
    block_row0 = jnp.arange(n_blocks, dtype=region_end.dtype) * MOE_BLOCK_ROWS
    block_expert = jnp.sum(region_end[None, :] <= block_row0[:, None], axis=1)
    block_expert = jnp.minimum(block_expert, N_EXPERTS - 1).astype(jnp.int32)
    n_used = (region_end[-1:] // MOE_BLOCK_ROWS).astype(jnp.int32)
    slot, wt = slots_of(w_t, pos_t, base.astype(F32).reshape(-1, N_EXPERTS, 1), tm)
    dest = slot.reshape(-1)
    ha, hb = pack_rows(h2, tm)
    rows = n_blocks * MOE_BLOCK_ROWS
    xa, xb = sc_scatter_rows(ha, dest, rows), sc_scatter_rows(hb, dest, rows)
    anchor, rest = overlap()
    n_used, _ = lax.optimization_barrier((n_used, anchor))
    oa, ob = experts_sorted(xa, xb, block_expert, n_used, wgu, wd)
    ya = sc_gather_rows(oa, dest).reshape(TOP_K, n, PACK_W)
    yb = sc_gather_rows(ob, dest).reshape(TOP_K, n, PACK_W)
    return combine_sorted(ya, yb, wt, h2, x1, mod3, mod_row0, t_per_b, sgu, sd, tm), rest


GATHER_PAGES = 8


def _gather_kernel(pt_ref, *refs):
    pages, new_ref = refs[:GATHER_PAGES], refs[GATHER_PAGES]
    rows_ref, cmpx_ref, stage_ref = refs[GATHER_PAGES + 1:]
    step = pl.program_id(1)
    last = pl.num_programs(1) - 1
    n_rows = GATHER_PAGES * PAGE_SIZE

    @pl.when(step < last)
    def _():
        for k in range(GATHER_PAGES):
            sl = slice(k * PAGE_SIZE, (k + 1) * PAGE_SIZE)
            for r in range(4):
                tile = jnp.transpose(pages[k][0, r])
                if r < 2:
                    stage_ref[r, sl, :] = tile
                else:
                    rows_ref[0, sl, (r - 2) * KV_WIDTH:(r - 1) * KV_WIDTH] = tile.astype(BF16)

    @pl.when(step == last)
    def _():
        new = new_ref[0]
        tn = new.shape[0]
        stage_ref[...] = jnp.zeros(stage_ref.shape, F32)
        for s in range(2):
            stage_ref[s, 0:tn, :] = new[:, s * KV_WIDTH:(s + 1) * KV_WIDTH]
        pad = jnp.zeros((n_rows - tn, 2 * KV_WIDTH), F32)
        rows_ref[0] = jnp.concatenate([new[:, 2 * KV_WIDTH:], pad], axis=0).astype(BF16)

    _stride_block_store(stage_ref, cmpx_ref, n_rows)


def gather_pages(cache_t, page_table, new_rows):
    b, n_pages = page_table.shape
    steps = n_pages // GATHER_PAGES
    rows = GATHER_PAGES * PAGE_SIZE
    s_out = (steps + 1) * rows

    def page_spec(k):
        def idx(i, s, pt):
            p = jnp.minimum(s, steps - 1) * GATHER_PAGES + k
            return (pt[i * n_pages + p], 0, 0, 0)
        return pl.BlockSpec((1, 4, KV_WIDTH, PAGE_SIZE), idx)

    grid_spec = pltpu.PrefetchScalarGridSpec(
        num_scalar_prefetch=1,
        grid=(b, steps + 1),
        in_specs=[page_spec(k) for k in range(GATHER_PAGES)]
        + [pl.BlockSpec((1,) + new_rows.shape[1:], lambda i, s, pt: (i, 0, 0))],
        out_specs=(
            pl.BlockSpec((1, rows, 2 * KV_WIDTH), lambda i, s, pt: (i, s, 0)),
            pl.BlockSpec((1, rows // CMP_STRIDE, CMP_STRIDE * 2 * KV_WIDTH), lambda i, s, pt: (i, s, 0)),
        ),
        scratch_shapes=[pltpu.VMEM((2, rows, KV_WIDTH), F32)],
    )
    return pl.pallas_call(
        _gather_kernel,
        grid_spec=grid_spec,
        out_shape=(jax.ShapeDtypeStruct((b, s_out, 2 * KV_WIDTH), BF16),
                   jax.ShapeDtypeStruct((b, s_out // CMP_STRIDE, CMP_STRIDE * 2 * KV_WIDTH), BF16)),
        compiler_params=_cparams(("arbitrary", "arbitrary")),
        name="gather_pages",
    )(page_table.reshape(-1), *([cache_t] * GATHER_PAGES), new_rows)


def _attention(qp, cmpx, kvb, sel_col, winb, misc, cmp_w, q_off, win_pos0, tq):
    t = qp.shape[1]
    cur_lo, cur_hi = q_off // SEL_BLOCK, (q_off + t - 1) // SEL_BLOCK
    assert cur_hi < N_SEL_LANES or (cur_lo == cur_hi == N_SEL_LANES), (q_off, t)
    n_pick = N_SEL - (1 if cur_hi >= N_SEL_LANES else 0)
    kcv = compress(cmpx, *cmp_w)
    o_cmp, mneg = cmp_select(qp, kcv, q_off, n_pick, tq)
    return sel_win_attention(qp, mneg, kvb, sel_col, winb, o_cmp, misc, q_off, win_pos0, tq)


def kernel(x_prompt, x_sample, cache_kv, cache_win, state_ssm, state_conv, page_table, c_prompt, c_sample, w_ada, b_ada, norm1_w, norm2_w, w_in, q_norm_w, k_norm_w, cmp_pe, cmp_w1, cmp_w2, attn_out_norm_w, conv_w, conv_b, dt_bias, a_log, d_skip, ssm_norm_w, w_out, w_router, e_bias, w_exp_gu, w_exp_down, w_sh_gu, w_sh_down):
    xp, xq = x_prompt, x_sample
    bp, tp, d = xp.shape
    bq, tq, _ = xq.shape
    depth = w_ada.shape[0]
    past_len = page_table.shape[1] * PAGE_SIZE
    nq = bq * tq
    tq_pad = LANES // GQA_GROUP
    assert tp % 512 == 0 and tp >= WINDOW and nq % 8 == 0 and tq <= tq_pad
    pos_p = jnp.arange(tp, dtype=jnp.int32)
    pos_q = jnp.tile(past_len + jnp.arange(tq, dtype=jnp.int32), bq)
    c_all = jnp.concatenate([c_prompt, c_sample], axis=0)
    c_all = jnp.pad(c_all, ((0, -c_all.shape[0] % 8), (0, 0)))
    outs = [[] for _ in range(8)]
    for l in range(depth):
        mod = adaln_all(c_all, w_ada[l], b_ada[l])
        mod_p = mod.reshape(mod.shape[0], 1, 6 * d)
        mod_q = jnp.repeat(mod[bp:bp + bq], tq, axis=0)
        wp = _prep_w_in(w_in[l])
        cmp_w = _prep_compress(cmp_pe[l], cmp_w1[l], cmp_w2[l])
        wo = w_out[l].astype(BF16)
        wgu, wd = w_exp_gu[l], w_exp_down[l]
        sgu, sd = w_sh_gu[l].astype(BF16), w_sh_down[l].astype(BF16)
        ssm_w = (conv_w[l], conv_b[l], dt_bias[l], a_log[l], d_skip[l], ssm_norm_w[l])
        n1w, n2w, anw = norm1_w[l:l + 1], norm2_w[l:l + 1], attn_out_norm_w[l:l + 1]

        qp, kvb, win, winb, z, xbc, misc, kvt, cmpx = inproj(xp, mod_p, 0, n1w, wp, q_norm_w[l], k_norm_w[l], pos_p,
                                                            512, True)
        r3 = lambda a: a.reshape(bp, tp, a.shape[-1])
        o_attn = _attention(r3(qp), cmpx, r3(kvb), 2, r3(winb), r3(misc), cmp_w, 0, 0, 128)
        y_ssm, h_new, conv_new = ssd(r3(xbc), r3(z), r3(misc), jnp.zeros((bp, CONV_WIDTH - 1, CONV_DIM), F32),
                                     jnp.zeros((bp, SSM_HEADS, SSM_HEAD_DIM, SSM_STATE), F32), *ssm_w)
        x1, h2, lg = merge(o_attn, y_ssm, xp, mod_p, 0, anw, wo, n2w, w_router[l], 512)
        w_t, pos_t, cnt = route(lg, e_bias[l], 512)
        outs[0].append(jnp.transpose(kvt.reshape(bp, 4, N_KV_HEADS, HEAD_DIM, tp), (0, 4, 1, 2, 3)))
        outs[1].append(win.reshape(bp, tp, 2, N_KV_HEADS, HEAD_DIM)[:, tp - WINDOW:])
        outs[2].append(h_new)
        outs[3].append(conv_new)

        xq1 = xq.reshape(1, nq, d)
        rq = lambda a: a.reshape(bq, tq, a.shape[-1])
        padq = lambda a: jnp.pad(rq(a), ((0, 0), (0, tq_pad - tq), (0, 0)))

        def sample_front():
            proj = inproj(xq1, mod_q, 0, n1w, wp, q_norm_w[l], k_norm_w[l], pos_q, nq, False)
            cache_t = jnp.transpose(cache_kv[l], (0, 2, 3, 4, 1)).reshape(cache_kv.shape[1], 4, KV_WIDTH, PAGE_SIZE)
            past, cmpx = gather_pages(cache_t, page_table, rq(proj[-1]))
            return past[:1, :8, :LANES], (proj, past, cmpx)

        xp, (proj, past, cmpx) = moe_sorted(h2, w_t, pos_t, cnt, x1, mod_p, 0, tp, wgu, wd, sgu, sd, 512, sample_front)
        xp = xp.reshape(bp, tp, d)
        qp, kvb, win, winb, z, xbc, misc, kv = proj
        win_all = jnp.concatenate([cache_win[l].reshape(bq, WINDOW, 2 * KV_WIDTH).astype(BF16), rq(winb),
                                   jnp.zeros((bq, -(WINDOW + tq_pad) % WIN_CHUNK + tq_pad - tq, 2 * KV_WIDTH), BF16)],
                                  axis=1)
        o_attn = _attention(padq(qp), cmpx, past, 0, win_all, padq(misc), cmp_w, past_len, past_len - WINDOW,
                            tq_pad)[:, :tq]
        y_ssm, h_new, conv_new = ssd(rq(xbc), rq(z), rq(misc), state_conv[l], state_ssm[l], *ssm_w)
        x1, h2, lg = merge(o_attn.reshape(1, nq, ATTN_WIDTH), y_ssm.reshape(1, nq, SSM_WIDTH), xq1, mod_q, 0,
                           anw, wo, n2w, w_router[l], nq)
        w_t, pos_t, cnt = route(lg, e_bias[l], nq)
        xq = moe(h2, w_t, pos_t, cnt, x1, mod_q, 0, nq, wgu, wd, sgu, sd, nq).reshape(bq, tq, d)
        win_rows = win.reshape(bq, tq, 2, N_KV_HEADS, HEAD_DIM)
        outs[4].append(kv.reshape(bq, tq, 4, N_KV_HEADS, HEAD_DIM))
        outs[5].append(jnp.concatenate([cache_win[l], win_rows.astype(cache_win.dtype)], axis=1)[:, tq:])
        outs[6].append(h_new)
        outs[7].append(conv_new)
    return (xp, xq) + tuple(jnp.stack(o) for o in outs)
```

```python
import functools
import math

import jax
import jax.numpy as jnp
import numpy as np
from jax import lax
from jax.experimental import pallas as pl
from jax.experimental.pallas import tpu as pltpu
from jax.experimental.pallas import tpu_sc as plsc

D_MODEL = 1024
PAGE_SIZE = 128
HEAD_DIM = 64
N_Q_HEADS = 8
N_KV_HEADS = 2
GQA_GROUP = N_Q_HEADS // N_KV_HEADS
ATTN_WIDTH = N_Q_HEADS * HEAD_DIM
KV_WIDTH = N_KV_HEADS * HEAD_DIM
ROPE_DIM = HEAD_DIM // 4
ROPE_THETA = 500000.0
CMP_LEN = 32
CMP_STRIDE = 16
CMP_HIDDEN = 4 * HEAD_DIM
SEL_BLOCK = 64
N_SEL = 16
N_LOCAL = 2
WINDOW = 512
SSM_HEADS = 8
SSM_HEAD_DIM = 64
SSM_WIDTH = SSM_HEADS * SSM_HEAD_DIM
SSM_GROUPS = 2
SSM_STATE = 128
CONV_WIDTH = 4
CONV_DIM = SSM_WIDTH + 2 * SSM_GROUPS * SSM_STATE
SSD_CHUNK = 128
MIX_WIDTH = ATTN_WIDTH + SSM_WIDTH
N_EXPERTS = 64
N_EXPERT_GROUPS = 8
TOPK_GROUPS = 4
TOP_K = 8
D_EXPERT = 256
D_SHARED = 256
ROUTED_SCALE = 2.5
IN_SIZES = (ATTN_WIDTH, 6 * KV_WIDTH, 3 * N_Q_HEADS, SSM_WIDTH, CONV_DIM, SSM_HEADS)
N_IN = sum(IN_SIZES)
EPS = 1e-6
NEG = -1e30
BIG = 1e6

LANES = 128
VMEM_LIMIT = 56 * 1024 * 1024

BF16 = jnp.bfloat16
F32 = jnp.float32
LOG2E = math.log2(math.e)


def _cparams(sem, flags=None):
    return pltpu.CompilerParams(dimension_semantics=sem, vmem_limit_bytes=VMEM_LIMIT, flags=flags)


def _silu(x):
    return x * jax.nn.sigmoid(x)


def _dot(a, b):
    return jnp.dot(a, b, preferred_element_type=F32)


def _dot_nt(a, b):
    return lax.dot_general(a, b, (((1,), (1,)), ((), ())), preferred_element_type=F32)


def _mod_spec(mod, col, tm, tiles_per_b, row0):
    if mod.ndim == 3:
        return pl.BlockSpec((1, 1, D_MODEL), lambda i, *_: (row0 + i // tiles_per_b, 0, col))
    return pl.BlockSpec((tm, D_MODEL), lambda i, *_: (i, col))


def _mod(ref):
    return ref[0] if len(ref.shape) == 3 else ref[...]


def _adaln_kernel(c_ref, w_ref, b_ref, o_ref):
    c = c_ref[...]
    a = _silu(c).astype(BF16)
    o_ref[...] = _dot(a, w_ref[...].astype(BF16)) + b_ref[...]


def adaln_all(c_all, w_ada, b_ada):
    rows = c_all.shape[0]
    n = w_ada.shape[1]
    tn = 1024
    return pl.pallas_call(
        _adaln_kernel,
        grid=(n // tn,),
        in_specs=[
            pl.BlockSpec((rows, D_MODEL), lambda j: (0, 0)),
            pl.BlockSpec((D_MODEL, tn), lambda j: (0, j)),
            pl.BlockSpec((1, tn), lambda j: (0, j)),
        ],
        out_specs=pl.BlockSpec((rows, tn), lambda j: (0, j)),
        out_shape=jax.ShapeDtypeStruct((rows, n), F32),
        compiler_params=_cparams(("arbitrary",)),
        name="adaln",
    )(c_all, w_ada, b_ada.reshape(1, n))


_C_Q = 0
_C_KV = _C_Q + ATTN_WIDTH
_C_Z = _C_KV + 6 * KV_WIDTH
_C_XBC = _C_Z + SSM_WIDTH
_C_MISC = _C_XBC + CONV_DIM
N_IN_PAD = _C_MISC + LANES
N_GATES = 3 * N_Q_HEADS


def _prep_w_in(w_in):
    s = np.cumsum((0,) + IN_SIZES)
    q, kv, g, z, xbc, dt = (w_in[:, int(s[i]):int(s[i + 1])] for i in range(6))
    pad = jnp.zeros((w_in.shape[0], LANES - N_GATES - SSM_HEADS), w_in.dtype)
    return jnp.concatenate([q, kv, z, xbc, dt, g, pad], axis=1).astype(BF16)


def _group_mean_matrix(width):
    i = np.arange(width)
    m = (i[:, None] // HEAD_DIM == i[None, :] // HEAD_DIM).astype(np.float32) / HEAD_DIM
    return jnp.asarray(m, BF16)


def _rope_tables(pos):
    half = ROPE_DIM // 2
    inv_freq = ROPE_THETA ** (-jnp.arange(half, dtype=F32) / half)
    ang = pos.astype(F32)[:, None] * inv_freq[None, :]
    cos, sin = jnp.cos(ang), jnp.sin(ang)
    t = pos.shape[0]
    one = jnp.ones((t, HEAD_DIM - ROPE_DIM), F32)
    zero = jnp.zeros((t, HEAD_DIM - ROPE_DIM), F32)
    zh = jnp.zeros((t, half), F32)
    c = jnp.concatenate([cos, cos, one], axis=1)
    s_up = jnp.concatenate([-sin, zh, zero], axis=1)
    s_dn = jnp.concatenate([zh, sin, zero], axis=1)
    rep = LANES // HEAD_DIM
    return jnp.tile(c, (1, rep)), jnp.tile(s_up, (1, rep)), jnp.tile(s_dn, (1, rep))


def _rope(x, c, s_up, s_dn):
    w = x.shape[1]
    half = ROPE_DIM // 2
    rep = w // LANES
    ct = jnp.concatenate([c] * rep, axis=1) if rep > 1 else c
    su = jnp.concatenate([s_up] * rep, axis=1) if rep > 1 else s_up
    sd = jnp.concatenate([s_dn] * rep, axis=1) if rep > 1 else s_dn
    up = pltpu.roll(x, w - half, axis=1)
    dn = pltpu.roll(x, half, axis=1)
    return x * ct + up * su + dn * sd


def _stride_block_store(stage_ref, cmpx_ref, n_rows):
    nb = n_rows // CMP_STRIDE
    lane = lax.broadcasted_iota(jnp.int32, (nb, KV_WIDTH), 1)
    lo = lane < HEAD_DIM
    span = CMP_STRIDE * HEAD_DIM
    for s in range(2):
        for m in range(CMP_STRIDE // 2):
            r0 = stage_ref[s, pl.ds(2 * m, nb, stride=CMP_STRIDE), :]
            r1 = stage_ref[s, pl.ds(2 * m + 1, nb, stride=CMP_STRIDE), :]
            head0 = jnp.where(lo, r0, pltpu.roll(r1, HEAD_DIM, axis=1))
            head1 = jnp.where(lo, pltpu.roll(r0, HEAD_DIM, axis=1), r1)
            for h, piece in enumerate((head0, head1)):
                c0 = (2 * s + h) * span + m * KV_WIDTH
                cmpx_ref[0, :, c0:c0 + KV_WIDTH] = piece.astype(BF16)


def _inproj_kernel(x_ref, shift_ref, scale_ref, nw_ref, w_ref, qw_ref, kw_ref, gq_ref, gk_ref,
                   c_ref, su_ref, sd_ref,
                   qp_ref, kvb_ref, win_ref, winb_ref, z_ref, xbc_ref, misc_ref, *rest, seq_layout):
    x = x_ref[...]
    ms = jnp.mean(x * x, axis=-1, keepdims=True)
    h = x * lax.rsqrt(ms + EPS) * nw_ref[...]
    h = h * (1.0 + _mod(scale_ref)) + _mod(shift_ref)
    hb = h.astype(BF16)
    c, su, sd = c_ref[...], su_ref[...], sd_ref[...]

    q = _dot(hb, w_ref[:, _C_Q:_C_Q + ATTN_WIDTH])
    qms = _dot((q * q).astype(BF16), gq_ref[...])
    q = q * lax.rsqrt(qms + EPS) * qw_ref[...]
    q = _rope(q, c, su, sd) * (HEAD_DIM ** -0.5 * LOG2E)
    lane = lax.broadcasted_iota(jnp.int32, q.shape, 1) % LANES
    lo = lane < HEAD_DIM
    q_up = pltpu.roll(q, ATTN_WIDTH - HEAD_DIM, axis=1)
    q_dn = pltpu.roll(q, HEAD_DIM, axis=1)
    zero = jnp.zeros_like(q)
    nat_lo = jnp.where(lo, q, zero)
    nat_hi = jnp.where(lo, zero, q)
    up_lo = jnp.where(lo, q_up, zero)
    dn_hi = jnp.where(lo, zero, q_dn)
    blocks = []
    for hd in range(N_Q_HEADS):
        pair = hd // 2
        sl = slice(pair * LANES, (pair + 1) * LANES)
        if hd < GQA_GROUP:
            blocks.append((nat_lo if hd % 2 == 0 else up_lo)[:, sl])
        else:
            blocks.append((dn_hi if hd % 2 == 0 else nat_hi)[:, sl])
    qp_ref[...] = jnp.concatenate(blocks, axis=1).astype(BF16)

    kv = _dot(hb, w_ref[:, _C_KV:_C_KV + 6 * KV_WIDTH])
    outs = []
    for br in range(3):
        k = kv[:, br * 2 * KV_WIDTH:br * 2 * KV_WIDTH + KV_WIDTH]
        v = kv[:, br * 2 * KV_WIDTH + KV_WIDTH:(br + 1) * 2 * KV_WIDTH]
        kms = _dot((k * k).astype(BF16), gk_ref[...])
        k = k * lax.rsqrt(kms + EPS) * kw_ref[:, br * KV_WIDTH:(br + 1) * KV_WIDTH]
        k = _rope(k, c, su, sd)
        outs += [k, v]
    kvrows = jnp.concatenate(outs[:4], axis=1)
    winrows = jnp.concatenate(outs[4:], axis=1)
    kvb_ref[...] = kvrows.astype(BF16)
    win_ref[...] = winrows
    winb_ref[...] = winrows.astype(BF16)
    if seq_layout:
        kvt_ref, cmpx_ref, stage_ref = rest
        tm = kvrows.shape[0]
        for r in range(4):
            kvt_ref[0, r] = jnp.transpose(kvrows[:, r * KV_WIDTH:(r + 1) * KV_WIDTH])
        for s in range(2):
            stage_ref[s] = kvrows[:, s * KV_WIDTH:(s + 1) * KV_WIDTH]
        _stride_block_store(stage_ref, cmpx_ref, tm)
    else:
        rest[0][...] = kvrows

    z_ref[...] = _dot(hb, w_ref[:, _C_Z:_C_Z + SSM_WIDTH])
    xbc_ref[...] = _dot(hb, w_ref[:, _C_XBC:_C_XBC + CONV_DIM])
    misc_ref[...] = _dot(hb, w_ref[:, _C_MISC:_C_MISC + LANES])


def inproj(x, mod3, mod_row0, norm_w, wp, q_norm_w, k_norm_w, pos, tm, seq_layout):
    b, t, d = x.shape
    n = b * t
    tiles_per_b = t // tm
    xf = x.reshape(n, d)
    c, su, sd = _rope_tables(pos)
    qw = jnp.tile(q_norm_w, N_Q_HEADS).reshape(1, ATTN_WIDTH)
    kw = jnp.concatenate([jnp.tile(k_norm_w[i], N_KV_HEADS) for i in range(3)]).reshape(1, 3 * KV_WIDTH)
    gq = _group_mean_matrix(ATTN_WIDTH)
    gk = _group_mean_matrix(KV_WIDTH)

    def mod_spec(col):
        return _mod_spec(mod3, col, tm, tiles_per_b, mod_row0)

    def tok(wd):
        return pl.BlockSpec((tm, wd), lambda i: (i, 0))

    def full(a):
        return pl.BlockSpec(a.shape, lambda i: (0,) * a.ndim)

    rope_spec = pl.BlockSpec((tm, LANES), lambda i: (i % tiles_per_b, 0))
    out_shape = [
        jax.ShapeDtypeStruct((n, N_Q_HEADS * LANES), BF16),
        jax.ShapeDtypeStruct((n, 4 * KV_WIDTH), BF16),
        jax.ShapeDtypeStruct((n, 2 * KV_WIDTH), F32),
        jax.ShapeDtypeStruct((n, 2 * KV_WIDTH), BF16),
        jax.ShapeDtypeStruct((n, SSM_WIDTH), F32),
        jax.ShapeDtypeStruct((n, CONV_DIM), F32),
        jax.ShapeDtypeStruct((n, LANES), F32),
    ]
    out_specs = [tok(s.shape[1]) for s in out_shape]
    scratch = []
    if seq_layout:
        out_shape += [jax.ShapeDtypeStruct((b, 4, KV_WIDTH, t), F32),
                      jax.ShapeDtypeStruct((b, t // CMP_STRIDE, CMP_STRIDE * 2 * KV_WIDTH), BF16)]
        out_specs += [pl.BlockSpec((1, 4, KV_WIDTH, tm), lambda i: (i // tiles_per_b, 0, 0, i % tiles_per_b)),
                      pl.BlockSpec((1, tm // CMP_STRIDE, CMP_STRIDE * 2 * KV_WIDTH),
                                   lambda i: (i // tiles_per_b, i % tiles_per_b, 0))]
        scratch = [pltpu.VMEM((2, tm, KV_WIDTH), F32)]
    else:
        out_shape += [jax.ShapeDtypeStruct((n, 4 * KV_WIDTH), F32)]
        out_specs += [tok(4 * KV_WIDTH)]
    return pl.pallas_call(
        functools.partial(_inproj_kernel, seq_layout=seq_layout),
        grid=(n // tm,),
        in_specs=[tok(d), mod_spec(0), mod_spec(1), full(norm_w), full(wp), full(qw), full(kw), full(gq), full(gk),
                  rope_spec, rope_spec, rope_spec],
        out_specs=tuple(out_specs),
        out_shape=tuple(out_shape),
        scratch_shapes=scratch,
        compiler_params=_cparams(("arbitrary",)),
        name="inproj",
    )(xf, mod3, mod3, norm_w, wp, qw, kw, gq, gk, c, su, sd)


def _prep_compress(cmp_pe, cmp_w1, cmp_w2):
    span = CMP_STRIDE * HEAD_DIM
    w1p = jnp.concatenate([cmp_w1[:, :span], cmp_w1[:, span:]], axis=2).astype(BF16)
    pep = cmp_pe.reshape(2, 2, span)
    eye = jnp.eye(N_KV_HEADS, dtype=F32)
    w2p = jnp.einsum("poe,hg->phoge", cmp_w2, eye).reshape(2, N_KV_HEADS, CMP_HIDDEN, KV_WIDTH).astype(BF16)
    return w1p, pep, w2p


def _compress_kernel(x_ref, w1_ref, pe_ref, w2_ref, o_ref):
    part = pl.program_id(1)
    nb = x_ref.shape[1]
    span = CMP_STRIDE * HEAD_DIM
    pe = pe_ref[0]
    out = jnp.zeros((nb, KV_WIDTH), F32)
    for h in range(N_KV_HEADS):
        xk = x_ref[0, :, h * span:(h + 1) * span]
        xv = x_ref[0, :, (N_KV_HEADS + h) * span:(N_KV_HEADS + h + 1) * span]
        x = jnp.where(part == 0, xk, xv).astype(F32)
        u = _dot((x + pe[0:1]).astype(BF16), w1_ref[0, :, :CMP_HIDDEN])
        v = _dot((x + pe[1:2]).astype(BF16), w1_ref[0, :, CMP_HIDDEN:])
        h1 = u + pltpu.roll(v, nb - 1, axis=0)
        out = out + _dot(_silu(h1).astype(BF16), w2_ref[0, h])
    row = lax.broadcasted_iota(jnp.int32, out.shape, 0)
    o_ref[0, 0] = jnp.where(row < nb - 1, out, 0.0).astype(o_ref.dtype)


def compress(x, w1p, pep, w2p):
    b, nb, width = x.shape
    return pl.pallas_call(
        _compress_kernel,
        grid=(b, 2),
        in_specs=[
            pl.BlockSpec((1, nb, width), lambda i, p: (i, 0, 0)),
            pl.BlockSpec((1,) + w1p.shape[1:], lambda i, p: (p, 0, 0)),
            pl.BlockSpec((1,) + pep.shape[1:], lambda i, p: (p, 0, 0)),
            pl.BlockSpec((1,) + w2p.shape[1:], lambda i, p: (p, 0, 0, 0)),
        ],
        out_specs=pl.BlockSpec((1, 1, nb, KV_WIDTH), lambda i, p: (i, p, 0, 0)),
        out_shape=jax.ShapeDtypeStruct((b, 2, nb, KV_WIDTH), BF16),
        compiler_params=_cparams(("arbitrary", "arbitrary")),
        name="compress",
    )(x, w1p, pep, w2p)


N_SEL_LANES = LANES


def _cover_matrix(nb):
    c = np.arange(nb)[:, None]
    j = np.arange(N_SEL_LANES)[None, :]
    start = c * CMP_STRIDE
    m = (start < (j + 1) * SEL_BLOCK) & (start + CMP_LEN > j * SEL_BLOCK)
    return jnp.asarray(m.astype(np.float32), BF16)


def _place_heads(res, kv):
    lane = lax.broadcasted_iota(jnp.int32, res[0].shape, 1)
    lo = lane < HEAD_DIM
    blocks = []
    for pair in range(GQA_GROUP // 2):
        a, b = res[2 * pair], res[2 * pair + 1]
        if kv == 0:
            blocks.append(jnp.where(lo, a, pltpu.roll(b, HEAD_DIM, axis=1)))
        else:
            blocks.append(jnp.where(lo, pltpu.roll(a, HEAD_DIM, axis=1), b))
    return jnp.concatenate(blocks, axis=1)


def _group_rows(q_ref, kv):
    heads = range(kv * GQA_GROUP, (kv + 1) * GQA_GROUP)
    return jnp.concatenate([q_ref[0, :, hd * LANES:(hd + 1) * LANES] for hd in heads], axis=0)


def _heads_from_transposed(out_t, tq, kv):
    out = jnp.transpose(out_t)
    return _place_heads([out[g * tq:(g + 1) * tq] for g in range(GQA_GROUP)], kv)


def _cmp_select_kernel(q_ref, kc_ref, vc_ref, covt_ref, o_ref, m_ref, *, q_off, n_pick):
    tq = q_ref.shape[1]
    rows = GQA_GROUP * tq
    nb = kc_ref.shape[2]
    wl = max(tq, LANES)
    assert tq % LANES == 0 or rows == LANES
    t0 = q_off + pl.program_id(1) * tq
    kc = kc_ref[0, 0]
    vc = vc_ref[0, 0]
    qpos = t0 + lax.broadcasted_iota(jnp.int32, (nb, rows), 1) % tq
    cend = lax.broadcasted_iota(jnp.int32, (nb, rows), 0) * CMP_STRIDE + (CMP_LEN - 1)
    valid = cend <= qpos
    blk = lax.broadcasted_iota(jnp.int32, (N_SEL_LANES, wl), 0)
    cur = (t0 + lax.broadcasted_iota(jnp.int32, (N_SEL_LANES, wl), 1) % tq) // SEL_BLOCK
    forced = (blk == 0) | ((blk <= cur) & (blk > cur - N_LOCAL))
    o_groups = []
    for kv in range(N_KV_HEADS):
        s = _dot_nt(kc, _group_rows(q_ref, kv))
        s = jnp.where(valid, s, NEG)
        e = jnp.exp2(s - jnp.max(s, axis=0, keepdims=True))
        p = e / jnp.sum(e, axis=0, keepdims=True)
        p = jnp.where(valid, p, 0.0)
        o_t = lax.dot_general(vc, p.astype(BF16), (((0,), (0,)), ((), ())), preferred_element_type=F32)
        o_groups.append(_heads_from_transposed(o_t, tq, kv))
        if tq % LANES == 0:
            psum = sum(p[:, g * tq:(g + 1) * tq] for g in range(GQA_GROUP))
        else:
            psum = p + sum(pltpu.roll(p, g * tq, axis=1) for g in range(1, GQA_GROUP))
        hi, lo = _split2(psum)
        imp = _dot(covt_ref[...], hi) + _dot(covt_ref[...], lo)
        x = jnp.where(forced, BIG, jnp.where(blk > cur, -BIG, imp))
        sel = jnp.zeros(x.shape, jnp.bool_)
        for _ in range(n_pick):
            mx = jnp.max(x, axis=0, keepdims=True)
            idx = jnp.min(jnp.where(x == mx, blk, N_SEL_LANES), axis=0, keepdims=True)
            hit = blk == idx
            sel = sel | hit
            x = jnp.where(hit, -jnp.inf, x)
        mneg = jnp.transpose(jnp.where(sel, 0.0, NEG))
        m_ref[0, kv] = mneg[:tq].astype(m_ref.dtype)
    o_ref[0] = jnp.concatenate(o_groups, axis=1)


def cmp_select(qp, kcv, q_off, n_pick, tq):
    b, t, _ = qp.shape
    nb = kcv.shape[2]
    cover = jnp.transpose(_cover_matrix(nb))
    return pl.pallas_call(
        functools.partial(_cmp_select_kernel, q_off=q_off, n_pick=n_pick),
        grid=(b, t // tq),
        in_specs=[
            pl.BlockSpec((1, tq, N_Q_HEADS * LANES), lambda i, j: (i, j, 0)),
            pl.BlockSpec((1, 1, nb, KV_WIDTH), lambda i, j: (i, 0, 0, 0)),
            pl.BlockSpec((1, 1, nb, KV_WIDTH), lambda i, j: (i, 1, 0, 0)),
            pl.BlockSpec((N_SEL_LANES, nb), lambda i, j: (0, 0)),
        ],
        out_specs=(
            pl.BlockSpec((1, tq, ATTN_WIDTH), lambda i, j: (i, j, 0)),
            pl.BlockSpec((1, N_KV_HEADS, tq, N_SEL_LANES), lambda i, j: (i, 0, j, 0)),
        ),
        out_shape=(
            jax.ShapeDtypeStruct((b, t, ATTN_WIDTH), F32),
            jax.ShapeDtypeStruct((b, N_KV_HEADS, t, N_SEL_LANES), BF16),
        ),
        compiler_params=_cparams(("arbitrary", "arbitrary")),
        name="cmp_select",
    )(qp, kcv, kcv, cover)


SEL_TILE_ELEMS = 512 * 512
WIN_CHUNK = 256


def _block_onehot(s):
    key = np.arange(s)[:, None]
    j = np.arange(N_SEL_LANES)[None, :]
    return jnp.asarray((key // SEL_BLOCK == j).astype(np.float32), BF16)


def _gate_expand():
    m = np.zeros((3, LANES, ATTN_WIDTH), np.float32)
    for br in range(3):
        for hd in range(N_Q_HEADS):
            m[br, SSM_HEADS + 3 * hd + br, hd * HEAD_DIM:(hd + 1) * HEAD_DIM] = 1.0
    return jnp.asarray(m, BF16)


def _flash_update(ss, v, m_ref, acc_ref):
    lane = lax.broadcasted_iota(jnp.int32, v.shape, 1)
    one = jnp.ones(v.shape, v.dtype)
    stage = []
    for k, s in enumerate(ss):
        m_old = m_ref[k]
        m_new = jnp.maximum(m_old, jnp.max(s, axis=0, keepdims=True))
        alpha = jnp.exp2(m_old - m_new)
        p = jnp.exp2(s - m_new)
        m_ref[k] = m_new
        stage.append((alpha, p.astype(BF16)))
    for k, (alpha, p) in enumerate(stage):
        vk = jnp.where((lane < HEAD_DIM) == (k == 0), v, one)
        pv = lax.dot_general(vk, p, (((0,), (0,)), ((), ())), preferred_element_type=F32)
        acc_ref[k] = alpha * acc_ref[k] + pv


def _sel_chunk(rows, n_keys):
    chunk = SEL_TILE_ELEMS // rows
    while n_keys % chunk:
        chunk //= 2
    return chunk


def _sel_win_kernel(q_ref, mneg_ref, ksel_ref, vsel_ref, et_ref, kwin_ref, vwin_ref, ocmp_ref, misc_ref, eg_ref,
                    o_ref, lhs_ref, m_ref, acc_ref, *, q_off, win_pos0):
    tq = q_ref.shape[1]
    rows = GQA_GROUP * tq
    SEL_CHUNK = _sel_chunk(rows, ksel_ref.shape[1])
    t0 = q_off + pl.program_id(1) * tq
    n_sel = lax.shift_right_logical(t0 + tq - 1, int(math.log2(SEL_CHUNK))) + 1
    w_lo = jnp.maximum(t0 - (WINDOW - 1) - win_pos0, 0) // WIN_CHUNK
    w_hi = (t0 + tq - 1 - win_pos0) // WIN_CHUNK + 1

    def qrow(n_keys):
        return lax.broadcasted_iota(jnp.int32, (n_keys, rows), 1) % tq + t0

    def init():
        m_ref[...] = jnp.full(m_ref.shape, NEG, F32)
        acc_ref[...] = jnp.zeros(acc_ref.shape, F32)

    def finish():
        outs = []
        for kv in range(N_KV_HEADS):
            acc = acc_ref[kv]
            denom_row = HEAD_DIM * (1 - kv)
            outs.append(_heads_from_transposed(acc / acc[denom_row:denom_row + 1, :], tq, kv))
        return jnp.concatenate(outs, axis=1)

    for kv in range(N_KV_HEADS):
        for g in range(GQA_GROUP):
            hd = kv * GQA_GROUP + g
            lhs_ref[kv, g * tq:(g + 1) * tq, :LANES] = q_ref[0, :, hd * LANES:(hd + 1) * LANES]
            lhs_ref[kv, g * tq:(g + 1) * tq, LANES:] = mneg_ref[0, kv]

    init()

    def sel_step(c, carry, causal):
        r0 = pl.multiple_of(c * SEL_CHUNK, SEL_CHUNK)
        rhs = jnp.concatenate([ksel_ref[0, pl.ds(r0, SEL_CHUNK), :], et_ref[pl.ds(r0, SEL_CHUNK), :]], axis=1)
        v = vsel_ref[0, pl.ds(r0, SEL_CHUNK), :]
        if causal:
            ok = r0 + lax.broadcasted_iota(jnp.int32, (SEL_CHUNK, rows), 0) <= qrow(SEL_CHUNK)
        ss = [_dot_nt(rhs, lhs_ref[kv]) for kv in range(N_KV_HEADS)]
        if causal:
            ss = [jnp.where(ok, s, NEG) for s in ss]
        _flash_update(ss, v, m_ref, acc_ref)
        return carry

    n_full = lax.shift_right_logical(t0 + 1, int(math.log2(SEL_CHUNK)))
    lax.fori_loop(0, n_full, functools.partial(sel_step, causal=False), 0)
    lax.fori_loop(n_full, n_sel, functools.partial(sel_step, causal=True), 0)
    o_sel = finish()

    init()

    def win_step(c, carry):
        r0 = pl.multiple_of(c * WIN_CHUNK, WIN_CHUNK)
        k = kwin_ref[0, pl.ds(r0, WIN_CHUNK), :]
        v = vwin_ref[0, pl.ds(r0, WIN_CHUNK), :]
        wpos = win_pos0 + r0 + lax.broadcasted_iota(jnp.int32, (WIN_CHUNK, rows), 0)
        qr = qrow(WIN_CHUNK)
        ok = (wpos <= qr) & (wpos > qr - WINDOW)
        ss = [jnp.where(ok, _dot_nt(k, lhs_ref[kv, :, :LANES]), NEG) for kv in range(N_KV_HEADS)]
        _flash_update(ss, v, m_ref, acc_ref)
        return carry

    lax.fori_loop(w_lo, w_hi, win_step, 0)
    o_win = finish()

    gates = jax.nn.sigmoid(misc_ref[0])
    ghi = gates.astype(BF16)
    glo = (gates - ghi.astype(F32)).astype(BF16)
    branches = (ocmp_ref[0], o_sel, o_win)
    out = jnp.zeros(branches[0].shape, F32)
    for br in range(3):
        out = out + (_dot(ghi, eg_ref[br]) + _dot(glo, eg_ref[br])) * branches[br]
    o_ref[0] = out


def sel_win_attention(qp, mneg, kvb, sel_col, winb, o_cmp, misc, q_off, win_pos0, tq):
    b, t, _ = qp.shape
    s = kvb.shape[1]
    sw = winb.shape[1]
    et = _block_onehot(s)
    eg = _gate_expand()
    rows = GQA_GROUP * tq
    assert q_off + t <= s and q_off + t - win_pos0 <= sw and sw % WIN_CHUNK == 0
    return pl.pallas_call(
        functools.partial(_sel_win_kernel, q_off=q_off, win_pos0=win_pos0),
        grid=(b, t // tq),
        in_specs=[
            pl.BlockSpec((1, tq, N_Q_HEADS * LANES), lambda i, j: (i, j, 0)),
            pl.BlockSpec((1, N_KV_HEADS, tq, N_SEL_LANES), lambda i, j: (i, 0, j, 0)),
            pl.BlockSpec((1, s, KV_WIDTH), lambda i, j: (i, 0, sel_col)),
            pl.BlockSpec((1, s, KV_WIDTH), lambda i, j: (i, 0, sel_col + 1)),
            pl.BlockSpec((s, N_SEL_LANES), lambda i, j: (0, 0)),
            pl.BlockSpec((1, sw, KV_WIDTH), lambda i, j: (i, 0, 0)),
            pl.BlockSpec((1, sw, KV_WIDTH), lambda i, j: (i, 0, 1)),
            pl.BlockSpec((1, tq, ATTN_WIDTH), lambda i, j: (i, j, 0)),
            pl.BlockSpec((1, tq, LANES), lambda i, j: (i, j, 0)),
            pl.BlockSpec((3, LANES, ATTN_WIDTH), lambda i, j: (0, 0, 0)),
        ],
        out_specs=pl.BlockSpec((1, tq, ATTN_WIDTH), lambda i, j: (i, j, 0)),
        out_shape=jax.ShapeDtypeStruct((b, t, ATTN_WIDTH), F32),
        scratch_shapes=[
            pltpu.VMEM((N_KV_HEADS, rows, 2 * LANES), BF16),
            pltpu.VMEM((N_KV_HEADS, 1, rows), F32),
            pltpu.VMEM((N_KV_HEADS, LANES, rows), F32),
        ],
        compiler_params=_cparams(("arbitrary", "arbitrary")),
        name="sel_win_attention",
    )(qp, mneg, kvb, kvb, et, winb, winb, o_cmp, misc, eg)


CONV_PAD = 8
HEAD_PAIRS = SSM_HEADS // 2


def _split3(x):
    a = x.astype(BF16)
    r = x - a.astype(F32)
    b = r.astype(BF16)
    c = (r - b.astype(F32)).astype(BF16)
    return a, b, c


def _ssd_kernel(xbc_ref, z_ref, misc_ref, conv0_ref, h0_ref, cw_ref, cb_ref, dtb_ref, a_ref, dsk_ref, nw_ref,
                y_ref, hout_ref, cout_ref, xp_ref, h_ref, ms_ref, *, t_valid):
    ch = pl.program_id(1)
    L = SSD_CHUNK
    keep = CONV_WIDTH - 1

    @pl.when(ch == 0)
    def _():
        xp_ref[...] = jnp.zeros(xp_ref.shape, F32)
        xp_ref[CONV_PAD - keep:CONV_PAD, :] = conv0_ref[0]
        h_ref[...] = h0_ref[0]

    xp_ref[CONV_PAD:CONV_PAD + t_valid, :] = xbc_ref[0]
    conv = cb_ref[...]
    for j in range(CONV_WIDTH):
        conv = conv + cw_ref[j:j + 1, :] * xp_ref[CONV_PAD - keep + j:CONV_PAD - keep + j + L, :]
    last = xp_ref[CONV_PAD + t_valid - keep:CONV_PAD + t_valid, :]
    cout_ref[0] = last
    xp_ref[CONV_PAD - keep:CONV_PAD, :] = last
    xc = _silu(conv)

    row = lax.broadcasted_iota(jnp.int32, (L, LANES), 0)
    lane = lax.broadcasted_iota(jnp.int32, (L, LANES), 1)
    if t_valid == L:
        raw = misc_ref[0]
    else:
        ms_ref[...] = jnp.zeros(ms_ref.shape, F32)
        ms_ref[0:t_valid, :] = misc_ref[0]
        raw = ms_ref[...]
    v = raw + dtb_ref[...]
    dt = jnp.maximum(v, 0.0) + jnp.log(1.0 + jnp.exp(-jnp.abs(v)))
    dt = jnp.where((lane < SSM_HEADS) & (row < t_valid), dt, 0.0)
    da = dt * a_ref[...]
    tri = (lax.broadcasted_iota(jnp.int32, (L, L), 1) <= lax.broadcasted_iota(jnp.int32, (L, L), 0))
    trib = tri.astype(BF16)
    acum = sum(_dot(trib, part) for part in _split3(da))
    acum_t = jnp.transpose(acum)
    dt_t = jnp.transpose(dt)
    e_acum = jnp.exp(acum)
    e_last = jnp.exp(acum[L - 1:L, :])
    w_end = jnp.exp(acum[L - 1:L, :] - acum) * dt
    lo = lane < SSM_HEAD_DIM

    ys = []
    for pair in range(HEAD_PAIRS):
        grp = (2 * pair) // (SSM_HEADS // SSM_GROUPS)
        bg = xc[:, SSM_WIDTH + grp * SSM_STATE:SSM_WIDTH + (grp + 1) * SSM_STATE].astype(BF16)
        cg = xc[:, SSM_WIDTH + (SSM_GROUPS + grp) * SSM_STATE:SSM_WIDTH + (SSM_GROUPS + grp + 1) * SSM_STATE].astype(BF16)
        g = _dot_nt(cg, bg)
        xpair = xc[:, pair * LANES:(pair + 1) * LANES]
        y = jnp.zeros((L, LANES), F32)
        for sub in range(2):
            hd = 2 * pair + sub
            seg = acum[:, hd:hd + 1] - acum_t[hd:hd + 1, :]
            m = g * jnp.exp(jnp.where(tri, seg, NEG)) * dt_t[hd:hd + 1, :]
            xm = jnp.where(lo if sub == 0 else ~lo, xpair, 0.0)
            y = y + _dot(m.astype(BF16), xm.astype(BF16))
        col = lambda a: jnp.where(lo, a[:, 2 * pair:2 * pair + 1], a[:, 2 * pair + 1:2 * pair + 2])
        hp = h_ref[pair]
        y = y + _dot_nt(cg, hp.astype(BF16)) * col(e_acum)
        y = y + col(dsk_ref[...]) * xpair
        xw = (xpair * col(w_end)).astype(BF16)
        st = lax.dot_general(xw, bg, (((0,), (0,)), ((), ())), preferred_element_type=F32)
        prow = lax.broadcasted_iota(jnp.int32, (LANES, LANES), 0) < SSM_HEAD_DIM
        dec = jnp.where(prow, e_last[:, 2 * pair:2 * pair + 1], e_last[:, 2 * pair + 1:2 * pair + 2])
        h_ref[pair] = hp * dec + st
        ys.append(y)
    y = jnp.concatenate(ys, axis=1)
    if t_valid != L:
        y = y[:t_valid]
    y = y * _silu(z_ref[0])
    y = y * lax.rsqrt(jnp.mean(y * y, axis=-1, keepdims=True) + EPS) * nw_ref[...]
    y_ref[0] = y

    @pl.when(ch == pl.num_programs(1) - 1)
    def _():
        hout_ref[0] = h_ref[...]


def ssd(xbc, z, misc, conv0, h0, conv_w, conv_b, dt_bias, a_log, d_skip, norm_w):
    b, t, _ = xbc.shape
    L = SSD_CHUNK
    t_valid = L if t % L == 0 else t
    assert t_valid == L or t < L
    n_ch = max(t // L, 1)
    keep = CONV_WIDTH - 1
    pad8 = lambda v: jnp.pad(v.astype(F32), (0, LANES - SSM_HEADS)).reshape(1, LANES)
    dtb = pad8(dt_bias)
    a = pad8(-jnp.exp(a_log.astype(F32)))
    dsk = pad8(d_skip)
    h0p = h0.reshape(b, HEAD_PAIRS, 2 * SSM_HEAD_DIM, SSM_STATE)
    full = lambda arr: pl.BlockSpec(arr.shape, lambda i, c: (0,) * arr.ndim)
    tok = lambda wd: pl.BlockSpec((1, t_valid, wd), lambda i, c: (i, c, 0))
    y, hout, cout = pl.pallas_call(
        functools.partial(_ssd_kernel, t_valid=t_valid),
        grid=(b, n_ch),
        in_specs=[
            tok(CONV_DIM), tok(SSM_WIDTH), tok(LANES),
            pl.BlockSpec((1, keep, CONV_DIM), lambda i, c: (i, 0, 0)),
            pl.BlockSpec((1, HEAD_PAIRS, 2 * SSM_HEAD_DIM, SSM_STATE), lambda i, c: (i, 0, 0, 0)),
            full(conv_w), pl.BlockSpec((1, CONV_DIM), lambda i, c: (0, 0)),
            full(dtb), full(a), full(dsk), pl.BlockSpec((1, SSM_WIDTH), lambda i, c: (0, 0)),
        ],
        out_specs=(
            tok(SSM_WIDTH),
            pl.BlockSpec((1, HEAD_PAIRS, 2 * SSM_HEAD_DIM, SSM_STATE), lambda i, c: (i, 0, 0, 0)),
            pl.BlockSpec((1, keep, CONV_DIM), lambda i, c: (i, 0, 0)),
        ),
        out_shape=(
            jax.ShapeDtypeStruct((b, t, SSM_WIDTH), F32),
            jax.ShapeDtypeStruct((b, HEAD_PAIRS, 2 * SSM_HEAD_DIM, SSM_STATE), F32),
            jax.ShapeDtypeStruct((b, keep, CONV_DIM), F32),
        ),
        scratch_shapes=[
            pltpu.VMEM((CONV_PAD + L, CONV_DIM), F32),
            pltpu.VMEM((HEAD_PAIRS, 2 * SSM_HEAD_DIM, SSM_STATE), F32),
            pltpu.VMEM((L, LANES), F32),
        ],
        compiler_params=_cparams(("arbitrary", "arbitrary")),
        name="ssd",
    )(xbc, z, misc, conv0, h0p, conv_w, conv_b.reshape(1, CONV_DIM), dtb, a, dsk, norm_w.reshape(1, SSM_WIDTH))
    return y, hout.reshape(b, SSM_HEADS, SSM_HEAD_DIM, SSM_STATE), cout


def _split2(x):
    hi = x.astype(BF16)
    return hi, (x - hi.astype(F32)).astype(BF16)


def _merge_kernel(oa_ref, ys_ref, x_ref, g1_ref, sh2_ref, sc2_ref, anw_ref, wo_ref, n2w_ref, wrh_ref, wrl_ref,
                  x1_ref, h2_ref, lg_ref):
    oa = oa_ref[...]
    a = oa * lax.rsqrt(jnp.mean(oa * oa, axis=-1, keepdims=True) + EPS) * anw_ref[...]
    cat = jnp.concatenate([a.astype(BF16), ys_ref[...].astype(BF16)], axis=1)
    x1 = x_ref[...] + _mod(g1_ref) * _dot(cat, wo_ref[...])
    x1_ref[...] = x1
    h2 = x1 * lax.rsqrt(jnp.mean(x1 * x1, axis=-1, keepdims=True) + EPS) * n2w_ref[...]
    h2 = h2 * (1.0 + _mod(sc2_ref)) + _mod(sh2_ref)
    h2_ref[...] = h2.astype(BF16)
    hh, hl = _split2(h2)
    lg_ref[...] = _dot_nt(wrh_ref[...], hh) + _dot_nt(wrh_ref[...], hl) + _dot_nt(wrl_ref[...], hh)


def merge(o_attn, y_ssm, x, mod3, mod_row0, attn_norm_w, wo, norm2_w, w_router, tm):
    b, t, d = x.shape
    n = b * t
    tiles_per_b = t // tm
    wrt = jnp.transpose(w_router)
    wrh, wrl = _split2(wrt)

    def mod_spec(col):
        return _mod_spec(mod3, col, tm, tiles_per_b, mod_row0)

    tok = lambda wd: pl.BlockSpec((tm, wd), lambda i: (i, 0))
    full = lambda a: pl.BlockSpec(a.shape, lambda i: (0,) * a.ndim)
    return pl.pallas_call(
        _merge_kernel,
        grid=(n // tm,),
        in_specs=[tok(ATTN_WIDTH), tok(SSM_WIDTH), tok(d), mod_spec(2), mod_spec(3), mod_spec(4),
                  full(attn_norm_w), full(wo), full(norm2_w), full(wrh), full(wrl)],
        out_specs=(tok(d), tok(d), pl.BlockSpec((N_EXPERTS, tm), lambda i: (0, i))),
        out_shape=(jax.ShapeDtypeStruct((n, d), F32), jax.ShapeDtypeStruct((n, d), BF16),
                   jax.ShapeDtypeStruct((N_EXPERTS, n), F32)),
        compiler_params=_cparams(("arbitrary",)),
        name="merge",
    )(o_attn.reshape(n, ATTN_WIDTH), y_ssm.reshape(n, SSM_WIDTH), x.reshape(n, d), mod3, mod3, mod3,
      attn_norm_w, wo, norm2_w, wrh, wrl)


EXPERTS_PER_GROUP = N_EXPERTS // N_EXPERT_GROUPS


def _first_max(x, ids, axes, n_ids):
    mx = jnp.max(x, axis=axes, keepdims=True)
    return ids == jnp.min(jnp.where(x == mx, ids, n_ids), axis=axes, keepdims=True), mx


def _route_kernel(lg_ref, eb_ref, tri_ref, w_ref, pos_ref, cnt_ref):
    lg = lg_ref[...]
    tn = lg.shape[2]
    scores = jax.nn.sigmoid(lg)
    biased = scores + eb_ref[...]
    sub = lax.broadcasted_iota(jnp.int32, lg.shape, 1)
    grp = lax.broadcasted_iota(jnp.int32, (N_EXPERT_GROUPS, 1, tn), 0)
    eid = lax.broadcasted_iota(jnp.int32, lg.shape, 0) * EXPERTS_PER_GROUP + sub
    hit, m1 = _first_max(biased, sub, 1, EXPERTS_PER_GROUP)
    m2 = jnp.max(jnp.where(hit, -jnp.inf, biased), axis=1, keepdims=True)
    gs = m1 + m2
    keep = jnp.zeros(gs.shape, jnp.bool_)
    for _ in range(TOPK_GROUPS):
        hit, _m = _first_max(gs, grp, 0, N_EXPERT_GROUPS)
        keep = keep | hit
        gs = jnp.where(hit, -jnp.inf, gs)
    x = jnp.where(keep, biased, NEG)
    sel = jnp.zeros(lg.shape, jnp.bool_)
    for _ in range(TOP_K):
        hit, _m = _first_max(x, eid, (0, 1), N_EXPERTS)
        sel = sel | hit
        x = jnp.where(hit, -jnp.inf, x)
    w = jnp.where(sel, scores, 0.0)
    w = w / jnp.sum(w, axis=(0, 1), keepdims=True) * ROUTED_SCALE
    w_ref[...] = w
    selb = sel.astype(BF16).reshape(N_EXPERTS, tn)
    pos = _dot(selb, tri_ref[...])
    pos_ref[...] = jnp.where(sel, pos.reshape(lg.shape), -1.0)
    cnt = jnp.sum(sel.astype(F32), axis=2, keepdims=True)
    cnt_ref[0] = jnp.broadcast_to(cnt, cnt_ref.shape[1:]).astype(jnp.int32)


def route(logits_t, e_bias, tn):
    n = logits_t.shape[1]
    lg3 = logits_t.reshape(N_EXPERT_GROUPS, EXPERTS_PER_GROUP, n)
    eb = e_bias.astype(F32).reshape(N_EXPERT_GROUPS, EXPERTS_PER_GROUP, 1)
    tri = jnp.asarray(np.triu(np.ones((tn, tn), np.float32), 1), BF16)
    blk = pl.BlockSpec((N_EXPERT_GROUPS, EXPERTS_PER_GROUP, tn), lambda i: (0, 0, i))
    w, pos, cnt = pl.pallas_call(
        _route_kernel,
        grid=(n // tn,),
        in_specs=[blk, pl.BlockSpec(eb.shape, lambda i: (0, 0, 0)), pl.BlockSpec((tn, tn), lambda i: (0, 0))],
        out_specs=(blk, blk, pl.BlockSpec((1, N_EXPERT_GROUPS, EXPERTS_PER_GROUP, LANES), lambda i: (i, 0, 0, 0))),
        out_shape=(jax.ShapeDtypeStruct(lg3.shape, F32), jax.ShapeDtypeStruct(lg3.shape, F32),
                   jax.ShapeDtypeStruct((n // tn, N_EXPERT_GROUPS, EXPERTS_PER_GROUP, LANES), jnp.int32)),
        compiler_params=_cparams(("arbitrary",)),
        name="route",
    )(lg3, eb, tri)
    return w.reshape(N_EXPERTS, n), pos.reshape(N_EXPERTS, n), cnt[..., 0].reshape(n // tn, N_EXPERTS)


MOE_ROWS = 128


def _swiglu(xb, wgu, wd, width):
    gu = _dot(xb, wgu)
    act = _silu(gu[:, :width]) * gu[:, width:]
    return _dot(act.astype(BF16), wd)


MOE_EXPERTS_PER_STEP = 4


MOE_ALIGN = 16
MOE_GATHER_ROWS = 896


def _moe_slots(tm):
    worst = TOP_K * tm + N_EXPERTS * (MOE_ALIGN - 1) + MOE_ROWS
    return -(-worst // MOE_GATHER_ROWS) * MOE_GATHER_ROWS


def _moe_kernel(cnt_ref, start_ref, h2_ref, w_ref, pos_ref, x1_ref, g2_ref, wgu_ref, wd_ref, sgu_ref, sd_ref,
                o_ref, g_all, xs):
    i = pl.program_id(0)
    es = pl.program_id(1)
    tm = h2_ref.shape[0]
    slots = g_all.shape[0]
    slot = lax.broadcasted_iota(jnp.int32, (MOE_ROWS, tm), 0).astype(F32)
    row = lax.broadcasted_iota(jnp.int32, (MOE_ROWS, 1), 0)

    def n_windows(cnt):
        return (cnt + MOE_ROWS - 1) // MOE_ROWS

    def window_start(e, j):
        return pl.multiple_of(start_ref[i * N_EXPERTS + e] + j * MOE_ROWS, MOE_ALIGN)

    @pl.when(es == 0)
    def _():
        g_all[...] = jnp.zeros(g_all.shape, BF16)

        def mark(e, carry):
            pos = pos_ref[pl.ds(e, 1), :]

            def mark_window(j, carry):
                hit = pos == slot + (j * MOE_ROWS).astype(F32)
                g_all[pl.ds(window_start(e, j), MOE_ROWS), :] = hit.astype(BF16)
                return carry

            return lax.fori_loop(0, n_windows(cnt_ref[i * N_EXPERTS + e]), mark_window, carry)

        lax.fori_loop(0, N_EXPERTS, mark, 0)

        def gather(c, carry):
            r0 = pl.multiple_of(c * MOE_GATHER_ROWS, MOE_GATHER_ROWS)
            rows = _dot(g_all[pl.ds(r0, MOE_GATHER_ROWS), :], h2_ref[...])
            xs[pl.ds(r0, MOE_GATHER_ROWS), :] = rows.astype(BF16)
            return carry

        lax.fori_loop(0, slots // MOE_GATHER_ROWS, gather, 0)

    for q in range(MOE_EXPERTS_PER_STEP):
        e = es * MOE_EXPERTS_PER_STEP + q
        cnt = cnt_ref[i * N_EXPERTS + e]
        wrow = w_ref[pl.ds(e, 1), :]

        def window(j, carry, q=q, e=e, cnt=cnt, wrow=wrow):
            r0 = window_start(e, j)
            xg = xs[pl.ds(r0, MOE_ROWS), :]
            out = _swiglu(xg, wgu_ref[q].astype(BF16), wd_ref[q].astype(BF16), D_EXPERT)
            g = g_all[pl.ds(r0, MOE_ROWS), :].astype(F32)
            out = out * jnp.sum(g * wrow, axis=1, keepdims=True)
            mine = row < cnt - j * MOE_ROWS
            xs[pl.ds(r0, MOE_ROWS), :] = jnp.where(mine, out.astype(BF16), xg)
            return carry

        lax.fori_loop(0, n_windows(cnt), window, 0)

    @pl.when(es == pl.num_programs(1) - 1)
    def _():
        y = lax.dot_general(g_all[...], xs[...], (((0,), (0,)), ((), ())), preferred_element_type=F32)
        y = y + _swiglu(h2_ref[...], sgu_ref[...], sd_ref[...], D_SHARED)
        o_ref[...] = x1_ref[...] + _mod(g2_ref) * y


def moe(h2, w_t, pos_t, counts, x1, mod3, mod_row0, t_per_b, wgu, wd, sgu, sd, tm):
    n, d = h2.shape
    tiles_per_b = t_per_b // tm
    eps = MOE_EXPERTS_PER_STEP
    slots = _moe_slots(tm)
    padded = (counts + MOE_ALIGN - 1) // MOE_ALIGN * MOE_ALIGN
    starts = jnp.cumsum(padded, axis=1) - padded
    grid_spec = pltpu.PrefetchScalarGridSpec(
        num_scalar_prefetch=2,
        grid=(n // tm, N_EXPERTS // eps),
        in_specs=[
            pl.BlockSpec((tm, d), lambda i, e, *_: (i, 0)),
            pl.BlockSpec((N_EXPERTS, tm), lambda i, e, *_: (0, i)),
            pl.BlockSpec((N_EXPERTS, tm), lambda i, e, *_: (0, i)),
            pl.BlockSpec((tm, d), lambda i, e, *_: (i, 0)),
            _mod_spec(mod3, 5, tm, tiles_per_b, mod_row0),
            pl.BlockSpec((eps, d, 2 * D_EXPERT), lambda i, e, *_: (e, 0, 0)),
            pl.BlockSpec((eps, D_EXPERT, d), lambda i, e, *_: (e, 0, 0)),
            pl.BlockSpec(sgu.shape, lambda i, e, *_: (0, 0)),
            pl.BlockSpec(sd.shape, lambda i, e, *_: (0, 0)),
        ],
        out_specs=pl.BlockSpec((tm, d), lambda i, e, *_: (i, 0)),
        scratch_shapes=[pltpu.VMEM((slots, tm), BF16), pltpu.VMEM((slots, d), BF16)],
    )
    return pl.pallas_call(
        _moe_kernel,
        grid_spec=grid_spec,
        out_shape=jax.ShapeDtypeStruct((n, d), F32),
        compiler_params=_cparams(("arbitrary", "arbitrary")),
        name="moe",
    )(counts.reshape(-1), starts.reshape(-1).astype(jnp.int32), h2, w_t, pos_t, x1, mod3, wgu, wd, sgu, sd)


SC_WINDOW = 128
PACK_W = 256
MOE_BLOCK_ROWS = 512
HI_MASK = -65536


def _pack_pair(x):
    bits = pltpu.bitcast(x.astype(BF16).astype(F32), jnp.int32)
    return lax.shift_right_logical(bits[:, :PACK_W], 16) | (bits[:, PACK_W:] & HI_MASK)


def _unpack_pair(word):
    lo = pltpu.bitcast(lax.shift_left(word, 16), F32)
    hi = pltpu.bitcast(word & HI_MASK, F32)
    return jnp.concatenate([lo, hi], axis=1)


def _pack_kernel(x_ref, a_ref, b_ref):
    x = x_ref[...]
    a_ref[...] = _pack_pair(x[:, :2 * PACK_W])
    b_ref[...] = _pack_pair(x[:, 2 * PACK_W:])


def pack_rows(x, tm):
    n, d = x.shape
    tok = lambda wd: pl.BlockSpec((tm, wd), lambda i: (i, 0))
    return pl.pallas_call(
        _pack_kernel, grid=(n // tm,), in_specs=[tok(d)], out_specs=(tok(PACK_W), tok(PACK_W)),
        out_shape=(jax.ShapeDtypeStruct((n, PACK_W), jnp.int32),) * 2,
        compiler_params=_cparams(("arbitrary",)), name="pack_rows",
    )(x)


def _slots_kernel(w_ref, pos_ref, base_ref, tri_ref, slot_ref, wt_ref):
    w = w_ref[...]
    pos = pos_ref[...]
    sel = pos >= 0.0
    rank = _dot(tri_ref[...], sel.astype(BF16))
    dest = base_ref[0] + pos
    slots, wts = [], []
    for j in range(TOP_K):
        mine = sel & (rank == float(j))
        slots.append(jnp.sum(jnp.where(mine, dest, 0.0), axis=0, keepdims=True))
        wts.append(jnp.sum(jnp.where(mine, w, 0.0), axis=0, keepdims=True))
    slot_ref[...] = jnp.concatenate(slots, axis=0).astype(jnp.int32)
    wpad = jnp.concatenate(wts + [jnp.zeros((LANES - TOP_K, w.shape[1]), F32)], axis=0)
    wt_ref[...] = jnp.transpose(wpad)


def slots_of(w_t, pos_t, base, tn):
    n = w_t.shape[1]
    tri = jnp.asarray(np.tril(np.ones((N_EXPERTS, N_EXPERTS), np.float32), -1), BF16)
    blk = pl.BlockSpec((N_EXPERTS, tn), lambda i: (0, i))
    return pl.pallas_call(
        _slots_kernel, grid=(n // tn,),
        in_specs=[blk, blk, pl.BlockSpec((1, N_EXPERTS, 1), lambda i: (i, 0, 0)),
                  pl.BlockSpec((N_EXPERTS, N_EXPERTS), lambda i: (0, 0))],
        out_specs=(pl.BlockSpec((TOP_K, tn), lambda i: (0, i)), pl.BlockSpec((tn, LANES), lambda i: (i, 0))),
        out_shape=(jax.ShapeDtypeStruct((TOP_K, n), jnp.int32), jax.ShapeDtypeStruct((n, LANES), F32)),
        compiler_params=_cparams(("arbitrary",)), name="moe_slots",
    )(w_t, pos_t, base, tri)


def sc_scatter_rows(rows, idx, n_out):
    n, d = rows.shape
    m = idx.shape[0]
    nb = n // SC_WINDOW
    mesh = plsc.VectorSubcoreMesh(core_axis_name="core", subcore_axis_name="subcore")

    @functools.partial(pl.kernel, out_type=jax.ShapeDtypeStruct((n_out, d), rows.dtype), mesh=mesh)
    def scatter(x_hbm, i_hbm, o_hbm):
        def body(x_vmem, i_vmem):
            pltpu.sync_copy(x_vmem, o_hbm.at[i_vmem.at[0]])

        pltpu.emit_pipeline(
            body, grid=(m // SC_WINDOW,),
            in_specs=[pl.BlockSpec((SC_WINDOW, d), index_map=lambda i: (i % nb, 0)),
                      pl.BlockSpec((1, SC_WINDOW), index_map=lambda i: (0, i))],
            out_specs=[], core_axis_name=("core", "subcore"), dimension_semantics=(pltpu.PARALLEL,),
        )(x_hbm, i_hbm)

    return scatter(rows, idx.reshape(1, m))


def sc_gather_rows(table, idx):
    d = table.shape[1]
    m = idx.shape[0]
    mesh = plsc.VectorSubcoreMesh(core_axis_name="core", subcore_axis_name="subcore")

    @functools.partial(pl.kernel, out_type=jax.ShapeDtypeStruct((m, d), table.dtype), mesh=mesh)
    def gather(x_hbm, i_hbm, o_hbm):
        def body(i_vmem, o_vmem):
            pltpu.sync_copy(x_hbm.at[i_vmem.at[0]], o_vmem)

        pltpu.emit_pipeline(
            body, grid=(m // SC_WINDOW,),
            in_specs=[pl.BlockSpec((1, SC_WINDOW), index_map=lambda i: (0, i))],
            out_specs=[pl.BlockSpec((SC_WINDOW, d), index_map=lambda i: (i, 0))],
            core_axis_name=("core", "subcore"), dimension_semantics=(pltpu.PARALLEL,),
        )(i_hbm, o_hbm)

    return gather(table, idx.reshape(1, m))


def _experts_kernel(be_ref, nu_ref, xa_ref, xb_ref, wgu_ref, wd_ref, oa_ref, ob_ref, wgu_bf, wd_bf):
    b = pl.program_id(0)

    @pl.when(b < nu_ref[0])
    def _():
        @pl.when((b == 0) | (be_ref[b] != be_ref[jnp.maximum(b - 1, 0)]))
        def _():
            wgu_bf[...] = wgu_ref[0].astype(BF16)
            wd_bf[...] = wd_ref[0].astype(BF16)

        x = jnp.concatenate([_unpack_pair(xa_ref[...]), _unpack_pair(xb_ref[...])], axis=1).astype(BF16)
        out = _swiglu(x, wgu_bf[...], wd_bf[...], D_EXPERT)
        oa_ref[...] = _pack_pair(out[:, :2 * PACK_W])
        ob_ref[...] = _pack_pair(out[:, 2 * PACK_W:])


def experts_sorted(xa, xb, block_expert, n_used, wgu, wd):
    r = xa.shape[0]
    d = wd.shape[2]
    row = lambda b, be, nu: (jnp.minimum(b, nu[0] - 1), 0)
    blk = pl.BlockSpec((MOE_BLOCK_ROWS, PACK_W), row)
    grid_spec = pltpu.PrefetchScalarGridSpec(
        num_scalar_prefetch=2, grid=(r // MOE_BLOCK_ROWS,),
        in_specs=[blk, blk,
                  pl.BlockSpec((1, d, 2 * D_EXPERT), lambda b, be, nu: (be[b], 0, 0)),
                  pl.BlockSpec((1, D_EXPERT, d), lambda b, be, nu: (be[b], 0, 0))],
        out_specs=(blk, blk),
        scratch_shapes=[pltpu.VMEM((d, 2 * D_EXPERT), BF16), pltpu.VMEM((D_EXPERT, d), BF16)],
    )
    return pl.pallas_call(
        _experts_kernel, grid_spec=grid_spec,
        out_shape=(jax.ShapeDtypeStruct((r, PACK_W), jnp.int32),) * 2,
        compiler_params=_cparams(("arbitrary",)), name="moe_experts",
    )(block_expert, n_used, xa, xb, wgu, wd)


def _combine_kernel(ya_ref, yb_ref, wt_ref, h2_ref, x1_ref, g2_ref, sgu_ref, sd_ref, o_ref):
    wt = wt_ref[...]
    acc = _swiglu(h2_ref[...], sgu_ref[...], sd_ref[...], D_SHARED)
    for j in range(TOP_K):
        y = jnp.concatenate([_unpack_pair(ya_ref[j]), _unpack_pair(yb_ref[j])], axis=1)
        acc = acc + wt[:, j:j + 1] * y
    o_ref[...] = x1_ref[...] + _mod(g2_ref) * acc


def combine_sorted(ya, yb, wt, h2, x1, mod3, mod_row0, t_per_b, sgu, sd, tm):
    n, d = h2.shape
    tiles_per_b = t_per_b // tm
    tok = lambda wd: pl.BlockSpec((tm, wd), lambda i: (i, 0))
    yblk = pl.BlockSpec((TOP_K, tm, PACK_W), lambda i: (0, i, 0))
    full = lambda a: pl.BlockSpec(a.shape, lambda i: (0,) * a.ndim)
    return pl.pallas_call(
        _combine_kernel, grid=(n // tm,),
        in_specs=[yblk, yblk, tok(LANES), tok(d), tok(d), _mod_spec(mod3, 5, tm, tiles_per_b, mod_row0),
                  full(sgu), full(sd)],
        out_specs=tok(d), out_shape=jax.ShapeDtypeStruct((n, d), F32),
        compiler_params=_cparams(("arbitrary",)), name="moe_combine",
    )(ya, yb, wt, h2, x1, mod3, sgu, sd)


def moe_sorted(h2, w_t, pos_t, counts, x1, mod3, mod_row0, t_per_b, wgu, wd, sgu, sd, tm, overlap):
    n, d = h2.shape
    assert d == 4 * PACK_W and n % SC_WINDOW == 0
    n_blocks = (TOP_K * n + N_EXPERTS * (MOE_BLOCK_ROWS - 1)) // MOE_BLOCK_ROWS
    total = jnp.sum(counts, axis=0)
    region = (total + MOE_BLOCK_ROWS - 1) // MOE_BLOCK_ROWS * MOE_BLOCK_ROWS
    region_end = jnp.cumsum(region)
    base = (region_end - region)[None, :] + jnp.cumsum(counts, axis=0) - counts
    block_row0 = jnp.arange(n_blocks, dtype=region_end.dtype) * MOE_BLOCK_ROWS
    block_expert = jnp.sum(region_end[None, :] <= block_row0[:, None], axis=1)
    block_expert = jnp.minimum(block_expert, N_EXPERTS - 1).astype(jnp.int32)
    n_used = (region_end[-1:] // MOE_BLOCK_ROWS).astype(jnp.int32)
    slot, wt = slots_of(w_t, pos_t, base.astype(F32).reshape(-1, N_EXPERTS, 1), tm)
    dest = slot.reshape(-1)
    ha, hb = pack_rows(h2, tm)
    rows = n_blocks * MOE_BLOCK_ROWS
    xa, xb = sc_scatter_rows(ha, dest, rows), sc_scatter_rows(hb, dest, rows)
    n_used, rest = lax.optimization_barrier((n_used, overlap()))
    oa, ob = experts_sorted(xa, xb, block_expert, n_used, wgu, wd)
    ya = sc_gather_rows(oa, dest).reshape(TOP_K, n, PACK_W)
    yb = sc_gather_rows(ob, dest).reshape(TOP_K, n, PACK_W)
    return combine_sorted(ya, yb, wt, h2, x1, mod3, mod_row0, t_per_b, sgu, sd, tm), rest


GATHER_PAGES = 8


def _gather_kernel(pt_ref, *refs):
    pages, new_ref = refs[:GATHER_PAGES], refs[GATHER_PAGES]
    rows_ref, cmpx_ref, stage_ref = refs[GATHER_PAGES + 1:]
    step = pl.program_id(1)
    last = pl.num_programs(1) - 1
    n_rows = GATHER_PAGES * PAGE_SIZE

    @pl.when(step < last)
    def _():
        for k in range(GATHER_PAGES):
            sl = slice(k * PAGE_SIZE, (k + 1) * PAGE_SIZE)
            for r in range(4):
                tile = jnp.transpose(pages[k][0, r])
                if r < 2:
                    stage_ref[r, sl, :] = tile
                else:
                    rows_ref[0, sl, (r - 2) * KV_WIDTH:(r - 1) * KV_WIDTH] = tile.astype(BF16)

    @pl.when(step == last)
    def _():
        new = new_ref[0]
        tn = new.shape[0]
        stage_ref[...] = jnp.zeros(stage_ref.shape, F32)
        for s in range(2):
            stage_ref[s, 0:tn, :] = new[:, s * KV_WIDTH:(s + 1) * KV_WIDTH]
        pad = jnp.zeros((n_rows - tn, 2 * KV_WIDTH), F32)
        rows_ref[0] = jnp.concatenate([new[:, 2 * KV_WIDTH:], pad], axis=0).astype(BF16)

    _stride_block_store(stage_ref, cmpx_ref, n_rows)


def gather_pages(cache_t, page_table, new_rows):
    b, n_pages = page_table.shape
    steps = n_pages // GATHER_PAGES
    rows = GATHER_PAGES * PAGE_SIZE
    s_out = (steps + 1) * rows

    def page_spec(k):
        def idx(i, s, pt):
            p = jnp.minimum(s, steps - 1) * GATHER_PAGES + k
            return (pt[i * n_pages + p], 0, 0, 0)
        return pl.BlockSpec((1, 4, KV_WIDTH, PAGE_SIZE), idx)

    grid_spec = pltpu.PrefetchScalarGridSpec(
        num_scalar_prefetch=1,
        grid=(b, steps + 1),
        in_specs=[page_spec(k) for k in range(GATHER_PAGES)]
        + [pl.BlockSpec((1,) + new_rows.shape[1:], lambda i, s, pt: (i, 0, 0))],
        out_specs=(
            pl.BlockSpec((1, rows, 2 * KV_WIDTH), lambda i, s, pt: (i, s, 0)),
            pl.BlockSpec((1, rows // CMP_STRIDE, CMP_STRIDE * 2 * KV_WIDTH), lambda i, s, pt: (i, s, 0)),
        ),
        scratch_shapes=[pltpu.VMEM((2, rows, KV_WIDTH), F32)],
    )
    return pl.pallas_call(
        _gather_kernel,
        grid_spec=grid_spec,
        out_shape=(jax.ShapeDtypeStruct((b, s_out, 2 * KV_WIDTH), BF16),
                   jax.ShapeDtypeStruct((b, s_out // CMP_STRIDE, CMP_STRIDE * 2 * KV_WIDTH), BF16)),
        compiler_params=_cparams(("arbitrary", "arbitrary")),
        name="gather_pages",
    )(page_table.reshape(-1), *([cache_t] * GATHER_PAGES), new_rows)


def _attention(qp, cmpx, kvb, sel_col, winb, misc, cmp_w, q_off, win_pos0, tq):
    t = qp.shape[1]
    cur_lo, cur_hi = q_off // SEL_BLOCK, (q_off + t - 1) // SEL_BLOCK
    assert cur_hi < N_SEL_LANES or (cur_lo == cur_hi == N_SEL_LANES), (q_off, t)
    n_pick = N_SEL - (1 if cur_hi >= N_SEL_LANES else 0)
    kcv = compress(cmpx, *cmp_w)
    o_cmp, mneg = cmp_select(qp, kcv, q_off, n_pick, tq)
    return sel_win_attention(qp, mneg, kvb, sel_col, winb, o_cmp, misc, q_off, win_pos0, tq)


def kernel(x_prompt, x_sample, cache_kv, cache_win, state_ssm, state_conv, page_table, c_prompt, c_sample, w_ada, b_ada, norm1_w, norm2_w, w_in, q_norm_w, k_norm_w, cmp_pe, cmp_w1, cmp_w2, attn_out_norm_w, conv_w, conv_b, dt_bias, a_log, d_skip, ssm_norm_w, w_out, w_router, e_bias, w_exp_gu, w_exp_down, w_sh_gu, w_sh_down):
    xp, xq = x_prompt, x_sample
    bp, tp, d = xp.shape
    bq, tq, _ = xq.shape
    depth = w_ada.shape[0]
    past_len = page_table.shape[1] * PAGE_SIZE
    nq = bq * tq
    tq_pad = LANES // GQA_GROUP
    assert tp % 512 == 0 and tp >= WINDOW and nq % 8 == 0 and tq <= tq_pad
    pos_p = jnp.arange(tp, dtype=jnp.int32)
    pos_q = jnp.tile(past_len + jnp.arange(tq, dtype=jnp.int32), bq)
    c_all = jnp.concatenate([c_prompt, c_sample], axis=0)
    c_all = jnp.pad(c_all, ((0, -c_all.shape[0] % 8), (0, 0)))
    outs = [[] for _ in range(8)]
    for l in range(depth):
        mod = adaln_all(c_all, w_ada[l], b_ada[l])
        mod_p = mod.reshape(mod.shape[0], 1, 6 * d)
        mod_q = jnp.repeat(mod[bp:bp + bq], tq, axis=0)
        wp = _prep_w_in(w_in[l])
        cmp_w = _prep_compress(cmp_pe[l], cmp_w1[l], cmp_w2[l])
        wo = w_out[l].astype(BF16)
        wgu, wd = w_exp_gu[l], w_exp_down[l]
        sgu, sd = w_sh_gu[l].astype(BF16), w_sh_down[l].astype(BF16)
        ssm_w = (conv_w[l], conv_b[l], dt_bias[l], a_log[l], d_skip[l], ssm_norm_w[l])
        n1w, n2w, anw = norm1_w[l:l + 1], norm2_w[l:l + 1], attn_out_norm_w[l:l + 1]

        qp, kvb, win, winb, z, xbc, misc, kvt, cmpx = inproj(xp, mod_p, 0, n1w, wp, q_norm_w[l], k_norm_w[l], pos_p,
                                                            512, True)
        r3 = lambda a: a.reshape(bp, tp, a.shape[-1])
        o_attn = _attention(r3(qp), cmpx, r3(kvb), 2, r3(winb), r3(misc), cmp_w, 0, 0, 128)
        y_ssm, h_new, conv_new = ssd(r3(xbc), r3(z), r3(misc), jnp.zeros((bp, CONV_WIDTH - 1, CONV_DIM), F32),
                                     jnp.zeros((bp, SSM_HEADS, SSM_HEAD_DIM, SSM_STATE), F32), *ssm_w)
        x1, h2, lg = merge(o_attn, y_ssm, xp, mod_p, 0, anw, wo, n2w, w_router[l], 512)
        w_t, pos_t, cnt = route(lg, e_bias[l], 512)
        outs[0].append(jnp.transpose(kvt.reshape(bp, 4, N_KV_HEADS, HEAD_DIM, tp), (0, 4, 1, 2, 3)))
        outs[1].append(win.reshape(bp, tp, 2, N_KV_HEADS, HEAD_DIM)[:, tp - WINDOW:])
        outs[2].append(h_new)
        outs[3].append(conv_new)

        xq1 = xq.reshape(1, nq, d)
        rq = lambda a: a.reshape(bq, tq, a.shape[-1])
        padq = lambda a: jnp.pad(rq(a), ((0, 0), (0, tq_pad - tq), (0, 0)))

        def sample_front():
            proj = inproj(xq1, mod_q, 0, n1w, wp, q_norm_w[l], k_norm_w[l], pos_q, nq, False)
            cache_t = jnp.transpose(cache_kv[l], (0, 2, 3, 4, 1)).reshape(cache_kv.shape[1], 4, KV_WIDTH, PAGE_SIZE)
            past, cmpx = gather_pages(cache_t, page_table, rq(proj[-1]))
            return proj, past, cmpx

        xp, (proj, past, cmpx) = moe_sorted(h2, w_t, pos_t, cnt, x1, mod_p, 0, tp, wgu, wd, sgu, sd, 512, sample_front)
        xp = xp.reshape(bp, tp, d)
        qp, kvb, win, winb, z, xbc, misc, kv = proj
        win_all = jnp.concatenate([cache_win[l].reshape(bq, WINDOW, 2 * KV_WIDTH).astype(BF16), rq(winb),
                                   jnp.zeros((bq, -(WINDOW + tq_pad) % WIN_CHUNK + tq_pad - tq, 2 * KV_WIDTH), BF16)],
                                  axis=1)
        o_attn = _attention(padq(qp), cmpx, past, 0, win_all, padq(misc), cmp_w, past_len, past_len - WINDOW,
                            tq_pad)[:, :tq]
        y_ssm, h_new, conv_new = ssd(rq(xbc), rq(z), rq(misc), state_conv[l], state_ssm[l], *ssm_w)
        x1, h2, lg = merge(o_attn.reshape(1, nq, ATTN_WIDTH), y_ssm.reshape(1, nq, SSM_WIDTH), xq1, mod_q, 0,
                           anw, wo, n2w, w_router[l], nq)
        w_t, pos_t, cnt = route(lg, e_bias[l], nq)
        xq = moe(h2, w_t, pos_t, cnt, x1, mod_q, 0, nq, wgu, wd, sgu, sd, nq).reshape(bq, tq, d)
        win_rows = win.reshape(bq, tq, 2, N_KV_HEADS, HEAD_DIM)
        outs[4].append(kv.reshape(bq, tq, 4, N_KV_HEADS, HEAD_DIM))
        outs[5].append(jnp.concatenate([cache_win[l], win_rows.astype(cache_win.dtype)], axis=1)[:, tq:])
        outs[6].append(h_new)
        outs[7].append(conv_new)
    return (xp, xq) + tuple(jnp.stack(o) for o in outs)
```

```python
import functools
import math

import jax
import jax.numpy as jnp
import numpy as np
from jax import lax
from jax.experimental import pallas as pl
from jax.experimental.pallas import tpu as pltpu
from jax.experimental.pallas import tpu_sc as plsc

D_MODEL = 1024
PAGE_SIZE = 128
HEAD_DIM = 64
N_Q_HEADS = 8
N_KV_HEADS = 2
GQA_GROUP = N_Q_HEADS // N_KV_HEADS
ATTN_WIDTH = N_Q_HEADS * HEAD_DIM
KV_WIDTH = N_KV_HEADS * HEAD_DIM
ROPE_DIM = HEAD_DIM // 4
ROPE_THETA = 500000.0
CMP_LEN = 32
CMP_STRIDE = 16
CMP_HIDDEN = 4 * HEAD_DIM
SEL_BLOCK = 64
N_SEL = 16
N_LOCAL = 2
WINDOW = 512
SSM_HEADS = 8
SSM_HEAD_DIM = 64
SSM_WIDTH = SSM_HEADS * SSM_HEAD_DIM
SSM_GROUPS = 2
SSM_STATE = 128
CONV_WIDTH = 4
CONV_DIM = SSM_WIDTH + 2 * SSM_GROUPS * SSM_STATE
SSD_CHUNK = 128
MIX_WIDTH = ATTN_WIDTH + SSM_WIDTH
N_EXPERTS = 64
N_EXPERT_GROUPS = 8
TOPK_GROUPS = 4
TOP_K = 8
D_EXPERT = 256
D_SHARED = 256
ROUTED_SCALE = 2.5
IN_SIZES = (ATTN_WIDTH, 6 * KV_WIDTH, 3 * N_Q_HEADS, SSM_WIDTH, CONV_DIM, SSM_HEADS)
N_IN = sum(IN_SIZES)
EPS = 1e-6
NEG = -1e30
BIG = 1e6

LANES = 128
TOKEN_TILE = 512
QUERY_TILE = 128
VMEM_LIMIT = 56 * 1024 * 1024

BF16 = jnp.bfloat16
F32 = jnp.float32
LOG2E = math.log2(math.e)


def _cparams(sem, flags=None):
    return pltpu.CompilerParams(dimension_semantics=sem, vmem_limit_bytes=VMEM_LIMIT, flags=flags)


def _silu(x):
    return x * jax.nn.sigmoid(x)


def _dot(a, b):
    return jnp.dot(a, b, preferred_element_type=F32)


def _dot_nt(a, b):
    return lax.dot_general(a, b, (((1,), (1,)), ((), ())), preferred_element_type=F32)


def _mod_spec(mod, col, tm, tiles_per_b, row0):
    if mod.ndim == 3:
        return pl.BlockSpec((1, 1, D_MODEL), lambda i, *_: (row0 + i // tiles_per_b, 0, col))
    return pl.BlockSpec((tm, D_MODEL), lambda i, *_: (i, col))


def _mod(ref):
    return ref[0] if len(ref.shape) == 3 else ref[...]


def _adaln_kernel(c_ref, w_ref, b_ref, o_ref):
    c = c_ref[...]
    a = _silu(c).astype(BF16)
    o_ref[...] = _dot(a, w_ref[...].astype(BF16)) + b_ref[...]


def adaln_all(c_all, w_ada, b_ada):
    rows = c_all.shape[0]
    n = w_ada.shape[1]
    tn = 1024
    return pl.pallas_call(
        _adaln_kernel,
        grid=(n // tn,),
        in_specs=[
            pl.BlockSpec((rows, D_MODEL), lambda j: (0, 0)),
            pl.BlockSpec((D_MODEL, tn), lambda j: (0, j)),
            pl.BlockSpec((1, tn), lambda j: (0, j)),
        ],
        out_specs=pl.BlockSpec((rows, tn), lambda j: (0, j)),
        out_shape=jax.ShapeDtypeStruct((rows, n), F32),
        compiler_params=_cparams(("arbitrary",)),
        name="adaln",
    )(c_all, w_ada, b_ada.reshape(1, n))


_C_Q = 0
_C_KV = _C_Q + ATTN_WIDTH
_C_Z = _C_KV + 6 * KV_WIDTH
_C_XBC = _C_Z + SSM_WIDTH
_C_MISC = _C_XBC + CONV_DIM
N_IN_PAD = _C_MISC + LANES
N_GATES = 3 * N_Q_HEADS


def _prep_w_in(w_in):
    s = np.cumsum((0,) + IN_SIZES)
    q, kv, g, z, xbc, dt = (w_in[:, int(s[i]):int(s[i + 1])] for i in range(6))
    pad = jnp.zeros((w_in.shape[0], LANES - N_GATES - SSM_HEADS), w_in.dtype)
    return jnp.concatenate([q, kv, z, xbc, dt, g, pad], axis=1).astype(BF16)


def _group_mean_matrix(width):
    i = np.arange(width)
    m = (i[:, None] // HEAD_DIM == i[None, :] // HEAD_DIM).astype(np.float32) / HEAD_DIM
    return jnp.asarray(m, BF16)


def _rope_tables(pos):
    half = ROPE_DIM // 2
    inv_freq = ROPE_THETA ** (-jnp.arange(half, dtype=F32) / half)
    ang = pos.astype(F32)[:, None] * inv_freq[None, :]
    cos, sin = jnp.cos(ang), jnp.sin(ang)
    t = pos.shape[0]
    one = jnp.ones((t, HEAD_DIM - ROPE_DIM), F32)
    zero = jnp.zeros((t, HEAD_DIM - ROPE_DIM), F32)
    zh = jnp.zeros((t, half), F32)
    c = jnp.concatenate([cos, cos, one], axis=1)
    s_up = jnp.concatenate([-sin, zh, zero], axis=1)
    s_dn = jnp.concatenate([zh, sin, zero], axis=1)
    rep = LANES // HEAD_DIM
    return jnp.tile(c, (1, rep)), jnp.tile(s_up, (1, rep)), jnp.tile(s_dn, (1, rep))


def _rope(x, c, s_up, s_dn):
    w = x.shape[1]
    half = ROPE_DIM // 2
    rep = w // LANES
    ct = jnp.concatenate([c] * rep, axis=1) if rep > 1 else c
    su = jnp.concatenate([s_up] * rep, axis=1) if rep > 1 else s_up
    sd = jnp.concatenate([s_dn] * rep, axis=1) if rep > 1 else s_dn
    up = pltpu.roll(x, w - half, axis=1)
    dn = pltpu.roll(x, half, axis=1)
    return x * ct + up * su + dn * sd


def _stride_block_store(stage_ref, cmpx_ref, n_rows):
    nb = n_rows // CMP_STRIDE
    lane = lax.broadcasted_iota(jnp.int32, (nb, KV_WIDTH), 1)
    lo = lane < HEAD_DIM
    span = CMP_STRIDE * HEAD_DIM
    for s in range(2):
        for m in range(CMP_STRIDE // 2):
            r0 = stage_ref[s, pl.ds(2 * m, nb, stride=CMP_STRIDE), :]
            r1 = stage_ref[s, pl.ds(2 * m + 1, nb, stride=CMP_STRIDE), :]
            head0 = jnp.where(lo, r0, pltpu.roll(r1, HEAD_DIM, axis=1))
            head1 = jnp.where(lo, pltpu.roll(r0, HEAD_DIM, axis=1), r1)
            for h, piece in enumerate((head0, head1)):
                c0 = (2 * s + h) * span + m * KV_WIDTH
                cmpx_ref[0, :, c0:c0 + KV_WIDTH] = piece.astype(BF16)


def _inproj_kernel(x_ref, shift_ref, scale_ref, nw_ref, w_ref, qw_ref, kw_ref, gq_ref, gk_ref,
                   c_ref, su_ref, sd_ref,
                   qp_ref, kvb_ref, win_ref, winb_ref, z_ref, xbc_ref, misc_ref, *rest, seq_layout):
    x = x_ref[...]
    ms = jnp.mean(x * x, axis=-1, keepdims=True)
    h = x * lax.rsqrt(ms + EPS) * nw_ref[...]
    h = h * (1.0 + _mod(scale_ref)) + _mod(shift_ref)
    hb = h.astype(BF16)
    c, su, sd = c_ref[...], su_ref[...], sd_ref[...]

    q = _dot(hb, w_ref[:, _C_Q:_C_Q + ATTN_WIDTH])
    qms = _dot((q * q).astype(BF16), gq_ref[...])
    q = q * lax.rsqrt(qms + EPS) * qw_ref[...]
    q = _rope(q, c, su, sd) * (HEAD_DIM ** -0.5 * LOG2E)
    lane = lax.broadcasted_iota(jnp.int32, q.shape, 1) % LANES
    lo = lane < HEAD_DIM
    q_up = pltpu.roll(q, ATTN_WIDTH - HEAD_DIM, axis=1)
    q_dn = pltpu.roll(q, HEAD_DIM, axis=1)
    zero = jnp.zeros_like(q)
    nat_lo = jnp.where(lo, q, zero)
    nat_hi = jnp.where(lo, zero, q)
    up_lo = jnp.where(lo, q_up, zero)
    dn_hi = jnp.where(lo, zero, q_dn)
    blocks = []
    for hd in range(N_Q_HEADS):
        pair = hd // 2
        sl = slice(pair * LANES, (pair + 1) * LANES)
        if hd < GQA_GROUP:
            blocks.append((nat_lo if hd % 2 == 0 else up_lo)[:, sl])
        else:
            blocks.append((dn_hi if hd % 2 == 0 else nat_hi)[:, sl])
    qp_ref[...] = jnp.concatenate(blocks, axis=1).astype(BF16)

    kv = _dot(hb, w_ref[:, _C_KV:_C_KV + 6 * KV_WIDTH])
    outs = []
    for br in range(3):
        k = kv[:, br * 2 * KV_WIDTH:br * 2 * KV_WIDTH + KV_WIDTH]
        v = kv[:, br * 2 * KV_WIDTH + KV_WIDTH:(br + 1) * 2 * KV_WIDTH]
        kms = _dot((k * k).astype(BF16), gk_ref[...])
        k = k * lax.rsqrt(kms + EPS) * kw_ref[:, br * KV_WIDTH:(br + 1) * KV_WIDTH]
        k = _rope(k, c, su, sd)
        outs += [k, v]
    kvrows = jnp.concatenate(outs[:4], axis=1)
    winrows = jnp.concatenate(outs[4:], axis=1)
    kvb_ref[...] = kvrows.astype(BF16)
    win_ref[...] = winrows
    winb_ref[...] = winrows.astype(BF16)
    if seq_layout:
        kvt_ref, cmpx_ref, stage_ref = rest
        tm = kvrows.shape[0]
        for r in range(4):
            kvt_ref[0, r] = jnp.transpose(kvrows[:, r * KV_WIDTH:(r + 1) * KV_WIDTH])
        for s in range(2):
            stage_ref[s] = kvrows[:, s * KV_WIDTH:(s + 1) * KV_WIDTH]
        _stride_block_store(stage_ref, cmpx_ref, tm)
    else:
        rest[0][...] = kvrows

    z_ref[...] = _dot(hb, w_ref[:, _C_Z:_C_Z + SSM_WIDTH])
    xbc_ref[...] = _dot(hb, w_ref[:, _C_XBC:_C_XBC + CONV_DIM])
    misc_ref[...] = _dot(hb, w_ref[:, _C_MISC:_C_MISC + LANES])


def inproj(x, mod3, mod_row0, norm_w, wp, q_norm_w, k_norm_w, pos, tm, seq_layout):
    b, t, d = x.shape
    n = b * t
    tiles_per_b = t // tm
    xf = x.reshape(n, d)
    c, su, sd = _rope_tables(pos)
    qw = jnp.tile(q_norm_w, N_Q_HEADS).reshape(1, ATTN_WIDTH)
    kw = jnp.concatenate([jnp.tile(k_norm_w[i], N_KV_HEADS) for i in range(3)]).reshape(1, 3 * KV_WIDTH)
    gq = _group_mean_matrix(ATTN_WIDTH)
    gk = _group_mean_matrix(KV_WIDTH)

    def mod_spec(col):
        return _mod_spec(mod3, col, tm, tiles_per_b, mod_row0)

    def tok(wd):
        return pl.BlockSpec((tm, wd), lambda i: (i, 0))

    def full(a):
        return pl.BlockSpec(a.shape, lambda i: (0,) * a.ndim)

    rope_spec = pl.BlockSpec((tm, LANES), lambda i: (i % tiles_per_b, 0))
    out_shape = [
        jax.ShapeDtypeStruct((n, N_Q_HEADS * LANES), BF16),
        jax.ShapeDtypeStruct((n, 4 * KV_WIDTH), BF16),
        jax.ShapeDtypeStruct((n, 2 * KV_WIDTH), F32),
        jax.ShapeDtypeStruct((n, 2 * KV_WIDTH), BF16),
        jax.ShapeDtypeStruct((n, SSM_WIDTH), F32),
        jax.ShapeDtypeStruct((n, CONV_DIM), F32),
        jax.ShapeDtypeStruct((n, LANES), F32),
    ]
    out_specs = [tok(s.shape[1]) for s in out_shape]
    scratch = []
    if seq_layout:
        out_shape += [jax.ShapeDtypeStruct((b, 4, KV_WIDTH, t), F32),
                      jax.ShapeDtypeStruct((b, t // CMP_STRIDE, CMP_STRIDE * 2 * KV_WIDTH), BF16)]
        out_specs += [pl.BlockSpec((1, 4, KV_WIDTH, tm), lambda i: (i // tiles_per_b, 0, 0, i % tiles_per_b)),
                      pl.BlockSpec((1, tm // CMP_STRIDE, CMP_STRIDE * 2 * KV_WIDTH),
                                   lambda i: (i // tiles_per_b, i % tiles_per_b, 0))]
        scratch = [pltpu.VMEM((2, tm, KV_WIDTH), F32)]
    else:
        out_shape += [jax.ShapeDtypeStruct((n, 4 * KV_WIDTH), F32)]
        out_specs += [tok(4 * KV_WIDTH)]
    return pl.pallas_call(
        functools.partial(_inproj_kernel, seq_layout=seq_layout),
        grid=(n // tm,),
        in_specs=[tok(d), mod_spec(0), mod_spec(1), full(norm_w), full(wp), full(qw), full(kw), full(gq), full(gk),
                  rope_spec, rope_spec, rope_spec],
        out_specs=tuple(out_specs),
        out_shape=tuple(out_shape),
        scratch_shapes=scratch,
        compiler_params=_cparams(("arbitrary",)),
        name="inproj",
    )(xf, mod3, mod3, norm_w, wp, qw, kw, gq, gk, c, su, sd)


def _prep_compress(cmp_pe, cmp_w1, cmp_w2):
    span = CMP_STRIDE * HEAD_DIM
    w1p = jnp.concatenate([cmp_w1[:, :span], cmp_w1[:, span:]], axis=2).astype(BF16)
    pep = cmp_pe.reshape(2, 2, span)
    eye = jnp.eye(N_KV_HEADS, dtype=F32)
    w2p = jnp.einsum("poe,hg->phoge", cmp_w2, eye).reshape(2, N_KV_HEADS, CMP_HIDDEN, KV_WIDTH).astype(BF16)
    return w1p, pep, w2p


def _compress_kernel(x_ref, w1_ref, pe_ref, w2_ref, o_ref):
    part = pl.program_id(1)
    nb = x_ref.shape[1]
    span = CMP_STRIDE * HEAD_DIM
    pe = pe_ref[0]
    out = jnp.zeros((nb, KV_WIDTH), F32)
    for h in range(N_KV_HEADS):
        xk = x_ref[0, :, h * span:(h + 1) * span]
        xv = x_ref[0, :, (N_KV_HEADS + h) * span:(N_KV_HEADS + h + 1) * span]
        x = jnp.where(part == 0, xk, xv).astype(F32)
        u = _dot((x + pe[0:1]).astype(BF16), w1_ref[0, :, :CMP_HIDDEN])
        v = _dot((x + pe[1:2]).astype(BF16), w1_ref[0, :, CMP_HIDDEN:])
        h1 = u + pltpu.roll(v, nb - 1, axis=0)
        out = out + _dot(_silu(h1).astype(BF16), w2_ref[0, h])
    row = lax.broadcasted_iota(jnp.int32, out.shape, 0)
    o_ref[0, 0] = jnp.where(row < nb - 1, out, 0.0).astype(o_ref.dtype)


def compress(x, w1p, pep, w2p):
    b, nb, width = x.shape
    return pl.pallas_call(
        _compress_kernel,
        grid=(b, 2),
        in_specs=[
            pl.BlockSpec((1, nb, width), lambda i, p: (i, 0, 0)),
            pl.BlockSpec((1,) + w1p.shape[1:], lambda i, p: (p, 0, 0)),
            pl.BlockSpec((1,) + pep.shape[1:], lambda i, p: (p, 0, 0)),
            pl.BlockSpec((1,) + w2p.shape[1:], lambda i, p: (p, 0, 0, 0)),
        ],
        out_specs=pl.BlockSpec((1, 1, nb, KV_WIDTH), lambda i, p: (i, p, 0, 0)),
        out_shape=jax.ShapeDtypeStruct((b, 2, nb, KV_WIDTH), BF16),
        compiler_params=_cparams(("arbitrary", "arbitrary")),
        name="compress",
    )(x, w1p, pep, w2p)


N_SEL_LANES = LANES


def _cover_matrix(nb):
    c = np.arange(nb)[:, None]
    j = np.arange(N_SEL_LANES)[None, :]
    start = c * CMP_STRIDE
    m = (start < (j + 1) * SEL_BLOCK) & (start + CMP_LEN > j * SEL_BLOCK)
    return jnp.asarray(m.astype(np.float32), BF16)


def _place_heads(res, kv):
    lane = lax.broadcasted_iota(jnp.int32, res[0].shape, 1)
    lo = lane < HEAD_DIM
    blocks = []
    for pair in range(GQA_GROUP // 2):
        a, b = res[2 * pair], res[2 * pair + 1]
        if kv == 0:
            blocks.append(jnp.where(lo, a, pltpu.roll(b, HEAD_DIM, axis=1)))
        else:
            blocks.append(jnp.where(lo, pltpu.roll(a, HEAD_DIM, axis=1), b))
    return jnp.concatenate(blocks, axis=1)


def _group_rows(q_ref, kv):
    heads = range(kv * GQA_GROUP, (kv + 1) * GQA_GROUP)
    return jnp.concatenate([q_ref[0, :, hd * LANES:(hd + 1) * LANES] for hd in heads], axis=0)


def _heads_from_transposed(out_t, tq, kv):
    out = jnp.transpose(out_t)
    return _place_heads([out[g * tq:(g + 1) * tq] for g in range(GQA_GROUP)], kv)


def _cmp_select_kernel(q_ref, kc_ref, vc_ref, covt_ref, o_ref, m_ref, *, q_off, n_pick):
    tq = q_ref.shape[1]
    rows = GQA_GROUP * tq
    nb = kc_ref.shape[2]
    wl = max(tq, LANES)
    assert tq % LANES == 0 or rows == LANES
    t0 = q_off + pl.program_id(1) * tq
    kc = kc_ref[0, 0]
    vc = vc_ref[0, 0]
    qpos = t0 + lax.broadcasted_iota(jnp.int32, (nb, rows), 1) % tq
    cend = lax.broadcasted_iota(jnp.int32, (nb, rows), 0) * CMP_STRIDE + (CMP_LEN - 1)
    valid = cend <= qpos
    blk = lax.broadcasted_iota(jnp.int32, (N_SEL_LANES, wl), 0)
    cur = (t0 + lax.broadcasted_iota(jnp.int32, (N_SEL_LANES, wl), 1) % tq) // SEL_BLOCK
    forced = (blk == 0) | ((blk <= cur) & (blk > cur - N_LOCAL))
    o_groups = []
    for kv in range(N_KV_HEADS):
        s = _dot_nt(kc, _group_rows(q_ref, kv))
        s = jnp.where(valid, s, NEG)
        e = jnp.exp2(s - jnp.max(s, axis=0, keepdims=True))
        p = e / jnp.sum(e, axis=0, keepdims=True)
        p = jnp.where(valid, p, 0.0)
        o_t = lax.dot_general(vc, p.astype(BF16), (((0,), (0,)), ((), ())), preferred_element_type=F32)
        o_groups.append(_heads_from_transposed(o_t, tq, kv))
        if tq % LANES == 0:
            psum = sum(p[:, g * tq:(g + 1) * tq] for g in range(GQA_GROUP))
        else:
            psum = p + sum(pltpu.roll(p, g * tq, axis=1) for g in range(1, GQA_GROUP))
        hi, lo = _split2(psum)
        imp = _dot(covt_ref[...], hi) + _dot(covt_ref[...], lo)
        x = jnp.where(forced, BIG, jnp.where(blk > cur, -BIG, imp))
        sel = jnp.zeros(x.shape, jnp.bool_)
        for _ in range(n_pick):
            mx = jnp.max(x, axis=0, keepdims=True)
            idx = jnp.min(jnp.where(x == mx, blk, N_SEL_LANES), axis=0, keepdims=True)
            hit = blk == idx
            sel = sel | hit
            x = jnp.where(hit, -jnp.inf, x)
        mneg = jnp.transpose(jnp.where(sel, 0.0, NEG))
        m_ref[0, kv] = mneg[:tq].astype(m_ref.dtype)
    o_ref[0] = jnp.concatenate(o_groups, axis=1)


def cmp_select(qp, kcv, q_off, n_pick, tq):
    b, t, _ = qp.shape
    nb = kcv.shape[2]
    cover = jnp.transpose(_cover_matrix(nb))
    return pl.pallas_call(
        functools.partial(_cmp_select_kernel, q_off=q_off, n_pick=n_pick),
        grid=(b, t // tq),
        in_specs=[
            pl.BlockSpec((1, tq, N_Q_HEADS * LANES), lambda i, j: (i, j, 0)),
            pl.BlockSpec((1, 1, nb, KV_WIDTH), lambda i, j: (i, 0, 0, 0)),
            pl.BlockSpec((1, 1, nb, KV_WIDTH), lambda i, j: (i, 1, 0, 0)),
            pl.BlockSpec((N_SEL_LANES, nb), lambda i, j: (0, 0)),
        ],
        out_specs=(
            pl.BlockSpec((1, tq, ATTN_WIDTH), lambda i, j: (i, j, 0)),
            pl.BlockSpec((1, N_KV_HEADS, tq, N_SEL_LANES), lambda i, j: (i, 0, j, 0)),
        ),
        out_shape=(
            jax.ShapeDtypeStruct((b, t, ATTN_WIDTH), F32),
            jax.ShapeDtypeStruct((b, N_KV_HEADS, t, N_SEL_LANES), BF16),
        ),
        compiler_params=_cparams(("arbitrary", "arbitrary")),
        name="cmp_select",
    )(qp, kcv, kcv, cover)


SEL_TILE_ELEMS = 512 * 512
SEL_WIDE = 2
WIN_CHUNK = 256


def _block_onehot(s):
    key = np.arange(s)[:, None]
    j = np.arange(N_SEL_LANES)[None, :]
    return jnp.asarray((key // SEL_BLOCK == j).astype(np.float32), BF16)


def _gate_expand():
    m = np.zeros((3, LANES, ATTN_WIDTH), np.float32)
    for br in range(3):
        for hd in range(N_Q_HEADS):
            m[br, SSM_HEADS + 3 * hd + br, hd * HEAD_DIM:(hd + 1) * HEAD_DIM] = 1.0
    return jnp.asarray(m, BF16)


def _flash_update(ss, v, m_ref, acc_ref):
    lane = lax.broadcasted_iota(jnp.int32, v.shape, 1)
    one = jnp.ones(v.shape, v.dtype)
    stage = []
    for k, s in enumerate(ss):
        m_old = m_ref[k]
        m_new = jnp.maximum(m_old, jnp.max(s, axis=0, keepdims=True))
        alpha = jnp.exp2(m_old - m_new)
        p = jnp.exp2(s - m_new)
        m_ref[k] = m_new
        stage.append((alpha, p.astype(BF16)))
    for k, (alpha, p) in enumerate(stage):
        vk = jnp.where((lane < HEAD_DIM) == (k == 0), v, one)
        pv = lax.dot_general(vk, p, (((0,), (0,)), ((), ())), preferred_element_type=F32)
        acc_ref[k] = alpha * acc_ref[k] + pv


def _sel_chunk(rows, n_keys):
    chunk = SEL_TILE_ELEMS // rows
    while n_keys % chunk:
        chunk //= 2
    return chunk


def _sel_win_kernel(q_ref, mneg_ref, ksel_ref, vsel_ref, et_ref, kwin_ref, vwin_ref, ocmp_ref, misc_ref, eg_ref,
                    o_ref, lhs_ref, m_ref, acc_ref, *, q_off, win_pos0):
    tq = q_ref.shape[1]
    rows = GQA_GROUP * tq
    SEL_CHUNK = _sel_chunk(rows, ksel_ref.shape[1])
    t0 = q_off + pl.program_id(1) * tq
    n_sel = lax.shift_right_logical(t0 + tq - 1, int(math.log2(SEL_CHUNK))) + 1
    w_lo = jnp.maximum(t0 - (WINDOW - 1) - win_pos0, 0) // WIN_CHUNK
    w_hi = (t0 + tq - 1 - win_pos0) // WIN_CHUNK + 1

    def qrow(n_keys):
        return lax.broadcasted_iota(jnp.int32, (n_keys, rows), 1) % tq + t0

    def init():
        m_ref[...] = jnp.full(m_ref.shape, NEG, F32)
        acc_ref[...] = jnp.zeros(acc_ref.shape, F32)

    def finish():
        outs = []
        for kv in range(N_KV_HEADS):
            acc = acc_ref[kv]
            denom_row = HEAD_DIM * (1 - kv)
            outs.append(_heads_from_transposed(acc / acc[denom_row:denom_row + 1, :], tq, kv))
        return jnp.concatenate(outs, axis=1)

    for kv in range(N_KV_HEADS):
        for g in range(GQA_GROUP):
            hd = kv * GQA_GROUP + g
            lhs_ref[kv, g * tq:(g + 1) * tq, :LANES] = q_ref[0, :, hd * LANES:(hd + 1) * LANES]
            lhs_ref[kv, g * tq:(g + 1) * tq, LANES:] = mneg_ref[0, kv]

    init()

    def sel_step(i, carry, causal, width):
        n_keys = width * SEL_CHUNK
        r0 = pl.multiple_of(i * n_keys, n_keys)
        rhs = jnp.concatenate([ksel_ref[0, pl.ds(r0, n_keys), :], et_ref[pl.ds(r0, n_keys), :]], axis=1)
        v = vsel_ref[0, pl.ds(r0, n_keys), :]
        if causal:
            ok = r0 + lax.broadcasted_iota(jnp.int32, (n_keys, rows), 0) <= qrow(n_keys)
        ss = [_dot_nt(rhs, lhs_ref[kv]) for kv in range(N_KV_HEADS)]
        if causal:
            ss = [jnp.where(ok, s, NEG) for s in ss]
        _flash_update(ss, v, m_ref, acc_ref)
        return carry

    n_full = lax.shift_right_logical(t0 + 1, int(math.log2(SEL_CHUNK)))
    n_wide = n_full // SEL_WIDE
    lax.fori_loop(0, n_wide, functools.partial(sel_step, causal=False, width=SEL_WIDE), 0)
    lax.fori_loop(n_wide * SEL_WIDE, n_full, functools.partial(sel_step, causal=False, width=1), 0)
    lax.fori_loop(n_full, n_sel, functools.partial(sel_step, causal=True, width=1), 0)
    o_sel = finish()

    init()

    def win_step(c, carry):
        r0 = pl.multiple_of(c * WIN_CHUNK, WIN_CHUNK)
        k = kwin_ref[0, pl.ds(r0, WIN_CHUNK), :]
        v = vwin_ref[0, pl.ds(r0, WIN_CHUNK), :]
        wpos = win_pos0 + r0 + lax.broadcasted_iota(jnp.int32, (WIN_CHUNK, rows), 0)
        qr = qrow(WIN_CHUNK)
        ok = (wpos <= qr) & (wpos > qr - WINDOW)
        ss = [jnp.where(ok, _dot_nt(k, lhs_ref[kv, :, :LANES]), NEG) for kv in range(N_KV_HEADS)]
        _flash_update(ss, v, m_ref, acc_ref)
        return carry

    lax.fori_loop(w_lo, w_hi, win_step, 0)
    o_win = finish()

    gates = jax.nn.sigmoid(misc_ref[0])
    ghi = gates.astype(BF16)
    glo = (gates - ghi.astype(F32)).astype(BF16)
    branches = (ocmp_ref[0], o_sel, o_win)
    out = jnp.zeros(branches[0].shape, F32)
    for br in range(3):
        out = out + (_dot(ghi, eg_ref[br]) + _dot(glo, eg_ref[br])) * branches[br]
    o_ref[0] = out


def sel_win_attention(qp, mneg, kvb, sel_col, winb, o_cmp, misc, q_off, win_pos0, tq):
    b, t, _ = qp.shape
    s = kvb.shape[1]
    sw = winb.shape[1]
    et = _block_onehot(s)
    eg = _gate_expand()
    rows = GQA_GROUP * tq
    assert q_off + t <= s and q_off + t - win_pos0 <= sw and sw % WIN_CHUNK == 0
    return pl.pallas_call(
        functools.partial(_sel_win_kernel, q_off=q_off, win_pos0=win_pos0),
        grid=(b, t // tq),
        in_specs=[
            pl.BlockSpec((1, tq, N_Q_HEADS * LANES), lambda i, j: (i, j, 0)),
            pl.BlockSpec((1, N_KV_HEADS, tq, N_SEL_LANES), lambda i, j: (i, 0, j, 0)),
            pl.BlockSpec((1, s, KV_WIDTH), lambda i, j: (i, 0, sel_col)),
            pl.BlockSpec((1, s, KV_WIDTH), lambda i, j: (i, 0, sel_col + 1)),
            pl.BlockSpec((s, N_SEL_LANES), lambda i, j: (0, 0)),
            pl.BlockSpec((1, sw, KV_WIDTH), lambda i, j: (i, 0, 0)),
            pl.BlockSpec((1, sw, KV_WIDTH), lambda i, j: (i, 0, 1)),
            pl.BlockSpec((1, tq, ATTN_WIDTH), lambda i, j: (i, j, 0)),
            pl.BlockSpec((1, tq, LANES), lambda i, j: (i, j, 0)),
            pl.BlockSpec((3, LANES, ATTN_WIDTH), lambda i, j: (0, 0, 0)),
        ],
        out_specs=pl.BlockSpec((1, tq, ATTN_WIDTH), lambda i, j: (i, j, 0)),
        out_shape=jax.ShapeDtypeStruct((b, t, ATTN_WIDTH), F32),
        scratch_shapes=[
            pltpu.VMEM((N_KV_HEADS, rows, 2 * LANES), BF16),
            pltpu.VMEM((N_KV_HEADS, 1, rows), F32),
            pltpu.VMEM((N_KV_HEADS, LANES, rows), F32),
        ],
        compiler_params=_cparams(("arbitrary", "arbitrary")),
        name="sel_win_attention",
    )(qp, mneg, kvb, kvb, et, winb, winb, o_cmp, misc, eg)


CONV_PAD = 8
HEAD_PAIRS = SSM_HEADS // 2


def _split3(x):
    a = x.astype(BF16)
    r = x - a.astype(F32)
    b = r.astype(BF16)
    c = (r - b.astype(F32)).astype(BF16)
    return a, b, c


def _ssd_kernel(xbc_ref, z_ref, misc_ref, conv0_ref, h0_ref, cw_ref, cb_ref, dtb_ref, a_ref, dsk_ref, nw_ref,
                y_ref, hout_ref, cout_ref, xp_ref, h_ref, ms_ref, *, t_valid):
    ch = pl.program_id(1)
    L = SSD_CHUNK
    keep = CONV_WIDTH - 1

    @pl.when(ch == 0)
    def _():
        xp_ref[...] = jnp.zeros(xp_ref.shape, F32)
        xp_ref[CONV_PAD - keep:CONV_PAD, :] = conv0_ref[0]
        h_ref[...] = h0_ref[0]

    xp_ref[CONV_PAD:CONV_PAD + t_valid, :] = xbc_ref[0]
    conv = cb_ref[...]
    for j in range(CONV_WIDTH):
        conv = conv + cw_ref[j:j + 1, :] * xp_ref[CONV_PAD - keep + j:CONV_PAD - keep + j + L, :]
    last = xp_ref[CONV_PAD + t_valid - keep:CONV_PAD + t_valid, :]
    cout_ref[0] = last
    xp_ref[CONV_PAD - keep:CONV_PAD, :] = last
    xc = _silu(conv)

    row = lax.broadcasted_iota(jnp.int32, (L, LANES), 0)
    lane = lax.broadcasted_iota(jnp.int32, (L, LANES), 1)
    if t_valid == L:
        raw = misc_ref[0]
    else:
        ms_ref[...] = jnp.zeros(ms_ref.shape, F32)
        ms_ref[0:t_valid, :] = misc_ref[0]
        raw = ms_ref[...]
    v = raw + dtb_ref[...]
    dt = jnp.maximum(v, 0.0) + jnp.log(1.0 + jnp.exp(-jnp.abs(v)))
    dt = jnp.where((lane < SSM_HEADS) & (row < t_valid), dt, 0.0)
    da = dt * a_ref[...]
    tri = (lax.broadcasted_iota(jnp.int32, (L, L), 1) <= lax.broadcasted_iota(jnp.int32, (L, L), 0))
    trib = tri.astype(BF16)
    acum = sum(_dot(trib, part) for part in _split3(da))
    acum_t = jnp.transpose(acum)
    dt_t = jnp.transpose(dt)
    e_acum = jnp.exp(acum)
    e_last = jnp.exp(acum[L - 1:L, :])
    w_end = jnp.exp(acum[L - 1:L, :] - acum) * dt
    lo = lane < SSM_HEAD_DIM

    ys = []
    for pair in range(HEAD_PAIRS):
        grp = (2 * pair) // (SSM_HEADS // SSM_GROUPS)
        bg = xc[:, SSM_WIDTH + grp * SSM_STATE:SSM_WIDTH + (grp + 1) * SSM_STATE].astype(BF16)
        cg = xc[:, SSM_WIDTH + (SSM_GROUPS + grp) * SSM_STATE:SSM_WIDTH + (SSM_GROUPS + grp + 1) * SSM_STATE].astype(BF16)
        g = _dot_nt(cg, bg)
        xpair = xc[:, pair * LANES:(pair + 1) * LANES]
        y = jnp.zeros((L, LANES), F32)
        for sub in range(2):
            hd = 2 * pair + sub
            seg = acum[:, hd:hd + 1] - acum_t[hd:hd + 1, :]
            m = g * jnp.exp(jnp.where(tri, seg, NEG)) * dt_t[hd:hd + 1, :]
            xm = jnp.where(lo if sub == 0 else ~lo, xpair, 0.0)
            y = y + _dot(m.astype(BF16), xm.astype(BF16))
        col = lambda a: jnp.where(lo, a[:, 2 * pair:2 * pair + 1], a[:, 2 * pair + 1:2 * pair + 2])
        hp = h_ref[pair]
        y = y + _dot_nt(cg, hp.astype(BF16)) * col(e_acum)
        y = y + col(dsk_ref[...]) * xpair
        xw = (xpair * col(w_end)).astype(BF16)
        st = lax.dot_general(xw, bg, (((0,), (0,)), ((), ())), preferred_element_type=F32)
        prow = lax.broadcasted_iota(jnp.int32, (LANES, LANES), 0) < SSM_HEAD_DIM
        dec = jnp.where(prow, e_last[:, 2 * pair:2 * pair + 1], e_last[:, 2 * pair + 1:2 * pair + 2])
        h_ref[pair] = hp * dec + st
        ys.append(y)
    y = jnp.concatenate(ys, axis=1)
    if t_valid != L:
        y = y[:t_valid]
    y = y * _silu(z_ref[0])
    y = y * lax.rsqrt(jnp.mean(y * y, axis=-1, keepdims=True) + EPS) * nw_ref[...]
    y_ref[0] = y

    @pl.when(ch == pl.num_programs(1) - 1)
    def _():
        hout_ref[0] = h_ref[...]


def ssd(xbc, z, misc, conv0, h0, conv_w, conv_b, dt_bias, a_log, d_skip, norm_w):
    b, t, _ = xbc.shape
    L = SSD_CHUNK
    t_valid = L if t % L == 0 else t
    assert t_valid == L or t < L
    n_ch = max(t // L, 1)
    keep = CONV_WIDTH - 1
    pad8 = lambda v: jnp.pad(v.astype(F32), (0, LANES - SSM_HEADS)).reshape(1, LANES)
    dtb = pad8(dt_bias)
    a = pad8(-jnp.exp(a_log.astype(F32)))
    dsk = pad8(d_skip)
    h0p = h0.reshape(b, HEAD_PAIRS, 2 * SSM_HEAD_DIM, SSM_STATE)
    full = lambda arr: pl.BlockSpec(arr.shape, lambda i, c: (0,) * arr.ndim)
    tok = lambda wd: pl.BlockSpec((1, t_valid, wd), lambda i, c: (i, c, 0))
    y, hout, cout = pl.pallas_call(
        functools.partial(_ssd_kernel, t_valid=t_valid),
        grid=(b, n_ch),
        in_specs=[
            tok(CONV_DIM), tok(SSM_WIDTH), tok(LANES),
            pl.BlockSpec((1, keep, CONV_DIM), lambda i, c: (i, 0, 0)),
            pl.BlockSpec((1, HEAD_PAIRS, 2 * SSM_HEAD_DIM, SSM_STATE), lambda i, c: (i, 0, 0, 0)),
            full(conv_w), pl.BlockSpec((1, CONV_DIM), lambda i, c: (0, 0)),
            full(dtb), full(a), full(dsk), pl.BlockSpec((1, SSM_WIDTH), lambda i, c: (0, 0)),
        ],
        out_specs=(
            tok(SSM_WIDTH),
            pl.BlockSpec((1, HEAD_PAIRS, 2 * SSM_HEAD_DIM, SSM_STATE), lambda i, c: (i, 0, 0, 0)),
            pl.BlockSpec((1, keep, CONV_DIM), lambda i, c: (i, 0, 0)),
        ),
        out_shape=(
            jax.ShapeDtypeStruct((b, t, SSM_WIDTH), F32),
            jax.ShapeDtypeStruct((b, HEAD_PAIRS, 2 * SSM_HEAD_DIM, SSM_STATE), F32),
            jax.ShapeDtypeStruct((b, keep, CONV_DIM), F32),
        ),
        scratch_shapes=[
            pltpu.VMEM((CONV_PAD + L, CONV_DIM), F32),
            pltpu.VMEM((HEAD_PAIRS, 2 * SSM_HEAD_DIM, SSM_STATE), F32),
            pltpu.VMEM((L, LANES), F32),
        ],
        compiler_params=_cparams(("arbitrary", "arbitrary")),
        name="ssd",
    )(xbc, z, misc, conv0, h0p, conv_w, conv_b.reshape(1, CONV_DIM), dtb, a, dsk, norm_w.reshape(1, SSM_WIDTH))
    return y, hout.reshape(b, SSM_HEADS, SSM_HEAD_DIM, SSM_STATE), cout


def _split2(x):
    hi = x.astype(BF16)
    return hi, (x - hi.astype(F32)).astype(BF16)


def _merge_kernel(oa_ref, ys_ref, x_ref, g1_ref, sh2_ref, sc2_ref, anw_ref, wo_ref, n2w_ref, wrh_ref, wrl_ref,
                  x1_ref, h2_ref, lg_ref):
    oa = oa_ref[...]
    a = oa * lax.rsqrt(jnp.mean(oa * oa, axis=-1, keepdims=True) + EPS) * anw_ref[...]
    cat = jnp.concatenate([a.astype(BF16), ys_ref[...].astype(BF16)], axis=1)
    x1 = x_ref[...] + _mod(g1_ref) * _dot(cat, wo_ref[...])
    x1_ref[...] = x1
    h2 = x1 * lax.rsqrt(jnp.mean(x1 * x1, axis=-1, keepdims=True) + EPS) * n2w_ref[...]
    h2 = h2 * (1.0 + _mod(sc2_ref)) + _mod(sh2_ref)
    h2_ref[...] = h2.astype(BF16)
    hh, hl = _split2(h2)
    lg_ref[...] = _dot_nt(wrh_ref[...], hh) + _dot_nt(wrh_ref[...], hl) + _dot_nt(wrl_ref[...], hh)


def merge(o_attn, y_ssm, x, mod3, mod_row0, attn_norm_w, wo, norm2_w, w_router, tm):
    b, t, d = x.shape
    n = b * t
    tiles_per_b = t // tm
    wrt = jnp.transpose(w_router)
    wrh, wrl = _split2(wrt)

    def mod_spec(col):
        return _mod_spec(mod3, col, tm, tiles_per_b, mod_row0)

    tok = lambda wd: pl.BlockSpec((tm, wd), lambda i: (i, 0))
    full = lambda a: pl.BlockSpec(a.shape, lambda i: (0,) * a.ndim)
    return pl.pallas_call(
        _merge_kernel,
        grid=(n // tm,),
        in_specs=[tok(ATTN_WIDTH), tok(SSM_WIDTH), tok(d), mod_spec(2), mod_spec(3), mod_spec(4),
                  full(attn_norm_w), full(wo), full(norm2_w), full(wrh), full(wrl)],
        out_specs=(tok(d), tok(d), pl.BlockSpec((N_EXPERTS, tm), lambda i: (0, i))),
        out_shape=(jax.ShapeDtypeStruct((n, d), F32), jax.ShapeDtypeStruct((n, d), BF16),
                   jax.ShapeDtypeStruct((N_EXPERTS, n), F32)),
        compiler_params=_cparams(("arbitrary",)),
        name="merge",
    )(o_attn.reshape(n, ATTN_WIDTH), y_ssm.reshape(n, SSM_WIDTH), x.reshape(n, d), mod3, mod3, mod3,
      attn_norm_w, wo, norm2_w, wrh, wrl)


EXPERTS_PER_GROUP = N_EXPERTS // N_EXPERT_GROUPS


def _first_max(x, ids, axes, n_ids):
    mx = jnp.max(x, axis=axes, keepdims=True)
    return ids == jnp.min(jnp.where(x == mx, ids, n_ids), axis=axes, keepdims=True), mx


def _route_kernel(lg_ref, eb_ref, tri_ref, w_ref, pos_ref, cnt_ref):
    lg = lg_ref[...]
    tn = lg.shape[2]
    scores = jax.nn.sigmoid(lg)
    biased = scores + eb_ref[...]
    sub = lax.broadcasted_iota(jnp.int32, lg.shape, 1)
    grp = lax.broadcasted_iota(jnp.int32, (N_EXPERT_GROUPS, 1, tn), 0)
    eid = lax.broadcasted_iota(jnp.int32, lg.shape, 0) * EXPERTS_PER_GROUP + sub
    hit, m1 = _first_max(biased, sub, 1, EXPERTS_PER_GROUP)
    m2 = jnp.max(jnp.where(hit, -jnp.inf, biased), axis=1, keepdims=True)
    gs = m1 + m2
    keep = jnp.zeros(gs.shape, jnp.bool_)
    for _ in range(TOPK_GROUPS):
        hit, _m = _first_max(gs, grp, 0, N_EXPERT_GROUPS)
        keep = keep | hit
        gs = jnp.where(hit, -jnp.inf, gs)
    x = jnp.where(keep, biased, NEG)
    sel = jnp.zeros(lg.shape, jnp.bool_)
    for _ in range(TOP_K):
        hit, _m = _first_max(x, eid, (0, 1), N_EXPERTS)
        sel = sel | hit
        x = jnp.where(hit, -jnp.inf, x)
    w = jnp.where(sel, scores, 0.0)
    w = w / jnp.sum(w, axis=(0, 1), keepdims=True) * ROUTED_SCALE
    w_ref[...] = w
    selb = sel.astype(BF16).reshape(N_EXPERTS, tn)
    pos = _dot(selb, tri_ref[...])
    pos_ref[...] = jnp.where(sel, pos.reshape(lg.shape), -1.0)
    cnt = jnp.sum(sel.astype(F32), axis=2, keepdims=True)
    cnt_ref[0] = jnp.broadcast_to(cnt, cnt_ref.shape[1:]).astype(jnp.int32)


def route(logits_t, e_bias, tn):
    n = logits_t.shape[1]
    lg3 = logits_t.reshape(N_EXPERT_GROUPS, EXPERTS_PER_GROUP, n)
    eb = e_bias.astype(F32).reshape(N_EXPERT_GROUPS, EXPERTS_PER_GROUP, 1)
    tri = jnp.asarray(np.triu(np.ones((tn, tn), np.float32), 1), BF16)
    blk = pl.BlockSpec((N_EXPERT_GROUPS, EXPERTS_PER_GROUP, tn), lambda i: (0, 0, i))
    w, pos, cnt = pl.pallas_call(
        _route_kernel,
        grid=(n // tn,),
        in_specs=[blk, pl.BlockSpec(eb.shape, lambda i: (0, 0, 0)), pl.BlockSpec((tn, tn), lambda i: (0, 0))],
        out_specs=(blk, blk, pl.BlockSpec((1, N_EXPERT_GROUPS, EXPERTS_PER_GROUP, LANES), lambda i: (i, 0, 0, 0))),
        out_shape=(jax.ShapeDtypeStruct(lg3.shape, F32), jax.ShapeDtypeStruct(lg3.shape, F32),
                   jax.ShapeDtypeStruct((n // tn, N_EXPERT_GROUPS, EXPERTS_PER_GROUP, LANES), jnp.int32)),
        compiler_params=_cparams(("arbitrary",)),
        name="route",
    )(lg3, eb, tri)
    return w.reshape(N_EXPERTS, n), pos.reshape(N_EXPERTS, n), cnt[..., 0].reshape(n // tn, N_EXPERTS)


MOE_ROWS = 128


def _swiglu(xb, wgu, wd, width):
    gu = _dot(xb, wgu)
    act = _silu(gu[:, :width]) * gu[:, width:]
    return _dot(act.astype(BF16), wd)


MOE_EXPERTS_PER_STEP = 4


MOE_ALIGN = 16
MOE_GATHER_ROWS = 896


def _moe_slots(tm):
    worst = TOP_K * tm + N_EXPERTS * (MOE_ALIGN - 1) + MOE_ROWS
    return -(-worst // MOE_GATHER_ROWS) * MOE_GATHER_ROWS


def _moe_kernel(cnt_ref, start_ref, h2_ref, w_ref, pos_ref, x1_ref, g2_ref, wgu_ref, wd_ref, sgu_ref, sd_ref,
                o_ref, g_all, xs):
    i = pl.program_id(0)
    es = pl.program_id(1)
    tm = h2_ref.shape[0]
    slots = g_all.shape[0]
    slot = lax.broadcasted_iota(jnp.int32, (MOE_ROWS, tm), 0).astype(F32)
    row = lax.broadcasted_iota(jnp.int32, (MOE_ROWS, 1), 0)

    def n_windows(cnt):
        return (cnt + MOE_ROWS - 1) // MOE_ROWS

    def window_start(e, j):
        return pl.multiple_of(start_ref[i * N_EXPERTS + e] + j * MOE_ROWS, MOE_ALIGN)

    @pl.when(es == 0)
    def _():
        g_all[...] = jnp.zeros(g_all.shape, BF16)

        def mark(e, carry):
            pos = pos_ref[pl.ds(e, 1), :]

            def mark_window(j, carry):
                hit = pos == slot + (j * MOE_ROWS).astype(F32)
                g_all[pl.ds(window_start(e, j), MOE_ROWS), :] = hit.astype(BF16)
                return carry

            return lax.fori_loop(0, n_windows(cnt_ref[i * N_EXPERTS + e]), mark_window, carry)

        lax.fori_loop(0, N_EXPERTS, mark, 0)

        def gather(c, carry):
            r0 = pl.multiple_of(c * MOE_GATHER_ROWS, MOE_GATHER_ROWS)
            rows = _dot(g_all[pl.ds(r0, MOE_GATHER_ROWS), :], h2_ref[...])
            xs[pl.ds(r0, MOE_GATHER_ROWS), :] = rows.astype(BF16)
            return carry

        lax.fori_loop(0, slots // MOE_GATHER_ROWS, gather, 0)

    for q in range(MOE_EXPERTS_PER_STEP):
        e = es * MOE_EXPERTS_PER_STEP + q
        cnt = cnt_ref[i * N_EXPERTS + e]
        wrow = w_ref[pl.ds(e, 1), :]

        def window(j, carry, q=q, e=e, cnt=cnt, wrow=wrow):
            r0 = window_start(e, j)
            xg = xs[pl.ds(r0, MOE_ROWS), :]
            out = _swiglu(xg, wgu_ref[q].astype(BF16), wd_ref[q].astype(BF16), D_EXPERT)
            g = g_all[pl.ds(r0, MOE_ROWS), :].astype(F32)
            out = out * jnp.sum(g * wrow, axis=1, keepdims=True)
            mine = row < cnt - j * MOE_ROWS
            xs[pl.ds(r0, MOE_ROWS), :] = jnp.where(mine, out.astype(BF16), xg)
            return carry

        lax.fori_loop(0, n_windows(cnt), window, 0)

    @pl.when(es == pl.num_programs(1) - 1)
    def _():
        y = lax.dot_general(g_all[...], xs[...], (((0,), (0,)), ((), ())), preferred_element_type=F32)
        y = y + _swiglu(h2_ref[...], sgu_ref[...], sd_ref[...], D_SHARED)
        o_ref[...] = x1_ref[...] + _mod(g2_ref) * y


def moe(h2, w_t, pos_t, counts, x1, mod3, mod_row0, t_per_b, wgu, wd, sgu, sd, tm):
    n, d = h2.shape
    tiles_per_b = t_per_b // tm
    eps = MOE_EXPERTS_PER_STEP
    slots = _moe_slots(tm)
    padded = (counts + MOE_ALIGN - 1) // MOE_ALIGN * MOE_ALIGN
    starts = jnp.cumsum(padded, axis=1) - padded
    grid_spec = pltpu.PrefetchScalarGridSpec(
        num_scalar_prefetch=2,
        grid=(n // tm, N_EXPERTS // eps),
        in_specs=[
            pl.BlockSpec((tm, d), lambda i, e, *_: (i, 0)),
            pl.BlockSpec((N_EXPERTS, tm), lambda i, e, *_: (0, i)),
            pl.BlockSpec((N_EXPERTS, tm), lambda i, e, *_: (0, i)),
            pl.BlockSpec((tm, d), lambda i, e, *_: (i, 0)),
            _mod_spec(mod3, 5, tm, tiles_per_b, mod_row0),
            pl.BlockSpec((eps, d, 2 * D_EXPERT), lambda i, e, *_: (e, 0, 0)),
            pl.BlockSpec((eps, D_EXPERT, d), lambda i, e, *_: (e, 0, 0)),
            pl.BlockSpec(sgu.shape, lambda i, e, *_: (0, 0)),
            pl.BlockSpec(sd.shape, lambda i, e, *_: (0, 0)),
        ],
        out_specs=pl.BlockSpec((tm, d), lambda i, e, *_: (i, 0)),
        scratch_shapes=[pltpu.VMEM((slots, tm), BF16), pltpu.VMEM((slots, d), BF16)],
    )
    return pl.pallas_call(
        _moe_kernel,
        grid_spec=grid_spec,
        out_shape=jax.ShapeDtypeStruct((n, d), F32),
        compiler_params=_cparams(("arbitrary", "arbitrary")),
        name="moe",
    )(counts.reshape(-1), starts.reshape(-1).astype(jnp.int32), h2, w_t, pos_t, x1, mod3, wgu, wd, sgu, sd)


SC_WINDOW = 128
PACK_W = 256
MOE_BLOCK_ROWS = 512
HI_MASK = -65536


def _pack_pair(x):
    bits = pltpu.bitcast(x.astype(BF16).astype(F32), jnp.int32)
    return lax.shift_right_logical(bits[:, :PACK_W], 16) | (bits[:, PACK_W:] & HI_MASK)


def _unpack_pair(word):
    lo = pltpu.bitcast(lax.shift_left(word, 16), F32)
    hi = pltpu.bitcast(word & HI_MASK, F32)
    return jnp.concatenate([lo, hi], axis=1)


def _pack_kernel(x_ref, a_ref, b_ref):
    x = x_ref[...]
    a_ref[...] = _pack_pair(x[:, :2 * PACK_W])
    b_ref[...] = _pack_pair(x[:, 2 * PACK_W:])


def pack_rows(x, tm):
    n, d = x.shape
    tok = lambda wd: pl.BlockSpec((tm, wd), lambda i: (i, 0))
    return pl.pallas_call(
        _pack_kernel, grid=(n // tm,), in_specs=[tok(d)], out_specs=(tok(PACK_W), tok(PACK_W)),
        out_shape=(jax.ShapeDtypeStruct((n, PACK_W), jnp.int32),) * 2,
        compiler_params=_cparams(("arbitrary",)), name="pack_rows",
    )(x)


def _slots_kernel(w_ref, pos_ref, base_ref, tri_ref, slot_ref, wt_ref):
    w = w_ref[...]
    pos = pos_ref[...]
    sel = pos >= 0.0
    rank = _dot(tri_ref[...], sel.astype(BF16))
    dest = base_ref[0] + pos
    slots, wts = [], []
    for j in range(TOP_K):
        mine = sel & (rank == float(j))
        slots.append(jnp.sum(jnp.where(mine, dest, 0.0), axis=0, keepdims=True))
        wts.append(jnp.sum(jnp.where(mine, w, 0.0), axis=0, keepdims=True))
    slot_ref[...] = jnp.concatenate(slots, axis=0).astype(jnp.int32)
    wpad = jnp.concatenate(wts + [jnp.zeros((LANES - TOP_K, w.shape[1]), F32)], axis=0)
    wt_ref[...] = jnp.transpose(wpad)


def slots_of(w_t, pos_t, base, tn):
    n = w_t.shape[1]
    tri = jnp.asarray(np.tril(np.ones((N_EXPERTS, N_EXPERTS), np.float32), -1), BF16)
    blk = pl.BlockSpec((N_EXPERTS, tn), lambda i: (0, i))
    return pl.pallas_call(
        _slots_kernel, grid=(n // tn,),
        in_specs=[blk, blk, pl.BlockSpec((1, N_EXPERTS, 1), lambda i: (i, 0, 0)),
                  pl.BlockSpec((N_EXPERTS, N_EXPERTS), lambda i: (0, 0))],
        out_specs=(pl.BlockSpec((TOP_K, tn), lambda i: (0, i)), pl.BlockSpec((tn, LANES), lambda i: (i, 0))),
        out_shape=(jax.ShapeDtypeStruct((TOP_K, n), jnp.int32), jax.ShapeDtypeStruct((n, LANES), F32)),
        compiler_params=_cparams(("arbitrary",)), name="moe_slots",
    )(w_t, pos_t, base, tri)


def sc_scatter_rows(rows, idx, n_out):
    n, d = rows.shape
    m = idx.shape[0]
    nb = n // SC_WINDOW
    mesh = plsc.VectorSubcoreMesh(core_axis_name="core", subcore_axis_name="subcore")

    @functools.partial(pl.kernel, out_type=jax.ShapeDtypeStruct((n_out, d), rows.dtype), mesh=mesh)
    def scatter(x_hbm, i_hbm, o_hbm):
        def body(x_vmem, i_vmem):
            pltpu.sync_copy(x_vmem, o_hbm.at[i_vmem.at[0]])

        pltpu.emit_pipeline(
            body, grid=(m // SC_WINDOW,),
            in_specs=[pl.BlockSpec((SC_WINDOW, d), index_map=lambda i: (i % nb, 0)),
                      pl.BlockSpec((1, SC_WINDOW), index_map=lambda i: (0, i))],
            out_specs=[], core_axis_name=("core", "subcore"), dimension_semantics=(pltpu.PARALLEL,),
        )(x_hbm, i_hbm)

    return scatter(rows, idx.reshape(1, m))


def sc_gather_rows(table, idx):
    d = table.shape[1]
    m = idx.shape[0]
    mesh = plsc.VectorSubcoreMesh(core_axis_name="core", subcore_axis_name="subcore")

    @functools.partial(pl.kernel, out_type=jax.ShapeDtypeStruct((m, d), table.dtype), mesh=mesh)
    def gather(x_hbm, i_hbm, o_hbm):
        def body(i_vmem, o_vmem):
            pltpu.sync_copy(x_hbm.at[i_vmem.at[0]], o_vmem)

        pltpu.emit_pipeline(
            body, grid=(m // SC_WINDOW,),
            in_specs=[pl.BlockSpec((1, SC_WINDOW), index_map=lambda i: (0, i))],
            out_specs=[pl.BlockSpec((SC_WINDOW, d), index_map=lambda i: (i, 0))],
            core_axis_name=("core", "subcore"), dimension_semantics=(pltpu.PARALLEL,),
        )(i_hbm, o_hbm)

    return gather(table, idx.reshape(1, m))


def _experts_kernel(be_ref, nu_ref, xa_ref, xb_ref, wgu_ref, wd_ref, oa_ref, ob_ref, wgu_bf, wd_bf):
    b = pl.program_id(0)

    @pl.when(b < nu_ref[0])
    def _():
        @pl.when((b == 0) | (be_ref[b] != be_ref[jnp.maximum(b - 1, 0)]))
        def _():
            wgu_bf[...] = wgu_ref[0].astype(BF16)
            wd_bf[...] = wd_ref[0].astype(BF16)

        x = jnp.concatenate([_unpack_pair(xa_ref[...]), _unpack_pair(xb_ref[...])], axis=1).astype(BF16)
        out = _swiglu(x, wgu_bf[...], wd_bf[...], D_EXPERT)
        oa_ref[...] = _pack_pair(out[:, :2 * PACK_W])
        ob_ref[...] = _pack_pair(out[:, 2 * PACK_W:])


def experts_sorted(xa, xb, block_expert, n_used, wgu, wd):
    r = xa.shape[0]
    d = wd.shape[2]
    row = lambda b, be, nu: (jnp.minimum(b, nu[0] - 1), 0)
    blk = pl.BlockSpec((MOE_BLOCK_ROWS, PACK_W), row)
    grid_spec = pltpu.PrefetchScalarGridSpec(
        num_scalar_prefetch=2, grid=(r // MOE_BLOCK_ROWS,),
        in_specs=[blk, blk,
                  pl.BlockSpec((1, d, 2 * D_EXPERT), lambda b, be, nu: (be[b], 0, 0)),
                  pl.BlockSpec((1, D_EXPERT, d), lambda b, be, nu: (be[b], 0, 0))],
        out_specs=(blk, blk),
        scratch_shapes=[pltpu.VMEM((d, 2 * D_EXPERT), BF16), pltpu.VMEM((D_EXPERT, d), BF16)],
    )
    return pl.pallas_call(
        _experts_kernel, grid_spec=grid_spec,
        out_shape=(jax.ShapeDtypeStruct((r, PACK_W), jnp.int32),) * 2,
        compiler_params=_cparams(("arbitrary",)), name="moe_experts",
    )(block_expert, n_used, xa, xb, wgu, wd)


def _combine_kernel(ya_ref, yb_ref, wt_ref, h2_ref, x1_ref, g2_ref, sgu_ref, sd_ref, o_ref):
    wt = wt_ref[...]
    acc = _swiglu(h2_ref[...], sgu_ref[...], sd_ref[...], D_SHARED)
    for j in range(TOP_K):
        y = jnp.concatenate([_unpack_pair(ya_ref[j]), _unpack_pair(yb_ref[j])], axis=1)
        acc = acc + wt[:, j:j + 1] * y
    o_ref[...] = x1_ref[...] + _mod(g2_ref) * acc


def combine_sorted(ya, yb, wt, h2, x1, mod3, mod_row0, t_per_b, sgu, sd, tm):
    n, d = h2.shape
    tiles_per_b = t_per_b // tm
    tok = lambda wd: pl.BlockSpec((tm, wd), lambda i: (i, 0))
    yblk = pl.BlockSpec((TOP_K, tm, PACK_W), lambda i: (0, i, 0))
    full = lambda a: pl.BlockSpec(a.shape, lambda i: (0,) * a.ndim)
    return pl.pallas_call(
        _combine_kernel, grid=(n // tm,),
        in_specs=[yblk, yblk, tok(LANES), tok(d), tok(d), _mod_spec(mod3, 5, tm, tiles_per_b, mod_row0),
                  full(sgu), full(sd)],
        out_specs=tok(d), out_shape=jax.ShapeDtypeStruct((n, d), F32),
        compiler_params=_cparams(("arbitrary",)), name="moe_combine",
    )(ya, yb, wt, h2, x1, mod3, sgu, sd)


def moe_sorted(h2, w_t, pos_t, counts, x1, mod3, mod_row0, t_per_b, wgu, wd, sgu, sd, tm, overlap):
    n, d = h2.shape
    assert d == 4 * PACK_W and n % SC_WINDOW == 0
    n_blocks = (TOP_K * n + N_EXPERTS * (MOE_BLOCK_ROWS - 1)) // MOE_BLOCK_ROWS
    total = jnp.sum(counts, axis=0)
    region = (total + MOE_BLOCK_ROWS - 1) // MOE_BLOCK_ROWS * MOE_BLOCK_ROWS
    region_end = jnp.cumsum(region)
    base = (region_end - region)[None, :] + jnp.cumsum(counts, axis=0) - counts
    block_row0 = jnp.arange(n_blocks, dtype=region_end.dtype) * MOE_BLOCK_ROWS
    block_expert = jnp.sum(region_end[None, :] <= block_row0[:, None], axis=1)
    block_expert = jnp.minimum(block_expert, N_EXPERTS - 1).astype(jnp.int32)
    n_used = (region_end[-1:] // MOE_BLOCK_ROWS).astype(jnp.int32)
    slot, wt = slots_of(w_t, pos_t, base.astype(F32).reshape(-1, N_EXPERTS, 1), tm)
    dest = slot.reshape(-1)
    ha, hb = pack_rows(h2, tm)
    rows = n_blocks * MOE_BLOCK_ROWS
    xa, xb = sc_scatter_rows(ha, dest, rows), sc_scatter_rows(hb, dest, rows)
    n_used, rest = lax.optimization_barrier((n_used, overlap()))
    oa, ob = experts_sorted(xa, xb, block_expert, n_used, wgu, wd)
    ya = sc_gather_rows(oa, dest).reshape(TOP_K, n, PACK_W)
    yb = sc_gather_rows(ob, dest).reshape(TOP_K, n, PACK_W)
    return combine_sorted(ya, yb, wt, h2, x1, mod3, mod_row0, t_per_b, sgu, sd, tm), rest


GATHER_PAGES = 8


def _gather_kernel(pt_ref, *refs):
    pages, new_ref = refs[:GATHER_PAGES], refs[GATHER_PAGES]
    rows_ref, cmpx_ref, stage_ref = refs[GATHER_PAGES + 1:]
    step = pl.program_id(1)
    last = pl.num_programs(1) - 1
    n_rows = GATHER_PAGES * PAGE_SIZE

    @pl.when(step < last)
    def _():
        for k in range(GATHER_PAGES):
            sl = slice(k * PAGE_SIZE, (k + 1) * PAGE_SIZE)
            for r in range(4):
                tile = jnp.transpose(pages[k][0, r])
                if r < 2:
                    stage_ref[r, sl, :] = tile
                else:
                    rows_ref[0, sl, (r - 2) * KV_WIDTH:(r - 1) * KV_WIDTH] = tile.astype(BF16)

    @pl.when(step == last)
    def _():
        new = new_ref[0]
        tn = new.shape[0]
        stage_ref[...] = jnp.zeros(stage_ref.shape, F32)
        for s in range(2):
            stage_ref[s, 0:tn, :] = new[:, s * KV_WIDTH:(s + 1) * KV_WIDTH]
        pad = jnp.zeros((n_rows - tn, 2 * KV_WIDTH), F32)
        rows_ref[0] = jnp.concatenate([new[:, 2 * KV_WIDTH:], pad], axis=0).astype(BF16)

    _stride_block_store(stage_ref, cmpx_ref, n_rows)


def gather_pages(cache_t, page_table, new_rows):
    b, n_pages = page_table.shape
    steps = n_pages // GATHER_PAGES
    rows = GATHER_PAGES * PAGE_SIZE
    s_out = (steps + 1) * rows

    def page_spec(k):
        def idx(i, s, pt):
            p = jnp.minimum(s, steps - 1) * GATHER_PAGES + k
            return (pt[i * n_pages + p], 0, 0, 0)
        return pl.BlockSpec((1, 4, KV_WIDTH, PAGE_SIZE), idx)

    grid_spec = pltpu.PrefetchScalarGridSpec(
        num_scalar_prefetch=1,
        grid=(b, steps + 1),
        in_specs=[page_spec(k) for k in range(GATHER_PAGES)]
        + [pl.BlockSpec((1,) + new_rows.shape[1:], lambda i, s, pt: (i, 0, 0))],
        out_specs=(
            pl.BlockSpec((1, rows, 2 * KV_WIDTH), lambda i, s, pt: (i, s, 0)),
            pl.BlockSpec((1, rows // CMP_STRIDE, CMP_STRIDE * 2 * KV_WIDTH), lambda i, s, pt: (i, s, 0)),
        ),
        scratch_shapes=[pltpu.VMEM((2, rows, KV_WIDTH), F32)],
    )
    return pl.pallas_call(
        _gather_kernel,
        grid_spec=grid_spec,
        out_shape=(jax.ShapeDtypeStruct((b, s_out, 2 * KV_WIDTH), BF16),
                   jax.ShapeDtypeStruct((b, s_out // CMP_STRIDE, CMP_STRIDE * 2 * KV_WIDTH), BF16)),
        compiler_params=_cparams(("arbitrary", "arbitrary")),
        name="gather_pages",
    )(page_table.reshape(-1), *([cache_t] * GATHER_PAGES), new_rows)


def _attention(qp, cmpx, kvb, sel_col, winb, misc, cmp_w, q_off, win_pos0, tq):
    t = qp.shape[1]
    cur_lo, cur_hi = q_off // SEL_BLOCK, (q_off + t - 1) // SEL_BLOCK
    assert cur_hi < N_SEL_LANES or (cur_lo == cur_hi == N_SEL_LANES), (q_off, t)
    n_pick = N_SEL - (1 if cur_hi >= N_SEL_LANES else 0)
    kcv = compress(cmpx, *cmp_w)
    o_cmp, mneg = cmp_select(qp, kcv, q_off, n_pick, tq)
    return sel_win_attention(qp, mneg, kvb, sel_col, winb, o_cmp, misc, q_off, win_pos0, tq)


def kernel(x_prompt, x_sample, cache_kv, cache_win, state_ssm, state_conv, page_table, c_prompt, c_sample, w_ada, b_ada, norm1_w, norm2_w, w_in, q_norm_w, k_norm_w, cmp_pe, cmp_w1, cmp_w2, attn_out_norm_w, conv_w, conv_b, dt_bias, a_log, d_skip, ssm_norm_w, w_out, w_router, e_bias, w_exp_gu, w_exp_down, w_sh_gu, w_sh_down):
    xp, xq = x_prompt, x_sample
    bp, tp, d = xp.shape
    bq, tq, _ = xq.shape
    depth = w_ada.shape[0]
    past_len = page_table.shape[1] * PAGE_SIZE
    nq = bq * tq
    tq_pad = LANES // GQA_GROUP
    assert tp % TOKEN_TILE == 0 and tp >= WINDOW and nq % 8 == 0 and tq <= tq_pad
    pos_p = jnp.arange(tp, dtype=jnp.int32)
    pos_q = jnp.tile(past_len + jnp.arange(tq, dtype=jnp.int32), bq)
    c_all = jnp.concatenate([c_prompt, c_sample], axis=0)
    c_all = jnp.pad(c_all, ((0, -c_all.shape[0] % 8), (0, 0)))
    outs = [[] for _ in range(8)]
    for l in range(depth):
        mod = adaln_all(c_all, w_ada[l], b_ada[l])
        mod_p = mod.reshape(mod.shape[0], 1, 6 * d)
        mod_q = jnp.repeat(mod[bp:bp + bq], tq, axis=0)
        wp = _prep_w_in(w_in[l])
        cmp_w = _prep_compress(cmp_pe[l], cmp_w1[l], cmp_w2[l])
        wo = w_out[l].astype(BF16)
        wgu, wd = w_exp_gu[l], w_exp_down[l]
        sgu, sd = w_sh_gu[l].astype(BF16), w_sh_down[l].astype(BF16)
        ssm_w = (conv_w[l], conv_b[l], dt_bias[l], a_log[l], d_skip[l], ssm_norm_w[l])
        n1w, n2w, anw = norm1_w[l:l + 1], norm2_w[l:l + 1], attn_out_norm_w[l:l + 1]

        qp, kvb, win, winb, z, xbc, misc, kvt, cmpx = inproj(xp, mod_p, 0, n1w, wp, q_norm_w[l], k_norm_w[l], pos_p,
                                                            TOKEN_TILE, True)
        r3 = lambda a: a.reshape(bp, tp, a.shape[-1])
        o_attn = _attention(r3(qp), cmpx, r3(kvb), 2, r3(winb), r3(misc), cmp_w, 0, 0, QUERY_TILE)
        y_ssm, h_new, conv_new = ssd(r3(xbc), r3(z), r3(misc), jnp.zeros((bp, CONV_WIDTH - 1, CONV_DIM), F32),
                                     jnp.zeros((bp, SSM_HEADS, SSM_HEAD_DIM, SSM_STATE), F32), *ssm_w)
        x1, h2, lg = merge(o_attn, y_ssm, xp, mod_p, 0, anw, wo, n2w, w_router[l], TOKEN_TILE)
        w_t, pos_t, cnt = route(lg, e_bias[l], TOKEN_TILE)
        outs[0].append(jnp.transpose(kvt.reshape(bp, 4, N_KV_HEADS, HEAD_DIM, tp), (0, 4, 1, 2, 3)))
        outs[1].append(win.reshape(bp, tp, 2, N_KV_HEADS, HEAD_DIM)[:, tp - WINDOW:])
        outs[2].append(h_new)
        outs[3].append(conv_new)

        xq1 = xq.reshape(1, nq, d)
        rq = lambda a: a.reshape(bq, tq, a.shape[-1])
        padq = lambda a: jnp.pad(rq(a), ((0, 0), (0, tq_pad - tq), (0, 0)))

        def sample_front():
            proj = inproj(xq1, mod_q, 0, n1w, wp, q_norm_w[l], k_norm_w[l], pos_q, nq, False)
            cache_t = jnp.transpose(cache_kv[l], (0, 2, 3, 4, 1)).reshape(cache_kv.shape[1], 4, KV_WIDTH, PAGE_SIZE)
            past, cmpx = gather_pages(cache_t, page_table, rq(proj[-1]))
            return proj, past, cmpx

        xp, (proj, past, cmpx) = moe_sorted(h2, w_t, pos_t, cnt, x1, mod_p, 0, tp, wgu, wd, sgu, sd, TOKEN_TILE,
                                            sample_front)
        xp = xp.reshape(bp, tp, d)
        qp, kvb, win, winb, z, xbc, misc, kv = proj
        win_all = jnp.concatenate([cache_win[l].reshape(bq, WINDOW, 2 * KV_WIDTH).astype(BF16), rq(winb),
                                   jnp.zeros((bq, -(WINDOW + tq_pad) % WIN_CHUNK + tq_pad - tq, 2 * KV_WIDTH), BF16)],
                                  axis=1)
        o_attn = _attention(padq(qp), cmpx, past, 0, win_all, padq(misc), cmp_w, past_len, past_len - WINDOW,
                            tq_pad)[:, :tq]
        y_ssm, h_new, conv_new = ssd(rq(xbc), rq(z), rq(misc), state_conv[l], state_ssm[l], *ssm_w)
        x1, h2, lg = merge(o_attn.reshape(1, nq, ATTN_WIDTH), y_ssm.reshape(1, nq, SSM_WIDTH), xq1, mod_q, 0,
                           anw, wo, n2w, w_router[l], nq)
        w_t, pos_t, cnt = route(lg, e_bias[l], nq)
        xq = moe(h2, w_t, pos_t, cnt, x1, mod_q, 0, nq, wgu, wd, sgu, sd, nq).reshape(bq, tq, d)
        win_rows = win.reshape(bq, tq, 2, N_KV_HEADS, HEAD_DIM)
        outs[4].append(kv.reshape(bq, tq, 4, N_KV_HEADS, HEAD_DIM))
        outs[5].append(jnp.concatenate([cache_win[l], win_rows.astype(cache_win.dtype)], axis=1)[:, tq:])
        outs[6].append(h_new)
        outs[7].append(conv_new)
    return (xp, xq) + tuple(jnp.stack(o) for o in outs)
```

```python
import functools
import math

import jax
import jax.numpy as jnp
import numpy as np
from jax import lax
from jax.experimental import pallas as pl
from jax.experimental.pallas import tpu as pltpu
from jax.experimental.pallas import tpu_sc as plsc

D_MODEL = 1024
PAGE_SIZE = 128
HEAD_DIM = 64
N_Q_HEADS = 8
N_KV_HEADS = 2
GQA_GROUP = N_Q_HEADS // N_KV_HEADS
ATTN_WIDTH = N_Q_HEADS * HEAD_DIM
KV_WIDTH = N_KV_HEADS * HEAD_DIM
ROPE_DIM = HEAD_DIM // 4
ROPE_THETA = 500000.0
CMP_LEN = 32
CMP_STRIDE = 16
CMP_HIDDEN = 4 * HEAD_DIM
SEL_BLOCK = 64
N_SEL = 16
N_LOCAL = 2
WINDOW = 512
SSM_HEADS = 8
SSM_HEAD_DIM = 64
SSM_WIDTH = SSM_HEADS * SSM_HEAD_DIM
SSM_GROUPS = 2
SSM_STATE = 128
CONV_WIDTH = 4
CONV_DIM = SSM_WIDTH + 2 * SSM_GROUPS * SSM_STATE
SSD_CHUNK = 128
MIX_WIDTH = ATTN_WIDTH + SSM_WIDTH
N_EXPERTS = 64
N_EXPERT_GROUPS = 8
TOPK_GROUPS = 4
TOP_K = 8
D_EXPERT = 256
D_SHARED = 256
ROUTED_SCALE = 2.5
IN_SIZES = (ATTN_WIDTH, 6 * KV_WIDTH, 3 * N_Q_HEADS, SSM_WIDTH, CONV_DIM, SSM_HEADS)
N_IN = sum(IN_SIZES)
EPS = 1e-6
NEG = -1e30
BIG = 1e6

LANES = 128
TOKEN_TILE = 512
QUERY_TILE = 128
VMEM_LIMIT = 56 * 1024 * 1024

BF16 = jnp.bfloat16
F32 = jnp.float32
LOG2E = math.log2(math.e)


def _cparams(sem, flags=None):
    return pltpu.CompilerParams(dimension_semantics=sem, vmem_limit_bytes=VMEM_LIMIT, flags=flags)


def _silu(x):
    return x * jax.nn.sigmoid(x)


def _dot(a, b):
    return jnp.dot(a, b, preferred_element_type=F32)


def _dot_nt(a, b):
    return lax.dot_general(a, b, (((1,), (1,)), ((), ())), preferred_element_type=F32)


def _mod_spec(mod, col, tm, tiles_per_b, row0):
    if mod.ndim == 3:
        return pl.BlockSpec((1, 1, D_MODEL), lambda i, *_: (row0 + i // tiles_per_b, 0, col))
    return pl.BlockSpec((tm, D_MODEL), lambda i, *_: (i, col))


def _mod(ref):
    return ref[0] if len(ref.shape) == 3 else ref[...]


def _adaln_kernel(c_ref, w_ref, b_ref, o_ref):
    c = c_ref[...]
    a = _silu(c).astype(BF16)
    o_ref[...] = _dot(a, w_ref[...].astype(BF16)) + b_ref[...]


def adaln_all(c_all, w_ada, b_ada):
    rows = c_all.shape[0]
    n = w_ada.shape[1]
    tn = 1024
    return pl.pallas_call(
        _adaln_kernel,
        grid=(n // tn,),
        in_specs=[
            pl.BlockSpec((rows, D_MODEL), lambda j: (0, 0)),
            pl.BlockSpec((D_MODEL, tn), lambda j: (0, j)),
            pl.BlockSpec((1, tn), lambda j: (0, j)),
        ],
        out_specs=pl.BlockSpec((rows, tn), lambda j: (0, j)),
        out_shape=jax.ShapeDtypeStruct((rows, n), F32),
        compiler_params=_cparams(("arbitrary",)),
        name="adaln",
    )(c_all, w_ada, b_ada.reshape(1, n))


_C_Q = 0
_C_KV = _C_Q + ATTN_WIDTH
_C_Z = _C_KV + 6 * KV_WIDTH
_C_XBC = _C_Z + SSM_WIDTH
_C_MISC = _C_XBC + CONV_DIM
N_IN_PAD = _C_MISC + LANES
N_GATES = 3 * N_Q_HEADS


def _prep_w_in(w_in):
    s = np.cumsum((0,) + IN_SIZES)
    q, kv, g, z, xbc, dt = (w_in[:, int(s[i]):int(s[i + 1])] for i in range(6))
    pad = jnp.zeros((w_in.shape[0], LANES - N_GATES - SSM_HEADS), w_in.dtype)
    return jnp.concatenate([q, kv, z, xbc, dt, g, pad], axis=1).astype(BF16)


def _group_mean_matrix(width):
    i = np.arange(width)
    m = (i[:, None] // HEAD_DIM == i[None, :] // HEAD_DIM).astype(np.float32) / HEAD_DIM
    return jnp.asarray(m, BF16)


def _rope_tables(pos):
    half = ROPE_DIM // 2
    inv_freq = ROPE_THETA ** (-jnp.arange(half, dtype=F32) / half)
    ang = pos.astype(F32)[:, None] * inv_freq[None, :]
    cos, sin = jnp.cos(ang), jnp.sin(ang)
    t = pos.shape[0]
    one = jnp.ones((t, HEAD_DIM - ROPE_DIM), F32)
    zero = jnp.zeros((t, HEAD_DIM - ROPE_DIM), F32)
    zh = jnp.zeros((t, half), F32)
    c = jnp.concatenate([cos, cos, one], axis=1)
    s_up = jnp.concatenate([-sin, zh, zero], axis=1)
    s_dn = jnp.concatenate([zh, sin, zero], axis=1)
    rep = LANES // HEAD_DIM
    return jnp.tile(c, (1, rep)), jnp.tile(s_up, (1, rep)), jnp.tile(s_dn, (1, rep))


def _rope(x, c, s_up, s_dn):
    w = x.shape[1]
    half = ROPE_DIM // 2
    rep = w // LANES
    ct = jnp.concatenate([c] * rep, axis=1) if rep > 1 else c
    su = jnp.concatenate([s_up] * rep, axis=1) if rep > 1 else s_up
    sd = jnp.concatenate([s_dn] * rep, axis=1) if rep > 1 else s_dn
    up = pltpu.roll(x, w - half, axis=1)
    dn = pltpu.roll(x, half, axis=1)
    return x * ct + up * su + dn * sd


def _stride_block_store(stage_ref, cmpx_ref, n_rows):
    nb = n_rows // CMP_STRIDE
    lane = lax.broadcasted_iota(jnp.int32, (nb, KV_WIDTH), 1)
    lo = lane < HEAD_DIM
    span = CMP_STRIDE * HEAD_DIM
    for s in range(2):
        for m in range(CMP_STRIDE // 2):
            r0 = stage_ref[s, pl.ds(2 * m, nb, stride=CMP_STRIDE), :]
            r1 = stage_ref[s, pl.ds(2 * m + 1, nb, stride=CMP_STRIDE), :]
            head0 = jnp.where(lo, r0, pltpu.roll(r1, HEAD_DIM, axis=1))
            head1 = jnp.where(lo, pltpu.roll(r0, HEAD_DIM, axis=1), r1)
            for h, piece in enumerate((head0, head1)):
                c0 = (2 * s + h) * span + m * KV_WIDTH
                cmpx_ref[0, :, c0:c0 + KV_WIDTH] = piece.astype(BF16)


def _inproj_kernel(x_ref, shift_ref, scale_ref, nw_ref, w_ref, qw_ref, kw_ref, gq_ref, gk_ref,
                   c_ref, su_ref, sd_ref,
                   qp_ref, kvb_ref, win_ref, winb_ref, z_ref, xbc_ref, misc_ref, *rest, seq_layout):
    x = x_ref[...]
    ms = jnp.mean(x * x, axis=-1, keepdims=True)
    h = x * lax.rsqrt(ms + EPS) * nw_ref[...]
    h = h * (1.0 + _mod(scale_ref)) + _mod(shift_ref)
    hb = h.astype(BF16)
    c, su, sd = c_ref[...], su_ref[...], sd_ref[...]

    q = _dot(hb, w_ref[:, _C_Q:_C_Q + ATTN_WIDTH])
    qms = _dot((q * q).astype(BF16), gq_ref[...])
    q = q * lax.rsqrt(qms + EPS) * qw_ref[...]
    q = _rope(q, c, su, sd) * (HEAD_DIM ** -0.5 * LOG2E)
    lane = lax.broadcasted_iota(jnp.int32, q.shape, 1) % LANES
    lo = lane < HEAD_DIM
    q_up = pltpu.roll(q, ATTN_WIDTH - HEAD_DIM, axis=1)
    q_dn = pltpu.roll(q, HEAD_DIM, axis=1)
    zero = jnp.zeros_like(q)
    nat_lo = jnp.where(lo, q, zero)
    nat_hi = jnp.where(lo, zero, q)
    up_lo = jnp.where(lo, q_up, zero)
    dn_hi = jnp.where(lo, zero, q_dn)
    blocks = []
    for hd in range(N_Q_HEADS):
        pair = hd // 2
        sl = slice(pair * LANES, (pair + 1) * LANES)
        if hd < GQA_GROUP:
            blocks.append((nat_lo if hd % 2 == 0 else up_lo)[:, sl])
        else:
            blocks.append((dn_hi if hd % 2 == 0 else nat_hi)[:, sl])
    qp_ref[...] = jnp.concatenate(blocks, axis=1).astype(BF16)

    kv = _dot(hb, w_ref[:, _C_KV:_C_KV + 6 * KV_WIDTH])
    outs = []
    for br in range(3):
        k = kv[:, br * 2 * KV_WIDTH:br * 2 * KV_WIDTH + KV_WIDTH]
        v = kv[:, br * 2 * KV_WIDTH + KV_WIDTH:(br + 1) * 2 * KV_WIDTH]
        kms = _dot((k * k).astype(BF16), gk_ref[...])
        k = k * lax.rsqrt(kms + EPS) * kw_ref[:, br * KV_WIDTH:(br + 1) * KV_WIDTH]
        k = _rope(k, c, su, sd)
        outs += [k, v]
    kvrows = jnp.concatenate(outs[:4], axis=1)
    winrows = jnp.concatenate(outs[4:], axis=1)
    kvb_ref[...] = kvrows.astype(BF16)
    win_ref[...] = winrows
    winb_ref[...] = winrows.astype(BF16)
    if seq_layout:
        kvt_ref, cmpx_ref, stage_ref = rest
        tm = kvrows.shape[0]
        for r in range(4):
            kvt_ref[0, r] = jnp.transpose(kvrows[:, r * KV_WIDTH:(r + 1) * KV_WIDTH])
        for s in range(2):
            stage_ref[s] = kvrows[:, s * KV_WIDTH:(s + 1) * KV_WIDTH]
        _stride_block_store(stage_ref, cmpx_ref, tm)
    else:
        rest[0][...] = kvrows

    z_ref[...] = _dot(hb, w_ref[:, _C_Z:_C_Z + SSM_WIDTH])
    xbc_ref[...] = _dot(hb, w_ref[:, _C_XBC:_C_XBC + CONV_DIM])
    misc_ref[...] = _dot(hb, w_ref[:, _C_MISC:_C_MISC + LANES])


def inproj(x, mod3, mod_row0, norm_w, wp, q_norm_w, k_norm_w, pos, tm, seq_layout):
    b, t, d = x.shape
    n = b * t
    tiles_per_b = t // tm
    xf = x.reshape(n, d)
    c, su, sd = _rope_tables(pos)
    qw = jnp.tile(q_norm_w, N_Q_HEADS).reshape(1, ATTN_WIDTH)
    kw = jnp.concatenate([jnp.tile(k_norm_w[i], N_KV_HEADS) for i in range(3)]).reshape(1, 3 * KV_WIDTH)
    gq = _group_mean_matrix(ATTN_WIDTH)
    gk = _group_mean_matrix(KV_WIDTH)

    def mod_spec(col):
        return _mod_spec(mod3, col, tm, tiles_per_b, mod_row0)

    def tok(wd):
        return pl.BlockSpec((tm, wd), lambda i: (i, 0))

    def full(a):
        return pl.BlockSpec(a.shape, lambda i: (0,) * a.ndim)

    rope_spec = pl.BlockSpec((tm, LANES), lambda i: (i % tiles_per_b, 0))
    out_shape = [
        jax.ShapeDtypeStruct((n, N_Q_HEADS * LANES), BF16),
        jax.ShapeDtypeStruct((n, 4 * KV_WIDTH), BF16),
        jax.ShapeDtypeStruct((n, 2 * KV_WIDTH), F32),
        jax.ShapeDtypeStruct((n, 2 * KV_WIDTH), BF16),
        jax.ShapeDtypeStruct((n, SSM_WIDTH), F32),
        jax.ShapeDtypeStruct((n, CONV_DIM), F32),
        jax.ShapeDtypeStruct((n, LANES), F32),
    ]
    out_specs = [tok(s.shape[1]) for s in out_shape]
    scratch = []
    if seq_layout:
        out_shape += [jax.ShapeDtypeStruct((b, 4, KV_WIDTH, t), F32),
                      jax.ShapeDtypeStruct((b, t // CMP_STRIDE, CMP_STRIDE * 2 * KV_WIDTH), BF16)]
        out_specs += [pl.BlockSpec((1, 4, KV_WIDTH, tm), lambda i: (i // tiles_per_b, 0, 0, i % tiles_per_b)),
                      pl.BlockSpec((1, tm // CMP_STRIDE, CMP_STRIDE * 2 * KV_WIDTH),
                                   lambda i: (i // tiles_per_b, i % tiles_per_b, 0))]
        scratch = [pltpu.VMEM((2, tm, KV_WIDTH), F32)]
    else:
        out_shape += [jax.ShapeDtypeStruct((n, 4 * KV_WIDTH), F32)]
        out_specs += [tok(4 * KV_WIDTH)]
    return pl.pallas_call(
        functools.partial(_inproj_kernel, seq_layout=seq_layout),
        grid=(n // tm,),
        in_specs=[tok(d), mod_spec(0), mod_spec(1), full(norm_w), full(wp), full(qw), full(kw), full(gq), full(gk),
                  rope_spec, rope_spec, rope_spec],
        out_specs=tuple(out_specs),
        out_shape=tuple(out_shape),
        scratch_shapes=scratch,
        compiler_params=_cparams(("arbitrary",)),
        name="inproj",
    )(xf, mod3, mod3, norm_w, wp, qw, kw, gq, gk, c, su, sd)


def _prep_compress(cmp_pe, cmp_w1, cmp_w2):
    span = CMP_STRIDE * HEAD_DIM
    w1p = jnp.concatenate([cmp_w1[:, :span], cmp_w1[:, span:]], axis=2).astype(BF16)
    pep = cmp_pe.reshape(2, 2, span)
    eye = jnp.eye(N_KV_HEADS, dtype=F32)
    w2p = jnp.einsum("poe,hg->phoge", cmp_w2, eye).reshape(2, N_KV_HEADS, CMP_HIDDEN, KV_WIDTH).astype(BF16)
    return w1p, pep, w2p


def _compress_kernel(x_ref, w1_ref, pe_ref, w2_ref, o_ref):
    part = pl.program_id(1)
    nb = x_ref.shape[1]
    span = CMP_STRIDE * HEAD_DIM
    pe = pe_ref[0]
    out = jnp.zeros((nb, KV_WIDTH), F32)
    for h in range(N_KV_HEADS):
        xk = x_ref[0, :, h * span:(h + 1) * span]
        xv = x_ref[0, :, (N_KV_HEADS + h) * span:(N_KV_HEADS + h + 1) * span]
        x = jnp.where(part == 0, xk, xv).astype(F32)
        u = _dot((x + pe[0:1]).astype(BF16), w1_ref[0, :, :CMP_HIDDEN])
        v = _dot((x + pe[1:2]).astype(BF16), w1_ref[0, :, CMP_HIDDEN:])
        h1 = u + pltpu.roll(v, nb - 1, axis=0)
        out = out + _dot(_silu(h1).astype(BF16), w2_ref[0, h])
    row = lax.broadcasted_iota(jnp.int32, out.shape, 0)
    o_ref[0, 0] = jnp.where(row < nb - 1, out, 0.0).astype(o_ref.dtype)


def compress(x, w1p, pep, w2p):
    b, nb, width = x.shape
    return pl.pallas_call(
        _compress_kernel,
        grid=(b, 2),
        in_specs=[
            pl.BlockSpec((1, nb, width), lambda i, p: (i, 0, 0)),
            pl.BlockSpec((1,) + w1p.shape[1:], lambda i, p: (p, 0, 0)),
            pl.BlockSpec((1,) + pep.shape[1:], lambda i, p: (p, 0, 0)),
            pl.BlockSpec((1,) + w2p.shape[1:], lambda i, p: (p, 0, 0, 0)),
        ],
        out_specs=pl.BlockSpec((1, 1, nb, KV_WIDTH), lambda i, p: (i, p, 0, 0)),
        out_shape=jax.ShapeDtypeStruct((b, 2, nb, KV_WIDTH), BF16),
        compiler_params=_cparams(("arbitrary", "arbitrary")),
        name="compress",
    )(x, w1p, pep, w2p)


N_SEL_LANES = LANES


def _cover_matrix(nb):
    c = np.arange(nb)[:, None]
    j = np.arange(N_SEL_LANES)[None, :]
    start = c * CMP_STRIDE
    m = (start < (j + 1) * SEL_BLOCK) & (start + CMP_LEN > j * SEL_BLOCK)
    return jnp.asarray(m.astype(np.float32), BF16)


def _place_heads(res, kv):
    lane = lax.broadcasted_iota(jnp.int32, res[0].shape, 1)
    lo = lane < HEAD_DIM
    blocks = []
    for pair in range(GQA_GROUP // 2):
        a, b = res[2 * pair], res[2 * pair + 1]
        if kv == 0:
            blocks.append(jnp.where(lo, a, pltpu.roll(b, HEAD_DIM, axis=1)))
        else:
            blocks.append(jnp.where(lo, pltpu.roll(a, HEAD_DIM, axis=1), b))
    return jnp.concatenate(blocks, axis=1)


def _group_rows(q_ref, kv):
    heads = range(kv * GQA_GROUP, (kv + 1) * GQA_GROUP)
    return jnp.concatenate([q_ref[0, :, hd * LANES:(hd + 1) * LANES] for hd in heads], axis=0)


def _heads_from_transposed(out_t, tq, kv):
    out = jnp.transpose(out_t)
    return _place_heads([out[g * tq:(g + 1) * tq] for g in range(GQA_GROUP)], kv)


def _cmp_select_kernel(q_ref, kc_ref, vc_ref, covt_ref, o_ref, m_ref, *, q_off, n_pick):
    tq = q_ref.shape[1]
    rows = GQA_GROUP * tq
    nb = kc_ref.shape[2]
    wl = max(tq, LANES)
    assert tq % LANES == 0 or rows == LANES
    t0 = q_off + pl.program_id(1) * tq
    kc = kc_ref[0, 0]
    vc = vc_ref[0, 0]
    qpos = t0 + lax.broadcasted_iota(jnp.int32, (nb, rows), 1) % tq
    cend = lax.broadcasted_iota(jnp.int32, (nb, rows), 0) * CMP_STRIDE + (CMP_LEN - 1)
    valid = cend <= qpos
    blk = lax.broadcasted_iota(jnp.int32, (N_SEL_LANES, wl), 0)
    cur = (t0 + lax.broadcasted_iota(jnp.int32, (N_SEL_LANES, wl), 1) % tq) // SEL_BLOCK
    forced = (blk == 0) | ((blk <= cur) & (blk > cur - N_LOCAL))
    o_groups = []
    for kv in range(N_KV_HEADS):
        s = _dot_nt(kc, _group_rows(q_ref, kv))
        s = jnp.where(valid, s, NEG)
        e = jnp.exp2(s - jnp.max(s, axis=0, keepdims=True))
        p = e / jnp.sum(e, axis=0, keepdims=True)
        p = jnp.where(valid, p, 0.0)
        o_t = lax.dot_general(vc, p.astype(BF16), (((0,), (0,)), ((), ())), preferred_element_type=F32)
        o_groups.append(_heads_from_transposed(o_t, tq, kv))
        if tq % LANES == 0:
            psum = sum(p[:, g * tq:(g + 1) * tq] for g in range(GQA_GROUP))
        else:
            psum = p + sum(pltpu.roll(p, g * tq, axis=1) for g in range(1, GQA_GROUP))
        hi, lo = _split2(psum)
        imp = _dot(covt_ref[...], hi) + _dot(covt_ref[...], lo)
        x = jnp.where(forced, BIG, jnp.where(blk > cur, -BIG, imp))
        sel = jnp.zeros(x.shape, jnp.bool_)
        for _ in range(n_pick):
            mx = jnp.max(x, axis=0, keepdims=True)
            idx = jnp.min(jnp.where(x == mx, blk, N_SEL_LANES), axis=0, keepdims=True)
            hit = blk == idx
            sel = sel | hit
            x = jnp.where(hit, -jnp.inf, x)
        mneg = jnp.transpose(jnp.where(sel, 0.0, NEG))
        m_ref[0, kv] = mneg[:tq].astype(m_ref.dtype)
    o_ref[0] = jnp.concatenate(o_groups, axis=1)


def cmp_select(qp, kcv, q_off, n_pick, tq):
    b, t, _ = qp.shape
    nb = kcv.shape[2]
    cover = jnp.transpose(_cover_matrix(nb))
    return pl.pallas_call(
        functools.partial(_cmp_select_kernel, q_off=q_off, n_pick=n_pick),
        grid=(b, t // tq),
        in_specs=[
            pl.BlockSpec((1, tq, N_Q_HEADS * LANES), lambda i, j: (i, j, 0)),
            pl.BlockSpec((1, 1, nb, KV_WIDTH), lambda i, j: (i, 0, 0, 0)),
            pl.BlockSpec((1, 1, nb, KV_WIDTH), lambda i, j: (i, 1, 0, 0)),
            pl.BlockSpec((N_SEL_LANES, nb), lambda i, j: (0, 0)),
        ],
        out_specs=(
            pl.BlockSpec((1, tq, ATTN_WIDTH), lambda i, j: (i, j, 0)),
            pl.BlockSpec((1, N_KV_HEADS, tq, N_SEL_LANES), lambda i, j: (i, 0, j, 0)),
        ),
        out_shape=(
            jax.ShapeDtypeStruct((b, t, ATTN_WIDTH), F32),
            jax.ShapeDtypeStruct((b, N_KV_HEADS, t, N_SEL_LANES), BF16),
        ),
        compiler_params=_cparams(("arbitrary", "arbitrary")),
        name="cmp_select",
    )(qp, kcv, kcv, cover)


SEL_TILE_ELEMS = 512 * 512
SEL_WIDTHS = (4, 2, 1)
WIN_CHUNK = 256
WIN_SPAN = 3


def _block_onehot(s):
    key = np.arange(s)[:, None]
    j = np.arange(N_SEL_LANES)[None, :]
    return jnp.asarray((key // SEL_BLOCK == j).astype(np.float32), BF16)


def _gate_expand():
    m = np.zeros((3, LANES, ATTN_WIDTH), np.float32)
    for br in range(3):
        for hd in range(N_Q_HEADS):
            m[br, SSM_HEADS + 3 * hd + br, hd * HEAD_DIM:(hd + 1) * HEAD_DIM] = 1.0
    return jnp.asarray(m, BF16)


def _flash_update(ss, v, m_ref, acc_ref):
    lane = lax.broadcasted_iota(jnp.int32, v.shape, 1)
    one = jnp.ones(v.shape, v.dtype)
    stage = []
    for k, s in enumerate(ss):
        m_old = m_ref[k]
        m_new = jnp.maximum(m_old, jnp.max(s, axis=0, keepdims=True))
        alpha = jnp.exp2(m_old - m_new)
        p = jnp.exp2(s - m_new)
        m_ref[k] = m_new
        stage.append((alpha, p.astype(BF16)))
    for k, (alpha, p) in enumerate(stage):
        vk = jnp.where((lane < HEAD_DIM) == (k == 0), v, one)
        pv = lax.dot_general(vk, p, (((0,), (0,)), ((), ())), preferred_element_type=F32)
        acc_ref[k] = alpha * acc_ref[k] + pv


def _sel_chunk(rows, n_keys):
    chunk = SEL_TILE_ELEMS // rows
    while n_keys % chunk:
        chunk //= 2
    return chunk


def _sel_win_kernel(q_ref, mneg_ref, ksel_ref, vsel_ref, et_ref, kwin_ref, vwin_ref, ocmp_ref, misc_ref, eg_ref,
                    o_ref, lhs_ref, m_ref, acc_ref, *, q_off, win_pos0):
    tq = q_ref.shape[1]
    rows = GQA_GROUP * tq
    SEL_CHUNK = _sel_chunk(rows, ksel_ref.shape[1])
    t0 = q_off + pl.program_id(1) * tq
    n_sel = lax.shift_right_logical(t0 + tq - 1, int(math.log2(SEL_CHUNK))) + 1
    w_lo = jnp.maximum(t0 - (WINDOW - 1) - win_pos0, 0) // WIN_CHUNK
    w_hi = (t0 + tq - 1 - win_pos0) // WIN_CHUNK + 1

    def qrow(n_keys):
        return lax.broadcasted_iota(jnp.int32, (n_keys, rows), 1) % tq + t0

    def init():
        m_ref[...] = jnp.full(m_ref.shape, NEG, F32)
        acc_ref[...] = jnp.zeros(acc_ref.shape, F32)

    def finish():
        outs = []
        for kv in range(N_KV_HEADS):
            acc = acc_ref[kv]
            denom_row = HEAD_DIM * (1 - kv)
            outs.append(_heads_from_transposed(acc / acc[denom_row:denom_row + 1, :], tq, kv))
        return jnp.concatenate(outs, axis=1)

    for kv in range(N_KV_HEADS):
        for g in range(GQA_GROUP):
            hd = kv * GQA_GROUP + g
            lhs_ref[kv, g * tq:(g + 1) * tq, :LANES] = q_ref[0, :, hd * LANES:(hd + 1) * LANES]
            lhs_ref[kv, g * tq:(g + 1) * tq, LANES:] = mneg_ref[0, kv]

    init()

    def sel_step(i, carry, causal, width):
        n_keys = width * SEL_CHUNK
        r0 = pl.multiple_of(i * n_keys, n_keys)
        rhs = jnp.concatenate([ksel_ref[0, pl.ds(r0, n_keys), :], et_ref[pl.ds(r0, n_keys), :]], axis=1)
        v = vsel_ref[0, pl.ds(r0, n_keys), :]
        if causal:
            ok = r0 + lax.broadcasted_iota(jnp.int32, (n_keys, rows), 0) <= qrow(n_keys)
        ss = [_dot_nt(rhs, lhs_ref[kv]) for kv in range(N_KV_HEADS)]
        if causal:
            ss = [jnp.where(ok, s, NEG) for s in ss]
        _flash_update(ss, v, m_ref, acc_ref)
        return carry

    n_full = lax.shift_right_logical(t0 + 1, int(math.log2(SEL_CHUNK)))
    done = 0
    for width in SEL_WIDTHS:
        n_steps = (n_full - done) // width
        lax.fori_loop(done // width, done // width + n_steps,
                      functools.partial(sel_step, causal=False, width=width), 0)
        done = done + n_steps * width
    lax.fori_loop(n_full, n_sel, functools.partial(sel_step, causal=True, width=1), 0)
    o_sel = finish()

    init()

    def win_step(c, carry, width):
        n_keys = width * WIN_CHUNK
        r0 = pl.multiple_of(c * WIN_CHUNK, WIN_CHUNK)
        k = kwin_ref[0, pl.ds(r0, n_keys), :]
        v = vwin_ref[0, pl.ds(r0, n_keys), :]
        wpos = win_pos0 + r0 + lax.broadcasted_iota(jnp.int32, (n_keys, rows), 0)
        qr = qrow(n_keys)
        ok = (wpos <= qr) & (wpos > qr - WINDOW)
        ss = [jnp.where(ok, _dot_nt(k, lhs_ref[kv, :, :LANES]), NEG) for kv in range(N_KV_HEADS)]
        _flash_update(ss, v, m_ref, acc_ref)
        return carry

    n_span = (w_hi - w_lo) // WIN_SPAN
    lax.fori_loop(w_lo, w_lo + n_span, functools.partial(win_step, width=WIN_SPAN), 0)
    lax.fori_loop(w_lo + n_span * WIN_SPAN, w_hi, functools.partial(win_step, width=1), 0)
    o_win = finish()

    gates = jax.nn.sigmoid(misc_ref[0])
    ghi = gates.astype(BF16)
    glo = (gates - ghi.astype(F32)).astype(BF16)
    branches = (ocmp_ref[0], o_sel, o_win)
    out = jnp.zeros(branches[0].shape, F32)
    for br in range(3):
        out = out + (_dot(ghi, eg_ref[br]) + _dot(glo, eg_ref[br])) * branches[br]
    o_ref[0] = out


def sel_win_attention(qp, mneg, kvb, sel_col, winb, o_cmp, misc, q_off, win_pos0, tq):
    b, t, _ = qp.shape
    s = kvb.shape[1]
    sw = winb.shape[1]
    et = _block_onehot(s)
    eg = _gate_expand()
    rows = GQA_GROUP * tq
    assert q_off + t <= s and q_off + t - win_pos0 <= sw and sw % WIN_CHUNK == 0
    return pl.pallas_call(
        functools.partial(_sel_win_kernel, q_off=q_off, win_pos0=win_pos0),
        grid=(b, t // tq),
        in_specs=[
            pl.BlockSpec((1, tq, N_Q_HEADS * LANES), lambda i, j: (i, j, 0)),
            pl.BlockSpec((1, N_KV_HEADS, tq, N_SEL_LANES), lambda i, j: (i, 0, j, 0)),
            pl.BlockSpec((1, s, KV_WIDTH), lambda i, j: (i, 0, sel_col)),
            pl.BlockSpec((1, s, KV_WIDTH), lambda i, j: (i, 0, sel_col + 1)),
            pl.BlockSpec((s, N_SEL_LANES), lambda i, j: (0, 0)),
            pl.BlockSpec((1, sw, KV_WIDTH), lambda i, j: (i, 0, 0)),
            pl.BlockSpec((1, sw, KV_WIDTH), lambda i, j: (i, 0, 1)),
            pl.BlockSpec((1, tq, ATTN_WIDTH), lambda i, j: (i, j, 0)),
            pl.BlockSpec((1, tq, LANES), lambda i, j: (i, j, 0)),
            pl.BlockSpec((3, LANES, ATTN_WIDTH), lambda i, j: (0, 0, 0)),
        ],
        out_specs=pl.BlockSpec((1, tq, ATTN_WIDTH), lambda i, j: (i, j, 0)),
        out_shape=jax.ShapeDtypeStruct((b, t, ATTN_WIDTH), F32),
        scratch_shapes=[
            pltpu.VMEM((N_KV_HEADS, rows, 2 * LANES), BF16),
            pltpu.VMEM((N_KV_HEADS, 1, rows), F32),
            pltpu.VMEM((N_KV_HEADS, LANES, rows), F32),
        ],
        compiler_params=_cparams(("arbitrary", "arbitrary")),
        name="sel_win_attention",
    )(qp, mneg, kvb, kvb, et, winb, winb, o_cmp, misc, eg)


CONV_PAD = 8
HEAD_PAIRS = SSM_HEADS // 2


def _split3(x):
    a = x.astype(BF16)
    r = x - a.astype(F32)
    b = r.astype(BF16)
    c = (r - b.astype(F32)).astype(BF16)
    return a, b, c


def _ssd_kernel(xbc_ref, z_ref, misc_ref, conv0_ref, h0_ref, cw_ref, cb_ref, dtb_ref, a_ref, dsk_ref, nw_ref,
                y_ref, hout_ref, cout_ref, xp_ref, h_ref, ms_ref, *, t_valid):
    ch = pl.program_id(1)
    L = SSD_CHUNK
    keep = CONV_WIDTH - 1

    @pl.when(ch == 0)
    def _():
        xp_ref[...] = jnp.zeros(xp_ref.shape, F32)
        xp_ref[CONV_PAD - keep:CONV_PAD, :] = conv0_ref[0]
        h_ref[...] = h0_ref[0]

    xp_ref[CONV_PAD:CONV_PAD + t_valid, :] = xbc_ref[0]
    conv = cb_ref[...]
    for j in range(CONV_WIDTH):
        conv = conv + cw_ref[j:j + 1, :] * xp_ref[CONV_PAD - keep + j:CONV_PAD - keep + j + L, :]
    last = xp_ref[CONV_PAD + t_valid - keep:CONV_PAD + t_valid, :]
    cout_ref[0] = last
    xp_ref[CONV_PAD - keep:CONV_PAD, :] = last
    xc = _silu(conv)

    row = lax.broadcasted_iota(jnp.int32, (L, LANES), 0)
    lane = lax.broadcasted_iota(jnp.int32, (L, LANES), 1)
    if t_valid == L:
        raw = misc_ref[0]
    else:
        ms_ref[...] = jnp.zeros(ms_ref.shape, F32)
        ms_ref[0:t_valid, :] = misc_ref[0]
        raw = ms_ref[...]
    v = raw + dtb_ref[...]
    dt = jnp.maximum(v, 0.0) + jnp.log(1.0 + jnp.exp(-jnp.abs(v)))
    dt = jnp.where((lane < SSM_HEADS) & (row < t_valid), dt, 0.0)
    da = dt * a_ref[...]
    tri = (lax.broadcasted_iota(jnp.int32, (L, L), 1) <= lax.broadcasted_iota(jnp.int32, (L, L), 0))
    trib = tri.astype(BF16)
    acum = sum(_dot(trib, part) for part in _split3(da))
    acum_t = jnp.transpose(acum)
    dt_t = jnp.transpose(dt)
    e_acum = jnp.exp(acum)
    e_last = jnp.exp(acum[L - 1:L, :])
    w_end = jnp.exp(acum[L - 1:L, :] - acum) * dt
    lo = lane < SSM_HEAD_DIM

    ys = []
    for pair in range(HEAD_PAIRS):
        grp = (2 * pair) // (SSM_HEADS // SSM_GROUPS)
        bg = xc[:, SSM_WIDTH + grp * SSM_STATE:SSM_WIDTH + (grp + 1) * SSM_STATE].astype(BF16)
        cg = xc[:, SSM_WIDTH + (SSM_GROUPS + grp) * SSM_STATE:SSM_WIDTH + (SSM_GROUPS + grp + 1) * SSM_STATE].astype(BF16)
        g = _dot_nt(cg, bg)
        xpair = xc[:, pair * LANES:(pair + 1) * LANES]
        y = jnp.zeros((L, LANES), F32)
        for sub in range(2):
            hd = 2 * pair + sub
            seg = acum[:, hd:hd + 1] - acum_t[hd:hd + 1, :]
            m = g * jnp.exp(jnp.where(tri, seg, NEG)) * dt_t[hd:hd + 1, :]
            xm = jnp.where(lo if sub == 0 else ~lo, xpair, 0.0)
            y = y + _dot(m.astype(BF16), xm.astype(BF16))
        col = lambda a: jnp.where(lo, a[:, 2 * pair:2 * pair + 1], a[:, 2 * pair + 1:2 * pair + 2])
        hp = h_ref[pair]
        y = y + _dot_nt(cg, hp.astype(BF16)) * col(e_acum)
        y = y + col(dsk_ref[...]) * xpair
        xw = (xpair * col(w_end)).astype(BF16)
        st = lax.dot_general(xw, bg, (((0,), (0,)), ((), ())), preferred_element_type=F32)
        prow = lax.broadcasted_iota(jnp.int32, (LANES, LANES), 0) < SSM_HEAD_DIM
        dec = jnp.where(prow, e_last[:, 2 * pair:2 * pair + 1], e_last[:, 2 * pair + 1:2 * pair + 2])
        h_ref[pair] = hp * dec + st
        ys.append(y)
    y = jnp.concatenate(ys, axis=1)
    if t_valid != L:
        y = y[:t_valid]
    y = y * _silu(z_ref[0])
    y = y * lax.rsqrt(jnp.mean(y * y, axis=-1, keepdims=True) + EPS) * nw_ref[...]
    y_ref[0] = y

    @pl.when(ch == pl.num_programs(1) - 1)
    def _():
        hout_ref[0] = h_ref[...]


def ssd(xbc, z, misc, conv0, h0, conv_w, conv_b, dt_bias, a_log, d_skip, norm_w):
    b, t, _ = xbc.shape
    L = SSD_CHUNK
    t_valid = L if t % L == 0 else t
    assert t_valid == L or t < L
    n_ch = max(t // L, 1)
    keep = CONV_WIDTH - 1
    pad8 = lambda v: jnp.pad(v.astype(F32), (0, LANES - SSM_HEADS)).reshape(1, LANES)
    dtb = pad8(dt_bias)
    a = pad8(-jnp.exp(a_log.astype(F32)))
    dsk = pad8(d_skip)
    h0p = h0.reshape(b, HEAD_PAIRS, 2 * SSM_HEAD_DIM, SSM_STATE)
    full = lambda arr: pl.BlockSpec(arr.shape, lambda i, c: (0,) * arr.ndim)
    tok = lambda wd: pl.BlockSpec((1, t_valid, wd), lambda i, c: (i, c, 0))
    y, hout, cout = pl.pallas_call(
        functools.partial(_ssd_kernel, t_valid=t_valid),
        grid=(b, n_ch),
        in_specs=[
            tok(CONV_DIM), tok(SSM_WIDTH), tok(LANES),
            pl.BlockSpec((1, keep, CONV_DIM), lambda i, c: (i, 0, 0)),
            pl.BlockSpec((1, HEAD_PAIRS, 2 * SSM_HEAD_DIM, SSM_STATE), lambda i, c: (i, 0, 0, 0)),
            full(conv_w), pl.BlockSpec((1, CONV_DIM), lambda i, c: (0, 0)),
            full(dtb), full(a), full(dsk), pl.BlockSpec((1, SSM_WIDTH), lambda i, c: (0, 0)),
        ],
        out_specs=(
            tok(SSM_WIDTH),
            pl.BlockSpec((1, HEAD_PAIRS, 2 * SSM_HEAD_DIM, SSM_STATE), lambda i, c: (i, 0, 0, 0)),
            pl.BlockSpec((1, keep, CONV_DIM), lambda i, c: (i, 0, 0)),
        ),
        out_shape=(
            jax.ShapeDtypeStruct((b, t, SSM_WIDTH), F32),
            jax.ShapeDtypeStruct((b, HEAD_PAIRS, 2 * SSM_HEAD_DIM, SSM_STATE), F32),
            jax.ShapeDtypeStruct((b, keep, CONV_DIM), F32),
        ),
        scratch_shapes=[
            pltpu.VMEM((CONV_PAD + L, CONV_DIM), F32),
            pltpu.VMEM((HEAD_PAIRS, 2 * SSM_HEAD_DIM, SSM_STATE), F32),
            pltpu.VMEM((L, LANES), F32),
        ],
        compiler_params=_cparams(("arbitrary", "arbitrary")),
        name="ssd",
    )(xbc, z, misc, conv0, h0p, conv_w, conv_b.reshape(1, CONV_DIM), dtb, a, dsk, norm_w.reshape(1, SSM_WIDTH))
    return y, hout.reshape(b, SSM_HEADS, SSM_HEAD_DIM, SSM_STATE), cout


def _split2(x):
    hi = x.astype(BF16)
    return hi, (x - hi.astype(F32)).astype(BF16)


def _merge_kernel(oa_ref, ys_ref, x_ref, g1_ref, sh2_ref, sc2_ref, anw_ref, wo_ref, n2w_ref, wrh_ref, wrl_ref,
                  x1_ref, h2_ref, lg_ref):
    oa = oa_ref[...]
    a = oa * lax.rsqrt(jnp.mean(oa * oa, axis=-1, keepdims=True) + EPS) * anw_ref[...]
    cat = jnp.concatenate([a.astype(BF16), ys_ref[...].astype(BF16)], axis=1)
    x1 = x_ref[...] + _mod(g1_ref) * _dot(cat, wo_ref[...])
    x1_ref[...] = x1
    h2 = x1 * lax.rsqrt(jnp.mean(x1 * x1, axis=-1, keepdims=True) + EPS) * n2w_ref[...]
    h2 = h2 * (1.0 + _mod(sc2_ref)) + _mod(sh2_ref)
    h2_ref[...] = h2.astype(BF16)
    hh, hl = _split2(h2)
    lg_ref[...] = _dot_nt(wrh_ref[...], hh) + _dot_nt(wrh_ref[...], hl) + _dot_nt(wrl_ref[...], hh)


def merge(o_attn, y_ssm, x, mod3, mod_row0, attn_norm_w, wo, norm2_w, w_router, tm):
    b, t, d = x.shape
    n = b * t
    tiles_per_b = t // tm
    wrt = jnp.transpose(w_router)
    wrh, wrl = _split2(wrt)

    def mod_spec(col):
        return _mod_spec(mod3, col, tm, tiles_per_b, mod_row0)

    tok = lambda wd: pl.BlockSpec((tm, wd), lambda i: (i, 0))
    full = lambda a: pl.BlockSpec(a.shape, lambda i: (0,) * a.ndim)
    return pl.pallas_call(
        _merge_kernel,
        grid=(n // tm,),
        in_specs=[tok(ATTN_WIDTH), tok(SSM_WIDTH), tok(d), mod_spec(2), mod_spec(3), mod_spec(4),
                  full(attn_norm_w), full(wo), full(norm2_w), full(wrh), full(wrl)],
        out_specs=(tok(d), tok(d), pl.BlockSpec((N_EXPERTS, tm), lambda i: (0, i))),
        out_shape=(jax.ShapeDtypeStruct((n, d), F32), jax.ShapeDtypeStruct((n, d), BF16),
                   jax.ShapeDtypeStruct((N_EXPERTS, n), F32)),
        compiler_params=_cparams(("arbitrary",)),
        name="merge",
    )(o_attn.reshape(n, ATTN_WIDTH), y_ssm.reshape(n, SSM_WIDTH), x.reshape(n, d), mod3, mod3, mod3,
      attn_norm_w, wo, norm2_w, wrh, wrl)


EXPERTS_PER_GROUP = N_EXPERTS // N_EXPERT_GROUPS


def _first_max(x, ids, axes, n_ids):
    mx = jnp.max(x, axis=axes, keepdims=True)
    return ids == jnp.min(jnp.where(x == mx, ids, n_ids), axis=axes, keepdims=True), mx


def _route_kernel(lg_ref, eb_ref, tri_ref, w_ref, pos_ref, cnt_ref):
    lg = lg_ref[...]
    tn = lg.shape[2]
    scores = jax.nn.sigmoid(lg)
    biased = scores + eb_ref[...]
    sub = lax.broadcasted_iota(jnp.int32, lg.shape, 1)
    grp = lax.broadcasted_iota(jnp.int32, (N_EXPERT_GROUPS, 1, tn), 0)
    eid = lax.broadcasted_iota(jnp.int32, lg.shape, 0) * EXPERTS_PER_GROUP + sub
    hit, m1 = _first_max(biased, sub, 1, EXPERTS_PER_GROUP)
    m2 = jnp.max(jnp.where(hit, -jnp.inf, biased), axis=1, keepdims=True)
    gs = m1 + m2
    keep = jnp.zeros(gs.shape, jnp.bool_)
    for _ in range(TOPK_GROUPS):
        hit, _m = _first_max(gs, grp, 0, N_EXPERT_GROUPS)
        keep = keep | hit
        gs = jnp.where(hit, -jnp.inf, gs)
    x = jnp.where(keep, biased, NEG)
    sel = jnp.zeros(lg.shape, jnp.bool_)
    for _ in range(TOP_K):
        hit, _m = _first_max(x, eid, (0, 1), N_EXPERTS)
        sel = sel | hit
        x = jnp.where(hit, -jnp.inf, x)
    w = jnp.where(sel, scores, 0.0)
    w = w / jnp.sum(w, axis=(0, 1), keepdims=True) * ROUTED_SCALE
    w_ref[...] = w
    selb = sel.astype(BF16).reshape(N_EXPERTS, tn)
    pos = _dot(selb, tri_ref[...])
    pos_ref[...] = jnp.where(sel, pos.reshape(lg.shape), -1.0)
    cnt = jnp.sum(sel.astype(F32), axis=2, keepdims=True)
    cnt_ref[0] = jnp.broadcast_to(cnt, cnt_ref.shape[1:]).astype(jnp.int32)


def route(logits_t, e_bias, tn):
    n = logits_t.shape[1]
    lg3 = logits_t.reshape(N_EXPERT_GROUPS, EXPERTS_PER_GROUP, n)
    eb = e_bias.astype(F32).reshape(N_EXPERT_GROUPS, EXPERTS_PER_GROUP, 1)
    tri = jnp.asarray(np.triu(np.ones((tn, tn), np.float32), 1), BF16)
    blk = pl.BlockSpec((N_EXPERT_GROUPS, EXPERTS_PER_GROUP, tn), lambda i: (0, 0, i))
    w, pos, cnt = pl.pallas_call(
        _route_kernel,
        grid=(n // tn,),
        in_specs=[blk, pl.BlockSpec(eb.shape, lambda i: (0, 0, 0)), pl.BlockSpec((tn, tn), lambda i: (0, 0))],
        out_specs=(blk, blk, pl.BlockSpec((1, N_EXPERT_GROUPS, EXPERTS_PER_GROUP, LANES), lambda i: (i, 0, 0, 0))),
        out_shape=(jax.ShapeDtypeStruct(lg3.shape, F32), jax.ShapeDtypeStruct(lg3.shape, F32),
                   jax.ShapeDtypeStruct((n // tn, N_EXPERT_GROUPS, EXPERTS_PER_GROUP, LANES), jnp.int32)),
        compiler_params=_cparams(("arbitrary",)),
        name="route",
    )(lg3, eb, tri)
    return w.reshape(N_EXPERTS, n), pos.reshape(N_EXPERTS, n), cnt[..., 0].reshape(n // tn, N_EXPERTS)


MOE_ROWS = 128


def _swiglu(xb, wgu, wd, width):
    gu = _dot(xb, wgu)
    act = _silu(gu[:, :width]) * gu[:, width:]
    return _dot(act.astype(BF16), wd)


MOE_EXPERTS_PER_STEP = 4


MOE_ALIGN = 16
MOE_GATHER_ROWS = 896


def _moe_slots(tm):
    worst = TOP_K * tm + N_EXPERTS * (MOE_ALIGN - 1) + MOE_ROWS
    return -(-worst // MOE_GATHER_ROWS) * MOE_GATHER_ROWS


def _moe_kernel(cnt_ref, start_ref, h2_ref, w_ref, pos_ref, x1_ref, g2_ref, wgu_ref, wd_ref, sgu_ref, sd_ref,
                o_ref, g_all, xs):
    i = pl.program_id(0)
    es = pl.program_id(1)
    tm = h2_ref.shape[0]
    slots = g_all.shape[0]
    slot = lax.broadcasted_iota(jnp.int32, (MOE_ROWS, tm), 0).astype(F32)
    row = lax.broadcasted_iota(jnp.int32, (MOE_ROWS, 1), 0)

    def n_windows(cnt):
        return (cnt + MOE_ROWS - 1) // MOE_ROWS

    def window_start(e, j):
        return pl.multiple_of(start_ref[i * N_EXPERTS + e] + j * MOE_ROWS, MOE_ALIGN)

    @pl.when(es == 0)
    def _():
        g_all[...] = jnp.zeros(g_all.shape, BF16)

        def mark(e, carry):
            pos = pos_ref[pl.ds(e, 1), :]

            def mark_window(j, carry):
                hit = pos == slot + (j * MOE_ROWS).astype(F32)
                g_all[pl.ds(window_start(e, j), MOE_ROWS), :] = hit.astype(BF16)
                return carry

            return lax.fori_loop(0, n_windows(cnt_ref[i * N_EXPERTS + e]), mark_window, carry)

        lax.fori_loop(0, N_EXPERTS, mark, 0)

        def gather(c, carry):
            r0 = pl.multiple_of(c * MOE_GATHER_ROWS, MOE_GATHER_ROWS)
            rows = _dot(g_all[pl.ds(r0, MOE_GATHER_ROWS), :], h2_ref[...])
            xs[pl.ds(r0, MOE_GATHER_ROWS), :] = rows.astype(BF16)
            return carry

        lax.fori_loop(0, slots // MOE_GATHER_ROWS, gather, 0)

    for q in range(MOE_EXPERTS_PER_STEP):
        e = es * MOE_EXPERTS_PER_STEP + q
        cnt = cnt_ref[i * N_EXPERTS + e]
        wrow = w_ref[pl.ds(e, 1), :]

        def window(j, carry, q=q, e=e, cnt=cnt, wrow=wrow):
            r0 = window_start(e, j)
            xg = xs[pl.ds(r0, MOE_ROWS), :]
            out = _swiglu(xg, wgu_ref[q].astype(BF16), wd_ref[q].astype(BF16), D_EXPERT)
            g = g_all[pl.ds(r0, MOE_ROWS), :].astype(F32)
            out = out * jnp.sum(g * wrow, axis=1, keepdims=True)
            mine = row < cnt - j * MOE_ROWS
            xs[pl.ds(r0, MOE_ROWS), :] = jnp.where(mine, out.astype(BF16), xg)
            return carry

        lax.fori_loop(0, n_windows(cnt), window, 0)

    @pl.when(es == pl.num_programs(1) - 1)
    def _():
        y = lax.dot_general(g_all[...], xs[...], (((0,), (0,)), ((), ())), preferred_element_type=F32)
        y = y + _swiglu(h2_ref[...], sgu_ref[...], sd_ref[...], D_SHARED)
        o_ref[...] = x1_ref[...] + _mod(g2_ref) * y


def moe(h2, w_t, pos_t, counts, x1, mod3, mod_row0, t_per_b, wgu, wd, sgu, sd, tm):
    n, d = h2.shape
    tiles_per_b = t_per_b // tm
    eps = MOE_EXPERTS_PER_STEP
    slots = _moe_slots(tm)
    padded = (counts + MOE_ALIGN - 1) // MOE_ALIGN * MOE_ALIGN
    starts = jnp.cumsum(padded, axis=1) - padded
    grid_spec = pltpu.PrefetchScalarGridSpec(
        num_scalar_prefetch=2,
        grid=(n // tm, N_EXPERTS // eps),
        in_specs=[
            pl.BlockSpec((tm, d), lambda i, e, *_: (i, 0)),
            pl.BlockSpec((N_EXPERTS, tm), lambda i, e, *_: (0, i)),
            pl.BlockSpec((N_EXPERTS, tm), lambda i, e, *_: (0, i)),
            pl.BlockSpec((tm, d), lambda i, e, *_: (i, 0)),
            _mod_spec(mod3, 5, tm, tiles_per_b, mod_row0),
            pl.BlockSpec((eps, d, 2 * D_EXPERT), lambda i, e, *_: (e, 0, 0)),
            pl.BlockSpec((eps, D_EXPERT, d), lambda i, e, *_: (e, 0, 0)),
            pl.BlockSpec(sgu.shape, lambda i, e, *_: (0, 0)),
            pl.BlockSpec(sd.shape, lambda i, e, *_: (0, 0)),
        ],
        out_specs=pl.BlockSpec((tm, d), lambda i, e, *_: (i, 0)),
        scratch_shapes=[pltpu.VMEM((slots, tm), BF16), pltpu.VMEM((slots, d), BF16)],
    )
    return pl.pallas_call(
        _moe_kernel,
        grid_spec=grid_spec,
        out_shape=jax.ShapeDtypeStruct((n, d), F32),
        compiler_params=_cparams(("arbitrary", "arbitrary")),
        name="moe",
    )(counts.reshape(-1), starts.reshape(-1).astype(jnp.int32), h2, w_t, pos_t, x1, mod3, wgu, wd, sgu, sd)


SC_WINDOW = 128
PACK_W = 256
MOE_BLOCK_ROWS = 512
HI_MASK = -65536


def _pack_pair(x):
    bits = pltpu.bitcast(x.astype(BF16).astype(F32), jnp.int32)
    return lax.shift_right_logical(bits[:, :PACK_W], 16) | (bits[:, PACK_W:] & HI_MASK)


def _unpack_pair(word):
    lo = pltpu.bitcast(lax.shift_left(word, 16), F32)
    hi = pltpu.bitcast(word & HI_MASK, F32)
    return jnp.concatenate([lo, hi], axis=1)


def _pack_kernel(x_ref, a_ref, b_ref):
    x = x_ref[...]
    a_ref[...] = _pack_pair(x[:, :2 * PACK_W])
    b_ref[...] = _pack_pair(x[:, 2 * PACK_W:])


def pack_rows(x, tm):
    n, d = x.shape
    tok = lambda wd: pl.BlockSpec((tm, wd), lambda i: (i, 0))
    return pl.pallas_call(
        _pack_kernel, grid=(n // tm,), in_specs=[tok(d)], out_specs=(tok(PACK_W), tok(PACK_W)),
        out_shape=(jax.ShapeDtypeStruct((n, PACK_W), jnp.int32),) * 2,
        compiler_params=_cparams(("arbitrary",)), name="pack_rows",
    )(x)


def _slots_kernel(w_ref, pos_ref, base_ref, tri_ref, slot_ref, wt_ref):
    w = w_ref[...]
    pos = pos_ref[...]
    sel = pos >= 0.0
    rank = _dot(tri_ref[...], sel.astype(BF16))
    dest = base_ref[0] + pos
    slots, wts = [], []
    for j in range(TOP_K):
        mine = sel & (rank == float(j))
        slots.append(jnp.sum(jnp.where(mine, dest, 0.0), axis=0, keepdims=True))
        wts.append(jnp.sum(jnp.where(mine, w, 0.0), axis=0, keepdims=True))
    slot_ref[...] = jnp.concatenate(slots, axis=0).astype(jnp.int32)
    wpad = jnp.concatenate(wts + [jnp.zeros((LANES - TOP_K, w.shape[1]), F32)], axis=0)
    wt_ref[...] = jnp.transpose(wpad)


def slots_of(w_t, pos_t, base, tn):
    n = w_t.shape[1]
    tri = jnp.asarray(np.tril(np.ones((N_EXPERTS, N_EXPERTS), np.float32), -1), BF16)
    blk = pl.BlockSpec((N_EXPERTS, tn), lambda i: (0, i))
    return pl.pallas_call(
        _slots_kernel, grid=(n // tn,),
        in_specs=[blk, blk, pl.BlockSpec((1, N_EXPERTS, 1), lambda i: (i, 0, 0)),
                  pl.BlockSpec((N_EXPERTS, N_EXPERTS), lambda i: (0, 0))],
        out_specs=(pl.BlockSpec((TOP_K, tn), lambda i: (0, i)), pl.BlockSpec((tn, LANES), lambda i: (i, 0))),
        out_shape=(jax.ShapeDtypeStruct((TOP_K, n), jnp.int32), jax.ShapeDtypeStruct((n, LANES), F32)),
        compiler_params=_cparams(("arbitrary",)), name="moe_slots",
    )(w_t, pos_t, base, tri)


def sc_scatter_rows(rows, idx, n_out):
    n, d = rows.shape
    m = idx.shape[0]
    nb = n // SC_WINDOW
    mesh = plsc.VectorSubcoreMesh(core_axis_name="core", subcore_axis_name="subcore")

    @functools.partial(pl.kernel, out_type=jax.ShapeDtypeStruct((n_out, d), rows.dtype), mesh=mesh)
    def scatter(x_hbm, i_hbm, o_hbm):
        def body(x_vmem, i_vmem):
            pltpu.sync_copy(x_vmem, o_hbm.at[i_vmem.at[0]])

        pltpu.emit_pipeline(
            body, grid=(m // SC_WINDOW,),
            in_specs=[pl.BlockSpec((SC_WINDOW, d), index_map=lambda i: (i % nb, 0)),
                      pl.BlockSpec((1, SC_WINDOW), index_map=lambda i: (0, i))],
            out_specs=[], core_axis_name=("core", "subcore"), dimension_semantics=(pltpu.PARALLEL,),
        )(x_hbm, i_hbm)

    return scatter(rows, idx.reshape(1, m))


def sc_gather_rows(table, idx):
    d = table.shape[1]
    m = idx.shape[0]
    mesh = plsc.VectorSubcoreMesh(core_axis_name="core", subcore_axis_name="subcore")

    @functools.partial(pl.kernel, out_type=jax.ShapeDtypeStruct((m, d), table.dtype), mesh=mesh)
    def gather(x_hbm, i_hbm, o_hbm):
        def body(i_vmem, o_vmem):
            pltpu.sync_copy(x_hbm.at[i_vmem.at[0]], o_vmem)

        pltpu.emit_pipeline(
            body, grid=(m // SC_WINDOW,),
            in_specs=[pl.BlockSpec((1, SC_WINDOW), index_map=lambda i: (0, i))],
            out_specs=[pl.BlockSpec((SC_WINDOW, d), index_map=lambda i: (i, 0))],
            core_axis_name=("core", "subcore"), dimension_semantics=(pltpu.PARALLEL,),
        )(i_hbm, o_hbm)

    return gather(table, idx.reshape(1, m))


def _experts_kernel(be_ref, nu_ref, xa_ref, xb_ref, wgu_ref, wd_ref, oa_ref, ob_ref, wgu_bf, wd_bf):
    b = pl.program_id(0)

    @pl.when(b < nu_ref[0])
    def _():
        @pl.when((b == 0) | (be_ref[b] != be_ref[jnp.maximum(b - 1, 0)]))
        def _():
            wgu_bf[...] = wgu_ref[0].astype(BF16)
            wd_bf[...] = wd_ref[0].astype(BF16)

        x = jnp.concatenate([_unpack_pair(xa_ref[...]), _unpack_pair(xb_ref[...])], axis=1).astype(BF16)
        out = _swiglu(x, wgu_bf[...], wd_bf[...], D_EXPERT)
        oa_ref[...] = _pack_pair(out[:, :2 * PACK_W])
        ob_ref[...] = _pack_pair(out[:, 2 * PACK_W:])


def experts_sorted(xa, xb, block_expert, n_used, wgu, wd):
    r = xa.shape[0]
    d = wd.shape[2]
    row = lambda b, be, nu: (jnp.minimum(b, nu[0] - 1), 0)
    blk = pl.BlockSpec((MOE_BLOCK_ROWS, PACK_W), row)
    grid_spec = pltpu.PrefetchScalarGridSpec(
        num_scalar_prefetch=2, grid=(r // MOE_BLOCK_ROWS,),
        in_specs=[blk, blk,
                  pl.BlockSpec((1, d, 2 * D_EXPERT), lambda b, be, nu: (be[b], 0, 0)),
                  pl.BlockSpec((1, D_EXPERT, d), lambda b, be, nu: (be[b], 0, 0))],
        out_specs=(blk, blk),
        scratch_shapes=[pltpu.VMEM((d, 2 * D_EXPERT), BF16), pltpu.VMEM((D_EXPERT, d), BF16)],
    )
    return pl.pallas_call(
        _experts_kernel, grid_spec=grid_spec,
        out_shape=(jax.ShapeDtypeStruct((r, PACK_W), jnp.int32),) * 2,
        compiler_params=_cparams(("arbitrary",)), name="moe_experts",
    )(block_expert, n_used, xa, xb, wgu, wd)


def _combine_kernel(ya_ref, yb_ref, wt_ref, h2_ref, x1_ref, g2_ref, sgu_ref, sd_ref, o_ref):
    wt = wt_ref[...]
    acc = _swiglu(h2_ref[...], sgu_ref[...], sd_ref[...], D_SHARED)
    for j in range(TOP_K):
        y = jnp.concatenate([_unpack_pair(ya_ref[j]), _unpack_pair(yb_ref[j])], axis=1)
        acc = acc + wt[:, j:j + 1] * y
    o_ref[...] = x1_ref[...] + _mod(g2_ref) * acc


def combine_sorted(ya, yb, wt, h2, x1, mod3, mod_row0, t_per_b, sgu, sd, tm):
    n, d = h2.shape
    tiles_per_b = t_per_b // tm
    tok = lambda wd: pl.BlockSpec((tm, wd), lambda i: (i, 0))
    yblk = pl.BlockSpec((TOP_K, tm, PACK_W), lambda i: (0, i, 0))
    full = lambda a: pl.BlockSpec(a.shape, lambda i: (0,) * a.ndim)
    return pl.pallas_call(
        _combine_kernel, grid=(n // tm,),
        in_specs=[yblk, yblk, tok(LANES), tok(d), tok(d), _mod_spec(mod3, 5, tm, tiles_per_b, mod_row0),
                  full(sgu), full(sd)],
        out_specs=tok(d), out_shape=jax.ShapeDtypeStruct((n, d), F32),
        compiler_params=_cparams(("arbitrary",)), name="moe_combine",
    )(ya, yb, wt, h2, x1, mod3, sgu, sd)


def moe_sorted(h2, w_t, pos_t, counts, x1, mod3, mod_row0, t_per_b, wgu, wd, sgu, sd, tm, overlap):
    n, d = h2.shape
    assert d == 4 * PACK_W and n % SC_WINDOW == 0
    n_blocks = (TOP_K * n + N_EXPERTS * (MOE_BLOCK_ROWS - 1)) // MOE_BLOCK_ROWS
    total = jnp.sum(counts, axis=0)
    region = (total + MOE_BLOCK_ROWS - 1) // MOE_BLOCK_ROWS * MOE_BLOCK_ROWS
    region_end = jnp.cumsum(region)
    base = (region_end - region)[None, :] + jnp.cumsum(counts, axis=0) - counts
    block_row0 = jnp.arange(n_blocks, dtype=region_end.dtype) * MOE_BLOCK_ROWS
    block_expert = jnp.sum(region_end[None, :] <= block_row0[:, None], axis=1)
    block_expert = jnp.minimum(block_expert, N_EXPERTS - 1).astype(jnp.int32)
    n_used = (region_end[-1:] // MOE_BLOCK_ROWS).astype(jnp.int32)
    slot, wt = slots_of(w_t, pos_t, base.astype(F32).reshape(-1, N_EXPERTS, 1), tm)
    dest = slot.reshape(-1)
    ha, hb = pack_rows(h2, tm)
    rows = n_blocks * MOE_BLOCK_ROWS
    xa, xb = sc_scatter_rows(ha, dest, rows), sc_scatter_rows(hb, dest, rows)
    n_used, rest = lax.optimization_barrier((n_used, overlap()))
    oa, ob = experts_sorted(xa, xb, block_expert, n_used, wgu, wd)
    ya = sc_gather_rows(oa, dest).reshape(TOP_K, n, PACK_W)
    yb = sc_gather_rows(ob, dest).reshape(TOP_K, n, PACK_W)
    return combine_sorted(ya, yb, wt, h2, x1, mod3, mod_row0, t_per_b, sgu, sd, tm), rest


GATHER_PAGES = 8


def _gather_kernel(pt_ref, *refs):
    pages, new_ref = refs[:GATHER_PAGES], refs[GATHER_PAGES]
    rows_ref, cmpx_ref, stage_ref = refs[GATHER_PAGES + 1:]
    step = pl.program_id(1)
    last = pl.num_programs(1) - 1
    n_rows = GATHER_PAGES * PAGE_SIZE

    @pl.when(step < last)
    def _():
        for k in range(GATHER_PAGES):
            sl = slice(k * PAGE_SIZE, (k + 1) * PAGE_SIZE)
            for r in range(4):
                tile = jnp.transpose(pages[k][0, r])
                if r < 2:
                    stage_ref[r, sl, :] = tile
                else:
                    rows_ref[0, sl, (r - 2) * KV_WIDTH:(r - 1) * KV_WIDTH] = tile.astype(BF16)

    @pl.when(step == last)
    def _():
        new = new_ref[0]
        tn = new.shape[0]
        stage_ref[...] = jnp.zeros(stage_ref.shape, F32)
        for s in range(2):
            stage_ref[s, 0:tn, :] = new[:, s * KV_WIDTH:(s + 1) * KV_WIDTH]
        pad = jnp.zeros((n_rows - tn, 2 * KV_WIDTH), F32)
        rows_ref[0] = jnp.concatenate([new[:, 2 * KV_WIDTH:], pad], axis=0).astype(BF16)

    _stride_block_store(stage_ref, cmpx_ref, n_rows)


def gather_pages(cache_t, page_table, new_rows):
    b, n_pages = page_table.shape
    steps = n_pages // GATHER_PAGES
    rows = GATHER_PAGES * PAGE_SIZE
    s_out = (steps + 1) * rows

    def page_spec(k):
        def idx(i, s, pt):
            p = jnp.minimum(s, steps - 1) * GATHER_PAGES + k
            return (pt[i * n_pages + p], 0, 0, 0)
        return pl.BlockSpec((1, 4, KV_WIDTH, PAGE_SIZE), idx)

    grid_spec = pltpu.PrefetchScalarGridSpec(
        num_scalar_prefetch=1,
        grid=(b, steps + 1),
        in_specs=[page_spec(k) for k in range(GATHER_PAGES)]
        + [pl.BlockSpec((1,) + new_rows.shape[1:], lambda i, s, pt: (i, 0, 0))],
        out_specs=(
            pl.BlockSpec((1, rows, 2 * KV_WIDTH), lambda i, s, pt: (i, s, 0)),
            pl.BlockSpec((1, rows // CMP_STRIDE, CMP_STRIDE * 2 * KV_WIDTH), lambda i, s, pt: (i, s, 0)),
        ),
        scratch_shapes=[pltpu.VMEM((2, rows, KV_WIDTH), F32)],
    )
    return pl.pallas_call(
        _gather_kernel,
        grid_spec=grid_spec,
        out_shape=(jax.ShapeDtypeStruct((b, s_out, 2 * KV_WIDTH), BF16),
                   jax.ShapeDtypeStruct((b, s_out // CMP_STRIDE, CMP_STRIDE * 2 * KV_WIDTH), BF16)),
        compiler_params=_cparams(("arbitrary", "arbitrary")),
        name="gather_pages",
    )(page_table.reshape(-1), *([cache_t] * GATHER_PAGES), new_rows)


def _attention(qp, cmpx, kvb, sel_col, winb, misc, cmp_w, q_off, win_pos0, tq):
    t = qp.shape[1]
    cur_lo, cur_hi = q_off // SEL_BLOCK, (q_off + t - 1) // SEL_BLOCK
    assert cur_hi < N_SEL_LANES or (cur_lo == cur_hi == N_SEL_LANES), (q_off, t)
    n_pick = N_SEL - (1 if cur_hi >= N_SEL_LANES else 0)
    kcv = compress(cmpx, *cmp_w)
    o_cmp, mneg = cmp_select(qp, kcv, q_off, n_pick, tq)
    return sel_win_attention(qp, mneg, kvb, sel_col, winb, o_cmp, misc, q_off, win_pos0, tq)


def kernel(x_prompt, x_sample, cache_kv, cache_win, state_ssm, state_conv, page_table, c_prompt, c_sample, w_ada, b_ada, norm1_w, norm2_w, w_in, q_norm_w, k_norm_w, cmp_pe, cmp_w1, cmp_w2, attn_out_norm_w, conv_w, conv_b, dt_bias, a_log, d_skip, ssm_norm_w, w_out, w_router, e_bias, w_exp_gu, w_exp_down, w_sh_gu, w_sh_down):
    xp, xq = x_prompt, x_sample
    bp, tp, d = xp.shape
    bq, tq, _ = xq.shape
    depth = w_ada.shape[0]
    past_len = page_table.shape[1] * PAGE_SIZE
    nq = bq * tq
    tq_pad = LANES // GQA_GROUP
    assert tp % TOKEN_TILE == 0 and tp >= WINDOW and nq % 8 == 0 and tq <= tq_pad
    pos_p = jnp.arange(tp, dtype=jnp.int32)
    pos_q = jnp.tile(past_len + jnp.arange(tq, dtype=jnp.int32), bq)
    c_all = jnp.concatenate([c_prompt, c_sample], axis=0)
    c_all = jnp.pad(c_all, ((0, -c_all.shape[0] % 8), (0, 0)))
    outs = [[] for _ in range(8)]
    for l in range(depth):
        mod = adaln_all(c_all, w_ada[l], b_ada[l])
        mod_p = mod.reshape(mod.shape[0], 1, 6 * d)
        mod_q = jnp.repeat(mod[bp:bp + bq], tq, axis=0)
        wp = _prep_w_in(w_in[l])
        cmp_w = _prep_compress(cmp_pe[l], cmp_w1[l], cmp_w2[l])
        wo = w_out[l].astype(BF16)
        wgu, wd = w_exp_gu[l], w_exp_down[l]
        sgu, sd = w_sh_gu[l].astype(BF16), w_sh_down[l].astype(BF16)
        ssm_w = (conv_w[l], conv_b[l], dt_bias[l], a_log[l], d_skip[l], ssm_norm_w[l])
        n1w, n2w, anw = norm1_w[l:l + 1], norm2_w[l:l + 1], attn_out_norm_w[l:l + 1]

        qp, kvb, win, winb, z, xbc, misc, kvt, cmpx = inproj(xp, mod_p, 0, n1w, wp, q_norm_w[l], k_norm_w[l], pos_p,
                                                            TOKEN_TILE, True)
        r3 = lambda a: a.reshape(bp, tp, a.shape[-1])
        o_attn = _attention(r3(qp), cmpx, r3(kvb), 2, r3(winb), r3(misc), cmp_w, 0, 0, QUERY_TILE)
        y_ssm, h_new, conv_new = ssd(r3(xbc), r3(z), r3(misc), jnp.zeros((bp, CONV_WIDTH - 1, CONV_DIM), F32),
                                     jnp.zeros((bp, SSM_HEADS, SSM_HEAD_DIM, SSM_STATE), F32), *ssm_w)
        x1, h2, lg = merge(o_attn, y_ssm, xp, mod_p, 0, anw, wo, n2w, w_router[l], TOKEN_TILE)
        w_t, pos_t, cnt = route(lg, e_bias[l], TOKEN_TILE)
        outs[0].append(jnp.transpose(kvt.reshape(bp, 4, N_KV_HEADS, HEAD_DIM, tp), (0, 4, 1, 2, 3)))
        outs[1].append(win.reshape(bp, tp, 2, N_KV_HEADS, HEAD_DIM)[:, tp - WINDOW:])
        outs[2].append(h_new)
        outs[3].append(conv_new)

        xq1 = xq.reshape(1, nq, d)
        rq = lambda a: a.reshape(bq, tq, a.shape[-1])
        padq = lambda a: jnp.pad(rq(a), ((0, 0), (0, tq_pad - tq), (0, 0)))

        def sample_front():
            proj = inproj(xq1, mod_q, 0, n1w, wp, q_norm_w[l], k_norm_w[l], pos_q, nq, False)
            cache_t = jnp.transpose(cache_kv[l], (0, 2, 3, 4, 1)).reshape(cache_kv.shape[1], 4, KV_WIDTH, PAGE_SIZE)
            past, cmpx = gather_pages(cache_t, page_table, rq(proj[-1]))
            return proj, past, cmpx

        xp, (proj, past, cmpx) = moe_sorted(h2, w_t, pos_t, cnt, x1, mod_p, 0, tp, wgu, wd, sgu, sd, TOKEN_TILE,
                                            sample_front)
        xp = xp.reshape(bp, tp, d)
        qp, kvb, win, winb, z, xbc, misc, kv = proj
        win_all = jnp.concatenate([cache_win[l].reshape(bq, WINDOW, 2 * KV_WIDTH).astype(BF16), rq(winb),
                                   jnp.zeros((bq, -(WINDOW + tq_pad) % WIN_CHUNK + tq_pad - tq, 2 * KV_WIDTH), BF16)],
                                  axis=1)
        o_attn = _attention(padq(qp), cmpx, past, 0, win_all, padq(misc), cmp_w, past_len, past_len - WINDOW,
                            tq_pad)[:, :tq]
        y_ssm, h_new, conv_new = ssd(rq(xbc), rq(z), rq(misc), state_conv[l], state_ssm[l], *ssm_w)
        x1, h2, lg = merge(o_attn.reshape(1, nq, ATTN_WIDTH), y_ssm.reshape(1, nq, SSM_WIDTH), xq1, mod_q, 0,
                           anw, wo, n2w, w_router[l], nq)
        w_t, pos_t, cnt = route(lg, e_bias[l], nq)
        xq = moe(h2, w_t, pos_t, cnt, x1, mod_q, 0, nq, wgu, wd, sgu, sd, nq).reshape(bq, tq, d)
        win_rows = win.reshape(bq, tq, 2, N_KV_HEADS, HEAD_DIM)
        outs[4].append(kv.reshape(bq, tq, 4, N_KV_HEADS, HEAD_DIM))
        outs[5].append(jnp.concatenate([cache_win[l], win_rows.astype(cache_win.dtype)], axis=1)[:, tq:])
        outs[6].append(h_new)
        outs[7].append(conv_new)
    return (xp, xq) + tuple(jnp.stack(o) for o in outs)
```

```python
import functools
import math

import jax
import jax.numpy as jnp
import numpy as np
from jax import lax
from jax.experimental import pallas as pl
from jax.experimental.pallas import tpu as pltpu
from jax.experimental.pallas import tpu_sc as plsc

D_MODEL = 1024
PAGE_SIZE = 128
HEAD_DIM = 64
N_Q_HEADS = 8
N_KV_HEADS = 2
GQA_GROUP = N_Q_HEADS // N_KV_HEADS
ATTN_WIDTH = N_Q_HEADS * HEAD_DIM
KV_WIDTH = N_KV_HEADS * HEAD_DIM
ROPE_DIM = HEAD_DIM // 4
ROPE_THETA = 500000.0
CMP_LEN = 32
CMP_STRIDE = 16
CMP_HIDDEN = 4 * HEAD_DIM
SEL_BLOCK = 64
N_SEL = 16
N_LOCAL = 2
WINDOW = 512
SSM_HEADS = 8
SSM_HEAD_DIM = 64
SSM_WIDTH = SSM_HEADS * SSM_HEAD_DIM
SSM_GROUPS = 2
SSM_STATE = 128
CONV_WIDTH = 4
CONV_DIM = SSM_WIDTH + 2 * SSM_GROUPS * SSM_STATE
SSD_CHUNK = 128
MIX_WIDTH = ATTN_WIDTH + SSM_WIDTH
N_EXPERTS = 64
N_EXPERT_GROUPS = 8
TOPK_GROUPS = 4
TOP_K = 8
D_EXPERT = 256
D_SHARED = 256
ROUTED_SCALE = 2.5
IN_SIZES = (ATTN_WIDTH, 6 * KV_WIDTH, 3 * N_Q_HEADS, SSM_WIDTH, CONV_DIM, SSM_HEADS)
N_IN = sum(IN_SIZES)
EPS = 1e-6
NEG = -1e30
BIG = 1e6

LANES = 128
TOKEN_TILE = 512
QUERY_TILE = 128
VMEM_LIMIT = 56 * 1024 * 1024

BF16 = jnp.bfloat16
F32 = jnp.float32
LOG2E = math.log2(math.e)


def _cparams(sem, flags=None):
    return pltpu.CompilerParams(dimension_semantics=sem, vmem_limit_bytes=VMEM_LIMIT, flags=flags)


def _silu(x):
    return x * jax.nn.sigmoid(x)


def _dot(a, b):
    return jnp.dot(a, b, preferred_element_type=F32)


def _dot_nt(a, b):
    return lax.dot_general(a, b, (((1,), (1,)), ((), ())), preferred_element_type=F32)


def _mod_spec(mod, col, tm, tiles_per_b, row0):
    if mod.ndim == 3:
        return pl.BlockSpec((1, 1, D_MODEL), lambda i, *_: (row0 + i // tiles_per_b, 0, col))
    return pl.BlockSpec((tm, D_MODEL), lambda i, *_: (i, col))


def _mod(ref):
    return ref[0] if len(ref.shape) == 3 else ref[...]


def _adaln_kernel(c_ref, w_ref, b_ref, o_ref):
    c = c_ref[...]
    a = _silu(c).astype(BF16)
    o_ref[...] = _dot(a, w_ref[...].astype(BF16)) + b_ref[...]


def adaln_all(c_all, w_ada, b_ada):
    rows = c_all.shape[0]
    n = w_ada.shape[1]
    tn = 1024
    return pl.pallas_call(
        _adaln_kernel,
        grid=(n // tn,),
        in_specs=[
            pl.BlockSpec((rows, D_MODEL), lambda j: (0, 0)),
            pl.BlockSpec((D_MODEL, tn), lambda j: (0, j)),
            pl.BlockSpec((1, tn), lambda j: (0, j)),
        ],
        out_specs=pl.BlockSpec((rows, tn), lambda j: (0, j)),
        out_shape=jax.ShapeDtypeStruct((rows, n), F32),
        compiler_params=_cparams(("arbitrary",)),
        name="adaln",
    )(c_all, w_ada, b_ada.reshape(1, n))


_C_Q = 0
_C_KV = _C_Q + ATTN_WIDTH
_C_Z = _C_KV + 6 * KV_WIDTH
_C_XBC = _C_Z + SSM_WIDTH
_C_MISC = _C_XBC + CONV_DIM
N_IN_PAD = _C_MISC + LANES
N_GATES = 3 * N_Q_HEADS


def _prep_w_in(w_in):
    s = np.cumsum((0,) + IN_SIZES)
    q, kv, g, z, xbc, dt = (w_in[:, int(s[i]):int(s[i + 1])] for i in range(6))
    pad = jnp.zeros((w_in.shape[0], LANES - N_GATES - SSM_HEADS), w_in.dtype)
    return jnp.concatenate([q, kv, z, xbc, dt, g, pad], axis=1).astype(BF16)


def _group_mean_matrix(width):
    i = np.arange(width)
    m = (i[:, None] // HEAD_DIM == i[None, :] // HEAD_DIM).astype(np.float32) / HEAD_DIM
    return jnp.asarray(m, BF16)


def _rope_tables(pos):
    half = ROPE_DIM // 2
    inv_freq = ROPE_THETA ** (-jnp.arange(half, dtype=F32) / half)
    ang = pos.astype(F32)[:, None] * inv_freq[None, :]
    cos, sin = jnp.cos(ang), jnp.sin(ang)
    t = pos.shape[0]
    one = jnp.ones((t, HEAD_DIM - ROPE_DIM), F32)
    zero = jnp.zeros((t, HEAD_DIM - ROPE_DIM), F32)
    zh = jnp.zeros((t, half), F32)
    c = jnp.concatenate([cos, cos, one], axis=1)
    s_up = jnp.concatenate([-sin, zh, zero], axis=1)
    s_dn = jnp.concatenate([zh, sin, zero], axis=1)
    rep = LANES // HEAD_DIM
    return jnp.tile(c, (1, rep)), jnp.tile(s_up, (1, rep)), jnp.tile(s_dn, (1, rep))


def _rope(x, c, s_up, s_dn):
    w = x.shape[1]
    half = ROPE_DIM // 2
    rep = w // LANES
    ct = jnp.concatenate([c] * rep, axis=1) if rep > 1 else c
    su = jnp.concatenate([s_up] * rep, axis=1) if rep > 1 else s_up
    sd = jnp.concatenate([s_dn] * rep, axis=1) if rep > 1 else s_dn
    up = pltpu.roll(x, w - half, axis=1)
    dn = pltpu.roll(x, half, axis=1)
    return x * ct + up * su + dn * sd


def _stride_block_store(stage_ref, cmpx_ref, n_rows):
    nb = n_rows // CMP_STRIDE
    lane = lax.broadcasted_iota(jnp.int32, (nb, KV_WIDTH), 1)
    lo = lane < HEAD_DIM
    span = CMP_STRIDE * HEAD_DIM
    for s in range(2):
        for m in range(CMP_STRIDE // 2):
            r0 = stage_ref[s, pl.ds(2 * m, nb, stride=CMP_STRIDE), :]
            r1 = stage_ref[s, pl.ds(2 * m + 1, nb, stride=CMP_STRIDE), :]
            head0 = jnp.where(lo, r0, pltpu.roll(r1, HEAD_DIM, axis=1))
            head1 = jnp.where(lo, pltpu.roll(r0, HEAD_DIM, axis=1), r1)
            for h, piece in enumerate((head0, head1)):
                c0 = (2 * s + h) * span + m * KV_WIDTH
                cmpx_ref[0, :, c0:c0 + KV_WIDTH] = piece.astype(BF16)


def _inproj_kernel(x_ref, shift_ref, scale_ref, nw_ref, w_ref, qw_ref, kw_ref, gq_ref, gk_ref,
                   c_ref, su_ref, sd_ref,
                   qp_ref, kvb_ref, win_ref, winb_ref, z_ref, xbc_ref, misc_ref, *rest, seq_layout):
    x = x_ref[...]
    ms = jnp.mean(x * x, axis=-1, keepdims=True)
    h = x * lax.rsqrt(ms + EPS) * nw_ref[...]
    h = h * (1.0 + _mod(scale_ref)) + _mod(shift_ref)
    hb = h.astype(BF16)
    c, su, sd = c_ref[...], su_ref[...], sd_ref[...]

    q = _dot(hb, w_ref[:, _C_Q:_C_Q + ATTN_WIDTH])
    qms = _dot((q * q).astype(BF16), gq_ref[...])
    q = q * lax.rsqrt(qms + EPS) * qw_ref[...]
    q = _rope(q, c, su, sd) * (HEAD_DIM ** -0.5 * LOG2E)
    lane = lax.broadcasted_iota(jnp.int32, q.shape, 1) % LANES
    lo = lane < HEAD_DIM
    q_up = pltpu.roll(q, ATTN_WIDTH - HEAD_DIM, axis=1)
    q_dn = pltpu.roll(q, HEAD_DIM, axis=1)
    zero = jnp.zeros_like(q)
    nat_lo = jnp.where(lo, q, zero)
    nat_hi = jnp.where(lo, zero, q)
    up_lo = jnp.where(lo, q_up, zero)
    dn_hi = jnp.where(lo, zero, q_dn)
    blocks = []
    for hd in range(N_Q_HEADS):
        pair = hd // 2
        sl = slice(pair * LANES, (pair + 1) * LANES)
        if hd < GQA_GROUP:
            blocks.append((nat_lo if hd % 2 == 0 else up_lo)[:, sl])
        else:
            blocks.append((dn_hi if hd % 2 == 0 else nat_hi)[:, sl])
    qp_ref[...] = jnp.concatenate(blocks, axis=1).astype(BF16)

    kv = _dot(hb, w_ref[:, _C_KV:_C_KV + 6 * KV_WIDTH])
    outs = []
    for br in range(3):
        k = kv[:, br * 2 * KV_WIDTH:br * 2 * KV_WIDTH + KV_WIDTH]
        v = kv[:, br * 2 * KV_WIDTH + KV_WIDTH:(br + 1) * 2 * KV_WIDTH]
        kms = _dot((k * k).astype(BF16), gk_ref[...])
        k = k * lax.rsqrt(kms + EPS) * kw_ref[:, br * KV_WIDTH:(br + 1) * KV_WIDTH]
        k = _rope(k, c, su, sd)
        outs += [k, v]
    kvrows = jnp.concatenate(outs[:4], axis=1)
    winrows = jnp.concatenate(outs[4:], axis=1)
    kvb_ref[...] = kvrows.astype(BF16)
    win_ref[...] = winrows
    winb_ref[...] = winrows.astype(BF16)
    if seq_layout:
        kvt_ref, cmpx_ref, stage_ref = rest
        tm = kvrows.shape[0]
        for r in range(4):
            kvt_ref[0, r] = jnp.transpose(kvrows[:, r * KV_WIDTH:(r + 1) * KV_WIDTH])
        for s in range(2):
            stage_ref[s] = kvrows[:, s * KV_WIDTH:(s + 1) * KV_WIDTH]
        _stride_block_store(stage_ref, cmpx_ref, tm)
    else:
        rest[0][...] = kvrows

    z_ref[...] = _dot(hb, w_ref[:, _C_Z:_C_Z + SSM_WIDTH])
    xbc_ref[...] = _dot(hb, w_ref[:, _C_XBC:_C_XBC + CONV_DIM])
    misc_ref[...] = _dot(hb, w_ref[:, _C_MISC:_C_MISC + LANES])


def inproj(x, mod3, mod_row0, norm_w, wp, q_norm_w, k_norm_w, pos, tm, seq_layout):
    b, t, d = x.shape
    n = b * t
    tiles_per_b = t // tm
    xf = x.reshape(n, d)
    c, su, sd = _rope_tables(pos)
    qw = jnp.tile(q_norm_w, N_Q_HEADS).reshape(1, ATTN_WIDTH)
    kw = jnp.concatenate([jnp.tile(k_norm_w[i], N_KV_HEADS) for i in range(3)]).reshape(1, 3 * KV_WIDTH)
    gq = _group_mean_matrix(ATTN_WIDTH)
    gk = _group_mean_matrix(KV_WIDTH)

    def mod_spec(col):
        return _mod_spec(mod3, col, tm, tiles_per_b, mod_row0)

    def tok(wd):
        return pl.BlockSpec((tm, wd), lambda i: (i, 0))

    def full(a):
        return pl.BlockSpec(a.shape, lambda i: (0,) * a.ndim)

    rope_spec = pl.BlockSpec((tm, LANES), lambda i: (i % tiles_per_b, 0))
    out_shape = [
        jax.ShapeDtypeStruct((n, N_Q_HEADS * LANES), BF16),
        jax.ShapeDtypeStruct((n, 4 * KV_WIDTH), BF16),
        jax.ShapeDtypeStruct((n, 2 * KV_WIDTH), F32),
        jax.ShapeDtypeStruct((n, 2 * KV_WIDTH), BF16),
        jax.ShapeDtypeStruct((n, SSM_WIDTH), F32),
        jax.ShapeDtypeStruct((n, CONV_DIM), F32),
        jax.ShapeDtypeStruct((n, LANES), F32),
    ]
    out_specs = [tok(s.shape[1]) for s in out_shape]
    scratch = []
    if seq_layout:
        out_shape += [jax.ShapeDtypeStruct((b, 4, KV_WIDTH, t), F32),
                      jax.ShapeDtypeStruct((b, t // CMP_STRIDE, CMP_STRIDE * 2 * KV_WIDTH), BF16)]
        out_specs += [pl.BlockSpec((1, 4, KV_WIDTH, tm), lambda i: (i // tiles_per_b, 0, 0, i % tiles_per_b)),
                      pl.BlockSpec((1, tm // CMP_STRIDE, CMP_STRIDE * 2 * KV_WIDTH),
                                   lambda i: (i // tiles_per_b, i % tiles_per_b, 0))]
        scratch = [pltpu.VMEM((2, tm, KV_WIDTH), F32)]
    else:
        out_shape += [jax.ShapeDtypeStruct((n, 4 * KV_WIDTH), F32)]
        out_specs += [tok(4 * KV_WIDTH)]
    return pl.pallas_call(
        functools.partial(_inproj_kernel, seq_layout=seq_layout),
        grid=(n // tm,),
        in_specs=[tok(d), mod_spec(0), mod_spec(1), full(norm_w), full(wp), full(qw), full(kw), full(gq), full(gk),
                  rope_spec, rope_spec, rope_spec],
        out_specs=tuple(out_specs),
        out_shape=tuple(out_shape),
        scratch_shapes=scratch,
        compiler_params=_cparams(("arbitrary",)),
        name="inproj",
    )(xf, mod3, mod3, norm_w, wp, qw, kw, gq, gk, c, su, sd)


def _prep_compress(cmp_pe, cmp_w1, cmp_w2):
    span = CMP_STRIDE * HEAD_DIM
    w1p = jnp.concatenate([cmp_w1[:, :span], cmp_w1[:, span:]], axis=2).astype(BF16)
    pep = cmp_pe.reshape(2, 2, span)
    eye = jnp.eye(N_KV_HEADS, dtype=F32)
    w2p = jnp.einsum("poe,hg->phoge", cmp_w2, eye).reshape(2, N_KV_HEADS, CMP_HIDDEN, KV_WIDTH).astype(BF16)
    return w1p, pep, w2p


def _compress_kernel(x_ref, w1_ref, pe_ref, w2_ref, o_ref):
    part = pl.program_id(1)
    nb = x_ref.shape[1]
    span = CMP_STRIDE * HEAD_DIM
    pe = pe_ref[0]
    out = jnp.zeros((nb, KV_WIDTH), F32)
    for h in range(N_KV_HEADS):
        xk = x_ref[0, :, h * span:(h + 1) * span]
        xv = x_ref[0, :, (N_KV_HEADS + h) * span:(N_KV_HEADS + h + 1) * span]
        x = jnp.where(part == 0, xk, xv).astype(F32)
        u = _dot((x + pe[0:1]).astype(BF16), w1_ref[0, :, :CMP_HIDDEN])
        v = _dot((x + pe[1:2]).astype(BF16), w1_ref[0, :, CMP_HIDDEN:])
        h1 = u + pltpu.roll(v, nb - 1, axis=0)
        out = out + _dot(_silu(h1).astype(BF16), w2_ref[0, h])
    row = lax.broadcasted_iota(jnp.int32, out.shape, 0)
    o_ref[0, 0] = jnp.where(row < nb - 1, out, 0.0).astype(o_ref.dtype)


def compress(x, w1p, pep, w2p):
    b, nb, width = x.shape
    return pl.pallas_call(
        _compress_kernel,
        grid=(b, 2),
        in_specs=[
            pl.BlockSpec((1, nb, width), lambda i, p: (i, 0, 0)),
            pl.BlockSpec((1,) + w1p.shape[1:], lambda i, p: (p, 0, 0)),
            pl.BlockSpec((1,) + pep.shape[1:], lambda i, p: (p, 0, 0)),
            pl.BlockSpec((1,) + w2p.shape[1:], lambda i, p: (p, 0, 0, 0)),
        ],
        out_specs=pl.BlockSpec((1, 1, nb, KV_WIDTH), lambda i, p: (i, p, 0, 0)),
        out_shape=jax.ShapeDtypeStruct((b, 2, nb, KV_WIDTH), BF16),
        compiler_params=_cparams(("arbitrary", "arbitrary")),
        name="compress",
    )(x, w1p, pep, w2p)


N_SEL_LANES = LANES


def _cover_matrix(nb):
    c = np.arange(nb)[:, None]
    j = np.arange(N_SEL_LANES)[None, :]
    start = c * CMP_STRIDE
    m = (start < (j + 1) * SEL_BLOCK) & (start + CMP_LEN > j * SEL_BLOCK)
    return jnp.asarray(m.astype(np.float32), BF16)


def _place_heads(res, kv):
    lane = lax.broadcasted_iota(jnp.int32, res[0].shape, 1)
    lo = lane < HEAD_DIM
    blocks = []
    for pair in range(GQA_GROUP // 2):
        a, b = res[2 * pair], res[2 * pair + 1]
        if kv == 0:
            blocks.append(jnp.where(lo, a, pltpu.roll(b, HEAD_DIM, axis=1)))
        else:
            blocks.append(jnp.where(lo, pltpu.roll(a, HEAD_DIM, axis=1), b))
    return jnp.concatenate(blocks, axis=1)


def _group_rows(q_ref, kv):
    heads = range(kv * GQA_GROUP, (kv + 1) * GQA_GROUP)
    return jnp.concatenate([q_ref[0, :, hd * LANES:(hd + 1) * LANES] for hd in heads], axis=0)


def _heads_from_transposed(out_t, tq, kv):
    out = jnp.transpose(out_t)
    return _place_heads([out[g * tq:(g + 1) * tq] for g in range(GQA_GROUP)], kv)


def _cmp_select_kernel(q_ref, kc_ref, vc_ref, covt_ref, o_ref, m_ref, *, q_off, n_pick):
    tq = q_ref.shape[1]
    rows = GQA_GROUP * tq
    nb = kc_ref.shape[2]
    wl = max(tq, LANES)
    assert tq % LANES == 0 or rows == LANES
    t0 = q_off + pl.program_id(1) * tq
    kc = kc_ref[0, 0]
    vc = vc_ref[0, 0]
    qpos = t0 + lax.broadcasted_iota(jnp.int32, (nb, rows), 1) % tq
    cend = lax.broadcasted_iota(jnp.int32, (nb, rows), 0) * CMP_STRIDE + (CMP_LEN - 1)
    valid = cend <= qpos
    blk = lax.broadcasted_iota(jnp.int32, (N_SEL_LANES, wl), 0)
    cur = (t0 + lax.broadcasted_iota(jnp.int32, (N_SEL_LANES, wl), 1) % tq) // SEL_BLOCK
    forced = (blk == 0) | ((blk <= cur) & (blk > cur - N_LOCAL))
    o_groups = []
    for kv in range(N_KV_HEADS):
        s = _dot_nt(kc, _group_rows(q_ref, kv))
        s = jnp.where(valid, s, NEG)
        e = jnp.exp2(s - jnp.max(s, axis=0, keepdims=True))
        p = e / jnp.sum(e, axis=0, keepdims=True)
        p = jnp.where(valid, p, 0.0)
        o_t = lax.dot_general(vc, p.astype(BF16), (((0,), (0,)), ((), ())), preferred_element_type=F32)
        o_groups.append(_heads_from_transposed(o_t, tq, kv))
        if tq % LANES == 0:
            psum = sum(p[:, g * tq:(g + 1) * tq] for g in range(GQA_GROUP))
        else:
            psum = p + sum(pltpu.roll(p, g * tq, axis=1) for g in range(1, GQA_GROUP))
        hi, lo = _split2(psum)
        imp = _dot(covt_ref[...], hi) + _dot(covt_ref[...], lo)
        x = jnp.where(forced, BIG, jnp.where(blk > cur, -BIG, imp))
        sel = jnp.zeros(x.shape, jnp.bool_)
        for _ in range(n_pick):
            mx = jnp.max(x, axis=0, keepdims=True)
            idx = jnp.min(jnp.where(x == mx, blk, N_SEL_LANES), axis=0, keepdims=True)
            hit = blk == idx
            sel = sel | hit
            x = jnp.where(hit, -jnp.inf, x)
        mneg = jnp.transpose(jnp.where(sel, 0.0, NEG))
        m_ref[0, kv] = mneg[:tq].astype(m_ref.dtype)
    o_ref[0] = jnp.concatenate(o_groups, axis=1)


def cmp_select(qp, kcv, q_off, n_pick, tq):
    b, t, _ = qp.shape
    nb = kcv.shape[2]
    cover = jnp.transpose(_cover_matrix(nb))
    return pl.pallas_call(
        functools.partial(_cmp_select_kernel, q_off=q_off, n_pick=n_pick),
        grid=(b, t // tq),
        in_specs=[
            pl.BlockSpec((1, tq, N_Q_HEADS * LANES), lambda i, j: (i, j, 0)),
            pl.BlockSpec((1, 1, nb, KV_WIDTH), lambda i, j: (i, 0, 0, 0)),
            pl.BlockSpec((1, 1, nb, KV_WIDTH), lambda i, j: (i, 1, 0, 0)),
            pl.BlockSpec((N_SEL_LANES, nb), lambda i, j: (0, 0)),
        ],
        out_specs=(
            pl.BlockSpec((1, tq, ATTN_WIDTH), lambda i, j: (i, j, 0)),
            pl.BlockSpec((1, N_KV_HEADS, tq, N_SEL_LANES), lambda i, j: (i, 0, j, 0)),
        ),
        out_shape=(
            jax.ShapeDtypeStruct((b, t, ATTN_WIDTH), F32),
            jax.ShapeDtypeStruct((b, N_KV_HEADS, t, N_SEL_LANES), BF16),
        ),
        compiler_params=_cparams(("arbitrary", "arbitrary")),
        name="cmp_select",
    )(qp, kcv, kcv, cover)


SEL_TILE_ELEMS = 512 * 512
SEL_WIDTHS = (8, 4, 2, 1)
WIN_CHUNK = 256
WIN_SPAN = 3


def _block_onehot(s):
    key = np.arange(s)[:, None]
    j = np.arange(N_SEL_LANES)[None, :]
    return jnp.asarray((key // SEL_BLOCK == j).astype(np.float32), BF16)


def _gate_expand():
    m = np.zeros((3, LANES, ATTN_WIDTH), np.float32)
    for br in range(3):
        for hd in range(N_Q_HEADS):
            m[br, SSM_HEADS + 3 * hd + br, hd * HEAD_DIM:(hd + 1) * HEAD_DIM] = 1.0
    return jnp.asarray(m, BF16)


def _flash_update(ss, v, m_ref, acc_ref):
    lane = lax.broadcasted_iota(jnp.int32, v.shape, 1)
    one = jnp.ones(v.shape, v.dtype)
    stage = []
    for k, s in enumerate(ss):
        m_old = m_ref[k]
        m_new = jnp.maximum(m_old, jnp.max(s, axis=0, keepdims=True))
        alpha = jnp.exp2(m_old - m_new)
        p = jnp.exp2(s - m_new)
        m_ref[k] = m_new
        stage.append((alpha, p.astype(BF16)))
    for k, (alpha, p) in enumerate(stage):
        vk = jnp.where((lane < HEAD_DIM) == (k == 0), v, one)
        pv = lax.dot_general(vk, p, (((0,), (0,)), ((), ())), preferred_element_type=F32)
        acc_ref[k] = alpha * acc_ref[k] + pv


def _sel_chunk(rows, n_keys):
    chunk = SEL_TILE_ELEMS // rows
    while n_keys % chunk:
        chunk //= 2
    return chunk


def _sel_win_kernel(q_ref, mneg_ref, ksel_ref, vsel_ref, et_ref, kwin_ref, vwin_ref, ocmp_ref, misc_ref, eg_ref,
                    o_ref, lhs_ref, m_ref, acc_ref, *, q_off, win_pos0):
    tq = q_ref.shape[1]
    rows = GQA_GROUP * tq
    SEL_CHUNK = _sel_chunk(rows, ksel_ref.shape[1])
    t0 = q_off + pl.program_id(1) * tq
    n_sel = lax.shift_right_logical(t0 + tq - 1, int(math.log2(SEL_CHUNK))) + 1
    w_lo = jnp.maximum(t0 - (WINDOW - 1) - win_pos0, 0) // WIN_CHUNK
    w_hi = (t0 + tq - 1 - win_pos0) // WIN_CHUNK + 1

    def qrow(n_keys):
        return lax.broadcasted_iota(jnp.int32, (n_keys, rows), 1) % tq + t0

    def init():
        m_ref[...] = jnp.full(m_ref.shape, NEG, F32)
        acc_ref[...] = jnp.zeros(acc_ref.shape, F32)

    def finish():
        outs = []
        for kv in range(N_KV_HEADS):
            acc = acc_ref[kv]
            denom_row = HEAD_DIM * (1 - kv)
            outs.append(_heads_from_transposed(acc / acc[denom_row:denom_row + 1, :], tq, kv))
        return jnp.concatenate(outs, axis=1)

    for kv in range(N_KV_HEADS):
        for g in range(GQA_GROUP):
            hd = kv * GQA_GROUP + g
            lhs_ref[kv, g * tq:(g + 1) * tq, :LANES] = q_ref[0, :, hd * LANES:(hd + 1) * LANES]
            lhs_ref[kv, g * tq:(g + 1) * tq, LANES:] = mneg_ref[0, kv]

    init()

    def sel_step(i, carry, causal, width):
        n_keys = width * SEL_CHUNK
        r0 = pl.multiple_of(i * n_keys, n_keys)
        rhs = jnp.concatenate([ksel_ref[0, pl.ds(r0, n_keys), :], et_ref[pl.ds(r0, n_keys), :]], axis=1)
        v = vsel_ref[0, pl.ds(r0, n_keys), :]
        if causal:
            ok = r0 + lax.broadcasted_iota(jnp.int32, (n_keys, rows), 0) <= qrow(n_keys)
        ss = [_dot_nt(rhs, lhs_ref[kv]) for kv in range(N_KV_HEADS)]
        if causal:
            ss = [jnp.where(ok, s, NEG) for s in ss]
        _flash_update(ss, v, m_ref, acc_ref)
        return carry

    n_full = lax.shift_right_logical(t0 + 1, int(math.log2(SEL_CHUNK)))
    done = 0
    for width in SEL_WIDTHS:
        n_steps = (n_full - done) // width
        lax.fori_loop(done // width, done // width + n_steps,
                      functools.partial(sel_step, causal=False, width=width), 0)
        done = done + n_steps * width
    lax.fori_loop(n_full, n_sel, functools.partial(sel_step, causal=True, width=1), 0)
    o_sel = finish()

    init()

    def win_step(c, carry, width):
        n_keys = width * WIN_CHUNK
        r0 = pl.multiple_of(c * WIN_CHUNK, WIN_CHUNK)
        k = kwin_ref[0, pl.ds(r0, n_keys), :]
        v = vwin_ref[0, pl.ds(r0, n_keys), :]
        wpos = win_pos0 + r0 + lax.broadcasted_iota(jnp.int32, (n_keys, rows), 0)
        qr = qrow(n_keys)
        ok = (wpos <= qr) & (wpos > qr - WINDOW)
        ss = [jnp.where(ok, _dot_nt(k, lhs_ref[kv, :, :LANES]), NEG) for kv in range(N_KV_HEADS)]
        _flash_update(ss, v, m_ref, acc_ref)
        return carry

    n_span = (w_hi - w_lo) // WIN_SPAN
    lax.fori_loop(w_lo, w_lo + n_span, functools.partial(win_step, width=WIN_SPAN), 0)
    lax.fori_loop(w_lo + n_span * WIN_SPAN, w_hi, functools.partial(win_step, width=1), 0)
    o_win = finish()

    gates = jax.nn.sigmoid(misc_ref[0])
    ghi = gates.astype(BF16)
    glo = (gates - ghi.astype(F32)).astype(BF16)
    branches = (ocmp_ref[0], o_sel, o_win)
    out = jnp.zeros(branches[0].shape, F32)
    for br in range(3):
        out = out + (_dot(ghi, eg_ref[br]) + _dot(glo, eg_ref[br])) * branches[br]
    o_ref[0] = out


def sel_win_attention(qp, mneg, kvb, sel_col, winb, o_cmp, misc, q_off, win_pos0, tq):
    b, t, _ = qp.shape
    s = kvb.shape[1]
    sw = winb.shape[1]
    et = _block_onehot(s)
    eg = _gate_expand()
    rows = GQA_GROUP * tq
    assert q_off + t <= s and q_off + t - win_pos0 <= sw and sw % WIN_CHUNK == 0
    return pl.pallas_call(
        functools.partial(_sel_win_kernel, q_off=q_off, win_pos0=win_pos0),
        grid=(b, t // tq),
        in_specs=[
            pl.BlockSpec((1, tq, N_Q_HEADS * LANES), lambda i, j: (i, j, 0)),
            pl.BlockSpec((1, N_KV_HEADS, tq, N_SEL_LANES), lambda i, j: (i, 0, j, 0)),
            pl.BlockSpec((1, s, KV_WIDTH), lambda i, j: (i, 0, sel_col)),
            pl.BlockSpec((1, s, KV_WIDTH), lambda i, j: (i, 0, sel_col + 1)),
            pl.BlockSpec((s, N_SEL_LANES), lambda i, j: (0, 0)),
            pl.BlockSpec((1, sw, KV_WIDTH), lambda i, j: (i, 0, 0)),
            pl.BlockSpec((1, sw, KV_WIDTH), lambda i, j: (i, 0, 1)),
            pl.BlockSpec((1, tq, ATTN_WIDTH), lambda i, j: (i, j, 0)),
            pl.BlockSpec((1, tq, LANES), lambda i, j: (i, j, 0)),
            pl.BlockSpec((3, LANES, ATTN_WIDTH), lambda i, j: (0, 0, 0)),
        ],
        out_specs=pl.BlockSpec((1, tq, ATTN_WIDTH), lambda i, j: (i, j, 0)),
        out_shape=jax.ShapeDtypeStruct((b, t, ATTN_WIDTH), F32),
        scratch_shapes=[
            pltpu.VMEM((N_KV_HEADS, rows, 2 * LANES), BF16),
            pltpu.VMEM((N_KV_HEADS, 1, rows), F32),
            pltpu.VMEM((N_KV_HEADS, LANES, rows), F32),
        ],
        compiler_params=_cparams(("arbitrary", "arbitrary")),
        name="sel_win_attention",
    )(qp, mneg, kvb, kvb, et, winb, winb, o_cmp, misc, eg)


CONV_PAD = 8
HEAD_PAIRS = SSM_HEADS // 2


def _split3(x):
    a = x.astype(BF16)
    r = x - a.astype(F32)
    b = r.astype(BF16)
    c = (r - b.astype(F32)).astype(BF16)
    return a, b, c


def _ssd_kernel(xbc_ref, z_ref, misc_ref, conv0_ref, h0_ref, cw_ref, cb_ref, dtb_ref, a_ref, dsk_ref, nw_ref,
                y_ref, hout_ref, cout_ref, xp_ref, h_ref, ms_ref, *, t_valid):
    ch = pl.program_id(1)
    L = SSD_CHUNK
    keep = CONV_WIDTH - 1

    @pl.when(ch == 0)
    def _():
        xp_ref[...] = jnp.zeros(xp_ref.shape, F32)
        xp_ref[CONV_PAD - keep:CONV_PAD, :] = conv0_ref[0]
        h_ref[...] = h0_ref[0]

    xp_ref[CONV_PAD:CONV_PAD + t_valid, :] = xbc_ref[0]
    conv = cb_ref[...]
    for j in range(CONV_WIDTH):
        conv = conv + cw_ref[j:j + 1, :] * xp_ref[CONV_PAD - keep + j:CONV_PAD - keep + j + L, :]
    last = xp_ref[CONV_PAD + t_valid - keep:CONV_PAD + t_valid, :]
    cout_ref[0] = last
    xp_ref[CONV_PAD - keep:CONV_PAD, :] = last
    xc = _silu(conv)

    row = lax.broadcasted_iota(jnp.int32, (L, LANES), 0)
    lane = lax.broadcasted_iota(jnp.int32, (L, LANES), 1)
    if t_valid == L:
        raw = misc_ref[0]
    else:
        ms_ref[...] = jnp.zeros(ms_ref.shape, F32)
        ms_ref[0:t_valid, :] = misc_ref[0]
        raw = ms_ref[...]
    v = raw + dtb_ref[...]
    dt = jnp.maximum(v, 0.0) + jnp.log(1.0 + jnp.exp(-jnp.abs(v)))
    dt = jnp.where((lane < SSM_HEADS) & (row < t_valid), dt, 0.0)
    da = dt * a_ref[...]
    tri = (lax.broadcasted_iota(jnp.int32, (L, L), 1) <= lax.broadcasted_iota(jnp.int32, (L, L), 0))
    trib = tri.astype(BF16)
    acum = sum(_dot(trib, part) for part in _split3(da))
    acum_t = jnp.transpose(acum)
    dt_t = jnp.transpose(dt)
    e_acum = jnp.exp(acum)
    e_last = jnp.exp(acum[L - 1:L, :])
    w_end = jnp.exp(acum[L - 1:L, :] - acum) * dt
    lo = lane < SSM_HEAD_DIM

    ys = []
    for pair in range(HEAD_PAIRS):
        grp = (2 * pair) // (SSM_HEADS // SSM_GROUPS)
        bg = xc[:, SSM_WIDTH + grp * SSM_STATE:SSM_WIDTH + (grp + 1) * SSM_STATE].astype(BF16)
        cg = xc[:, SSM_WIDTH + (SSM_GROUPS + grp) * SSM_STATE:SSM_WIDTH + (SSM_GROUPS + grp + 1) * SSM_STATE].astype(BF16)
        g = _dot_nt(cg, bg)
        xpair = xc[:, pair * LANES:(pair + 1) * LANES]
        y = jnp.zeros((L, LANES), F32)
        for sub in range(2):
            hd = 2 * pair + sub
            seg = acum[:, hd:hd + 1] - acum_t[hd:hd + 1, :]
            m = g * jnp.exp(jnp.where(tri, seg, NEG)) * dt_t[hd:hd + 1, :]
            xm = jnp.where(lo if sub == 0 else ~lo, xpair, 0.0)
            y = y + _dot(m.astype(BF16), xm.astype(BF16))
        col = lambda a: jnp.where(lo, a[:, 2 * pair:2 * pair + 1], a[:, 2 * pair + 1:2 * pair + 2])
        hp = h_ref[pair]
        y = y + _dot_nt(cg, hp.astype(BF16)) * col(e_acum)
        y = y + col(dsk_ref[...]) * xpair
        xw = (xpair * col(w_end)).astype(BF16)
        st = lax.dot_general(xw, bg, (((0,), (0,)), ((), ())), preferred_element_type=F32)
        prow = lax.broadcasted_iota(jnp.int32, (LANES, LANES), 0) < SSM_HEAD_DIM
        dec = jnp.where(prow, e_last[:, 2 * pair:2 * pair + 1], e_last[:, 2 * pair + 1:2 * pair + 2])
        h_ref[pair] = hp * dec + st
        ys.append(y)
    y = jnp.concatenate(ys, axis=1)
    if t_valid != L:
        y = y[:t_valid]
    y = y * _silu(z_ref[0])
    y = y * lax.rsqrt(jnp.mean(y * y, axis=-1, keepdims=True) + EPS) * nw_ref[...]
    y_ref[0] = y

    @pl.when(ch == pl.num_programs(1) - 1)
    def _():
        hout_ref[0] = h_ref[...]


def ssd(xbc, z, misc, conv0, h0, conv_w, conv_b, dt_bias, a_log, d_skip, norm_w):
    b, t, _ = xbc.shape
    L = SSD_CHUNK
    t_valid = L if t % L == 0 else t
    assert t_valid == L or t < L
    n_ch = max(t // L, 1)
    keep = CONV_WIDTH - 1
    pad8 = lambda v: jnp.pad(v.astype(F32), (0, LANES - SSM_HEADS)).reshape(1, LANES)
    dtb = pad8(dt_bias)
    a = pad8(-jnp.exp(a_log.astype(F32)))
    dsk = pad8(d_skip)
    h0p = h0.reshape(b, HEAD_PAIRS, 2 * SSM_HEAD_DIM, SSM_STATE)
    full = lambda arr: pl.BlockSpec(arr.shape, lambda i, c: (0,) * arr.ndim)
    tok = lambda wd: pl.BlockSpec((1, t_valid, wd), lambda i, c: (i, c, 0))
    y, hout, cout = pl.pallas_call(
        functools.partial(_ssd_kernel, t_valid=t_valid),
        grid=(b, n_ch),
        in_specs=[
            tok(CONV_DIM), tok(SSM_WIDTH), tok(LANES),
            pl.BlockSpec((1, keep, CONV_DIM), lambda i, c: (i, 0, 0)),
            pl.BlockSpec((1, HEAD_PAIRS, 2 * SSM_HEAD_DIM, SSM_STATE), lambda i, c: (i, 0, 0, 0)),
            full(conv_w), pl.BlockSpec((1, CONV_DIM), lambda i, c: (0, 0)),
            full(dtb), full(a), full(dsk), pl.BlockSpec((1, SSM_WIDTH), lambda i, c: (0, 0)),
        ],
        out_specs=(
            tok(SSM_WIDTH),
            pl.BlockSpec((1, HEAD_PAIRS, 2 * SSM_HEAD_DIM, SSM_STATE), lambda i, c: (i, 0, 0, 0)),
            pl.BlockSpec((1, keep, CONV_DIM), lambda i, c: (i, 0, 0)),
        ),
        out_shape=(
            jax.ShapeDtypeStruct((b, t, SSM_WIDTH), F32),
            jax.ShapeDtypeStruct((b, HEAD_PAIRS, 2 * SSM_HEAD_DIM, SSM_STATE), F32),
            jax.ShapeDtypeStruct((b, keep, CONV_DIM), F32),
        ),
        scratch_shapes=[
            pltpu.VMEM((CONV_PAD + L, CONV_DIM), F32),
            pltpu.VMEM((HEAD_PAIRS, 2 * SSM_HEAD_DIM, SSM_STATE), F32),
            pltpu.VMEM((L, LANES), F32),
        ],
        compiler_params=_cparams(("arbitrary", "arbitrary")),
        name="ssd",
    )(xbc, z, misc, conv0, h0p, conv_w, conv_b.reshape(1, CONV_DIM), dtb, a, dsk, norm_w.reshape(1, SSM_WIDTH))
    return y, hout.reshape(b, SSM_HEADS, SSM_HEAD_DIM, SSM_STATE), cout


def _split2(x):
    hi = x.astype(BF16)
    return hi, (x - hi.astype(F32)).astype(BF16)


def _merge_kernel(oa_ref, ys_ref, x_ref, g1_ref, sh2_ref, sc2_ref, anw_ref, wo_ref, n2w_ref, wrh_ref, wrl_ref,
                  x1_ref, h2_ref, lg_ref):
    oa = oa_ref[...]
    a = oa * lax.rsqrt(jnp.mean(oa * oa, axis=-1, keepdims=True) + EPS) * anw_ref[...]
    cat = jnp.concatenate([a.astype(BF16), ys_ref[...].astype(BF16)], axis=1)
    x1 = x_ref[...] + _mod(g1_ref) * _dot(cat, wo_ref[...])
    x1_ref[...] = x1
    h2 = x1 * lax.rsqrt(jnp.mean(x1 * x1, axis=-1, keepdims=True) + EPS) * n2w_ref[...]
    h2 = h2 * (1.0 + _mod(sc2_ref)) + _mod(sh2_ref)
    h2_ref[...] = h2.astype(BF16)
    hh, hl = _split2(h2)
    lg_ref[...] = _dot_nt(wrh_ref[...], hh) + _dot_nt(wrh_ref[...], hl) + _dot_nt(wrl_ref[...], hh)


def merge(o_attn, y_ssm, x, mod3, mod_row0, attn_norm_w, wo, norm2_w, w_router, tm):
    b, t, d = x.shape
    n = b * t
    tiles_per_b = t // tm
    wrt = jnp.transpose(w_router)
    wrh, wrl = _split2(wrt)

    def mod_spec(col):
        return _mod_spec(mod3, col, tm, tiles_per_b, mod_row0)

    tok = lambda wd: pl.BlockSpec((tm, wd), lambda i: (i, 0))
    full = lambda a: pl.BlockSpec(a.shape, lambda i: (0,) * a.ndim)
    return pl.pallas_call(
        _merge_kernel,
        grid=(n // tm,),
        in_specs=[tok(ATTN_WIDTH), tok(SSM_WIDTH), tok(d), mod_spec(2), mod_spec(3), mod_spec(4),
                  full(attn_norm_w), full(wo), full(norm2_w), full(wrh), full(wrl)],
        out_specs=(tok(d), tok(d), pl.BlockSpec((N_EXPERTS, tm), lambda i: (0, i))),
        out_shape=(jax.ShapeDtypeStruct((n, d), F32), jax.ShapeDtypeStruct((n, d), BF16),
                   jax.ShapeDtypeStruct((N_EXPERTS, n), F32)),
        compiler_params=_cparams(("arbitrary",)),
        name="merge",
    )(o_attn.reshape(n, ATTN_WIDTH), y_ssm.reshape(n, SSM_WIDTH), x.reshape(n, d), mod3, mod3, mod3,
      attn_norm_w, wo, norm2_w, wrh, wrl)


EXPERTS_PER_GROUP = N_EXPERTS // N_EXPERT_GROUPS


def _first_max(x, ids, axes, n_ids):
    mx = jnp.max(x, axis=axes, keepdims=True)
    return ids == jnp.min(jnp.where(x == mx, ids, n_ids), axis=axes, keepdims=True), mx


def _route_kernel(lg_ref, eb_ref, tri_ref, w_ref, pos_ref, cnt_ref):
    lg = lg_ref[...]
    tn = lg.shape[2]
    scores = jax.nn.sigmoid(lg)
    biased = scores + eb_ref[...]
    sub = lax.broadcasted_iota(jnp.int32, lg.shape, 1)
    grp = lax.broadcasted_iota(jnp.int32, (N_EXPERT_GROUPS, 1, tn), 0)
    eid = lax.broadcasted_iota(jnp.int32, lg.shape, 0) * EXPERTS_PER_GROUP + sub
    hit, m1 = _first_max(biased, sub, 1, EXPERTS_PER_GROUP)
    m2 = jnp.max(jnp.where(hit, -jnp.inf, biased), axis=1, keepdims=True)
    gs = m1 + m2
    keep = jnp.zeros(gs.shape, jnp.bool_)
    for _ in range(TOPK_GROUPS):
        hit, _m = _first_max(gs, grp, 0, N_EXPERT_GROUPS)
        keep = keep | hit
        gs = jnp.where(hit, -jnp.inf, gs)
    x = jnp.where(keep, biased, NEG)
    sel = jnp.zeros(lg.shape, jnp.bool_)
    for _ in range(TOP_K):
        hit, _m = _first_max(x, eid, (0, 1), N_EXPERTS)
        sel = sel | hit
        x = jnp.where(hit, -jnp.inf, x)
    w = jnp.where(sel, scores, 0.0)
    w = w / jnp.sum(w, axis=(0, 1), keepdims=True) * ROUTED_SCALE
    w_ref[...] = w
    selb = sel.astype(BF16).reshape(N_EXPERTS, tn)
    pos = _dot(selb, tri_ref[...])
    pos_ref[...] = jnp.where(sel, pos.reshape(lg.shape), -1.0)
    cnt = jnp.sum(sel.astype(F32), axis=2, keepdims=True)
    cnt_ref[0] = jnp.broadcast_to(cnt, cnt_ref.shape[1:]).astype(jnp.int32)


def route(logits_t, e_bias, tn):
    n = logits_t.shape[1]
    lg3 = logits_t.reshape(N_EXPERT_GROUPS, EXPERTS_PER_GROUP, n)
    eb = e_bias.astype(F32).reshape(N_EXPERT_GROUPS, EXPERTS_PER_GROUP, 1)
    tri = jnp.asarray(np.triu(np.ones((tn, tn), np.float32), 1), BF16)
    blk = pl.BlockSpec((N_EXPERT_GROUPS, EXPERTS_PER_GROUP, tn), lambda i: (0, 0, i))
    w, pos, cnt = pl.pallas_call(
        _route_kernel,
        grid=(n // tn,),
        in_specs=[blk, pl.BlockSpec(eb.shape, lambda i: (0, 0, 0)), pl.BlockSpec((tn, tn), lambda i: (0, 0))],
        out_specs=(blk, blk, pl.BlockSpec((1, N_EXPERT_GROUPS, EXPERTS_PER_GROUP, LANES), lambda i: (i, 0, 0, 0))),
        out_shape=(jax.ShapeDtypeStruct(lg3.shape, F32), jax.ShapeDtypeStruct(lg3.shape, F32),
                   jax.ShapeDtypeStruct((n // tn, N_EXPERT_GROUPS, EXPERTS_PER_GROUP, LANES), jnp.int32)),
        compiler_params=_cparams(("arbitrary",)),
        name="route",
    )(lg3, eb, tri)
    return w.reshape(N_EXPERTS, n), pos.reshape(N_EXPERTS, n), cnt[..., 0].reshape(n // tn, N_EXPERTS)


MOE_ROWS = 128


def _swiglu(xb, wgu, wd, width):
    gu = _dot(xb, wgu)
    act = _silu(gu[:, :width]) * gu[:, width:]
    return _dot(act.astype(BF16), wd)


MOE_EXPERTS_PER_STEP = 4


MOE_ALIGN = 16
MOE_GATHER_ROWS = 896


def _moe_slots(tm):
    worst = TOP_K * tm + N_EXPERTS * (MOE_ALIGN - 1) + MOE_ROWS
    return -(-worst // MOE_GATHER_ROWS) * MOE_GATHER_ROWS


def _moe_kernel(cnt_ref, start_ref, h2_ref, w_ref, pos_ref, x1_ref, g2_ref, wgu_ref, wd_ref, sgu_ref, sd_ref,
                o_ref, g_all, xs):
    i = pl.program_id(0)
    es = pl.program_id(1)
    tm = h2_ref.shape[0]
    slots = g_all.shape[0]
    slot = lax.broadcasted_iota(jnp.int32, (MOE_ROWS, tm), 0).astype(F32)
    row = lax.broadcasted_iota(jnp.int32, (MOE_ROWS, 1), 0)

    def n_windows(cnt):
        return (cnt + MOE_ROWS - 1) // MOE_ROWS

    def window_start(e, j):
        return pl.multiple_of(start_ref[i * N_EXPERTS + e] + j * MOE_ROWS, MOE_ALIGN)

    @pl.when(es == 0)
    def _():
        g_all[...] = jnp.zeros(g_all.shape, BF16)

        def mark(e, carry):
            pos = pos_ref[pl.ds(e, 1), :]

            def mark_window(j, carry):
                hit = pos == slot + (j * MOE_ROWS).astype(F32)
                g_all[pl.ds(window_start(e, j), MOE_ROWS), :] = hit.astype(BF16)
                return carry

            return lax.fori_loop(0, n_windows(cnt_ref[i * N_EXPERTS + e]), mark_window, carry)

        lax.fori_loop(0, N_EXPERTS, mark, 0)

        def gather(c, carry):
            r0 = pl.multiple_of(c * MOE_GATHER_ROWS, MOE_GATHER_ROWS)
            rows = _dot(g_all[pl.ds(r0, MOE_GATHER_ROWS), :], h2_ref[...])
            xs[pl.ds(r0, MOE_GATHER_ROWS), :] = rows.astype(BF16)
            return carry

        lax.fori_loop(0, slots // MOE_GATHER_ROWS, gather, 0)

    for q in range(MOE_EXPERTS_PER_STEP):
        e = es * MOE_EXPERTS_PER_STEP + q
        cnt = cnt_ref[i * N_EXPERTS + e]
        wrow = w_ref[pl.ds(e, 1), :]

        def window(j, carry, q=q, e=e, cnt=cnt, wrow=wrow):
            r0 = window_start(e, j)
            xg = xs[pl.ds(r0, MOE_ROWS), :]
            out = _swiglu(xg, wgu_ref[q].astype(BF16), wd_ref[q].astype(BF16), D_EXPERT)
            g = g_all[pl.ds(r0, MOE_ROWS), :].astype(F32)
            out = out * jnp.sum(g * wrow, axis=1, keepdims=True)
            mine = row < cnt - j * MOE_ROWS
            xs[pl.ds(r0, MOE_ROWS), :] = jnp.where(mine, out.astype(BF16), xg)
            return carry

        lax.fori_loop(0, n_windows(cnt), window, 0)

    @pl.when(es == pl.num_programs(1) - 1)
    def _():
        y = lax.dot_general(g_all[...], xs[...], (((0,), (0,)), ((), ())), preferred_element_type=F32)
        y = y + _swiglu(h2_ref[...], sgu_ref[...], sd_ref[...], D_SHARED)
        o_ref[...] = x1_ref[...] + _mod(g2_ref) * y


def moe(h2, w_t, pos_t, counts, x1, mod3, mod_row0, t_per_b, wgu, wd, sgu, sd, tm):
    n, d = h2.shape
    tiles_per_b = t_per_b // tm
    eps = MOE_EXPERTS_PER_STEP
    slots = _moe_slots(tm)
    padded = (counts + MOE_ALIGN - 1) // MOE_ALIGN * MOE_ALIGN
    starts = jnp.cumsum(padded, axis=1) - padded
    grid_spec = pltpu.PrefetchScalarGridSpec(
        num_scalar_prefetch=2,
        grid=(n // tm, N_EXPERTS // eps),
        in_specs=[
            pl.BlockSpec((tm, d), lambda i, e, *_: (i, 0)),
            pl.BlockSpec((N_EXPERTS, tm), lambda i, e, *_: (0, i)),
            pl.BlockSpec((N_EXPERTS, tm), lambda i, e, *_: (0, i)),
            pl.BlockSpec((tm, d), lambda i, e, *_: (i, 0)),
            _mod_spec(mod3, 5, tm, tiles_per_b, mod_row0),
            pl.BlockSpec((eps, d, 2 * D_EXPERT), lambda i, e, *_: (e, 0, 0)),
            pl.BlockSpec((eps, D_EXPERT, d), lambda i, e, *_: (e, 0, 0)),
            pl.BlockSpec(sgu.shape, lambda i, e, *_: (0, 0)),
            pl.BlockSpec(sd.shape, lambda i, e, *_: (0, 0)),
        ],
        out_specs=pl.BlockSpec((tm, d), lambda i, e, *_: (i, 0)),
        scratch_shapes=[pltpu.VMEM((slots, tm), BF16), pltpu.VMEM((slots, d), BF16)],
    )
    return pl.pallas_call(
        _moe_kernel,
        grid_spec=grid_spec,
        out_shape=jax.ShapeDtypeStruct((n, d), F32),
        compiler_params=_cparams(("arbitrary", "arbitrary")),
        name="moe",
    )(counts.reshape(-1), starts.reshape(-1).astype(jnp.int32), h2, w_t, pos_t, x1, mod3, wgu, wd, sgu, sd)


SC_WINDOW = 128
PACK_W = 256
MOE_BLOCK_ROWS = 512
HI_MASK = -65536


def _pack_pair(x):
    bits = pltpu.bitcast(x.astype(BF16).astype(F32), jnp.int32)
    return lax.shift_right_logical(bits[:, :PACK_W], 16) | (bits[:, PACK_W:] & HI_MASK)


def _unpack_pair(word):
    lo = pltpu.bitcast(lax.shift_left(word, 16), F32)
    hi = pltpu.bitcast(word & HI_MASK, F32)
    return jnp.concatenate([lo, hi], axis=1)


def _pack_kernel(x_ref, a_ref, b_ref):
    x = x_ref[...]
    a_ref[...] = _pack_pair(x[:, :2 * PACK_W])
    b_ref[...] = _pack_pair(x[:, 2 * PACK_W:])


def pack_rows(x, tm):
    n, d = x.shape
    tok = lambda wd: pl.BlockSpec((tm, wd), lambda i: (i, 0))
    return pl.pallas_call(
        _pack_kernel, grid=(n // tm,), in_specs=[tok(d)], out_specs=(tok(PACK_W), tok(PACK_W)),
        out_shape=(jax.ShapeDtypeStruct((n, PACK_W), jnp.int32),) * 2,
        compiler_params=_cparams(("arbitrary",)), name="pack_rows",
    )(x)


def _slots_kernel(w_ref, pos_ref, base_ref, tri_ref, slot_ref, wt_ref):
    w = w_ref[...]
    pos = pos_ref[...]
    sel = pos >= 0.0
    rank = _dot(tri_ref[...], sel.astype(BF16))
    dest = base_ref[0] + pos
    slots, wts = [], []
    for j in range(TOP_K):
        mine = sel & (rank == float(j))
        slots.append(jnp.sum(jnp.where(mine, dest, 0.0), axis=0, keepdims=True))
        wts.append(jnp.sum(jnp.where(mine, w, 0.0), axis=0, keepdims=True))
    slot_ref[...] = jnp.concatenate(slots, axis=0).astype(jnp.int32)
    wpad = jnp.concatenate(wts + [jnp.zeros((LANES - TOP_K, w.shape[1]), F32)], axis=0)
    wt_ref[...] = jnp.transpose(wpad)


def slots_of(w_t, pos_t, base, tn):
    n = w_t.shape[1]
    tri = jnp.asarray(np.tril(np.ones((N_EXPERTS, N_EXPERTS), np.float32), -1), BF16)
    blk = pl.BlockSpec((N_EXPERTS, tn), lambda i: (0, i))
    return pl.pallas_call(
        _slots_kernel, grid=(n // tn,),
        in_specs=[blk, blk, pl.BlockSpec((1, N_EXPERTS, 1), lambda i: (i, 0, 0)),
                  pl.BlockSpec((N_EXPERTS, N_EXPERTS), lambda i: (0, 0))],
        out_specs=(pl.BlockSpec((TOP_K, tn), lambda i: (0, i)), pl.BlockSpec((tn, LANES), lambda i: (i, 0))),
        out_shape=(jax.ShapeDtypeStruct((TOP_K, n), jnp.int32), jax.ShapeDtypeStruct((n, LANES), F32)),
        compiler_params=_cparams(("arbitrary",)), name="moe_slots",
    )(w_t, pos_t, base, tri)


def sc_scatter_rows(rows, idx, n_out):
    n, d = rows.shape
    m = idx.shape[0]
    nb = n // SC_WINDOW
    mesh = plsc.VectorSubcoreMesh(core_axis_name="core", subcore_axis_name="subcore")

    @functools.partial(pl.kernel, out_type=jax.ShapeDtypeStruct((n_out, d), rows.dtype), mesh=mesh)
    def scatter(x_hbm, i_hbm, o_hbm):
        def body(x_vmem, i_vmem):
            pltpu.sync_copy(x_vmem, o_hbm.at[i_vmem.at[0]])

        pltpu.emit_pipeline(
            body, grid=(m // SC_WINDOW,),
            in_specs=[pl.BlockSpec((SC_WINDOW, d), index_map=lambda i: (i % nb, 0)),
                      pl.BlockSpec((1, SC_WINDOW), index_map=lambda i: (0, i))],
            out_specs=[], core_axis_name=("core", "subcore"), dimension_semantics=(pltpu.PARALLEL,),
        )(x_hbm, i_hbm)

    return scatter(rows, idx.reshape(1, m))


def sc_gather_rows(table, idx):
    d = table.shape[1]
    m = idx.shape[0]
    mesh = plsc.VectorSubcoreMesh(core_axis_name="core", subcore_axis_name="subcore")

    @functools.partial(pl.kernel, out_type=jax.ShapeDtypeStruct((m, d), table.dtype), mesh=mesh)
    def gather(x_hbm, i_hbm, o_hbm):
        def body(i_vmem, o_vmem):
            pltpu.sync_copy(x_hbm.at[i_vmem.at[0]], o_vmem)

        pltpu.emit_pipeline(
            body, grid=(m // SC_WINDOW,),
            in_specs=[pl.BlockSpec((1, SC_WINDOW), index_map=lambda i: (0, i))],
            out_specs=[pl.BlockSpec((SC_WINDOW, d), index_map=lambda i: (i, 0))],
            core_axis_name=("core", "subcore"), dimension_semantics=(pltpu.PARALLEL,),
        )(i_hbm, o_hbm)

    return gather(table, idx.reshape(1, m))


def _experts_kernel(be_ref, nu_ref, xa_ref, xb_ref, wgu_ref, wd_ref, oa_ref, ob_ref, wgu_bf, wd_bf):
    b = pl.program_id(0)

    @pl.when(b < nu_ref[0])
    def _():
        @pl.when((b == 0) | (be_ref[b] != be_ref[jnp.maximum(b - 1, 0)]))
        def _():
            wgu_bf[...] = wgu_ref[0].astype(BF16)
            wd_bf[...] = wd_ref[0].astype(BF16)

        x = jnp.concatenate([_unpack_pair(xa_ref[...]), _unpack_pair(xb_ref[...])], axis=1).astype(BF16)
        out = _swiglu(x, wgu_bf[...], wd_bf[...], D_EXPERT)
        oa_ref[...] = _pack_pair(out[:, :2 * PACK_W])
        ob_ref[...] = _pack_pair(out[:, 2 * PACK_W:])


def experts_sorted(xa, xb, block_expert, n_used, wgu, wd):
    r = xa.shape[0]
    d = wd.shape[2]
    row = lambda b, be, nu: (jnp.minimum(b, nu[0] - 1), 0)
    blk = pl.BlockSpec((MOE_BLOCK_ROWS, PACK_W), row)
    grid_spec = pltpu.PrefetchScalarGridSpec(
        num_scalar_prefetch=2, grid=(r // MOE_BLOCK_ROWS,),
        in_specs=[blk, blk,
                  pl.BlockSpec((1, d, 2 * D_EXPERT), lambda b, be, nu: (be[b], 0, 0)),
                  pl.BlockSpec((1, D_EXPERT, d), lambda b, be, nu: (be[b], 0, 0))],
        out_specs=(blk, blk),
        scratch_shapes=[pltpu.VMEM((d, 2 * D_EXPERT), BF16), pltpu.VMEM((D_EXPERT, d), BF16)],
    )
    return pl.pallas_call(
        _experts_kernel, grid_spec=grid_spec,
        out_shape=(jax.ShapeDtypeStruct((r, PACK_W), jnp.int32),) * 2,
        compiler_params=_cparams(("arbitrary",)), name="moe_experts",
    )(block_expert, n_used, xa, xb, wgu, wd)


def _combine_kernel(ya_ref, yb_ref, wt_ref, h2_ref, x1_ref, g2_ref, sgu_ref, sd_ref, o_ref):
    wt = wt_ref[...]
    acc = _swiglu(h2_ref[...], sgu_ref[...], sd_ref[...], D_SHARED)
    for j in range(TOP_K):
        y = jnp.concatenate([_unpack_pair(ya_ref[j]), _unpack_pair(yb_ref[j])], axis=1)
        acc = acc + wt[:, j:j + 1] * y
    o_ref[...] = x1_ref[...] + _mod(g2_ref) * acc


def combine_sorted(ya, yb, wt, h2, x1, mod3, mod_row0, t_per_b, sgu, sd, tm):
    n, d = h2.shape
    tiles_per_b = t_per_b // tm
    tok = lambda wd: pl.BlockSpec((tm, wd), lambda i: (i, 0))
    yblk = pl.BlockSpec((TOP_K, tm, PACK_W), lambda i: (0, i, 0))
    full = lambda a: pl.BlockSpec(a.shape, lambda i: (0,) * a.ndim)
    return pl.pallas_call(
        _combine_kernel, grid=(n // tm,),
        in_specs=[yblk, yblk, tok(LANES), tok(d), tok(d), _mod_spec(mod3, 5, tm, tiles_per_b, mod_row0),
                  full(sgu), full(sd)],
        out_specs=tok(d), out_shape=jax.ShapeDtypeStruct((n, d), F32),
        compiler_params=_cparams(("arbitrary",)), name="moe_combine",
    )(ya, yb, wt, h2, x1, mod3, sgu, sd)


def moe_sorted(h2, w_t, pos_t, counts, x1, mod3, mod_row0, t_per_b, wgu, wd, sgu, sd, tm, overlap):
    n, d = h2.shape
    assert d == 4 * PACK_W and n % SC_WINDOW == 0
    n_blocks = (TOP_K * n + N_EXPERTS * (MOE_BLOCK_ROWS - 1)) // MOE_BLOCK_ROWS
    total = jnp.sum(counts, axis=0)
    region = (total + MOE_BLOCK_ROWS - 1) // MOE_BLOCK_ROWS * MOE_BLOCK_ROWS
    region_end = jnp.cumsum(region)
    base = (region_end - region)[None, :] + jnp.cumsum(counts, axis=0) - counts
    block_row0 = jnp.arange(n_blocks, dtype=region_end.dtype) * MOE_BLOCK_ROWS
    block_expert = jnp.sum(region_end[None, :] <= block_row0[:, None], axis=1)
    block_expert = jnp.minimum(block_expert, N_EXPERTS - 1).astype(jnp.int32)
    n_used = (region_end[-1:] // MOE_BLOCK_ROWS).astype(jnp.int32)
    slot, wt = slots_of(w_t, pos_t, base.astype(F32).reshape(-1, N_EXPERTS, 1), tm)
    dest = slot.reshape(-1)
    ha, hb = pack_rows(h2, tm)
    rows = n_blocks * MOE_BLOCK_ROWS
    xa, xb = sc_scatter_rows(ha, dest, rows), sc_scatter_rows(hb, dest, rows)
    n_used, rest = lax.optimization_barrier((n_used, overlap()))
    oa, ob = experts_sorted(xa, xb, block_expert, n_used, wgu, wd)
    ya = sc_gather_rows(oa, dest).reshape(TOP_K, n, PACK_W)
    yb = sc_gather_rows(ob, dest).reshape(TOP_K, n, PACK_W)
    return combine_sorted(ya, yb, wt, h2, x1, mod3, mod_row0, t_per_b, sgu, sd, tm), rest


GATHER_PAGES = 8


def _gather_kernel(pt_ref, *refs):
    pages, new_ref = refs[:GATHER_PAGES], refs[GATHER_PAGES]
    rows_ref, cmpx_ref, stage_ref = refs[GATHER_PAGES + 1:]
    step = pl.program_id(1)
    last = pl.num_programs(1) - 1
    n_rows = GATHER_PAGES * PAGE_SIZE

    @pl.when(step < last)
    def _():
        for k in range(GATHER_PAGES):
            sl = slice(k * PAGE_SIZE, (k + 1) * PAGE_SIZE)
            for r in range(4):
                tile = jnp.transpose(pages[k][0, r])
                if r < 2:
                    stage_ref[r, sl, :] = tile
                else:
                    rows_ref[0, sl, (r - 2) * KV_WIDTH:(r - 1) * KV_WIDTH] = tile.astype(BF16)

    @pl.when(step == last)
    def _():
        new = new_ref[0]
        tn = new.shape[0]
        stage_ref[...] = jnp.zeros(stage_ref.shape, F32)
        for s in range(2):
            stage_ref[s, 0:tn, :] = new[:, s * KV_WIDTH:(s + 1) * KV_WIDTH]
        pad = jnp.zeros((n_rows - tn, 2 * KV_WIDTH), F32)
        rows_ref[0] = jnp.concatenate([new[:, 2 * KV_WIDTH:], pad], axis=0).astype(BF16)

    _stride_block_store(stage_ref, cmpx_ref, n_rows)


def gather_pages(cache_t, page_table, new_rows):
    b, n_pages = page_table.shape
    steps = n_pages // GATHER_PAGES
    rows = GATHER_PAGES * PAGE_SIZE
    s_out = (steps + 1) * rows

    def page_spec(k):
        def idx(i, s, pt):
            p = jnp.minimum(s, steps - 1) * GATHER_PAGES + k
            return (pt[i * n_pages + p], 0, 0, 0)
        return pl.BlockSpec((1, 4, KV_WIDTH, PAGE_SIZE), idx)

    grid_spec = pltpu.PrefetchScalarGridSpec(
        num_scalar_prefetch=1,
        grid=(b, steps + 1),
        in_specs=[page_spec(k) for k in range(GATHER_PAGES)]
        + [pl.BlockSpec((1,) + new_rows.shape[1:], lambda i, s, pt: (i, 0, 0))],
        out_specs=(
            pl.BlockSpec((1, rows, 2 * KV_WIDTH), lambda i, s, pt: (i, s, 0)),
            pl.BlockSpec((1, rows // CMP_STRIDE, CMP_STRIDE * 2 * KV_WIDTH), lambda i, s, pt: (i, s, 0)),
        ),
        scratch_shapes=[pltpu.VMEM((2, rows, KV_WIDTH), F32)],
    )
    return pl.pallas_call(
        _gather_kernel,
        grid_spec=grid_spec,
        out_shape=(jax.ShapeDtypeStruct((b, s_out, 2 * KV_WIDTH), BF16),
                   jax.ShapeDtypeStruct((b, s_out // CMP_STRIDE, CMP_STRIDE * 2 * KV_WIDTH), BF16)),
        compiler_params=_cparams(("arbitrary", "arbitrary")),
        name="gather_pages",
    )(page_table.reshape(-1), *([cache_t] * GATHER_PAGES), new_rows)


def _attention(qp, cmpx, kvb, sel_col, winb, misc, cmp_w, q_off, win_pos0, tq):
    t = qp.shape[1]
    cur_lo, cur_hi = q_off // SEL_BLOCK, (q_off + t - 1) // SEL_BLOCK
    assert cur_hi < N_SEL_LANES or (cur_lo == cur_hi == N_SEL_LANES), (q_off, t)
    n_pick = N_SEL - (1 if cur_hi >= N_SEL_LANES else 0)
    kcv = compress(cmpx, *cmp_w)
    o_cmp, mneg = cmp_select(qp, kcv, q_off, n_pick, tq)
    return sel_win_attention(qp, mneg, kvb, sel_col, winb, o_cmp, misc, q_off, win_pos0, tq)


def kernel(x_prompt, x_sample, cache_kv, cache_win, state_ssm, state_conv, page_table, c_prompt, c_sample, w_ada, b_ada, norm1_w, norm2_w, w_in, q_norm_w, k_norm_w, cmp_pe, cmp_w1, cmp_w2, attn_out_norm_w, conv_w, conv_b, dt_bias, a_log, d_skip, ssm_norm_w, w_out, w_router, e_bias, w_exp_gu, w_exp_down, w_sh_gu, w_sh_down):
    xp, xq = x_prompt, x_sample
    bp, tp, d = xp.shape
    bq, tq, _ = xq.shape
    depth = w_ada.shape[0]
    past_len = page_table.shape[1] * PAGE_SIZE
    nq = bq * tq
    tq_pad = LANES // GQA_GROUP
    assert tp % TOKEN_TILE == 0 and tp >= WINDOW and nq % 8 == 0 and tq <= tq_pad
    pos_p = jnp.arange(tp, dtype=jnp.int32)
    pos_q = jnp.tile(past_len + jnp.arange(tq, dtype=jnp.int32), bq)
    c_all = jnp.concatenate([c_prompt, c_sample], axis=0)
    c_all = jnp.pad(c_all, ((0, -c_all.shape[0] % 8), (0, 0)))
    outs = [[] for _ in range(8)]
    for l in range(depth):
        mod = adaln_all(c_all, w_ada[l], b_ada[l])
        mod_p = mod.reshape(mod.shape[0], 1, 6 * d)
        mod_q = jnp.repeat(mod[bp:bp + bq], tq, axis=0)
        wp = _prep_w_in(w_in[l])
        cmp_w = _prep_compress(cmp_pe[l], cmp_w1[l], cmp_w2[l])
        wo = w_out[l].astype(BF16)
        wgu, wd = w_exp_gu[l], w_exp_down[l]
        sgu, sd = w_sh_gu[l].astype(BF16), w_sh_down[l].astype(BF16)
        ssm_w = (conv_w[l], conv_b[l], dt_bias[l], a_log[l], d_skip[l], ssm_norm_w[l])
        n1w, n2w, anw = norm1_w[l:l + 1], norm2_w[l:l + 1], attn_out_norm_w[l:l + 1]

        qp, kvb, win, winb, z, xbc, misc, kvt, cmpx = inproj(xp, mod_p, 0, n1w, wp, q_norm_w[l], k_norm_w[l], pos_p,
                                                            TOKEN_TILE, True)
        r3 = lambda a: a.reshape(bp, tp, a.shape[-1])
        o_attn = _attention(r3(qp), cmpx, r3(kvb), 2, r3(winb), r3(misc), cmp_w, 0, 0, QUERY_TILE)
        y_ssm, h_new, conv_new = ssd(r3(xbc), r3(z), r3(misc), jnp.zeros((bp, CONV_WIDTH - 1, CONV_DIM), F32),
                                     jnp.zeros((bp, SSM_HEADS, SSM_HEAD_DIM, SSM_STATE), F32), *ssm_w)
        x1, h2, lg = merge(o_attn, y_ssm, xp, mod_p, 0, anw, wo, n2w, w_router[l], TOKEN_TILE)
        w_t, pos_t, cnt = route(lg, e_bias[l], TOKEN_TILE)
        outs[0].append(jnp.transpose(kvt.reshape(bp, 4, N_KV_HEADS, HEAD_DIM, tp), (0, 4, 1, 2, 3)))
        outs[1].append(win.reshape(bp, tp, 2, N_KV_HEADS, HEAD_DIM)[:, tp - WINDOW:])
        outs[2].append(h_new)
        outs[3].append(conv_new)

        xq1 = xq.reshape(1, nq, d)
        rq = lambda a: a.reshape(bq, tq, a.shape[-1])
        padq = lambda a: jnp.pad(rq(a), ((0, 0), (0, tq_pad - tq), (0, 0)))

        def sample_front():
            proj = inproj(xq1, mod_q, 0, n1w, wp, q_norm_w[l], k_norm_w[l], pos_q, nq, False)
            cache_t = jnp.transpose(cache_kv[l], (0, 2, 3, 4, 1)).reshape(cache_kv.shape[1], 4, KV_WIDTH, PAGE_SIZE)
            past, cmpx = gather_pages(cache_t, page_table, rq(proj[-1]))
            return proj, past, cmpx

        xp, (proj, past, cmpx) = moe_sorted(h2, w_t, pos_t, cnt, x1, mod_p, 0, tp, wgu, wd, sgu, sd, TOKEN_TILE,
                                            sample_front)
        xp = xp.reshape(bp, tp, d)
        qp, kvb, win, winb, z, xbc, misc, kv = proj
        win_all = jnp.concatenate([cache_win[l].reshape(bq, WINDOW, 2 * KV_WIDTH).astype(BF16), rq(winb),
                                   jnp.zeros((bq, -(WINDOW + tq_pad) % WIN_CHUNK + tq_pad - tq, 2 * KV_WIDTH), BF16)],
                                  axis=1)
        o_attn = _attention(padq(qp), cmpx, past, 0, win_all, padq(misc), cmp_w, past_len, past_len - WINDOW,
                            tq_pad)[:, :tq]
        y_ssm, h_new, conv_new = ssd(rq(xbc), rq(z), rq(misc), state_conv[l], state_ssm[l], *ssm_w)
        x1, h2, lg = merge(o_attn.reshape(1, nq, ATTN_WIDTH), y_ssm.reshape(1, nq, SSM_WIDTH), xq1, mod_q, 0,
                           anw, wo, n2w, w_router[l], nq)
        w_t, pos_t, cnt = route(lg, e_bias[l], nq)
        xq = moe(h2, w_t, pos_t, cnt, x1, mod_q, 0, nq, wgu, wd, sgu, sd, nq).reshape(bq, tq, d)
        win_rows = win.reshape(bq, tq, 2, N_KV_HEADS, HEAD_DIM)
        outs[4].append(kv.reshape(bq, tq, 4, N_KV_HEADS, HEAD_DIM))
        outs[5].append(jnp.concatenate([cache_win[l], win_rows.astype(cache_win.dtype)], axis=1)[:, tq:])
        outs[6].append(h_new)
        outs[7].append(conv_new)
    return (xp, xq) + tuple(jnp.stack(o) for o in outs)
```

```python
import functools
import math

import jax
import jax.numpy as jnp
import numpy as np
from jax import lax
from jax.experimental import pallas as pl
from jax.experimental.pallas import tpu as pltpu
from jax.experimental.pallas import tpu_sc as plsc

D_MODEL = 1024
PAGE_SIZE = 128
HEAD_DIM = 64
N_Q_HEADS = 8
N_KV_HEADS = 2
GQA_GROUP = N_Q_HEADS // N_KV_HEADS
ATTN_WIDTH = N_Q_HEADS * HEAD_DIM
KV_WIDTH = N_KV_HEADS * HEAD_DIM
ROPE_DIM = HEAD_DIM // 4
ROPE_THETA = 500000.0
CMP_LEN = 32
CMP_STRIDE = 16
CMP_HIDDEN = 4 * HEAD_DIM
SEL_BLOCK = 64
N_SEL = 16
N_LOCAL = 2
WINDOW = 512
SSM_HEADS = 8
SSM_HEAD_DIM = 64
SSM_WIDTH = SSM_HEADS * SSM_HEAD_DIM
SSM_GROUPS = 2
SSM_STATE = 128
CONV_WIDTH = 4
CONV_DIM = SSM_WIDTH + 2 * SSM_GROUPS * SSM_STATE
SSD_CHUNK = 128
MIX_WIDTH = ATTN_WIDTH + SSM_WIDTH
N_EXPERTS = 64
N_EXPERT_GROUPS = 8
TOPK_GROUPS = 4
TOP_K = 8
D_EXPERT = 256
D_SHARED = 256
ROUTED_SCALE = 2.5
IN_SIZES = (ATTN_WIDTH, 6 * KV_WIDTH, 3 * N_Q_HEADS, SSM_WIDTH, CONV_DIM, SSM_HEADS)
N_IN = sum(IN_SIZES)
EPS = 1e-6
NEG = -1e30
BIG = 1e6

LANES = 128
TOKEN_TILE = 512
QUERY_TILE = 128
VMEM_LIMIT = 56 * 1024 * 1024

BF16 = jnp.bfloat16
F32 = jnp.float32
LOG2E = math.log2(math.e)


def _cparams(sem, flags=None):
    return pltpu.CompilerParams(dimension_semantics=sem, vmem_limit_bytes=VMEM_LIMIT, flags=flags)


def _silu(x):
    return x * jax.nn.sigmoid(x)


def _dot(a, b):
    return jnp.dot(a, b, preferred_element_type=F32)


def _dot_nt(a, b):
    return lax.dot_general(a, b, (((1,), (1,)), ((), ())), preferred_element_type=F32)


def _mod_spec(mod, col, tm, tiles_per_b, row0):
    if mod.ndim == 3:
        return pl.BlockSpec((1, 1, D_MODEL), lambda i, *_: (row0 + i // tiles_per_b, 0, col))
    return pl.BlockSpec((tm, D_MODEL), lambda i, *_: (i, col))


def _mod(ref):
    return ref[0] if len(ref.shape) == 3 else ref[...]


def _adaln_kernel(c_ref, w_ref, b_ref, o_ref):
    c = c_ref[...]
    a = _silu(c).astype(BF16)
    o_ref[...] = _dot(a, w_ref[...].astype(BF16)) + b_ref[...]


def adaln_all(c_all, w_ada, b_ada):
    rows = c_all.shape[0]
    n = w_ada.shape[1]
    tn = 1024
    return pl.pallas_call(
        _adaln_kernel,
        grid=(n // tn,),
        in_specs=[
            pl.BlockSpec((rows, D_MODEL), lambda j: (0, 0)),
            pl.BlockSpec((D_MODEL, tn), lambda j: (0, j)),
            pl.BlockSpec((1, tn), lambda j: (0, j)),
        ],
        out_specs=pl.BlockSpec((rows, tn), lambda j: (0, j)),
        out_shape=jax.ShapeDtypeStruct((rows, n), F32),
        compiler_params=_cparams(("arbitrary",)),
        name="adaln",
    )(c_all, w_ada, b_ada.reshape(1, n))


_C_Q = 0
_C_KV = _C_Q + ATTN_WIDTH
_C_Z = _C_KV + 6 * KV_WIDTH
_C_XBC = _C_Z + SSM_WIDTH
_C_MISC = _C_XBC + CONV_DIM
N_IN_PAD = _C_MISC + LANES
N_GATES = 3 * N_Q_HEADS


def _prep_w_in(w_in):
    s = np.cumsum((0,) + IN_SIZES)
    q, kv, g, z, xbc, dt = (w_in[:, int(s[i]):int(s[i + 1])] for i in range(6))
    pad = jnp.zeros((w_in.shape[0], LANES - N_GATES - SSM_HEADS), w_in.dtype)
    return jnp.concatenate([q, kv, z, xbc, dt, g, pad], axis=1).astype(BF16)


def _group_mean_matrix(width):
    i = np.arange(width)
    m = (i[:, None] // HEAD_DIM == i[None, :] // HEAD_DIM).astype(np.float32) / HEAD_DIM
    return jnp.asarray(m, BF16)


def _rope_tables(pos):
    half = ROPE_DIM // 2
    inv_freq = ROPE_THETA ** (-jnp.arange(half, dtype=F32) / half)
    ang = pos.astype(F32)[:, None] * inv_freq[None, :]
    cos, sin = jnp.cos(ang), jnp.sin(ang)
    t = pos.shape[0]
    one = jnp.ones((t, HEAD_DIM - ROPE_DIM), F32)
    zero = jnp.zeros((t, HEAD_DIM - ROPE_DIM), F32)
    zh = jnp.zeros((t, half), F32)
    c = jnp.concatenate([cos, cos, one], axis=1)
    s_up = jnp.concatenate([-sin, zh, zero], axis=1)
    s_dn = jnp.concatenate([zh, sin, zero], axis=1)
    rep = LANES // HEAD_DIM
    return jnp.tile(c, (1, rep)), jnp.tile(s_up, (1, rep)), jnp.tile(s_dn, (1, rep))


def _rope(x, c, s_up, s_dn):
    w = x.shape[1]
    half = ROPE_DIM // 2
    rep = w // LANES
    ct = jnp.concatenate([c] * rep, axis=1) if rep > 1 else c
    su = jnp.concatenate([s_up] * rep, axis=1) if rep > 1 else s_up
    sd = jnp.concatenate([s_dn] * rep, axis=1) if rep > 1 else s_dn
    up = pltpu.roll(x, w - half, axis=1)
    dn = pltpu.roll(x, half, axis=1)
    return x * ct + up * su + dn * sd


def _stride_block_store(stage_ref, cmpx_ref, n_rows):
    nb = n_rows // CMP_STRIDE
    lane = lax.broadcasted_iota(jnp.int32, (nb, KV_WIDTH), 1)
    lo = lane < HEAD_DIM
    span = CMP_STRIDE * HEAD_DIM
    for s in range(2):
        for m in range(CMP_STRIDE // 2):
            r0 = stage_ref[s, pl.ds(2 * m, nb, stride=CMP_STRIDE), :]
            r1 = stage_ref[s, pl.ds(2 * m + 1, nb, stride=CMP_STRIDE), :]
            head0 = jnp.where(lo, r0, pltpu.roll(r1, HEAD_DIM, axis=1))
            head1 = jnp.where(lo, pltpu.roll(r0, HEAD_DIM, axis=1), r1)
            for h, piece in enumerate((head0, head1)):
                c0 = (2 * s + h) * span + m * KV_WIDTH
                cmpx_ref[0, :, c0:c0 + KV_WIDTH] = piece.astype(BF16)


def _inproj_kernel(x_ref, shift_ref, scale_ref, nw_ref, w_ref, qw_ref, kw_ref, gq_ref, gk_ref,
                   c_ref, su_ref, sd_ref,
                   qp_ref, kvb_ref, win_ref, winb_ref, z_ref, xbc_ref, misc_ref, *rest, seq_layout):
    x = x_ref[...]
    ms = jnp.mean(x * x, axis=-1, keepdims=True)
    h = x * lax.rsqrt(ms + EPS) * nw_ref[...]
    h = h * (1.0 + _mod(scale_ref)) + _mod(shift_ref)
    hb = h.astype(BF16)
    c, su, sd = c_ref[...], su_ref[...], sd_ref[...]

    q = _dot(hb, w_ref[:, _C_Q:_C_Q + ATTN_WIDTH])
    qms = _dot((q * q).astype(BF16), gq_ref[...])
    q = q * lax.rsqrt(qms + EPS) * qw_ref[...]
    q = _rope(q, c, su, sd) * (HEAD_DIM ** -0.5 * LOG2E)
    lane = lax.broadcasted_iota(jnp.int32, q.shape, 1) % LANES
    lo = lane < HEAD_DIM
    q_up = pltpu.roll(q, ATTN_WIDTH - HEAD_DIM, axis=1)
    q_dn = pltpu.roll(q, HEAD_DIM, axis=1)
    zero = jnp.zeros_like(q)
    nat_lo = jnp.where(lo, q, zero)
    nat_hi = jnp.where(lo, zero, q)
    up_lo = jnp.where(lo, q_up, zero)
    dn_hi = jnp.where(lo, zero, q_dn)
    blocks = []
    for hd in range(N_Q_HEADS):
        pair = hd // 2
        sl = slice(pair * LANES, (pair + 1) * LANES)
        if hd < GQA_GROUP:
            blocks.append((nat_lo if hd % 2 == 0 else up_lo)[:, sl])
        else:
            blocks.append((dn_hi if hd % 2 == 0 else nat_hi)[:, sl])
    qp_ref[...] = jnp.concatenate(blocks, axis=1).astype(BF16)

    kv = _dot(hb, w_ref[:, _C_KV:_C_KV + 6 * KV_WIDTH])
    outs = []
    for br in range(3):
        k = kv[:, br * 2 * KV_WIDTH:br * 2 * KV_WIDTH + KV_WIDTH]
        v = kv[:, br * 2 * KV_WIDTH + KV_WIDTH:(br + 1) * 2 * KV_WIDTH]
        kms = _dot((k * k).astype(BF16), gk_ref[...])
        k = k * lax.rsqrt(kms + EPS) * kw_ref[:, br * KV_WIDTH:(br + 1) * KV_WIDTH]
        k = _rope(k, c, su, sd)
        outs += [k, v]
    kvrows = jnp.concatenate(outs[:4], axis=1)
    winrows = jnp.concatenate(outs[4:], axis=1)
    kvb_ref[...] = kvrows.astype(BF16)
    win_ref[...] = winrows
    winb_ref[...] = winrows.astype(BF16)
    if seq_layout:
        kvt_ref, cmpx_ref, stage_ref = rest
        tm = kvrows.shape[0]
        for r in range(4):
            kvt_ref[0, r] = jnp.transpose(kvrows[:, r * KV_WIDTH:(r + 1) * KV_WIDTH])
        for s in range(2):
            stage_ref[s] = kvrows[:, s * KV_WIDTH:(s + 1) * KV_WIDTH]
        _stride_block_store(stage_ref, cmpx_ref, tm)
    else:
        rest[0][...] = kvrows

    z_ref[...] = _dot(hb, w_ref[:, _C_Z:_C_Z + SSM_WIDTH])
    xbc_ref[...] = _dot(hb, w_ref[:, _C_XBC:_C_XBC + CONV_DIM])
    misc_ref[...] = _dot(hb, w_ref[:, _C_MISC:_C_MISC + LANES])


def inproj(x, mod3, mod_row0, norm_w, wp, q_norm_w, k_norm_w, pos, tm, seq_layout):
    b, t, d = x.shape
    n = b * t
    tiles_per_b = t // tm
    xf = x.reshape(n, d)
    c, su, sd = _rope_tables(pos)
    qw = jnp.tile(q_norm_w, N_Q_HEADS).reshape(1, ATTN_WIDTH)
    kw = jnp.concatenate([jnp.tile(k_norm_w[i], N_KV_HEADS) for i in range(3)]).reshape(1, 3 * KV_WIDTH)
    gq = _group_mean_matrix(ATTN_WIDTH)
    gk = _group_mean_matrix(KV_WIDTH)

    def mod_spec(col):
        return _mod_spec(mod3, col, tm, tiles_per_b, mod_row0)

    def tok(wd):
        return pl.BlockSpec((tm, wd), lambda i: (i, 0))

    def full(a):
        return pl.BlockSpec(a.shape, lambda i: (0,) * a.ndim)

    rope_spec = pl.BlockSpec((tm, LANES), lambda i: (i % tiles_per_b, 0))
    out_shape = [
        jax.ShapeDtypeStruct((n, N_Q_HEADS * LANES), BF16),
        jax.ShapeDtypeStruct((n, 4 * KV_WIDTH), BF16),
        jax.ShapeDtypeStruct((n, 2 * KV_WIDTH), F32),
        jax.ShapeDtypeStruct((n, 2 * KV_WIDTH), BF16),
        jax.ShapeDtypeStruct((n, SSM_WIDTH), F32),
        jax.ShapeDtypeStruct((n, CONV_DIM), F32),
        jax.ShapeDtypeStruct((n, LANES), F32),
    ]
    out_specs = [tok(s.shape[1]) for s in out_shape]
    scratch = []
    if seq_layout:
        out_shape += [jax.ShapeDtypeStruct((b, 4, KV_WIDTH, t), F32),
                      jax.ShapeDtypeStruct((b, t // CMP_STRIDE, CMP_STRIDE * 2 * KV_WIDTH), BF16)]
        out_specs += [pl.BlockSpec((1, 4, KV_WIDTH, tm), lambda i: (i // tiles_per_b, 0, 0, i % tiles_per_b)),
                      pl.BlockSpec((1, tm // CMP_STRIDE, CMP_STRIDE * 2 * KV_WIDTH),
                                   lambda i: (i // tiles_per_b, i % tiles_per_b, 0))]
        scratch = [pltpu.VMEM((2, tm, KV_WIDTH), F32)]
    else:
        out_shape += [jax.ShapeDtypeStruct((n, 4 * KV_WIDTH), F32)]
        out_specs += [tok(4 * KV_WIDTH)]
    return pl.pallas_call(
        functools.partial(_inproj_kernel, seq_layout=seq_layout),
        grid=(n // tm,),
        in_specs=[tok(d), mod_spec(0), mod_spec(1), full(norm_w), full(wp), full(qw), full(kw), full(gq), full(gk),
                  rope_spec, rope_spec, rope_spec],
        out_specs=tuple(out_specs),
        out_shape=tuple(out_shape),
        scratch_shapes=scratch,
        compiler_params=_cparams(("arbitrary",)),
        name="inproj",
    )(xf, mod3, mod3, norm_w, wp, qw, kw, gq, gk, c, su, sd)


def _prep_compress(cmp_pe, cmp_w1, cmp_w2):
    span = CMP_STRIDE * HEAD_DIM
    w1p = jnp.concatenate([cmp_w1[:, :span], cmp_w1[:, span:]], axis=2).astype(BF16)
    pep = cmp_pe.reshape(2, 2, span)
    eye = jnp.eye(N_KV_HEADS, dtype=F32)
    w2p = jnp.einsum("poe,hg->phoge", cmp_w2, eye).reshape(2, N_KV_HEADS, CMP_HIDDEN, KV_WIDTH).astype(BF16)
    return w1p, pep, w2p


def _compress_kernel(x_ref, w1_ref, pe_ref, w2_ref, o_ref):
    part = pl.program_id(1)
    nb = x_ref.shape[1]
    span = CMP_STRIDE * HEAD_DIM
    pe = pe_ref[0]
    out = jnp.zeros((nb, KV_WIDTH), F32)
    for h in range(N_KV_HEADS):
        xk = x_ref[0, :, h * span:(h + 1) * span]
        xv = x_ref[0, :, (N_KV_HEADS + h) * span:(N_KV_HEADS + h + 1) * span]
        x = jnp.where(part == 0, xk, xv).astype(F32)
        u = _dot((x + pe[0:1]).astype(BF16), w1_ref[0, :, :CMP_HIDDEN])
        v = _dot((x + pe[1:2]).astype(BF16), w1_ref[0, :, CMP_HIDDEN:])
        h1 = u + pltpu.roll(v, nb - 1, axis=0)
        out = out + _dot(_silu(h1).astype(BF16), w2_ref[0, h])
    row = lax.broadcasted_iota(jnp.int32, out.shape, 0)
    o_ref[0, 0] = jnp.where(row < nb - 1, out, 0.0).astype(o_ref.dtype)


def compress(x, w1p, pep, w2p):
    b, nb, width = x.shape
    return pl.pallas_call(
        _compress_kernel,
        grid=(b, 2),
        in_specs=[
            pl.BlockSpec((1, nb, width), lambda i, p: (i, 0, 0)),
            pl.BlockSpec((1,) + w1p.shape[1:], lambda i, p: (p, 0, 0)),
            pl.BlockSpec((1,) + pep.shape[1:], lambda i, p: (p, 0, 0)),
            pl.BlockSpec((1,) + w2p.shape[1:], lambda i, p: (p, 0, 0, 0)),
        ],
        out_specs=pl.BlockSpec((1, 1, nb, KV_WIDTH), lambda i, p: (i, p, 0, 0)),
        out_shape=jax.ShapeDtypeStruct((b, 2, nb, KV_WIDTH), BF16),
        compiler_params=_cparams(("arbitrary", "arbitrary")),
        name="compress",
    )(x, w1p, pep, w2p)


N_SEL_LANES = LANES


def _cover_matrix(nb):
    c = np.arange(nb)[:, None]
    j = np.arange(N_SEL_LANES)[None, :]
    start = c * CMP_STRIDE
    m = (start < (j + 1) * SEL_BLOCK) & (start + CMP_LEN > j * SEL_BLOCK)
    return jnp.asarray(m.astype(np.float32), BF16)


def _place_heads(res, kv):
    lane = lax.broadcasted_iota(jnp.int32, res[0].shape, 1)
    lo = lane < HEAD_DIM
    blocks = []
    for pair in range(GQA_GROUP // 2):
        a, b = res[2 * pair], res[2 * pair + 1]
        if kv == 0:
            blocks.append(jnp.where(lo, a, pltpu.roll(b, HEAD_DIM, axis=1)))
        else:
            blocks.append(jnp.where(lo, pltpu.roll(a, HEAD_DIM, axis=1), b))
    return jnp.concatenate(blocks, axis=1)


def _group_rows(q_ref, kv):
    heads = range(kv * GQA_GROUP, (kv + 1) * GQA_GROUP)
    return jnp.concatenate([q_ref[0, :, hd * LANES:(hd + 1) * LANES] for hd in heads], axis=0)


def _heads_from_transposed(out_t, tq, kv):
    out = jnp.transpose(out_t)
    return _place_heads([out[g * tq:(g + 1) * tq] for g in range(GQA_GROUP)], kv)


def _cmp_select_kernel(q_ref, kc_ref, vc_ref, covt_ref, o_ref, m_ref, *, q_off):
    tq = q_ref.shape[1]
    rows = GQA_GROUP * tq
    nb = kc_ref.shape[2]
    wl = max(tq, LANES)
    assert tq % LANES == 0 or rows == LANES
    t0 = q_off + pl.program_id(1) * tq
    kc = kc_ref[0, 0]
    vc = vc_ref[0, 0]
    qpos = t0 + lax.broadcasted_iota(jnp.int32, (nb, rows), 1) % tq
    cend = lax.broadcasted_iota(jnp.int32, (nb, rows), 0) * CMP_STRIDE + (CMP_LEN - 1)
    valid = cend <= qpos
    blk = lax.broadcasted_iota(jnp.int32, (N_SEL_LANES, wl), 0)
    cur = (t0 + lax.broadcasted_iota(jnp.int32, (N_SEL_LANES, wl), 1) % tq) // SEL_BLOCK
    forced = (blk == 0) | ((blk <= cur) & (blk > cur - N_LOCAL))
    o_groups = []
    for kv in range(N_KV_HEADS):
        s = _dot_nt(kc, _group_rows(q_ref, kv))
        s = jnp.where(valid, s, NEG)
        e = jnp.exp2(s - jnp.max(s, axis=0, keepdims=True))
        p = e / jnp.sum(e, axis=0, keepdims=True)
        p = jnp.where(valid, p, 0.0)
        o_t = lax.dot_general(vc, p.astype(BF16), (((0,), (0,)), ((), ())), preferred_element_type=F32)
        o_groups.append(_heads_from_transposed(o_t, tq, kv))
        if tq % LANES == 0:
            psum = sum(p[:, g * tq:(g + 1) * tq] for g in range(GQA_GROUP))
        else:
            psum = p + sum(pltpu.roll(p, g * tq, axis=1) for g in range(1, GQA_GROUP))
        hi, lo = _split2(psum)
        imp = _dot(covt_ref[...], hi) + _dot(covt_ref[...], lo)
        x = jnp.where(forced, -jnp.inf, jnp.where(blk > cur, -BIG, imp))
        sel = forced
        for _ in range(N_SEL - 1 - N_LOCAL):
            mx = jnp.max(x, axis=0, keepdims=True)
            idx = jnp.min(jnp.where(x == mx, blk, N_SEL_LANES), axis=0, keepdims=True)
            hit = blk == idx
            sel = sel | hit
            x = jnp.where(hit, -jnp.inf, x)
        mneg = jnp.transpose(jnp.where(sel, 0.0, NEG))
        m_ref[0, kv] = mneg[:tq].astype(m_ref.dtype)
    o_ref[0] = jnp.concatenate(o_groups, axis=1)


def cmp_select(qp, kcv, q_off, tq):
    b, t, _ = qp.shape
    nb = kcv.shape[2]
    cover = jnp.transpose(_cover_matrix(nb))
    return pl.pallas_call(
        functools.partial(_cmp_select_kernel, q_off=q_off),
        grid=(b, t // tq),
        in_specs=[
            pl.BlockSpec((1, tq, N_Q_HEADS * LANES), lambda i, j: (i, j, 0)),
            pl.BlockSpec((1, 1, nb, KV_WIDTH), lambda i, j: (i, 0, 0, 0)),
            pl.BlockSpec((1, 1, nb, KV_WIDTH), lambda i, j: (i, 1, 0, 0)),
            pl.BlockSpec((N_SEL_LANES, nb), lambda i, j: (0, 0)),
        ],
        out_specs=(
            pl.BlockSpec((1, tq, ATTN_WIDTH), lambda i, j: (i, j, 0)),
            pl.BlockSpec((1, N_KV_HEADS, tq, N_SEL_LANES), lambda i, j: (i, 0, j, 0)),
        ),
        out_shape=(
            jax.ShapeDtypeStruct((b, t, ATTN_WIDTH), F32),
            jax.ShapeDtypeStruct((b, N_KV_HEADS, t, N_SEL_LANES), BF16),
        ),
        compiler_params=_cparams(("arbitrary", "arbitrary")),
        name="cmp_select",
    )(qp, kcv, kcv, cover)


SEL_TILE_ELEMS = 512 * 512
SEL_WIDTHS = (8, 4, 2, 1)
WIN_CHUNK = 256
WIN_SPAN = 3


def _block_onehot(s):
    key = np.arange(s)[:, None]
    j = np.arange(N_SEL_LANES)[None, :]
    return jnp.asarray((key // SEL_BLOCK == j).astype(np.float32), BF16)


def _gate_expand():
    m = np.zeros((3, LANES, ATTN_WIDTH), np.float32)
    for br in range(3):
        for hd in range(N_Q_HEADS):
            m[br, SSM_HEADS + 3 * hd + br, hd * HEAD_DIM:(hd + 1) * HEAD_DIM] = 1.0
    return jnp.asarray(m, BF16)


def _flash_update(ss, v, m_ref, acc_ref):
    lane = lax.broadcasted_iota(jnp.int32, v.shape, 1)
    one = jnp.ones(v.shape, v.dtype)
    stage = []
    for k, s in enumerate(ss):
        m_old = m_ref[k]
        m_new = jnp.maximum(m_old, jnp.max(s, axis=0, keepdims=True))
        alpha = jnp.exp2(m_old - m_new)
        p = jnp.exp2(s - m_new)
        m_ref[k] = m_new
        stage.append((alpha, p.astype(BF16)))
    for k, (alpha, p) in enumerate(stage):
        vk = jnp.where((lane < HEAD_DIM) == (k == 0), v, one)
        pv = lax.dot_general(vk, p, (((0,), (0,)), ((), ())), preferred_element_type=F32)
        acc_ref[k] = alpha * acc_ref[k] + pv


def _sel_chunk(rows, n_keys):
    chunk = SEL_TILE_ELEMS // rows
    while n_keys % chunk:
        chunk //= 2
    return chunk


def _sel_win_kernel(q_ref, mneg_ref, ksel_ref, vsel_ref, et_ref, kwin_ref, vwin_ref, ocmp_ref, misc_ref, eg_ref,
                    o_ref, lhs_ref, m_ref, acc_ref, *, q_off, win_pos0):
    tq = q_ref.shape[1]
    rows = GQA_GROUP * tq
    SEL_CHUNK = _sel_chunk(rows, ksel_ref.shape[1])
    t0 = q_off + pl.program_id(1) * tq
    n_sel = lax.shift_right_logical(t0 + tq - 1, int(math.log2(SEL_CHUNK))) + 1
    w_lo = jnp.maximum(t0 - (WINDOW - 1) - win_pos0, 0) // WIN_CHUNK
    w_hi = (t0 + tq - 1 - win_pos0) // WIN_CHUNK + 1

    def qrow(n_keys):
        return lax.broadcasted_iota(jnp.int32, (n_keys, rows), 1) % tq + t0

    def init():
        m_ref[...] = jnp.full(m_ref.shape, NEG, F32)
        acc_ref[...] = jnp.zeros(acc_ref.shape, F32)

    def finish():
        outs = []
        for kv in range(N_KV_HEADS):
            acc = acc_ref[kv]
            denom_row = HEAD_DIM * (1 - kv)
            outs.append(_heads_from_transposed(acc / acc[denom_row:denom_row + 1, :], tq, kv))
        return jnp.concatenate(outs, axis=1)

    for kv in range(N_KV_HEADS):
        for g in range(GQA_GROUP):
            hd = kv * GQA_GROUP + g
            lhs_ref[kv, g * tq:(g + 1) * tq, :LANES] = q_ref[0, :, hd * LANES:(hd + 1) * LANES]
            lhs_ref[kv, g * tq:(g + 1) * tq, LANES:] = mneg_ref[0, kv]

    init()

    def sel_step(i, carry, causal, width):
        n_keys = width * SEL_CHUNK
        r0 = pl.multiple_of(i * n_keys, n_keys)
        rhs = jnp.concatenate([ksel_ref[0, pl.ds(r0, n_keys), :], et_ref[pl.ds(r0, n_keys), :]], axis=1)
        v = vsel_ref[0, pl.ds(r0, n_keys), :]
        if causal:
            ok = r0 + lax.broadcasted_iota(jnp.int32, (n_keys, rows), 0) <= qrow(n_keys)
        ss = [_dot_nt(rhs, lhs_ref[kv]) for kv in range(N_KV_HEADS)]
        if causal:
            ss = [jnp.where(ok, s, NEG) for s in ss]
        _flash_update(ss, v, m_ref, acc_ref)
        return carry

    n_full = lax.shift_right_logical(t0 + 1, int(math.log2(SEL_CHUNK)))
    done = 0
    for width in SEL_WIDTHS:
        n_steps = (n_full - done) // width
        lax.fori_loop(done // width, done // width + n_steps,
                      functools.partial(sel_step, causal=False, width=width), 0)
        done = done + n_steps * width
    lax.fori_loop(n_full, n_sel, functools.partial(sel_step, causal=True, width=1), 0)
    o_sel = finish()

    init()

    def win_step(c, carry, width):
        n_keys = width * WIN_CHUNK
        r0 = pl.multiple_of(c * WIN_CHUNK, WIN_CHUNK)
        k = kwin_ref[0, pl.ds(r0, n_keys), :]
        v = vwin_ref[0, pl.ds(r0, n_keys), :]
        wpos = win_pos0 + r0 + lax.broadcasted_iota(jnp.int32, (n_keys, rows), 0)
        qr = qrow(n_keys)
        ok = (wpos <= qr) & (wpos > qr - WINDOW)
        ss = [jnp.where(ok, _dot_nt(k, lhs_ref[kv, :, :LANES]), NEG) for kv in range(N_KV_HEADS)]
        _flash_update(ss, v, m_ref, acc_ref)
        return carry

    n_span = (w_hi - w_lo) // WIN_SPAN
    lax.fori_loop(w_lo, w_lo + n_span, functools.partial(win_step, width=WIN_SPAN), 0)
    lax.fori_loop(w_lo + n_span * WIN_SPAN, w_hi, functools.partial(win_step, width=1), 0)
    o_win = finish()

    gates = jax.nn.sigmoid(misc_ref[0])
    ghi = gates.astype(BF16)
    glo = (gates - ghi.astype(F32)).astype(BF16)
    branches = (ocmp_ref[0], o_sel, o_win)
    out = jnp.zeros(branches[0].shape, F32)
    for br in range(3):
        out = out + (_dot(ghi, eg_ref[br]) + _dot(glo, eg_ref[br])) * branches[br]
    o_ref[0] = out


def sel_win_attention(qp, mneg, kvb, sel_col, winb, o_cmp, misc, q_off, win_pos0, tq):
    b, t, _ = qp.shape
    s = kvb.shape[1]
    sw = winb.shape[1]
    et = _block_onehot(s)
    eg = _gate_expand()
    rows = GQA_GROUP * tq
    assert q_off + t <= s and q_off + t - win_pos0 <= sw and sw % WIN_CHUNK == 0
    return pl.pallas_call(
        functools.partial(_sel_win_kernel, q_off=q_off, win_pos0=win_pos0),
        grid=(b, t // tq),
        in_specs=[
            pl.BlockSpec((1, tq, N_Q_HEADS * LANES), lambda i, j: (i, j, 0)),
            pl.BlockSpec((1, N_KV_HEADS, tq, N_SEL_LANES), lambda i, j: (i, 0, j, 0)),
            pl.BlockSpec((1, s, KV_WIDTH), lambda i, j: (i, 0, sel_col)),
            pl.BlockSpec((1, s, KV_WIDTH), lambda i, j: (i, 0, sel_col + 1)),
            pl.BlockSpec((s, N_SEL_LANES), lambda i, j: (0, 0)),
            pl.BlockSpec((1, sw, KV_WIDTH), lambda i, j: (i, 0, 0)),
            pl.BlockSpec((1, sw, KV_WIDTH), lambda i, j: (i, 0, 1)),
            pl.BlockSpec((1, tq, ATTN_WIDTH), lambda i, j: (i, j, 0)),
            pl.BlockSpec((1, tq, LANES), lambda i, j: (i, j, 0)),
            pl.BlockSpec((3, LANES, ATTN_WIDTH), lambda i, j: (0, 0, 0)),
        ],
        out_specs=pl.BlockSpec((1, tq, ATTN_WIDTH), lambda i, j: (i, j, 0)),
        out_shape=jax.ShapeDtypeStruct((b, t, ATTN_WIDTH), F32),
        scratch_shapes=[
            pltpu.VMEM((N_KV_HEADS, rows, 2 * LANES), BF16),
            pltpu.VMEM((N_KV_HEADS, 1, rows), F32),
            pltpu.VMEM((N_KV_HEADS, LANES, rows), F32),
        ],
        compiler_params=_cparams(("arbitrary", "arbitrary")),
        name="sel_win_attention",
    )(qp, mneg, kvb, kvb, et, winb, winb, o_cmp, misc, eg)


CONV_PAD = 8
HEAD_PAIRS = SSM_HEADS // 2


def _split3(x):
    a = x.astype(BF16)
    r = x - a.astype(F32)
    b = r.astype(BF16)
    c = (r - b.astype(F32)).astype(BF16)
    return a, b, c


def _ssd_kernel(xbc_ref, z_ref, misc_ref, conv0_ref, h0_ref, cw_ref, cb_ref, dtb_ref, a_ref, dsk_ref, nw_ref,
                y_ref, hout_ref, cout_ref, xp_ref, h_ref, ms_ref, *, t_valid):
    ch = pl.program_id(1)
    L = SSD_CHUNK
    keep = CONV_WIDTH - 1

    @pl.when(ch == 0)
    def _():
        xp_ref[...] = jnp.zeros(xp_ref.shape, F32)
        xp_ref[CONV_PAD - keep:CONV_PAD, :] = conv0_ref[0]
        h_ref[...] = h0_ref[0]

    xp_ref[CONV_PAD:CONV_PAD + t_valid, :] = xbc_ref[0]
    conv = cb_ref[...]
    for j in range(CONV_WIDTH):
        conv = conv + cw_ref[j:j + 1, :] * xp_ref[CONV_PAD - keep + j:CONV_PAD - keep + j + L, :]
    last = xp_ref[CONV_PAD + t_valid - keep:CONV_PAD + t_valid, :]
    cout_ref[0] = last
    xp_ref[CONV_PAD - keep:CONV_PAD, :] = last
    xc = _silu(conv)

    row = lax.broadcasted_iota(jnp.int32, (L, LANES), 0)
    lane = lax.broadcasted_iota(jnp.int32, (L, LANES), 1)
    if t_valid == L:
        raw = misc_ref[0]
    else:
        ms_ref[...] = jnp.zeros(ms_ref.shape, F32)
        ms_ref[0:t_valid, :] = misc_ref[0]
        raw = ms_ref[...]
    v = raw + dtb_ref[...]
    dt = jnp.maximum(v, 0.0) + jnp.log(1.0 + jnp.exp(-jnp.abs(v)))
    dt = jnp.where((lane < SSM_HEADS) & (row < t_valid), dt, 0.0)
    da = dt * a_ref[...]
    tri = (lax.broadcasted_iota(jnp.int32, (L, L), 1) <= lax.broadcasted_iota(jnp.int32, (L, L), 0))
    trib = tri.astype(BF16)
    acum = sum(_dot(trib, part) for part in _split3(da))
    acum_t = jnp.transpose(acum)
    dt_t = jnp.transpose(dt)
    e_acum = jnp.exp(acum)
    e_last = jnp.exp(acum[L - 1:L, :])
    w_end = jnp.exp(acum[L - 1:L, :] - acum) * dt
    lo = lane < SSM_HEAD_DIM

    ys = []
    for pair in range(HEAD_PAIRS):
        grp = (2 * pair) // (SSM_HEADS // SSM_GROUPS)
        bg = xc[:, SSM_WIDTH + grp * SSM_STATE:SSM_WIDTH + (grp + 1) * SSM_STATE].astype(BF16)
        cg = xc[:, SSM_WIDTH + (SSM_GROUPS + grp) * SSM_STATE:SSM_WIDTH + (SSM_GROUPS + grp + 1) * SSM_STATE].astype(BF16)
        g = _dot_nt(cg, bg)
        xpair = xc[:, pair * LANES:(pair + 1) * LANES]
        y = jnp.zeros((L, LANES), F32)
        for sub in range(2):
            hd = 2 * pair + sub
            seg = acum[:, hd:hd + 1] - acum_t[hd:hd + 1, :]
            m = g * jnp.exp(jnp.where(tri, seg, NEG)) * dt_t[hd:hd + 1, :]
            xm = jnp.where(lo if sub == 0 else ~lo, xpair, 0.0)
            y = y + _dot(m.astype(BF16), xm.astype(BF16))
        col = lambda a: jnp.where(lo, a[:, 2 * pair:2 * pair + 1], a[:, 2 * pair + 1:2 * pair + 2])
        hp = h_ref[pair]
        y = y + _dot_nt(cg, hp.astype(BF16)) * col(e_acum)
        y = y + col(dsk_ref[...]) * xpair
        xw = (xpair * col(w_end)).astype(BF16)
        st = lax.dot_general(xw, bg, (((0,), (0,)), ((), ())), preferred_element_type=F32)
        prow = lax.broadcasted_iota(jnp.int32, (LANES, LANES), 0) < SSM_HEAD_DIM
        dec = jnp.where(prow, e_last[:, 2 * pair:2 * pair + 1], e_last[:, 2 * pair + 1:2 * pair + 2])
        h_ref[pair] = hp * dec + st
        ys.append(y)
    y = jnp.concatenate(ys, axis=1)
    if t_valid != L:
        y = y[:t_valid]
    y = y * _silu(z_ref[0])
    y = y * lax.rsqrt(jnp.mean(y * y, axis=-1, keepdims=True) + EPS) * nw_ref[...]
    y_ref[0] = y

    @pl.when(ch == pl.num_programs(1) - 1)
    def _():
        hout_ref[0] = h_ref[...]


def ssd(xbc, z, misc, conv0, h0, conv_w, conv_b, dt_bias, a_log, d_skip, norm_w):
    b, t, _ = xbc.shape
    L = SSD_CHUNK
    t_valid = L if t % L == 0 else t
    assert t_valid == L or t < L
    n_ch = max(t // L, 1)
    keep = CONV_WIDTH - 1
    pad8 = lambda v: jnp.pad(v.astype(F32), (0, LANES - SSM_HEADS)).reshape(1, LANES)
    dtb = pad8(dt_bias)
    a = pad8(-jnp.exp(a_log.astype(F32)))
    dsk = pad8(d_skip)
    h0p = h0.reshape(b, HEAD_PAIRS, 2 * SSM_HEAD_DIM, SSM_STATE)
    full = lambda arr: pl.BlockSpec(arr.shape, lambda i, c: (0,) * arr.ndim)
    tok = lambda wd: pl.BlockSpec((1, t_valid, wd), lambda i, c: (i, c, 0))
    y, hout, cout = pl.pallas_call(
        functools.partial(_ssd_kernel, t_valid=t_valid),
        grid=(b, n_ch),
        in_specs=[
            tok(CONV_DIM), tok(SSM_WIDTH), tok(LANES),
            pl.BlockSpec((1, keep, CONV_DIM), lambda i, c: (i, 0, 0)),
            pl.BlockSpec((1, HEAD_PAIRS, 2 * SSM_HEAD_DIM, SSM_STATE), lambda i, c: (i, 0, 0, 0)),
            full(conv_w), pl.BlockSpec((1, CONV_DIM), lambda i, c: (0, 0)),
            full(dtb), full(a), full(dsk), pl.BlockSpec((1, SSM_WIDTH), lambda i, c: (0, 0)),
        ],
        out_specs=(
            tok(SSM_WIDTH),
            pl.BlockSpec((1, HEAD_PAIRS, 2 * SSM_HEAD_DIM, SSM_STATE), lambda i, c: (i, 0, 0, 0)),
            pl.BlockSpec((1, keep, CONV_DIM), lambda i, c: (i, 0, 0)),
        ),
        out_shape=(
            jax.ShapeDtypeStruct((b, t, SSM_WIDTH), F32),
            jax.ShapeDtypeStruct((b, HEAD_PAIRS, 2 * SSM_HEAD_DIM, SSM_STATE), F32),
            jax.ShapeDtypeStruct((b, keep, CONV_DIM), F32),
        ),
        scratch_shapes=[
            pltpu.VMEM((CONV_PAD + L, CONV_DIM), F32),
            pltpu.VMEM((HEAD_PAIRS, 2 * SSM_HEAD_DIM, SSM_STATE), F32),
            pltpu.VMEM((L, LANES), F32),
        ],
        compiler_params=_cparams(("arbitrary", "arbitrary")),
        name="ssd",
    )(xbc, z, misc, conv0, h0p, conv_w, conv_b.reshape(1, CONV_DIM), dtb, a, dsk, norm_w.reshape(1, SSM_WIDTH))
    return y, hout.reshape(b, SSM_HEADS, SSM_HEAD_DIM, SSM_STATE), cout


def _split2(x):
    hi = x.astype(BF16)
    return hi, (x - hi.astype(F32)).astype(BF16)


def _merge_kernel(oa_ref, ys_ref, x_ref, g1_ref, sh2_ref, sc2_ref, anw_ref, wo_ref, n2w_ref, wrh_ref, wrl_ref,
                  x1_ref, h2_ref, lg_ref):
    oa = oa_ref[...]
    a = oa * lax.rsqrt(jnp.mean(oa * oa, axis=-1, keepdims=True) + EPS) * anw_ref[...]
    cat = jnp.concatenate([a.astype(BF16), ys_ref[...].astype(BF16)], axis=1)
    x1 = x_ref[...] + _mod(g1_ref) * _dot(cat, wo_ref[...])
    x1_ref[...] = x1
    h2 = x1 * lax.rsqrt(jnp.mean(x1 * x1, axis=-1, keepdims=True) + EPS) * n2w_ref[...]
    h2 = h2 * (1.0 + _mod(sc2_ref)) + _mod(sh2_ref)
    h2_ref[...] = h2.astype(BF16)
    hh, hl = _split2(h2)
    lg_ref[...] = _dot_nt(wrh_ref[...], hh) + _dot_nt(wrh_ref[...], hl) + _dot_nt(wrl_ref[...], hh)


def merge(o_attn, y_ssm, x, mod3, mod_row0, attn_norm_w, wo, norm2_w, w_router, tm):
    b, t, d = x.shape
    n = b * t
    tiles_per_b = t // tm
    wrt = jnp.transpose(w_router)
    wrh, wrl = _split2(wrt)

    def mod_spec(col):
        return _mod_spec(mod3, col, tm, tiles_per_b, mod_row0)

    tok = lambda wd: pl.BlockSpec((tm, wd), lambda i: (i, 0))
    full = lambda a: pl.BlockSpec(a.shape, lambda i: (0,) * a.ndim)
    return pl.pallas_call(
        _merge_kernel,
        grid=(n // tm,),
        in_specs=[tok(ATTN_WIDTH), tok(SSM_WIDTH), tok(d), mod_spec(2), mod_spec(3), mod_spec(4),
                  full(attn_norm_w), full(wo), full(norm2_w), full(wrh), full(wrl)],
        out_specs=(tok(d), tok(d), pl.BlockSpec((N_EXPERTS, tm), lambda i: (0, i))),
        out_shape=(jax.ShapeDtypeStruct((n, d), F32), jax.ShapeDtypeStruct((n, d), BF16),
                   jax.ShapeDtypeStruct((N_EXPERTS, n), F32)),
        compiler_params=_cparams(("arbitrary",)),
        name="merge",
    )(o_attn.reshape(n, ATTN_WIDTH), y_ssm.reshape(n, SSM_WIDTH), x.reshape(n, d), mod3, mod3, mod3,
      attn_norm_w, wo, norm2_w, wrh, wrl)


EXPERTS_PER_GROUP = N_EXPERTS // N_EXPERT_GROUPS


def _first_max(x, ids, axes, n_ids):
    mx = jnp.max(x, axis=axes, keepdims=True)
    return ids == jnp.min(jnp.where(x == mx, ids, n_ids), axis=axes, keepdims=True), mx


def _route_kernel(lg_ref, eb_ref, tri_ref, w_ref, pos_ref, cnt_ref):
    lg = lg_ref[...]
    tn = lg.shape[2]
    scores = jax.nn.sigmoid(lg)
    biased = scores + eb_ref[...]
    sub = lax.broadcasted_iota(jnp.int32, lg.shape, 1)
    grp = lax.broadcasted_iota(jnp.int32, (N_EXPERT_GROUPS, 1, tn), 0)
    eid = lax.broadcasted_iota(jnp.int32, lg.shape, 0) * EXPERTS_PER_GROUP + sub
    hit, m1 = _first_max(biased, sub, 1, EXPERTS_PER_GROUP)
    m2 = jnp.max(jnp.where(hit, -jnp.inf, biased), axis=1, keepdims=True)
    gs = m1 + m2
    keep = jnp.zeros(gs.shape, jnp.bool_)
    for _ in range(TOPK_GROUPS):
        hit, _m = _first_max(gs, grp, 0, N_EXPERT_GROUPS)
        keep = keep | hit
        gs = jnp.where(hit, -jnp.inf, gs)
    x = jnp.where(keep, biased, NEG)
    sel = jnp.zeros(lg.shape, jnp.bool_)
    for _ in range(TOP_K):
        hit, _m = _first_max(x, eid, (0, 1), N_EXPERTS)
        sel = sel | hit
        x = jnp.where(hit, -jnp.inf, x)
    w = jnp.where(sel, scores, 0.0)
    w = w / jnp.sum(w, axis=(0, 1), keepdims=True) * ROUTED_SCALE
    w_ref[...] = w
    selb = sel.astype(BF16).reshape(N_EXPERTS, tn)
    pos = _dot(selb, tri_ref[...])
    pos_ref[...] = jnp.where(sel, pos.reshape(lg.shape), -1.0)
    cnt = jnp.sum(sel.astype(F32), axis=2, keepdims=True)
    cnt_ref[0] = jnp.broadcast_to(cnt, cnt_ref.shape[1:]).astype(jnp.int32)


def route(logits_t, e_bias, tn):
    n = logits_t.shape[1]
    lg3 = logits_t.reshape(N_EXPERT_GROUPS, EXPERTS_PER_GROUP, n)
    eb = e_bias.astype(F32).reshape(N_EXPERT_GROUPS, EXPERTS_PER_GROUP, 1)
    tri = jnp.asarray(np.triu(np.ones((tn, tn), np.float32), 1), BF16)
    blk = pl.BlockSpec((N_EXPERT_GROUPS, EXPERTS_PER_GROUP, tn), lambda i: (0, 0, i))
    w, pos, cnt = pl.pallas_call(
        _route_kernel,
        grid=(n // tn,),
        in_specs=[blk, pl.BlockSpec(eb.shape, lambda i: (0, 0, 0)), pl.BlockSpec((tn, tn), lambda i: (0, 0))],
        out_specs=(blk, blk, pl.BlockSpec((1, N_EXPERT_GROUPS, EXPERTS_PER_GROUP, LANES), lambda i: (i, 0, 0, 0))),
        out_shape=(jax.ShapeDtypeStruct(lg3.shape, F32), jax.ShapeDtypeStruct(lg3.shape, F32),
                   jax.ShapeDtypeStruct((n // tn, N_EXPERT_GROUPS, EXPERTS_PER_GROUP, LANES), jnp.int32)),
        compiler_params=_cparams(("arbitrary",)),
        name="route",
    )(lg3, eb, tri)
    return w.reshape(N_EXPERTS, n), pos.reshape(N_EXPERTS, n), cnt[..., 0].reshape(n // tn, N_EXPERTS)


MOE_ROWS = 128


def _swiglu(xb, wgu, wd, width):
    gu = _dot(xb, wgu)
    act = _silu(gu[:, :width]) * gu[:, width:]
    return _dot(act.astype(BF16), wd)


MOE_EXPERTS_PER_STEP = 4


MOE_ALIGN = 16
MOE_GATHER_ROWS = 896


def _moe_slots(tm):
    worst = TOP_K * tm + N_EXPERTS * (MOE_ALIGN - 1) + MOE_ROWS
    return -(-worst // MOE_GATHER_ROWS) * MOE_GATHER_ROWS


def _moe_kernel(cnt_ref, start_ref, h2_ref, w_ref, pos_ref, x1_ref, g2_ref, wgu_ref, wd_ref, sgu_ref, sd_ref,
                o_ref, g_all, xs):
    i = pl.program_id(0)
    es = pl.program_id(1)
    tm = h2_ref.shape[0]
    slots = g_all.shape[0]
    slot = lax.broadcasted_iota(jnp.int32, (MOE_ROWS, tm), 0).astype(F32)
    row = lax.broadcasted_iota(jnp.int32, (MOE_ROWS, 1), 0)

    def n_windows(cnt):
        return (cnt + MOE_ROWS - 1) // MOE_ROWS

    def window_start(e, j):
        return pl.multiple_of(start_ref[i * N_EXPERTS + e] + j * MOE_ROWS, MOE_ALIGN)

    @pl.when(es == 0)
    def _():
        g_all[...] = jnp.zeros(g_all.shape, BF16)

        def mark(e, carry):
            pos = pos_ref[pl.ds(e, 1), :]

            def mark_window(j, carry):
                hit = pos == slot + (j * MOE_ROWS).astype(F32)
                g_all[pl.ds(window_start(e, j), MOE_ROWS), :] = hit.astype(BF16)
                return carry

            return lax.fori_loop(0, n_windows(cnt_ref[i * N_EXPERTS + e]), mark_window, carry)

        lax.fori_loop(0, N_EXPERTS, mark, 0)

        def gather(c, carry):
            r0 = pl.multiple_of(c * MOE_GATHER_ROWS, MOE_GATHER_ROWS)
            rows = _dot(g_all[pl.ds(r0, MOE_GATHER_ROWS), :], h2_ref[...])
            xs[pl.ds(r0, MOE_GATHER_ROWS), :] = rows.astype(BF16)
            return carry

        lax.fori_loop(0, slots // MOE_GATHER_ROWS, gather, 0)

    for q in range(MOE_EXPERTS_PER_STEP):
        e = es * MOE_EXPERTS_PER_STEP + q
        cnt = cnt_ref[i * N_EXPERTS + e]
        wrow = w_ref[pl.ds(e, 1), :]

        def window(j, carry, q=q, e=e, cnt=cnt, wrow=wrow):
            r0 = window_start(e, j)
            xg = xs[pl.ds(r0, MOE_ROWS), :]
            out = _swiglu(xg, wgu_ref[q].astype(BF16), wd_ref[q].astype(BF16), D_EXPERT)
            g = g_all[pl.ds(r0, MOE_ROWS), :].astype(F32)
            out = out * jnp.sum(g * wrow, axis=1, keepdims=True)
            mine = row < cnt - j * MOE_ROWS
            xs[pl.ds(r0, MOE_ROWS), :] = jnp.where(mine, out.astype(BF16), xg)
            return carry

        lax.fori_loop(0, n_windows(cnt), window, 0)

    @pl.when(es == pl.num_programs(1) - 1)
    def _():
        y = lax.dot_general(g_all[...], xs[...], (((0,), (0,)), ((), ())), preferred_element_type=F32)
        y = y + _swiglu(h2_ref[...], sgu_ref[...], sd_ref[...], D_SHARED)
        o_ref[...] = x1_ref[...] + _mod(g2_ref) * y


def moe(h2, w_t, pos_t, counts, x1, mod3, mod_row0, t_per_b, wgu, wd, sgu, sd, tm):
    n, d = h2.shape
    tiles_per_b = t_per_b // tm
    eps = MOE_EXPERTS_PER_STEP
    slots = _moe_slots(tm)
    padded = (counts + MOE_ALIGN - 1) // MOE_ALIGN * MOE_ALIGN
    starts = jnp.cumsum(padded, axis=1) - padded
    grid_spec = pltpu.PrefetchScalarGridSpec(
        num_scalar_prefetch=2,
        grid=(n // tm, N_EXPERTS // eps),
        in_specs=[
            pl.BlockSpec((tm, d), lambda i, e, *_: (i, 0)),
            pl.BlockSpec((N_EXPERTS, tm), lambda i, e, *_: (0, i)),
            pl.BlockSpec((N_EXPERTS, tm), lambda i, e, *_: (0, i)),
            pl.BlockSpec((tm, d), lambda i, e, *_: (i, 0)),
            _mod_spec(mod3, 5, tm, tiles_per_b, mod_row0),
            pl.BlockSpec((eps, d, 2 * D_EXPERT), lambda i, e, *_: (e, 0, 0)),
            pl.BlockSpec((eps, D_EXPERT, d), lambda i, e, *_: (e, 0, 0)),
            pl.BlockSpec(sgu.shape, lambda i, e, *_: (0, 0)),
            pl.BlockSpec(sd.shape, lambda i, e, *_: (0, 0)),
        ],
        out_specs=pl.BlockSpec((tm, d), lambda i, e, *_: (i, 0)),
        scratch_shapes=[pltpu.VMEM((slots, tm), BF16), pltpu.VMEM((slots, d), BF16)],
    )
    return pl.pallas_call(
        _moe_kernel,
        grid_spec=grid_spec,
        out_shape=jax.ShapeDtypeStruct((n, d), F32),
        compiler_params=_cparams(("arbitrary", "arbitrary")),
        name="moe",
    )(counts.reshape(-1), starts.reshape(-1).astype(jnp.int32), h2, w_t, pos_t, x1, mod3, wgu, wd, sgu, sd)


SC_WINDOW = 128
PACK_W = 256
MOE_BLOCK_ROWS = 1024
HI_MASK = -65536


def _pack_pair(x):
    bits = pltpu.bitcast(x.astype(BF16).astype(F32), jnp.int32)
    return lax.shift_right_logical(bits[:, :PACK_W], 16) | (bits[:, PACK_W:] & HI_MASK)


def _unpack_pair(word):
    lo = pltpu.bitcast(lax.shift_left(word, 16), F32)
    hi = pltpu.bitcast(word & HI_MASK, F32)
    return jnp.concatenate([lo, hi], axis=1)


def _pack_kernel(x_ref, a_ref, b_ref):
    x = x_ref[...]
    a_ref[...] = _pack_pair(x[:, :2 * PACK_W])
    b_ref[...] = _pack_pair(x[:, 2 * PACK_W:])


def pack_rows(x, tm):
    n, d = x.shape
    tok = lambda wd: pl.BlockSpec((tm, wd), lambda i: (i, 0))
    return pl.pallas_call(
        _pack_kernel, grid=(n // tm,), in_specs=[tok(d)], out_specs=(tok(PACK_W), tok(PACK_W)),
        out_shape=(jax.ShapeDtypeStruct((n, PACK_W), jnp.int32),) * 2,
        compiler_params=_cparams(("arbitrary",)), name="pack_rows",
    )(x)


def _slots_kernel(w_ref, pos_ref, base_ref, tri_ref, slot_ref, wt_ref):
    w = w_ref[...]
    pos = pos_ref[...]
    sel = pos >= 0.0
    rank = _dot(tri_ref[...], sel.astype(BF16))
    dest = base_ref[0] + pos
    slots, wts = [], []
    for j in range(TOP_K):
        mine = sel & (rank == float(j))
        slots.append(jnp.sum(jnp.where(mine, dest, 0.0), axis=0, keepdims=True))
        wts.append(jnp.sum(jnp.where(mine, w, 0.0), axis=0, keepdims=True))
    slot_ref[...] = jnp.concatenate(slots, axis=0).astype(jnp.int32)
    wpad = jnp.concatenate(wts + [jnp.zeros((LANES - TOP_K, w.shape[1]), F32)], axis=0)
    wt_ref[...] = jnp.transpose(wpad)


def slots_of(w_t, pos_t, base, tn):
    n = w_t.shape[1]
    tri = jnp.asarray(np.tril(np.ones((N_EXPERTS, N_EXPERTS), np.float32), -1), BF16)
    blk = pl.BlockSpec((N_EXPERTS, tn), lambda i: (0, i))
    return pl.pallas_call(
        _slots_kernel, grid=(n // tn,),
        in_specs=[blk, blk, pl.BlockSpec((1, N_EXPERTS, 1), lambda i: (i, 0, 0)),
                  pl.BlockSpec((N_EXPERTS, N_EXPERTS), lambda i: (0, 0))],
        out_specs=(pl.BlockSpec((TOP_K, tn), lambda i: (0, i)), pl.BlockSpec((tn, LANES), lambda i: (i, 0))),
        out_shape=(jax.ShapeDtypeStruct((TOP_K, n), jnp.int32), jax.ShapeDtypeStruct((n, LANES), F32)),
        compiler_params=_cparams(("arbitrary",)), name="moe_slots",
    )(w_t, pos_t, base, tri)


def sc_scatter_rows(rows, idx, n_out):
    n, d = rows.shape
    m = idx.shape[0]
    nb = n // SC_WINDOW
    mesh = plsc.VectorSubcoreMesh(core_axis_name="core", subcore_axis_name="subcore")

    @functools.partial(pl.kernel, out_type=jax.ShapeDtypeStruct((n_out, d), rows.dtype), mesh=mesh)
    def scatter(x_hbm, i_hbm, o_hbm):
        def body(x_vmem, i_vmem):
            pltpu.sync_copy(x_vmem, o_hbm.at[i_vmem.at[0]])

        pltpu.emit_pipeline(
            body, grid=(m // SC_WINDOW,),
            in_specs=[pl.BlockSpec((SC_WINDOW, d), index_map=lambda i: (i % nb, 0)),
                      pl.BlockSpec((1, SC_WINDOW), index_map=lambda i: (0, i))],
            out_specs=[], core_axis_name=("core", "subcore"), dimension_semantics=(pltpu.PARALLEL,),
        )(x_hbm, i_hbm)

    return scatter(rows, idx.reshape(1, m))


def sc_gather_rows(table, idx):
    d = table.shape[1]
    m = idx.shape[0]
    mesh = plsc.VectorSubcoreMesh(core_axis_name="core", subcore_axis_name="subcore")

    @functools.partial(pl.kernel, out_type=jax.ShapeDtypeStruct((m, d), table.dtype), mesh=mesh)
    def gather(x_hbm, i_hbm, o_hbm):
        def body(i_vmem, o_vmem):
            pltpu.sync_copy(x_hbm.at[i_vmem.at[0]], o_vmem)

        pltpu.emit_pipeline(
            body, grid=(m // SC_WINDOW,),
            in_specs=[pl.BlockSpec((1, SC_WINDOW), index_map=lambda i: (0, i))],
            out_specs=[pl.BlockSpec((SC_WINDOW, d), index_map=lambda i: (i, 0))],
            core_axis_name=("core", "subcore"), dimension_semantics=(pltpu.PARALLEL,),
        )(i_hbm, o_hbm)

    return gather(table, idx.reshape(1, m))


def _experts_kernel(be_ref, nu_ref, xa_ref, xb_ref, wgu_ref, wd_ref, oa_ref, ob_ref, wgu_bf, wd_bf):
    b = pl.program_id(0)

    @pl.when(b < nu_ref[0])
    def _():
        @pl.when((b == 0) | (be_ref[b] != be_ref[jnp.maximum(b - 1, 0)]))
        def _():
            wgu_bf[...] = wgu_ref[0].astype(BF16)
            wd_bf[...] = wd_ref[0].astype(BF16)

        x = jnp.concatenate([_unpack_pair(xa_ref[...]), _unpack_pair(xb_ref[...])], axis=1).astype(BF16)
        out = _swiglu(x, wgu_bf[...], wd_bf[...], D_EXPERT)
        oa_ref[...] = _pack_pair(out[:, :2 * PACK_W])
        ob_ref[...] = _pack_pair(out[:, 2 * PACK_W:])


def experts_sorted(xa, xb, block_expert, n_used, wgu, wd):
    r = xa.shape[0]
    d = wd.shape[2]
    row = lambda b, be, nu: (jnp.minimum(b, nu[0] - 1), 0)
    blk = pl.BlockSpec((MOE_BLOCK_ROWS, PACK_W), row)
    grid_spec = pltpu.PrefetchScalarGridSpec(
        num_scalar_prefetch=2, grid=(r // MOE_BLOCK_ROWS,),
        in_specs=[blk, blk,
                  pl.BlockSpec((1, d, 2 * D_EXPERT), lambda b, be, nu: (be[b], 0, 0)),
                  pl.BlockSpec((1, D_EXPERT, d), lambda b, be, nu: (be[b], 0, 0))],
        out_specs=(blk, blk),
        scratch_shapes=[pltpu.VMEM((d, 2 * D_EXPERT), BF16), pltpu.VMEM((D_EXPERT, d), BF16)],
    )
    return pl.pallas_call(
        _experts_kernel, grid_spec=grid_spec,
        out_shape=(jax.ShapeDtypeStruct((r, PACK_W), jnp.int32),) * 2,
        compiler_params=_cparams(("arbitrary",)), name="moe_experts",
    )(block_expert, n_used, xa, xb, wgu, wd)


def _combine_kernel(ya_ref, yb_ref, wt_ref, h2_ref, x1_ref, g2_ref, sgu_ref, sd_ref, o_ref):
    wt = wt_ref[...]
    acc = _swiglu(h2_ref[...], sgu_ref[...], sd_ref[...], D_SHARED)
    for j in range(TOP_K):
        y = jnp.concatenate([_unpack_pair(ya_ref[j]), _unpack_pair(yb_ref[j])], axis=1)
        acc = acc + wt[:, j:j + 1] * y
    o_ref[...] = x1_ref[...] + _mod(g2_ref) * acc


def combine_sorted(ya, yb, wt, h2, x1, mod3, mod_row0, t_per_b, sgu, sd, tm):
    n, d = h2.shape
    tiles_per_b = t_per_b // tm
    tok = lambda wd: pl.BlockSpec((tm, wd), lambda i: (i, 0))
    yblk = pl.BlockSpec((TOP_K, tm, PACK_W), lambda i: (0, i, 0))
    full = lambda a: pl.BlockSpec(a.shape, lambda i: (0,) * a.ndim)
    return pl.pallas_call(
        _combine_kernel, grid=(n // tm,),
        in_specs=[yblk, yblk, tok(LANES), tok(d), tok(d), _mod_spec(mod3, 5, tm, tiles_per_b, mod_row0),
                  full(sgu), full(sd)],
        out_specs=tok(d), out_shape=jax.ShapeDtypeStruct((n, d), F32),
        compiler_params=_cparams(("arbitrary",)), name="moe_combine",
    )(ya, yb, wt, h2, x1, mod3, sgu, sd)


def moe_sorted(h2, w_t, pos_t, counts, x1, mod3, mod_row0, t_per_b, wgu, wd, sgu, sd, tm, overlap):
    n, d = h2.shape
    assert d == 4 * PACK_W and n % SC_WINDOW == 0
    n_blocks = (TOP_K * n + N_EXPERTS * (MOE_BLOCK_ROWS - 1)) // MOE_BLOCK_ROWS
    total = jnp.sum(counts, axis=0)
    region = (total + MOE_BLOCK_ROWS - 1) // MOE_BLOCK_ROWS * MOE_BLOCK_ROWS
    region_end = jnp.cumsum(region)
    base = (region_end - region)[None, :] + jnp.cumsum(counts, axis=0) - counts
    block_row0 = jnp.arange(n_blocks, dtype=region_end.dtype) * MOE_BLOCK_ROWS
    block_expert = jnp.sum(region_end[None, :] <= block_row0[:, None], axis=1)
    block_expert = jnp.minimum(block_expert, N_EXPERTS - 1).astype(jnp.int32)
    n_used = (region_end[-1:] // MOE_BLOCK_ROWS).astype(jnp.int32)
    slot, wt = slots_of(w_t, pos_t, base.astype(F32).reshape(-1, N_EXPERTS, 1), tm)
    dest = slot.reshape(-1)
    ha, hb = pack_rows(h2, tm)
    rows = n_blocks * MOE_BLOCK_ROWS
    xa, xb = sc_scatter_rows(ha, dest, rows), sc_scatter_rows(hb, dest, rows)
    n_used, rest = lax.optimization_barrier((n_used, overlap()))
    oa, ob = experts_sorted(xa, xb, block_expert, n_used, wgu, wd)
    ya = sc_gather_rows(oa, dest).reshape(TOP_K, n, PACK_W)
    yb = sc_gather_rows(ob, dest).reshape(TOP_K, n, PACK_W)
    return combine_sorted(ya, yb, wt, h2, x1, mod3, mod_row0, t_per_b, sgu, sd, tm), rest


GATHER_PAGES = 8


def _gather_kernel(pt_ref, *refs):
    pages, new_ref = refs[:GATHER_PAGES], refs[GATHER_PAGES]
    rows_ref, cmpx_ref, stage_ref = refs[GATHER_PAGES + 1:]
    step = pl.program_id(1)
    last = pl.num_programs(1) - 1
    n_rows = GATHER_PAGES * PAGE_SIZE

    @pl.when(step < last)
    def _():
        for k in range(GATHER_PAGES):
            sl = slice(k * PAGE_SIZE, (k + 1) * PAGE_SIZE)
            for r in range(4):
                tile = jnp.transpose(pages[k][0, r])
                if r < 2:
                    stage_ref[r, sl, :] = tile
                else:
                    rows_ref[0, sl, (r - 2) * KV_WIDTH:(r - 1) * KV_WIDTH] = tile.astype(BF16)

    @pl.when(step == last)
    def _():
        new = new_ref[0]
        tn = new.shape[0]
        stage_ref[...] = jnp.zeros(stage_ref.shape, F32)
        for s in range(2):
            stage_ref[s, 0:tn, :] = new[:, s * KV_WIDTH:(s + 1) * KV_WIDTH]
        pad = jnp.zeros((n_rows - tn, 2 * KV_WIDTH), F32)
        rows_ref[0] = jnp.concatenate([new[:, 2 * KV_WIDTH:], pad], axis=0).astype(BF16)

    _stride_block_store(stage_ref, cmpx_ref, n_rows)


def gather_pages(cache_t, page_table, new_rows):
    b, n_pages = page_table.shape
    steps = n_pages // GATHER_PAGES
    rows = GATHER_PAGES * PAGE_SIZE
    s_out = (steps + 1) * rows

    def page_spec(k):
        def idx(i, s, pt):
            p = jnp.minimum(s, steps - 1) * GATHER_PAGES + k
            return (pt[i * n_pages + p], 0, 0, 0)
        return pl.BlockSpec((1, 4, KV_WIDTH, PAGE_SIZE), idx)

    grid_spec = pltpu.PrefetchScalarGridSpec(
        num_scalar_prefetch=1,
        grid=(b, steps + 1),
        in_specs=[page_spec(k) for k in range(GATHER_PAGES)]
        + [pl.BlockSpec((1,) + new_rows.shape[1:], lambda i, s, pt: (i, 0, 0))],
        out_specs=(
            pl.BlockSpec((1, rows, 2 * KV_WIDTH), lambda i, s, pt: (i, s, 0)),
            pl.BlockSpec((1, rows // CMP_STRIDE, CMP_STRIDE * 2 * KV_WIDTH), lambda i, s, pt: (i, s, 0)),
        ),
        scratch_shapes=[pltpu.VMEM((2, rows, KV_WIDTH), F32)],
    )
    return pl.pallas_call(
        _gather_kernel,
        grid_spec=grid_spec,
        out_shape=(jax.ShapeDtypeStruct((b, s_out, 2 * KV_WIDTH), BF16),
                   jax.ShapeDtypeStruct((b, s_out // CMP_STRIDE, CMP_STRIDE * 2 * KV_WIDTH), BF16)),
        compiler_params=_cparams(("arbitrary", "arbitrary")),
        name="gather_pages",
    )(page_table.reshape(-1), *([cache_t] * GATHER_PAGES), new_rows)


def _attention(qp, cmpx, kvb, sel_col, winb, misc, cmp_w, q_off, win_pos0, tq):
    t = qp.shape[1]
    cur_lo, cur_hi = q_off // SEL_BLOCK, (q_off + t - 1) // SEL_BLOCK
    assert cur_hi < N_SEL_LANES or (cur_lo == cur_hi == N_SEL_LANES), (q_off, t)
    kcv = compress(cmpx, *cmp_w)
    o_cmp, mneg = cmp_select(qp, kcv, q_off, tq)
    return sel_win_attention(qp, mneg, kvb, sel_col, winb, o_cmp, misc, q_off, win_pos0, tq)


def kernel(x_prompt, x_sample, cache_kv, cache_win, state_ssm, state_conv, page_table, c_prompt, c_sample, w_ada, b_ada, norm1_w, norm2_w, w_in, q_norm_w, k_norm_w, cmp_pe, cmp_w1, cmp_w2, attn_out_norm_w, conv_w, conv_b, dt_bias, a_log, d_skip, ssm_norm_w, w_out, w_router, e_bias, w_exp_gu, w_exp_down, w_sh_gu, w_sh_down):
    xp, xq = x_prompt, x_sample
    bp, tp, d = xp.shape
    bq, tq, _ = xq.shape
    depth = w_ada.shape[0]
    past_len = page_table.shape[1] * PAGE_SIZE
    nq = bq * tq
    tq_pad = LANES // GQA_GROUP
    assert tp % TOKEN_TILE == 0 and tp >= WINDOW and nq % 8 == 0 and tq <= tq_pad
    pos_p = jnp.arange(tp, dtype=jnp.int32)
    pos_q = jnp.tile(past_len + jnp.arange(tq, dtype=jnp.int32), bq)
    c_all = jnp.concatenate([c_prompt, c_sample], axis=0)
    c_all = jnp.pad(c_all, ((0, -c_all.shape[0] % 8), (0, 0)))
    outs = [[] for _ in range(8)]
    for l in range(depth):
        mod = adaln_all(c_all, w_ada[l], b_ada[l])
        mod_p = mod.reshape(mod.shape[0], 1, 6 * d)
        mod_q = jnp.repeat(mod[bp:bp + bq], tq, axis=0)
        wp = _prep_w_in(w_in[l])
        cmp_w = _prep_compress(cmp_pe[l], cmp_w1[l], cmp_w2[l])
        wo = w_out[l].astype(BF16)
        wgu, wd = w_exp_gu[l], w_exp_down[l]
        sgu, sd = w_sh_gu[l].astype(BF16), w_sh_down[l].astype(BF16)
        ssm_w = (conv_w[l], conv_b[l], dt_bias[l], a_log[l], d_skip[l], ssm_norm_w[l])
        n1w, n2w, anw = norm1_w[l:l + 1], norm2_w[l:l + 1], attn_out_norm_w[l:l + 1]

        qp, kvb, win, winb, z, xbc, misc, kvt, cmpx = inproj(xp, mod_p, 0, n1w, wp, q_norm_w[l], k_norm_w[l], pos_p,
                                                            TOKEN_TILE, True)
        r3 = lambda a: a.reshape(bp, tp, a.shape[-1])
        o_attn = _attention(r3(qp), cmpx, r3(kvb), 2, r3(winb), r3(misc), cmp_w, 0, 0, QUERY_TILE)
        y_ssm, h_new, conv_new = ssd(r3(xbc), r3(z), r3(misc), jnp.zeros((bp, CONV_WIDTH - 1, CONV_DIM), F32),
                                     jnp.zeros((bp, SSM_HEADS, SSM_HEAD_DIM, SSM_STATE), F32), *ssm_w)
        x1, h2, lg = merge(o_attn, y_ssm, xp, mod_p, 0, anw, wo, n2w, w_router[l], TOKEN_TILE)
        w_t, pos_t, cnt = route(lg, e_bias[l], TOKEN_TILE)
        outs[0].append(jnp.transpose(kvt.reshape(bp, 4, N_KV_HEADS, HEAD_DIM, tp), (0, 4, 1, 2, 3)))
        outs[1].append(win.reshape(bp, tp, 2, N_KV_HEADS, HEAD_DIM)[:, tp - WINDOW:])
        outs[2].append(h_new)
        outs[3].append(conv_new)

        xq1 = xq.reshape(1, nq, d)
        rq = lambda a: a.reshape(bq, tq, a.shape[-1])
        padq = lambda a: jnp.pad(rq(a), ((0, 0), (0, tq_pad - tq), (0, 0)))

        def sample_front():
            proj = inproj(xq1, mod_q, 0, n1w, wp, q_norm_w[l], k_norm_w[l], pos_q, nq, False)
            cache_t = jnp.transpose(cache_kv[l], (0, 2, 3, 4, 1)).reshape(cache_kv.shape[1], 4, KV_WIDTH, PAGE_SIZE)
            past, cmpx = gather_pages(cache_t, page_table, rq(proj[-1]))
            return proj, past, cmpx

        xp, (proj, past, cmpx) = moe_sorted(h2, w_t, pos_t, cnt, x1, mod_p, 0, tp, wgu, wd, sgu, sd, TOKEN_TILE,
                                            sample_front)
        xp = xp.reshape(bp, tp, d)
        qp, kvb, win, winb, z, xbc, misc, kv = proj
        win_all = jnp.concatenate([cache_win[l].reshape(bq, WINDOW, 2 * KV_WIDTH).astype(BF16), rq(winb),
                                   jnp.zeros((bq, -(WINDOW + tq_pad) % WIN_CHUNK + tq_pad - tq, 2 * KV_WIDTH), BF16)],
                                  axis=1)
        o_attn = _attention(padq(qp), cmpx, past, 0, win_all, padq(misc), cmp_w, past_len, past_len - WINDOW,
                            tq_pad)[:, :tq]
        y_ssm, h_new, conv_new = ssd(rq(xbc), rq(z), rq(misc), state_conv[l], state_ssm[l], *ssm_w)
        x1, h2, lg = merge(o_attn.reshape(1, nq, ATTN_WIDTH), y_ssm.reshape(1, nq, SSM_WIDTH), xq1, mod_q, 0,
                           anw, wo, n2w, w_router[l], nq)
        w_t, pos_t, cnt = route(lg, e_bias[l], nq)
        xq = moe(h2, w_t, pos_t, cnt, x1, mod_q, 0, nq, wgu, wd, sgu, sd, nq).reshape(bq, tq, d)
        win_rows = win.reshape(bq, tq, 2, N_KV_HEADS, HEAD_DIM)
        outs[4].append(kv.reshape(bq, tq, 4, N_KV_HEADS, HEAD_DIM))
        outs[5].append(jnp.concatenate([cache_win[l], win_rows.astype(cache_win.dtype)], axis=1)[:, tq:])
        outs[6].append(h_new)
        outs[7].append(conv_new)
    return (xp, xq) + tuple(jnp.stack(o) for o in outs)
```

```python
import functools
import math

import jax
import jax.numpy as jnp
import numpy as np
from jax import lax
from jax.experimental import pallas as pl
from jax.experimental.pallas import tpu as pltpu
from jax.experimental.pallas import tpu_sc as plsc

D_MODEL = 1024
PAGE_SIZE = 128
HEAD_DIM = 64
N_Q_HEADS = 8
N_KV_HEADS = 2
GQA_GROUP = N_Q_HEADS // N_KV_HEADS
ATTN_WIDTH = N_Q_HEADS * HEAD_DIM
KV_WIDTH = N_KV_HEADS * HEAD_DIM
ROPE_DIM = HEAD_DIM // 4
ROPE_THETA = 500000.0
CMP_LEN = 32
CMP_STRIDE = 16
CMP_HIDDEN = 4 * HEAD_DIM
SEL_BLOCK = 64
N_SEL = 16
N_LOCAL = 2
WINDOW = 512
SSM_HEADS = 8
SSM_HEAD_DIM = 64
SSM_WIDTH = SSM_HEADS * SSM_HEAD_DIM
SSM_GROUPS = 2
SSM_STATE = 128
CONV_WIDTH = 4
CONV_DIM = SSM_WIDTH + 2 * SSM_GROUPS * SSM_STATE
SSD_CHUNK = 128
MIX_WIDTH = ATTN_WIDTH + SSM_WIDTH
N_EXPERTS = 64
N_EXPERT_GROUPS = 8
TOPK_GROUPS = 4
TOP_K = 8
D_EXPERT = 256
D_SHARED = 256
ROUTED_SCALE = 2.5
IN_SIZES = (ATTN_WIDTH, 6 * KV_WIDTH, 3 * N_Q_HEADS, SSM_WIDTH, CONV_DIM, SSM_HEADS)
N_IN = sum(IN_SIZES)
EPS = 1e-6
NEG = -1e30
BIG = 1e6

LANES = 128
TOKEN_TILE = 512
QUERY_TILE = 128
VMEM_LIMIT = 56 * 1024 * 1024

BF16 = jnp.bfloat16
F32 = jnp.float32
LOG2E = math.log2(math.e)


def _cparams(sem, flags=None):
    return pltpu.CompilerParams(dimension_semantics=sem, vmem_limit_bytes=VMEM_LIMIT, flags=flags)


def _silu(x):
    return x * jax.nn.sigmoid(x)


def _dot(a, b):
    return jnp.dot(a, b, preferred_element_type=F32)


def _dot_nt(a, b):
    return lax.dot_general(a, b, (((1,), (1,)), ((), ())), preferred_element_type=F32)


def _mod_spec(mod, col, tm, tiles_per_b, row0):
    if mod.ndim == 3:
        return pl.BlockSpec((1, 1, D_MODEL), lambda i, *_: (row0 + i // tiles_per_b, 0, col))
    return pl.BlockSpec((tm, D_MODEL), lambda i, *_: (i, col))


def _mod(ref):
    return ref[0] if len(ref.shape) == 3 else ref[...]


def _adaln_kernel(c_ref, w_ref, b_ref, o_ref):
    c = c_ref[...]
    a = _silu(c).astype(BF16)
    o_ref[...] = _dot(a, w_ref[...].astype(BF16)) + b_ref[...]


def adaln_all(c_all, w_ada, b_ada):
    rows = c_all.shape[0]
    n = w_ada.shape[1]
    tn = 1024
    return pl.pallas_call(
        _adaln_kernel,
        grid=(n // tn,),
        in_specs=[
            pl.BlockSpec((rows, D_MODEL), lambda j: (0, 0)),
            pl.BlockSpec((D_MODEL, tn), lambda j: (0, j)),
            pl.BlockSpec((1, tn), lambda j: (0, j)),
        ],
        out_specs=pl.BlockSpec((rows, tn), lambda j: (0, j)),
        out_shape=jax.ShapeDtypeStruct((rows, n), F32),
        compiler_params=_cparams(("arbitrary",)),
        name="adaln",
    )(c_all, w_ada, b_ada.reshape(1, n))


_C_Q = 0
_C_KV = _C_Q + ATTN_WIDTH
_C_Z = _C_KV + 6 * KV_WIDTH
_C_XBC = _C_Z + SSM_WIDTH
_C_MISC = _C_XBC + CONV_DIM
N_IN_PAD = _C_MISC + LANES
N_GATES = 3 * N_Q_HEADS


def _prep_w_in(w_in):
    s = np.cumsum((0,) + IN_SIZES)
    q, kv, g, z, xbc, dt = (w_in[:, int(s[i]):int(s[i + 1])] for i in range(6))
    pad = jnp.zeros((w_in.shape[0], LANES - N_GATES - SSM_HEADS), w_in.dtype)
    return jnp.concatenate([q, kv, z, xbc, dt, g, pad], axis=1).astype(BF16)


def _group_mean_matrix(width):
    i = np.arange(width)
    m = (i[:, None] // HEAD_DIM == i[None, :] // HEAD_DIM).astype(np.float32) / HEAD_DIM
    return jnp.asarray(m, BF16)


def _rope_tables(pos):
    half = ROPE_DIM // 2
    inv_freq = ROPE_THETA ** (-jnp.arange(half, dtype=F32) / half)
    ang = pos.astype(F32)[:, None] * inv_freq[None, :]
    cos, sin = jnp.cos(ang), jnp.sin(ang)
    t = pos.shape[0]
    one = jnp.ones((t, HEAD_DIM - ROPE_DIM), F32)
    zero = jnp.zeros((t, HEAD_DIM - ROPE_DIM), F32)
    zh = jnp.zeros((t, half), F32)
    c = jnp.concatenate([cos, cos, one], axis=1)
    s_up = jnp.concatenate([-sin, zh, zero], axis=1)
    s_dn = jnp.concatenate([zh, sin, zero], axis=1)
    rep = LANES // HEAD_DIM
    return jnp.tile(c, (1, rep)), jnp.tile(s_up, (1, rep)), jnp.tile(s_dn, (1, rep))


def _rope(x, c, s_up, s_dn):
    w = x.shape[1]
    half = ROPE_DIM // 2
    rep = w // LANES
    ct = jnp.concatenate([c] * rep, axis=1) if rep > 1 else c
    su = jnp.concatenate([s_up] * rep, axis=1) if rep > 1 else s_up
    sd = jnp.concatenate([s_dn] * rep, axis=1) if rep > 1 else s_dn
    up = pltpu.roll(x, w - half, axis=1)
    dn = pltpu.roll(x, half, axis=1)
    return x * ct + up * su + dn * sd


def _stride_block_store(stage_ref, cmpx_ref, n_rows):
    nb = n_rows // CMP_STRIDE
    lane = lax.broadcasted_iota(jnp.int32, (nb, KV_WIDTH), 1)
    lo = lane < HEAD_DIM
    span = CMP_STRIDE * HEAD_DIM
    for s in range(2):
        for m in range(CMP_STRIDE // 2):
            r0 = stage_ref[s, pl.ds(2 * m, nb, stride=CMP_STRIDE), :]
            r1 = stage_ref[s, pl.ds(2 * m + 1, nb, stride=CMP_STRIDE), :]
            head0 = jnp.where(lo, r0, pltpu.roll(r1, HEAD_DIM, axis=1))
            head1 = jnp.where(lo, pltpu.roll(r0, HEAD_DIM, axis=1), r1)
            for h, piece in enumerate((head0, head1)):
                c0 = (2 * s + h) * span + m * KV_WIDTH
                cmpx_ref[0, :, c0:c0 + KV_WIDTH] = piece.astype(BF16)


def _inproj_kernel(x_ref, shift_ref, scale_ref, nw_ref, w_ref, qw_ref, kw_ref, gq_ref, gk_ref,
                   c_ref, su_ref, sd_ref,
                   qp_ref, kvb_ref, win_ref, winb_ref, z_ref, xbc_ref, misc_ref, *rest, seq_layout):
    x = x_ref[...]
    ms = jnp.mean(x * x, axis=-1, keepdims=True)
    h = x * lax.rsqrt(ms + EPS) * nw_ref[...]
    h = h * (1.0 + _mod(scale_ref)) + _mod(shift_ref)
    hb = h.astype(BF16)
    c, su, sd = c_ref[...], su_ref[...], sd_ref[...]

    q = _dot(hb, w_ref[:, _C_Q:_C_Q + ATTN_WIDTH])
    qms = _dot((q * q).astype(BF16), gq_ref[...])
    q = q * lax.rsqrt(qms + EPS) * qw_ref[...]
    q = _rope(q, c, su, sd) * (HEAD_DIM ** -0.5 * LOG2E)
    lane = lax.broadcasted_iota(jnp.int32, q.shape, 1) % LANES
    lo = lane < HEAD_DIM
    q_up = pltpu.roll(q, ATTN_WIDTH - HEAD_DIM, axis=1)
    q_dn = pltpu.roll(q, HEAD_DIM, axis=1)
    zero = jnp.zeros_like(q)
    nat_lo = jnp.where(lo, q, zero)
    nat_hi = jnp.where(lo, zero, q)
    up_lo = jnp.where(lo, q_up, zero)
    dn_hi = jnp.where(lo, zero, q_dn)
    blocks = []
    for hd in range(N_Q_HEADS):
        pair = hd // 2
        sl = slice(pair * LANES, (pair + 1) * LANES)
        if hd < GQA_GROUP:
            blocks.append((nat_lo if hd % 2 == 0 else up_lo)[:, sl])
        else:
            blocks.append((dn_hi if hd % 2 == 0 else nat_hi)[:, sl])
    qp_ref[...] = jnp.concatenate(blocks, axis=1).astype(BF16)

    kv = _dot(hb, w_ref[:, _C_KV:_C_KV + 6 * KV_WIDTH])
    outs = []
    for br in range(3):
        k = kv[:, br * 2 * KV_WIDTH:br * 2 * KV_WIDTH + KV_WIDTH]
        v = kv[:, br * 2 * KV_WIDTH + KV_WIDTH:(br + 1) * 2 * KV_WIDTH]
        kms = _dot((k * k).astype(BF16), gk_ref[...])
        k = k * lax.rsqrt(kms + EPS) * kw_ref[:, br * KV_WIDTH:(br + 1) * KV_WIDTH]
        k = _rope(k, c, su, sd)
        outs += [k, v]
    kvrows = jnp.concatenate(outs[:4], axis=1)
    winrows = jnp.concatenate(outs[4:], axis=1)
    kvb_ref[...] = kvrows.astype(BF16)
    win_ref[...] = winrows
    winb_ref[...] = winrows.astype(BF16)
    if seq_layout:
        kvt_ref, cmpx_ref, stage_ref = rest
        tm = kvrows.shape[0]
        for r in range(4):
            kvt_ref[0, r] = jnp.transpose(kvrows[:, r * KV_WIDTH:(r + 1) * KV_WIDTH])
        for s in range(2):
            stage_ref[s] = kvrows[:, s * KV_WIDTH:(s + 1) * KV_WIDTH]
        _stride_block_store(stage_ref, cmpx_ref, tm)
    else:
        rest[0][...] = kvrows

    z_ref[...] = _dot(hb, w_ref[:, _C_Z:_C_Z + SSM_WIDTH])
    xbc_ref[...] = _dot(hb, w_ref[:, _C_XBC:_C_XBC + CONV_DIM])
    misc_ref[...] = _dot(hb, w_ref[:, _C_MISC:_C_MISC + LANES])


def inproj(x, mod3, mod_row0, norm_w, wp, q_norm_w, k_norm_w, pos, tm, seq_layout):
    b, t, d = x.shape
    n = b * t
    tiles_per_b = t // tm
    xf = x.reshape(n, d)
    c, su, sd = _rope_tables(pos)
    qw = jnp.tile(q_norm_w, N_Q_HEADS).reshape(1, ATTN_WIDTH)
    kw = jnp.concatenate([jnp.tile(k_norm_w[i], N_KV_HEADS) for i in range(3)]).reshape(1, 3 * KV_WIDTH)
    gq = _group_mean_matrix(ATTN_WIDTH)
    gk = _group_mean_matrix(KV_WIDTH)

    def mod_spec(col):
        return _mod_spec(mod3, col, tm, tiles_per_b, mod_row0)

    def tok(wd):
        return pl.BlockSpec((tm, wd), lambda i: (i, 0))

    def full(a):
        return pl.BlockSpec(a.shape, lambda i: (0,) * a.ndim)

    rope_spec = pl.BlockSpec((tm, LANES), lambda i: (i % tiles_per_b, 0))
    out_shape = [
        jax.ShapeDtypeStruct((n, N_Q_HEADS * LANES), BF16),
        jax.ShapeDtypeStruct((n, 4 * KV_WIDTH), BF16),
        jax.ShapeDtypeStruct((n, 2 * KV_WIDTH), F32),
        jax.ShapeDtypeStruct((n, 2 * KV_WIDTH), BF16),
        jax.ShapeDtypeStruct((n, SSM_WIDTH), F32),
        jax.ShapeDtypeStruct((n, CONV_DIM), F32),
        jax.ShapeDtypeStruct((n, LANES), F32),
    ]
    out_specs = [tok(s.shape[1]) for s in out_shape]
    scratch = []
    if seq_layout:
        out_shape += [jax.ShapeDtypeStruct((b, 4, KV_WIDTH, t), F32),
                      jax.ShapeDtypeStruct((b, t // CMP_STRIDE, CMP_STRIDE * 2 * KV_WIDTH), BF16)]
        out_specs += [pl.BlockSpec((1, 4, KV_WIDTH, tm), lambda i: (i // tiles_per_b, 0, 0, i % tiles_per_b)),
                      pl.BlockSpec((1, tm // CMP_STRIDE, CMP_STRIDE * 2 * KV_WIDTH),
                                   lambda i: (i // tiles_per_b, i % tiles_per_b, 0))]
        scratch = [pltpu.VMEM((2, tm, KV_WIDTH), F32)]
    else:
        out_shape += [jax.ShapeDtypeStruct((n, 4 * KV_WIDTH), F32)]
        out_specs += [tok(4 * KV_WIDTH)]
    return pl.pallas_call(
        functools.partial(_inproj_kernel, seq_layout=seq_layout),
        grid=(n // tm,),
        in_specs=[tok(d), mod_spec(0), mod_spec(1), full(norm_w), full(wp), full(qw), full(kw), full(gq), full(gk),
                  rope_spec, rope_spec, rope_spec],
        out_specs=tuple(out_specs),
        out_shape=tuple(out_shape),
        scratch_shapes=scratch,
        compiler_params=_cparams(("arbitrary",)),
        name="inproj",
    )(xf, mod3, mod3, norm_w, wp, qw, kw, gq, gk, c, su, sd)


def _prep_compress(cmp_pe, cmp_w1, cmp_w2):
    span = CMP_STRIDE * HEAD_DIM
    w1p = jnp.concatenate([cmp_w1[:, :span], cmp_w1[:, span:]], axis=2).astype(BF16)
    pep = cmp_pe.reshape(2, 2, span)
    eye = jnp.eye(N_KV_HEADS, dtype=F32)
    w2p = jnp.einsum("poe,hg->phoge", cmp_w2, eye).reshape(2, N_KV_HEADS, CMP_HIDDEN, KV_WIDTH).astype(BF16)
    return w1p, pep, w2p


def _compress_kernel(x_ref, w1_ref, pe_ref, w2_ref, o_ref):
    part = pl.program_id(1)
    nb = x_ref.shape[1]
    span = CMP_STRIDE * HEAD_DIM
    pe = pe_ref[0]
    out = jnp.zeros((nb, KV_WIDTH), F32)
    for h in range(N_KV_HEADS):
        xk = x_ref[0, :, h * span:(h + 1) * span]
        xv = x_ref[0, :, (N_KV_HEADS + h) * span:(N_KV_HEADS + h + 1) * span]
        x = jnp.where(part == 0, xk, xv).astype(F32)
        u = _dot((x + pe[0:1]).astype(BF16), w1_ref[0, :, :CMP_HIDDEN])
        v = _dot((x + pe[1:2]).astype(BF16), w1_ref[0, :, CMP_HIDDEN:])
        h1 = u + pltpu.roll(v, nb - 1, axis=0)
        out = out + _dot(_silu(h1).astype(BF16), w2_ref[0, h])
    row = lax.broadcasted_iota(jnp.int32, out.shape, 0)
    o_ref[0, 0] = jnp.where(row < nb - 1, out, 0.0).astype(o_ref.dtype)


def compress(x, w1p, pep, w2p):
    b, nb, width = x.shape
    return pl.pallas_call(
        _compress_kernel,
        grid=(b, 2),
        in_specs=[
            pl.BlockSpec((1, nb, width), lambda i, p: (i, 0, 0)),
            pl.BlockSpec((1,) + w1p.shape[1:], lambda i, p: (p, 0, 0)),
            pl.BlockSpec((1,) + pep.shape[1:], lambda i, p: (p, 0, 0)),
            pl.BlockSpec((1,) + w2p.shape[1:], lambda i, p: (p, 0, 0, 0)),
        ],
        out_specs=pl.BlockSpec((1, 1, nb, KV_WIDTH), lambda i, p: (i, p, 0, 0)),
        out_shape=jax.ShapeDtypeStruct((b, 2, nb, KV_WIDTH), BF16),
        compiler_params=_cparams(("arbitrary", "arbitrary")),
        name="compress",
    )(x, w1p, pep, w2p)


N_SEL_LANES = LANES


def _cover_matrix(nb):
    c = np.arange(nb)[:, None]
    j = np.arange(N_SEL_LANES)[None, :]
    start = c * CMP_STRIDE
    m = (start < (j + 1) * SEL_BLOCK) & (start + CMP_LEN > j * SEL_BLOCK)
    return jnp.asarray(m.astype(np.float32), BF16)


def _place_heads(res, kv):
    lane = lax.broadcasted_iota(jnp.int32, res[0].shape, 1)
    lo = lane < HEAD_DIM
    blocks = []
    for pair in range(GQA_GROUP // 2):
        a, b = res[2 * pair], res[2 * pair + 1]
        if kv == 0:
            blocks.append(jnp.where(lo, a, pltpu.roll(b, HEAD_DIM, axis=1)))
        else:
            blocks.append(jnp.where(lo, pltpu.roll(a, HEAD_DIM, axis=1), b))
    return jnp.concatenate(blocks, axis=1)


def _group_rows(q_ref, kv):
    heads = range(kv * GQA_GROUP, (kv + 1) * GQA_GROUP)
    return jnp.concatenate([q_ref[0, :, hd * LANES:(hd + 1) * LANES] for hd in heads], axis=0)


def _heads_from_transposed(out_t, tq, kv):
    out = jnp.transpose(out_t)
    return _place_heads([out[g * tq:(g + 1) * tq] for g in range(GQA_GROUP)], kv)


def _cmp_select_kernel(q_ref, kc_ref, vc_ref, covt_ref, o_ref, m_ref, *, q_off):
    tq = q_ref.shape[1]
    rows = GQA_GROUP * tq
    nb = kc_ref.shape[2]
    wl = max(tq, LANES)
    assert tq % LANES == 0 or rows == LANES
    t0 = q_off + pl.program_id(1) * tq
    kc = kc_ref[0, 0]
    vc = vc_ref[0, 0]
    qpos = t0 + lax.broadcasted_iota(jnp.int32, (nb, rows), 1) % tq
    cend = lax.broadcasted_iota(jnp.int32, (nb, rows), 0) * CMP_STRIDE + (CMP_LEN - 1)
    valid = cend <= qpos
    blk = lax.broadcasted_iota(jnp.int32, (N_SEL_LANES, wl), 0)
    cur = (t0 + lax.broadcasted_iota(jnp.int32, (N_SEL_LANES, wl), 1) % tq) // SEL_BLOCK
    forced = (blk == 0) | ((blk <= cur) & (blk > cur - N_LOCAL))
    o_groups = []
    for kv in range(N_KV_HEADS):
        s = _dot_nt(kc, _group_rows(q_ref, kv))
        s = jnp.where(valid, s, NEG)
        e = jnp.exp2(s - jnp.max(s, axis=0, keepdims=True))
        p = e / jnp.sum(e, axis=0, keepdims=True)
        p = jnp.where(valid, p, 0.0)
        o_t = lax.dot_general(vc, p.astype(BF16), (((0,), (0,)), ((), ())), preferred_element_type=F32)
        o_groups.append(_heads_from_transposed(o_t, tq, kv))
        if tq % LANES == 0:
            psum = sum(p[:, g * tq:(g + 1) * tq] for g in range(GQA_GROUP))
        else:
            psum = p + sum(pltpu.roll(p, g * tq, axis=1) for g in range(1, GQA_GROUP))
        hi, lo = _split2(psum)
        imp = _dot(covt_ref[...], hi) + _dot(covt_ref[...], lo)
        x = jnp.where(forced, -jnp.inf, jnp.where(blk > cur, -BIG, imp))
        sel = forced
        for _ in range(N_SEL - 1 - N_LOCAL):
            mx = jnp.max(x, axis=0, keepdims=True)
            idx = jnp.min(jnp.where(x == mx, blk, N_SEL_LANES), axis=0, keepdims=True)
            hit = blk == idx
            sel = sel | hit
            x = jnp.where(hit, -jnp.inf, x)
        mneg = jnp.transpose(jnp.where(sel, 0.0, NEG))
        m_ref[0, kv] = mneg[:tq].astype(m_ref.dtype)
    o_ref[0] = jnp.concatenate(o_groups, axis=1)


def cmp_select(qp, kcv, q_off, tq):
    b, t, _ = qp.shape
    nb = kcv.shape[2]
    cover = jnp.transpose(_cover_matrix(nb))
    return pl.pallas_call(
        functools.partial(_cmp_select_kernel, q_off=q_off),
        grid=(b, t // tq),
        in_specs=[
            pl.BlockSpec((1, tq, N_Q_HEADS * LANES), lambda i, j: (i, j, 0)),
            pl.BlockSpec((1, 1, nb, KV_WIDTH), lambda i, j: (i, 0, 0, 0)),
            pl.BlockSpec((1, 1, nb, KV_WIDTH), lambda i, j: (i, 1, 0, 0)),
            pl.BlockSpec((N_SEL_LANES, nb), lambda i, j: (0, 0)),
        ],
        out_specs=(
            pl.BlockSpec((1, tq, ATTN_WIDTH), lambda i, j: (i, j, 0)),
            pl.BlockSpec((1, N_KV_HEADS, tq, N_SEL_LANES), lambda i, j: (i, 0, j, 0)),
        ),
        out_shape=(
            jax.ShapeDtypeStruct((b, t, ATTN_WIDTH), F32),
            jax.ShapeDtypeStruct((b, N_KV_HEADS, t, N_SEL_LANES), BF16),
        ),
        compiler_params=_cparams(("arbitrary", "arbitrary")),
        name="cmp_select",
    )(qp, kcv, kcv, cover)


SEL_TILE_ELEMS = 512 * 512
SEL_WIDTHS = (8, 4, 2, 1)
WIN_CHUNK = 256
WIN_SPAN = 3


def _block_onehot(s):
    key = np.arange(s)[:, None]
    j = np.arange(N_SEL_LANES)[None, :]
    return jnp.asarray((key // SEL_BLOCK == j).astype(np.float32), BF16)


def _gate_expand():
    m = np.zeros((3, LANES, ATTN_WIDTH), np.float32)
    for br in range(3):
        for hd in range(N_Q_HEADS):
            m[br, SSM_HEADS + 3 * hd + br, hd * HEAD_DIM:(hd + 1) * HEAD_DIM] = 1.0
    return jnp.asarray(m, BF16)


def _flash_update(ss, v, m_ref, acc_ref):
    lane = lax.broadcasted_iota(jnp.int32, v.shape, 1)
    one = jnp.ones(v.shape, v.dtype)
    stage = []
    for k, s in enumerate(ss):
        m_old = m_ref[k]
        m_new = jnp.maximum(m_old, jnp.max(s, axis=0, keepdims=True))
        alpha = jnp.exp2(m_old - m_new)
        p = jnp.exp2(s - m_new)
        m_ref[k] = m_new
        stage.append((alpha, p.astype(BF16)))
    for k, (alpha, p) in enumerate(stage):
        vk = jnp.where((lane < HEAD_DIM) == (k == 0), v, one)
        pv = lax.dot_general(vk, p, (((0,), (0,)), ((), ())), preferred_element_type=F32)
        acc_ref[k] = alpha * acc_ref[k] + pv


def _sel_chunk(rows, n_keys):
    chunk = SEL_TILE_ELEMS // rows
    while n_keys % chunk:
        chunk //= 2
    return chunk


def _sel_win_kernel(q_ref, mneg_ref, ksel_ref, vsel_ref, et_ref, kwin_ref, vwin_ref, ocmp_ref, misc_ref, eg_ref,
                    o_ref, lhs_ref, m_ref, acc_ref, *, q_off, win_pos0):
    tq = q_ref.shape[1]
    rows = GQA_GROUP * tq
    SEL_CHUNK = _sel_chunk(rows, ksel_ref.shape[1])
    t0 = q_off + pl.program_id(1) * tq
    n_sel = lax.shift_right_logical(t0 + tq - 1, int(math.log2(SEL_CHUNK))) + 1
    w_lo = jnp.maximum(t0 - (WINDOW - 1) - win_pos0, 0) // WIN_CHUNK
    w_hi = (t0 + tq - 1 - win_pos0) // WIN_CHUNK + 1

    def qrow(n_keys):
        return lax.broadcasted_iota(jnp.int32, (n_keys, rows), 1) % tq + t0

    def init():
        m_ref[...] = jnp.full(m_ref.shape, NEG, F32)
        acc_ref[...] = jnp.zeros(acc_ref.shape, F32)

    def finish():
        outs = []
        for kv in range(N_KV_HEADS):
            acc = acc_ref[kv]
            denom_row = HEAD_DIM * (1 - kv)
            outs.append(_heads_from_transposed(acc / acc[denom_row:denom_row + 1, :], tq, kv))
        return jnp.concatenate(outs, axis=1)

    for kv in range(N_KV_HEADS):
        for g in range(GQA_GROUP):
            hd = kv * GQA_GROUP + g
            lhs_ref[kv, g * tq:(g + 1) * tq, :LANES] = q_ref[0, :, hd * LANES:(hd + 1) * LANES]
            lhs_ref[kv, g * tq:(g + 1) * tq, LANES:] = mneg_ref[0, kv]

    init()

    def sel_step(i, carry, causal, width):
        n_keys = width * SEL_CHUNK
        r0 = pl.multiple_of(i * n_keys, n_keys)
        rhs = jnp.concatenate([ksel_ref[0, pl.ds(r0, n_keys), :], et_ref[pl.ds(r0, n_keys), :]], axis=1)
        v = vsel_ref[0, pl.ds(r0, n_keys), :]
        if causal:
            ok = r0 + lax.broadcasted_iota(jnp.int32, (n_keys, rows), 0) <= qrow(n_keys)
        ss = [_dot_nt(rhs, lhs_ref[kv]) for kv in range(N_KV_HEADS)]
        if causal:
            ss = [jnp.where(ok, s, NEG) for s in ss]
        _flash_update(ss, v, m_ref, acc_ref)
        return carry

    n_full = lax.shift_right_logical(t0 + 1, int(math.log2(SEL_CHUNK)))
    done = 0
    for width in SEL_WIDTHS:
        n_steps = (n_full - done) // width
        lax.fori_loop(done // width, done // width + n_steps,
                      functools.partial(sel_step, causal=False, width=width), 0)
        done = done + n_steps * width
    lax.fori_loop(n_full, n_sel, functools.partial(sel_step, causal=True, width=1), 0)
    o_sel = finish()

    init()

    def win_step(c, carry, width):
        n_keys = width * WIN_CHUNK
        r0 = pl.multiple_of(c * WIN_CHUNK, WIN_CHUNK)
        k = kwin_ref[0, pl.ds(r0, n_keys), :]
        v = vwin_ref[0, pl.ds(r0, n_keys), :]
        wpos = win_pos0 + r0 + lax.broadcasted_iota(jnp.int32, (n_keys, rows), 0)
        qr = qrow(n_keys)
        ok = (wpos <= qr) & (wpos > qr - WINDOW)
        ss = [jnp.where(ok, _dot_nt(k, lhs_ref[kv, :, :LANES]), NEG) for kv in range(N_KV_HEADS)]
        _flash_update(ss, v, m_ref, acc_ref)
        return carry

    n_span = (w_hi - w_lo) // WIN_SPAN
    lax.fori_loop(w_lo, w_lo + n_span, functools.partial(win_step, width=WIN_SPAN), 0)
    lax.fori_loop(w_lo + n_span * WIN_SPAN, w_hi, functools.partial(win_step, width=1), 0)
    o_win = finish()

    gates = jax.nn.sigmoid(misc_ref[0])
    ghi = gates.astype(BF16)
    glo = (gates - ghi.astype(F32)).astype(BF16)
    branches = (ocmp_ref[0], o_sel, o_win)
    out = jnp.zeros(branches[0].shape, F32)
    for br in range(3):
        out = out + (_dot(ghi, eg_ref[br]) + _dot(glo, eg_ref[br])) * branches[br]
    o_ref[0] = out


def sel_win_attention(qp, mneg, kvb, sel_col, winb, o_cmp, misc, q_off, win_pos0, tq):
    b, t, _ = qp.shape
    s = kvb.shape[1]
    sw = winb.shape[1]
    et = _block_onehot(s)
    eg = _gate_expand()
    rows = GQA_GROUP * tq
    assert q_off + t <= s and q_off + t - win_pos0 <= sw and sw % WIN_CHUNK == 0
    return pl.pallas_call(
        functools.partial(_sel_win_kernel, q_off=q_off, win_pos0=win_pos0),
        grid=(b, t // tq),
        in_specs=[
            pl.BlockSpec((1, tq, N_Q_HEADS * LANES), lambda i, j: (i, j, 0)),
            pl.BlockSpec((1, N_KV_HEADS, tq, N_SEL_LANES), lambda i, j: (i, 0, j, 0)),
            pl.BlockSpec((1, s, KV_WIDTH), lambda i, j: (i, 0, sel_col)),
            pl.BlockSpec((1, s, KV_WIDTH), lambda i, j: (i, 0, sel_col + 1)),
            pl.BlockSpec((s, N_SEL_LANES), lambda i, j: (0, 0)),
            pl.BlockSpec((1, sw, KV_WIDTH), lambda i, j: (i, 0, 0)),
            pl.BlockSpec((1, sw, KV_WIDTH), lambda i, j: (i, 0, 1)),
            pl.BlockSpec((1, tq, ATTN_WIDTH), lambda i, j: (i, j, 0)),
            pl.BlockSpec((1, tq, LANES), lambda i, j: (i, j, 0)),
            pl.BlockSpec((3, LANES, ATTN_WIDTH), lambda i, j: (0, 0, 0)),
        ],
        out_specs=pl.BlockSpec((1, tq, ATTN_WIDTH), lambda i, j: (i, j, 0)),
        out_shape=jax.ShapeDtypeStruct((b, t, ATTN_WIDTH), F32),
        scratch_shapes=[
            pltpu.VMEM((N_KV_HEADS, rows, 2 * LANES), BF16),
            pltpu.VMEM((N_KV_HEADS, 1, rows), F32),
            pltpu.VMEM((N_KV_HEADS, LANES, rows), F32),
        ],
        compiler_params=_cparams(("arbitrary", "arbitrary")),
        name="sel_win_attention",
    )(qp, mneg, kvb, kvb, et, winb, winb, o_cmp, misc, eg)


CONV_PAD = 8
HEAD_PAIRS = SSM_HEADS // 2


def _split3(x):
    a = x.astype(BF16)
    r = x - a.astype(F32)
    b = r.astype(BF16)
    c = (r - b.astype(F32)).astype(BF16)
    return a, b, c


def _ssd_kernel(xbc_ref, z_ref, misc_ref, conv0_ref, h0_ref, cw_ref, cb_ref, dtb_ref, a_ref, dsk_ref, nw_ref,
                y_ref, hout_ref, cout_ref, xp_ref, h_ref, ms_ref, *, t_valid):
    ch = pl.program_id(1)
    L = SSD_CHUNK
    keep = CONV_WIDTH - 1

    @pl.when(ch == 0)
    def _():
        xp_ref[...] = jnp.zeros(xp_ref.shape, F32)
        xp_ref[CONV_PAD - keep:CONV_PAD, :] = conv0_ref[0]
        h_ref[...] = h0_ref[0]

    xp_ref[CONV_PAD:CONV_PAD + t_valid, :] = xbc_ref[0]
    conv = cb_ref[...]
    for j in range(CONV_WIDTH):
        conv = conv + cw_ref[j:j + 1, :] * xp_ref[CONV_PAD - keep + j:CONV_PAD - keep + j + L, :]
    last = xp_ref[CONV_PAD + t_valid - keep:CONV_PAD + t_valid, :]
    cout_ref[0] = last
    xp_ref[CONV_PAD - keep:CONV_PAD, :] = last
    xc = _silu(conv)

    row = lax.broadcasted_iota(jnp.int32, (L, LANES), 0)
    lane = lax.broadcasted_iota(jnp.int32, (L, LANES), 1)
    if t_valid == L:
        raw = misc_ref[0]
    else:
        ms_ref[...] = jnp.zeros(ms_ref.shape, F32)
        ms_ref[0:t_valid, :] = misc_ref[0]
        raw = ms_ref[...]
    v = raw + dtb_ref[...]
    dt = jnp.maximum(v, 0.0) + jnp.log(1.0 + jnp.exp(-jnp.abs(v)))
    dt = jnp.where((lane < SSM_HEADS) & (row < t_valid), dt, 0.0)
    da = dt * a_ref[...]
    tri = (lax.broadcasted_iota(jnp.int32, (L, L), 1) <= lax.broadcasted_iota(jnp.int32, (L, L), 0))
    trib = tri.astype(BF16)
    acum = sum(_dot(trib, part) for part in _split3(da))
    acum_t = jnp.transpose(acum)
    dt_t = jnp.transpose(dt)
    e_acum = jnp.exp(acum)
    e_last = jnp.exp(acum[L - 1:L, :])
    w_end = jnp.exp(acum[L - 1:L, :] - acum) * dt
    lo = lane < SSM_HEAD_DIM

    ys = []
    for pair in range(HEAD_PAIRS):
        grp = (2 * pair) // (SSM_HEADS // SSM_GROUPS)
        bg = xc[:, SSM_WIDTH + grp * SSM_STATE:SSM_WIDTH + (grp + 1) * SSM_STATE].astype(BF16)
        cg = xc[:, SSM_WIDTH + (SSM_GROUPS + grp) * SSM_STATE:SSM_WIDTH + (SSM_GROUPS + grp + 1) * SSM_STATE].astype(BF16)
        g = _dot_nt(cg, bg)
        xpair = xc[:, pair * LANES:(pair + 1) * LANES]
        y = jnp.zeros((L, LANES), F32)
        for sub in range(2):
            hd = 2 * pair + sub
            seg = acum[:, hd:hd + 1] - acum_t[hd:hd + 1, :]
            m = g * jnp.exp(jnp.where(tri, seg, NEG)) * dt_t[hd:hd + 1, :]
            xm = jnp.where(lo if sub == 0 else ~lo, xpair, 0.0)
            y = y + _dot(m.astype(BF16), xm.astype(BF16))
        col = lambda a: jnp.where(lo, a[:, 2 * pair:2 * pair + 1], a[:, 2 * pair + 1:2 * pair + 2])
        hp = h_ref[pair]
        y = y + _dot_nt(cg, hp.astype(BF16)) * col(e_acum)
        y = y + col(dsk_ref[...]) * xpair
        xw = (xpair * col(w_end)).astype(BF16)
        st = lax.dot_general(xw, bg, (((0,), (0,)), ((), ())), preferred_element_type=F32)
        prow = lax.broadcasted_iota(jnp.int32, (LANES, LANES), 0) < SSM_HEAD_DIM
        dec = jnp.where(prow, e_last[:, 2 * pair:2 * pair + 1], e_last[:, 2 * pair + 1:2 * pair + 2])
        h_ref[pair] = hp * dec + st
        ys.append(y)
    y = jnp.concatenate(ys, axis=1)
    if t_valid != L:
        y = y[:t_valid]
    y = y * _silu(z_ref[0])
    y = y * lax.rsqrt(jnp.mean(y * y, axis=-1, keepdims=True) + EPS) * nw_ref[...]
    y_ref[0] = y

    @pl.when(ch == pl.num_programs(1) - 1)
    def _():
        hout_ref[0] = h_ref[...]


def ssd(xbc, z, misc, conv0, h0, conv_w, conv_b, dt_bias, a_log, d_skip, norm_w):
    b, t, _ = xbc.shape
    L = SSD_CHUNK
    t_valid = L if t % L == 0 else t
    assert t_valid == L or t < L
    n_ch = max(t // L, 1)
    keep = CONV_WIDTH - 1
    pad8 = lambda v: jnp.pad(v.astype(F32), (0, LANES - SSM_HEADS)).reshape(1, LANES)
    dtb = pad8(dt_bias)
    a = pad8(-jnp.exp(a_log.astype(F32)))
    dsk = pad8(d_skip)
    h0p = h0.reshape(b, HEAD_PAIRS, 2 * SSM_HEAD_DIM, SSM_STATE)
    full = lambda arr: pl.BlockSpec(arr.shape, lambda i, c: (0,) * arr.ndim)
    tok = lambda wd: pl.BlockSpec((1, t_valid, wd), lambda i, c: (i, c, 0))
    y, hout, cout = pl.pallas_call(
        functools.partial(_ssd_kernel, t_valid=t_valid),
        grid=(b, n_ch),
        in_specs=[
            tok(CONV_DIM), tok(SSM_WIDTH), tok(LANES),
            pl.BlockSpec((1, keep, CONV_DIM), lambda i, c: (i, 0, 0)),
            pl.BlockSpec((1, HEAD_PAIRS, 2 * SSM_HEAD_DIM, SSM_STATE), lambda i, c: (i, 0, 0, 0)),
            full(conv_w), pl.BlockSpec((1, CONV_DIM), lambda i, c: (0, 0)),
            full(dtb), full(a), full(dsk), pl.BlockSpec((1, SSM_WIDTH), lambda i, c: (0, 0)),
        ],
        out_specs=(
            tok(SSM_WIDTH),
            pl.BlockSpec((1, HEAD_PAIRS, 2 * SSM_HEAD_DIM, SSM_STATE), lambda i, c: (i, 0, 0, 0)),
            pl.BlockSpec((1, keep, CONV_DIM), lambda i, c: (i, 0, 0)),
        ),
        out_shape=(
            jax.ShapeDtypeStruct((b, t, SSM_WIDTH), F32),
            jax.ShapeDtypeStruct((b, HEAD_PAIRS, 2 * SSM_HEAD_DIM, SSM_STATE), F32),
            jax.ShapeDtypeStruct((b, keep, CONV_DIM), F32),
        ),
        scratch_shapes=[
            pltpu.VMEM((CONV_PAD + L, CONV_DIM), F32),
            pltpu.VMEM((HEAD_PAIRS, 2 * SSM_HEAD_DIM, SSM_STATE), F32),
            pltpu.VMEM((L, LANES), F32),
        ],
        compiler_params=_cparams(("arbitrary", "arbitrary")),
        name="ssd",
    )(xbc, z, misc, conv0, h0p, conv_w, conv_b.reshape(1, CONV_DIM), dtb, a, dsk, norm_w.reshape(1, SSM_WIDTH))
    return y, hout.reshape(b, SSM_HEADS, SSM_HEAD_DIM, SSM_STATE), cout


def _split2(x):
    hi = x.astype(BF16)
    return hi, (x - hi.astype(F32)).astype(BF16)


def _merge_kernel(oa_ref, ys_ref, x_ref, g1_ref, sh2_ref, sc2_ref, anw_ref, wo_ref, n2w_ref, wrh_ref, wrl_ref,
                  x1_ref, h2_ref, lg_ref):
    oa = oa_ref[...]
    a = oa * lax.rsqrt(jnp.mean(oa * oa, axis=-1, keepdims=True) + EPS) * anw_ref[...]
    cat = jnp.concatenate([a.astype(BF16), ys_ref[...].astype(BF16)], axis=1)
    x1 = x_ref[...] + _mod(g1_ref) * _dot(cat, wo_ref[...])
    x1_ref[...] = x1
    h2 = x1 * lax.rsqrt(jnp.mean(x1 * x1, axis=-1, keepdims=True) + EPS) * n2w_ref[...]
    h2 = h2 * (1.0 + _mod(sc2_ref)) + _mod(sh2_ref)
    h2_ref[...] = h2.astype(BF16)
    hh, hl = _split2(h2)
    lg_ref[...] = _dot_nt(wrh_ref[...], hh) + _dot_nt(wrh_ref[...], hl) + _dot_nt(wrl_ref[...], hh)


def merge(o_attn, y_ssm, x, mod3, mod_row0, attn_norm_w, wo, norm2_w, w_router, tm):
    b, t, d = x.shape
    n = b * t
    tiles_per_b = t // tm
    wrt = jnp.transpose(w_router)
    wrh, wrl = _split2(wrt)

    def mod_spec(col):
        return _mod_spec(mod3, col, tm, tiles_per_b, mod_row0)

    tok = lambda wd: pl.BlockSpec((tm, wd), lambda i: (i, 0))
    full = lambda a: pl.BlockSpec(a.shape, lambda i: (0,) * a.ndim)
    return pl.pallas_call(
        _merge_kernel,
        grid=(n // tm,),
        in_specs=[tok(ATTN_WIDTH), tok(SSM_WIDTH), tok(d), mod_spec(2), mod_spec(3), mod_spec(4),
                  full(attn_norm_w), full(wo), full(norm2_w), full(wrh), full(wrl)],
        out_specs=(tok(d), tok(d), pl.BlockSpec((N_EXPERTS, tm), lambda i: (0, i))),
        out_shape=(jax.ShapeDtypeStruct((n, d), F32), jax.ShapeDtypeStruct((n, d), BF16),
                   jax.ShapeDtypeStruct((N_EXPERTS, n), F32)),
        compiler_params=_cparams(("arbitrary",)),
        name="merge",
    )(o_attn.reshape(n, ATTN_WIDTH), y_ssm.reshape(n, SSM_WIDTH), x.reshape(n, d), mod3, mod3, mod3,
      attn_norm_w, wo, norm2_w, wrh, wrl)


EXPERTS_PER_GROUP = N_EXPERTS // N_EXPERT_GROUPS


def _first_max(x, ids, axes, n_ids):
    mx = jnp.max(x, axis=axes, keepdims=True)
    return ids == jnp.min(jnp.where(x == mx, ids, n_ids), axis=axes, keepdims=True), mx


def _route_kernel(lg_ref, eb_ref, tri_ref, w_ref, pos_ref, cnt_ref):
    lg = lg_ref[...]
    tn = lg.shape[2]
    scores = jax.nn.sigmoid(lg)
    biased = scores + eb_ref[...]
    sub = lax.broadcasted_iota(jnp.int32, lg.shape, 1)
    grp = lax.broadcasted_iota(jnp.int32, (N_EXPERT_GROUPS, 1, tn), 0)
    eid = lax.broadcasted_iota(jnp.int32, lg.shape, 0) * EXPERTS_PER_GROUP + sub
    hit, m1 = _first_max(biased, sub, 1, EXPERTS_PER_GROUP)
    m2 = jnp.max(jnp.where(hit, -jnp.inf, biased), axis=1, keepdims=True)
    gs = m1 + m2
    keep = jnp.zeros(gs.shape, jnp.bool_)
    for _ in range(TOPK_GROUPS):
        hit, _m = _first_max(gs, grp, 0, N_EXPERT_GROUPS)
        keep = keep | hit
        gs = jnp.where(hit, -jnp.inf, gs)
    x = jnp.where(keep, biased, NEG)
    sel = jnp.zeros(lg.shape, jnp.bool_)
    for _ in range(TOP_K):
        hit, _m = _first_max(x, eid, (0, 1), N_EXPERTS)
        sel = sel | hit
        x = jnp.where(hit, -jnp.inf, x)
    w = jnp.where(sel, scores, 0.0)
    w = w / jnp.sum(w, axis=(0, 1), keepdims=True) * ROUTED_SCALE
    w_ref[...] = w
    selb = sel.astype(BF16).reshape(N_EXPERTS, tn)
    pos = _dot(selb, tri_ref[...])
    pos_ref[...] = jnp.where(sel, pos.reshape(lg.shape), -1.0)
    cnt = jnp.sum(sel.astype(F32), axis=2, keepdims=True)
    cnt_ref[0] = jnp.broadcast_to(cnt, cnt_ref.shape[1:]).astype(jnp.int32)


def route(logits_t, e_bias, tn):
    n = logits_t.shape[1]
    lg3 = logits_t.reshape(N_EXPERT_GROUPS, EXPERTS_PER_GROUP, n)
    eb = e_bias.astype(F32).reshape(N_EXPERT_GROUPS, EXPERTS_PER_GROUP, 1)
    tri = jnp.asarray(np.triu(np.ones((tn, tn), np.float32), 1), BF16)
    blk = pl.BlockSpec((N_EXPERT_GROUPS, EXPERTS_PER_GROUP, tn), lambda i: (0, 0, i))
    w, pos, cnt = pl.pallas_call(
        _route_kernel,
        grid=(n // tn,),
        in_specs=[blk, pl.BlockSpec(eb.shape, lambda i: (0, 0, 0)), pl.BlockSpec((tn, tn), lambda i: (0, 0))],
        out_specs=(blk, blk, pl.BlockSpec((1, N_EXPERT_GROUPS, EXPERTS_PER_GROUP, LANES), lambda i: (i, 0, 0, 0))),
        out_shape=(jax.ShapeDtypeStruct(lg3.shape, F32), jax.ShapeDtypeStruct(lg3.shape, F32),
                   jax.ShapeDtypeStruct((n // tn, N_EXPERT_GROUPS, EXPERTS_PER_GROUP, LANES), jnp.int32)),
        compiler_params=_cparams(("arbitrary",)),
        name="route",
    )(lg3, eb, tri)
    return w.reshape(N_EXPERTS, n), pos.reshape(N_EXPERTS, n), cnt[..., 0].reshape(n // tn, N_EXPERTS)


MOE_ROWS = 128


def _swiglu(xb, wgu, wd, width):
    gu = _dot(xb, wgu)
    act = _silu(gu[:, :width]) * gu[:, width:]
    return _dot(act.astype(BF16), wd)


MOE_EXPERTS_PER_STEP = 4


MOE_ALIGN = 16
MOE_GATHER_ROWS = 896


def _moe_slots(tm):
    worst = TOP_K * tm + N_EXPERTS * (MOE_ALIGN - 1) + MOE_ROWS
    return -(-worst // MOE_GATHER_ROWS) * MOE_GATHER_ROWS


def _moe_kernel(cnt_ref, start_ref, h2_ref, w_ref, pos_ref, x1_ref, g2_ref, wgu_ref, wd_ref, sgu_ref, sd_ref,
                o_ref, g_all, xs):
    i = pl.program_id(0)
    es = pl.program_id(1)
    tm = h2_ref.shape[0]
    slots = g_all.shape[0]
    slot = lax.broadcasted_iota(jnp.int32, (MOE_ROWS, tm), 0).astype(F32)
    row = lax.broadcasted_iota(jnp.int32, (MOE_ROWS, 1), 0)

    def n_windows(cnt):
        return (cnt + MOE_ROWS - 1) // MOE_ROWS

    def window_start(e, j):
        return pl.multiple_of(start_ref[i * N_EXPERTS + e] + j * MOE_ROWS, MOE_ALIGN)

    @pl.when(es == 0)
    def _():
        g_all[...] = jnp.zeros(g_all.shape, BF16)

        def mark(e, carry):
            pos = pos_ref[pl.ds(e, 1), :]

            def mark_window(j, carry):
                hit = pos == slot + (j * MOE_ROWS).astype(F32)
                g_all[pl.ds(window_start(e, j), MOE_ROWS), :] = hit.astype(BF16)
                return carry

            return lax.fori_loop(0, n_windows(cnt_ref[i * N_EXPERTS + e]), mark_window, carry)

        lax.fori_loop(0, N_EXPERTS, mark, 0)

        def gather(c, carry):
            r0 = pl.multiple_of(c * MOE_GATHER_ROWS, MOE_GATHER_ROWS)
            rows = _dot(g_all[pl.ds(r0, MOE_GATHER_ROWS), :], h2_ref[...])
            xs[pl.ds(r0, MOE_GATHER_ROWS), :] = rows.astype(BF16)
            return carry

        lax.fori_loop(0, slots // MOE_GATHER_ROWS, gather, 0)

    for q in range(MOE_EXPERTS_PER_STEP):
        e = es * MOE_EXPERTS_PER_STEP + q
        cnt = cnt_ref[i * N_EXPERTS + e]
        wrow = w_ref[pl.ds(e, 1), :]

        def window(j, carry, q=q, e=e, cnt=cnt, wrow=wrow):
            r0 = window_start(e, j)
            xg = xs[pl.ds(r0, MOE_ROWS), :]
            out = _swiglu(xg, wgu_ref[q].astype(BF16), wd_ref[q].astype(BF16), D_EXPERT)
            g = g_all[pl.ds(r0, MOE_ROWS), :].astype(F32)
            out = out * jnp.sum(g * wrow, axis=1, keepdims=True)
            mine = row < cnt - j * MOE_ROWS
            xs[pl.ds(r0, MOE_ROWS), :] = jnp.where(mine, out.astype(BF16), xg)
            return carry

        lax.fori_loop(0, n_windows(cnt), window, 0)

    @pl.when(es == pl.num_programs(1) - 1)
    def _():
        y = lax.dot_general(g_all[...], xs[...], (((0,), (0,)), ((), ())), preferred_element_type=F32)
        y = y + _swiglu(h2_ref[...], sgu_ref[...], sd_ref[...], D_SHARED)
        o_ref[...] = x1_ref[...] + _mod(g2_ref) * y


def moe(h2, w_t, pos_t, counts, x1, mod3, mod_row0, t_per_b, wgu, wd, sgu, sd, tm):
    n, d = h2.shape
    tiles_per_b = t_per_b // tm
    eps = MOE_EXPERTS_PER_STEP
    slots = _moe_slots(tm)
    padded = (counts + MOE_ALIGN - 1) // MOE_ALIGN * MOE_ALIGN
    starts = jnp.cumsum(padded, axis=1) - padded
    grid_spec = pltpu.PrefetchScalarGridSpec(
        num_scalar_prefetch=2,
        grid=(n // tm, N_EXPERTS // eps),
        in_specs=[
            pl.BlockSpec((tm, d), lambda i, e, *_: (i, 0)),
            pl.BlockSpec((N_EXPERTS, tm), lambda i, e, *_: (0, i)),
            pl.BlockSpec((N_EXPERTS, tm), lambda i, e, *_: (0, i)),
            pl.BlockSpec((tm, d), lambda i, e, *_: (i, 0)),
            _mod_spec(mod3, 5, tm, tiles_per_b, mod_row0),
            pl.BlockSpec((eps, d, 2 * D_EXPERT), lambda i, e, *_: (e, 0, 0)),
            pl.BlockSpec((eps, D_EXPERT, d), lambda i, e, *_: (e, 0, 0)),
            pl.BlockSpec(sgu.shape, lambda i, e, *_: (0, 0)),
            pl.BlockSpec(sd.shape, lambda i, e, *_: (0, 0)),
        ],
        out_specs=pl.BlockSpec((tm, d), lambda i, e, *_: (i, 0)),
        scratch_shapes=[pltpu.VMEM((slots, tm), BF16), pltpu.VMEM((slots, d), BF16)],
    )
    return pl.pallas_call(
        _moe_kernel,
        grid_spec=grid_spec,
        out_shape=jax.ShapeDtypeStruct((n, d), F32),
        compiler_params=_cparams(("arbitrary", "arbitrary")),
        name="moe",
    )(counts.reshape(-1), starts.reshape(-1).astype(jnp.int32), h2, w_t, pos_t, x1, mod3, wgu, wd, sgu, sd)


SC_WINDOW = 128
PACK_W = 256
MOE_BLOCK_ROWS = 1024
HI_MASK = -65536


def _pack_pair(x):
    bits = pltpu.bitcast(x.astype(BF16).astype(F32), jnp.int32)
    return lax.shift_right_logical(bits[:, :PACK_W], 16) | (bits[:, PACK_W:] & HI_MASK)


def _unpack_pair(word):
    lo = pltpu.bitcast(lax.shift_left(word, 16), F32)
    hi = pltpu.bitcast(word & HI_MASK, F32)
    return jnp.concatenate([lo, hi], axis=1)


def _pack_kernel(x_ref, a_ref, b_ref):
    x = x_ref[...]
    a_ref[...] = _pack_pair(x[:, :2 * PACK_W])
    b_ref[...] = _pack_pair(x[:, 2 * PACK_W:])


def pack_rows(x, tm):
    n, d = x.shape
    tok = lambda wd: pl.BlockSpec((tm, wd), lambda i: (i, 0))
    return pl.pallas_call(
        _pack_kernel, grid=(n // tm,), in_specs=[tok(d)], out_specs=(tok(PACK_W), tok(PACK_W)),
        out_shape=(jax.ShapeDtypeStruct((n, PACK_W), jnp.int32),) * 2,
        compiler_params=_cparams(("arbitrary",)), name="pack_rows",
    )(x)


def _slots_kernel(w_ref, pos_ref, base_ref, tri_ref, slot_ref, wt_ref):
    w = w_ref[...]
    pos = pos_ref[...]
    sel = pos >= 0.0
    rank = _dot(tri_ref[...], sel.astype(BF16))
    dest = base_ref[0] + pos
    slots, wts = [], []
    for j in range(TOP_K):
        mine = sel & (rank == float(j))
        slots.append(jnp.sum(jnp.where(mine, dest, 0.0), axis=0, keepdims=True))
        wts.append(jnp.sum(jnp.where(mine, w, 0.0), axis=0, keepdims=True))
    slot_ref[...] = jnp.concatenate(slots, axis=0).astype(jnp.int32)
    wpad = jnp.concatenate(wts + [jnp.zeros((LANES - TOP_K, w.shape[1]), F32)], axis=0)
    wt_ref[...] = jnp.transpose(wpad)


def slots_of(w_t, pos_t, base, tn):
    n = w_t.shape[1]
    tri = jnp.asarray(np.tril(np.ones((N_EXPERTS, N_EXPERTS), np.float32), -1), BF16)
    blk = pl.BlockSpec((N_EXPERTS, tn), lambda i: (0, i))
    return pl.pallas_call(
        _slots_kernel, grid=(n // tn,),
        in_specs=[blk, blk, pl.BlockSpec((1, N_EXPERTS, 1), lambda i: (i, 0, 0)),
                  pl.BlockSpec((N_EXPERTS, N_EXPERTS), lambda i: (0, 0))],
        out_specs=(pl.BlockSpec((TOP_K, tn), lambda i: (0, i)), pl.BlockSpec((tn, LANES), lambda i: (i, 0))),
        out_shape=(jax.ShapeDtypeStruct((TOP_K, n), jnp.int32), jax.ShapeDtypeStruct((n, LANES), F32)),
        compiler_params=_cparams(("arbitrary",)), name="moe_slots",
    )(w_t, pos_t, base, tri)


def sc_scatter_rows(rows, idx, n_out):
    n, d = rows.shape
    m = idx.shape[0]
    nb = n // SC_WINDOW
    mesh = plsc.VectorSubcoreMesh(core_axis_name="core", subcore_axis_name="subcore")

    @functools.partial(pl.kernel, out_type=jax.ShapeDtypeStruct((n_out, d), rows.dtype), mesh=mesh)
    def scatter(x_hbm, i_hbm, o_hbm):
        def body(x_vmem, i_vmem):
            pltpu.sync_copy(x_vmem, o_hbm.at[i_vmem.at[0]])

        pltpu.emit_pipeline(
            body, grid=(m // SC_WINDOW,),
            in_specs=[pl.BlockSpec((SC_WINDOW, d), index_map=lambda i: (i % nb, 0)),
                      pl.BlockSpec((1, SC_WINDOW), index_map=lambda i: (0, i))],
            out_specs=[], core_axis_name=("core", "subcore"), dimension_semantics=(pltpu.PARALLEL,),
        )(x_hbm, i_hbm)

    return scatter(rows, idx.reshape(1, m))


def sc_gather_rows(table, idx):
    d = table.shape[1]
    m = idx.shape[0]
    mesh = plsc.VectorSubcoreMesh(core_axis_name="core", subcore_axis_name="subcore")

    @functools.partial(pl.kernel, out_type=jax.ShapeDtypeStruct((m, d), table.dtype), mesh=mesh)
    def gather(x_hbm, i_hbm, o_hbm):
        def body(i_vmem, o_vmem):
            pltpu.sync_copy(x_hbm.at[i_vmem.at[0]], o_vmem)

        pltpu.emit_pipeline(
            body, grid=(m // SC_WINDOW,),
            in_specs=[pl.BlockSpec((1, SC_WINDOW), index_map=lambda i: (0, i))],
            out_specs=[pl.BlockSpec((SC_WINDOW, d), index_map=lambda i: (i, 0))],
            core_axis_name=("core", "subcore"), dimension_semantics=(pltpu.PARALLEL,),
        )(i_hbm, o_hbm)

    return gather(table, idx.reshape(1, m))


def _experts_kernel(be_ref, nu_ref, xa_ref, xb_ref, wgu_ref, wd_ref, oa_ref, ob_ref, wgu_bf, wd_bf):
    b = pl.program_id(0)

    @pl.when(b < nu_ref[0])
    def _():
        @pl.when((b == 0) | (be_ref[b] != be_ref[jnp.maximum(b - 1, 0)]))
        def _():
            wgu_bf[...] = wgu_ref[0].astype(BF16)
            wd_bf[...] = wd_ref[0].astype(BF16)

        x = jnp.concatenate([_unpack_pair(xa_ref[...]), _unpack_pair(xb_ref[...])], axis=1).astype(BF16)
        out = _swiglu(x, wgu_bf[...], wd_bf[...], D_EXPERT)
        oa_ref[...] = _pack_pair(out[:, :2 * PACK_W])
        ob_ref[...] = _pack_pair(out[:, 2 * PACK_W:])


def experts_sorted(xa, xb, block_expert, n_used, wgu, wd):
    r = xa.shape[0]
    d = wd.shape[2]
    row = lambda b, be, nu: (jnp.minimum(b, nu[0] - 1), 0)
    blk = pl.BlockSpec((MOE_BLOCK_ROWS, PACK_W), row)
    grid_spec = pltpu.PrefetchScalarGridSpec(
        num_scalar_prefetch=2, grid=(r // MOE_BLOCK_ROWS,),
        in_specs=[blk, blk,
                  pl.BlockSpec((1, d, 2 * D_EXPERT), lambda b, be, nu: (be[b], 0, 0)),
                  pl.BlockSpec((1, D_EXPERT, d), lambda b, be, nu: (be[b], 0, 0))],
        out_specs=(blk, blk),
        scratch_shapes=[pltpu.VMEM((d, 2 * D_EXPERT), BF16), pltpu.VMEM((D_EXPERT, d), BF16)],
    )
    return pl.pallas_call(
        _experts_kernel, grid_spec=grid_spec,
        out_shape=(jax.ShapeDtypeStruct((r, PACK_W), jnp.int32),) * 2,
        compiler_params=_cparams(("arbitrary",)), name="moe_experts",
    )(block_expert, n_used, xa, xb, wgu, wd)


def _combine_kernel(ya_ref, yb_ref, wt_ref, h2_ref, x1_ref, g2_ref, sgu_ref, sd_ref, o_ref):
    wt = wt_ref[...]
    acc = _swiglu(h2_ref[...], sgu_ref[...], sd_ref[...], D_SHARED)
    for j in range(TOP_K):
        y = jnp.concatenate([_unpack_pair(ya_ref[j]), _unpack_pair(yb_ref[j])], axis=1)
        acc = acc + wt[:, j:j + 1] * y
    o_ref[...] = x1_ref[...] + _mod(g2_ref) * acc


def combine_sorted(ya, yb, wt, h2, x1, mod3, mod_row0, t_per_b, sgu, sd, tm):
    n, d = h2.shape
    tiles_per_b = t_per_b // tm
    tok = lambda wd: pl.BlockSpec((tm, wd), lambda i: (i, 0))
    yblk = pl.BlockSpec((TOP_K, tm, PACK_W), lambda i: (0, i, 0))
    full = lambda a: pl.BlockSpec(a.shape, lambda i: (0,) * a.ndim)
    return pl.pallas_call(
        _combine_kernel, grid=(n // tm,),
        in_specs=[yblk, yblk, tok(LANES), tok(d), tok(d), _mod_spec(mod3, 5, tm, tiles_per_b, mod_row0),
                  full(sgu), full(sd)],
        out_specs=tok(d), out_shape=jax.ShapeDtypeStruct((n, d), F32),
        compiler_params=_cparams(("arbitrary",)), name="moe_combine",
    )(ya, yb, wt, h2, x1, mod3, sgu, sd)


def moe_sorted(h2, w_t, pos_t, counts, x1, mod3, mod_row0, t_per_b, wgu, wd, sgu, sd, tm, overlap):
    n, d = h2.shape
    assert d == 4 * PACK_W and n % SC_WINDOW == 0
    n_blocks = (TOP_K * n + N_EXPERTS * (MOE_BLOCK_ROWS - 1)) // MOE_BLOCK_ROWS
    total = jnp.sum(counts, axis=0)
    region = (total + MOE_BLOCK_ROWS - 1) // MOE_BLOCK_ROWS * MOE_BLOCK_ROWS
    region_end = jnp.cumsum(region)
    base = (region_end - region)[None, :] + jnp.cumsum(counts, axis=0) - counts
    block_row0 = jnp.arange(n_blocks, dtype=region_end.dtype) * MOE_BLOCK_ROWS
    block_expert = jnp.sum(region_end[None, :] <= block_row0[:, None], axis=1)
    block_expert = jnp.minimum(block_expert, N_EXPERTS - 1).astype(jnp.int32)
    n_used = (region_end[-1:] // MOE_BLOCK_ROWS).astype(jnp.int32)
    slot, wt = slots_of(w_t, pos_t, base.astype(F32).reshape(-1, N_EXPERTS, 1), tm)
    dest = slot.reshape(-1)
    ha, hb = pack_rows(h2, tm)
    rows = n_blocks * MOE_BLOCK_ROWS
    xa, xb = sc_scatter_rows(ha, dest, rows), sc_scatter_rows(hb, dest, rows)
    n_used, rest = lax.optimization_barrier((n_used, overlap()))
    oa, ob = experts_sorted(xa, xb, block_expert, n_used, wgu, wd)
    ya = sc_gather_rows(oa, dest).reshape(TOP_K, n, PACK_W)
    yb = sc_gather_rows(ob, dest).reshape(TOP_K, n, PACK_W)
    return combine_sorted(ya, yb, wt, h2, x1, mod3, mod_row0, t_per_b, sgu, sd, tm), rest


GATHER_PAGES = 8


def _gather_kernel(pt_ref, *refs):
    pages, new_ref = refs[:GATHER_PAGES], refs[GATHER_PAGES]
    rows_ref, cmpx_ref, stage_ref = refs[GATHER_PAGES + 1:]
    step = pl.program_id(1)
    last = pl.num_programs(1) - 1
    n_rows = GATHER_PAGES * PAGE_SIZE

    @pl.when(step < last)
    def _():
        for k in range(GATHER_PAGES):
            sl = slice(k * PAGE_SIZE, (k + 1) * PAGE_SIZE)
            for r in range(4):
                tile = jnp.transpose(pages[k][0, r])
                if r < 2:
                    stage_ref[r, sl, :] = tile
                else:
                    rows_ref[0, sl, (r - 2) * KV_WIDTH:(r - 1) * KV_WIDTH] = tile.astype(BF16)

    @pl.when(step == last)
    def _():
        new = new_ref[0]
        tn = new.shape[0]
        stage_ref[...] = jnp.zeros(stage_ref.shape, F32)
        for s in range(2):
            stage_ref[s, 0:tn, :] = new[:, s * KV_WIDTH:(s + 1) * KV_WIDTH]
        pad = jnp.zeros((n_rows - tn, 2 * KV_WIDTH), F32)
        rows_ref[0] = jnp.concatenate([new[:, 2 * KV_WIDTH:], pad], axis=0).astype(BF16)

    _stride_block_store(stage_ref, cmpx_ref, n_rows)


def gather_pages(cache_t, page_table, new_rows):
    b, n_pages = page_table.shape
    steps = n_pages // GATHER_PAGES
    rows = GATHER_PAGES * PAGE_SIZE
    s_out = (steps + 1) * rows

    def page_spec(k):
        def idx(i, s, pt):
            p = jnp.minimum(s, steps - 1) * GATHER_PAGES + k
            return (pt[i * n_pages + p], 0, 0, 0)
        return pl.BlockSpec((1, 4, KV_WIDTH, PAGE_SIZE), idx)

    grid_spec = pltpu.PrefetchScalarGridSpec(
        num_scalar_prefetch=1,
        grid=(b, steps + 1),
        in_specs=[page_spec(k) for k in range(GATHER_PAGES)]
        + [pl.BlockSpec((1,) + new_rows.shape[1:], lambda i, s, pt: (i, 0, 0))],
        out_specs=(
            pl.BlockSpec((1, rows, 2 * KV_WIDTH), lambda i, s, pt: (i, s, 0)),
            pl.BlockSpec((1, rows // CMP_STRIDE, CMP_STRIDE * 2 * KV_WIDTH), lambda i, s, pt: (i, s, 0)),
        ),
        scratch_shapes=[pltpu.VMEM((2, rows, KV_WIDTH), F32)],
    )
    return pl.pallas_call(
        _gather_kernel,
        grid_spec=grid_spec,
        out_shape=(jax.ShapeDtypeStruct((b, s_out, 2 * KV_WIDTH), BF16),
                   jax.ShapeDtypeStruct((b, s_out // CMP_STRIDE, CMP_STRIDE * 2 * KV_WIDTH), BF16)),
        compiler_params=_cparams(("arbitrary", "arbitrary")),
        name="gather_pages",
    )(page_table.reshape(-1), *([cache_t] * GATHER_PAGES), new_rows)


def _attention(qp, cmpx, kvb, sel_col, winb, misc, cmp_w, q_off, win_pos0, tq):
    t = qp.shape[1]
    cur_lo, cur_hi = q_off // SEL_BLOCK, (q_off + t - 1) // SEL_BLOCK
    assert cur_hi < N_SEL_LANES or (cur_lo == cur_hi == N_SEL_LANES), (q_off, t)
    kcv = compress(cmpx, *cmp_w)
    o_cmp, mneg = cmp_select(qp, kcv, q_off, tq)
    return sel_win_attention(qp, mneg, kvb, sel_col, winb, o_cmp, misc, q_off, win_pos0, tq)


def kernel(x_prompt, x_sample, cache_kv, cache_win, state_ssm, state_conv, page_table, c_prompt, c_sample, w_ada, b_ada, norm1_w, norm2_w, w_in, q_norm_w, k_norm_w, cmp_pe, cmp_w1, cmp_w2, attn_out_norm_w, conv_w, conv_b, dt_bias, a_log, d_skip, ssm_norm_w, w_out, w_router, e_bias, w_exp_gu, w_exp_down, w_sh_gu, w_sh_down):
    xp, xq = x_prompt, x_sample
    bp, tp, d = xp.shape
    bq, tq, _ = xq.shape
    depth = w_ada.shape[0]
    past_len = page_table.shape[1] * PAGE_SIZE
    nq = bq * tq
    tq_pad = LANES // GQA_GROUP
    assert tp % TOKEN_TILE == 0 and tp >= WINDOW and nq % 8 == 0 and tq <= tq_pad
    pos_p = jnp.arange(tp, dtype=jnp.int32)
    pos_q = jnp.tile(past_len + jnp.arange(tq, dtype=jnp.int32), bq)
    c_all = jnp.concatenate([c_prompt, c_sample], axis=0)
    c_all = jnp.pad(c_all, ((0, -c_all.shape[0] % 8), (0, 0)))
    outs = [[] for _ in range(8)]
    for l in range(depth):
        mod = adaln_all(c_all, w_ada[l], b_ada[l])
        mod_p = mod.reshape(mod.shape[0], 1, 6 * d)
        mod_q = jnp.repeat(mod[bp:bp + bq], tq, axis=0)
        wp = _prep_w_in(w_in[l])
        cmp_w = _prep_compress(cmp_pe[l], cmp_w1[l], cmp_w2[l])
        wo = w_out[l].astype(BF16)
        wgu, wd = w_exp_gu[l], w_exp_down[l]
        sgu, sd = w_sh_gu[l].astype(BF16), w_sh_down[l].astype(BF16)
        ssm_w = (conv_w[l], conv_b[l], dt_bias[l], a_log[l], d_skip[l], ssm_norm_w[l])
        n1w, n2w, anw = norm1_w[l:l + 1], norm2_w[l:l + 1], attn_out_norm_w[l:l + 1]

        xq1 = xq.reshape(1, nq, d)
        rq = lambda a: a.reshape(bq, tq, a.shape[-1])
        padq = lambda a: jnp.pad(rq(a), ((0, 0), (0, tq_pad - tq), (0, 0)))
        proj_q = inproj(xq1, mod_q, 0, n1w, wp, q_norm_w[l], k_norm_w[l], pos_q, nq, False)
        cache_t = jnp.transpose(cache_kv[l], (0, 2, 3, 4, 1)).reshape(cache_kv.shape[1], 4, KV_WIDTH, PAGE_SIZE)
        past, cmpx_q = gather_pages(cache_t, page_table, rq(proj_q[-1]))
        xp, (proj_q, past, cmpx_q) = lax.optimization_barrier((xp, (proj_q, past, cmpx_q)))

        qp, kvb, win, winb, z, xbc, misc, kvt, cmpx = inproj(xp, mod_p, 0, n1w, wp, q_norm_w[l], k_norm_w[l], pos_p,
                                                            TOKEN_TILE, True)
        r3 = lambda a: a.reshape(bp, tp, a.shape[-1])
        o_attn = _attention(r3(qp), cmpx, r3(kvb), 2, r3(winb), r3(misc), cmp_w, 0, 0, QUERY_TILE)
        y_ssm, h_new, conv_new = ssd(r3(xbc), r3(z), r3(misc), jnp.zeros((bp, CONV_WIDTH - 1, CONV_DIM), F32),
                                     jnp.zeros((bp, SSM_HEADS, SSM_HEAD_DIM, SSM_STATE), F32), *ssm_w)
        x1, h2, lg = merge(o_attn, y_ssm, xp, mod_p, 0, anw, wo, n2w, w_router[l], TOKEN_TILE)
        w_t, pos_t, cnt = route(lg, e_bias[l], TOKEN_TILE)
        outs[0].append(jnp.transpose(kvt.reshape(bp, 4, N_KV_HEADS, HEAD_DIM, tp), (0, 4, 1, 2, 3)))
        outs[1].append(win.reshape(bp, tp, 2, N_KV_HEADS, HEAD_DIM)[:, tp - WINDOW:])
        outs[2].append(h_new)
        outs[3].append(conv_new)

        qp, kvb, win, winb, z, xbc, misc, kv = proj_q
        assert past_len // SEL_BLOCK == (past_len + tq_pad - 1) // SEL_BLOCK == N_SEL_LANES

        def sample_select():
            return cmp_select(padq(qp), compress(cmpx_q, *cmp_w), past_len, tq_pad)

        xp, (o_cmp, mneg) = moe_sorted(h2, w_t, pos_t, cnt, x1, mod_p, 0, tp, wgu, wd, sgu, sd, TOKEN_TILE,
                                       sample_select)
        xp = xp.reshape(bp, tp, d)
        win_all = jnp.concatenate([cache_win[l].reshape(bq, WINDOW, 2 * KV_WIDTH).astype(BF16), rq(winb),
                                   jnp.zeros((bq, -(WINDOW + tq_pad) % WIN_CHUNK + tq_pad - tq, 2 * KV_WIDTH), BF16)],
                                  axis=1)
        o_attn = sel_win_attention(padq(qp), mneg, past, 0, win_all, o_cmp, padq(misc), past_len,
                                   past_len - WINDOW, tq_pad)[:, :tq]
        y_ssm, h_new, conv_new = ssd(rq(xbc), rq(z), rq(misc), state_conv[l], state_ssm[l], *ssm_w)
        x1, h2, lg = merge(o_attn.reshape(1, nq, ATTN_WIDTH), y_ssm.reshape(1, nq, SSM_WIDTH), xq1, mod_q, 0,
                           anw, wo, n2w, w_router[l], nq)
        w_t, pos_t, cnt = route(lg, e_bias[l], nq)
        xq = moe(h2, w_t, pos_t, cnt, x1, mod_q, 0, nq, wgu, wd, sgu, sd, nq).reshape(bq, tq, d)
        win_rows = win.reshape(bq, tq, 2, N_KV_HEADS, HEAD_DIM)
        outs[4].append(kv.reshape(bq, tq, 4, N_KV_HEADS, HEAD_DIM))
        outs[5].append(jnp.concatenate([cache_win[l], win_rows.astype(cache_win.dtype)], axis=1)[:, tq:])
        outs[6].append(h_new)
        outs[7].append(conv_new)
    return (xp, xq) + tuple(jnp.stack(o) for o in outs)
```

```python
import functools
import math

import jax
import jax.numpy as jnp
import numpy as np
from jax import lax
from jax.experimental import pallas as pl
from jax.experimental.pallas import tpu as pltpu
from jax.experimental.pallas import tpu_sc as plsc

D_MODEL = 1024
PAGE_SIZE = 128
HEAD_DIM = 64
N_Q_HEADS = 8
N_KV_HEADS = 2
GQA_GROUP = N_Q_HEADS // N_KV_HEADS
ATTN_WIDTH = N_Q_HEADS * HEAD_DIM
KV_WIDTH = N_KV_HEADS * HEAD_DIM
ROPE_DIM = HEAD_DIM // 4
ROPE_THETA = 500000.0
CMP_LEN = 32
CMP_STRIDE = 16
CMP_HIDDEN = 4 * HEAD_DIM
SEL_BLOCK = 64
N_SEL = 16
N_LOCAL = 2
WINDOW = 512
SSM_HEADS = 8
SSM_HEAD_DIM = 64
SSM_WIDTH = SSM_HEADS * SSM_HEAD_DIM
SSM_GROUPS = 2
SSM_STATE = 128
CONV_WIDTH = 4
CONV_DIM = SSM_WIDTH + 2 * SSM_GROUPS * SSM_STATE
SSD_CHUNK = 128
MIX_WIDTH = ATTN_WIDTH + SSM_WIDTH
N_EXPERTS = 64
N_EXPERT_GROUPS = 8
TOPK_GROUPS = 4
TOP_K = 8
D_EXPERT = 256
D_SHARED = 256
ROUTED_SCALE = 2.5
IN_SIZES = (ATTN_WIDTH, 6 * KV_WIDTH, 3 * N_Q_HEADS, SSM_WIDTH, CONV_DIM, SSM_HEADS)
N_IN = sum(IN_SIZES)
EPS = 1e-6
NEG = -1e30
BIG = 1e6

LANES = 128
TOKEN_TILE = 512
QUERY_TILE = 128
VMEM_LIMIT = 56 * 1024 * 1024

BF16 = jnp.bfloat16
F32 = jnp.float32
LOG2E = math.log2(math.e)


def _cparams(sem, flags=None):
    return pltpu.CompilerParams(dimension_semantics=sem, vmem_limit_bytes=VMEM_LIMIT, flags=flags)


def _silu(x):
    return x * jax.nn.sigmoid(x)


def _dot(a, b):
    return jnp.dot(a, b, preferred_element_type=F32)


def _dot_nt(a, b):
    return lax.dot_general(a, b, (((1,), (1,)), ((), ())), preferred_element_type=F32)


def _mod_spec(mod, col, tm, tiles_per_b, row0):
    if mod.ndim == 3:
        return pl.BlockSpec((1, 1, D_MODEL), lambda i, *_: (row0 + i // tiles_per_b, 0, col))
    return pl.BlockSpec((tm, D_MODEL), lambda i, *_: (i, col))


def _mod(ref):
    return ref[0] if len(ref.shape) == 3 else ref[...]


def _adaln_kernel(c_ref, w_ref, b_ref, o_ref):
    c = c_ref[...]
    a = _silu(c).astype(BF16)
    o_ref[...] = _dot(a, w_ref[...].astype(BF16)) + b_ref[...]


def adaln_all(c_all, w_ada, b_ada):
    rows = c_all.shape[0]
    n = w_ada.shape[1]
    tn = 1024
    return pl.pallas_call(
        _adaln_kernel,
        grid=(n // tn,),
        in_specs=[
            pl.BlockSpec((rows, D_MODEL), lambda j: (0, 0)),
            pl.BlockSpec((D_MODEL, tn), lambda j: (0, j)),
            pl.BlockSpec((1, tn), lambda j: (0, j)),
        ],
        out_specs=pl.BlockSpec((rows, tn), lambda j: (0, j)),
        out_shape=jax.ShapeDtypeStruct((rows, n), F32),
        compiler_params=_cparams(("arbitrary",)),
        name="adaln",
    )(c_all, w_ada, b_ada.reshape(1, n))


_C_Q = 0
_C_KV = _C_Q + ATTN_WIDTH
_C_Z = _C_KV + 6 * KV_WIDTH
_C_XBC = _C_Z + SSM_WIDTH
_C_MISC = _C_XBC + CONV_DIM
N_IN_PAD = _C_MISC + LANES
N_GATES = 3 * N_Q_HEADS


def _prep_w_in(w_in):
    s = np.cumsum((0,) + IN_SIZES)
    q, kv, g, z, xbc, dt = (w_in[:, int(s[i]):int(s[i + 1])] for i in range(6))
    pad = jnp.zeros((w_in.shape[0], LANES - N_GATES - SSM_HEADS), w_in.dtype)
    return jnp.concatenate([q, kv, z, xbc, dt, g, pad], axis=1).astype(BF16)


def _group_mean_matrix(width):
    i = np.arange(width)
    m = (i[:, None] // HEAD_DIM == i[None, :] // HEAD_DIM).astype(np.float32) / HEAD_DIM
    return jnp.asarray(m, BF16)


def _rope_tables(pos):
    half = ROPE_DIM // 2
    inv_freq = ROPE_THETA ** (-jnp.arange(half, dtype=F32) / half)
    ang = pos.astype(F32)[:, None] * inv_freq[None, :]
    cos, sin = jnp.cos(ang), jnp.sin(ang)
    t = pos.shape[0]
    one = jnp.ones((t, HEAD_DIM - ROPE_DIM), F32)
    zero = jnp.zeros((t, HEAD_DIM - ROPE_DIM), F32)
    zh = jnp.zeros((t, half), F32)
    c = jnp.concatenate([cos, cos, one], axis=1)
    s_up = jnp.concatenate([-sin, zh, zero], axis=1)
    s_dn = jnp.concatenate([zh, sin, zero], axis=1)
    rep = LANES // HEAD_DIM
    return jnp.tile(c, (1, rep)), jnp.tile(s_up, (1, rep)), jnp.tile(s_dn, (1, rep))


def _rope(x, c, s_up, s_dn):
    w = x.shape[1]
    half = ROPE_DIM // 2
    rep = w // LANES
    ct = jnp.concatenate([c] * rep, axis=1) if rep > 1 else c
    su = jnp.concatenate([s_up] * rep, axis=1) if rep > 1 else s_up
    sd = jnp.concatenate([s_dn] * rep, axis=1) if rep > 1 else s_dn
    up = pltpu.roll(x, w - half, axis=1)
    dn = pltpu.roll(x, half, axis=1)
    return x * ct + up * su + dn * sd


def _stride_block_store(stage_ref, cmpx_ref, n_rows):
    nb = n_rows // CMP_STRIDE
    lane = lax.broadcasted_iota(jnp.int32, (nb, KV_WIDTH), 1)
    lo = lane < HEAD_DIM
    span = CMP_STRIDE * HEAD_DIM
    for s in range(2):
        for m in range(CMP_STRIDE // 2):
            r0 = stage_ref[s, pl.ds(2 * m, nb, stride=CMP_STRIDE), :]
            r1 = stage_ref[s, pl.ds(2 * m + 1, nb, stride=CMP_STRIDE), :]
            head0 = jnp.where(lo, r0, pltpu.roll(r1, HEAD_DIM, axis=1))
            head1 = jnp.where(lo, pltpu.roll(r0, HEAD_DIM, axis=1), r1)
            for h, piece in enumerate((head0, head1)):
                c0 = (2 * s + h) * span + m * KV_WIDTH
                cmpx_ref[0, :, c0:c0 + KV_WIDTH] = piece.astype(BF16)


def _inproj_kernel(x_ref, shift_ref, scale_ref, nw_ref, w_ref, qw_ref, kw_ref, gq_ref, gk_ref,
                   c_ref, su_ref, sd_ref,
                   qp_ref, kvb_ref, win_ref, winb_ref, z_ref, xbc_ref, misc_ref, *rest, seq_layout):
    x = x_ref[...]
    ms = jnp.mean(x * x, axis=-1, keepdims=True)
    h = x * lax.rsqrt(ms + EPS) * nw_ref[...]
    h = h * (1.0 + _mod(scale_ref)) + _mod(shift_ref)
    hb = h.astype(BF16)
    c, su, sd = c_ref[...], su_ref[...], sd_ref[...]

    q = _dot(hb, w_ref[:, _C_Q:_C_Q + ATTN_WIDTH])
    qms = _dot((q * q).astype(BF16), gq_ref[...])
    q = q * lax.rsqrt(qms + EPS) * qw_ref[...]
    q = _rope(q, c, su, sd) * (HEAD_DIM ** -0.5 * LOG2E)
    lane = lax.broadcasted_iota(jnp.int32, q.shape, 1) % LANES
    lo = lane < HEAD_DIM
    q_up = pltpu.roll(q, ATTN_WIDTH - HEAD_DIM, axis=1)
    q_dn = pltpu.roll(q, HEAD_DIM, axis=1)
    zero = jnp.zeros_like(q)
    nat_lo = jnp.where(lo, q, zero)
    nat_hi = jnp.where(lo, zero, q)
    up_lo = jnp.where(lo, q_up, zero)
    dn_hi = jnp.where(lo, zero, q_dn)
    blocks = []
    for hd in range(N_Q_HEADS):
        pair = hd // 2
        sl = slice(pair * LANES, (pair + 1) * LANES)
        if hd < GQA_GROUP:
            blocks.append((nat_lo if hd % 2 == 0 else up_lo)[:, sl])
        else:
            blocks.append((dn_hi if hd % 2 == 0 else nat_hi)[:, sl])
    qp_ref[...] = jnp.concatenate(blocks, axis=1).astype(BF16)

    kv = _dot(hb, w_ref[:, _C_KV:_C_KV + 6 * KV_WIDTH])
    outs = []
    for br in range(3):
        k = kv[:, br * 2 * KV_WIDTH:br * 2 * KV_WIDTH + KV_WIDTH]
        v = kv[:, br * 2 * KV_WIDTH + KV_WIDTH:(br + 1) * 2 * KV_WIDTH]
        kms = _dot((k * k).astype(BF16), gk_ref[...])
        k = k * lax.rsqrt(kms + EPS) * kw_ref[:, br * KV_WIDTH:(br + 1) * KV_WIDTH]
        k = _rope(k, c, su, sd)
        outs += [k, v]
    kvrows = jnp.concatenate(outs[:4], axis=1)
    winrows = jnp.concatenate(outs[4:], axis=1)
    kvb_ref[...] = kvrows.astype(BF16)
    win_ref[...] = winrows
    winb_ref[...] = winrows.astype(BF16)
    if seq_layout:
        kvt_ref, cmpx_ref, stage_ref = rest
        tm = kvrows.shape[0]
        for r in range(4):
            kvt_ref[0, r] = jnp.transpose(kvrows[:, r * KV_WIDTH:(r + 1) * KV_WIDTH])
        for s in range(2):
            stage_ref[s] = kvrows[:, s * KV_WIDTH:(s + 1) * KV_WIDTH]
        _stride_block_store(stage_ref, cmpx_ref, tm)
    else:
        rest[0][...] = kvrows

    z_ref[...] = _dot(hb, w_ref[:, _C_Z:_C_Z + SSM_WIDTH])
    xbc_ref[...] = _dot(hb, w_ref[:, _C_XBC:_C_XBC + CONV_DIM])
    misc_ref[...] = _dot(hb, w_ref[:, _C_MISC:_C_MISC + LANES])


def inproj(x, mod3, mod_row0, norm_w, wp, q_norm_w, k_norm_w, pos, tm, seq_layout):
    b, t, d = x.shape
    n = b * t
    tiles_per_b = t // tm
    xf = x.reshape(n, d)
    c, su, sd = _rope_tables(pos)
    qw = jnp.tile(q_norm_w, N_Q_HEADS).reshape(1, ATTN_WIDTH)
    kw = jnp.concatenate([jnp.tile(k_norm_w[i], N_KV_HEADS) for i in range(3)]).reshape(1, 3 * KV_WIDTH)
    gq = _group_mean_matrix(ATTN_WIDTH)
    gk = _group_mean_matrix(KV_WIDTH)

    def mod_spec(col):
        return _mod_spec(mod3, col, tm, tiles_per_b, mod_row0)

    def tok(wd):
        return pl.BlockSpec((tm, wd), lambda i: (i, 0))

    def full(a):
        return pl.BlockSpec(a.shape, lambda i: (0,) * a.ndim)

    rope_spec = pl.BlockSpec((tm, LANES), lambda i: (i % tiles_per_b, 0))
    out_shape = [
        jax.ShapeDtypeStruct((n, N_Q_HEADS * LANES), BF16),
        jax.ShapeDtypeStruct((n, 4 * KV_WIDTH), BF16),
        jax.ShapeDtypeStruct((n, 2 * KV_WIDTH), F32),
        jax.ShapeDtypeStruct((n, 2 * KV_WIDTH), BF16),
        jax.ShapeDtypeStruct((n, SSM_WIDTH), F32),
        jax.ShapeDtypeStruct((n, CONV_DIM), F32),
        jax.ShapeDtypeStruct((n, LANES), F32),
    ]
    out_specs = [tok(s.shape[1]) for s in out_shape]
    scratch = []
    if seq_layout:
        out_shape += [jax.ShapeDtypeStruct((b, 4, KV_WIDTH, t), F32),
                      jax.ShapeDtypeStruct((b, t // CMP_STRIDE, CMP_STRIDE * 2 * KV_WIDTH), BF16)]
        out_specs += [pl.BlockSpec((1, 4, KV_WIDTH, tm), lambda i: (i // tiles_per_b, 0, 0, i % tiles_per_b)),
                      pl.BlockSpec((1, tm // CMP_STRIDE, CMP_STRIDE * 2 * KV_WIDTH),
                                   lambda i: (i // tiles_per_b, i % tiles_per_b, 0))]
        scratch = [pltpu.VMEM((2, tm, KV_WIDTH), F32)]
    else:
        out_shape += [jax.ShapeDtypeStruct((n, 4 * KV_WIDTH), F32)]
        out_specs += [tok(4 * KV_WIDTH)]
    return pl.pallas_call(
        functools.partial(_inproj_kernel, seq_layout=seq_layout),
        grid=(n // tm,),
        in_specs=[tok(d), mod_spec(0), mod_spec(1), full(norm_w), full(wp), full(qw), full(kw), full(gq), full(gk),
                  rope_spec, rope_spec, rope_spec],
        out_specs=tuple(out_specs),
        out_shape=tuple(out_shape),
        scratch_shapes=scratch,
        compiler_params=_cparams(("arbitrary",)),
        name="inproj",
    )(xf, mod3, mod3, norm_w, wp, qw, kw, gq, gk, c, su, sd)


def _prep_compress(cmp_pe, cmp_w1, cmp_w2):
    span = CMP_STRIDE * HEAD_DIM
    w1p = jnp.concatenate([cmp_w1[:, :span], cmp_w1[:, span:]], axis=2).astype(BF16)
    pep = cmp_pe.reshape(2, 2, span)
    eye = jnp.eye(N_KV_HEADS, dtype=F32)
    w2p = jnp.einsum("poe,hg->phoge", cmp_w2, eye).reshape(2, N_KV_HEADS, CMP_HIDDEN, KV_WIDTH).astype(BF16)
    return w1p, pep, w2p


def _compress_kernel(x_ref, w1_ref, pe_ref, w2_ref, o_ref):
    part = pl.program_id(1)
    nb = x_ref.shape[1]
    span = CMP_STRIDE * HEAD_DIM
    pe = pe_ref[0]
    out = jnp.zeros((nb, KV_WIDTH), F32)
    for h in range(N_KV_HEADS):
        xk = x_ref[0, :, h * span:(h + 1) * span]
        xv = x_ref[0, :, (N_KV_HEADS + h) * span:(N_KV_HEADS + h + 1) * span]
        x = jnp.where(part == 0, xk, xv).astype(F32)
        u = _dot((x + pe[0:1]).astype(BF16), w1_ref[0, :, :CMP_HIDDEN])
        v = _dot((x + pe[1:2]).astype(BF16), w1_ref[0, :, CMP_HIDDEN:])
        h1 = u + pltpu.roll(v, nb - 1, axis=0)
        out = out + _dot(_silu(h1).astype(BF16), w2_ref[0, h])
    row = lax.broadcasted_iota(jnp.int32, out.shape, 0)
    o_ref[0, 0] = jnp.where(row < nb - 1, out, 0.0).astype(o_ref.dtype)


def compress(x, w1p, pep, w2p):
    b, nb, width = x.shape
    return pl.pallas_call(
        _compress_kernel,
        grid=(b, 2),
        in_specs=[
            pl.BlockSpec((1, nb, width), lambda i, p: (i, 0, 0)),
            pl.BlockSpec((1,) + w1p.shape[1:], lambda i, p: (p, 0, 0)),
            pl.BlockSpec((1,) + pep.shape[1:], lambda i, p: (p, 0, 0)),
            pl.BlockSpec((1,) + w2p.shape[1:], lambda i, p: (p, 0, 0, 0)),
        ],
        out_specs=pl.BlockSpec((1, 1, nb, KV_WIDTH), lambda i, p: (i, p, 0, 0)),
        out_shape=jax.ShapeDtypeStruct((b, 2, nb, KV_WIDTH), BF16),
        compiler_params=_cparams(("arbitrary", "arbitrary")),
        name="compress",
    )(x, w1p, pep, w2p)


N_SEL_LANES = LANES


def _cover_matrix(nb):
    c = np.arange(nb)[:, None]
    j = np.arange(N_SEL_LANES)[None, :]
    start = c * CMP_STRIDE
    m = (start < (j + 1) * SEL_BLOCK) & (start + CMP_LEN > j * SEL_BLOCK)
    return jnp.asarray(m.astype(np.float32), BF16)


def _place_heads(res, kv):
    lane = lax.broadcasted_iota(jnp.int32, res[0].shape, 1)
    lo = lane < HEAD_DIM
    blocks = []
    for pair in range(GQA_GROUP // 2):
        a, b = res[2 * pair], res[2 * pair + 1]
        if kv == 0:
            blocks.append(jnp.where(lo, a, pltpu.roll(b, HEAD_DIM, axis=1)))
        else:
            blocks.append(jnp.where(lo, pltpu.roll(a, HEAD_DIM, axis=1), b))
    return jnp.concatenate(blocks, axis=1)


def _group_rows(q_ref, kv):
    heads = range(kv * GQA_GROUP, (kv + 1) * GQA_GROUP)
    return jnp.concatenate([q_ref[0, :, hd * LANES:(hd + 1) * LANES] for hd in heads], axis=0)


def _heads_from_transposed(out_t, tq, kv):
    out = jnp.transpose(out_t)
    return _place_heads([out[g * tq:(g + 1) * tq] for g in range(GQA_GROUP)], kv)


def _cmp_select_kernel(q_ref, kc_ref, vc_ref, covt_ref, o_ref, m_ref, *, q_off):
    tq = q_ref.shape[1]
    rows = GQA_GROUP * tq
    nb = kc_ref.shape[2]
    wl = max(tq, LANES)
    assert tq % LANES == 0 or rows == LANES
    t0 = q_off + pl.program_id(1) * tq
    kc = kc_ref[0, 0]
    vc = vc_ref[0, 0]
    qpos = t0 + lax.broadcasted_iota(jnp.int32, (nb, rows), 1) % tq
    cend = lax.broadcasted_iota(jnp.int32, (nb, rows), 0) * CMP_STRIDE + (CMP_LEN - 1)
    valid = cend <= qpos
    blk = lax.broadcasted_iota(jnp.int32, (N_SEL_LANES, wl), 0)
    cur = (t0 + lax.broadcasted_iota(jnp.int32, (N_SEL_LANES, wl), 1) % tq) // SEL_BLOCK
    forced = (blk == 0) | ((blk <= cur) & (blk > cur - N_LOCAL))
    o_groups = []
    for kv in range(N_KV_HEADS):
        s = _dot_nt(kc, _group_rows(q_ref, kv))
        s = jnp.where(valid, s, NEG)
        e = jnp.exp2(s - jnp.max(s, axis=0, keepdims=True))
        p = e / jnp.sum(e, axis=0, keepdims=True)
        p = jnp.where(valid, p, 0.0)
        o_t = lax.dot_general(vc, p.astype(BF16), (((0,), (0,)), ((), ())), preferred_element_type=F32)
        o_groups.append(_heads_from_transposed(o_t, tq, kv))
        if tq % LANES == 0:
            psum = sum(p[:, g * tq:(g + 1) * tq] for g in range(GQA_GROUP))
        else:
            psum = p + sum(pltpu.roll(p, g * tq, axis=1) for g in range(1, GQA_GROUP))
        hi, lo = _split2(psum)
        imp = _dot(covt_ref[...], hi) + _dot(covt_ref[...], lo)
        x = jnp.where(forced, -jnp.inf, jnp.where(blk > cur, -BIG, imp))
        sel = forced
        for _ in range(N_SEL - 1 - N_LOCAL):
            mx = jnp.max(x, axis=0, keepdims=True)
            idx = jnp.min(jnp.where(x == mx, blk, N_SEL_LANES), axis=0, keepdims=True)
            hit = blk == idx
            sel = sel | hit
            x = jnp.where(hit, -jnp.inf, x)
        mneg = jnp.transpose(jnp.where(sel, 0.0, NEG))
        m_ref[0, kv] = mneg[:tq].astype(m_ref.dtype)
    o_ref[0] = jnp.concatenate(o_groups, axis=1)


def cmp_select(qp, kcv, q_off, tq):
    b, t, _ = qp.shape
    nb = kcv.shape[2]
    cover = jnp.transpose(_cover_matrix(nb))
    return pl.pallas_call(
        functools.partial(_cmp_select_kernel, q_off=q_off),
        grid=(b, t // tq),
        in_specs=[
            pl.BlockSpec((1, tq, N_Q_HEADS * LANES), lambda i, j: (i, j, 0)),
            pl.BlockSpec((1, 1, nb, KV_WIDTH), lambda i, j: (i, 0, 0, 0)),
            pl.BlockSpec((1, 1, nb, KV_WIDTH), lambda i, j: (i, 1, 0, 0)),
            pl.BlockSpec((N_SEL_LANES, nb), lambda i, j: (0, 0)),
        ],
        out_specs=(
            pl.BlockSpec((1, tq, ATTN_WIDTH), lambda i, j: (i, j, 0)),
            pl.BlockSpec((1, N_KV_HEADS, tq, N_SEL_LANES), lambda i, j: (i, 0, j, 0)),
        ),
        out_shape=(
            jax.ShapeDtypeStruct((b, t, ATTN_WIDTH), F32),
            jax.ShapeDtypeStruct((b, N_KV_HEADS, t, N_SEL_LANES), BF16),
        ),
        compiler_params=_cparams(("arbitrary", "arbitrary")),
        name="cmp_select",
    )(qp, kcv, kcv, cover)


SEL_TILE_ELEMS = 512 * 512
SEL_WIDTHS = (8, 4, 2, 1)
WIN_CHUNK = 256
WIN_SPAN = 3


def _block_onehot(s):
    key = np.arange(s)[:, None]
    j = np.arange(N_SEL_LANES)[None, :]
    return jnp.asarray((key // SEL_BLOCK == j).astype(np.float32), BF16)


def _gate_expand():
    m = np.zeros((3, LANES, ATTN_WIDTH), np.float32)
    for br in range(3):
        for hd in range(N_Q_HEADS):
            m[br, SSM_HEADS + 3 * hd + br, hd * HEAD_DIM:(hd + 1) * HEAD_DIM] = 1.0
    return jnp.asarray(m, BF16)


def _flash_update(ss, v, m_ref, acc_ref):
    lane = lax.broadcasted_iota(jnp.int32, v.shape, 1)
    one = jnp.ones(v.shape, v.dtype)
    stage = []
    for k, s in enumerate(ss):
        m_old = m_ref[k]
        m_new = jnp.maximum(m_old, jnp.max(s, axis=0, keepdims=True))
        alpha = jnp.exp2(m_old - m_new)
        p = jnp.exp2(s - m_new)
        m_ref[k] = m_new
        stage.append((alpha, p.astype(BF16)))
    for k, (alpha, p) in enumerate(stage):
        vk = jnp.where((lane < HEAD_DIM) == (k == 0), v, one)
        pv = lax.dot_general(vk, p, (((0,), (0,)), ((), ())), preferred_element_type=F32)
        acc_ref[k] = alpha * acc_ref[k] + pv


def _sel_chunk(rows, n_keys):
    chunk = SEL_TILE_ELEMS // rows
    while n_keys % chunk:
        chunk //= 2
    return chunk


def _sel_win_kernel(q_ref, mneg_ref, ksel_ref, vsel_ref, et_ref, kwin_ref, vwin_ref, ocmp_ref, misc_ref, eg_ref,
                    o_ref, lhs_ref, m_ref, acc_ref, *, q_off, win_pos0):
    tq = q_ref.shape[1]
    rows = GQA_GROUP * tq
    SEL_CHUNK = _sel_chunk(rows, ksel_ref.shape[1])
    t0 = q_off + pl.program_id(1) * tq
    n_sel = lax.shift_right_logical(t0 + tq - 1, int(math.log2(SEL_CHUNK))) + 1
    w_lo = jnp.maximum(t0 - (WINDOW - 1) - win_pos0, 0) // WIN_CHUNK
    w_hi = (t0 + tq - 1 - win_pos0) // WIN_CHUNK + 1

    def qrow(n_keys):
        return lax.broadcasted_iota(jnp.int32, (n_keys, rows), 1) % tq + t0

    def init():
        m_ref[...] = jnp.full(m_ref.shape, NEG, F32)
        acc_ref[...] = jnp.zeros(acc_ref.shape, F32)

    def finish():
        outs = []
        for kv in range(N_KV_HEADS):
            acc = acc_ref[kv]
            denom_row = HEAD_DIM * (1 - kv)
            outs.append(_heads_from_transposed(acc / acc[denom_row:denom_row + 1, :], tq, kv))
        return jnp.concatenate(outs, axis=1)

    for kv in range(N_KV_HEADS):
        for g in range(GQA_GROUP):
            hd = kv * GQA_GROUP + g
            lhs_ref[kv, g * tq:(g + 1) * tq, :LANES] = q_ref[0, :, hd * LANES:(hd + 1) * LANES]
            lhs_ref[kv, g * tq:(g + 1) * tq, LANES:] = mneg_ref[0, kv]

    init()

    def sel_step(i, carry, causal, width):
        n_keys = width * SEL_CHUNK
        r0 = pl.multiple_of(i * n_keys, n_keys)
        rhs = jnp.concatenate([ksel_ref[0, pl.ds(r0, n_keys), :], et_ref[pl.ds(r0, n_keys), :]], axis=1)
        v = vsel_ref[0, pl.ds(r0, n_keys), :]
        if causal:
            ok = r0 + lax.broadcasted_iota(jnp.int32, (n_keys, rows), 0) <= qrow(n_keys)
        ss = [_dot_nt(rhs, lhs_ref[kv]) for kv in range(N_KV_HEADS)]
        if causal:
            ss = [jnp.where(ok, s, NEG) for s in ss]
        _flash_update(ss, v, m_ref, acc_ref)
        return carry

    n_full = lax.shift_right_logical(t0 + 1, int(math.log2(SEL_CHUNK)))
    done = 0
    for width in SEL_WIDTHS:
        n_steps = (n_full - done) // width
        lax.fori_loop(done // width, done // width + n_steps,
                      functools.partial(sel_step, causal=False, width=width), 0)
        done = done + n_steps * width
    lax.fori_loop(n_full, n_sel, functools.partial(sel_step, causal=True, width=1), 0)
    o_sel = finish()

    init()

    def win_step(c, carry, width):
        n_keys = width * WIN_CHUNK
        r0 = pl.multiple_of(c * WIN_CHUNK, WIN_CHUNK)
        k = kwin_ref[0, pl.ds(r0, n_keys), :]
        v = vwin_ref[0, pl.ds(r0, n_keys), :]
        wpos = win_pos0 + r0 + lax.broadcasted_iota(jnp.int32, (n_keys, rows), 0)
        qr = qrow(n_keys)
        ok = (wpos <= qr) & (wpos > qr - WINDOW)
        ss = [jnp.where(ok, _dot_nt(k, lhs_ref[kv, :, :LANES]), NEG) for kv in range(N_KV_HEADS)]
        _flash_update(ss, v, m_ref, acc_ref)
        return carry

    n_span = (w_hi - w_lo) // WIN_SPAN
    lax.fori_loop(w_lo, w_lo + n_span, functools.partial(win_step, width=WIN_SPAN), 0)
    lax.fori_loop(w_lo + n_span * WIN_SPAN, w_hi, functools.partial(win_step, width=1), 0)
    o_win = finish()

    gates = jax.nn.sigmoid(misc_ref[0])
    ghi = gates.astype(BF16)
    glo = (gates - ghi.astype(F32)).astype(BF16)
    branches = (ocmp_ref[0], o_sel, o_win)
    out = jnp.zeros(branches[0].shape, F32)
    for br in range(3):
        out = out + (_dot(ghi, eg_ref[br]) + _dot(glo, eg_ref[br])) * branches[br]
    o_ref[0] = out


def sel_win_attention(qp, mneg, kvb, sel_col, winb, o_cmp, misc, q_off, win_pos0, tq):
    b, t, _ = qp.shape
    s = kvb.shape[1]
    sw = winb.shape[1]
    et = _block_onehot(s)
    eg = _gate_expand()
    rows = GQA_GROUP * tq
    assert q_off + t <= s and q_off + t - win_pos0 <= sw and sw % WIN_CHUNK == 0
    return pl.pallas_call(
        functools.partial(_sel_win_kernel, q_off=q_off, win_pos0=win_pos0),
        grid=(b, t // tq),
        in_specs=[
            pl.BlockSpec((1, tq, N_Q_HEADS * LANES), lambda i, j: (i, j, 0)),
            pl.BlockSpec((1, N_KV_HEADS, tq, N_SEL_LANES), lambda i, j: (i, 0, j, 0)),
            pl.BlockSpec((1, s, KV_WIDTH), lambda i, j: (i, 0, sel_col)),
            pl.BlockSpec((1, s, KV_WIDTH), lambda i, j: (i, 0, sel_col + 1)),
            pl.BlockSpec((s, N_SEL_LANES), lambda i, j: (0, 0)),
            pl.BlockSpec((1, sw, KV_WIDTH), lambda i, j: (i, 0, 0)),
            pl.BlockSpec((1, sw, KV_WIDTH), lambda i, j: (i, 0, 1)),
            pl.BlockSpec((1, tq, ATTN_WIDTH), lambda i, j: (i, j, 0)),
            pl.BlockSpec((1, tq, LANES), lambda i, j: (i, j, 0)),
            pl.BlockSpec((3, LANES, ATTN_WIDTH), lambda i, j: (0, 0, 0)),
        ],
        out_specs=pl.BlockSpec((1, tq, ATTN_WIDTH), lambda i, j: (i, j, 0)),
        out_shape=jax.ShapeDtypeStruct((b, t, ATTN_WIDTH), F32),
        scratch_shapes=[
            pltpu.VMEM((N_KV_HEADS, rows, 2 * LANES), BF16),
            pltpu.VMEM((N_KV_HEADS, 1, rows), F32),
            pltpu.VMEM((N_KV_HEADS, LANES, rows), F32),
        ],
        compiler_params=_cparams(("arbitrary", "arbitrary")),
        name="sel_win_attention",
    )(qp, mneg, kvb, kvb, et, winb, winb, o_cmp, misc, eg)


CONV_PAD = 8
HEAD_PAIRS = SSM_HEADS // 2


def _split3(x):
    a = x.astype(BF16)
    r = x - a.astype(F32)
    b = r.astype(BF16)
    c = (r - b.astype(F32)).astype(BF16)
    return a, b, c


def _ssd_kernel(xbc_ref, z_ref, misc_ref, conv0_ref, h0_ref, cw_ref, cb_ref, dtb_ref, a_ref, dsk_ref, nw_ref,
                y_ref, hout_ref, cout_ref, xp_ref, h_ref, ms_ref, *, t_valid):
    ch = pl.program_id(1)
    L = SSD_CHUNK
    keep = CONV_WIDTH - 1

    @pl.when(ch == 0)
    def _():
        xp_ref[...] = jnp.zeros(xp_ref.shape, F32)
        xp_ref[CONV_PAD - keep:CONV_PAD, :] = conv0_ref[0]
        h_ref[...] = h0_ref[0]

    xp_ref[CONV_PAD:CONV_PAD + t_valid, :] = xbc_ref[0]
    conv = cb_ref[...]
    for j in range(CONV_WIDTH):
        conv = conv + cw_ref[j:j + 1, :] * xp_ref[CONV_PAD - keep + j:CONV_PAD - keep + j + L, :]
    last = xp_ref[CONV_PAD + t_valid - keep:CONV_PAD + t_valid, :]
    cout_ref[0] = last
    xp_ref[CONV_PAD - keep:CONV_PAD, :] = last
    xc = _silu(conv)

    row = lax.broadcasted_iota(jnp.int32, (L, LANES), 0)
    lane = lax.broadcasted_iota(jnp.int32, (L, LANES), 1)
    if t_valid == L:
        raw = misc_ref[0]
    else:
        ms_ref[...] = jnp.zeros(ms_ref.shape, F32)
        ms_ref[0:t_valid, :] = misc_ref[0]
        raw = ms_ref[...]
    v = raw + dtb_ref[...]
    dt = jnp.maximum(v, 0.0) + jnp.log(1.0 + jnp.exp(-jnp.abs(v)))
    dt = jnp.where((lane < SSM_HEADS) & (row < t_valid), dt, 0.0)
    da = dt * a_ref[...]
    tri = (lax.broadcasted_iota(jnp.int32, (L, L), 1) <= lax.broadcasted_iota(jnp.int32, (L, L), 0))
    trib = tri.astype(BF16)
    acum = sum(_dot(trib, part) for part in _split3(da))
    acum_t = jnp.transpose(acum)
    dt_t = jnp.transpose(dt)
    e_acum = jnp.exp(acum)
    e_last = jnp.exp(acum[L - 1:L, :])
    w_end = jnp.exp(acum[L - 1:L, :] - acum) * dt
    lo = lane < SSM_HEAD_DIM

    ys = []
    for pair in range(HEAD_PAIRS):
        grp = (2 * pair) // (SSM_HEADS // SSM_GROUPS)
        bg = xc[:, SSM_WIDTH + grp * SSM_STATE:SSM_WIDTH + (grp + 1) * SSM_STATE].astype(BF16)
        cg = xc[:, SSM_WIDTH + (SSM_GROUPS + grp) * SSM_STATE:SSM_WIDTH + (SSM_GROUPS + grp + 1) * SSM_STATE].astype(BF16)
        g = _dot_nt(cg, bg)
        xpair = xc[:, pair * LANES:(pair + 1) * LANES]
        y = jnp.zeros((L, LANES), F32)
        for sub in range(2):
            hd = 2 * pair + sub
            seg = acum[:, hd:hd + 1] - acum_t[hd:hd + 1, :]
            m = g * jnp.exp(jnp.where(tri, seg, NEG)) * dt_t[hd:hd + 1, :]
            xm = jnp.where(lo if sub == 0 else ~lo, xpair, 0.0)
            y = y + _dot(m.astype(BF16), xm.astype(BF16))
        col = lambda a: jnp.where(lo, a[:, 2 * pair:2 * pair + 1], a[:, 2 * pair + 1:2 * pair + 2])
        hp = h_ref[pair]
        y = y + _dot_nt(cg, hp.astype(BF16)) * col(e_acum)
        y = y + col(dsk_ref[...]) * xpair
        xw = (xpair * col(w_end)).astype(BF16)
        st = lax.dot_general(xw, bg, (((0,), (0,)), ((), ())), preferred_element_type=F32)
        prow = lax.broadcasted_iota(jnp.int32, (LANES, LANES), 0) < SSM_HEAD_DIM
        dec = jnp.where(prow, e_last[:, 2 * pair:2 * pair + 1], e_last[:, 2 * pair + 1:2 * pair + 2])
        h_ref[pair] = hp * dec + st
        ys.append(y)
    y = jnp.concatenate(ys, axis=1)
    if t_valid != L:
        y = y[:t_valid]
    y = y * _silu(z_ref[0])
    y = y * lax.rsqrt(jnp.mean(y * y, axis=-1, keepdims=True) + EPS) * nw_ref[...]
    y_ref[0] = y

    @pl.when(ch == pl.num_programs(1) - 1)
    def _():
        hout_ref[0] = h_ref[...]


def ssd(xbc, z, misc, conv0, h0, conv_w, conv_b, dt_bias, a_log, d_skip, norm_w):
    b, t, _ = xbc.shape
    L = SSD_CHUNK
    t_valid = L if t % L == 0 else t
    assert t_valid == L or t < L
    n_ch = max(t // L, 1)
    keep = CONV_WIDTH - 1
    pad8 = lambda v: jnp.pad(v.astype(F32), (0, LANES - SSM_HEADS)).reshape(1, LANES)
    dtb = pad8(dt_bias)
    a = pad8(-jnp.exp(a_log.astype(F32)))
    dsk = pad8(d_skip)
    h0p = h0.reshape(b, HEAD_PAIRS, 2 * SSM_HEAD_DIM, SSM_STATE)
    full = lambda arr: pl.BlockSpec(arr.shape, lambda i, c: (0,) * arr.ndim)
    tok = lambda wd: pl.BlockSpec((1, t_valid, wd), lambda i, c: (i, c, 0))
    y, hout, cout = pl.pallas_call(
        functools.partial(_ssd_kernel, t_valid=t_valid),
        grid=(b, n_ch),
        in_specs=[
            tok(CONV_DIM), tok(SSM_WIDTH), tok(LANES),
            pl.BlockSpec((1, keep, CONV_DIM), lambda i, c: (i, 0, 0)),
            pl.BlockSpec((1, HEAD_PAIRS, 2 * SSM_HEAD_DIM, SSM_STATE), lambda i, c: (i, 0, 0, 0)),
            full(conv_w), pl.BlockSpec((1, CONV_DIM), lambda i, c: (0, 0)),
            full(dtb), full(a), full(dsk), pl.BlockSpec((1, SSM_WIDTH), lambda i, c: (0, 0)),
        ],
        out_specs=(
            tok(SSM_WIDTH),
            pl.BlockSpec((1, HEAD_PAIRS, 2 * SSM_HEAD_DIM, SSM_STATE), lambda i, c: (i, 0, 0, 0)),
            pl.BlockSpec((1, keep, CONV_DIM), lambda i, c: (i, 0, 0)),
        ),
        out_shape=(
            jax.ShapeDtypeStruct((b, t, SSM_WIDTH), F32),
            jax.ShapeDtypeStruct((b, HEAD_PAIRS, 2 * SSM_HEAD_DIM, SSM_STATE), F32),
            jax.ShapeDtypeStruct((b, keep, CONV_DIM), F32),
        ),
        scratch_shapes=[
            pltpu.VMEM((CONV_PAD + L, CONV_DIM), F32),
            pltpu.VMEM((HEAD_PAIRS, 2 * SSM_HEAD_DIM, SSM_STATE), F32),
            pltpu.VMEM((L, LANES), F32),
        ],
        compiler_params=_cparams(("arbitrary", "arbitrary")),
        name="ssd",
    )(xbc, z, misc, conv0, h0p, conv_w, conv_b.reshape(1, CONV_DIM), dtb, a, dsk, norm_w.reshape(1, SSM_WIDTH))
    return y, hout.reshape(b, SSM_HEADS, SSM_HEAD_DIM, SSM_STATE), cout


def _split2(x):
    hi = x.astype(BF16)
    return hi, (x - hi.astype(F32)).astype(BF16)


def _merge_kernel(oa_ref, ys_ref, x_ref, g1_ref, sh2_ref, sc2_ref, anw_ref, wo_ref, n2w_ref, wrh_ref, wrl_ref,
                  x1_ref, h2_ref, lg_ref, *packed_refs):
    oa = oa_ref[...]
    a = oa * lax.rsqrt(jnp.mean(oa * oa, axis=-1, keepdims=True) + EPS) * anw_ref[...]
    cat = jnp.concatenate([a.astype(BF16), ys_ref[...].astype(BF16)], axis=1)
    x1 = x_ref[...] + _mod(g1_ref) * _dot(cat, wo_ref[...])
    x1_ref[...] = x1
    h2 = x1 * lax.rsqrt(jnp.mean(x1 * x1, axis=-1, keepdims=True) + EPS) * n2w_ref[...]
    h2 = h2 * (1.0 + _mod(sc2_ref)) + _mod(sh2_ref)
    h2_ref[...] = h2.astype(BF16)
    if packed_refs:
        packed_refs[0][...] = _pack_pair(h2[:, :2 * PACK_W])
        packed_refs[1][...] = _pack_pair(h2[:, 2 * PACK_W:])
    hh, hl = _split2(h2)
    lg_ref[...] = _dot_nt(wrh_ref[...], hh) + _dot_nt(wrh_ref[...], hl) + _dot_nt(wrl_ref[...], hh)


def merge(o_attn, y_ssm, x, mod3, mod_row0, attn_norm_w, wo, norm2_w, w_router, tm, packed=False):
    b, t, d = x.shape
    n = b * t
    tiles_per_b = t // tm
    wrt = jnp.transpose(w_router)
    wrh, wrl = _split2(wrt)

    def mod_spec(col):
        return _mod_spec(mod3, col, tm, tiles_per_b, mod_row0)

    tok = lambda wd: pl.BlockSpec((tm, wd), lambda i: (i, 0))
    full = lambda a: pl.BlockSpec(a.shape, lambda i: (0,) * a.ndim)
    extra = 2 if packed else 0
    return pl.pallas_call(
        _merge_kernel,
        grid=(n // tm,),
        in_specs=[tok(ATTN_WIDTH), tok(SSM_WIDTH), tok(d), mod_spec(2), mod_spec(3), mod_spec(4),
                  full(attn_norm_w), full(wo), full(norm2_w), full(wrh), full(wrl)],
        out_specs=(tok(d), tok(d), pl.BlockSpec((N_EXPERTS, tm), lambda i: (0, i))) + (tok(PACK_W),) * extra,
        out_shape=(jax.ShapeDtypeStruct((n, d), F32), jax.ShapeDtypeStruct((n, d), BF16),
                   jax.ShapeDtypeStruct((N_EXPERTS, n), F32))
        + (jax.ShapeDtypeStruct((n, PACK_W), jnp.int32),) * extra,
        compiler_params=_cparams(("arbitrary",)),
        name="merge",
    )(o_attn.reshape(n, ATTN_WIDTH), y_ssm.reshape(n, SSM_WIDTH), x.reshape(n, d), mod3, mod3, mod3,
      attn_norm_w, wo, norm2_w, wrh, wrl)


EXPERTS_PER_GROUP = N_EXPERTS // N_EXPERT_GROUPS


def _first_max(x, ids, axes, n_ids):
    mx = jnp.max(x, axis=axes, keepdims=True)
    return ids == jnp.min(jnp.where(x == mx, ids, n_ids), axis=axes, keepdims=True), mx


def _route_kernel(lg_ref, eb_ref, tri_ref, w_ref, pos_ref, cnt_ref):
    lg = lg_ref[...]
    tn = lg.shape[2]
    scores = jax.nn.sigmoid(lg)
    biased = scores + eb_ref[...]
    sub = lax.broadcasted_iota(jnp.int32, lg.shape, 1)
    grp = lax.broadcasted_iota(jnp.int32, (N_EXPERT_GROUPS, 1, tn), 0)
    eid = lax.broadcasted_iota(jnp.int32, lg.shape, 0) * EXPERTS_PER_GROUP + sub
    hit, m1 = _first_max(biased, sub, 1, EXPERTS_PER_GROUP)
    m2 = jnp.max(jnp.where(hit, -jnp.inf, biased), axis=1, keepdims=True)
    gs = m1 + m2
    keep = jnp.zeros(gs.shape, jnp.bool_)
    for _ in range(TOPK_GROUPS):
        hit, _m = _first_max(gs, grp, 0, N_EXPERT_GROUPS)
        keep = keep | hit
        gs = jnp.where(hit, -jnp.inf, gs)
    x = jnp.where(keep, biased, NEG)
    sel = jnp.zeros(lg.shape, jnp.bool_)
    for _ in range(TOP_K):
        hit, _m = _first_max(x, eid, (0, 1), N_EXPERTS)
        sel = sel | hit
        x = jnp.where(hit, -jnp.inf, x)
    w = jnp.where(sel, scores, 0.0)
    w = w / jnp.sum(w, axis=(0, 1), keepdims=True) * ROUTED_SCALE
    w_ref[...] = w
    selb = sel.astype(BF16).reshape(N_EXPERTS, tn)
    pos = _dot(selb, tri_ref[...])
    pos_ref[...] = jnp.where(sel, pos.reshape(lg.shape), -1.0)
    cnt = jnp.sum(sel.astype(F32), axis=2, keepdims=True)
    cnt_ref[0] = jnp.broadcast_to(cnt, cnt_ref.shape[1:]).astype(jnp.int32)


def route(logits_t, e_bias, tn):
    n = logits_t.shape[1]
    lg3 = logits_t.reshape(N_EXPERT_GROUPS, EXPERTS_PER_GROUP, n)
    eb = e_bias.astype(F32).reshape(N_EXPERT_GROUPS, EXPERTS_PER_GROUP, 1)
    tri = jnp.asarray(np.triu(np.ones((tn, tn), np.float32), 1), BF16)
    blk = pl.BlockSpec((N_EXPERT_GROUPS, EXPERTS_PER_GROUP, tn), lambda i: (0, 0, i))
    w, pos, cnt = pl.pallas_call(
        _route_kernel,
        grid=(n // tn,),
        in_specs=[blk, pl.BlockSpec(eb.shape, lambda i: (0, 0, 0)), pl.BlockSpec((tn, tn), lambda i: (0, 0))],
        out_specs=(blk, blk, pl.BlockSpec((1, N_EXPERT_GROUPS, EXPERTS_PER_GROUP, LANES), lambda i: (i, 0, 0, 0))),
        out_shape=(jax.ShapeDtypeStruct(lg3.shape, F32), jax.ShapeDtypeStruct(lg3.shape, F32),
                   jax.ShapeDtypeStruct((n // tn, N_EXPERT_GROUPS, EXPERTS_PER_GROUP, LANES), jnp.int32)),
        compiler_params=_cparams(("arbitrary",)),
        name="route",
    )(lg3, eb, tri)
    return w.reshape(N_EXPERTS, n), pos.reshape(N_EXPERTS, n), cnt[..., 0].reshape(n // tn, N_EXPERTS)


MOE_ROWS = 128


def _swiglu(xb, wgu, wd, width):
    gu = _dot(xb, wgu)
    act = _silu(gu[:, :width]) * gu[:, width:]
    return _dot(act.astype(BF16), wd)


MOE_EXPERTS_PER_STEP = 4


MOE_ALIGN = 16
MOE_GATHER_ROWS = 896


def _moe_slots(tm):
    worst = TOP_K * tm + N_EXPERTS * (MOE_ALIGN - 1) + MOE_ROWS
    return -(-worst // MOE_GATHER_ROWS) * MOE_GATHER_ROWS


def _moe_kernel(cnt_ref, start_ref, h2_ref, w_ref, pos_ref, x1_ref, g2_ref, wgu_ref, wd_ref, sgu_ref, sd_ref,
                o_ref, g_all, xs):
    i = pl.program_id(0)
    es = pl.program_id(1)
    tm = h2_ref.shape[0]
    slots = g_all.shape[0]
    slot = lax.broadcasted_iota(jnp.int32, (MOE_ROWS, tm), 0).astype(F32)
    row = lax.broadcasted_iota(jnp.int32, (MOE_ROWS, 1), 0)

    def n_windows(cnt):
        return (cnt + MOE_ROWS - 1) // MOE_ROWS

    def window_start(e, j):
        return pl.multiple_of(start_ref[i * N_EXPERTS + e] + j * MOE_ROWS, MOE_ALIGN)

    @pl.when(es == 0)
    def _():
        g_all[...] = jnp.zeros(g_all.shape, BF16)

        def mark(e, carry):
            pos = pos_ref[pl.ds(e, 1), :]

            def mark_window(j, carry):
                hit = pos == slot + (j * MOE_ROWS).astype(F32)
                g_all[pl.ds(window_start(e, j), MOE_ROWS), :] = hit.astype(BF16)
                return carry

            return lax.fori_loop(0, n_windows(cnt_ref[i * N_EXPERTS + e]), mark_window, carry)

        lax.fori_loop(0, N_EXPERTS, mark, 0)

        def gather(c, carry):
            r0 = pl.multiple_of(c * MOE_GATHER_ROWS, MOE_GATHER_ROWS)
            rows = _dot(g_all[pl.ds(r0, MOE_GATHER_ROWS), :], h2_ref[...])
            xs[pl.ds(r0, MOE_GATHER_ROWS), :] = rows.astype(BF16)
            return carry

        lax.fori_loop(0, slots // MOE_GATHER_ROWS, gather, 0)

    for q in range(MOE_EXPERTS_PER_STEP):
        e = es * MOE_EXPERTS_PER_STEP + q
        cnt = cnt_ref[i * N_EXPERTS + e]
        wrow = w_ref[pl.ds(e, 1), :]

        def window(j, carry, q=q, e=e, cnt=cnt, wrow=wrow):
            r0 = window_start(e, j)
            xg = xs[pl.ds(r0, MOE_ROWS), :]
            out = _swiglu(xg, wgu_ref[q].astype(BF16), wd_ref[q].astype(BF16), D_EXPERT)
            g = g_all[pl.ds(r0, MOE_ROWS), :].astype(F32)
            out = out * jnp.sum(g * wrow, axis=1, keepdims=True)
            mine = row < cnt - j * MOE_ROWS
            xs[pl.ds(r0, MOE_ROWS), :] = jnp.where(mine, out.astype(BF16), xg)
            return carry

        lax.fori_loop(0, n_windows(cnt), window, 0)

    @pl.when(es == pl.num_programs(1) - 1)
    def _():
        y = lax.dot_general(g_all[...], xs[...], (((0,), (0,)), ((), ())), preferred_element_type=F32)
        y = y + _swiglu(h2_ref[...], sgu_ref[...], sd_ref[...], D_SHARED)
        o_ref[...] = x1_ref[...] + _mod(g2_ref) * y


def moe(h2, w_t, pos_t, counts, x1, mod3, mod_row0, t_per_b, wgu, wd, sgu, sd, tm):
    n, d = h2.shape
    tiles_per_b = t_per_b // tm
    eps = MOE_EXPERTS_PER_STEP
    slots = _moe_slots(tm)
    padded = (counts + MOE_ALIGN - 1) // MOE_ALIGN * MOE_ALIGN
    starts = jnp.cumsum(padded, axis=1) - padded
    grid_spec = pltpu.PrefetchScalarGridSpec(
        num_scalar_prefetch=2,
        grid=(n // tm, N_EXPERTS // eps),
        in_specs=[
            pl.BlockSpec((tm, d), lambda i, e, *_: (i, 0)),
            pl.BlockSpec((N_EXPERTS, tm), lambda i, e, *_: (0, i)),
            pl.BlockSpec((N_EXPERTS, tm), lambda i, e, *_: (0, i)),
            pl.BlockSpec((tm, d), lambda i, e, *_: (i, 0)),
            _mod_spec(mod3, 5, tm, tiles_per_b, mod_row0),
            pl.BlockSpec((eps, d, 2 * D_EXPERT), lambda i, e, *_: (e, 0, 0)),
            pl.BlockSpec((eps, D_EXPERT, d), lambda i, e, *_: (e, 0, 0)),
            pl.BlockSpec(sgu.shape, lambda i, e, *_: (0, 0)),
            pl.BlockSpec(sd.shape, lambda i, e, *_: (0, 0)),
        ],
        out_specs=pl.BlockSpec((tm, d), lambda i, e, *_: (i, 0)),
        scratch_shapes=[pltpu.VMEM((slots, tm), BF16), pltpu.VMEM((slots, d), BF16)],
    )
    return pl.pallas_call(
        _moe_kernel,
        grid_spec=grid_spec,
        out_shape=jax.ShapeDtypeStruct((n, d), F32),
        compiler_params=_cparams(("arbitrary", "arbitrary")),
        name="moe",
    )(counts.reshape(-1), starts.reshape(-1).astype(jnp.int32), h2, w_t, pos_t, x1, mod3, wgu, wd, sgu, sd)


SC_WINDOW = 128
PACK_W = 256
MOE_BLOCK_ROWS = 1024
HI_MASK = -65536


def _pack_pair(x):
    bits = pltpu.bitcast(x.astype(BF16).astype(F32), jnp.int32)
    return lax.shift_right_logical(bits[:, :PACK_W], 16) | (bits[:, PACK_W:] & HI_MASK)


def _unpack_pair(word):
    lo = pltpu.bitcast(lax.shift_left(word, 16), F32)
    hi = pltpu.bitcast(word & HI_MASK, F32)
    return jnp.concatenate([lo, hi], axis=1)


def _pack_kernel(x_ref, a_ref, b_ref):
    x = x_ref[...]
    a_ref[...] = _pack_pair(x[:, :2 * PACK_W])
    b_ref[...] = _pack_pair(x[:, 2 * PACK_W:])


def pack_rows(x, tm):
    n, d = x.shape
    tok = lambda wd: pl.BlockSpec((tm, wd), lambda i: (i, 0))
    return pl.pallas_call(
        _pack_kernel, grid=(n // tm,), in_specs=[tok(d)], out_specs=(tok(PACK_W), tok(PACK_W)),
        out_shape=(jax.ShapeDtypeStruct((n, PACK_W), jnp.int32),) * 2,
        compiler_params=_cparams(("arbitrary",)), name="pack_rows",
    )(x)


def _slots_kernel(w_ref, pos_ref, base_ref, tri_ref, slot_ref, wt_ref):
    w = w_ref[...]
    pos = pos_ref[...]
    sel = pos >= 0.0
    rank = _dot(tri_ref[...], sel.astype(BF16))
    dest = base_ref[0] + pos
    slots, wts = [], []
    for j in range(TOP_K):
        mine = sel & (rank == float(j))
        slots.append(jnp.sum(jnp.where(mine, dest, 0.0), axis=0, keepdims=True))
        wts.append(jnp.sum(jnp.where(mine, w, 0.0), axis=0, keepdims=True))
    slot_ref[...] = jnp.concatenate(slots, axis=0).astype(jnp.int32)
    wpad = jnp.concatenate(wts + [jnp.zeros((LANES - TOP_K, w.shape[1]), F32)], axis=0)
    wt_ref[...] = jnp.transpose(wpad)


def slots_of(w_t, pos_t, base, tn):
    n = w_t.shape[1]
    tri = jnp.asarray(np.tril(np.ones((N_EXPERTS, N_EXPERTS), np.float32), -1), BF16)
    blk = pl.BlockSpec((N_EXPERTS, tn), lambda i: (0, i))
    return pl.pallas_call(
        _slots_kernel, grid=(n // tn,),
        in_specs=[blk, blk, pl.BlockSpec((1, N_EXPERTS, 1), lambda i: (i, 0, 0)),
                  pl.BlockSpec((N_EXPERTS, N_EXPERTS), lambda i: (0, 0))],
        out_specs=(pl.BlockSpec((TOP_K, tn), lambda i: (0, i)), pl.BlockSpec((tn, LANES), lambda i: (i, 0))),
        out_shape=(jax.ShapeDtypeStruct((TOP_K, n), jnp.int32), jax.ShapeDtypeStruct((n, LANES), F32)),
        compiler_params=_cparams(("arbitrary",)), name="moe_slots",
    )(w_t, pos_t, base, tri)


def sc_scatter_rows(rows, idx, n_out):
    n, d = rows.shape
    m = idx.shape[0]
    nb = n // SC_WINDOW
    mesh = plsc.VectorSubcoreMesh(core_axis_name="core", subcore_axis_name="subcore")

    @functools.partial(pl.kernel, out_type=jax.ShapeDtypeStruct((n_out, d), rows.dtype), mesh=mesh)
    def scatter(x_hbm, i_hbm, o_hbm):
        def body(x_vmem, i_vmem):
            pltpu.sync_copy(x_vmem, o_hbm.at[i_vmem.at[0]])

        pltpu.emit_pipeline(
            body, grid=(m // SC_WINDOW,),
            in_specs=[pl.BlockSpec((SC_WINDOW, d), index_map=lambda i: (i % nb, 0)),
                      pl.BlockSpec((1, SC_WINDOW), index_map=lambda i: (0, i))],
            out_specs=[], core_axis_name=("core", "subcore"), dimension_semantics=(pltpu.PARALLEL,),
        )(x_hbm, i_hbm)

    return scatter(rows, idx.reshape(1, m))


def sc_gather_rows(table, idx):
    d = table.shape[1]
    m = idx.shape[0]
    mesh = plsc.VectorSubcoreMesh(core_axis_name="core", subcore_axis_name="subcore")

    @functools.partial(pl.kernel, out_type=jax.ShapeDtypeStruct((m, d), table.dtype), mesh=mesh)
    def gather(x_hbm, i_hbm, o_hbm):
        def body(i_vmem, o_vmem):
            pltpu.sync_copy(x_hbm.at[i_vmem.at[0]], o_vmem)

        pltpu.emit_pipeline(
            body, grid=(m // SC_WINDOW,),
            in_specs=[pl.BlockSpec((1, SC_WINDOW), index_map=lambda i: (0, i))],
            out_specs=[pl.BlockSpec((SC_WINDOW, d), index_map=lambda i: (i, 0))],
            core_axis_name=("core", "subcore"), dimension_semantics=(pltpu.PARALLEL,),
        )(i_hbm, o_hbm)

    return gather(table, idx.reshape(1, m))


def _experts_kernel(be_ref, nu_ref, xa_ref, xb_ref, wgu_ref, wd_ref, oa_ref, ob_ref, wgu_bf, wd_bf):
    b = pl.program_id(0)

    @pl.when(b < nu_ref[0])
    def _():
        @pl.when((b == 0) | (be_ref[b] != be_ref[jnp.maximum(b - 1, 0)]))
        def _():
            wgu_bf[...] = wgu_ref[0].astype(BF16)
            wd_bf[...] = wd_ref[0].astype(BF16)

        x = jnp.concatenate([_unpack_pair(xa_ref[...]), _unpack_pair(xb_ref[...])], axis=1).astype(BF16)
        out = _swiglu(x, wgu_bf[...], wd_bf[...], D_EXPERT)
        oa_ref[...] = _pack_pair(out[:, :2 * PACK_W])
        ob_ref[...] = _pack_pair(out[:, 2 * PACK_W:])


def experts_sorted(xa, xb, block_expert, n_used, wgu, wd):
    r = xa.shape[0]
    d = wd.shape[2]
    row = lambda b, be, nu: (jnp.minimum(b, nu[0] - 1), 0)
    blk = pl.BlockSpec((MOE_BLOCK_ROWS, PACK_W), row)
    grid_spec = pltpu.PrefetchScalarGridSpec(
        num_scalar_prefetch=2, grid=(r // MOE_BLOCK_ROWS,),
        in_specs=[blk, blk,
                  pl.BlockSpec((1, d, 2 * D_EXPERT), lambda b, be, nu: (be[b], 0, 0)),
                  pl.BlockSpec((1, D_EXPERT, d), lambda b, be, nu: (be[b], 0, 0))],
        out_specs=(blk, blk),
        scratch_shapes=[pltpu.VMEM((d, 2 * D_EXPERT), BF16), pltpu.VMEM((D_EXPERT, d), BF16)],
    )
    return pl.pallas_call(
        _experts_kernel, grid_spec=grid_spec,
        out_shape=(jax.ShapeDtypeStruct((r, PACK_W), jnp.int32),) * 2,
        compiler_params=_cparams(("arbitrary",)), name="moe_experts",
    )(block_expert, n_used, xa, xb, wgu, wd)


def _combine_kernel(ya_ref, yb_ref, wt_ref, h2_ref, x1_ref, g2_ref, sgu_ref, sd_ref, o_ref):
    wt = wt_ref[...]
    acc = _swiglu(h2_ref[...], sgu_ref[...], sd_ref[...], D_SHARED)
    for j in range(TOP_K):
        y = jnp.concatenate([_unpack_pair(ya_ref[j]), _unpack_pair(yb_ref[j])], axis=1)
        acc = acc + wt[:, j:j + 1] * y
    o_ref[...] = x1_ref[...] + _mod(g2_ref) * acc


def combine_sorted(ya, yb, wt, h2, x1, mod3, mod_row0, t_per_b, sgu, sd, tm):
    n, d = h2.shape
    tiles_per_b = t_per_b // tm
    tok = lambda wd: pl.BlockSpec((tm, wd), lambda i: (i, 0))
    yblk = pl.BlockSpec((TOP_K, tm, PACK_W), lambda i: (0, i, 0))
    full = lambda a: pl.BlockSpec(a.shape, lambda i: (0,) * a.ndim)
    return pl.pallas_call(
        _combine_kernel, grid=(n // tm,),
        in_specs=[yblk, yblk, tok(LANES), tok(d), tok(d), _mod_spec(mod3, 5, tm, tiles_per_b, mod_row0),
                  full(sgu), full(sd)],
        out_specs=tok(d), out_shape=jax.ShapeDtypeStruct((n, d), F32),
        compiler_params=_cparams(("arbitrary",)), name="moe_combine",
    )(ya, yb, wt, h2, x1, mod3, sgu, sd)


def moe_sorted(h2, h2_packed, w_t, pos_t, counts, x1, mod3, mod_row0, t_per_b, wgu, wd, sgu, sd, tm, overlap):
    n, d = h2.shape
    assert d == 4 * PACK_W and n % SC_WINDOW == 0
    n_blocks = (TOP_K * n + N_EXPERTS * (MOE_BLOCK_ROWS - 1)) // MOE_BLOCK_ROWS
    total = jnp.sum(counts, axis=0)
    region = (total + MOE_BLOCK_ROWS - 1) // MOE_BLOCK_ROWS * MOE_BLOCK_ROWS
    region_end = jnp.cumsum(region)
    base = (region_end - region)[None, :] + jnp.cumsum(counts, axis=0) - counts
    block_row0 = jnp.arange(n_blocks, dtype=region_end.dtype) * MOE_BLOCK_ROWS
    block_expert = jnp.sum(region_end[None, :] <= block_row0[:, None], axis=1)
    block_expert = jnp.minimum(block_expert, N_EXPERTS - 1).astype(jnp.int32)
    n_used = (region_end[-1:] // MOE_BLOCK_ROWS).astype(jnp.int32)
    slot, wt = slots_of(w_t, pos_t, base.astype(F32).reshape(-1, N_EXPERTS, 1), tm)
    dest = slot.reshape(-1)
    ha, hb = h2_packed if h2_packed is not None else pack_rows(h2, tm)
    rows = n_blocks * MOE_BLOCK_ROWS
    xa, xb = sc_scatter_rows(ha, dest, rows), sc_scatter_rows(hb, dest, rows)
    n_used, rest = lax.optimization_barrier((n_used, overlap()))
    oa, ob = experts_sorted(xa, xb, block_expert, n_used, wgu, wd)
    ya = sc_gather_rows(oa, dest).reshape(TOP_K, n, PACK_W)
    yb = sc_gather_rows(ob, dest).reshape(TOP_K, n, PACK_W)
    return combine_sorted(ya, yb, wt, h2, x1, mod3, mod_row0, t_per_b, sgu, sd, tm), rest


GATHER_PAGES = 8


def _gather_kernel(pt_ref, *refs):
    pages, new_ref = refs[:GATHER_PAGES], refs[GATHER_PAGES]
    rows_ref, cmpx_ref, stage_ref = refs[GATHER_PAGES + 1:]
    step = pl.program_id(1)
    last = pl.num_programs(1) - 1
    n_rows = GATHER_PAGES * PAGE_SIZE

    @pl.when(step < last)
    def _():
        for k in range(GATHER_PAGES):
            sl = slice(k * PAGE_SIZE, (k + 1) * PAGE_SIZE)
            for r in range(4):
                tile = jnp.transpose(pages[k][0, r])
                if r < 2:
                    stage_ref[r, sl, :] = tile
                else:
                    rows_ref[0, sl, (r - 2) * KV_WIDTH:(r - 1) * KV_WIDTH] = tile.astype(BF16)

    @pl.when(step == last)
    def _():
        new = new_ref[0]
        tn = new.shape[0]
        stage_ref[...] = jnp.zeros(stage_ref.shape, F32)
        for s in range(2):
            stage_ref[s, 0:tn, :] = new[:, s * KV_WIDTH:(s + 1) * KV_WIDTH]
        pad = jnp.zeros((n_rows - tn, 2 * KV_WIDTH), F32)
        rows_ref[0] = jnp.concatenate([new[:, 2 * KV_WIDTH:], pad], axis=0).astype(BF16)

    _stride_block_store(stage_ref, cmpx_ref, n_rows)


def gather_pages(cache_t, page_table, new_rows):
    b, n_pages = page_table.shape
    steps = n_pages // GATHER_PAGES
    rows = GATHER_PAGES * PAGE_SIZE
    s_out = (steps + 1) * rows

    def page_spec(k):
        def idx(i, s, pt):
            p = jnp.minimum(s, steps - 1) * GATHER_PAGES + k
            return (pt[i * n_pages + p], 0, 0, 0)
        return pl.BlockSpec((1, 4, KV_WIDTH, PAGE_SIZE), idx)

    grid_spec = pltpu.PrefetchScalarGridSpec(
        num_scalar_prefetch=1,
        grid=(b, steps + 1),
        in_specs=[page_spec(k) for k in range(GATHER_PAGES)]
        + [pl.BlockSpec((1,) + new_rows.shape[1:], lambda i, s, pt: (i, 0, 0))],
        out_specs=(
            pl.BlockSpec((1, rows, 2 * KV_WIDTH), lambda i, s, pt: (i, s, 0)),
            pl.BlockSpec((1, rows // CMP_STRIDE, CMP_STRIDE * 2 * KV_WIDTH), lambda i, s, pt: (i, s, 0)),
        ),
        scratch_shapes=[pltpu.VMEM((2, rows, KV_WIDTH), F32)],
    )
    return pl.pallas_call(
        _gather_kernel,
        grid_spec=grid_spec,
        out_shape=(jax.ShapeDtypeStruct((b, s_out, 2 * KV_WIDTH), BF16),
                   jax.ShapeDtypeStruct((b, s_out // CMP_STRIDE, CMP_STRIDE * 2 * KV_WIDTH), BF16)),
        compiler_params=_cparams(("arbitrary", "arbitrary")),
        name="gather_pages",
    )(page_table.reshape(-1), *([cache_t] * GATHER_PAGES), new_rows)


def _attention(qp, cmpx, kvb, sel_col, winb, misc, cmp_w, q_off, win_pos0, tq):
    t = qp.shape[1]
    cur_lo, cur_hi = q_off // SEL_BLOCK, (q_off + t - 1) // SEL_BLOCK
    assert cur_hi < N_SEL_LANES or (cur_lo == cur_hi == N_SEL_LANES), (q_off, t)
    kcv = compress(cmpx, *cmp_w)
    o_cmp, mneg = cmp_select(qp, kcv, q_off, tq)
    return sel_win_attention(qp, mneg, kvb, sel_col, winb, o_cmp, misc, q_off, win_pos0, tq)


def kernel(x_prompt, x_sample, cache_kv, cache_win, state_ssm, state_conv, page_table, c_prompt, c_sample, w_ada, b_ada, norm1_w, norm2_w, w_in, q_norm_w, k_norm_w, cmp_pe, cmp_w1, cmp_w2, attn_out_norm_w, conv_w, conv_b, dt_bias, a_log, d_skip, ssm_norm_w, w_out, w_router, e_bias, w_exp_gu, w_exp_down, w_sh_gu, w_sh_down):
    xp, xq = x_prompt, x_sample
    bp, tp, d = xp.shape
    bq, tq, _ = xq.shape
    depth = w_ada.shape[0]
    past_len = page_table.shape[1] * PAGE_SIZE
    nq = bq * tq
    tq_pad = LANES // GQA_GROUP
    assert tp % TOKEN_TILE == 0 and tp >= WINDOW and nq % 8 == 0 and tq <= tq_pad
    pos_p = jnp.arange(tp, dtype=jnp.int32)
    pos_q = jnp.tile(past_len + jnp.arange(tq, dtype=jnp.int32), bq)
    c_all = jnp.concatenate([c_prompt, c_sample], axis=0)
    c_all = jnp.pad(c_all, ((0, -c_all.shape[0] % 8), (0, 0)))
    outs = [[] for _ in range(8)]
    for l in range(depth):
        mod = adaln_all(c_all, w_ada[l], b_ada[l])
        mod_p = mod.reshape(mod.shape[0], 1, 6 * d)
        mod_q = jnp.repeat(mod[bp:bp + bq], tq, axis=0)
        wp = _prep_w_in(w_in[l])
        cmp_w = _prep_compress(cmp_pe[l], cmp_w1[l], cmp_w2[l])
        wo = w_out[l].astype(BF16)
        wgu, wd = w_exp_gu[l], w_exp_down[l]
        sgu, sd = w_sh_gu[l].astype(BF16), w_sh_down[l].astype(BF16)
        ssm_w = (conv_w[l], conv_b[l], dt_bias[l], a_log[l], d_skip[l], ssm_norm_w[l])
        n1w, n2w, anw = norm1_w[l:l + 1], norm2_w[l:l + 1], attn_out_norm_w[l:l + 1]

        xq1 = xq.reshape(1, nq, d)
        rq = lambda a: a.reshape(bq, tq, a.shape[-1])
        padq = lambda a: jnp.pad(rq(a), ((0, 0), (0, tq_pad - tq), (0, 0)))
        proj_q = inproj(xq1, mod_q, 0, n1w, wp, q_norm_w[l], k_norm_w[l], pos_q, nq, False)
        cache_t = jnp.transpose(cache_kv[l], (0, 2, 3, 4, 1)).reshape(cache_kv.shape[1], 4, KV_WIDTH, PAGE_SIZE)
        past, cmpx_q = gather_pages(cache_t, page_table, rq(proj_q[-1]))
        xp, (proj_q, past, cmpx_q) = lax.optimization_barrier((xp, (proj_q, past, cmpx_q)))

        qp, kvb, win, winb, z, xbc, misc, kvt, cmpx = inproj(xp, mod_p, 0, n1w, wp, q_norm_w[l], k_norm_w[l], pos_p,
                                                            TOKEN_TILE, True)
        r3 = lambda a: a.reshape(bp, tp, a.shape[-1])
        o_attn = _attention(r3(qp), cmpx, r3(kvb), 2, r3(winb), r3(misc), cmp_w, 0, 0, QUERY_TILE)
        y_ssm, h_new, conv_new = ssd(r3(xbc), r3(z), r3(misc), jnp.zeros((bp, CONV_WIDTH - 1, CONV_DIM), F32),
                                     jnp.zeros((bp, SSM_HEADS, SSM_HEAD_DIM, SSM_STATE), F32), *ssm_w)
        x1, h2, lg, ha, hb = merge(o_attn, y_ssm, xp, mod_p, 0, anw, wo, n2w, w_router[l], TOKEN_TILE, packed=True)
        w_t, pos_t, cnt = route(lg, e_bias[l], TOKEN_TILE)
        outs[0].append(jnp.transpose(kvt.reshape(bp, 4, N_KV_HEADS, HEAD_DIM, tp), (0, 4, 1, 2, 3)))
        outs[1].append(win.reshape(bp, tp, 2, N_KV_HEADS, HEAD_DIM)[:, tp - WINDOW:])
        outs[2].append(h_new)
        outs[3].append(conv_new)

        qp, kvb, win, winb, z, xbc, misc, kv = proj_q
        assert past_len // SEL_BLOCK == (past_len + tq_pad - 1) // SEL_BLOCK == N_SEL_LANES

        def sample_select():
            return cmp_select(padq(qp), compress(cmpx_q, *cmp_w), past_len, tq_pad)

        xp, (o_cmp, mneg) = moe_sorted(h2, (ha, hb), w_t, pos_t, cnt, x1, mod_p, 0, tp, wgu, wd, sgu, sd, TOKEN_TILE,
                                       sample_select)
        xp = xp.reshape(bp, tp, d)
        win_all = jnp.concatenate([cache_win[l].reshape(bq, WINDOW, 2 * KV_WIDTH).astype(BF16), rq(winb),
                                   jnp.zeros((bq, -(WINDOW + tq_pad) % WIN_CHUNK + tq_pad - tq, 2 * KV_WIDTH), BF16)],
                                  axis=1)
        o_attn = sel_win_attention(padq(qp), mneg, past, 0, win_all, o_cmp, padq(misc), past_len,
                                   past_len - WINDOW, tq_pad)[:, :tq]
        y_ssm, h_new, conv_new = ssd(rq(xbc), rq(z), rq(misc), state_conv[l], state_ssm[l], *ssm_w)
        x1, h2, lg = merge(o_attn.reshape(1, nq, ATTN_WIDTH), y_ssm.reshape(1, nq, SSM_WIDTH), xq1, mod_q, 0,
                           anw, wo, n2w, w_router[l], nq)
        w_t, pos_t, cnt = route(lg, e_bias[l], nq)
        xq = moe(h2, w_t, pos_t, cnt, x1, mod_q, 0, nq, wgu, wd, sgu, sd, nq).reshape(bq, tq, d)
        win_rows = win.reshape(bq, tq, 2, N_KV_HEADS, HEAD_DIM)
        outs[4].append(kv.reshape(bq, tq, 4, N_KV_HEADS, HEAD_DIM))
        outs[5].append(jnp.concatenate([cache_win[l], win_rows.astype(cache_win.dtype)], axis=1)[:, tq:])
        outs[6].append(h_new)
        outs[7].append(conv_new)
    return (xp, xq) + tuple(jnp.stack(o) for o in outs)
```
